```python
import math
import jax, jax.numpy as jnp
from jax import lax
import numpy as np

D_MODEL = 2048
BATCH = 32
SEQ = 256
DEPTH = 2
DEC_BATCH = 4
DEC_SEQ = 4096
PAST_LEN = 256

F32 = jnp.float32
GRID_W = 64
RMS_EPS = 1e-6
HEAD_DIM = 128
N_HEADS = 8
N_KV_HEADS = 2
GQA_GROUP = N_HEADS // N_KV_HEADS
Q_WIDTH = N_HEADS * HEAD_DIM
KV_WIDTH = N_KV_HEADS * HEAD_DIM
ROPE_THETA = 10000.0
ROPE_AXIS_DIM = HEAD_DIM // 2
Q_BLOCK = 128
HYENA_WIDTH = D_MODEL // 2
HYENA_ORDER = 2
FILTER_BANDS = 16
FILTER_EMB = 1 + 2 * FILTER_BANDS
FILTER_HIDDEN = 64
DECAY_TARGET = 1e-2
DECAY_PCT_SHORT = 0.3
DECAY_PCT_LONG = 1.5
IN_WIDTH_L0 = Q_WIDTH + 2 * KV_WIDTH + (HYENA_ORDER + 1) * HYENA_WIDTH
MIX_WIDTH_L0 = Q_WIDTH + HYENA_WIDTH
S5_WIDTH = D_MODEL // 2
S5_GROUP_CH = 16
S5_GROUPS = S5_WIDTH // S5_GROUP_CH
S5_STATE = 64
N_EXPERTS = 64
TOP_K = 8
N_EXPERT_GROUPS = 8
TOPK_GROUPS = 4
EXPERT_HIDDEN = 512
ROUTED_SCALE = 2.5
MOE_BLOCK = 128

kernel_name = 'hybrid_flow_ctx_prefix_step'


def rms_norm(x, g):
    xf = x.astype(F32)
    y = xf * lax.rsqrt(jnp.mean(xf * xf, axis=-1, keepdims=True) + RMS_EPS)
    return (y * g.astype(F32)).astype(x.dtype)


def axial_rope(L):
    n_rows = L // GRID_W
    row_idx = jnp.repeat(jnp.arange(n_rows, dtype=F32), GRID_W)
    col_idx = (jnp.arange(L) % GRID_W).astype(F32)
    inv = ROPE_THETA ** (-jnp.arange(0, ROPE_AXIS_DIM, 2, dtype=F32) / ROPE_AXIS_DIM)
    ang = jnp.concatenate([row_idx[:, None] * inv, col_idx[:, None] * inv], axis=-1)
    return jnp.cos(ang), jnp.sin(ang)


def apply_rope(x, cos, sin):
    xf = x.astype(F32).reshape(*x.shape[:-1], HEAD_DIM // 2, 2)
    x0, x1 = xf[..., 0], xf[..., 1]
    cs, sn = cos[None, :, None, :], sin[None, :, None, :]
    out = jnp.stack([x0 * cs - x1 * sn, x0 * sn + x1 * cs], axis=-1)
    return out.reshape(x.shape).astype(x.dtype)


def block_attention(q, k, v):
    B_, Lq = q.shape[0], q.shape[1]
    nb = Lq // Q_BLOCK
    qb = q.reshape(B_, nb, Q_BLOCK, N_KV_HEADS, GQA_GROUP, HEAD_DIM).transpose(1, 0, 2, 3, 4, 5)
    scale = HEAD_DIM ** -0.5

    def one_block(qblk):
        s = jnp.einsum('bqkgd,bskd->bkgqs', qblk, k, preferred_element_type=F32) * scale
        p = jax.nn.softmax(s, axis=-1)
        return jnp.einsum('bkgqs,bskd->bqkgd', p.astype(v.dtype), v)

    o = lax.map(one_block, qb)
    return o.transpose(1, 0, 2, 3, 4, 5).reshape(B_, Lq, Q_WIDTH)


def short_conv3(u, w, b):
    up = jnp.pad(u, ((0, 0), (1, 1), (0, 0)))
    return up[:, :-2] * w[0] + up[:, 1:-1] * w[1] + up[:, 2:] * w[2] + b


def hyena_filters(L, w1, b1, w2, b2, w3, sin_freq):
    t_norm = jnp.linspace(0.0, 1.0, L, dtype=F32)[:, None]
    omega = 2.0 * math.pi * jnp.arange(L, dtype=F32)[:, None] / L
    bands = jnp.linspace(1e-4, FILTER_BANDS - 1, FILTER_BANDS, dtype=F32)[None, :]
    z = jnp.concatenate([t_norm, jnp.cos(bands * omega), -jnp.sin(bands * omega)], axis=-1)
    freq = sin_freq.astype(F32)
    h = jnp.sin(freq * (z @ w1.astype(F32) + b1.astype(F32)))
    h = jnp.sin(freq * (h @ w2.astype(F32) + b2.astype(F32)))
    h = (h @ w3.astype(F32)).reshape(L, 2, HYENA_ORDER, HYENA_WIDTH)
    deltas = jnp.abs(jnp.linspace(math.log(DECAY_TARGET) / DECAY_PCT_LONG,
                                  math.log(DECAY_TARGET) / DECAY_PCT_SHORT, HYENA_WIDTH, dtype=F32))
    h = h * jnp.exp(-t_norm * deltas)[:, None, None, :]
    fwd, bwd = h[:, 0], h[:, 1]
    filt = jnp.concatenate([fwd, jnp.zeros_like(fwd[:1]), bwd[:0:-1]], axis=0)
    return filt / jnp.sum(jnp.abs(filt), axis=0, keepdims=True)


def fft_long_conv(u, filt, bias):
    L = u.shape[1]
    u_f = jnp.fft.rfft(u, n=2 * L, axis=1)
    f_f = jnp.fft.rfft(filt, n=2 * L, axis=0)
    y = jnp.fft.irfft(u_f * f_f[None], n=2 * L, axis=1)[:, :L]
    return y + u * bias.astype(F32)


def hyena(hy, conv_w, conv_b, f_w1, f_b1, f_w2, f_b2, f_w3, sin_freq, hy_bias):
    s = short_conv3(hy.astype(F32), conv_w.astype(F32), conv_b.astype(F32))
    v, x1, x2 = jnp.split(s, HYENA_ORDER + 1, axis=-1)
    filt = hyena_filters(hy.shape[1], f_w1, f_b1, f_w2, f_b2, f_w3, sin_freq)
    z = x1 * fft_long_conv(v, filt[:, 0], hy_bias[0])
    z = x2 * fft_long_conv(z, filt[:, 1], hy_bias[1])
    return z.astype(hy.dtype)


def mixer_attn_hyena(h, ctx, w_in, w_out, q_norm, k_norm, conv_w, conv_b,
                     f_w1, f_b1, f_w2, f_b2, f_w3, sin_freq, hy_bias):
    B_, L, _ = h.shape
    q, k, v, hy = jnp.split(h @ w_in, [Q_WIDTH, Q_WIDTH + KV_WIDTH, Q_WIDTH + 2 * KV_WIDTH], axis=-1)
    q = rms_norm(q.reshape(B_, L, N_HEADS, HEAD_DIM), q_norm)
    k = rms_norm(k.reshape(B_, L, N_KV_HEADS, HEAD_DIM), k_norm)
    v = v.reshape(B_, L, N_KV_HEADS, HEAD_DIM)
    if ctx is None:
        attn = block_attention(q, k, v)
        aux = (k, v)
    else:
        ctx_k, ctx_v = ctx
        cos, sin = axial_rope(L)
        keys = jnp.concatenate([ctx_k.astype(k.dtype), apply_rope(k, cos, sin)], axis=1)
        vals = jnp.concatenate([ctx_v.astype(v.dtype), v], axis=1)
        attn = block_attention(apply_rope(q, cos, sin), keys, vals)
        aux = ()
    z = hyena(hy, conv_w, conv_b, f_w1, f_b1, f_w2, f_b2, f_w3, sin_freq, hy_bias)
    return jnp.concatenate([attn, z], axis=-1) @ w_out, aux


def _ssm_combine(left, right):
    a_l, b_l = left
    a_r, b_r = right
    return a_r * a_l, a_r * b_l + b_r


def _ssm_scan(lam_bar, bu):
    a = jnp.broadcast_to(lam_bar, bu.shape)
    return lax.associative_scan(_ssm_combine, (a, bu))[1]


def s5_direction(ug, lam_bar, b_bar, c_mat, s0):
    bu = lax.complex(jnp.einsum('blgc,gpc->blgp', ug, jnp.real(b_bar)),
                     jnp.einsum('blgc,gpc->blgp', ug, jnp.imag(b_bar)))
    bu = bu.at[:, 0].add(lam_bar * s0)
    states = jax.vmap(_ssm_scan, in_axes=(None, 0))(lam_bar, bu)
    y = jnp.real(jnp.einsum('blgp,gcp->blgc', states, c_mat))
    return y, states[:, -1]


def mixer_s5(h, ctx, w_in, a_re, a_im, log_dt, b_re, b_im, c_re, c_im, d_skip, w_glu):
    s0_re, s0_im = ctx
    B_, L, _ = h.shape
    u = (h @ w_in).astype(F32)
    ug = u.reshape(B_, L, S5_GROUPS, S5_GROUP_CH)
    lam = lax.complex(a_re.astype(F32), a_im.astype(F32))
    lam_bar = jnp.exp(lam * jnp.exp(log_dt.astype(F32))[..., None])
    b_bar = ((lam_bar - 1.0) / lam)[..., None] * lax.complex(b_re.astype(F32), b_im.astype(F32))
    c_mat = lax.complex(c_re.astype(F32), c_im.astype(F32))
    s0 = lax.complex(s0_re.astype(F32), s0_im.astype(F32))
    y_f, s_f = s5_direction(ug, lam_bar[0], b_bar[0], c_mat[0], s0[:, 0])
    y_b, s_b = s5_direction(ug[:, ::-1], lam_bar[1], b_bar[1], c_mat[1], s0[:, 1])
    y = (y_f + y_b[:, ::-1]).reshape(B_, L, S5_WIDTH) + d_skip.astype(F32) * u
    a, g = jnp.split(jax.nn.gelu(y) @ w_glu.astype(F32), 2, axis=-1)
    s_final = jnp.stack([s_f, s_b], axis=1)
    return (a * jax.nn.sigmoid(g)).astype(h.dtype), (jnp.real(s_final), jnp.imag(s_final))


def moe_ffn(h, router_w, router_bias, exp_w13, exp_w2, shared_w13, shared_w2):
    B_, L, D_ = h.shape
    T = B_ * L
    x = h.reshape(T, D_)
    scores = jax.nn.sigmoid(x.astype(F32) @ router_w.astype(F32))
    choice = scores + router_bias.astype(F32)
    per_group = N_EXPERTS // N_EXPERT_GROUPS
    grp_score = lax.top_k(choice.reshape(T, N_EXPERT_GROUPS, per_group), 2)[0].sum(-1)
    _, grp_idx = lax.top_k(grp_score, TOPK_GROUPS)
    grp_keep = jax.nn.one_hot(grp_idx, N_EXPERT_GROUPS, dtype=F32).sum(1) > 0
    keep = jnp.repeat(grp_keep, per_group, axis=1)
    _, top_idx = lax.top_k(jnp.where(keep, choice, -jnp.inf), TOP_K)
    gate = jnp.take_along_axis(scores, top_idx, axis=1)
    gate = gate / jnp.sum(gate, axis=-1, keepdims=True) * ROUTED_SCALE
    n_assign = T * TOP_K
    flat_e = top_idx.reshape(-1)
    order = jnp.argsort(flat_e)
    se = flat_e[order]
    stok = (order // TOP_K).astype(jnp.int32)
    sw = gate.reshape(-1)[order]
    counts = jnp.bincount(flat_e, length=N_EXPERTS)
    padded = (counts + MOE_BLOCK - 1) // MOE_BLOCK * MOE_BLOCK
    start = jnp.cumsum(counts) - counts
    pend = jnp.cumsum(padded)
    dest = (pend - padded)[se] + jnp.arange(n_assign) - start[se]
    n_rows = -(-(n_assign + N_EXPERTS * (MOE_BLOCK - 1)) // MOE_BLOCK) * MOE_BLOCK
    n_blocks = n_rows // MOE_BLOCK
    row_tok = jnp.zeros((n_rows,), jnp.int32).at[dest].set(stok)
    row_w = jnp.zeros((n_rows,), F32).at[dest].set(sw)
    blk_e = jnp.minimum(jnp.searchsorted(pend, jnp.arange(n_blocks) * MOE_BLOCK, side='right'), N_EXPERTS - 1)
    xs = x[row_tok].reshape(n_blocks, MOE_BLOCK, D_)

    def expert_block(args):
        xb, e = args
        g, u = jnp.split(xb @ exp_w13[e], 2, axis=-1)
        return (jax.nn.silu(g) * u) @ exp_w2[e]

    ys = lax.map(expert_block, (xs, blk_e)).reshape(n_rows, D_)
    routed = jax.ops.segment_sum(ys.astype(F32) * row_w[:, None], row_tok, num_segments=T)
    sg, su = jnp.split(x @ shared_w13, 2, axis=-1)
    shared = (jax.nn.silu(sg) * su) @ shared_w2
    return (routed + shared.astype(F32)).astype(h.dtype).reshape(B_, L, D_)


def trunk_layer(x, cond, ada_w, ada_b, norm1, norm2, mixer, mix_p, ctx, moe_p):
    mods = jax.nn.silu(cond) @ ada_w + ada_b
    sh1, sc1, g1, sh2, sc2, g2 = jnp.split(mods[:, None, :], 6, axis=-1)
    h = rms_norm(x, norm1) * (1 + sc1) + sh1
    mix_out, aux = mixer(h, ctx, *mix_p)
    x = x + g1 * mix_out
    h = rms_norm(x, norm2) * (1 + sc2) + sh2
    x = x + g2 * moe_ffn(h, *moe_p)
    return x, aux


def setup_inputs(seed: int = 0) -> dict:
    key = jax.random.key(seed)
    ks = iter(jax.random.split(key, 64))

    def nrm(shape, scale):
        return scale * jax.random.normal(next(ks), shape, F32)

    def gain(n):
        return 1.0 + nrm((n,), 0.02)

    d = D_MODEL
    inp = {}
    inp['x_prompt'] = nrm((BATCH, SEQ, d), 1.0)
    inp['x_sample'] = nrm((DEC_BATCH, DEC_SEQ, d), 1.0)
    inp['cache_k_l0'] = nrm((DEC_BATCH, PAST_LEN, N_KV_HEADS, HEAD_DIM), 1.0)
    inp['cache_v_l0'] = nrm((DEC_BATCH, PAST_LEN, N_KV_HEADS, HEAD_DIM), 1.0)
    inp['state_s5_re_l1'] = nrm((DEC_BATCH, 2, S5_GROUPS, S5_STATE), 0.1)
    inp['state_s5_im_l1'] = nrm((DEC_BATCH, 2, S5_GROUPS, S5_STATE), 0.1)
    inp['c'] = nrm((DEC_BATCH, d), 1.0)
    inp['c_ctx'] = nrm((d,), 1.0)

    def add_common(l):
        inp['ada_w_l%d' % l] = nrm((d, 6 * d), 0.5 * d ** -0.5)
        inp['ada_b_l%d' % l] = nrm((6 * d,), 0.02)
        inp['norm1_l%d' % l] = gain(d)
        inp['norm2_l%d' % l] = gain(d)

    def add_moe(l):
        inp['router_l%d' % l] = nrm((d, N_EXPERTS), d ** -0.5)
        inp['router_bias_l%d' % l] = nrm((N_EXPERTS,), 0.01)
        inp['exp_w13_l%d' % l] = nrm((N_EXPERTS, d, 2 * EXPERT_HIDDEN), d ** -0.5)
        inp['exp_w2_l%d' % l] = nrm((N_EXPERTS, EXPERT_HIDDEN, d), EXPERT_HIDDEN ** -0.5)
        inp['shared_w13_l%d' % l] = nrm((d, 2 * EXPERT_HIDDEN), d ** -0.5)
        inp['shared_w2_l%d' % l] = nrm((EXPERT_HIDDEN, d), EXPERT_HIDDEN ** -0.5)

    add_common(0)
    inp['w_in_l0'] = nrm((d, IN_WIDTH_L0), d ** -0.5)
    inp['w_out_l0'] = nrm((MIX_WIDTH_L0, d), MIX_WIDTH_L0 ** -0.5)
    inp['q_norm_l0'] = gain(HEAD_DIM)
    inp['k_norm_l0'] = gain(HEAD_DIM)
    inp['hy_conv_w_l0'] = nrm((3, (HYENA_ORDER + 1) * HYENA_WIDTH), 0.5)
    inp['hy_conv_b_l0'] = nrm(((HYENA_ORDER + 1) * HYENA_WIDTH,), 0.02)
    inp['hy_ffn_w1_l0'] = nrm((FILTER_EMB, FILTER_HIDDEN), FILTER_EMB ** -0.5)
    inp['hy_ffn_b1_l0'] = nrm((FILTER_HIDDEN,), 0.1)
    inp['hy_ffn_w2_l0'] = nrm((FILTER_HIDDEN, FILTER_HIDDEN), FILTER_HIDDEN ** -0.5)
    inp['hy_ffn_b2_l0'] = nrm((FILTER_HIDDEN,), 0.1)
    inp['hy_ffn_w3_l0'] = nrm((FILTER_HIDDEN, 2 * HYENA_ORDER * HYENA_WIDTH), FILTER_HIDDEN ** -0.5)
    inp['hy_sin_freq_l0'] = 1.0 + nrm((FILTER_HIDDEN,), 0.1)
    inp['hy_bias_l0'] = nrm((HYENA_ORDER, HYENA_WIDTH), 1.0)
    add_moe(0)
    add_common(1)
    inp['w_in_l1'] = nrm((d, S5_WIDTH), d ** -0.5)
    inp['s5_a_re_l1'] = -0.5 + nrm((2, S5_GROUPS, S5_STATE), 0.01)
    inp['s5_a_im_l1'] = math.pi * jnp.arange(S5_STATE, dtype=F32) + nrm((2, S5_GROUPS, S5_STATE), 0.01)
    inp['s5_log_dt_l1'] = jax.random.uniform(next(ks), (2, S5_GROUPS), F32, math.log(1e-3), math.log(1e-1))
    inp['s5_b_re_l1'] = nrm((2, S5_GROUPS, S5_STATE, S5_GROUP_CH), (2 * S5_GROUP_CH) ** -0.5)
    inp['s5_b_im_l1'] = nrm((2, S5_GROUPS, S5_STATE, S5_GROUP_CH), (2 * S5_GROUP_CH) ** -0.5)
    inp['s5_c_re_l1'] = nrm((2, S5_GROUPS, S5_GROUP_CH, S5_STATE), (2 * S5_STATE) ** -0.5)
    inp['s5_c_im_l1'] = nrm((2, S5_GROUPS, S5_GROUP_CH, S5_STATE), (2 * S5_STATE) ** -0.5)
    inp['s5_d_l1'] = nrm((S5_WIDTH,), 1.0)
    inp['w_glu_l1'] = nrm((S5_WIDTH, 2 * d), S5_WIDTH ** -0.5)
    add_moe(1)
    inp['final_norm'] = gain(d)
    return inp


def reference(x_prompt, x_sample, cache_k_l0, cache_v_l0, state_s5_re_l1, state_s5_im_l1, c, c_ctx,
              ada_w_l0, ada_b_l0, norm1_l0, norm2_l0,
              w_in_l0, w_out_l0, q_norm_l0, k_norm_l0, hy_conv_w_l0, hy_conv_b_l0,
              hy_ffn_w1_l0, hy_ffn_b1_l0, hy_ffn_w2_l0, hy_ffn_b2_l0, hy_ffn_w3_l0, hy_sin_freq_l0, hy_bias_l0,
              router_l0, router_bias_l0, exp_w13_l0, exp_w2_l0, shared_w13_l0, shared_w2_l0,
              ada_w_l1, ada_b_l1, norm1_l1, norm2_l1,
              w_in_l1, s5_a_re_l1, s5_a_im_l1, s5_log_dt_l1, s5_b_re_l1, s5_b_im_l1,
              s5_c_re_l1, s5_c_im_l1, s5_d_l1, w_glu_l1,
              router_l1, router_bias_l1, exp_w13_l1, exp_w2_l1, shared_w13_l1, shared_w2_l1,
              final_norm):
    layers = [
        (ada_w_l0, ada_b_l0, norm1_l0, norm2_l0, mixer_attn_hyena,
         (w_in_l0, w_out_l0, q_norm_l0, k_norm_l0, hy_conv_w_l0, hy_conv_b_l0, hy_ffn_w1_l0, hy_ffn_b1_l0,
          hy_ffn_w2_l0, hy_ffn_b2_l0, hy_ffn_w3_l0, hy_sin_freq_l0, hy_bias_l0),
         (router_l0, router_bias_l0, exp_w13_l0, exp_w2_l0, shared_w13_l0, shared_w2_l0)),
        (ada_w_l1, ada_b_l1, norm1_l1, norm2_l1, mixer_s5,
         (w_in_l1, s5_a_re_l1, s5_a_im_l1, s5_log_dt_l1, s5_b_re_l1, s5_b_im_l1, s5_c_re_l1, s5_c_im_l1,
          s5_d_l1, w_glu_l1),
         (router_l1, router_bias_l1, exp_w13_l1, exp_w2_l1, shared_w13_l1, shared_w2_l1)),
    ]

    zero_state = jnp.zeros((x_prompt.shape[0], 2, S5_GROUPS, S5_STATE), F32)
    ctx_prompt = [None, (zero_state, zero_state)]
    xp = x_prompt
    new_state = []
    for layer in range(DEPTH):
        ada_w, ada_b, n1, n2, mixer, mix_p, moe_p = layers[layer]
        xp, aux = trunk_layer(xp, c_ctx[None, :], ada_w, ada_b, n1, n2, mixer, mix_p, ctx_prompt[layer], moe_p)
        new_state.extend(aux)
    new_k_l0, new_v_l0, new_s5_re_l1, new_s5_im_l1 = new_state
    y_prompt = rms_norm(xp, final_norm)

    ctx_sample = [(cache_k_l0, cache_v_l0), (state_s5_re_l1, state_s5_im_l1)]
    xs = x_sample
    for layer in range(DEPTH):
        ada_w, ada_b, n1, n2, mixer, mix_p, moe_p = layers[layer]
        xs, _ = trunk_layer(xs, c, ada_w, ada_b, n1, n2, mixer, mix_p, ctx_sample[layer], moe_p)
    y_sample = rms_norm(xs, final_norm)

    return (y_prompt, y_sample, new_k_l0, new_v_l0, new_s5_re_l1, new_s5_im_l1)
```

```python
import functools
import math

import jax
import jax.numpy as jnp
from jax import lax
from jax.experimental import pallas as pl
from jax.experimental.pallas import tpu as pltpu

F32 = jnp.float32
BF16 = jnp.bfloat16

D_MODEL = 2048
BATCH = 32
SEQ = 256
DEC_BATCH = 4
DEC_SEQ = 4096
PAST_LEN = 256
GRID_W = 64
RMS_EPS = 1e-6
HEAD_DIM = 128
N_HEADS = 8
N_KV_HEADS = 2
GQA_GROUP = N_HEADS // N_KV_HEADS
Q_WIDTH = N_HEADS * HEAD_DIM
KV_WIDTH = N_KV_HEADS * HEAD_DIM
QKV_WIDTH = Q_WIDTH + 2 * KV_WIDTH
ROPE_THETA = 10000.0
ROPE_AXIS_DIM = HEAD_DIM // 2
HYENA_WIDTH = D_MODEL // 2
HYENA_ORDER = 2
FILTER_BANDS = 16
DECAY_TARGET = 1e-2
DECAY_PCT_SHORT = 0.3
DECAY_PCT_LONG = 1.5
S5_WIDTH = D_MODEL // 2
S5_GROUP_CH = 16
S5_GROUPS = S5_WIDTH // S5_GROUP_CH
S5_STATE = 64
N_EXPERTS = 64
TOP_K = 8
N_EXPERT_GROUPS = 8
TOPK_GROUPS = 4
EXPERT_HIDDEN = 512
ROUTED_SCALE = 2.5
MOE_BLOCK = 128

T_CTX = BATCH * SEQ
T_LAT = DEC_BATCH * DEC_SEQ
T_ALL = T_CTX + T_LAT
N_COND = 8

ROW_TILE = 256
V7X_VMEM_LIMIT_BYTES = 56 * 1024 * 1024


def _params(semantics):
    return pltpu.CompilerParams(dimension_semantics=semantics, vmem_limit_bytes=V7X_VMEM_LIMIT_BYTES)


def _cond_row(i, tm):
    tok = i * tm
    return jnp.where(tok < T_CTX, 0, 1 + (tok - T_CTX) // DEC_SEQ)


def _mod_spec(k, tm):
    return pl.BlockSpec((1, 1, 1, D_MODEL), lambda i: (_cond_row(i, tm), k, 0, 0))


def _resident(shape):
    nd = len(shape)
    return pl.BlockSpec(shape, lambda i: (0,) * nd, pipeline_mode=pl.Buffered(1))


def _silu(x):
    return x * jax.nn.sigmoid(x)


def _norm_mod(x, g, sc, sh):
    ms = jnp.mean(x * x, axis=-1, keepdims=True)
    return (x * lax.rsqrt(ms + RMS_EPS) * g) * (1.0 + sc) + sh


def _ada_kernel(c_ref, w_ref, b_ref, o_ref):
    a = _silu(c_ref[...]).astype(BF16)
    o_ref[...] = jnp.dot(a, w_ref[...].astype(BF16), preferred_element_type=F32) + b_ref[...]


def _ada_mods(cond, w, b):
    n = w.shape[1]
    tn = 1024
    out = pl.pallas_call(
        _ada_kernel,
        grid=(n // tn,),
        in_specs=[pl.BlockSpec((N_COND, D_MODEL), lambda j: (0, 0)),
                  pl.BlockSpec((D_MODEL, tn), lambda j: (0, j)),
                  pl.BlockSpec((1, tn), lambda j: (0, j))],
        out_specs=pl.BlockSpec((N_COND, tn), lambda j: (0, j)),
        out_shape=jax.ShapeDtypeStruct((N_COND, n), F32),
        compiler_params=_params(("parallel",)),
        name="ada_mods",
    )(cond, w, b.reshape(1, n))
    return out.reshape(N_COND, 6, 1, D_MODEL)


def _nmm_kernel(x_ref, g_ref, sc_ref, sh_ref, w_ref, o_ref):
    h = _norm_mod(x_ref[...], g_ref[...], sc_ref[0, 0], sh_ref[0, 0])
    o_ref[...] = jnp.dot(h.astype(BF16), w_ref[...], preferred_element_type=F32)


def _norm_mod_matmul(x, g, mods4, k_shift, k_scale, w_bf16):
    t, n = x.shape[0], w_bf16.shape[1]
    tm = ROW_TILE
    return pl.pallas_call(
        _nmm_kernel,
        grid=(t // tm,),
        in_specs=[pl.BlockSpec((tm, D_MODEL), lambda i: (i, 0)),
                  _resident((1, D_MODEL)),
                  _mod_spec(k_scale, tm), _mod_spec(k_shift, tm),
                  _resident((D_MODEL, n))],
        out_specs=pl.BlockSpec((tm, n), lambda i: (i, 0)),
        out_shape=jax.ShapeDtypeStruct((t, n), F32),
        compiler_params=_params(("parallel",)),
        name="norm_mod_matmul",
    )(x, g.reshape(1, D_MODEL), mods4, mods4, w_bf16)


def _nm_kernel(x_ref, g_ref, sc_ref, sh_ref, o_ref, ob_ref):
    h = _norm_mod(x_ref[...], g_ref[...], sc_ref[0, 0], sh_ref[0, 0])
    o_ref[...] = h
    ob_ref[...] = h.astype(BF16)


def _norm_mod_only(x, g, mods4, k_shift, k_scale):
    t = x.shape[0]
    tm = ROW_TILE
    return pl.pallas_call(
        _nm_kernel,
        grid=(t // tm,),
        in_specs=[pl.BlockSpec((tm, D_MODEL), lambda i: (i, 0)),
                  _resident((1, D_MODEL)),
                  _mod_spec(k_scale, tm), _mod_spec(k_shift, tm)],
        out_specs=[pl.BlockSpec((tm, D_MODEL), lambda i: (i, 0)),
                   pl.BlockSpec((tm, D_MODEL), lambda i: (i, 0))],
        out_shape=[jax.ShapeDtypeStruct((t, D_MODEL), F32), jax.ShapeDtypeStruct((t, D_MODEL), BF16)],
        compiler_params=_params(("parallel",)),
        name="norm_mod",
    )(x, g.reshape(1, D_MODEL), mods4, mods4)


def _head_norm(xh, g):
    ms = jnp.mean(xh * xh, axis=-1, keepdims=True)
    return xh * lax.rsqrt(ms + RMS_EPS) * g


def _rope(y, cos, sin_signed):
    lane = lax.broadcasted_iota(jnp.int32, y.shape, 1)
    partner = jnp.where(lane % 2 == 0, pltpu.roll(y, HEAD_DIM - 1, 1), pltpu.roll(y, 1, 1))
    return y * cos + partner * sin_signed


def _qkprep_ctx_kernel(qkv_ref, qg_ref, kg_ref, q_out, kb_out, vb_out, kf_out):
    for h in range(N_HEADS):
        sl = slice(h * HEAD_DIM, (h + 1) * HEAD_DIM)
        q_out[:, sl] = _head_norm(qkv_ref[:, sl], qg_ref[...]).astype(BF16)
    for j in range(N_KV_HEADS):
        k = _head_norm(qkv_ref[:, Q_WIDTH + j * HEAD_DIM:Q_WIDTH + (j + 1) * HEAD_DIM], kg_ref[...])
        kf_out[:, j * HEAD_DIM:(j + 1) * HEAD_DIM] = k
        kb_out[:, j * HEAD_DIM:(j + 1) * HEAD_DIM] = k.astype(BF16)
    vb_out[...] = qkv_ref[:, Q_WIDTH + KV_WIDTH:QKV_WIDTH].astype(BF16)


def _qkprep_lat_kernel(qkv_ref, qg_ref, kg_ref, cos_ref, sin_ref, q_out, kb_out, vb_out):
    cos, sin = cos_ref[...], sin_ref[...]
    for h in range(N_HEADS):
        sl = slice(h * HEAD_DIM, (h + 1) * HEAD_DIM)
        q_out[:, sl] = _rope(_head_norm(qkv_ref[:, sl], qg_ref[...]), cos, sin).astype(BF16)
    for j in range(N_KV_HEADS):
        k = _head_norm(qkv_ref[:, Q_WIDTH + j * HEAD_DIM:Q_WIDTH + (j + 1) * HEAD_DIM], kg_ref[...])
        kb_out[:, j * HEAD_DIM:(j + 1) * HEAD_DIM] = _rope(k, cos, sin).astype(BF16)
    vb_out[...] = qkv_ref[:, Q_WIDTH + KV_WIDTH:QKV_WIDTH].astype(BF16)


def _qk_prep(proj, row0, t, q_norm, k_norm, rope_tabs):
    tm = ROW_TILE
    b0 = row0 // tm
    qkv_spec = pl.BlockSpec((tm, QKV_WIDTH), lambda i: (b0 + i, 0))
    gain = _resident((1, HEAD_DIM))
    outs = [pl.BlockSpec((tm, Q_WIDTH), lambda i: (i, 0)),
            pl.BlockSpec((tm, KV_WIDTH), lambda i: (i, 0)),
            pl.BlockSpec((tm, KV_WIDTH), lambda i: (i, 0))]
    shapes = [jax.ShapeDtypeStruct((t, Q_WIDTH), BF16),
              jax.ShapeDtypeStruct((t, KV_WIDTH), BF16),
              jax.ShapeDtypeStruct((t, KV_WIDTH), BF16)]
    qg, kg = q_norm.reshape(1, HEAD_DIM), k_norm.reshape(1, HEAD_DIM)
    if rope_tabs is None:
        return pl.pallas_call(
            _qkprep_ctx_kernel, grid=(t // tm,),
            in_specs=[qkv_spec, gain, gain],
            out_specs=outs + [pl.BlockSpec((tm, KV_WIDTH), lambda i: (i, 0))],
            out_shape=shapes + [jax.ShapeDtypeStruct((t, KV_WIDTH), F32)],
            compiler_params=_params(("parallel",)), name="qk_prep_ctx",
        )(proj, qg, kg)
    cos, sin = rope_tabs
    nq = DEC_SEQ // tm
    tab = pl.BlockSpec((tm, HEAD_DIM), lambda i: (i % nq, 0))
    return pl.pallas_call(
        _qkprep_lat_kernel, grid=(t // tm,),
        in_specs=[qkv_spec, gain, gain, tab, tab],
        out_specs=outs, out_shape=shapes,
        compiler_params=_params(("parallel",)), name="qk_prep_lat",
    )(proj, qg, kg, cos, sin)


def _attn_kernel(*refs, with_ctx):
    if with_ctx:
        q_ref, k_ref, v_ref, ck_ref, cv_ref, o_ref = refs
    else:
        q_ref, k_ref, v_ref, o_ref = refs
    scale = HEAD_DIM ** -0.5
    nt = (((1,), (1,)), ((), ()))
    q = q_ref[...]
    s = lax.dot_general(q, k_ref[...], nt, preferred_element_type=F32) * scale
    m = jnp.max(s, axis=-1, keepdims=True)
    if with_ctx:
        sc = lax.dot_general(q, ck_ref[...], nt, preferred_element_type=F32) * scale
        m = jnp.maximum(m, jnp.max(sc, axis=-1, keepdims=True))
    p = jnp.exp(s - m)
    l = jnp.sum(p, axis=-1, keepdims=True)
    o = jnp.dot(p.astype(BF16), v_ref[...], preferred_element_type=F32)
    if with_ctx:
        pc = jnp.exp(sc - m)
        l = l + jnp.sum(pc, axis=-1, keepdims=True)
        o = o + jnp.dot(pc.astype(BF16), cv_ref[...], preferred_element_type=F32)
    o_ref[...] = (o / l).astype(BF16)


def _attention(q, k, v, n_batch, seq, ctx_kv):
    tq = ROW_TILE
    nq = seq // tq
    t = n_batch * seq
    q_spec = pl.BlockSpec((tq, HEAD_DIM), lambda b, h, i: (b * nq + i, h))
    kv_spec = pl.BlockSpec((seq, HEAD_DIM), lambda b, h, i: (b, h // GQA_GROUP))
    in_specs = [q_spec, kv_spec, kv_spec]
    args = [q, k, v]
    if ctx_kv is not None:
        c_spec = pl.BlockSpec((PAST_LEN, HEAD_DIM), lambda b, h, i: (b, h // GQA_GROUP))
        in_specs += [c_spec, c_spec]
        args += list(ctx_kv)
    return pl.pallas_call(
        functools.partial(_attn_kernel, with_ctx=ctx_kv is not None),
        grid=(n_batch, N_HEADS, nq),
        in_specs=in_specs,
        out_specs=pl.BlockSpec((tq, HEAD_DIM), lambda b, h, i: (b * nq + i, h)),
        out_shape=jax.ShapeDtypeStruct((t, Q_WIDTH), BF16),
        compiler_params=_params(("parallel", "parallel", "parallel")),
        name="attention_lat" if ctx_kv is not None else "attention_ctx",
    )(*args)


def _outproj_kernel(a_ref, z_ref, x_ref, gate_ref, wa_ref, wz_ref, o_ref):
    acc = jnp.dot(a_ref[...], wa_ref[...], preferred_element_type=F32)
    acc = acc + jnp.dot(z_ref[...].astype(BF16), wz_ref[...], preferred_element_type=F32)
    o_ref[...] = x_ref[...] + gate_ref[0, 0] * acc


def _out_proj(attn, z, x, mods4, k_gate, wa_bf16, wz_bf16):
    t = x.shape[0]
    tm = ROW_TILE
    return pl.pallas_call(
        _outproj_kernel,
        grid=(t // tm,),
        in_specs=[pl.BlockSpec((tm, Q_WIDTH), lambda i: (i, 0)),
                  pl.BlockSpec((tm, HYENA_WIDTH), lambda i: (i, 0)),
                  pl.BlockSpec((tm, D_MODEL), lambda i: (i, 0)),
                  _mod_spec(k_gate, tm),
                  _resident((Q_WIDTH, D_MODEL)), _resident((HYENA_WIDTH, D_MODEL))],
        out_specs=pl.BlockSpec((tm, D_MODEL), lambda i: (i, 0)),
        out_shape=jax.ShapeDtypeStruct((t, D_MODEL), F32),
        compiler_params=_params(("parallel",)),
        name="out_proj",
    )(attn, z, x, mods4, wa_bf16, wz_bf16)


def _gelu_tanh(x):
    return 0.5 * x * (1.0 + jnp.tanh(math.sqrt(2.0 / math.pi) * (x + 0.044715 * (x * x * x))))


def _glu_kernel(y_ref, x_ref, gate_ref, w_ref, o_ref):
    ag = jnp.dot(_gelu_tanh(y_ref[...]).astype(BF16), w_ref[...], preferred_element_type=F32)
    a, g = ag[:, :D_MODEL], ag[:, D_MODEL:]
    o_ref[...] = x_ref[...] + gate_ref[0, 0] * (a * jax.nn.sigmoid(g))


def _glu_proj(y, x, mods4, k_gate, w_bf16):
    t = x.shape[0]
    tm = ROW_TILE
    return pl.pallas_call(
        _glu_kernel,
        grid=(t // tm,),
        in_specs=[pl.BlockSpec((tm, S5_WIDTH), lambda i: (i, 0)),
                  pl.BlockSpec((tm, D_MODEL), lambda i: (i, 0)),
                  _mod_spec(k_gate, tm),
                  _resident((S5_WIDTH, 2 * D_MODEL))],
        out_specs=pl.BlockSpec((tm, D_MODEL), lambda i: (i, 0)),
        out_shape=jax.ShapeDtypeStruct((t, D_MODEL), F32),
        compiler_params=_params(("parallel",)),
        name="glu_proj",
    )(y, x, mods4, w_bf16)


def _gmm_kernel(be_ref, xs_ref, w13_ref, w2_ref, o_ref, w13_b, w2_b):
    i = pl.program_id(0)
    fresh = jnp.logical_or(i == 0, be_ref[i] != be_ref[jnp.maximum(i - 1, 0)])

    @pl.when(fresh)
    def _():
        w13_b[...] = w13_ref[0].astype(BF16)
        w2_b[...] = w2_ref[0].astype(BF16)

    gu = jnp.dot(xs_ref[...], w13_b[...], preferred_element_type=F32)
    hmid = _silu(gu[:, :EXPERT_HIDDEN]) * gu[:, EXPERT_HIDDEN:]
    o_ref[...] = jnp.dot(hmid.astype(BF16), w2_b[...], preferred_element_type=F32)


def _grouped_experts(blk_e, xs, w13, w2):
    n_rows = xs.shape[0]
    n_blocks = n_rows // MOE_BLOCK
    grid_spec = pltpu.PrefetchScalarGridSpec(
        num_scalar_prefetch=1,
        grid=(n_blocks,),
        in_specs=[pl.BlockSpec((MOE_BLOCK, D_MODEL), lambda i, be: (i, 0)),
                  pl.BlockSpec((1, D_MODEL, 2 * EXPERT_HIDDEN), lambda i, be: (be[i], 0, 0)),
                  pl.BlockSpec((1, EXPERT_HIDDEN, D_MODEL), lambda i, be: (be[i], 0, 0))],
        out_specs=pl.BlockSpec((MOE_BLOCK, D_MODEL), lambda i, be: (i, 0)),
        scratch_shapes=[pltpu.VMEM((D_MODEL, 2 * EXPERT_HIDDEN), BF16),
                        pltpu.VMEM((EXPERT_HIDDEN, D_MODEL), BF16)],
    )
    return pl.pallas_call(
        _gmm_kernel,
        grid_spec=grid_spec,
        out_shape=jax.ShapeDtypeStruct((n_rows, D_MODEL), F32),
        compiler_params=_params(("arbitrary",)),
        name="grouped_experts",
    )(blk_e, xs, w13, w2)


def _shared_kernel(h_ref, x_ref, routed_ref, gate_ref, w13_ref, w2_ref, o_ref):
    gu = jnp.dot(h_ref[...], w13_ref[...], preferred_element_type=F32)
    hmid = _silu(gu[:, :EXPERT_HIDDEN]) * gu[:, EXPERT_HIDDEN:]
    shared = jnp.dot(hmid.astype(BF16), w2_ref[...], preferred_element_type=F32)
    o_ref[...] = x_ref[...] + gate_ref[0, 0] * (routed_ref[...] + shared)


def _shared_and_combine(h_bf16, x, routed, mods4, k_gate, w13_bf16, w2_bf16):
    t = x.shape[0]
    tm = ROW_TILE
    row = lambda i: (i, 0)
    return pl.pallas_call(
        _shared_kernel,
        grid=(t // tm,),
        in_specs=[pl.BlockSpec((tm, D_MODEL), row), pl.BlockSpec((tm, D_MODEL), row),
                  pl.BlockSpec((tm, D_MODEL), row), _mod_spec(k_gate, tm),
                  _resident((D_MODEL, 2 * EXPERT_HIDDEN)), _resident((EXPERT_HIDDEN, D_MODEL))],
        out_specs=pl.BlockSpec((tm, D_MODEL), row),
        out_shape=jax.ShapeDtypeStruct((t, D_MODEL), F32),
        compiler_params=_params(("parallel",)),
        name="shared_expert_combine",
    )(h_bf16, x, routed, mods4, w13_bf16, w2_bf16)


def _route(h, router_w, router_bias):
    t = h.shape[0]
    scores = jax.nn.sigmoid(h @ router_w)
    choice = scores + router_bias
    per_group = N_EXPERTS // N_EXPERT_GROUPS
    grp_score = lax.top_k(choice.reshape(t, N_EXPERT_GROUPS, per_group), 2)[0].sum(-1)
    _, grp_idx = lax.top_k(grp_score, TOPK_GROUPS)
    grp_keep = jax.nn.one_hot(grp_idx, N_EXPERT_GROUPS, dtype=F32).sum(1) > 0
    keep = jnp.repeat(grp_keep, per_group, axis=1)
    _, top_idx = lax.top_k(jnp.where(keep, choice, -jnp.inf), TOP_K)
    gate = jnp.take_along_axis(scores, top_idx, axis=1)
    gate = gate / jnp.sum(gate, axis=-1, keepdims=True) * ROUTED_SCALE
    return top_idx, gate


def _moe(x, norm_g, mods4, k_shift, k_scale, k_gate, router_w, router_bias, exp_w13, exp_w2, sh_w13, sh_w2):
    t = x.shape[0]
    h, h_bf16 = _norm_mod_only(x, norm_g, mods4, k_shift, k_scale)
    top_idx, gate = _route(h, router_w, router_bias)
    n_assign = t * TOP_K
    flat_e = top_idx.reshape(-1)
    order = jnp.argsort(flat_e)
    se = flat_e[order]
    stok = (order // TOP_K).astype(jnp.int32)
    sw = gate.reshape(-1)[order]
    counts = jnp.bincount(flat_e, length=N_EXPERTS)
    padded = (counts + MOE_BLOCK - 1) // MOE_BLOCK * MOE_BLOCK
    start = jnp.cumsum(counts) - counts
    pend = jnp.cumsum(padded)
    dest = (pend - padded)[se] + jnp.arange(n_assign) - start[se]
    n_rows = -(-(n_assign + N_EXPERTS * (MOE_BLOCK - 1)) // MOE_BLOCK) * MOE_BLOCK
    n_blocks = n_rows // MOE_BLOCK
    row_tok = jnp.zeros((n_rows,), jnp.int32).at[dest].set(stok)
    row_w = jnp.zeros((n_rows,), F32).at[dest].set(sw)
    blk_e = jnp.minimum(jnp.searchsorted(pend, jnp.arange(n_blocks) * MOE_BLOCK, side='right'),
                        N_EXPERTS - 1).astype(jnp.int32)
    xs = h_bf16[row_tok]
    ys = _grouped_experts(blk_e, xs, exp_w13, exp_w2)
    routed = jax.ops.segment_sum(ys * row_w[:, None], row_tok, num_segments=t)
    return _shared_and_combine(h_bf16, x, routed, mods4, k_gate, sh_w13.astype(BF16), sh_w2.astype(BF16))


def _short_conv3(u, w, b):
    up = jnp.pad(u, ((0, 0), (1, 1), (0, 0)))
    return up[:, :-2] * w[0] + up[:, 1:-1] * w[1] + up[:, 2:] * w[2] + b


def _hyena_filters(L, w1, b1, w2, b2, w3, sin_freq):
    t_norm = jnp.linspace(0.0, 1.0, L, dtype=F32)[:, None]
    omega = 2.0 * math.pi * jnp.arange(L, dtype=F32)[:, None] / L
    bands = jnp.linspace(1e-4, FILTER_BANDS - 1, FILTER_BANDS, dtype=F32)[None, :]
    z = jnp.concatenate([t_norm, jnp.cos(bands * omega), -jnp.sin(bands * omega)], axis=-1)
    h = jnp.sin(sin_freq * (z @ w1 + b1))
    h = jnp.sin(sin_freq * (h @ w2 + b2))
    h = (h @ w3).reshape(L, 2, HYENA_ORDER, HYENA_WIDTH)
    deltas = jnp.abs(jnp.linspace(math.log(DECAY_TARGET) / DECAY_PCT_LONG,
                                  math.log(DECAY_TARGET) / DECAY_PCT_SHORT, HYENA_WIDTH, dtype=F32))
    h = h * jnp.exp(-t_norm * deltas)[:, None, None, :]
    fwd, bwd = h[:, 0], h[:, 1]
    filt = jnp.concatenate([fwd, jnp.zeros_like(fwd[:1]), bwd[:0:-1]], axis=0)
    return filt / jnp.sum(jnp.abs(filt), axis=0, keepdims=True)


def _fft_long_conv(u, filt, bias):
    L = u.shape[1]
    u_f = jnp.fft.rfft(u, n=2 * L, axis=1)
    f_f = jnp.fft.rfft(filt, n=2 * L, axis=0)
    y = jnp.fft.irfft(u_f * f_f[None], n=2 * L, axis=1)[:, :L]
    return y + u * bias


def _hyena(hy, conv_w, conv_b, f_w1, f_b1, f_w2, f_b2, f_w3, sin_freq, hy_bias):
    s = _short_conv3(hy, conv_w, conv_b)
    v, x1, x2 = jnp.split(s, HYENA_ORDER + 1, axis=-1)
    filt = _hyena_filters(hy.shape[1], f_w1, f_b1, f_w2, f_b2, f_w3, sin_freq)
    z = x1 * _fft_long_conv(v, filt[:, 0], hy_bias[0])
    return x2 * _fft_long_conv(z, filt[:, 1], hy_bias[1])


def _ssm_combine(left, right):
    a_l, b_l = left
    a_r, b_r = right
    return a_r * a_l, a_r * b_l + b_r


def _ssm_scan(lam_bar, bu):
    a = jnp.broadcast_to(lam_bar, bu.shape)
    return lax.associative_scan(_ssm_combine, (a, bu))[1]


def _s5_direction(ug, lam_bar, b_bar, c_mat, s0):
    bu = lax.complex(jnp.einsum('blgc,gpc->blgp', ug, jnp.real(b_bar)),
                     jnp.einsum('blgc,gpc->blgp', ug, jnp.imag(b_bar)))
    bu = bu.at[:, 0].add(lam_bar * s0)
    states = jax.vmap(_ssm_scan, in_axes=(None, 0))(lam_bar, bu)
    y = jnp.real(jnp.einsum('blgp,gcp->blgc', states, c_mat))
    return y, states[:, -1]


def _s5_core(u, s0_re, s0_im, a_re, a_im, log_dt, b_re, b_im, c_re, c_im, d_skip):
    B_, L, _ = u.shape
    ug = u.reshape(B_, L, S5_GROUPS, S5_GROUP_CH)
    lam = lax.complex(a_re, a_im)
    lam_bar = jnp.exp(lam * jnp.exp(log_dt)[..., None])
    b_bar = ((lam_bar - 1.0) / lam)[..., None] * lax.complex(b_re, b_im)
    c_mat = lax.complex(c_re, c_im)
    s0 = lax.complex(s0_re, s0_im)
    y_f, s_f = _s5_direction(ug, lam_bar[0], b_bar[0], c_mat[0], s0[:, 0])
    y_b, s_b = _s5_direction(ug[:, ::-1], lam_bar[1], b_bar[1], c_mat[1], s0[:, 1])
    y = (y_f + y_b[:, ::-1]).reshape(B_, L, S5_WIDTH) + d_skip * u
    s_final = jnp.stack([s_f, s_b], axis=1)
    return y, jnp.real(s_final), jnp.imag(s_final)


def _final_norm_kernel(x_ref, g_ref, o_ref):
    x = x_ref[...]
    ms = jnp.mean(x * x, axis=-1, keepdims=True)
    o_ref[...] = x * lax.rsqrt(ms + RMS_EPS) * g_ref[...]


def _final_norm(x, g):
    t = x.shape[0]
    tm = ROW_TILE
    return pl.pallas_call(
        _final_norm_kernel,
        grid=(t // tm,),
        in_specs=[pl.BlockSpec((tm, D_MODEL), lambda i: (i, 0)), _resident((1, D_MODEL))],
        out_specs=pl.BlockSpec((tm, D_MODEL), lambda i: (i, 0)),
        out_shape=jax.ShapeDtypeStruct((t, D_MODEL), F32),
        compiler_params=_params(("parallel",)),
        name="final_norm",
    )(x, g.reshape(1, D_MODEL))


def _rope_tables(L):
    n_rows = L // GRID_W
    row_idx = jnp.repeat(jnp.arange(n_rows, dtype=F32), GRID_W)
    col_idx = (jnp.arange(L) % GRID_W).astype(F32)
    inv = ROPE_THETA ** (-jnp.arange(0, ROPE_AXIS_DIM, 2, dtype=F32) / ROPE_AXIS_DIM)
    ang = jnp.concatenate([row_idx[:, None] * inv, col_idx[:, None] * inv], axis=-1)
    cos, sin = jnp.cos(ang), jnp.sin(ang)
    cos_full = jnp.repeat(cos, 2, axis=-1)
    sin_signed = jnp.stack([-sin, sin], axis=-1).reshape(L, HEAD_DIM)
    return cos_full, sin_signed


def kernel(x_prompt, x_sample, cache_k_l0, cache_v_l0, state_s5_re_l1, state_s5_im_l1, c, c_ctx,
           ada_w_l0, ada_b_l0, norm1_l0, norm2_l0,
           w_in_l0, w_out_l0, q_norm_l0, k_norm_l0, hy_conv_w_l0, hy_conv_b_l0,
           hy_ffn_w1_l0, hy_ffn_b1_l0, hy_ffn_w2_l0, hy_ffn_b2_l0, hy_ffn_w3_l0, hy_sin_freq_l0, hy_bias_l0,
           router_l0, router_bias_l0, exp_w13_l0, exp_w2_l0, shared_w13_l0, shared_w2_l0,
           ada_w_l1, ada_b_l1, norm1_l1, norm2_l1,
           w_in_l1, s5_a_re_l1, s5_a_im_l1, s5_log_dt_l1, s5_b_re_l1, s5_b_im_l1,
           s5_c_re_l1, s5_c_im_l1, s5_d_l1, w_glu_l1,
           router_l1, router_bias_l1, exp_w13_l1, exp_w2_l1, shared_w13_l1, shared_w2_l1,
           final_norm):
    x = jnp.concatenate([x_prompt.reshape(T_CTX, D_MODEL), x_sample.reshape(T_LAT, D_MODEL)], axis=0)
    cond = jnp.concatenate([c_ctx[None, :], c, jnp.zeros((N_COND - 1 - DEC_BATCH, D_MODEL), F32)], axis=0)

    mods = _ada_mods(cond, ada_w_l0, ada_b_l0)
    proj = _norm_mod_matmul(x, norm1_l0, mods, 0, 1, w_in_l0.astype(BF16))

    q_c, k_c, v_c, new_k = _qk_prep(proj, 0, T_CTX, q_norm_l0, k_norm_l0, None)
    attn_c = _attention(q_c, k_c, v_c, BATCH, SEQ, None)
    q_l, k_l, v_l = _qk_prep(proj, T_CTX, T_LAT, q_norm_l0, k_norm_l0, _rope_tables(DEC_SEQ))
    ctx_kv = (cache_k_l0.reshape(DEC_BATCH * PAST_LEN, KV_WIDTH).astype(BF16),
              cache_v_l0.reshape(DEC_BATCH * PAST_LEN, KV_WIDTH).astype(BF16))
    attn_l = _attention(q_l, k_l, v_l, DEC_BATCH, DEC_SEQ, ctx_kv)
    attn = jnp.concatenate([attn_c, attn_l], axis=0)

    hy_p = (hy_conv_w_l0, hy_conv_b_l0, hy_ffn_w1_l0, hy_ffn_b1_l0, hy_ffn_w2_l0, hy_ffn_b2_l0,
            hy_ffn_w3_l0, hy_sin_freq_l0, hy_bias_l0)
    z_c = _hyena(proj[:T_CTX, QKV_WIDTH:].reshape(BATCH, SEQ, -1), *hy_p).reshape(T_CTX, HYENA_WIDTH)
    z_l = _hyena(proj[T_CTX:, QKV_WIDTH:].reshape(DEC_BATCH, DEC_SEQ, -1), *hy_p).reshape(T_LAT, HYENA_WIDTH)
    z = jnp.concatenate([z_c, z_l], axis=0)

    w_out = w_out_l0.astype(BF16)
    x = _out_proj(attn, z, x, mods, 2, w_out[:Q_WIDTH], w_out[Q_WIDTH:])
    x = _moe(x, norm2_l0, mods, 3, 4, 5, router_l0, router_bias_l0, exp_w13_l0, exp_w2_l0,
             shared_w13_l0, shared_w2_l0)

    new_v = proj[:T_CTX, Q_WIDTH + KV_WIDTH:QKV_WIDTH]

    mods = _ada_mods(cond, ada_w_l1, ada_b_l1)
    u = _norm_mod_matmul(x, norm1_l1, mods, 0, 1, w_in_l1.astype(BF16))
    s5_p = (s5_a_re_l1, s5_a_im_l1, s5_log_dt_l1, s5_b_re_l1, s5_b_im_l1, s5_c_re_l1, s5_c_im_l1, s5_d_l1)
    zero_state = jnp.zeros((BATCH, 2, S5_GROUPS, S5_STATE), F32)
    y_c, s_re, s_im = _s5_core(u[:T_CTX].reshape(BATCH, SEQ, S5_WIDTH), zero_state, zero_state, *s5_p)
    y_l, _, _ = _s5_core(u[T_CTX:].reshape(DEC_BATCH, DEC_SEQ, S5_WIDTH), state_s5_re_l1, state_s5_im_l1, *s5_p)
    y = jnp.concatenate([y_c.reshape(T_CTX, S5_WIDTH), y_l.reshape(T_LAT, S5_WIDTH)], axis=0)
    x = _glu_proj(y, x, mods, 2, w_glu_l1.astype(BF16))
    x = _moe(x, norm2_l1, mods, 3, 4, 5, router_l1, router_bias_l1, exp_w13_l1, exp_w2_l1,
             shared_w13_l1, shared_w2_l1)

    y_all = _final_norm(x, final_norm)
    return (y_all[:T_CTX].reshape(BATCH, SEQ, D_MODEL),
            y_all[T_CTX:].reshape(DEC_BATCH, DEC_SEQ, D_MODEL),
            new_k.reshape(BATCH, SEQ, N_KV_HEADS, HEAD_DIM),
            new_v.reshape(BATCH, SEQ, N_KV_HEADS, HEAD_DIM),
            s_re, s_im)
```

```python
import functools
import math

import jax
import jax.numpy as jnp
from jax import lax
from jax.experimental import pallas as pl
from jax.experimental.pallas import tpu as pltpu

F32 = jnp.float32
BF16 = jnp.bfloat16

D_MODEL = 2048
BATCH = 32
SEQ = 256
DEC_BATCH = 4
DEC_SEQ = 4096
PAST_LEN = 256
GRID_W = 64
RMS_EPS = 1e-6
HEAD_DIM = 128
N_HEADS = 8
N_KV_HEADS = 2
GQA_GROUP = N_HEADS // N_KV_HEADS
Q_WIDTH = N_HEADS * HEAD_DIM
KV_WIDTH = N_KV_HEADS * HEAD_DIM
QKV_WIDTH = Q_WIDTH + 2 * KV_WIDTH
ROPE_THETA = 10000.0
ROPE_AXIS_DIM = HEAD_DIM // 2
HYENA_WIDTH = D_MODEL // 2
HYENA_ORDER = 2
FILTER_BANDS = 16
DECAY_TARGET = 1e-2
DECAY_PCT_SHORT = 0.3
DECAY_PCT_LONG = 1.5
S5_WIDTH = D_MODEL // 2
S5_GROUP_CH = 16
S5_GROUPS = S5_WIDTH // S5_GROUP_CH
S5_STATE = 64
N_EXPERTS = 64
TOP_K = 8
N_EXPERT_GROUPS = 8
TOPK_GROUPS = 4
EXPERT_HIDDEN = 512
ROUTED_SCALE = 2.5
MOE_BLOCK = 128

T_CTX = BATCH * SEQ
T_LAT = DEC_BATCH * DEC_SEQ
T_ALL = T_CTX + T_LAT
N_COND = 8

ROW_TILE = 256
V7X_VMEM_LIMIT_BYTES = 56 * 1024 * 1024


def _params(semantics):
    return pltpu.CompilerParams(dimension_semantics=semantics, vmem_limit_bytes=V7X_VMEM_LIMIT_BYTES)


def _cond_row(i, tm):
    tok = i * tm
    return jnp.where(tok < T_CTX, 0, 1 + (tok - T_CTX) // DEC_SEQ)


def _mod_spec(k, tm):
    return pl.BlockSpec((1, 1, 1, D_MODEL), lambda i: (_cond_row(i, tm), k, 0, 0))


def _resident(shape):
    nd = len(shape)
    return pl.BlockSpec(shape, lambda i: (0,) * nd, pipeline_mode=pl.Buffered(1))


def _silu(x):
    return x * jax.nn.sigmoid(x)


def _norm_mod(x, g, sc, sh):
    ms = jnp.mean(x * x, axis=-1, keepdims=True)
    return (x * lax.rsqrt(ms + RMS_EPS) * g) * (1.0 + sc) + sh


def _ada_kernel(c_ref, w_ref, b_ref, o_ref):
    a = _silu(c_ref[...]).astype(BF16)
    o_ref[...] = jnp.dot(a, w_ref[...].astype(BF16), preferred_element_type=F32) + b_ref[...]


def _ada_mods(cond, w, b):
    n = w.shape[1]
    tn = 1024
    out = pl.pallas_call(
        _ada_kernel,
        grid=(n // tn,),
        in_specs=[pl.BlockSpec((N_COND, D_MODEL), lambda j: (0, 0)),
                  pl.BlockSpec((D_MODEL, tn), lambda j: (0, j)),
                  pl.BlockSpec((1, tn), lambda j: (0, j))],
        out_specs=pl.BlockSpec((N_COND, tn), lambda j: (0, j)),
        out_shape=jax.ShapeDtypeStruct((N_COND, n), F32),
        compiler_params=_params(("parallel",)),
        name="ada_mods",
    )(cond, w, b.reshape(1, n))
    return out.reshape(N_COND, 6, 1, D_MODEL)


def _nmm_kernel(x_ref, g_ref, sc_ref, sh_ref, w_ref, o_ref):
    h = _norm_mod(x_ref[...], g_ref[...], sc_ref[0, 0], sh_ref[0, 0])
    o_ref[...] = jnp.dot(h.astype(BF16), w_ref[...], preferred_element_type=F32)


def _norm_mod_matmul(x, g, mods4, k_shift, k_scale, w_bf16):
    t, n = x.shape[0], w_bf16.shape[1]
    tm = ROW_TILE
    return pl.pallas_call(
        _nmm_kernel,
        grid=(t // tm,),
        in_specs=[pl.BlockSpec((tm, D_MODEL), lambda i: (i, 0)),
                  _resident((1, D_MODEL)),
                  _mod_spec(k_scale, tm), _mod_spec(k_shift, tm),
                  _resident((D_MODEL, n))],
        out_specs=pl.BlockSpec((tm, n), lambda i: (i, 0)),
        out_shape=jax.ShapeDtypeStruct((t, n), F32),
        compiler_params=_params(("parallel",)),
        name="norm_mod_matmul",
    )(x, g.reshape(1, D_MODEL), mods4, mods4, w_bf16)


def _nm_kernel(x_ref, g_ref, sc_ref, sh_ref, o_ref, ob_ref):
    h = _norm_mod(x_ref[...], g_ref[...], sc_ref[0, 0], sh_ref[0, 0])
    o_ref[...] = h
    ob_ref[...] = h.astype(BF16)


def _norm_mod_only(x, g, mods4, k_shift, k_scale):
    t = x.shape[0]
    tm = ROW_TILE
    return pl.pallas_call(
        _nm_kernel,
        grid=(t // tm,),
        in_specs=[pl.BlockSpec((tm, D_MODEL), lambda i: (i, 0)),
                  _resident((1, D_MODEL)),
                  _mod_spec(k_scale, tm), _mod_spec(k_shift, tm)],
        out_specs=[pl.BlockSpec((tm, D_MODEL), lambda i: (i, 0)),
                   pl.BlockSpec((tm, D_MODEL), lambda i: (i, 0))],
        out_shape=[jax.ShapeDtypeStruct((t, D_MODEL), F32), jax.ShapeDtypeStruct((t, D_MODEL), BF16)],
        compiler_params=_params(("parallel",)),
        name="norm_mod",
    )(x, g.reshape(1, D_MODEL), mods4, mods4)


def _head_norm(xh, g):
    ms = jnp.mean(xh * xh, axis=-1, keepdims=True)
    return xh * lax.rsqrt(ms + RMS_EPS) * g


def _rope(y, cos, sin_signed):
    lane = lax.broadcasted_iota(jnp.int32, y.shape, 1)
    partner = jnp.where(lane % 2 == 0, pltpu.roll(y, HEAD_DIM - 1, 1), pltpu.roll(y, 1, 1))
    return y * cos + partner * sin_signed


def _qkprep_ctx_kernel(qkv_ref, qg_ref, kg_ref, q_out, kb_out, vb_out, kf_out):
    for h in range(N_HEADS):
        sl = slice(h * HEAD_DIM, (h + 1) * HEAD_DIM)
        q_out[:, sl] = _head_norm(qkv_ref[:, sl], qg_ref[...]).astype(BF16)
    for j in range(N_KV_HEADS):
        k = _head_norm(qkv_ref[:, Q_WIDTH + j * HEAD_DIM:Q_WIDTH + (j + 1) * HEAD_DIM], kg_ref[...])
        kf_out[:, j * HEAD_DIM:(j + 1) * HEAD_DIM] = k
        kb_out[:, j * HEAD_DIM:(j + 1) * HEAD_DIM] = k.astype(BF16)
    vb_out[...] = qkv_ref[:, Q_WIDTH + KV_WIDTH:QKV_WIDTH].astype(BF16)


def _qkprep_lat_kernel(qkv_ref, qg_ref, kg_ref, cos_ref, sin_ref, q_out, kb_out, vb_out):
    cos, sin = cos_ref[...], sin_ref[...]
    for h in range(N_HEADS):
        sl = slice(h * HEAD_DIM, (h + 1) * HEAD_DIM)
        q_out[:, sl] = _rope(_head_norm(qkv_ref[:, sl], qg_ref[...]), cos, sin).astype(BF16)
    for j in range(N_KV_HEADS):
        k = _head_norm(qkv_ref[:, Q_WIDTH + j * HEAD_DIM:Q_WIDTH + (j + 1) * HEAD_DIM], kg_ref[...])
        kb_out[:, j * HEAD_DIM:(j + 1) * HEAD_DIM] = _rope(k, cos, sin).astype(BF16)
    vb_out[...] = qkv_ref[:, Q_WIDTH + KV_WIDTH:QKV_WIDTH].astype(BF16)


def _qk_prep(proj, row0, t, q_norm, k_norm, rope_tabs):
    tm = ROW_TILE
    b0 = row0 // tm
    qkv_spec = pl.BlockSpec((tm, QKV_WIDTH), lambda i: (b0 + i, 0))
    gain = _resident((1, HEAD_DIM))
    outs = [pl.BlockSpec((tm, Q_WIDTH), lambda i: (i, 0)),
            pl.BlockSpec((tm, KV_WIDTH), lambda i: (i, 0)),
            pl.BlockSpec((tm, KV_WIDTH), lambda i: (i, 0))]
    shapes = [jax.ShapeDtypeStruct((t, Q_WIDTH), BF16),
              jax.ShapeDtypeStruct((t, KV_WIDTH), BF16),
              jax.ShapeDtypeStruct((t, KV_WIDTH), BF16)]
    qg, kg = q_norm.reshape(1, HEAD_DIM), k_norm.reshape(1, HEAD_DIM)
    if rope_tabs is None:
        return pl.pallas_call(
            _qkprep_ctx_kernel, grid=(t // tm,),
            in_specs=[qkv_spec, gain, gain],
            out_specs=outs + [pl.BlockSpec((tm, KV_WIDTH), lambda i: (i, 0))],
            out_shape=shapes + [jax.ShapeDtypeStruct((t, KV_WIDTH), F32)],
            compiler_params=_params(("parallel",)), name="qk_prep_ctx",
        )(proj, qg, kg)
    cos, sin = rope_tabs
    nq = DEC_SEQ // tm
    tab = pl.BlockSpec((tm, HEAD_DIM), lambda i: (i % nq, 0))
    return pl.pallas_call(
        _qkprep_lat_kernel, grid=(t // tm,),
        in_specs=[qkv_spec, gain, gain, tab, tab],
        out_specs=outs, out_shape=shapes,
        compiler_params=_params(("parallel",)), name="qk_prep_lat",
    )(proj, qg, kg, cos, sin)


def _attn_kernel(*refs, with_ctx):
    if with_ctx:
        q_ref, k_ref, v_ref, ck_ref, cv_ref, o_ref = refs
    else:
        q_ref, k_ref, v_ref, o_ref = refs
    scale = HEAD_DIM ** -0.5
    nt = (((1,), (1,)), ((), ()))
    q = q_ref[...]
    s = lax.dot_general(q, k_ref[...], nt, preferred_element_type=F32) * scale
    m = jnp.max(s, axis=-1, keepdims=True)
    if with_ctx:
        sc = lax.dot_general(q, ck_ref[...], nt, preferred_element_type=F32) * scale
        m = jnp.maximum(m, jnp.max(sc, axis=-1, keepdims=True))
    p = jnp.exp(s - m)
    l = jnp.sum(p, axis=-1, keepdims=True)
    o = jnp.dot(p.astype(BF16), v_ref[...], preferred_element_type=F32)
    if with_ctx:
        pc = jnp.exp(sc - m)
        l = l + jnp.sum(pc, axis=-1, keepdims=True)
        o = o + jnp.dot(pc.astype(BF16), cv_ref[...], preferred_element_type=F32)
    o_ref[...] = (o / l).astype(BF16)


def _attention(q, k, v, n_batch, seq, ctx_kv):
    tq = ROW_TILE
    nq = seq // tq
    t = n_batch * seq
    q_spec = pl.BlockSpec((tq, HEAD_DIM), lambda b, h, i: (b * nq + i, h))
    kv_spec = pl.BlockSpec((seq, HEAD_DIM), lambda b, h, i: (b, h // GQA_GROUP))
    in_specs = [q_spec, kv_spec, kv_spec]
    args = [q, k, v]
    if ctx_kv is not None:
        c_spec = pl.BlockSpec((PAST_LEN, HEAD_DIM), lambda b, h, i: (b, h // GQA_GROUP))
        in_specs += [c_spec, c_spec]
        args += list(ctx_kv)
    return pl.pallas_call(
        functools.partial(_attn_kernel, with_ctx=ctx_kv is not None),
        grid=(n_batch, N_HEADS, nq),
        in_specs=in_specs,
        out_specs=pl.BlockSpec((tq, HEAD_DIM), lambda b, h, i: (b * nq + i, h)),
        out_shape=jax.ShapeDtypeStruct((t, Q_WIDTH), BF16),
        compiler_params=_params(("parallel", "parallel", "parallel")),
        name="attention_lat" if ctx_kv is not None else "attention_ctx",
    )(*args)


def _outproj_kernel(a_ref, z_ref, x_ref, gate_ref, wa_ref, wz_ref, o_ref):
    acc = jnp.dot(a_ref[...], wa_ref[...], preferred_element_type=F32)
    acc = acc + jnp.dot(z_ref[...].astype(BF16), wz_ref[...], preferred_element_type=F32)
    o_ref[...] = x_ref[...] + gate_ref[0, 0] * acc


def _out_proj(attn, z, x, mods4, k_gate, wa_bf16, wz_bf16):
    t = x.shape[0]
    tm = ROW_TILE
    return pl.pallas_call(
        _outproj_kernel,
        grid=(t // tm,),
        in_specs=[pl.BlockSpec((tm, Q_WIDTH), lambda i: (i, 0)),
                  pl.BlockSpec((tm, HYENA_WIDTH), lambda i: (i, 0)),
                  pl.BlockSpec((tm, D_MODEL), lambda i: (i, 0)),
                  _mod_spec(k_gate, tm),
                  _resident((Q_WIDTH, D_MODEL)), _resident((HYENA_WIDTH, D_MODEL))],
        out_specs=pl.BlockSpec((tm, D_MODEL), lambda i: (i, 0)),
        out_shape=jax.ShapeDtypeStruct((t, D_MODEL), F32),
        compiler_params=_params(("parallel",)),
        name="out_proj",
    )(attn, z, x, mods4, wa_bf16, wz_bf16)


def _gelu_tanh(x):
    return 0.5 * x * (1.0 + jnp.tanh(math.sqrt(2.0 / math.pi) * (x + 0.044715 * (x * x * x))))


def _glu_kernel(yf_ref, yb_ref, u_ref, d_ref, x_ref, gate_ref, w_ref, o_ref):
    y = (yf_ref[...] + yb_ref[...]) + d_ref[...] * u_ref[...]
    ag = jnp.dot(_gelu_tanh(y).astype(BF16), w_ref[...], preferred_element_type=F32)
    a, g = ag[:, :D_MODEL], ag[:, D_MODEL:]
    o_ref[...] = x_ref[...] + gate_ref[0, 0] * (a * jax.nn.sigmoid(g))


def _glu_proj(y_fwd, y_bwd, u, d_skip, x, mods4, k_gate, w_bf16):
    t = x.shape[0]
    tm = ROW_TILE
    s5_rows = pl.BlockSpec((tm, S5_WIDTH), lambda i: (i, 0))
    return pl.pallas_call(
        _glu_kernel,
        grid=(t // tm,),
        in_specs=[s5_rows, s5_rows, s5_rows, _resident((1, S5_WIDTH)),
                  pl.BlockSpec((tm, D_MODEL), lambda i: (i, 0)),
                  _mod_spec(k_gate, tm),
                  _resident((S5_WIDTH, 2 * D_MODEL))],
        out_specs=pl.BlockSpec((tm, D_MODEL), lambda i: (i, 0)),
        out_shape=jax.ShapeDtypeStruct((t, D_MODEL), F32),
        compiler_params=_params(("parallel",)),
        name="glu_proj",
    )(y_fwd, y_bwd, u, d_skip.reshape(1, S5_WIDTH), x, mods4, w_bf16)


def _gmm_kernel(be_ref, xs_ref, w13_ref, w2_ref, o_ref, w13_b, w2_b):
    i = pl.program_id(0)
    fresh = jnp.logical_or(i == 0, be_ref[i] != be_ref[jnp.maximum(i - 1, 0)])

    @pl.when(fresh)
    def _():
        w13_b[...] = w13_ref[0].astype(BF16)
        w2_b[...] = w2_ref[0].astype(BF16)

    gu = jnp.dot(xs_ref[...], w13_b[...], preferred_element_type=F32)
    hmid = _silu(gu[:, :EXPERT_HIDDEN]) * gu[:, EXPERT_HIDDEN:]
    o_ref[...] = jnp.dot(hmid.astype(BF16), w2_b[...], preferred_element_type=F32)


def _grouped_experts(blk_e, xs, w13, w2):
    n_rows = xs.shape[0]
    n_blocks = n_rows // MOE_BLOCK
    grid_spec = pltpu.PrefetchScalarGridSpec(
        num_scalar_prefetch=1,
        grid=(n_blocks,),
        in_specs=[pl.BlockSpec((MOE_BLOCK, D_MODEL), lambda i, be: (i, 0)),
                  pl.BlockSpec((1, D_MODEL, 2 * EXPERT_HIDDEN), lambda i, be: (be[i], 0, 0)),
                  pl.BlockSpec((1, EXPERT_HIDDEN, D_MODEL), lambda i, be: (be[i], 0, 0))],
        out_specs=pl.BlockSpec((MOE_BLOCK, D_MODEL), lambda i, be: (i, 0)),
        scratch_shapes=[pltpu.VMEM((D_MODEL, 2 * EXPERT_HIDDEN), BF16),
                        pltpu.VMEM((EXPERT_HIDDEN, D_MODEL), BF16)],
    )
    return pl.pallas_call(
        _gmm_kernel,
        grid_spec=grid_spec,
        out_shape=jax.ShapeDtypeStruct((n_rows, D_MODEL), F32),
        compiler_params=_params(("arbitrary",)),
        name="grouped_experts",
    )(blk_e, xs, w13, w2)


COMBINE_TILE = 128


def _shared_kernel(h_ref, x_ref, ye_ref, w_ref, gate_ref, w13_ref, w2_ref, o_ref):
    gu = jnp.dot(h_ref[...], w13_ref[...], preferred_element_type=F32)
    hmid = _silu(gu[:, :EXPERT_HIDDEN]) * gu[:, EXPERT_HIDDEN:]
    acc = jnp.dot(hmid.astype(BF16), w2_ref[...], preferred_element_type=F32)
    wts = w_ref[...]
    for k in range(TOP_K):
        acc = acc + wts[:, k:k + 1] * ye_ref[k]
    o_ref[...] = x_ref[...] + gate_ref[0, 0] * acc


def _shared_and_combine(h_bf16, x, y_exp, wts, mods4, k_gate, w13_bf16, w2_bf16):
    t = x.shape[0]
    tm = COMBINE_TILE
    row = lambda i: (i, 0)
    return pl.pallas_call(
        _shared_kernel,
        grid=(t // tm,),
        in_specs=[pl.BlockSpec((tm, D_MODEL), row), pl.BlockSpec((tm, D_MODEL), row),
                  pl.BlockSpec((TOP_K, tm, D_MODEL), lambda i: (0, i, 0)),
                  pl.BlockSpec((tm, TOP_K), row), _mod_spec(k_gate, tm),
                  _resident((D_MODEL, 2 * EXPERT_HIDDEN)), _resident((EXPERT_HIDDEN, D_MODEL))],
        out_specs=pl.BlockSpec((tm, D_MODEL), row),
        out_shape=jax.ShapeDtypeStruct((t, D_MODEL), F32),
        compiler_params=_params(("parallel",)),
        name="shared_expert_combine",
    )(h_bf16, x, y_exp, wts, mods4, w13_bf16, w2_bf16)


def _route(h, router_w, router_bias):
    t = h.shape[0]
    scores = jax.nn.sigmoid(h @ router_w)
    choice = scores + router_bias
    per_group = N_EXPERTS // N_EXPERT_GROUPS
    grp_score = lax.top_k(choice.reshape(t, N_EXPERT_GROUPS, per_group), 2)[0].sum(-1)
    _, grp_idx = lax.top_k(grp_score, TOPK_GROUPS)
    grp_keep = jax.nn.one_hot(grp_idx, N_EXPERT_GROUPS, dtype=F32).sum(1) > 0
    keep = jnp.repeat(grp_keep, per_group, axis=1)
    _, top_idx = lax.top_k(jnp.where(keep, choice, -jnp.inf), TOP_K)
    gate = jnp.take_along_axis(scores, top_idx, axis=1)
    gate = gate / jnp.sum(gate, axis=-1, keepdims=True) * ROUTED_SCALE
    return top_idx, gate


def _moe(x, norm_g, mods4, k_shift, k_scale, k_gate, router_w, router_bias, exp_w13, exp_w2, sh_w13, sh_w2):
    t = x.shape[0]
    h, h_bf16 = _norm_mod_only(x, norm_g, mods4, k_shift, k_scale)
    top_idx, gate = _route(h, router_w, router_bias)
    onehot = jnp.sum((top_idx[:, :, None] == jnp.arange(N_EXPERTS, dtype=top_idx.dtype)).astype(jnp.int32), axis=1)
    cum = jnp.cumsum(onehot, axis=0)
    counts = cum[-1]
    rank = jnp.take_along_axis(cum, top_idx, axis=1) - 1
    padded = (counts + MOE_BLOCK - 1) // MOE_BLOCK * MOE_BLOCK
    pend = jnp.cumsum(padded)
    pos = ((pend - padded)[top_idx] + rank).astype(jnp.int32)
    n_rows = -(-(t * TOP_K + N_EXPERTS * (MOE_BLOCK - 1)) // MOE_BLOCK) * MOE_BLOCK
    n_blocks = n_rows // MOE_BLOCK
    tok = jnp.broadcast_to(jnp.arange(t, dtype=jnp.int32)[:, None], (t, TOP_K))
    row_tok = jnp.zeros((n_rows,), jnp.int32).at[pos.reshape(-1)].set(tok.reshape(-1), unique_indices=True)
    blk_e = jnp.minimum(jnp.searchsorted(pend, jnp.arange(n_blocks) * MOE_BLOCK, side='right'),
                        N_EXPERTS - 1).astype(jnp.int32)
    xs = h_bf16[row_tok]
    ys = _grouped_experts(blk_e, xs, exp_w13, exp_w2)
    y_exp = ys[pos.T.reshape(-1)].reshape(TOP_K, t, D_MODEL)
    return _shared_and_combine(h_bf16, x, y_exp, gate, mods4, k_gate, sh_w13.astype(BF16), sh_w2.astype(BF16))


def _short_conv3(u, w, b):
    up = jnp.pad(u, ((0, 0), (1, 1), (0, 0)))
    return up[:, :-2] * w[0] + up[:, 1:-1] * w[1] + up[:, 2:] * w[2] + b


def _hyena_filters(L, w1, b1, w2, b2, w3, sin_freq):
    t_norm = jnp.linspace(0.0, 1.0, L, dtype=F32)[:, None]
    omega = 2.0 * math.pi * jnp.arange(L, dtype=F32)[:, None] / L
    bands = jnp.linspace(1e-4, FILTER_BANDS - 1, FILTER_BANDS, dtype=F32)[None, :]
    z = jnp.concatenate([t_norm, jnp.cos(bands * omega), -jnp.sin(bands * omega)], axis=-1)
    h = jnp.sin(sin_freq * (z @ w1 + b1))
    h = jnp.sin(sin_freq * (h @ w2 + b2))
    h = (h @ w3).reshape(L, 2, HYENA_ORDER, HYENA_WIDTH)
    deltas = jnp.abs(jnp.linspace(math.log(DECAY_TARGET) / DECAY_PCT_LONG,
                                  math.log(DECAY_TARGET) / DECAY_PCT_SHORT, HYENA_WIDTH, dtype=F32))
    h = h * jnp.exp(-t_norm * deltas)[:, None, None, :]
    fwd, bwd = h[:, 0], h[:, 1]
    filt = jnp.concatenate([fwd, jnp.zeros_like(fwd[:1]), bwd[:0:-1]], axis=0)
    return filt / jnp.sum(jnp.abs(filt), axis=0, keepdims=True)


def _fft_long_conv(u, filt, bias):
    L = u.shape[1]
    u_f = jnp.fft.rfft(u, n=2 * L, axis=1)
    f_f = jnp.fft.rfft(filt, n=2 * L, axis=0)
    y = jnp.fft.irfft(u_f * f_f[None], n=2 * L, axis=1)[:, :L]
    return y + u * bias


def _hyena(hy, conv_w, conv_b, f_w1, f_b1, f_w2, f_b2, f_w3, sin_freq, hy_bias):
    s = _short_conv3(hy, conv_w, conv_b)
    v, x1, x2 = jnp.split(s, HYENA_ORDER + 1, axis=-1)
    filt = _hyena_filters(hy.shape[1], f_w1, f_b1, f_w2, f_b2, f_w3, sin_freq)
    z = x1 * _fft_long_conv(v, filt[:, 0], hy_bias[0])
    return x2 * _fft_long_conv(z, filt[:, 1], hy_bias[1])


S5_LANES = S5_GROUPS * S5_STATE
S5_CLUSTER_GROUPS = 16
S5_CLUSTERS = S5_GROUPS // S5_CLUSTER_GROUPS
S5_CLUSTER_CH = S5_CLUSTER_GROUPS * S5_GROUP_CH
S5_CLUSTER_LANES = S5_CLUSTER_GROUPS * S5_STATE
SCAN_CHUNK = 256
SCAN_TILE = 8
SCAN_LEVELS = (1, 2, 4)


def _s5_kernel(u_ref, s0r_ref, s0i_ref, bw_ref, cwr_ref, cwi_ref, tab_ref, y_ref, sfr_ref, sfi_ref,
               sre, sim, car, cai, *, reverse):
    @pl.when(pl.program_id(1) == 0)
    def _():
        car[...] = s0r_ref[0]
        cai[...] = s0i_ref[0]

    ub = u_ref[...].astype(BF16)
    for k in range(S5_CLUSTERS):
        bu = jnp.dot(ub[:, k * S5_CLUSTER_CH:(k + 1) * S5_CLUSTER_CH], bw_ref[k], preferred_element_type=F32)
        sre[:, k * S5_CLUSTER_LANES:(k + 1) * S5_CLUSTER_LANES] = bu[:, :S5_CLUSTER_LANES]
        sim[:, k * S5_CLUSTER_LANES:(k + 1) * S5_CLUSTER_LANES] = bu[:, S5_CLUSTER_LANES:]

    n_tiles = SCAN_CHUNK // SCAN_TILE
    shifts = tuple(SCAN_TILE - k for k in SCAN_LEVELS) if reverse else SCAN_LEVELS
    boundary_row = 0 if reverse else SCAN_TILE - 1
    for k in range(S5_CLUSTERS):
        lanes = slice(k * S5_CLUSTER_LANES, (k + 1) * S5_CLUSTER_LANES)

        def tile_step(i, carry, lanes=lanes):
            cr, ci = carry
            tile = (n_tiles - 1 - i) if reverse else i
            rows = pl.ds(pl.multiple_of(tile * SCAN_TILE, SCAN_TILE), SCAN_TILE)
            xr, xi = sre[rows, lanes], sim[rows, lanes]
            for lvl, sh in enumerate(shifts):
                pr, pi = tab_ref[2 * lvl, :, lanes], tab_ref[2 * lvl + 1, :, lanes]
                rr, ri = pltpu.roll(xr, sh, 0), pltpu.roll(xi, sh, 0)
                xr, xi = xr + (pr * rr - pi * ri), xi + (pr * ri + pi * rr)
            pr, pi = tab_ref[2 * len(shifts), :, lanes], tab_ref[2 * len(shifts) + 1, :, lanes]
            xr, xi = xr + (pr * cr - pi * ci), xi + (pr * ci + pi * cr)
            sre[rows, lanes] = xr
            sim[rows, lanes] = xi
            return xr[boundary_row:boundary_row + 1, :], xi[boundary_row:boundary_row + 1, :]

        cr, ci = lax.fori_loop(0, n_tiles, tile_step, (car[:, lanes], cai[:, lanes]))
        car[:, lanes] = cr
        cai[:, lanes] = ci

    for k in range(S5_CLUSTERS):
        lanes = slice(k * S5_CLUSTER_LANES, (k + 1) * S5_CLUSTER_LANES)
        yk = jnp.dot(sre[:, lanes].astype(BF16), cwr_ref[k], preferred_element_type=F32)
        yk = yk + jnp.dot(sim[:, lanes].astype(BF16), cwi_ref[k], preferred_element_type=F32)
        y_ref[:, k * S5_CLUSTER_CH:(k + 1) * S5_CLUSTER_CH] = yk
    sfr_ref[0] = car[...]
    sfi_ref[0] = cai[...]


def _s5_direction(u, row0, n_seq, seq, s0_re, s0_im, bw, cwr, cwi, tabs, reverse):
    nc = seq // SCAN_CHUNK
    b0 = row0 // SCAN_CHUNK

    def chunk(c):
        return (nc - 1 - c) if reverse else c

    state_spec = pl.BlockSpec((1, 1, S5_LANES), lambda b, c: (b, 0, 0))
    full3 = lambda shape: pl.BlockSpec(shape, lambda b, c: (0, 0, 0), pipeline_mode=pl.Buffered(1))
    return pl.pallas_call(
        functools.partial(_s5_kernel, reverse=reverse),
        grid=(n_seq, nc),
        in_specs=[pl.BlockSpec((SCAN_CHUNK, S5_WIDTH), lambda b, c: (b0 + b * nc + chunk(c), 0)),
                  state_spec, state_spec,
                  full3(bw.shape), full3(cwr.shape), full3(cwi.shape), full3(tabs.shape)],
        out_specs=[pl.BlockSpec((SCAN_CHUNK, S5_WIDTH), lambda b, c: (b * nc + chunk(c), 0)),
                   state_spec, state_spec],
        out_shape=[jax.ShapeDtypeStruct((n_seq * seq, S5_WIDTH), F32),
                   jax.ShapeDtypeStruct((n_seq, 1, S5_LANES), F32),
                   jax.ShapeDtypeStruct((n_seq, 1, S5_LANES), F32)],
        scratch_shapes=[pltpu.VMEM((SCAN_CHUNK, S5_LANES), F32), pltpu.VMEM((SCAN_CHUNK, S5_LANES), F32),
                        pltpu.VMEM((1, S5_LANES), F32), pltpu.VMEM((1, S5_LANES), F32)],
        compiler_params=_params(("parallel", "arbitrary")),
        name="s5_bwd" if reverse else "s5_fwd",
    )(u, s0_re, s0_im, bw, cwr, cwi, tabs)


def _s5_weights(a_re, a_im, log_dt, b_re, b_im, c_re, c_im):
    lam = lax.complex(a_re, a_im)
    lam_bar = jnp.exp(lam * jnp.exp(log_dt)[..., None])
    b_bar = ((lam_bar - 1.0) / lam)[..., None] * lax.complex(b_re, b_im)
    eye = jnp.eye(S5_CLUSTER_GROUPS, dtype=F32)
    out = []
    for d in range(2):
        def cluster_in(w):
            w = w.reshape(S5_CLUSTERS, S5_CLUSTER_GROUPS, S5_STATE, S5_GROUP_CH)
            return jnp.einsum('ab,kapc->kacbp', eye, w).reshape(S5_CLUSTERS, S5_CLUSTER_CH, S5_CLUSTER_LANES)

        def cluster_out(w):
            w = w.reshape(S5_CLUSTERS, S5_CLUSTER_GROUPS, S5_GROUP_CH, S5_STATE)
            return jnp.einsum('ab,kacp->kapbc', eye, w).reshape(S5_CLUSTERS, S5_CLUSTER_LANES, S5_CLUSTER_CH)

        bw = jnp.concatenate([cluster_in(jnp.real(b_bar[d])), cluster_in(jnp.imag(b_bar[d]))], axis=-1)
        cwr = cluster_out(c_re[d])
        cwi = cluster_out(-c_im[d])
        lam_d = lam_bar[d].reshape(S5_LANES)
        pows = [lam_d]
        for _ in range(SCAN_TILE - 1):
            pows.append(pows[-1] * lam_d)
        pows = jnp.stack(pows)
        j = jnp.arange(SCAN_TILE)
        tabs = []
        for k in SCAN_LEVELS:
            valid = (j <= SCAN_TILE - 1 - k) if d == 1 else (j >= k)
            p = jnp.where(valid[:, None], pows[k - 1][None, :], 0.0)
            tabs += [jnp.real(p), jnp.imag(p)]
        pc = pows[::-1] if d == 1 else pows
        tabs += [jnp.real(pc), jnp.imag(pc)]
        out.append((bw.astype(BF16), cwr.astype(BF16), cwi.astype(BF16), jnp.stack(tabs).astype(F32)))
    return out


def _final_norm_kernel(x_ref, g_ref, o_ref):
    x = x_ref[...]
    ms = jnp.mean(x * x, axis=-1, keepdims=True)
    o_ref[...] = x * lax.rsqrt(ms + RMS_EPS) * g_ref[...]


def _final_norm(x, g):
    t = x.shape[0]
    tm = ROW_TILE
    return pl.pallas_call(
        _final_norm_kernel,
        grid=(t // tm,),
        in_specs=[pl.BlockSpec((tm, D_MODEL), lambda i: (i, 0)), _resident((1, D_MODEL))],
        out_specs=pl.BlockSpec((tm, D_MODEL), lambda i: (i, 0)),
        out_shape=jax.ShapeDtypeStruct((t, D_MODEL), F32),
        compiler_params=_params(("parallel",)),
        name="final_norm",
    )(x, g.reshape(1, D_MODEL))


def _rope_tables(L):
    n_rows = L // GRID_W
    row_idx = jnp.repeat(jnp.arange(n_rows, dtype=F32), GRID_W)
    col_idx = (jnp.arange(L) % GRID_W).astype(F32)
    inv = ROPE_THETA ** (-jnp.arange(0, ROPE_AXIS_DIM, 2, dtype=F32) / ROPE_AXIS_DIM)
    ang = jnp.concatenate([row_idx[:, None] * inv, col_idx[:, None] * inv], axis=-1)
    cos, sin = jnp.cos(ang), jnp.sin(ang)
    cos_full = jnp.repeat(cos, 2, axis=-1)
    sin_signed = jnp.stack([-sin, sin], axis=-1).reshape(L, HEAD_DIM)
    return cos_full, sin_signed


def kernel(x_prompt, x_sample, cache_k_l0, cache_v_l0, state_s5_re_l1, state_s5_im_l1, c, c_ctx,
           ada_w_l0, ada_b_l0, norm1_l0, norm2_l0,
           w_in_l0, w_out_l0, q_norm_l0, k_norm_l0, hy_conv_w_l0, hy_conv_b_l0,
           hy_ffn_w1_l0, hy_ffn_b1_l0, hy_ffn_w2_l0, hy_ffn_b2_l0, hy_ffn_w3_l0, hy_sin_freq_l0, hy_bias_l0,
           router_l0, router_bias_l0, exp_w13_l0, exp_w2_l0, shared_w13_l0, shared_w2_l0,
           ada_w_l1, ada_b_l1, norm1_l1, norm2_l1,
           w_in_l1, s5_a_re_l1, s5_a_im_l1, s5_log_dt_l1, s5_b_re_l1, s5_b_im_l1,
           s5_c_re_l1, s5_c_im_l1, s5_d_l1, w_glu_l1,
           router_l1, router_bias_l1, exp_w13_l1, exp_w2_l1, shared_w13_l1, shared_w2_l1,
           final_norm):
    x = jnp.concatenate([x_prompt.reshape(T_CTX, D_MODEL), x_sample.reshape(T_LAT, D_MODEL)], axis=0)
    cond = jnp.concatenate([c_ctx[None, :], c, jnp.zeros((N_COND - 1 - DEC_BATCH, D_MODEL), F32)], axis=0)

    mods = _ada_mods(cond, ada_w_l0, ada_b_l0)
    proj = _norm_mod_matmul(x, norm1_l0, mods, 0, 1, w_in_l0.astype(BF16))

    q_c, k_c, v_c, new_k = _qk_prep(proj, 0, T_CTX, q_norm_l0, k_norm_l0, None)
    attn_c = _attention(q_c, k_c, v_c, BATCH, SEQ, None)
    q_l, k_l, v_l = _qk_prep(proj, T_CTX, T_LAT, q_norm_l0, k_norm_l0, _rope_tables(DEC_SEQ))
    ctx_kv = (cache_k_l0.reshape(DEC_BATCH * PAST_LEN, KV_WIDTH).astype(BF16),
              cache_v_l0.reshape(DEC_BATCH * PAST_LEN, KV_WIDTH).astype(BF16))
    attn_l = _attention(q_l, k_l, v_l, DEC_BATCH, DEC_SEQ, ctx_kv)
    attn = jnp.concatenate([attn_c, attn_l], axis=0)

    hy_p = (hy_conv_w_l0, hy_conv_b_l0, hy_ffn_w1_l0, hy_ffn_b1_l0, hy_ffn_w2_l0, hy_ffn_b2_l0,
            hy_ffn_w3_l0, hy_sin_freq_l0, hy_bias_l0)
    z_c = _hyena(proj[:T_CTX, QKV_WIDTH:].reshape(BATCH, SEQ, -1), *hy_p).reshape(T_CTX, HYENA_WIDTH)
    z_l = _hyena(proj[T_CTX:, QKV_WIDTH:].reshape(DEC_BATCH, DEC_SEQ, -1), *hy_p).reshape(T_LAT, HYENA_WIDTH)
    z = jnp.concatenate([z_c, z_l], axis=0)

    w_out = w_out_l0.astype(BF16)
    x = _out_proj(attn, z, x, mods, 2, w_out[:Q_WIDTH], w_out[Q_WIDTH:])
    x = _moe(x, norm2_l0, mods, 3, 4, 5, router_l0, router_bias_l0, exp_w13_l0, exp_w2_l0,
             shared_w13_l0, shared_w2_l0)

    new_v = proj[:T_CTX, Q_WIDTH + KV_WIDTH:QKV_WIDTH]

    mods = _ada_mods(cond, ada_w_l1, ada_b_l1)
    u = _norm_mod_matmul(x, norm1_l1, mods, 0, 1, w_in_l1.astype(BF16))
    s5_w = _s5_weights(s5_a_re_l1, s5_a_im_l1, s5_log_dt_l1, s5_b_re_l1, s5_b_im_l1, s5_c_re_l1, s5_c_im_l1)
    zero_state = jnp.zeros((BATCH, 1, S5_LANES), F32)
    ys, finals = [], []
    for d in range(2):
        y_c, f_re, f_im = _s5_direction(u, 0, BATCH, SEQ, zero_state, zero_state, *s5_w[d], reverse=d == 1)
        y_l, _, _ = _s5_direction(u, T_CTX, DEC_BATCH, DEC_SEQ,
                                  state_s5_re_l1[:, d].reshape(DEC_BATCH, 1, S5_LANES),
                                  state_s5_im_l1[:, d].reshape(DEC_BATCH, 1, S5_LANES), *s5_w[d], reverse=d == 1)
        ys.append(jnp.concatenate([y_c, y_l], axis=0))
        finals.append((f_re.reshape(BATCH, S5_GROUPS, S5_STATE), f_im.reshape(BATCH, S5_GROUPS, S5_STATE)))
    s_re = jnp.stack([finals[0][0], finals[1][0]], axis=1)
    s_im = jnp.stack([finals[0][1], finals[1][1]], axis=1)
    x = _glu_proj(ys[0], ys[1], u, s5_d_l1, x, mods, 2, w_glu_l1.astype(BF16))
    x = _moe(x, norm2_l1, mods, 3, 4, 5, router_l1, router_bias_l1, exp_w13_l1, exp_w2_l1,
             shared_w13_l1, shared_w2_l1)

    y_all = _final_norm(x, final_norm)
    return (y_all[:T_CTX].reshape(BATCH, SEQ, D_MODEL),
            y_all[T_CTX:].reshape(DEC_BATCH, DEC_SEQ, D_MODEL),
            new_k.reshape(BATCH, SEQ, N_KV_HEADS, HEAD_DIM),
            new_v.reshape(BATCH, SEQ, N_KV_HEADS, HEAD_DIM),
            s_re, s_im)
```

```python
import functools
import math

import jax
import jax.numpy as jnp
from jax import lax
from jax.experimental import pallas as pl
from jax.experimental.pallas import tpu as pltpu

F32 = jnp.float32
BF16 = jnp.bfloat16

D_MODEL = 2048
BATCH = 32
SEQ = 256
DEC_BATCH = 4
DEC_SEQ = 4096
PAST_LEN = 256
GRID_W = 64
RMS_EPS = 1e-6
HEAD_DIM = 128
N_HEADS = 8
N_KV_HEADS = 2
GQA_GROUP = N_HEADS // N_KV_HEADS
Q_WIDTH = N_HEADS * HEAD_DIM
KV_WIDTH = N_KV_HEADS * HEAD_DIM
QKV_WIDTH = Q_WIDTH + 2 * KV_WIDTH
ROPE_THETA = 10000.0
ROPE_AXIS_DIM = HEAD_DIM // 2
HYENA_WIDTH = D_MODEL // 2
HYENA_ORDER = 2
FILTER_BANDS = 16
DECAY_TARGET = 1e-2
DECAY_PCT_SHORT = 0.3
DECAY_PCT_LONG = 1.5
S5_WIDTH = D_MODEL // 2
S5_GROUP_CH = 16
S5_GROUPS = S5_WIDTH // S5_GROUP_CH
S5_STATE = 64
N_EXPERTS = 64
TOP_K = 8
N_EXPERT_GROUPS = 8
TOPK_GROUPS = 4
EXPERT_HIDDEN = 512
ROUTED_SCALE = 2.5
MOE_BLOCK = 128

T_CTX = BATCH * SEQ
T_LAT = DEC_BATCH * DEC_SEQ
T_ALL = T_CTX + T_LAT
N_COND = 8

ROW_TILE = 256
V7X_VMEM_LIMIT_BYTES = 56 * 1024 * 1024


def _params(semantics):
    return pltpu.CompilerParams(dimension_semantics=semantics, vmem_limit_bytes=V7X_VMEM_LIMIT_BYTES)


def _cond_row(i, tm):
    tok = i * tm
    return jnp.where(tok < T_CTX, 0, 1 + (tok - T_CTX) // DEC_SEQ)


def _mod_spec(k, tm):
    return pl.BlockSpec((1, 1, 1, D_MODEL), lambda i: (_cond_row(i, tm), k, 0, 0))


def _resident(shape):
    nd = len(shape)
    return pl.BlockSpec(shape, lambda i: (0,) * nd, pipeline_mode=pl.Buffered(1))


def _silu(x):
    return x * jax.nn.sigmoid(x)


def _norm_mod(x, g, sc, sh):
    ms = jnp.mean(x * x, axis=-1, keepdims=True)
    return (x * lax.rsqrt(ms + RMS_EPS) * g) * (1.0 + sc) + sh


def _ada_kernel(c_ref, w_ref, b_ref, o_ref):
    a = _silu(c_ref[...]).astype(BF16)
    o_ref[...] = jnp.dot(a, w_ref[...].astype(BF16), preferred_element_type=F32) + b_ref[...]


def _ada_mods(cond, w, b):
    n = w.shape[1]
    tn = 1024
    out = pl.pallas_call(
        _ada_kernel,
        grid=(n // tn,),
        in_specs=[pl.BlockSpec((N_COND, D_MODEL), lambda j: (0, 0)),
                  pl.BlockSpec((D_MODEL, tn), lambda j: (0, j)),
                  pl.BlockSpec((1, tn), lambda j: (0, j))],
        out_specs=pl.BlockSpec((N_COND, tn), lambda j: (0, j)),
        out_shape=jax.ShapeDtypeStruct((N_COND, n), F32),
        compiler_params=_params(("parallel",)),
        name="ada_mods",
    )(cond, w, b.reshape(1, n))
    return out.reshape(N_COND, 6, 1, D_MODEL)


def _nmm_kernel(x_ref, g_ref, sc_ref, sh_ref, w_ref, o_ref):
    h = _norm_mod(x_ref[...], g_ref[...], sc_ref[0, 0], sh_ref[0, 0])
    o_ref[...] = jnp.dot(h.astype(BF16), w_ref[...], preferred_element_type=F32)


def _norm_mod_matmul(x, g, mods4, k_shift, k_scale, w_bf16):
    t, n = x.shape[0], w_bf16.shape[1]
    tm = ROW_TILE
    return pl.pallas_call(
        _nmm_kernel,
        grid=(t // tm,),
        in_specs=[pl.BlockSpec((tm, D_MODEL), lambda i: (i, 0)),
                  _resident((1, D_MODEL)),
                  _mod_spec(k_scale, tm), _mod_spec(k_shift, tm),
                  _resident((D_MODEL, n))],
        out_specs=pl.BlockSpec((tm, n), lambda i: (i, 0)),
        out_shape=jax.ShapeDtypeStruct((t, n), F32),
        compiler_params=_params(("parallel",)),
        name="norm_mod_matmul",
    )(x, g.reshape(1, D_MODEL), mods4, mods4, w_bf16)


def _head_norm(xh, g):
    ms = jnp.mean(xh * xh, axis=-1, keepdims=True)
    return xh * lax.rsqrt(ms + RMS_EPS) * g


def _rope(y, cos, sin_signed):
    lane = lax.broadcasted_iota(jnp.int32, y.shape, 1)
    partner = jnp.where(lane % 2 == 0, pltpu.roll(y, HEAD_DIM - 1, 1), pltpu.roll(y, 1, 1))
    return y * cos + partner * sin_signed


def _qkprep_ctx_kernel(qkv_ref, qg_ref, kg_ref, q_out, kb_out, vb_out, kf_out):
    for h in range(N_HEADS):
        sl = slice(h * HEAD_DIM, (h + 1) * HEAD_DIM)
        q_out[:, sl] = _head_norm(qkv_ref[:, sl], qg_ref[...]).astype(BF16)
    for j in range(N_KV_HEADS):
        k = _head_norm(qkv_ref[:, Q_WIDTH + j * HEAD_DIM:Q_WIDTH + (j + 1) * HEAD_DIM], kg_ref[...])
        kf_out[:, j * HEAD_DIM:(j + 1) * HEAD_DIM] = k
        kb_out[:, j * HEAD_DIM:(j + 1) * HEAD_DIM] = k.astype(BF16)
    vb_out[...] = qkv_ref[:, Q_WIDTH + KV_WIDTH:QKV_WIDTH].astype(BF16)


def _qkprep_lat_kernel(qkv_ref, qg_ref, kg_ref, cos_ref, sin_ref, q_out, kb_out, vb_out):
    cos, sin = cos_ref[...], sin_ref[...]
    for h in range(N_HEADS):
        sl = slice(h * HEAD_DIM, (h + 1) * HEAD_DIM)
        q_out[:, sl] = _rope(_head_norm(qkv_ref[:, sl], qg_ref[...]), cos, sin).astype(BF16)
    for j in range(N_KV_HEADS):
        k = _head_norm(qkv_ref[:, Q_WIDTH + j * HEAD_DIM:Q_WIDTH + (j + 1) * HEAD_DIM], kg_ref[...])
        kb_out[:, j * HEAD_DIM:(j + 1) * HEAD_DIM] = _rope(k, cos, sin).astype(BF16)
    vb_out[...] = qkv_ref[:, Q_WIDTH + KV_WIDTH:QKV_WIDTH].astype(BF16)


def _qk_prep(proj, row0, t, q_norm, k_norm, rope_tabs):
    tm = ROW_TILE
    b0 = row0 // tm
    qkv_spec = pl.BlockSpec((tm, QKV_WIDTH), lambda i: (b0 + i, 0))
    gain = _resident((1, HEAD_DIM))
    outs = [pl.BlockSpec((tm, Q_WIDTH), lambda i: (i, 0)),
            pl.BlockSpec((tm, KV_WIDTH), lambda i: (i, 0)),
            pl.BlockSpec((tm, KV_WIDTH), lambda i: (i, 0))]
    shapes = [jax.ShapeDtypeStruct((t, Q_WIDTH), BF16),
              jax.ShapeDtypeStruct((t, KV_WIDTH), BF16),
              jax.ShapeDtypeStruct((t, KV_WIDTH), BF16)]
    qg, kg = q_norm.reshape(1, HEAD_DIM), k_norm.reshape(1, HEAD_DIM)
    if rope_tabs is None:
        return pl.pallas_call(
            _qkprep_ctx_kernel, grid=(t // tm,),
            in_specs=[qkv_spec, gain, gain],
            out_specs=outs + [pl.BlockSpec((tm, KV_WIDTH), lambda i: (i, 0))],
            out_shape=shapes + [jax.ShapeDtypeStruct((t, KV_WIDTH), F32)],
            compiler_params=_params(("parallel",)), name="qk_prep_ctx",
        )(proj, qg, kg)
    cos, sin = rope_tabs
    nq = DEC_SEQ // tm
    tab = pl.BlockSpec((tm, HEAD_DIM), lambda i: (i % nq, 0))
    return pl.pallas_call(
        _qkprep_lat_kernel, grid=(t // tm,),
        in_specs=[qkv_spec, gain, gain, tab, tab],
        out_specs=outs, out_shape=shapes,
        compiler_params=_params(("parallel",)), name="qk_prep_lat",
    )(proj, qg, kg, cos, sin)


def _attn_kernel(*refs, with_ctx):
    if with_ctx:
        q_ref, k_ref, v_ref, ck_ref, cv_ref, o_ref = refs
    else:
        q_ref, k_ref, v_ref, o_ref = refs
    scale = HEAD_DIM ** -0.5
    nt = (((1,), (1,)), ((), ()))
    q = q_ref[...]
    s = lax.dot_general(q, k_ref[...], nt, preferred_element_type=F32) * scale
    m = jnp.max(s, axis=-1, keepdims=True)
    if with_ctx:
        sc = lax.dot_general(q, ck_ref[...], nt, preferred_element_type=F32) * scale
        m = jnp.maximum(m, jnp.max(sc, axis=-1, keepdims=True))
    p = jnp.exp(s - m)
    l = jnp.sum(p, axis=-1, keepdims=True)
    o = jnp.dot(p.astype(BF16), v_ref[...], preferred_element_type=F32)
    if with_ctx:
        pc = jnp.exp(sc - m)
        l = l + jnp.sum(pc, axis=-1, keepdims=True)
        o = o + jnp.dot(pc.astype(BF16), cv_ref[...], preferred_element_type=F32)
    o_ref[...] = (o / l).astype(BF16)


def _attention(q, k, v, n_batch, seq, ctx_kv):
    tq = ROW_TILE
    nq = seq // tq
    t = n_batch * seq
    q_spec = pl.BlockSpec((tq, HEAD_DIM), lambda b, h, i: (b * nq + i, h))
    kv_spec = pl.BlockSpec((seq, HEAD_DIM), lambda b, h, i: (b, h // GQA_GROUP))
    in_specs = [q_spec, kv_spec, kv_spec]
    args = [q, k, v]
    if ctx_kv is not None:
        c_spec = pl.BlockSpec((PAST_LEN, HEAD_DIM), lambda b, h, i: (b, h // GQA_GROUP))
        in_specs += [c_spec, c_spec]
        args += list(ctx_kv)
    return pl.pallas_call(
        functools.partial(_attn_kernel, with_ctx=ctx_kv is not None),
        grid=(n_batch, N_HEADS, nq),
        in_specs=in_specs,
        out_specs=pl.BlockSpec((tq, HEAD_DIM), lambda b, h, i: (b * nq + i, h)),
        out_shape=jax.ShapeDtypeStruct((t, Q_WIDTH), BF16),
        compiler_params=_params(("parallel", "parallel", "parallel")),
        name="attention_lat" if ctx_kv is not None else "attention_ctx",
    )(*args)


def _outproj_kernel(a_ref, z_ref, x_ref, gate_ref, wa_ref, wz_ref, o_ref):
    acc = jnp.dot(a_ref[...], wa_ref[...], preferred_element_type=F32)
    acc = acc + jnp.dot(z_ref[...].astype(BF16), wz_ref[...], preferred_element_type=F32)
    o_ref[...] = x_ref[...] + gate_ref[0, 0] * acc


def _out_proj(attn, z, x, mods4, k_gate, wa_bf16, wz_bf16):
    t = x.shape[0]
    tm = ROW_TILE
    return pl.pallas_call(
        _outproj_kernel,
        grid=(t // tm,),
        in_specs=[pl.BlockSpec((tm, Q_WIDTH), lambda i: (i, 0)),
                  pl.BlockSpec((tm, HYENA_WIDTH), lambda i: (i, 0)),
                  pl.BlockSpec((tm, D_MODEL), lambda i: (i, 0)),
                  _mod_spec(k_gate, tm),
                  _resident((Q_WIDTH, D_MODEL)), _resident((HYENA_WIDTH, D_MODEL))],
        out_specs=pl.BlockSpec((tm, D_MODEL), lambda i: (i, 0)),
        out_shape=jax.ShapeDtypeStruct((t, D_MODEL), F32),
        compiler_params=_params(("parallel",)),
        name="out_proj",
    )(attn, z, x, mods4, wa_bf16, wz_bf16)


def _gelu_tanh(x):
    return 0.5 * x * (1.0 + jnp.tanh(math.sqrt(2.0 / math.pi) * (x + 0.044715 * (x * x * x))))


def _glu_kernel(yf_ref, yb_ref, u_ref, d_ref, x_ref, gate_ref, w_ref, o_ref):
    y = (yf_ref[...] + yb_ref[...]) + d_ref[...] * u_ref[...]
    ag = jnp.dot(_gelu_tanh(y).astype(BF16), w_ref[...], preferred_element_type=F32)
    a, g = ag[:, :D_MODEL], ag[:, D_MODEL:]
    o_ref[...] = x_ref[...] + gate_ref[0, 0] * (a * jax.nn.sigmoid(g))


def _glu_proj(y_fwd, y_bwd, u, d_skip, x, mods4, k_gate, w_bf16):
    t = x.shape[0]
    tm = ROW_TILE
    s5_rows = pl.BlockSpec((tm, S5_WIDTH), lambda i: (i, 0))
    return pl.pallas_call(
        _glu_kernel,
        grid=(t // tm,),
        in_specs=[s5_rows, s5_rows, s5_rows, _resident((1, S5_WIDTH)),
                  pl.BlockSpec((tm, D_MODEL), lambda i: (i, 0)),
                  _mod_spec(k_gate, tm),
                  _resident((S5_WIDTH, 2 * D_MODEL))],
        out_specs=pl.BlockSpec((tm, D_MODEL), lambda i: (i, 0)),
        out_shape=jax.ShapeDtypeStruct((t, D_MODEL), F32),
        compiler_params=_params(("parallel",)),
        name="glu_proj",
    )(y_fwd, y_bwd, u, d_skip.reshape(1, S5_WIDTH), x, mods4, w_bf16)


def _gmm_kernel(be_ref, xs_ref, w13_ref, w2_ref, o_ref, w13_b, w2_b):
    i = pl.program_id(0)
    fresh = jnp.logical_or(i == 0, be_ref[i] != be_ref[jnp.maximum(i - 1, 0)])

    @pl.when(fresh)
    def _():
        w13_b[...] = w13_ref[0].astype(BF16)
        w2_b[...] = w2_ref[0].astype(BF16)

    gu = jnp.dot(xs_ref[...], w13_b[...], preferred_element_type=F32)
    hmid = _silu(gu[:, :EXPERT_HIDDEN]) * gu[:, EXPERT_HIDDEN:]
    o_ref[...] = jnp.dot(hmid.astype(BF16), w2_b[...], preferred_element_type=F32)


def _grouped_experts(blk_e, xs, w13, w2):
    n_rows = xs.shape[0]
    n_blocks = n_rows // MOE_BLOCK
    grid_spec = pltpu.PrefetchScalarGridSpec(
        num_scalar_prefetch=1,
        grid=(n_blocks,),
        in_specs=[pl.BlockSpec((MOE_BLOCK, D_MODEL), lambda i, be: (i, 0)),
                  pl.BlockSpec((1, D_MODEL, 2 * EXPERT_HIDDEN), lambda i, be: (be[i], 0, 0)),
                  pl.BlockSpec((1, EXPERT_HIDDEN, D_MODEL), lambda i, be: (be[i], 0, 0))],
        out_specs=pl.BlockSpec((MOE_BLOCK, D_MODEL), lambda i, be: (i, 0)),
        scratch_shapes=[pltpu.VMEM((D_MODEL, 2 * EXPERT_HIDDEN), BF16),
                        pltpu.VMEM((EXPERT_HIDDEN, D_MODEL), BF16)],
    )
    return pl.pallas_call(
        _gmm_kernel,
        grid_spec=grid_spec,
        out_shape=jax.ShapeDtypeStruct((n_rows, D_MODEL), F32),
        compiler_params=_params(("arbitrary",)),
        name="grouped_experts",
    )(blk_e, xs, w13, w2)


COMBINE_TILE = 128


def _shared_kernel(h_ref, x_ref, ye_ref, w_ref, gate_ref, w13_ref, w2_ref, o_ref):
    gu = jnp.dot(h_ref[...], w13_ref[...], preferred_element_type=F32)
    hmid = _silu(gu[:, :EXPERT_HIDDEN]) * gu[:, EXPERT_HIDDEN:]
    acc = jnp.dot(hmid.astype(BF16), w2_ref[...], preferred_element_type=F32)
    wts = w_ref[...]
    for k in range(TOP_K):
        acc = acc + wts[:, k:k + 1] * ye_ref[k]
    o_ref[...] = x_ref[...] + gate_ref[0, 0] * acc


def _shared_and_combine(h_bf16, x, y_exp, wts, mods4, k_gate, w13_bf16, w2_bf16):
    t = x.shape[0]
    tm = COMBINE_TILE
    row = lambda i: (i, 0)
    return pl.pallas_call(
        _shared_kernel,
        grid=(t // tm,),
        in_specs=[pl.BlockSpec((tm, D_MODEL), row), pl.BlockSpec((tm, D_MODEL), row),
                  pl.BlockSpec((TOP_K, tm, D_MODEL), lambda i: (0, i, 0)),
                  pl.BlockSpec((tm, TOP_K), row), _mod_spec(k_gate, tm),
                  _resident((D_MODEL, 2 * EXPERT_HIDDEN)), _resident((EXPERT_HIDDEN, D_MODEL))],
        out_specs=pl.BlockSpec((tm, D_MODEL), row),
        out_shape=jax.ShapeDtypeStruct((t, D_MODEL), F32),
        compiler_params=_params(("parallel",)),
        name="shared_expert_combine",
    )(h_bf16, x, y_exp, wts, mods4, w13_bf16, w2_bf16)


ROUTE_TILE = 256
GROUP_SIZE = N_EXPERTS // N_EXPERT_GROUPS
NEG_INF = float("-inf")


def _first_argmax(v, index, sentinel):
    m = jnp.max(v, axis=0, keepdims=True)
    first = jnp.min(jnp.where(v == m, index, sentinel), axis=0, keepdims=True)
    return m, first


def _router_kernel(x_ref, g_ref, sc_ref, sh_ref, wt_ref, rb_ref, tri_ref,
                   h_out, idx_out, gate_out, rank_out, cnt_out, base_ref):
    tm = x_ref.shape[0]

    @pl.when(pl.program_id(0) == 0)
    def _():
        base_ref[...] = jnp.zeros_like(base_ref)

    hb = _norm_mod(x_ref[...], g_ref[...], sc_ref[0, 0], sh_ref[0, 0]).astype(BF16)
    h_out[...] = hb
    logits = lax.dot_general(wt_ref[...], hb, (((1,), (1,)), ((), ())), preferred_element_type=F32)
    scores = jax.nn.sigmoid(logits)
    choice = scores + rb_ref[...]

    member = lax.broadcasted_iota(jnp.int32, (GROUP_SIZE, tm), 0).astype(F32)
    group_scores = []
    for g in range(N_EXPERT_GROUPS):
        cg = choice[g * GROUP_SIZE:(g + 1) * GROUP_SIZE, :]
        m1, first = _first_argmax(cg, member, float(GROUP_SIZE))
        m2 = jnp.max(jnp.where(member == first, NEG_INF, cg), axis=0, keepdims=True)
        group_scores.append(m1 + m2)
    gs = jnp.concatenate(group_scores, axis=0)

    group = lax.broadcasted_iota(jnp.int32, (N_EXPERT_GROUPS, tm), 0).astype(F32)
    keep = jnp.zeros((N_EXPERT_GROUPS, tm), F32)
    for _ in range(TOPK_GROUPS):
        _, first = _first_argmax(gs, group, float(N_EXPERT_GROUPS))
        sel = group == first
        keep = jnp.where(sel, 1.0, keep)
        gs = jnp.where(sel, NEG_INF, gs)
    masked = jnp.concatenate(
        [jnp.where(keep[g:g + 1, :] > 0.0, choice[g * GROUP_SIZE:(g + 1) * GROUP_SIZE, :], NEG_INF)
         for g in range(N_EXPERT_GROUPS)], axis=0)

    expert = lax.broadcasted_iota(jnp.int32, (N_EXPERTS, tm), 0).astype(F32)
    onehot = jnp.zeros((N_EXPERTS, tm), F32)
    sels, idx_rows, gate_rows = [], [], []
    for _ in range(TOP_K):
        _, first = _first_argmax(masked, expert, float(N_EXPERTS))
        sel = expert == first
        sels.append(sel)
        idx_rows.append(first)
        gate_rows.append(jnp.sum(jnp.where(sel, scores, 0.0), axis=0, keepdims=True))
        masked = jnp.where(sel, NEG_INF, masked)
        onehot = jnp.where(sel, 1.0, onehot)
    gates = jnp.concatenate(gate_rows, axis=0)
    gate_out[...] = gates / jnp.sum(gates, axis=0, keepdims=True) * ROUTED_SCALE
    idx_out[...] = jnp.concatenate(idx_rows, axis=0).astype(jnp.int32)

    cum = jnp.dot(onehot.astype(BF16), tri_ref[...], preferred_element_type=F32) + base_ref[:, 0:1]
    rank_rows = [jnp.sum(jnp.where(sel, cum, 0.0), axis=0, keepdims=True) - 1.0 for sel in sels]
    rank_out[...] = jnp.concatenate(rank_rows, axis=0).astype(jnp.int32)
    base_ref[...] = jnp.broadcast_to(cum[:, tm - 1:tm], base_ref.shape)
    cnt_out[...] = base_ref[...].astype(jnp.int32)


def _router(x, norm_g, mods4, k_shift, k_scale, router_w, router_bias):
    t = x.shape[0]
    tm = ROUTE_TILE
    tri = (jnp.arange(tm)[:, None] <= jnp.arange(tm)[None, :]).astype(BF16)
    choice_rows = pl.BlockSpec((TOP_K, tm), lambda i: (0, i))
    return pl.pallas_call(
        _router_kernel,
        grid=(t // tm,),
        in_specs=[pl.BlockSpec((tm, D_MODEL), lambda i: (i, 0)), _resident((1, D_MODEL)),
                  _mod_spec(k_scale, tm), _mod_spec(k_shift, tm),
                  _resident((N_EXPERTS, D_MODEL)), _resident((N_EXPERTS, 1)), _resident((tm, tm))],
        out_specs=[pl.BlockSpec((tm, D_MODEL), lambda i: (i, 0)), choice_rows, choice_rows, choice_rows,
                   pl.BlockSpec((N_EXPERTS, 128), lambda i: (0, 0))],
        out_shape=[jax.ShapeDtypeStruct((t, D_MODEL), BF16),
                   jax.ShapeDtypeStruct((TOP_K, t), jnp.int32), jax.ShapeDtypeStruct((TOP_K, t), F32),
                   jax.ShapeDtypeStruct((TOP_K, t), jnp.int32), jax.ShapeDtypeStruct((N_EXPERTS, 128), jnp.int32)],
        scratch_shapes=[pltpu.VMEM((N_EXPERTS, 128), F32)],
        compiler_params=_params(("arbitrary",)),
        name="router",
    )(x, norm_g.reshape(1, D_MODEL), mods4, mods4, router_w.T.astype(BF16), router_bias.reshape(N_EXPERTS, 1), tri)


def _pos_kernel(idx_ref, rank_ref, off_ref, pos_out):
    tm = idx_ref.shape[1]
    expert = lax.broadcasted_iota(jnp.int32, (N_EXPERTS, tm), 0)
    rows = []
    for k in range(TOP_K):
        sel = expert == idx_ref[k:k + 1, :]
        rows.append(jnp.sum(jnp.where(sel, off_ref[...], 0.0), axis=0, keepdims=True))
    pos_out[...] = jnp.concatenate(rows, axis=0).astype(jnp.int32) + rank_ref[...]


def _positions(idx, rank, offsets):
    t = idx.shape[1]
    tm = 1024
    rows = pl.BlockSpec((TOP_K, tm), lambda i: (0, i))
    return pl.pallas_call(
        _pos_kernel, grid=(t // tm,),
        in_specs=[rows, rows, _resident((N_EXPERTS, 1))],
        out_specs=rows, out_shape=jax.ShapeDtypeStruct((TOP_K, t), jnp.int32),
        compiler_params=_params(("parallel",)), name="expert_positions",
    )(idx, rank, offsets.astype(F32).reshape(N_EXPERTS, 1))


def _moe(x, norm_g, mods4, k_shift, k_scale, k_gate, router_w, router_bias, exp_w13, exp_w2, sh_w13, sh_w2):
    t = x.shape[0]
    h_bf16, top_idx, gate, rank, counts = _router(x, norm_g, mods4, k_shift, k_scale, router_w, router_bias)
    counts = counts[:, 0]
    padded = (counts + MOE_BLOCK - 1) // MOE_BLOCK * MOE_BLOCK
    pend = jnp.cumsum(padded)
    pos = _positions(top_idx, rank, pend - padded)
    n_rows = -(-(t * TOP_K + N_EXPERTS * (MOE_BLOCK - 1)) // MOE_BLOCK) * MOE_BLOCK
    n_blocks = n_rows // MOE_BLOCK
    tok = jnp.broadcast_to(jnp.arange(t, dtype=jnp.int32)[None, :], (TOP_K, t))
    row_tok = jnp.zeros((n_rows,), jnp.int32).at[pos.reshape(-1)].set(tok.reshape(-1), unique_indices=True)
    blk_e = jnp.minimum(jnp.searchsorted(pend, jnp.arange(n_blocks) * MOE_BLOCK, side='right'),
                        N_EXPERTS - 1).astype(jnp.int32)
    xs = h_bf16[row_tok]
    ys = _grouped_experts(blk_e, xs, exp_w13, exp_w2)
    y_exp = ys[pos.reshape(-1)].reshape(TOP_K, t, D_MODEL)
    return _shared_and_combine(h_bf16, x, y_exp, gate.T, mods4, k_gate, sh_w13.astype(BF16), sh_w2.astype(BF16))


HY_BLOCK_CTX, HY_TC_CTX = 256, 512
HY_BLOCK_LAT, HY_TC_LAT = 512, 128


def _hyena_filters(L, w1, b1, w2, b2, w3, sin_freq):
    t_norm = jnp.linspace(0.0, 1.0, L, dtype=F32)[:, None]
    omega = 2.0 * math.pi * jnp.arange(L, dtype=F32)[:, None] / L
    bands = jnp.linspace(1e-4, FILTER_BANDS - 1, FILTER_BANDS, dtype=F32)[None, :]
    z = jnp.concatenate([t_norm, jnp.cos(bands * omega), -jnp.sin(bands * omega)], axis=-1)
    h = jnp.sin(sin_freq * (z @ w1 + b1))
    h = jnp.sin(sin_freq * (h @ w2 + b2))
    h = (h @ w3).reshape(L, 2, HYENA_ORDER, HYENA_WIDTH)
    deltas = jnp.abs(jnp.linspace(math.log(DECAY_TARGET) / DECAY_PCT_LONG,
                                  math.log(DECAY_TARGET) / DECAY_PCT_SHORT, HYENA_WIDTH, dtype=F32))
    h = h * jnp.exp(-t_norm * deltas)[:, None, None, :]
    fwd, bwd = h[:, 0], h[:, 1]
    filt = jnp.concatenate([fwd, jnp.zeros_like(fwd[:1]), bwd[:0:-1]], axis=0)
    return filt / jnp.sum(jnp.abs(filt), axis=0, keepdims=True)


HY_ACC_ELEMS = 8192


def _dft_matrices(bk):
    k = jnp.arange(bk, dtype=jnp.int32)
    phase = ((2 * k[:, None] + 1) * k[None, :]) % (4 * bk)
    ang = phase.astype(F32) * (math.pi / (2 * bk))
    return jnp.cos(ang), jnp.sin(ang)


def _filter_spectra(filt, bk):
    L = filt.shape[0] // 2
    nb = L // bk
    g = jnp.concatenate([filt[L:], filt[:L]], axis=0)
    cos, sin = _dft_matrices(bk)
    causal, anti = [], []
    for d in range(-(nb - 1), nb):
        base = L + bk * d
        causal.append(g[base:base + bk])
        anti.append(jnp.concatenate([jnp.zeros_like(g[:1]), g[base - bk + 1:base][::-1]], axis=0))
    causal, anti = jnp.stack(causal), jnp.stack(anti)
    hi = lax.Precision.HIGHEST
    re = jnp.einsum('kj,djow->odkw', cos, causal + anti, precision=hi)
    im = -jnp.einsum('kj,djow->odkw', sin, causal - anti, precision=hi)
    return jnp.stack([re, im], axis=2)


def _hyena_kernel(v_ref, x1_ref, x2_ref, cw_ref, cb_ref, hb_ref, k0_ref, k1_ref, c_ref, s_ref, ct_ref, st_ref,
                  o_ref, vs_ref, z1_ref, x2s_ref, ur_ref, ui_ref, yr_ref, yi_ref, *, seq, bk, tc):
    nb = seq // bk
    ft = HY_ACC_ELEMS // tc
    row = lax.broadcasted_iota(jnp.int32, (seq, tc), 0)

    def short_conv(x, j):
        prev = jnp.where(row == 0, 0.0, pltpu.roll(x, 1, 0))
        nxt = jnp.where(row == seq - 1, 0.0, pltpu.roll(x, seq - 1, 0))
        return (prev * cw_ref[0, j:j + 1, :] + x * cw_ref[1, j:j + 1, :] + nxt * cw_ref[2, j:j + 1, :]
                + cb_ref[j:j + 1, :])

    vs_ref[...] = short_conv(v_ref[...], 0)
    z1_ref[...] = short_conv(x1_ref[...], 1)
    x2s_ref[...] = short_conv(x2_ref[...], 2)

    def long_conv(src_ref, k_ref, emit):
        for blk in range(nb):
            ub = src_ref[blk * bk:(blk + 1) * bk, :].astype(BF16)
            ur_ref[blk] = jnp.dot(c_ref[...], ub, preferred_element_type=F32)
            ui_ref[blk] = -jnp.dot(s_ref[...], ub, preferred_element_type=F32)
        for out_blk in range(nb):
            def acc_tile(f, carry, out_blk=out_blk):
                rows = pl.ds(pl.multiple_of(f * ft, ft), ft)
                ar = jnp.zeros((ft, tc), F32)
                ai = jnp.zeros((ft, tc), F32)
                for in_blk in range(nb):
                    d = out_blk - in_blk + nb - 1
                    kr, ki = k_ref[d, 0, rows, :], k_ref[d, 1, rows, :]
                    xr, xi = ur_ref[in_blk, rows, :], ui_ref[in_blk, rows, :]
                    ar = ar + (kr * xr - ki * xi)
                    ai = ai + (kr * xi + ki * xr)
                yr_ref[rows, :] = ar
                yi_ref[rows, :] = ai
                return carry

            lax.fori_loop(0, bk // ft, acc_tile, 0)
            y = jnp.dot(ct_ref[...], yr_ref[...].astype(BF16), preferred_element_type=F32)
            y = y - jnp.dot(st_ref[...], yi_ref[...].astype(BF16), preferred_element_type=F32)
            emit(slice(out_blk * bk, (out_blk + 1) * bk), y * (1.0 / bk))

    def emit_z1(rows, y):
        z1_ref[rows, :] = z1_ref[rows, :] * (y + hb_ref[0:1, :] * vs_ref[rows, :])

    def emit_out(rows, y):
        o_ref[rows, :] = (x2s_ref[rows, :] * (y + hb_ref[1:2, :] * z1_ref[rows, :])).astype(BF16)

    long_conv(vs_ref, k0_ref, emit_z1)
    long_conv(z1_ref, k1_ref, emit_out)


def _hyena(proj, row0, n_batch, seq, bk, tc, conv_w, conv_b, hy_bias, spectra, mats):
    nb = seq // bk
    n_ct = HYENA_WIDTH // tc
    col0 = QKV_WIDTH // tc
    per = HYENA_WIDTH // tc
    b0 = row0 // seq

    def data(which):
        return pl.BlockSpec((seq, tc), lambda ct, b: (b0 + b, col0 + which * per + ct))

    const2 = lambda shape: pl.BlockSpec(shape, lambda ct, b: (0, 0), pipeline_mode=pl.Buffered(1))
    kspec = pl.BlockSpec((2 * nb - 1, 2, bk, tc), lambda ct, b: (0, 0, 0, ct), pipeline_mode=pl.Buffered(1))
    cw = conv_w.reshape(3, HYENA_ORDER + 1, HYENA_WIDTH)
    cb = conv_b.reshape(HYENA_ORDER + 1, HYENA_WIDTH)
    return pl.pallas_call(
        functools.partial(_hyena_kernel, seq=seq, bk=bk, tc=tc),
        grid=(n_ct, n_batch),
        in_specs=[data(0), data(1), data(2),
                  pl.BlockSpec((3, HYENA_ORDER + 1, tc), lambda ct, b: (0, 0, ct)),
                  pl.BlockSpec((HYENA_ORDER + 1, tc), lambda ct, b: (0, ct)),
                  pl.BlockSpec((HYENA_ORDER, tc), lambda ct, b: (0, ct)),
                  kspec, kspec, const2((bk, bk)), const2((bk, bk)), const2((bk, bk)), const2((bk, bk))],
        out_specs=pl.BlockSpec((seq, tc), lambda ct, b: (b, ct)),
        out_shape=jax.ShapeDtypeStruct((n_batch * seq, HYENA_WIDTH), BF16),
        scratch_shapes=[pltpu.VMEM((seq, tc), F32), pltpu.VMEM((seq, tc), F32), pltpu.VMEM((seq, tc), F32),
                        pltpu.VMEM((nb, bk, tc), F32), pltpu.VMEM((nb, bk, tc), F32),
                        pltpu.VMEM((bk, tc), F32), pltpu.VMEM((bk, tc), F32)],
        compiler_params=_params(("parallel", "parallel")),
        name="hyena_seq%d" % seq,
    )(proj, proj, proj, cw, cb, hy_bias, spectra[0], spectra[1], *mats)


def _hyena_operands(L, bk, f_w1, f_b1, f_w2, f_b2, f_w3, sin_freq):
    filt = _hyena_filters(L, f_w1, f_b1, f_w2, f_b2, f_w3, sin_freq)
    cos, sin = _dft_matrices(bk)
    mats = tuple(m.astype(BF16) for m in (cos, sin, cos.T, sin.T))
    return _filter_spectra(filt, bk), mats


S5_LANES = S5_GROUPS * S5_STATE
S5_CLUSTER_GROUPS = 16
S5_CLUSTERS = S5_GROUPS // S5_CLUSTER_GROUPS
S5_CLUSTER_CH = S5_CLUSTER_GROUPS * S5_GROUP_CH
S5_CLUSTER_LANES = S5_CLUSTER_GROUPS * S5_STATE
SCAN_CHUNK = 256
SCAN_TILE = 8
SCAN_LEVELS = (1, 2, 4)


def _s5_kernel(u_ref, s0r_ref, s0i_ref, bw_ref, cwr_ref, cwi_ref, tab_ref, y_ref, sfr_ref, sfi_ref,
               sre, sim, car, cai, *, reverse):
    @pl.when(pl.program_id(1) == 0)
    def _():
        car[...] = s0r_ref[0]
        cai[...] = s0i_ref[0]

    ub = u_ref[...].astype(BF16)
    for k in range(S5_CLUSTERS):
        bu = jnp.dot(ub[:, k * S5_CLUSTER_CH:(k + 1) * S5_CLUSTER_CH], bw_ref[k], preferred_element_type=F32)
        sre[:, k * S5_CLUSTER_LANES:(k + 1) * S5_CLUSTER_LANES] = bu[:, :S5_CLUSTER_LANES]
        sim[:, k * S5_CLUSTER_LANES:(k + 1) * S5_CLUSTER_LANES] = bu[:, S5_CLUSTER_LANES:]

    n_tiles = SCAN_CHUNK // SCAN_TILE
    shifts = tuple(SCAN_TILE - k for k in SCAN_LEVELS) if reverse else SCAN_LEVELS
    boundary_row = 0 if reverse else SCAN_TILE - 1
    for k in range(S5_CLUSTERS):
        lanes = slice(k * S5_CLUSTER_LANES, (k + 1) * S5_CLUSTER_LANES)

        def tile_step(i, carry, lanes=lanes):
            cr, ci = carry
            tile = (n_tiles - 1 - i) if reverse else i
            rows = pl.ds(pl.multiple_of(tile * SCAN_TILE, SCAN_TILE), SCAN_TILE)
            xr, xi = sre[rows, lanes], sim[rows, lanes]
            for lvl, sh in enumerate(shifts):
                pr, pi = tab_ref[2 * lvl, :, lanes], tab_ref[2 * lvl + 1, :, lanes]
                rr, ri = pltpu.roll(xr, sh, 0), pltpu.roll(xi, sh, 0)
                xr, xi = xr + (pr * rr - pi * ri), xi + (pr * ri + pi * rr)
            pr, pi = tab_ref[2 * len(shifts), :, lanes], tab_ref[2 * len(shifts) + 1, :, lanes]
            xr, xi = xr + (pr * cr - pi * ci), xi + (pr * ci + pi * cr)
            sre[rows, lanes] = xr
            sim[rows, lanes] = xi
            return xr[boundary_row:boundary_row + 1, :], xi[boundary_row:boundary_row + 1, :]

        cr, ci = lax.fori_loop(0, n_tiles, tile_step, (car[:, lanes], cai[:, lanes]))
        car[:, lanes] = cr
        cai[:, lanes] = ci

    for k in range(S5_CLUSTERS):
        lanes = slice(k * S5_CLUSTER_LANES, (k + 1) * S5_CLUSTER_LANES)
        yk = jnp.dot(sre[:, lanes].astype(BF16), cwr_ref[k], preferred_element_type=F32)
        yk = yk + jnp.dot(sim[:, lanes].astype(BF16), cwi_ref[k], preferred_element_type=F32)
        y_ref[:, k * S5_CLUSTER_CH:(k + 1) * S5_CLUSTER_CH] = yk
    sfr_ref[0] = car[...]
    sfi_ref[0] = cai[...]


def _s5_direction(u, row0, n_seq, seq, s0_re, s0_im, bw, cwr, cwi, tabs, reverse):
    nc = seq // SCAN_CHUNK
    b0 = row0 // SCAN_CHUNK

    def chunk(c):
        return (nc - 1 - c) if reverse else c

    state_spec = pl.BlockSpec((1, 1, S5_LANES), lambda b, c: (b, 0, 0))
    full3 = lambda shape: pl.BlockSpec(shape, lambda b, c: (0, 0, 0), pipeline_mode=pl.Buffered(1))
    return pl.pallas_call(
        functools.partial(_s5_kernel, reverse=reverse),
        grid=(n_seq, nc),
        in_specs=[pl.BlockSpec((SCAN_CHUNK, S5_WIDTH), lambda b, c: (b0 + b * nc + chunk(c), 0)),
                  state_spec, state_spec,
                  full3(bw.shape), full3(cwr.shape), full3(cwi.shape), full3(tabs.shape)],
        out_specs=[pl.BlockSpec((SCAN_CHUNK, S5_WIDTH), lambda b, c: (b * nc + chunk(c), 0)),
                   state_spec, state_spec],
        out_shape=[jax.ShapeDtypeStruct((n_seq * seq, S5_WIDTH), F32),
                   jax.ShapeDtypeStruct((n_seq, 1, S5_LANES), F32),
                   jax.ShapeDtypeStruct((n_seq, 1, S5_LANES), F32)],
        scratch_shapes=[pltpu.VMEM((SCAN_CHUNK, S5_LANES), F32), pltpu.VMEM((SCAN_CHUNK, S5_LANES), F32),
                        pltpu.VMEM((1, S5_LANES), F32), pltpu.VMEM((1, S5_LANES), F32)],
        compiler_params=_params(("parallel", "arbitrary")),
        name="s5_bwd" if reverse else "s5_fwd",
    )(u, s0_re, s0_im, bw, cwr, cwi, tabs)


def _s5_weights(a_re, a_im, log_dt, b_re, b_im, c_re, c_im):
    lam = lax.complex(a_re, a_im)
    lam_bar = jnp.exp(lam * jnp.exp(log_dt)[..., None])
    b_bar = ((lam_bar - 1.0) / lam)[..., None] * lax.complex(b_re, b_im)
    eye = jnp.eye(S5_CLUSTER_GROUPS, dtype=F32)
    out = []
    for d in range(2):
        def cluster_in(w):
            w = w.reshape(S5_CLUSTERS, S5_CLUSTER_GROUPS, S5_STATE, S5_GROUP_CH)
            return jnp.einsum('ab,kapc->kacbp', eye, w).reshape(S5_CLUSTERS, S5_CLUSTER_CH, S5_CLUSTER_LANES)

        def cluster_out(w):
            w = w.reshape(S5_CLUSTERS, S5_CLUSTER_GROUPS, S5_GROUP_CH, S5_STATE)
            return jnp.einsum('ab,kacp->kapbc', eye, w).reshape(S5_CLUSTERS, S5_CLUSTER_LANES, S5_CLUSTER_CH)

        bw = jnp.concatenate([cluster_in(jnp.real(b_bar[d])), cluster_in(jnp.imag(b_bar[d]))], axis=-1)
        cwr = cluster_out(c_re[d])
        cwi = cluster_out(-c_im[d])
        lam_d = lam_bar[d].reshape(S5_LANES)
        pows = [lam_d]
        for _ in range(SCAN_TILE - 1):
            pows.append(pows[-1] * lam_d)
        pows = jnp.stack(pows)
        j = jnp.arange(SCAN_TILE)
        tabs = []
        for k in SCAN_LEVELS:
            valid = (j <= SCAN_TILE - 1 - k) if d == 1 else (j >= k)
            p = jnp.where(valid[:, None], pows[k - 1][None, :], 0.0)
            tabs += [jnp.real(p), jnp.imag(p)]
        pc = pows[::-1] if d == 1 else pows
        tabs += [jnp.real(pc), jnp.imag(pc)]
        out.append((bw.astype(BF16), cwr.astype(BF16), cwi.astype(BF16), jnp.stack(tabs).astype(F32)))
    return out


def _final_norm_kernel(x_ref, g_ref, o_ref):
    x = x_ref[...]
    ms = jnp.mean(x * x, axis=-1, keepdims=True)
    o_ref[...] = x * lax.rsqrt(ms + RMS_EPS) * g_ref[...]


def _final_norm(x, g):
    t = x.shape[0]
    tm = ROW_TILE
    return pl.pallas_call(
        _final_norm_kernel,
        grid=(t // tm,),
        in_specs=[pl.BlockSpec((tm, D_MODEL), lambda i: (i, 0)), _resident((1, D_MODEL))],
        out_specs=pl.BlockSpec((tm, D_MODEL), lambda i: (i, 0)),
        out_shape=jax.ShapeDtypeStruct((t, D_MODEL), F32),
        compiler_params=_params(("parallel",)),
        name="final_norm",
    )(x, g.reshape(1, D_MODEL))


def _rope_tables(L):
    n_rows = L // GRID_W
    row_idx = jnp.repeat(jnp.arange(n_rows, dtype=F32), GRID_W)
    col_idx = (jnp.arange(L) % GRID_W).astype(F32)
    inv = ROPE_THETA ** (-jnp.arange(0, ROPE_AXIS_DIM, 2, dtype=F32) / ROPE_AXIS_DIM)
    ang = jnp.concatenate([row_idx[:, None] * inv, col_idx[:, None] * inv], axis=-1)
    cos, sin = jnp.cos(ang), jnp.sin(ang)
    cos_full = jnp.repeat(cos, 2, axis=-1)
    sin_signed = jnp.stack([-sin, sin], axis=-1).reshape(L, HEAD_DIM)
    return cos_full, sin_signed


def kernel(x_prompt, x_sample, cache_k_l0, cache_v_l0, state_s5_re_l1, state_s5_im_l1, c, c_ctx,
           ada_w_l0, ada_b_l0, norm1_l0, norm2_l0,
           w_in_l0, w_out_l0, q_norm_l0, k_norm_l0, hy_conv_w_l0, hy_conv_b_l0,
           hy_ffn_w1_l0, hy_ffn_b1_l0, hy_ffn_w2_l0, hy_ffn_b2_l0, hy_ffn_w3_l0, hy_sin_freq_l0, hy_bias_l0,
           router_l0, router_bias_l0, exp_w13_l0, exp_w2_l0, shared_w13_l0, shared_w2_l0,
           ada_w_l1, ada_b_l1, norm1_l1, norm2_l1,
           w_in_l1, s5_a_re_l1, s5_a_im_l1, s5_log_dt_l1, s5_b_re_l1, s5_b_im_l1,
           s5_c_re_l1, s5_c_im_l1, s5_d_l1, w_glu_l1,
           router_l1, router_bias_l1, exp_w13_l1, exp_w2_l1, shared_w13_l1, shared_w2_l1,
           final_norm):
    x = jnp.concatenate([x_prompt.reshape(T_CTX, D_MODEL), x_sample.reshape(T_LAT, D_MODEL)], axis=0)
    cond = jnp.concatenate([c_ctx[None, :], c, jnp.zeros((N_COND - 1 - DEC_BATCH, D_MODEL), F32)], axis=0)

    mods = _ada_mods(cond, ada_w_l0, ada_b_l0)
    proj = _norm_mod_matmul(x, norm1_l0, mods, 0, 1, w_in_l0.astype(BF16))

    q_c, k_c, v_c, new_k = _qk_prep(proj, 0, T_CTX, q_norm_l0, k_norm_l0, None)
    attn_c = _attention(q_c, k_c, v_c, BATCH, SEQ, None)
    q_l, k_l, v_l = _qk_prep(proj, T_CTX, T_LAT, q_norm_l0, k_norm_l0, _rope_tables(DEC_SEQ))
    ctx_kv = (cache_k_l0.reshape(DEC_BATCH * PAST_LEN, KV_WIDTH).astype(BF16),
              cache_v_l0.reshape(DEC_BATCH * PAST_LEN, KV_WIDTH).astype(BF16))
    attn_l = _attention(q_l, k_l, v_l, DEC_BATCH, DEC_SEQ, ctx_kv)
    attn = jnp.concatenate([attn_c, attn_l], axis=0)

    hy_f = (hy_ffn_w1_l0, hy_ffn_b1_l0, hy_ffn_w2_l0, hy_ffn_b2_l0, hy_ffn_w3_l0, hy_sin_freq_l0)
    spec_c, mats_c = _hyena_operands(SEQ, HY_BLOCK_CTX, *hy_f)
    z_c = _hyena(proj, 0, BATCH, SEQ, HY_BLOCK_CTX, HY_TC_CTX, hy_conv_w_l0, hy_conv_b_l0, hy_bias_l0,
                 spec_c, mats_c)
    spec_l, mats_l = _hyena_operands(DEC_SEQ, HY_BLOCK_LAT, *hy_f)
    z_l = _hyena(proj, T_CTX, DEC_BATCH, DEC_SEQ, HY_BLOCK_LAT, HY_TC_LAT, hy_conv_w_l0, hy_conv_b_l0, hy_bias_l0,
                 spec_l, mats_l)
    z = jnp.concatenate([z_c, z_l], axis=0)

    w_out = w_out_l0.astype(BF16)
    x = _out_proj(attn, z, x, mods, 2, w_out[:Q_WIDTH], w_out[Q_WIDTH:])
    x = _moe(x, norm2_l0, mods, 3, 4, 5, router_l0, router_bias_l0, exp_w13_l0, exp_w2_l0,
             shared_w13_l0, shared_w2_l0)

    new_v = proj[:T_CTX, Q_WIDTH + KV_WIDTH:QKV_WIDTH]

    mods = _ada_mods(cond, ada_w_l1, ada_b_l1)
    u = _norm_mod_matmul(x, norm1_l1, mods, 0, 1, w_in_l1.astype(BF16))
    s5_w = _s5_weights(s5_a_re_l1, s5_a_im_l1, s5_log_dt_l1, s5_b_re_l1, s5_b_im_l1, s5_c_re_l1, s5_c_im_l1)
    zero_state = jnp.zeros((BATCH, 1, S5_LANES), F32)
    ys, finals = [], []
    for d in range(2):
        y_c, f_re, f_im = _s5_direction(u, 0, BATCH, SEQ, zero_state, zero_state, *s5_w[d], reverse=d == 1)
        y_l, _, _ = _s5_direction(u, T_CTX, DEC_BATCH, DEC_SEQ,
                                  state_s5_re_l1[:, d].reshape(DEC_BATCH, 1, S5_LANES),
                                  state_s5_im_l1[:, d].reshape(DEC_BATCH, 1, S5_LANES), *s5_w[d], reverse=d == 1)
        ys.append(jnp.concatenate([y_c, y_l], axis=0))
        finals.append((f_re.reshape(BATCH, S5_GROUPS, S5_STATE), f_im.reshape(BATCH, S5_GROUPS, S5_STATE)))
    s_re = jnp.stack([finals[0][0], finals[1][0]], axis=1)
    s_im = jnp.stack([finals[0][1], finals[1][1]], axis=1)
    x = _glu_proj(ys[0], ys[1], u, s5_d_l1, x, mods, 2, w_glu_l1.astype(BF16))
    x = _moe(x, norm2_l1, mods, 3, 4, 5, router_l1, router_bias_l1, exp_w13_l1, exp_w2_l1,
             shared_w13_l1, shared_w2_l1)

    y_all = _final_norm(x, final_norm)
    return (y_all[:T_CTX].reshape(BATCH, SEQ, D_MODEL),
            y_all[T_CTX:].reshape(DEC_BATCH, DEC_SEQ, D_MODEL),
            new_k.reshape(BATCH, SEQ, N_KV_HEADS, HEAD_DIM),
            new_v.reshape(BATCH, SEQ, N_KV_HEADS, HEAD_DIM),
            s_re, s_im)
```

```python
import functools
import math

import jax
import jax.numpy as jnp
from jax import lax
from jax.experimental import pallas as pl
from jax.experimental.pallas import tpu as pltpu

F32 = jnp.float32
BF16 = jnp.bfloat16

D_MODEL = 2048
BATCH = 32
SEQ = 256
DEC_BATCH = 4
DEC_SEQ = 4096
PAST_LEN = 256
GRID_W = 64
RMS_EPS = 1e-6
HEAD_DIM = 128
N_HEADS = 8
N_KV_HEADS = 2
GQA_GROUP = N_HEADS // N_KV_HEADS
Q_WIDTH = N_HEADS * HEAD_DIM
KV_WIDTH = N_KV_HEADS * HEAD_DIM
QKV_WIDTH = Q_WIDTH + 2 * KV_WIDTH
ROPE_THETA = 10000.0
ROPE_AXIS_DIM = HEAD_DIM // 2
HYENA_WIDTH = D_MODEL // 2
HYENA_ORDER = 2
FILTER_BANDS = 16
DECAY_TARGET = 1e-2
DECAY_PCT_SHORT = 0.3
DECAY_PCT_LONG = 1.5
S5_WIDTH = D_MODEL // 2
S5_GROUP_CH = 16
S5_GROUPS = S5_WIDTH // S5_GROUP_CH
S5_STATE = 64
N_EXPERTS = 64
TOP_K = 8
N_EXPERT_GROUPS = 8
TOPK_GROUPS = 4
EXPERT_HIDDEN = 512
ROUTED_SCALE = 2.5
MOE_BLOCK = 256

T_CTX = BATCH * SEQ
T_LAT = DEC_BATCH * DEC_SEQ
T_ALL = T_CTX + T_LAT
N_COND = 8

ROW_TILE = 256
V7X_VMEM_LIMIT_BYTES = 56 * 1024 * 1024


def _params(semantics):
    return pltpu.CompilerParams(dimension_semantics=semantics, vmem_limit_bytes=V7X_VMEM_LIMIT_BYTES)


def _cond_row(i, tm):
    tok = i * tm
    return jnp.where(tok < T_CTX, 0, 1 + (tok - T_CTX) // DEC_SEQ)


def _mod_spec(k, tm):
    return pl.BlockSpec((1, 1, 1, D_MODEL), lambda i: (_cond_row(i, tm), k, 0, 0))


def _resident(shape):
    nd = len(shape)
    return pl.BlockSpec(shape, lambda i: (0,) * nd, pipeline_mode=pl.Buffered(1))


def _silu(x):
    return x * jax.nn.sigmoid(x)


def _norm_mod(x, g, sc, sh):
    ms = jnp.mean(x * x, axis=-1, keepdims=True)
    return (x * lax.rsqrt(ms + RMS_EPS) * g) * (1.0 + sc) + sh


def _ada_kernel(c_ref, w_ref, b_ref, o_ref):
    a = _silu(c_ref[...]).astype(BF16)
    o_ref[...] = jnp.dot(a, w_ref[...].astype(BF16), preferred_element_type=F32) + b_ref[...]


def _ada_mods(cond, w, b):
    n = w.shape[1]
    tn = 1024
    out = pl.pallas_call(
        _ada_kernel,
        grid=(n // tn,),
        in_specs=[pl.BlockSpec((N_COND, D_MODEL), lambda j: (0, 0)),
                  pl.BlockSpec((D_MODEL, tn), lambda j: (0, j)),
                  pl.BlockSpec((1, tn), lambda j: (0, j))],
        out_specs=pl.BlockSpec((N_COND, tn), lambda j: (0, j)),
        out_shape=jax.ShapeDtypeStruct((N_COND, n), F32),
        compiler_params=_params(("parallel",)),
        name="ada_mods",
    )(cond, w, b.reshape(1, n))
    return out.reshape(N_COND, 6, 1, D_MODEL)


def _nmm_kernel(x_ref, g_ref, sc_ref, sh_ref, w_ref, o_ref):
    h = _norm_mod(x_ref[...], g_ref[...], sc_ref[0, 0], sh_ref[0, 0])
    o_ref[...] = jnp.dot(h.astype(BF16), w_ref[...], preferred_element_type=F32)


def _norm_mod_matmul(x, g, mods4, k_shift, k_scale, w_bf16):
    t, n = x.shape[0], w_bf16.shape[1]
    tm = ROW_TILE
    return pl.pallas_call(
        _nmm_kernel,
        grid=(t // tm,),
        in_specs=[pl.BlockSpec((tm, D_MODEL), lambda i: (i, 0)),
                  _resident((1, D_MODEL)),
                  _mod_spec(k_scale, tm), _mod_spec(k_shift, tm),
                  _resident((D_MODEL, n))],
        out_specs=pl.BlockSpec((tm, n), lambda i: (i, 0)),
        out_shape=jax.ShapeDtypeStruct((t, n), F32),
        compiler_params=_params(("parallel",)),
        name="norm_mod_matmul",
    )(x, g.reshape(1, D_MODEL), mods4, mods4, w_bf16)


def _head_norm(xh, g):
    ms = jnp.mean(xh * xh, axis=-1, keepdims=True)
    return xh * lax.rsqrt(ms + RMS_EPS) * g


def _rope(y, cos, sin_signed):
    lane = lax.broadcasted_iota(jnp.int32, y.shape, 1)
    partner = jnp.where(lane % 2 == 0, pltpu.roll(y, HEAD_DIM - 1, 1), pltpu.roll(y, 1, 1))
    return y * cos + partner * sin_signed


def _qkprep_ctx_kernel(qkv_ref, qg_ref, kg_ref, q_out, kb_out, vb_out, kf_out):
    for h in range(N_HEADS):
        sl = slice(h * HEAD_DIM, (h + 1) * HEAD_DIM)
        q_out[:, sl] = _head_norm(qkv_ref[:, sl], qg_ref[...]).astype(BF16)
    for j in range(N_KV_HEADS):
        k = _head_norm(qkv_ref[:, Q_WIDTH + j * HEAD_DIM:Q_WIDTH + (j + 1) * HEAD_DIM], kg_ref[...])
        kf_out[:, j * HEAD_DIM:(j + 1) * HEAD_DIM] = k
        kb_out[:, j * HEAD_DIM:(j + 1) * HEAD_DIM] = k.astype(BF16)
    vb_out[...] = qkv_ref[:, Q_WIDTH + KV_WIDTH:QKV_WIDTH].astype(BF16)


def _qkprep_lat_kernel(qkv_ref, qg_ref, kg_ref, cos_ref, sin_ref, q_out, kb_out, vb_out):
    cos, sin = cos_ref[...], sin_ref[...]
    for h in range(N_HEADS):
        sl = slice(h * HEAD_DIM, (h + 1) * HEAD_DIM)
        q_out[:, sl] = _rope(_head_norm(qkv_ref[:, sl], qg_ref[...]), cos, sin).astype(BF16)
    for j in range(N_KV_HEADS):
        k = _head_norm(qkv_ref[:, Q_WIDTH + j * HEAD_DIM:Q_WIDTH + (j + 1) * HEAD_DIM], kg_ref[...])
        kb_out[:, j * HEAD_DIM:(j + 1) * HEAD_DIM] = _rope(k, cos, sin).astype(BF16)
    vb_out[...] = qkv_ref[:, Q_WIDTH + KV_WIDTH:QKV_WIDTH].astype(BF16)


def _qk_prep(proj, row0, t, q_norm, k_norm, rope_tabs):
    tm = ROW_TILE
    b0 = row0 // tm
    qkv_spec = pl.BlockSpec((tm, QKV_WIDTH), lambda i: (b0 + i, 0))
    gain = _resident((1, HEAD_DIM))
    outs = [pl.BlockSpec((tm, Q_WIDTH), lambda i: (i, 0)),
            pl.BlockSpec((tm, KV_WIDTH), lambda i: (i, 0)),
            pl.BlockSpec((tm, KV_WIDTH), lambda i: (i, 0))]
    shapes = [jax.ShapeDtypeStruct((t, Q_WIDTH), BF16),
              jax.ShapeDtypeStruct((t, KV_WIDTH), BF16),
              jax.ShapeDtypeStruct((t, KV_WIDTH), BF16)]
    qg, kg = q_norm.reshape(1, HEAD_DIM), k_norm.reshape(1, HEAD_DIM)
    if rope_tabs is None:
        return pl.pallas_call(
            _qkprep_ctx_kernel, grid=(t // tm,),
            in_specs=[qkv_spec, gain, gain],
            out_specs=outs + [pl.BlockSpec((tm, KV_WIDTH), lambda i: (i, 0))],
            out_shape=shapes + [jax.ShapeDtypeStruct((t, KV_WIDTH), F32)],
            compiler_params=_params(("parallel",)), name="qk_prep_ctx",
        )(proj, qg, kg)
    cos, sin = rope_tabs
    nq = DEC_SEQ // tm
    tab = pl.BlockSpec((tm, HEAD_DIM), lambda i: (i % nq, 0))
    return pl.pallas_call(
        _qkprep_lat_kernel, grid=(t // tm,),
        in_specs=[qkv_spec, gain, gain, tab, tab],
        out_specs=outs, out_shape=shapes,
        compiler_params=_params(("parallel",)), name="qk_prep_lat",
    )(proj, qg, kg, cos, sin)


def _attn_kernel(*refs, with_ctx):
    if with_ctx:
        q_ref, k_ref, v_ref, ck_ref, cv_ref, o_ref = refs
    else:
        q_ref, k_ref, v_ref, o_ref = refs
    scale = HEAD_DIM ** -0.5
    nt = (((1,), (1,)), ((), ()))
    q = q_ref[...]
    s = lax.dot_general(q, k_ref[...], nt, preferred_element_type=F32) * scale
    m = jnp.max(s, axis=-1, keepdims=True)
    if with_ctx:
        sc = lax.dot_general(q, ck_ref[...], nt, preferred_element_type=F32) * scale
        m = jnp.maximum(m, jnp.max(sc, axis=-1, keepdims=True))
    p = jnp.exp(s - m)
    l = jnp.sum(p, axis=-1, keepdims=True)
    o = jnp.dot(p.astype(BF16), v_ref[...], preferred_element_type=F32)
    if with_ctx:
        pc = jnp.exp(sc - m)
        l = l + jnp.sum(pc, axis=-1, keepdims=True)
        o = o + jnp.dot(pc.astype(BF16), cv_ref[...], preferred_element_type=F32)
    o_ref[...] = (o / l).astype(BF16)


def _attention(q, k, v, n_batch, seq, ctx_kv):
    tq = ROW_TILE
    nq = seq // tq
    t = n_batch * seq
    q_spec = pl.BlockSpec((tq, HEAD_DIM), lambda b, h, i: (b * nq + i, h))
    kv_spec = pl.BlockSpec((seq, HEAD_DIM), lambda b, h, i: (b, h // GQA_GROUP))
    in_specs = [q_spec, kv_spec, kv_spec]
    args = [q, k, v]
    if ctx_kv is not None:
        c_spec = pl.BlockSpec((PAST_LEN, HEAD_DIM), lambda b, h, i: (b, h // GQA_GROUP))
        in_specs += [c_spec, c_spec]
        args += list(ctx_kv)
    return pl.pallas_call(
        functools.partial(_attn_kernel, with_ctx=ctx_kv is not None),
        grid=(n_batch, N_HEADS, nq),
        in_specs=in_specs,
        out_specs=pl.BlockSpec((tq, HEAD_DIM), lambda b, h, i: (b * nq + i, h)),
        out_shape=jax.ShapeDtypeStruct((t, Q_WIDTH), BF16),
        compiler_params=_params(("parallel", "parallel", "parallel")),
        name="attention_lat" if ctx_kv is not None else "attention_ctx",
    )(*args)


def _outproj_kernel(a_ref, z_ref, x_ref, gate_ref, wa_ref, wz_ref, o_ref):
    acc = jnp.dot(a_ref[...], wa_ref[...], preferred_element_type=F32)
    acc = acc + jnp.dot(z_ref[...].astype(BF16), wz_ref[...], preferred_element_type=F32)
    o_ref[...] = x_ref[...] + gate_ref[0, 0] * acc


def _out_proj(attn, z, x, mods4, k_gate, wa_bf16, wz_bf16):
    t = x.shape[0]
    tm = ROW_TILE
    return pl.pallas_call(
        _outproj_kernel,
        grid=(t // tm,),
        in_specs=[pl.BlockSpec((tm, Q_WIDTH), lambda i: (i, 0)),
                  pl.BlockSpec((tm, HYENA_WIDTH), lambda i: (i, 0)),
                  pl.BlockSpec((tm, D_MODEL), lambda i: (i, 0)),
                  _mod_spec(k_gate, tm),
                  _resident((Q_WIDTH, D_MODEL)), _resident((HYENA_WIDTH, D_MODEL))],
        out_specs=pl.BlockSpec((tm, D_MODEL), lambda i: (i, 0)),
        out_shape=jax.ShapeDtypeStruct((t, D_MODEL), F32),
        compiler_params=_params(("parallel",)),
        name="out_proj",
    )(attn, z, x, mods4, wa_bf16, wz_bf16)


def _gelu_tanh(x):
    return 0.5 * x * (1.0 + jnp.tanh(math.sqrt(2.0 / math.pi) * (x + 0.044715 * (x * x * x))))


def _glu_kernel(yf_ref, yb_ref, u_ref, d_ref, x_ref, gate_ref, w_ref, o_ref):
    y = (yf_ref[...] + yb_ref[...]) + d_ref[...] * u_ref[...]
    ag = jnp.dot(_gelu_tanh(y).astype(BF16), w_ref[...], preferred_element_type=F32)
    a, g = ag[:, :D_MODEL], ag[:, D_MODEL:]
    o_ref[...] = x_ref[...] + gate_ref[0, 0] * (a * jax.nn.sigmoid(g))


def _glu_proj(y_fwd, y_bwd, u, d_skip, x, mods4, k_gate, w_bf16):
    t = x.shape[0]
    tm = ROW_TILE
    s5_rows = pl.BlockSpec((tm, S5_WIDTH), lambda i: (i, 0))
    return pl.pallas_call(
        _glu_kernel,
        grid=(t // tm,),
        in_specs=[s5_rows, s5_rows, s5_rows, _resident((1, S5_WIDTH)),
                  pl.BlockSpec((tm, D_MODEL), lambda i: (i, 0)),
                  _mod_spec(k_gate, tm),
                  _resident((S5_WIDTH, 2 * D_MODEL))],
        out_specs=pl.BlockSpec((tm, D_MODEL), lambda i: (i, 0)),
        out_shape=jax.ShapeDtypeStruct((t, D_MODEL), F32),
        compiler_params=_params(("parallel",)),
        name="glu_proj",
    )(y_fwd, y_bwd, u, d_skip.reshape(1, S5_WIDTH), x, mods4, w_bf16)


def _gmm_kernel(be_ref, xs_ref, w13_ref, w2_ref, o_ref, w13_b, w2_b):
    i = pl.program_id(0)
    fresh = jnp.logical_or(i == 0, be_ref[i] != be_ref[jnp.maximum(i - 1, 0)])

    @pl.when(fresh)
    def _():
        w13_b[...] = w13_ref[0].astype(BF16)
        w2_b[...] = w2_ref[0].astype(BF16)

    gu = jnp.dot(xs_ref[...], w13_b[...], preferred_element_type=F32)
    hmid = _silu(gu[:, :EXPERT_HIDDEN]) * gu[:, EXPERT_HIDDEN:]
    o_ref[...] = jnp.dot(hmid.astype(BF16), w2_b[...], preferred_element_type=F32).astype(BF16)


def _grouped_experts(blk_e, xs, w13, w2):
    n_rows = xs.shape[0]
    n_blocks = n_rows // MOE_BLOCK
    grid_spec = pltpu.PrefetchScalarGridSpec(
        num_scalar_prefetch=1,
        grid=(n_blocks,),
        in_specs=[pl.BlockSpec((MOE_BLOCK, D_MODEL), lambda i, be: (i, 0)),
                  pl.BlockSpec((1, D_MODEL, 2 * EXPERT_HIDDEN), lambda i, be: (be[i], 0, 0)),
                  pl.BlockSpec((1, EXPERT_HIDDEN, D_MODEL), lambda i, be: (be[i], 0, 0))],
        out_specs=pl.BlockSpec((MOE_BLOCK, D_MODEL), lambda i, be: (i, 0)),
        scratch_shapes=[pltpu.VMEM((D_MODEL, 2 * EXPERT_HIDDEN), BF16),
                        pltpu.VMEM((EXPERT_HIDDEN, D_MODEL), BF16)],
    )
    return pl.pallas_call(
        _gmm_kernel,
        grid_spec=grid_spec,
        out_shape=jax.ShapeDtypeStruct((n_rows, D_MODEL), BF16),
        compiler_params=_params(("arbitrary",)),
        name="grouped_experts",
    )(blk_e, xs, w13, w2)


COMBINE_TILE = 256


def _shared_kernel(h_ref, x_ref, ye_ref, w_ref, gate_ref, w13_ref, w2_ref, o_ref):
    gu = jnp.dot(h_ref[...], w13_ref[...], preferred_element_type=F32)
    hmid = _silu(gu[:, :EXPERT_HIDDEN]) * gu[:, EXPERT_HIDDEN:]
    acc = jnp.dot(hmid.astype(BF16), w2_ref[...], preferred_element_type=F32)
    wts = w_ref[...]
    for k in range(TOP_K):
        acc = acc + wts[:, k:k + 1] * ye_ref[k].astype(F32)
    o_ref[...] = x_ref[...] + gate_ref[0, 0] * acc


def _shared_and_combine(h_bf16, x, y_exp, wts, mods4, k_gate, w13_bf16, w2_bf16):
    t = x.shape[0]
    tm = COMBINE_TILE
    row = lambda i: (i, 0)
    return pl.pallas_call(
        _shared_kernel,
        grid=(t // tm,),
        in_specs=[pl.BlockSpec((tm, D_MODEL), row), pl.BlockSpec((tm, D_MODEL), row),
                  pl.BlockSpec((TOP_K, tm, D_MODEL), lambda i: (0, i, 0)),
                  pl.BlockSpec((tm, TOP_K), row), _mod_spec(k_gate, tm),
                  _resident((D_MODEL, 2 * EXPERT_HIDDEN)), _resident((EXPERT_HIDDEN, D_MODEL))],
        out_specs=pl.BlockSpec((tm, D_MODEL), row),
        out_shape=jax.ShapeDtypeStruct((t, D_MODEL), F32),
        compiler_params=_params(("parallel",)),
        name="shared_expert_combine",
    )(h_bf16, x, y_exp, wts, mods4, w13_bf16, w2_bf16)


ROUTE_TILE = 256
GROUP_SIZE = N_EXPERTS // N_EXPERT_GROUPS
NEG_INF = float("-inf")


def _first_argmax(v, index, sentinel):
    m = jnp.max(v, axis=0, keepdims=True)
    first = jnp.min(jnp.where(v == m, index, sentinel), axis=0, keepdims=True)
    return m, first


def _router_kernel(x_ref, g_ref, sc_ref, sh_ref, wt_ref, rb_ref, tri_ref,
                   h_out, idx_out, gate_out, rank_out, cnt_out, base_ref):
    tm = x_ref.shape[0]

    @pl.when(pl.program_id(0) == 0)
    def _():
        base_ref[...] = jnp.zeros_like(base_ref)

    hb = _norm_mod(x_ref[...], g_ref[...], sc_ref[0, 0], sh_ref[0, 0]).astype(BF16)
    h_out[...] = hb
    logits = lax.dot_general(wt_ref[...], hb, (((1,), (1,)), ((), ())), preferred_element_type=F32)
    scores = jax.nn.sigmoid(logits)
    choice = scores + rb_ref[...]

    member = lax.broadcasted_iota(jnp.int32, (GROUP_SIZE, tm), 0).astype(F32)
    group_scores = []
    for g in range(N_EXPERT_GROUPS):
        cg = choice[g * GROUP_SIZE:(g + 1) * GROUP_SIZE, :]
        m1, first = _first_argmax(cg, member, float(GROUP_SIZE))
        m2 = jnp.max(jnp.where(member == first, NEG_INF, cg), axis=0, keepdims=True)
        group_scores.append(m1 + m2)
    gs = jnp.concatenate(group_scores, axis=0)

    group = lax.broadcasted_iota(jnp.int32, (N_EXPERT_GROUPS, tm), 0).astype(F32)
    keep = jnp.zeros((N_EXPERT_GROUPS, tm), F32)
    for _ in range(TOPK_GROUPS):
        _, first = _first_argmax(gs, group, float(N_EXPERT_GROUPS))
        sel = group == first
        keep = jnp.where(sel, 1.0, keep)
        gs = jnp.where(sel, NEG_INF, gs)
    masked = jnp.concatenate(
        [jnp.where(keep[g:g + 1, :] > 0.0, choice[g * GROUP_SIZE:(g + 1) * GROUP_SIZE, :], NEG_INF)
         for g in range(N_EXPERT_GROUPS)], axis=0)

    expert = lax.broadcasted_iota(jnp.int32, (N_EXPERTS, tm), 0).astype(F32)
    onehot = jnp.zeros((N_EXPERTS, tm), F32)
    sels, idx_rows, gate_rows = [], [], []
    for _ in range(TOP_K):
        _, first = _first_argmax(masked, expert, float(N_EXPERTS))
        sel = expert == first
        sels.append(sel)
        idx_rows.append(first)
        gate_rows.append(jnp.sum(jnp.where(sel, scores, 0.0), axis=0, keepdims=True))
        masked = jnp.where(sel, NEG_INF, masked)
        onehot = jnp.where(sel, 1.0, onehot)
    gates = jnp.concatenate(gate_rows, axis=0)
    gate_out[...] = gates / jnp.sum(gates, axis=0, keepdims=True) * ROUTED_SCALE
    idx_out[...] = jnp.concatenate(idx_rows, axis=0).astype(jnp.int32)

    cum = jnp.dot(onehot.astype(BF16), tri_ref[...], preferred_element_type=F32) + base_ref[:, 0:1]
    rank_rows = [jnp.sum(jnp.where(sel, cum, 0.0), axis=0, keepdims=True) - 1.0 for sel in sels]
    rank_out[...] = jnp.concatenate(rank_rows, axis=0).astype(jnp.int32)
    base_ref[...] = jnp.broadcast_to(cum[:, tm - 1:tm], base_ref.shape)
    cnt_out[...] = base_ref[...].astype(jnp.int32)


def _router(x, norm_g, mods4, k_shift, k_scale, router_w, router_bias):
    t = x.shape[0]
    tm = ROUTE_TILE
    tri = (jnp.arange(tm)[:, None] <= jnp.arange(tm)[None, :]).astype(BF16)
    choice_rows = pl.BlockSpec((TOP_K, tm), lambda i: (0, i))
    return pl.pallas_call(
        _router_kernel,
        grid=(t // tm,),
        in_specs=[pl.BlockSpec((tm, D_MODEL), lambda i: (i, 0)), _resident((1, D_MODEL)),
                  _mod_spec(k_scale, tm), _mod_spec(k_shift, tm),
                  _resident((N_EXPERTS, D_MODEL)), _resident((N_EXPERTS, 1)), _resident((tm, tm))],
        out_specs=[pl.BlockSpec((tm, D_MODEL), lambda i: (i, 0)), choice_rows, choice_rows, choice_rows,
                   pl.BlockSpec((N_EXPERTS, 128), lambda i: (0, 0))],
        out_shape=[jax.ShapeDtypeStruct((t, D_MODEL), BF16),
                   jax.ShapeDtypeStruct((TOP_K, t), jnp.int32), jax.ShapeDtypeStruct((TOP_K, t), F32),
                   jax.ShapeDtypeStruct((TOP_K, t), jnp.int32), jax.ShapeDtypeStruct((N_EXPERTS, 128), jnp.int32)],
        scratch_shapes=[pltpu.VMEM((N_EXPERTS, 128), F32)],
        compiler_params=_params(("arbitrary",)),
        name="router",
    )(x, norm_g.reshape(1, D_MODEL), mods4, mods4, router_w.T.astype(BF16), router_bias.reshape(N_EXPERTS, 1), tri)


def _pos_kernel(idx_ref, rank_ref, off_ref, pos_out):
    tm = idx_ref.shape[1]
    expert = lax.broadcasted_iota(jnp.int32, (N_EXPERTS, tm), 0)
    rows = []
    for k in range(TOP_K):
        sel = expert == idx_ref[k:k + 1, :]
        rows.append(jnp.sum(jnp.where(sel, off_ref[...], 0.0), axis=0, keepdims=True))
    pos_out[...] = jnp.concatenate(rows, axis=0).astype(jnp.int32) + rank_ref[...]


def _positions(idx, rank, offsets):
    t = idx.shape[1]
    tm = 1024
    rows = pl.BlockSpec((TOP_K, tm), lambda i: (0, i))
    return pl.pallas_call(
        _pos_kernel, grid=(t // tm,),
        in_specs=[rows, rows, _resident((N_EXPERTS, 1))],
        out_specs=rows, out_shape=jax.ShapeDtypeStruct((TOP_K, t), jnp.int32),
        compiler_params=_params(("parallel",)), name="expert_positions",
    )(idx, rank, offsets.astype(F32).reshape(N_EXPERTS, 1))


def _moe(x, norm_g, mods4, k_shift, k_scale, k_gate, router_w, router_bias, exp_w13, exp_w2, sh_w13, sh_w2):
    t = x.shape[0]
    h_bf16, top_idx, gate, rank, counts = _router(x, norm_g, mods4, k_shift, k_scale, router_w, router_bias)
    counts = counts[:, 0]
    padded = (counts + MOE_BLOCK - 1) // MOE_BLOCK * MOE_BLOCK
    pend = jnp.cumsum(padded)
    pos = _positions(top_idx, rank, pend - padded)
    n_rows = -(-(t * TOP_K + N_EXPERTS * (MOE_BLOCK - 1)) // MOE_BLOCK) * MOE_BLOCK
    n_blocks = n_rows // MOE_BLOCK
    tok = jnp.broadcast_to(jnp.arange(t, dtype=jnp.int32)[None, :], (TOP_K, t))
    row_tok = jnp.zeros((n_rows,), jnp.int32).at[pos.reshape(-1)].set(tok.reshape(-1), unique_indices=True)
    blk_start = jnp.arange(n_blocks, dtype=pend.dtype) * MOE_BLOCK
    blk_e = jnp.minimum(jnp.sum((pend[None, :] <= blk_start[:, None]).astype(jnp.int32), axis=1), N_EXPERTS - 1)
    xs = h_bf16[row_tok]
    ys = _grouped_experts(blk_e, xs, exp_w13, exp_w2)
    y_exp = ys[pos.reshape(-1)].reshape(TOP_K, t, D_MODEL)
    return _shared_and_combine(h_bf16, x, y_exp, gate.T, mods4, k_gate, sh_w13.astype(BF16), sh_w2.astype(BF16))


HY_BLOCK_CTX, HY_TC_CTX = 256, 512
HY_BLOCK_LAT, HY_TC_LAT = 512, 128


def _hyena_filters(L, w1, b1, w2, b2, w3, sin_freq):
    t_norm = jnp.linspace(0.0, 1.0, L, dtype=F32)[:, None]
    omega = 2.0 * math.pi * jnp.arange(L, dtype=F32)[:, None] / L
    bands = jnp.linspace(1e-4, FILTER_BANDS - 1, FILTER_BANDS, dtype=F32)[None, :]
    z = jnp.concatenate([t_norm, jnp.cos(bands * omega), -jnp.sin(bands * omega)], axis=-1)
    h = jnp.sin(sin_freq * (z @ w1 + b1))
    h = jnp.sin(sin_freq * (h @ w2 + b2))
    h = (h @ w3).reshape(L, 2, HYENA_ORDER, HYENA_WIDTH)
    deltas = jnp.abs(jnp.linspace(math.log(DECAY_TARGET) / DECAY_PCT_LONG,
                                  math.log(DECAY_TARGET) / DECAY_PCT_SHORT, HYENA_WIDTH, dtype=F32))
    h = h * jnp.exp(-t_norm * deltas)[:, None, None, :]
    fwd, bwd = h[:, 0], h[:, 1]
    filt = jnp.concatenate([fwd, jnp.zeros_like(fwd[:1]), bwd[:0:-1]], axis=0)
    return filt / jnp.sum(jnp.abs(filt), axis=0, keepdims=True)


HY_ACC_ELEMS = 8192


def _dft_matrices(bk):
    k = jnp.arange(bk, dtype=jnp.int32)
    phase = ((2 * k[:, None] + 1) * k[None, :]) % (4 * bk)
    ang = phase.astype(F32) * (math.pi / (2 * bk))
    return jnp.cos(ang), jnp.sin(ang)


def _filter_spectra(filt, bk):
    L = filt.shape[0] // 2
    nb = L // bk
    g = jnp.concatenate([filt[L:], filt[:L]], axis=0)
    seg = g.reshape(2 * nb, bk, HYENA_ORDER, HYENA_WIDTH)
    cur, prev = seg[1:], seg[:-1]
    cos, sin = _dft_matrices(bk)
    k = jnp.arange(bk, dtype=jnp.int32)
    phase = ((2 * k[:, None] + 1) * (bk - k[None, :])) % (4 * bk)
    ang = phase.astype(F32) * (math.pi / (2 * bk))
    live = (k[None, :] > 0).astype(F32)
    cos_neg, sin_neg = jnp.cos(ang) * live, jnp.sin(ang) * live
    hi = lax.Precision.HIGHEST
    ein = lambda m, a: jnp.einsum('kj,djow->odkw', m, a, precision=hi)
    re = ein(cos, cur) + ein(cos_neg, prev)
    im = ein(sin_neg, prev) - ein(sin, cur)
    return jnp.stack([re, im], axis=2)


def _hyena_kernel(v_ref, x1_ref, x2_ref, cw_ref, cb_ref, hb_ref, k0_ref, k1_ref, c_ref, s_ref, ct_ref, st_ref,
                  o_ref, vs_ref, z1_ref, x2s_ref, ur_ref, ui_ref, yr_ref, yi_ref, *, seq, bk, tc):
    nb = seq // bk
    ft = HY_ACC_ELEMS // tc
    row = lax.broadcasted_iota(jnp.int32, (seq, tc), 0)

    def short_conv(x, j):
        prev = jnp.where(row == 0, 0.0, pltpu.roll(x, 1, 0))
        nxt = jnp.where(row == seq - 1, 0.0, pltpu.roll(x, seq - 1, 0))
        return (prev * cw_ref[0, j:j + 1, :] + x * cw_ref[1, j:j + 1, :] + nxt * cw_ref[2, j:j + 1, :]
                + cb_ref[j:j + 1, :])

    vs_ref[...] = short_conv(v_ref[...], 0)
    z1_ref[...] = short_conv(x1_ref[...], 1)
    x2s_ref[...] = short_conv(x2_ref[...], 2)

    def long_conv(src_ref, k_ref, emit):
        for blk in range(nb):
            ub = src_ref[blk * bk:(blk + 1) * bk, :].astype(BF16)
            ur_ref[blk] = jnp.dot(c_ref[...], ub, preferred_element_type=F32)
            ui_ref[blk] = -jnp.dot(s_ref[...], ub, preferred_element_type=F32)
        for out_blk in range(nb):
            def acc_tile(f, carry, out_blk=out_blk):
                rows = pl.ds(pl.multiple_of(f * ft, ft), ft)
                ar = jnp.zeros((ft, tc), F32)
                ai = jnp.zeros((ft, tc), F32)
                for in_blk in range(nb):
                    d = out_blk - in_blk + nb - 1
                    kr, ki = k_ref[d, 0, rows, :], k_ref[d, 1, rows, :]
                    xr, xi = ur_ref[in_blk, rows, :], ui_ref[in_blk, rows, :]
                    ar = ar + (kr * xr - ki * xi)
                    ai = ai + (kr * xi + ki * xr)
                yr_ref[rows, :] = ar
                yi_ref[rows, :] = ai
                return carry

            lax.fori_loop(0, bk // ft, acc_tile, 0)
            y = jnp.dot(ct_ref[...], yr_ref[...].astype(BF16), preferred_element_type=F32)
            y = y - jnp.dot(st_ref[...], yi_ref[...].astype(BF16), preferred_element_type=F32)
            emit(slice(out_blk * bk, (out_blk + 1) * bk), y * (1.0 / bk))

    def emit_z1(rows, y):
        z1_ref[rows, :] = z1_ref[rows, :] * (y + hb_ref[0:1, :] * vs_ref[rows, :])

    def emit_out(rows, y):
        o_ref[rows, :] = (x2s_ref[rows, :] * (y + hb_ref[1:2, :] * z1_ref[rows, :])).astype(BF16)

    long_conv(vs_ref, k0_ref, emit_z1)
    long_conv(z1_ref, k1_ref, emit_out)


def _hyena(proj, row0, n_batch, seq, bk, tc, conv_w, conv_b, hy_bias, spectra, mats):
    nb = seq // bk
    n_ct = HYENA_WIDTH // tc
    col0 = QKV_WIDTH // tc
    per = HYENA_WIDTH // tc
    b0 = row0 // seq

    def data(which):
        return pl.BlockSpec((seq, tc), lambda ct, b: (b0 + b, col0 + which * per + ct))

    const2 = lambda shape: pl.BlockSpec(shape, lambda ct, b: (0, 0), pipeline_mode=pl.Buffered(1))
    kspec = pl.BlockSpec((2 * nb - 1, 2, bk, tc), lambda ct, b: (0, 0, 0, ct), pipeline_mode=pl.Buffered(1))
    cw = conv_w.reshape(3, HYENA_ORDER + 1, HYENA_WIDTH)
    cb = conv_b.reshape(HYENA_ORDER + 1, HYENA_WIDTH)
    return pl.pallas_call(
        functools.partial(_hyena_kernel, seq=seq, bk=bk, tc=tc),
        grid=(n_ct, n_batch),
        in_specs=[data(0), data(1), data(2),
                  pl.BlockSpec((3, HYENA_ORDER + 1, tc), lambda ct, b: (0, 0, ct)),
                  pl.BlockSpec((HYENA_ORDER + 1, tc), lambda ct, b: (0, ct)),
                  pl.BlockSpec((HYENA_ORDER, tc), lambda ct, b: (0, ct)),
                  kspec, kspec, const2((bk, bk)), const2((bk, bk)), const2((bk, bk)), const2((bk, bk))],
        out_specs=pl.BlockSpec((seq, tc), lambda ct, b: (b, ct)),
        out_shape=jax.ShapeDtypeStruct((n_batch * seq, HYENA_WIDTH), BF16),
        scratch_shapes=[pltpu.VMEM((seq, tc), F32), pltpu.VMEM((seq, tc), F32), pltpu.VMEM((seq, tc), F32),
                        pltpu.VMEM((nb, bk, tc), F32), pltpu.VMEM((nb, bk, tc), F32),
                        pltpu.VMEM((bk, tc), F32), pltpu.VMEM((bk, tc), F32)],
        compiler_params=_params(("parallel", "parallel")),
        name="hyena_seq%d" % seq,
    )(proj, proj, proj, cw, cb, hy_bias, spectra[0], spectra[1], *mats)


def _hyena_operands(L, bk, f_w1, f_b1, f_w2, f_b2, f_w3, sin_freq):
    filt = _hyena_filters(L, f_w1, f_b1, f_w2, f_b2, f_w3, sin_freq)
    cos, sin = _dft_matrices(bk)
    mats = tuple(m.astype(BF16) for m in (cos, sin, cos.T, sin.T))
    return _filter_spectra(filt, bk), mats


S5_LANES = S5_GROUPS * S5_STATE
S5_CLUSTER_GROUPS = 16
S5_CLUSTERS = S5_GROUPS // S5_CLUSTER_GROUPS
S5_CLUSTER_CH = S5_CLUSTER_GROUPS * S5_GROUP_CH
S5_CLUSTER_LANES = S5_CLUSTER_GROUPS * S5_STATE
SCAN_CHUNK = 256
SCAN_TILE = 8
SCAN_LEVELS = (1, 2, 4)


def _s5_kernel(u_ref, s0r_ref, s0i_ref, bw_ref, cwr_ref, cwi_ref, tab_ref, y_ref, sfr_ref, sfi_ref,
               sre, sim, car, cai, *, reverse):
    @pl.when(pl.program_id(1) == 0)
    def _():
        car[...] = s0r_ref[0]
        cai[...] = s0i_ref[0]

    ub = u_ref[...].astype(BF16)
    for k in range(S5_CLUSTERS):
        bu = jnp.dot(ub[:, k * S5_CLUSTER_CH:(k + 1) * S5_CLUSTER_CH], bw_ref[k], preferred_element_type=F32)
        sre[:, k * S5_CLUSTER_LANES:(k + 1) * S5_CLUSTER_LANES] = bu[:, :S5_CLUSTER_LANES]
        sim[:, k * S5_CLUSTER_LANES:(k + 1) * S5_CLUSTER_LANES] = bu[:, S5_CLUSTER_LANES:]

    n_tiles = SCAN_CHUNK // SCAN_TILE
    shifts = tuple(SCAN_TILE - k for k in SCAN_LEVELS) if reverse else SCAN_LEVELS
    boundary_row = 0 if reverse else SCAN_TILE - 1
    for k in range(S5_CLUSTERS):
        lanes = slice(k * S5_CLUSTER_LANES, (k + 1) * S5_CLUSTER_LANES)

        def tile_step(i, carry, lanes=lanes):
            cr, ci = carry
            tile = (n_tiles - 1 - i) if reverse else i
            rows = pl.ds(pl.multiple_of(tile * SCAN_TILE, SCAN_TILE), SCAN_TILE)
            xr, xi = sre[rows, lanes], sim[rows, lanes]
            for lvl, sh in enumerate(shifts):
                pr, pi = tab_ref[2 * lvl, :, lanes], tab_ref[2 * lvl + 1, :, lanes]
                rr, ri = pltpu.roll(xr, sh, 0), pltpu.roll(xi, sh, 0)
                xr, xi = xr + (pr * rr - pi * ri), xi + (pr * ri + pi * rr)
            pr, pi = tab_ref[2 * len(shifts), :, lanes], tab_ref[2 * len(shifts) + 1, :, lanes]
            xr, xi = xr + (pr * cr - pi * ci), xi + (pr * ci + pi * cr)
            sre[rows, lanes] = xr
            sim[rows, lanes] = xi
            return xr[boundary_row:boundary_row + 1, :], xi[boundary_row:boundary_row + 1, :]

        cr, ci = lax.fori_loop(0, n_tiles, tile_step, (car[:, lanes], cai[:, lanes]))
        car[:, lanes] = cr
        cai[:, lanes] = ci

    for k in range(S5_CLUSTERS):
        lanes = slice(k * S5_CLUSTER_LANES, (k + 1) * S5_CLUSTER_LANES)
        yk = jnp.dot(sre[:, lanes].astype(BF16), cwr_ref[k], preferred_element_type=F32)
        yk = yk + jnp.dot(sim[:, lanes].astype(BF16), cwi_ref[k], preferred_element_type=F32)
        y_ref[:, k * S5_CLUSTER_CH:(k + 1) * S5_CLUSTER_CH] = yk
    sfr_ref[0] = car[...]
    sfi_ref[0] = cai[...]


def _s5_direction(u, row0, n_seq, seq, s0_re, s0_im, bw, cwr, cwi, tabs, reverse):
    nc = seq // SCAN_CHUNK
    b0 = row0 // SCAN_CHUNK

    def chunk(c):
        return (nc - 1 - c) if reverse else c

    state_spec = pl.BlockSpec((1, 1, S5_LANES), lambda b, c: (b, 0, 0))
    full3 = lambda shape: pl.BlockSpec(shape, lambda b, c: (0, 0, 0), pipeline_mode=pl.Buffered(1))
    return pl.pallas_call(
        functools.partial(_s5_kernel, reverse=reverse),
        grid=(n_seq, nc),
        in_specs=[pl.BlockSpec((SCAN_CHUNK, S5_WIDTH), lambda b, c: (b0 + b * nc + chunk(c), 0)),
                  state_spec, state_spec,
                  full3(bw.shape), full3(cwr.shape), full3(cwi.shape), full3(tabs.shape)],
        out_specs=[pl.BlockSpec((SCAN_CHUNK, S5_WIDTH), lambda b, c: (b * nc + chunk(c), 0)),
                   state_spec, state_spec],
        out_shape=[jax.ShapeDtypeStruct((n_seq * seq, S5_WIDTH), F32),
                   jax.ShapeDtypeStruct((n_seq, 1, S5_LANES), F32),
                   jax.ShapeDtypeStruct((n_seq, 1, S5_LANES), F32)],
        scratch_shapes=[pltpu.VMEM((SCAN_CHUNK, S5_LANES), F32), pltpu.VMEM((SCAN_CHUNK, S5_LANES), F32),
                        pltpu.VMEM((1, S5_LANES), F32), pltpu.VMEM((1, S5_LANES), F32)],
        compiler_params=_params(("parallel", "arbitrary")),
        name="s5_bwd" if reverse else "s5_fwd",
    )(u, s0_re, s0_im, bw, cwr, cwi, tabs)


def _s5_weights(a_re, a_im, log_dt, b_re, b_im, c_re, c_im):
    lam = lax.complex(a_re, a_im)
    lam_bar = jnp.exp(lam * jnp.exp(log_dt)[..., None])
    b_bar = ((lam_bar - 1.0) / lam)[..., None] * lax.complex(b_re, b_im)
    eye = jnp.eye(S5_CLUSTER_GROUPS, dtype=F32)
    out = []
    for d in range(2):
        def cluster_in(w):
            w = w.reshape(S5_CLUSTERS, S5_CLUSTER_GROUPS, S5_STATE, S5_GROUP_CH)
            return jnp.einsum('ab,kapc->kacbp', eye, w).reshape(S5_CLUSTERS, S5_CLUSTER_CH, S5_CLUSTER_LANES)

        def cluster_out(w):
            w = w.reshape(S5_CLUSTERS, S5_CLUSTER_GROUPS, S5_GROUP_CH, S5_STATE)
            return jnp.einsum('ab,kacp->kapbc', eye, w).reshape(S5_CLUSTERS, S5_CLUSTER_LANES, S5_CLUSTER_CH)

        bw = jnp.concatenate([cluster_in(jnp.real(b_bar[d])), cluster_in(jnp.imag(b_bar[d]))], axis=-1)
        cwr = cluster_out(c_re[d])
        cwi = cluster_out(-c_im[d])
        lam_d = lam_bar[d].reshape(S5_LANES)
        pows = [lam_d]
        for _ in range(SCAN_TILE - 1):
            pows.append(pows[-1] * lam_d)
        pows = jnp.stack(pows)
        j = jnp.arange(SCAN_TILE)
        tabs = []
        for k in SCAN_LEVELS:
            valid = (j <= SCAN_TILE - 1 - k) if d == 1 else (j >= k)
            p = jnp.where(valid[:, None], pows[k - 1][None, :], 0.0)
            tabs += [jnp.real(p), jnp.imag(p)]
        pc = pows[::-1] if d == 1 else pows
        tabs += [jnp.real(pc), jnp.imag(pc)]
        out.append((bw.astype(BF16), cwr.astype(BF16), cwi.astype(BF16), jnp.stack(tabs).astype(F32)))
    return out


def _final_norm_kernel(x_ref, g_ref, o_ref):
    x = x_ref[...]
    ms = jnp.mean(x * x, axis=-1, keepdims=True)
    o_ref[...] = x * lax.rsqrt(ms + RMS_EPS) * g_ref[...]


def _final_norm(x, g, row0, t):
    tm = ROW_TILE
    b0 = row0 // tm
    return pl.pallas_call(
        _final_norm_kernel,
        grid=(t // tm,),
        in_specs=[pl.BlockSpec((tm, D_MODEL), lambda i: (b0 + i, 0)), _resident((1, D_MODEL))],
        out_specs=pl.BlockSpec((tm, D_MODEL), lambda i: (i, 0)),
        out_shape=jax.ShapeDtypeStruct((t, D_MODEL), F32),
        compiler_params=_params(("parallel",)),
        name="final_norm",
    )(x, g.reshape(1, D_MODEL))


def _rope_tables(L):
    n_rows = L // GRID_W
    row_idx = jnp.repeat(jnp.arange(n_rows, dtype=F32), GRID_W)
    col_idx = (jnp.arange(L) % GRID_W).astype(F32)
    inv = ROPE_THETA ** (-jnp.arange(0, ROPE_AXIS_DIM, 2, dtype=F32) / ROPE_AXIS_DIM)
    ang = jnp.concatenate([row_idx[:, None] * inv, col_idx[:, None] * inv], axis=-1)
    cos, sin = jnp.cos(ang), jnp.sin(ang)
    cos_full = jnp.repeat(cos, 2, axis=-1)
    sin_signed = jnp.stack([-sin, sin], axis=-1).reshape(L, HEAD_DIM)
    return cos_full, sin_signed


def kernel(x_prompt, x_sample, cache_k_l0, cache_v_l0, state_s5_re_l1, state_s5_im_l1, c, c_ctx,
           ada_w_l0, ada_b_l0, norm1_l0, norm2_l0,
           w_in_l0, w_out_l0, q_norm_l0, k_norm_l0, hy_conv_w_l0, hy_conv_b_l0,
           hy_ffn_w1_l0, hy_ffn_b1_l0, hy_ffn_w2_l0, hy_ffn_b2_l0, hy_ffn_w3_l0, hy_sin_freq_l0, hy_bias_l0,
           router_l0, router_bias_l0, exp_w13_l0, exp_w2_l0, shared_w13_l0, shared_w2_l0,
           ada_w_l1, ada_b_l1, norm1_l1, norm2_l1,
           w_in_l1, s5_a_re_l1, s5_a_im_l1, s5_log_dt_l1, s5_b_re_l1, s5_b_im_l1,
           s5_c_re_l1, s5_c_im_l1, s5_d_l1, w_glu_l1,
           router_l1, router_bias_l1, exp_w13_l1, exp_w2_l1, shared_w13_l1, shared_w2_l1,
           final_norm):
    x = jnp.concatenate([x_prompt.reshape(T_CTX, D_MODEL), x_sample.reshape(T_LAT, D_MODEL)], axis=0)
    cond = jnp.concatenate([c_ctx[None, :], c, jnp.zeros((N_COND - 1 - DEC_BATCH, D_MODEL), F32)], axis=0)

    mods = _ada_mods(cond, ada_w_l0, ada_b_l0)
    proj = _norm_mod_matmul(x, norm1_l0, mods, 0, 1, w_in_l0.astype(BF16))

    q_c, k_c, v_c, new_k = _qk_prep(proj, 0, T_CTX, q_norm_l0, k_norm_l0, None)
    attn_c = _attention(q_c, k_c, v_c, BATCH, SEQ, None)
    q_l, k_l, v_l = _qk_prep(proj, T_CTX, T_LAT, q_norm_l0, k_norm_l0, _rope_tables(DEC_SEQ))
    ctx_kv = (cache_k_l0.reshape(DEC_BATCH * PAST_LEN, KV_WIDTH).astype(BF16),
              cache_v_l0.reshape(DEC_BATCH * PAST_LEN, KV_WIDTH).astype(BF16))
    attn_l = _attention(q_l, k_l, v_l, DEC_BATCH, DEC_SEQ, ctx_kv)
    attn = jnp.concatenate([attn_c, attn_l], axis=0)

    hy_f = (hy_ffn_w1_l0, hy_ffn_b1_l0, hy_ffn_w2_l0, hy_ffn_b2_l0, hy_ffn_w3_l0, hy_sin_freq_l0)
    spec_c, mats_c = _hyena_operands(SEQ, HY_BLOCK_CTX, *hy_f)
    z_c = _hyena(proj, 0, BATCH, SEQ, HY_BLOCK_CTX, HY_TC_CTX, hy_conv_w_l0, hy_conv_b_l0, hy_bias_l0,
                 spec_c, mats_c)
    spec_l, mats_l = _hyena_operands(DEC_SEQ, HY_BLOCK_LAT, *hy_f)
    z_l = _hyena(proj, T_CTX, DEC_BATCH, DEC_SEQ, HY_BLOCK_LAT, HY_TC_LAT, hy_conv_w_l0, hy_conv_b_l0, hy_bias_l0,
                 spec_l, mats_l)
    z = jnp.concatenate([z_c, z_l], axis=0)

    w_out = w_out_l0.astype(BF16)
    x = _out_proj(attn, z, x, mods, 2, w_out[:Q_WIDTH], w_out[Q_WIDTH:])
    x = _moe(x, norm2_l0, mods, 3, 4, 5, router_l0, router_bias_l0, exp_w13_l0, exp_w2_l0,
             shared_w13_l0, shared_w2_l0)

    new_v = proj[:T_CTX, Q_WIDTH + KV_WIDTH:QKV_WIDTH]

    mods = _ada_mods(cond, ada_w_l1, ada_b_l1)
    u = _norm_mod_matmul(x, norm1_l1, mods, 0, 1, w_in_l1.astype(BF16))
    s5_w = _s5_weights(s5_a_re_l1, s5_a_im_l1, s5_log_dt_l1, s5_b_re_l1, s5_b_im_l1, s5_c_re_l1, s5_c_im_l1)
    zero_state = jnp.zeros((BATCH, 1, S5_LANES), F32)
    ys, finals = [], []
    for d in range(2):
        y_c, f_re, f_im = _s5_direction(u, 0, BATCH, SEQ, zero_state, zero_state, *s5_w[d], reverse=d == 1)
        y_l, _, _ = _s5_direction(u, T_CTX, DEC_BATCH, DEC_SEQ,
                                  state_s5_re_l1[:, d].reshape(DEC_BATCH, 1, S5_LANES),
                                  state_s5_im_l1[:, d].reshape(DEC_BATCH, 1, S5_LANES), *s5_w[d], reverse=d == 1)
        ys.append(jnp.concatenate([y_c, y_l], axis=0))
        finals.append((f_re.reshape(BATCH, S5_GROUPS, S5_STATE), f_im.reshape(BATCH, S5_GROUPS, S5_STATE)))
    s_re = jnp.stack([finals[0][0], finals[1][0]], axis=1)
    s_im = jnp.stack([finals[0][1], finals[1][1]], axis=1)
    x = _glu_proj(ys[0], ys[1], u, s5_d_l1, x, mods, 2, w_glu_l1.astype(BF16))
    x = _moe(x, norm2_l1, mods, 3, 4, 5, router_l1, router_bias_l1, exp_w13_l1, exp_w2_l1,
             shared_w13_l1, shared_w2_l1)

    return (_final_norm(x, final_norm, 0, T_CTX).reshape(BATCH, SEQ, D_MODEL),
            _final_norm(x, final_norm, T_CTX, T_LAT).reshape(DEC_BATCH, DEC_SEQ, D_MODEL),
            new_k.reshape(BATCH, SEQ, N_KV_HEADS, HEAD_DIM),
            new_v.reshape(BATCH, SEQ, N_KV_HEADS, HEAD_DIM),
            s_re, s_im)
```

```python
import functools
import math

import jax
import jax.numpy as jnp
from jax import lax
from jax.experimental import pallas as pl
from jax.experimental.pallas import tpu as pltpu

F32 = jnp.float32
BF16 = jnp.bfloat16

D_MODEL = 2048
BATCH = 32
SEQ = 256
DEC_BATCH = 4
DEC_SEQ = 4096
PAST_LEN = 256
GRID_W = 64
RMS_EPS = 1e-6
HEAD_DIM = 128
N_HEADS = 8
N_KV_HEADS = 2
GQA_GROUP = N_HEADS // N_KV_HEADS
Q_WIDTH = N_HEADS * HEAD_DIM
KV_WIDTH = N_KV_HEADS * HEAD_DIM
QKV_WIDTH = Q_WIDTH + 2 * KV_WIDTH
ROPE_THETA = 10000.0
ROPE_AXIS_DIM = HEAD_DIM // 2
HYENA_WIDTH = D_MODEL // 2
HYENA_ORDER = 2
FILTER_BANDS = 16
DECAY_TARGET = 1e-2
DECAY_PCT_SHORT = 0.3
DECAY_PCT_LONG = 1.5
S5_WIDTH = D_MODEL // 2
S5_GROUP_CH = 16
S5_GROUPS = S5_WIDTH // S5_GROUP_CH
S5_STATE = 64
N_EXPERTS = 64
TOP_K = 8
N_EXPERT_GROUPS = 8
TOPK_GROUPS = 4
EXPERT_HIDDEN = 512
ROUTED_SCALE = 2.5
MOE_BLOCK = 256

T_CTX = BATCH * SEQ
T_LAT = DEC_BATCH * DEC_SEQ
T_ALL = T_CTX + T_LAT
N_COND = 8

ROW_TILE = 256
V7X_VMEM_LIMIT_BYTES = 56 * 1024 * 1024


def _params(semantics):
    return pltpu.CompilerParams(dimension_semantics=semantics, vmem_limit_bytes=V7X_VMEM_LIMIT_BYTES)


def _cond_row(i, tm):
    tok = i * tm
    return jnp.where(tok < T_CTX, 0, 1 + (tok - T_CTX) // DEC_SEQ)


def _mod_spec(k, tm):
    return pl.BlockSpec((1, 1, 1, D_MODEL), lambda i: (_cond_row(i, tm), k, 0, 0))


def _resident(shape):
    nd = len(shape)
    return pl.BlockSpec(shape, lambda i: (0,) * nd, pipeline_mode=pl.Buffered(1))


def _silu(x):
    return x * jax.nn.sigmoid(x)


def _norm_mod(x, g, sc, sh):
    ms = jnp.mean(x * x, axis=-1, keepdims=True)
    return (x * lax.rsqrt(ms + RMS_EPS) * g) * (1.0 + sc) + sh


def _ada_kernel(c_ref, w_ref, b_ref, o_ref):
    a = _silu(c_ref[...]).astype(BF16)
    o_ref[...] = jnp.dot(a, w_ref[...].astype(BF16), preferred_element_type=F32) + b_ref[...]


def _ada_mods(cond, w, b):
    n = w.shape[1]
    tn = 1024
    out = pl.pallas_call(
        _ada_kernel,
        grid=(n // tn,),
        in_specs=[pl.BlockSpec((N_COND, D_MODEL), lambda j: (0, 0)),
                  pl.BlockSpec((D_MODEL, tn), lambda j: (0, j)),
                  pl.BlockSpec((1, tn), lambda j: (0, j))],
        out_specs=pl.BlockSpec((N_COND, tn), lambda j: (0, j)),
        out_shape=jax.ShapeDtypeStruct((N_COND, n), F32),
        compiler_params=_params(("parallel",)),
        name="ada_mods",
    )(cond, w, b.reshape(1, n))
    return out.reshape(N_COND, 6, 1, D_MODEL)


def _split_row_specs(width, tm):
    n_ctx = T_CTX // tm
    return [pl.BlockSpec((tm, width), lambda i: (jnp.minimum(i, n_ctx - 1), 0)),
            pl.BlockSpec((tm, width), lambda i: (jnp.maximum(i - n_ctx, 0), 0))]


def _pick_rows(ctx_ref, lat_ref):
    return jnp.where(pl.program_id(0) < T_CTX // ctx_ref.shape[0], ctx_ref[...], lat_ref[...])


def _nmm_kernel(*refs, split):
    if split:
        xc_ref, xl_ref, g_ref, sc_ref, sh_ref, w_ref, o_ref = refs
        x = _pick_rows(xc_ref, xl_ref)
    else:
        x_ref, g_ref, sc_ref, sh_ref, w_ref, o_ref = refs
        x = x_ref[...]
    h = _norm_mod(x, g_ref[...], sc_ref[0, 0], sh_ref[0, 0])
    o_ref[...] = jnp.dot(h.astype(BF16), w_ref[...], preferred_element_type=F32)


def _norm_mod_matmul(x, g, mods4, k_shift, k_scale, w_bf16):
    n = w_bf16.shape[1]
    tm = ROW_TILE
    split = isinstance(x, tuple)
    x_args = list(x) if split else [x]
    x_specs = _split_row_specs(D_MODEL, tm) if split else [pl.BlockSpec((tm, D_MODEL), lambda i: (i, 0))]
    return pl.pallas_call(
        functools.partial(_nmm_kernel, split=split),
        grid=(T_ALL // tm,),
        in_specs=x_specs + [_resident((1, D_MODEL)), _mod_spec(k_scale, tm), _mod_spec(k_shift, tm),
                            _resident((D_MODEL, n))],
        out_specs=pl.BlockSpec((tm, n), lambda i: (i, 0)),
        out_shape=jax.ShapeDtypeStruct((T_ALL, n), F32),
        compiler_params=_params(("parallel",)),
        name="norm_mod_matmul",
    )(*x_args, g.reshape(1, D_MODEL), mods4, mods4, w_bf16)


def _head_norm(xh, g):
    ms = jnp.mean(xh * xh, axis=-1, keepdims=True)
    return xh * lax.rsqrt(ms + RMS_EPS) * g


def _rope(y, cos, sin_signed):
    lane = lax.broadcasted_iota(jnp.int32, y.shape, 1)
    partner = jnp.where(lane % 2 == 0, pltpu.roll(y, HEAD_DIM - 1, 1), pltpu.roll(y, 1, 1))
    return y * cos + partner * sin_signed


def _qkprep_ctx_kernel(qkv_ref, qg_ref, kg_ref, q_out, kb_out, vb_out, kf_out):
    for h in range(N_HEADS):
        sl = slice(h * HEAD_DIM, (h + 1) * HEAD_DIM)
        q_out[:, sl] = _head_norm(qkv_ref[:, sl], qg_ref[...]).astype(BF16)
    for j in range(N_KV_HEADS):
        k = _head_norm(qkv_ref[:, Q_WIDTH + j * HEAD_DIM:Q_WIDTH + (j + 1) * HEAD_DIM], kg_ref[...])
        kf_out[:, j * HEAD_DIM:(j + 1) * HEAD_DIM] = k
        kb_out[:, j * HEAD_DIM:(j + 1) * HEAD_DIM] = k.astype(BF16)
    vb_out[...] = qkv_ref[:, Q_WIDTH + KV_WIDTH:QKV_WIDTH].astype(BF16)


def _qkprep_lat_kernel(qkv_ref, qg_ref, kg_ref, cos_ref, sin_ref, q_out, kb_out, vb_out):
    cos, sin = cos_ref[...], sin_ref[...]
    for h in range(N_HEADS):
        sl = slice(h * HEAD_DIM, (h + 1) * HEAD_DIM)
        q_out[:, sl] = _rope(_head_norm(qkv_ref[:, sl], qg_ref[...]), cos, sin).astype(BF16)
    for j in range(N_KV_HEADS):
        k = _head_norm(qkv_ref[:, Q_WIDTH + j * HEAD_DIM:Q_WIDTH + (j + 1) * HEAD_DIM], kg_ref[...])
        kb_out[:, j * HEAD_DIM:(j + 1) * HEAD_DIM] = _rope(k, cos, sin).astype(BF16)
    vb_out[...] = qkv_ref[:, Q_WIDTH + KV_WIDTH:QKV_WIDTH].astype(BF16)


def _qk_prep(proj, row0, t, q_norm, k_norm, rope_tabs):
    tm = ROW_TILE
    b0 = row0 // tm
    qkv_spec = pl.BlockSpec((tm, QKV_WIDTH), lambda i: (b0 + i, 0))
    gain = _resident((1, HEAD_DIM))
    outs = [pl.BlockSpec((tm, Q_WIDTH), lambda i: (i, 0)),
            pl.BlockSpec((tm, KV_WIDTH), lambda i: (i, 0)),
            pl.BlockSpec((tm, KV_WIDTH), lambda i: (i, 0))]
    shapes = [jax.ShapeDtypeStruct((t, Q_WIDTH), BF16),
              jax.ShapeDtypeStruct((t, KV_WIDTH), BF16),
              jax.ShapeDtypeStruct((t, KV_WIDTH), BF16)]
    qg, kg = q_norm.reshape(1, HEAD_DIM), k_norm.reshape(1, HEAD_DIM)
    if rope_tabs is None:
        return pl.pallas_call(
            _qkprep_ctx_kernel, grid=(t // tm,),
            in_specs=[qkv_spec, gain, gain],
            out_specs=outs + [pl.BlockSpec((tm, KV_WIDTH), lambda i: (i, 0))],
            out_shape=shapes + [jax.ShapeDtypeStruct((t, KV_WIDTH), F32)],
            compiler_params=_params(("parallel",)), name="qk_prep_ctx",
        )(proj, qg, kg)
    cos, sin = rope_tabs
    nq = DEC_SEQ // tm
    tab = pl.BlockSpec((tm, HEAD_DIM), lambda i: (i % nq, 0))
    return pl.pallas_call(
        _qkprep_lat_kernel, grid=(t // tm,),
        in_specs=[qkv_spec, gain, gain, tab, tab],
        out_specs=outs, out_shape=shapes,
        compiler_params=_params(("parallel",)), name="qk_prep_lat",
    )(proj, qg, kg, cos, sin)


def _attn_kernel(*refs, with_ctx):
    if with_ctx:
        q_ref, k_ref, v_ref, ck_ref, cv_ref, o_ref = refs
    else:
        q_ref, k_ref, v_ref, o_ref = refs
    c = HEAD_DIM ** -0.5 * math.log2(math.e)
    nt = (((1,), (1,)), ((), ()))
    q = q_ref[...]
    s = lax.dot_general(q, k_ref[...], nt, preferred_element_type=F32)
    m = jnp.max(s, axis=-1, keepdims=True)
    if with_ctx:
        sc = lax.dot_general(q, ck_ref[...], nt, preferred_element_type=F32)
        m = jnp.maximum(m, jnp.max(sc, axis=-1, keepdims=True))
    p = jnp.exp2((s - m) * c)
    l = jnp.sum(p, axis=-1, keepdims=True)
    o = jnp.dot(p.astype(BF16), v_ref[...], preferred_element_type=F32)
    if with_ctx:
        pc = jnp.exp2((sc - m) * c)
        l = l + jnp.sum(pc, axis=-1, keepdims=True)
        o = o + jnp.dot(pc.astype(BF16), cv_ref[...], preferred_element_type=F32)
    o_ref[...] = (o / l).astype(BF16)


def _attention(q, k, v, n_batch, seq, ctx_kv):
    tq = ROW_TILE
    nq = seq // tq
    t = n_batch * seq
    q_spec = pl.BlockSpec((tq, HEAD_DIM), lambda b, h, i: (b * nq + i, h))
    kv_spec = pl.BlockSpec((seq, HEAD_DIM), lambda b, h, i: (b, h // GQA_GROUP))
    in_specs = [q_spec, kv_spec, kv_spec]
    args = [q, k, v]
    if ctx_kv is not None:
        c_spec = pl.BlockSpec((PAST_LEN, HEAD_DIM), lambda b, h, i: (b, h // GQA_GROUP))
        in_specs += [c_spec, c_spec]
        args += list(ctx_kv)
    return pl.pallas_call(
        functools.partial(_attn_kernel, with_ctx=ctx_kv is not None),
        grid=(n_batch, N_HEADS, nq),
        in_specs=in_specs,
        out_specs=pl.BlockSpec((tq, HEAD_DIM), lambda b, h, i: (b * nq + i, h)),
        out_shape=jax.ShapeDtypeStruct((t, Q_WIDTH), BF16),
        compiler_params=_params(("parallel", "parallel", "parallel")),
        name="attention_lat" if ctx_kv is not None else "attention_ctx",
    )(*args)


def _outproj_kernel(ac_ref, al_ref, zc_ref, zl_ref, xc_ref, xl_ref, gate_ref, wa_ref, wz_ref, o_ref):
    acc = jnp.dot(_pick_rows(ac_ref, al_ref), wa_ref[...], preferred_element_type=F32)
    acc = acc + jnp.dot(_pick_rows(zc_ref, zl_ref), wz_ref[...], preferred_element_type=F32)
    o_ref[...] = _pick_rows(xc_ref, xl_ref) + gate_ref[0, 0] * acc


def _out_proj(attn, z, x, mods4, k_gate, wa_bf16, wz_bf16):
    tm = ROW_TILE
    return pl.pallas_call(
        _outproj_kernel,
        grid=(T_ALL // tm,),
        in_specs=(_split_row_specs(Q_WIDTH, tm) + _split_row_specs(HYENA_WIDTH, tm) + _split_row_specs(D_MODEL, tm)
                  + [_mod_spec(k_gate, tm), _resident((Q_WIDTH, D_MODEL)), _resident((HYENA_WIDTH, D_MODEL))]),
        out_specs=pl.BlockSpec((tm, D_MODEL), lambda i: (i, 0)),
        out_shape=jax.ShapeDtypeStruct((T_ALL, D_MODEL), F32),
        compiler_params=_params(("parallel",)),
        name="out_proj",
    )(*attn, *z, *x, mods4, wa_bf16, wz_bf16)


def _gelu_tanh(x):
    return 0.5 * x * (1.0 + jnp.tanh(math.sqrt(2.0 / math.pi) * (x + 0.044715 * (x * x * x))))


def _glu_kernel(yfc_ref, yfl_ref, ybc_ref, ybl_ref, u_ref, d_ref, x_ref, gate_ref, w_ref, o_ref):
    y = (_pick_rows(yfc_ref, yfl_ref) + _pick_rows(ybc_ref, ybl_ref)) + d_ref[...] * u_ref[...]
    ag = jnp.dot(_gelu_tanh(y).astype(BF16), w_ref[...], preferred_element_type=F32)
    a, g = ag[:, :D_MODEL], ag[:, D_MODEL:]
    o_ref[...] = x_ref[...] + gate_ref[0, 0] * (a * jax.nn.sigmoid(g))


def _glu_proj(y_fwd, y_bwd, u, d_skip, x, mods4, k_gate, w_bf16):
    tm = ROW_TILE
    return pl.pallas_call(
        _glu_kernel,
        grid=(T_ALL // tm,),
        in_specs=(_split_row_specs(S5_WIDTH, tm) + _split_row_specs(S5_WIDTH, tm)
                  + [pl.BlockSpec((tm, S5_WIDTH), lambda i: (i, 0)), _resident((1, S5_WIDTH)),
                     pl.BlockSpec((tm, D_MODEL), lambda i: (i, 0)), _mod_spec(k_gate, tm),
                     _resident((S5_WIDTH, 2 * D_MODEL))]),
        out_specs=pl.BlockSpec((tm, D_MODEL), lambda i: (i, 0)),
        out_shape=jax.ShapeDtypeStruct((T_ALL, D_MODEL), F32),
        compiler_params=_params(("parallel",)),
        name="glu_proj",
    )(*y_fwd, *y_bwd, u, d_skip.reshape(1, S5_WIDTH), x, mods4, w_bf16)


def _gmm_kernel(be_ref, xs_ref, w13_ref, w2_ref, o_ref, w13_b, w2_b):
    i = pl.program_id(0)
    fresh = jnp.logical_or(i == 0, be_ref[i] != be_ref[jnp.maximum(i - 1, 0)])

    @pl.when(fresh)
    def _():
        w13_b[...] = w13_ref[0].astype(BF16)
        w2_b[...] = w2_ref[0].astype(BF16)

    gu = jnp.dot(xs_ref[...], w13_b[...], preferred_element_type=F32)
    hmid = _silu(gu[:, :EXPERT_HIDDEN]) * gu[:, EXPERT_HIDDEN:]
    o_ref[...] = jnp.dot(hmid.astype(BF16), w2_b[...], preferred_element_type=F32).astype(BF16)


def _grouped_experts(blk_e, xs, w13, w2):
    n_rows = xs.shape[0]
    n_blocks = n_rows // MOE_BLOCK
    grid_spec = pltpu.PrefetchScalarGridSpec(
        num_scalar_prefetch=1,
        grid=(n_blocks,),
        in_specs=[pl.BlockSpec((MOE_BLOCK, D_MODEL), lambda i, be: (i, 0)),
                  pl.BlockSpec((1, D_MODEL, 2 * EXPERT_HIDDEN), lambda i, be: (be[i], 0, 0)),
                  pl.BlockSpec((1, EXPERT_HIDDEN, D_MODEL), lambda i, be: (be[i], 0, 0))],
        out_specs=pl.BlockSpec((MOE_BLOCK, D_MODEL), lambda i, be: (i, 0)),
        scratch_shapes=[pltpu.VMEM((D_MODEL, 2 * EXPERT_HIDDEN), BF16),
                        pltpu.VMEM((EXPERT_HIDDEN, D_MODEL), BF16)],
    )
    return pl.pallas_call(
        _gmm_kernel,
        grid_spec=grid_spec,
        out_shape=jax.ShapeDtypeStruct((n_rows, D_MODEL), BF16),
        compiler_params=_params(("arbitrary",)),
        name="grouped_experts",
    )(blk_e, xs, w13, w2)


COMBINE_TILE = 256


def _shared_kernel(h_ref, x_ref, ye_ref, w_ref, gate_ref, w13_ref, w2_ref, o_ref):
    gu = jnp.dot(h_ref[...], w13_ref[...], preferred_element_type=F32)
    hmid = _silu(gu[:, :EXPERT_HIDDEN]) * gu[:, EXPERT_HIDDEN:]
    acc = jnp.dot(hmid.astype(BF16), w2_ref[...], preferred_element_type=F32)
    wts = w_ref[...]
    for k in range(TOP_K):
        acc = acc + wts[:, k:k + 1] * ye_ref[k].astype(F32)
    o_ref[...] = x_ref[...] + gate_ref[0, 0] * acc


def _shared_and_combine(h_bf16, x, y_exp, wts, mods4, k_gate, w13_bf16, w2_bf16):
    t = x.shape[0]
    tm = COMBINE_TILE
    row = lambda i: (i, 0)
    return pl.pallas_call(
        _shared_kernel,
        grid=(t // tm,),
        in_specs=[pl.BlockSpec((tm, D_MODEL), row), pl.BlockSpec((tm, D_MODEL), row),
                  pl.BlockSpec((TOP_K, tm, D_MODEL), lambda i: (0, i, 0)),
                  pl.BlockSpec((tm, TOP_K), row), _mod_spec(k_gate, tm),
                  _resident((D_MODEL, 2 * EXPERT_HIDDEN)), _resident((EXPERT_HIDDEN, D_MODEL))],
        out_specs=pl.BlockSpec((tm, D_MODEL), row),
        out_shape=jax.ShapeDtypeStruct((t, D_MODEL), F32),
        compiler_params=_params(("parallel",)),
        name="shared_expert_combine",
    )(h_bf16, x, y_exp, wts, mods4, w13_bf16, w2_bf16)


ROUTE_TILE = 256
GROUP_SIZE = N_EXPERTS // N_EXPERT_GROUPS
NEG_INF = float("-inf")


def _first_argmax(v, index, sentinel):
    m = jnp.max(v, axis=0, keepdims=True)
    first = jnp.min(jnp.where(v == m, index, sentinel), axis=0, keepdims=True)
    return m, first


def _router_kernel(x_ref, g_ref, sc_ref, sh_ref, wt_ref, rb_ref, tri_ref,
                   h_out, idx_out, gate_out, rank_out, cnt_out, base_ref):
    tm = x_ref.shape[0]

    @pl.when(pl.program_id(0) == 0)
    def _():
        base_ref[...] = jnp.zeros_like(base_ref)

    hb = _norm_mod(x_ref[...], g_ref[...], sc_ref[0, 0], sh_ref[0, 0]).astype(BF16)
    h_out[...] = hb
    logits = lax.dot_general(wt_ref[...], hb, (((1,), (1,)), ((), ())), preferred_element_type=F32)
    scores = jax.nn.sigmoid(logits)
    choice = scores + rb_ref[...]

    member = lax.broadcasted_iota(jnp.int32, (GROUP_SIZE, tm), 0).astype(F32)
    group_scores = []
    for g in range(N_EXPERT_GROUPS):
        cg = choice[g * GROUP_SIZE:(g + 1) * GROUP_SIZE, :]
        m1, first = _first_argmax(cg, member, float(GROUP_SIZE))
        m2 = jnp.max(jnp.where(member == first, NEG_INF, cg), axis=0, keepdims=True)
        group_scores.append(m1 + m2)
    gs = jnp.concatenate(group_scores, axis=0)

    group = lax.broadcasted_iota(jnp.int32, (N_EXPERT_GROUPS, tm), 0).astype(F32)
    keep = jnp.zeros((N_EXPERT_GROUPS, tm), F32)
    for _ in range(TOPK_GROUPS):
        _, first = _first_argmax(gs, group, float(N_EXPERT_GROUPS))
        sel = group == first
        keep = jnp.where(sel, 1.0, keep)
        gs = jnp.where(sel, NEG_INF, gs)
    masked = jnp.concatenate(
        [jnp.where(keep[g:g + 1, :] > 0.0, choice[g * GROUP_SIZE:(g + 1) * GROUP_SIZE, :], NEG_INF)
         for g in range(N_EXPERT_GROUPS)], axis=0)

    expert = lax.broadcasted_iota(jnp.int32, (N_EXPERTS, tm), 0).astype(F32)
    onehot = jnp.zeros((N_EXPERTS, tm), F32)
    sels, idx_rows, gate_rows = [], [], []
    for _ in range(TOP_K):
        _, first = _first_argmax(masked, expert, float(N_EXPERTS))
        sel = expert == first
        sels.append(sel)
        idx_rows.append(first)
        gate_rows.append(jnp.sum(jnp.where(sel, scores, 0.0), axis=0, keepdims=True))
        masked = jnp.where(sel, NEG_INF, masked)
        onehot = jnp.where(sel, 1.0, onehot)
    gates = jnp.concatenate(gate_rows, axis=0)
    gate_out[...] = gates / jnp.sum(gates, axis=0, keepdims=True) * ROUTED_SCALE
    idx_out[...] = jnp.concatenate(idx_rows, axis=0).astype(jnp.int32)

    cum = jnp.dot(onehot.astype(BF16), tri_ref[...], preferred_element_type=F32) + base_ref[:, 0:1]
    rank_rows = [jnp.sum(jnp.where(sel, cum, 0.0), axis=0, keepdims=True) - 1.0 for sel in sels]
    rank_out[...] = jnp.concatenate(rank_rows, axis=0).astype(jnp.int32)
    base_ref[...] = jnp.broadcast_to(cum[:, tm - 1:tm], base_ref.shape)
    cnt_out[...] = base_ref[...].astype(jnp.int32)


def _router(x, norm_g, mods4, k_shift, k_scale, router_w, router_bias):
    t = x.shape[0]
    tm = ROUTE_TILE
    tri = (jnp.arange(tm)[:, None] <= jnp.arange(tm)[None, :]).astype(BF16)
    choice_rows = pl.BlockSpec((TOP_K, tm), lambda i: (0, i))
    return pl.pallas_call(
        _router_kernel,
        grid=(t // tm,),
        in_specs=[pl.BlockSpec((tm, D_MODEL), lambda i: (i, 0)), _resident((1, D_MODEL)),
                  _mod_spec(k_scale, tm), _mod_spec(k_shift, tm),
                  _resident((N_EXPERTS, D_MODEL)), _resident((N_EXPERTS, 1)), _resident((tm, tm))],
        out_specs=[pl.BlockSpec((tm, D_MODEL), lambda i: (i, 0)), choice_rows, choice_rows, choice_rows,
                   pl.BlockSpec((N_EXPERTS, 128), lambda i: (0, 0))],
        out_shape=[jax.ShapeDtypeStruct((t, D_MODEL), BF16),
                   jax.ShapeDtypeStruct((TOP_K, t), jnp.int32), jax.ShapeDtypeStruct((TOP_K, t), F32),
                   jax.ShapeDtypeStruct((TOP_K, t), jnp.int32), jax.ShapeDtypeStruct((N_EXPERTS, 128), jnp.int32)],
        scratch_shapes=[pltpu.VMEM((N_EXPERTS, 128), F32)],
        compiler_params=_params(("arbitrary",)),
        name="router",
    )(x, norm_g.reshape(1, D_MODEL), mods4, mods4, router_w.T.astype(BF16), router_bias.reshape(N_EXPERTS, 1), tri)


def _pos_kernel(idx_ref, rank_ref, off_ref, pos_out):
    tm = idx_ref.shape[1]
    expert = lax.broadcasted_iota(jnp.int32, (N_EXPERTS, tm), 0)
    rows = []
    for k in range(TOP_K):
        sel = expert == idx_ref[k:k + 1, :]
        rows.append(jnp.sum(jnp.where(sel, off_ref[...], 0.0), axis=0, keepdims=True))
    pos_out[...] = jnp.concatenate(rows, axis=0).astype(jnp.int32) + rank_ref[...]


def _positions(idx, rank, offsets):
    t = idx.shape[1]
    tm = 1024
    rows = pl.BlockSpec((TOP_K, tm), lambda i: (0, i))
    return pl.pallas_call(
        _pos_kernel, grid=(t // tm,),
        in_specs=[rows, rows, _resident((N_EXPERTS, 1))],
        out_specs=rows, out_shape=jax.ShapeDtypeStruct((TOP_K, t), jnp.int32),
        compiler_params=_params(("parallel",)), name="expert_positions",
    )(idx, rank, offsets.astype(F32).reshape(N_EXPERTS, 1))


def _moe(x, norm_g, mods4, k_shift, k_scale, k_gate, router_w, router_bias, exp_w13, exp_w2, sh_w13, sh_w2):
    t = x.shape[0]
    h_bf16, top_idx, gate, rank, counts = _router(x, norm_g, mods4, k_shift, k_scale, router_w, router_bias)
    counts = counts[:, 0]
    padded = (counts + MOE_BLOCK - 1) // MOE_BLOCK * MOE_BLOCK
    e_iota = jnp.arange(N_EXPERTS)
    pend = jnp.sum(jnp.where(e_iota[:, None] <= e_iota[None, :], padded[:, None], 0), axis=0)
    pos = _positions(top_idx, rank, pend - padded)
    n_rows = -(-(t * TOP_K + N_EXPERTS * (MOE_BLOCK - 1)) // MOE_BLOCK) * MOE_BLOCK
    n_blocks = n_rows // MOE_BLOCK
    tok = jnp.broadcast_to(jnp.arange(t, dtype=jnp.int32)[None, :], (TOP_K, t))
    row_tok = (jnp.arange(n_rows, dtype=jnp.int32) % t).at[pos.reshape(-1)].set(tok.reshape(-1), unique_indices=True)
    blk_start = jnp.arange(n_blocks, dtype=pend.dtype) * MOE_BLOCK
    blk_e = jnp.minimum(jnp.sum((pend[None, :] <= blk_start[:, None]).astype(jnp.int32), axis=1), N_EXPERTS - 1)
    xs = h_bf16[row_tok]
    ys = _grouped_experts(blk_e, xs, exp_w13, exp_w2)
    y_exp = ys[pos.reshape(-1)].reshape(TOP_K, t, D_MODEL)
    return _shared_and_combine(h_bf16, x, y_exp, gate.T, mods4, k_gate, sh_w13.astype(BF16), sh_w2.astype(BF16))


HY_BLOCK_CTX, HY_TC_CTX = 256, 512
HY_BLOCK_LAT, HY_TC_LAT = 512, 128


def _hyena_filters(L, w1, b1, w2, b2, w3, sin_freq):
    t_norm = jnp.linspace(0.0, 1.0, L, dtype=F32)[:, None]
    omega = 2.0 * math.pi * jnp.arange(L, dtype=F32)[:, None] / L
    bands = jnp.linspace(1e-4, FILTER_BANDS - 1, FILTER_BANDS, dtype=F32)[None, :]
    z = jnp.concatenate([t_norm, jnp.cos(bands * omega), -jnp.sin(bands * omega)], axis=-1)
    h = jnp.sin(sin_freq * (z @ w1 + b1))
    h = jnp.sin(sin_freq * (h @ w2 + b2))
    h = (h @ w3).reshape(L, 2, HYENA_ORDER, HYENA_WIDTH)
    deltas = jnp.abs(jnp.linspace(math.log(DECAY_TARGET) / DECAY_PCT_LONG,
                                  math.log(DECAY_TARGET) / DECAY_PCT_SHORT, HYENA_WIDTH, dtype=F32))
    h = h * jnp.exp(-t_norm * deltas)[:, None, None, :]
    fwd, bwd = h[:, 0], h[:, 1]
    filt = jnp.concatenate([fwd, jnp.zeros_like(fwd[:1]), bwd[:0:-1]], axis=0)
    return filt / jnp.sum(jnp.abs(filt), axis=0, keepdims=True)


HY_ACC_ELEMS = 8192


def _dft_matrices(bk):
    k = jnp.arange(bk, dtype=jnp.int32)
    phase = ((2 * k[:, None] + 1) * k[None, :]) % (4 * bk)
    ang = phase.astype(F32) * (math.pi / (2 * bk))
    return jnp.cos(ang), jnp.sin(ang)


def _filter_spectra(filt, bk):
    L = filt.shape[0] // 2
    nb = L // bk
    g = jnp.concatenate([filt[L:], filt[:L]], axis=0)
    seg = g.reshape(2 * nb, bk, HYENA_ORDER, HYENA_WIDTH)
    cur, prev = seg[1:], seg[:-1]
    cos, sin = _dft_matrices(bk)
    k = jnp.arange(bk, dtype=jnp.int32)
    phase = ((2 * k[:, None] + 1) * (bk - k[None, :])) % (4 * bk)
    ang = phase.astype(F32) * (math.pi / (2 * bk))
    live = (k[None, :] > 0).astype(F32)
    cos_neg, sin_neg = jnp.cos(ang) * live, jnp.sin(ang) * live
    hi = lax.Precision.HIGHEST
    ein = lambda m, a: jnp.einsum('kj,djow->odkw', m, a, precision=hi)
    re = ein(cos, cur) + ein(cos_neg, prev)
    im = ein(sin_neg, prev) - ein(sin, cur)
    return jnp.stack([re, im], axis=2)


def _hyena_kernel(v_ref, x1_ref, x2_ref, cw_ref, cb_ref, hb_ref, k0_ref, k1_ref, c_ref, s_ref, ct_ref, st_ref,
                  o_ref, vs_ref, z1_ref, x2s_ref, ur_ref, ui_ref, yr_ref, yi_ref, *, seq, bk, tc):
    nb = seq // bk
    ft = HY_ACC_ELEMS // tc
    row = lax.broadcasted_iota(jnp.int32, (seq, tc), 0)

    def short_conv(x, j):
        prev = jnp.where(row == 0, 0.0, pltpu.roll(x, 1, 0))
        nxt = jnp.where(row == seq - 1, 0.0, pltpu.roll(x, seq - 1, 0))
        return (prev * cw_ref[0, j:j + 1, :] + x * cw_ref[1, j:j + 1, :] + nxt * cw_ref[2, j:j + 1, :]
                + cb_ref[j:j + 1, :])

    vs_ref[...] = short_conv(v_ref[...], 0)
    z1_ref[...] = short_conv(x1_ref[...], 1)
    x2s_ref[...] = short_conv(x2_ref[...], 2)

    def long_conv(src_ref, k_ref, emit):
        for blk in range(nb):
            ub = src_ref[blk * bk:(blk + 1) * bk, :].astype(BF16)
            ur_ref[blk] = jnp.dot(c_ref[...], ub, preferred_element_type=F32)
            ui_ref[blk] = -jnp.dot(s_ref[...], ub, preferred_element_type=F32)
        for out_blk in range(nb):
            def acc_tile(f, carry, out_blk=out_blk):
                rows = pl.ds(pl.multiple_of(f * ft, ft), ft)
                ar = jnp.zeros((ft, tc), F32)
                ai = jnp.zeros((ft, tc), F32)
                for in_blk in range(nb):
                    d = out_blk - in_blk + nb - 1
                    kr, ki = k_ref[d, 0, rows, :], k_ref[d, 1, rows, :]
                    xr, xi = ur_ref[in_blk, rows, :], ui_ref[in_blk, rows, :]
                    ar = ar + (kr * xr - ki * xi)
                    ai = ai + (kr * xi + ki * xr)
                yr_ref[rows, :] = ar
                yi_ref[rows, :] = ai
                return carry

            lax.fori_loop(0, bk // ft, acc_tile, 0)
            y = jnp.dot(ct_ref[...], yr_ref[...].astype(BF16), preferred_element_type=F32)
            y = y - jnp.dot(st_ref[...], yi_ref[...].astype(BF16), preferred_element_type=F32)
            emit(slice(out_blk * bk, (out_blk + 1) * bk), y * (1.0 / bk))

    def emit_z1(rows, y):
        z1_ref[rows, :] = z1_ref[rows, :] * (y + hb_ref[0:1, :] * vs_ref[rows, :])

    def emit_out(rows, y):
        o_ref[rows, :] = (x2s_ref[rows, :] * (y + hb_ref[1:2, :] * z1_ref[rows, :])).astype(BF16)

    long_conv(vs_ref, k0_ref, emit_z1)
    long_conv(z1_ref, k1_ref, emit_out)


def _hyena(proj, row0, n_batch, seq, bk, tc, conv_w, conv_b, hy_bias, spectra, mats):
    nb = seq // bk
    n_ct = HYENA_WIDTH // tc
    col0 = QKV_WIDTH // tc
    per = HYENA_WIDTH // tc
    b0 = row0 // seq

    def data(which):
        return pl.BlockSpec((seq, tc), lambda ct, b: (b0 + b, col0 + which * per + ct))

    const2 = lambda shape: pl.BlockSpec(shape, lambda ct, b: (0, 0), pipeline_mode=pl.Buffered(1))
    kspec = pl.BlockSpec((2 * nb - 1, 2, bk, tc), lambda ct, b: (0, 0, 0, ct), pipeline_mode=pl.Buffered(1))
    cw = conv_w.reshape(3, HYENA_ORDER + 1, HYENA_WIDTH)
    cb = conv_b.reshape(HYENA_ORDER + 1, HYENA_WIDTH)
    return pl.pallas_call(
        functools.partial(_hyena_kernel, seq=seq, bk=bk, tc=tc),
        grid=(n_ct, n_batch),
        in_specs=[data(0), data(1), data(2),
                  pl.BlockSpec((3, HYENA_ORDER + 1, tc), lambda ct, b: (0, 0, ct)),
                  pl.BlockSpec((HYENA_ORDER + 1, tc), lambda ct, b: (0, ct)),
                  pl.BlockSpec((HYENA_ORDER, tc), lambda ct, b: (0, ct)),
                  kspec, kspec, const2((bk, bk)), const2((bk, bk)), const2((bk, bk)), const2((bk, bk))],
        out_specs=pl.BlockSpec((seq, tc), lambda ct, b: (b, ct)),
        out_shape=jax.ShapeDtypeStruct((n_batch * seq, HYENA_WIDTH), BF16),
        scratch_shapes=[pltpu.VMEM((seq, tc), F32), pltpu.VMEM((seq, tc), F32), pltpu.VMEM((seq, tc), F32),
                        pltpu.VMEM((nb, bk, tc), F32), pltpu.VMEM((nb, bk, tc), F32),
                        pltpu.VMEM((bk, tc), F32), pltpu.VMEM((bk, tc), F32)],
        compiler_params=_params(("parallel", "parallel")),
        name="hyena_seq%d" % seq,
    )(proj, proj, proj, cw, cb, hy_bias, spectra[0], spectra[1], *mats)


def _hyena_operands(L, bk, f_w1, f_b1, f_w2, f_b2, f_w3, sin_freq):
    filt = _hyena_filters(L, f_w1, f_b1, f_w2, f_b2, f_w3, sin_freq)
    cos, sin = _dft_matrices(bk)
    mats = tuple(m.astype(BF16) for m in (cos, sin, cos.T, sin.T))
    return _filter_spectra(filt, bk), mats


S5_LANES = S5_GROUPS * S5_STATE
S5_CLUSTER_GROUPS = 16
S5_CLUSTERS = S5_GROUPS // S5_CLUSTER_GROUPS
S5_CLUSTER_CH = S5_CLUSTER_GROUPS * S5_GROUP_CH
S5_CLUSTER_LANES = S5_CLUSTER_GROUPS * S5_STATE
SCAN_CHUNK = 256
SCAN_TILE = 8
SCAN_LEVELS = (1, 2, 4)


def _s5_kernel(u_ref, s0r_ref, s0i_ref, bw_ref, cwr_ref, cwi_ref, tab_ref, y_ref, sfr_ref, sfi_ref,
               sre, sim, car, cai, *, reverse):
    @pl.when(pl.program_id(1) == 0)
    def _():
        car[...] = s0r_ref[0]
        cai[...] = s0i_ref[0]

    ub = u_ref[...].astype(BF16)
    for k in range(S5_CLUSTERS):
        bu = jnp.dot(ub[:, k * S5_CLUSTER_CH:(k + 1) * S5_CLUSTER_CH], bw_ref[k], preferred_element_type=F32)
        sre[:, k * S5_CLUSTER_LANES:(k + 1) * S5_CLUSTER_LANES] = bu[:, :S5_CLUSTER_LANES]
        sim[:, k * S5_CLUSTER_LANES:(k + 1) * S5_CLUSTER_LANES] = bu[:, S5_CLUSTER_LANES:]

    n_tiles = SCAN_CHUNK // SCAN_TILE
    shifts = tuple(SCAN_TILE - k for k in SCAN_LEVELS) if reverse else SCAN_LEVELS
    boundary_row = 0 if reverse else SCAN_TILE - 1
    for k in range(S5_CLUSTERS):
        lanes = slice(k * S5_CLUSTER_LANES, (k + 1) * S5_CLUSTER_LANES)

        def tile_step(i, carry, lanes=lanes):
            cr, ci = carry
            tile = (n_tiles - 1 - i) if reverse else i
            rows = pl.ds(pl.multiple_of(tile * SCAN_TILE, SCAN_TILE), SCAN_TILE)
            xr, xi = sre[rows, lanes], sim[rows, lanes]
            for lvl, sh in enumerate(shifts):
                pr, pi = tab_ref[2 * lvl, :, lanes], tab_ref[2 * lvl + 1, :, lanes]
                rr, ri = pltpu.roll(xr, sh, 0), pltpu.roll(xi, sh, 0)
                xr, xi = xr + (pr * rr - pi * ri), xi + (pr * ri + pi * rr)
            pr, pi = tab_ref[2 * len(shifts), :, lanes], tab_ref[2 * len(shifts) + 1, :, lanes]
            xr, xi = xr + (pr * cr - pi * ci), xi + (pr * ci + pi * cr)
            sre[rows, lanes] = xr
            sim[rows, lanes] = xi
            return xr[boundary_row:boundary_row + 1, :], xi[boundary_row:boundary_row + 1, :]

        cr, ci = lax.fori_loop(0, n_tiles, tile_step, (car[:, lanes], cai[:, lanes]))
        car[:, lanes] = cr
        cai[:, lanes] = ci

    for k in range(S5_CLUSTERS):
        lanes = slice(k * S5_CLUSTER_LANES, (k + 1) * S5_CLUSTER_LANES)
        yk = jnp.dot(sre[:, lanes].astype(BF16), cwr_ref[k], preferred_element_type=F32)
        yk = yk + jnp.dot(sim[:, lanes].astype(BF16), cwi_ref[k], preferred_element_type=F32)
        y_ref[:, k * S5_CLUSTER_CH:(k + 1) * S5_CLUSTER_CH] = yk
    sfr_ref[0] = car[...]
    sfi_ref[0] = cai[...]


def _s5_direction(u, row0, n_seq, seq, s0_re, s0_im, bw, cwr, cwi, tabs, reverse):
    nc = seq // SCAN_CHUNK
    b0 = row0 // SCAN_CHUNK

    def chunk(c):
        return (nc - 1 - c) if reverse else c

    state_spec = pl.BlockSpec((1, 1, S5_LANES), lambda b, c: (b, 0, 0))
    full3 = lambda shape: pl.BlockSpec(shape, lambda b, c: (0, 0, 0), pipeline_mode=pl.Buffered(1))
    return pl.pallas_call(
        functools.partial(_s5_kernel, reverse=reverse),
        grid=(n_seq, nc),
        in_specs=[pl.BlockSpec((SCAN_CHUNK, S5_WIDTH), lambda b, c: (b0 + b * nc + chunk(c), 0)),
                  state_spec, state_spec,
                  full3(bw.shape), full3(cwr.shape), full3(cwi.shape), full3(tabs.shape)],
        out_specs=[pl.BlockSpec((SCAN_CHUNK, S5_WIDTH), lambda b, c: (b * nc + chunk(c), 0)),
                   state_spec, state_spec],
        out_shape=[jax.ShapeDtypeStruct((n_seq * seq, S5_WIDTH), F32),
                   jax.ShapeDtypeStruct((n_seq, 1, S5_LANES), F32),
                   jax.ShapeDtypeStruct((n_seq, 1, S5_LANES), F32)],
        scratch_shapes=[pltpu.VMEM((SCAN_CHUNK, S5_LANES), F32), pltpu.VMEM((SCAN_CHUNK, S5_LANES), F32),
                        pltpu.VMEM((1, S5_LANES), F32), pltpu.VMEM((1, S5_LANES), F32)],
        compiler_params=_params(("parallel", "arbitrary")),
        name="s5_bwd" if reverse else "s5_fwd",
    )(u, s0_re, s0_im, bw, cwr, cwi, tabs)


def _s5_weights(a_re, a_im, log_dt, b_re, b_im, c_re, c_im):
    lam = lax.complex(a_re, a_im)
    lam_bar = jnp.exp(lam * jnp.exp(log_dt)[..., None])
    b_bar = ((lam_bar - 1.0) / lam)[..., None] * lax.complex(b_re, b_im)
    eye = jnp.eye(S5_CLUSTER_GROUPS, dtype=F32)
    out = []
    for d in range(2):
        def cluster_in(w):
            w = w.reshape(S5_CLUSTERS, S5_CLUSTER_GROUPS, S5_STATE, S5_GROUP_CH)
            return jnp.einsum('ab,kapc->kacbp', eye, w).reshape(S5_CLUSTERS, S5_CLUSTER_CH, S5_CLUSTER_LANES)

        def cluster_out(w):
            w = w.reshape(S5_CLUSTERS, S5_CLUSTER_GROUPS, S5_GROUP_CH, S5_STATE)
            return jnp.einsum('ab,kacp->kapbc', eye, w).reshape(S5_CLUSTERS, S5_CLUSTER_LANES, S5_CLUSTER_CH)

        bw = jnp.concatenate([cluster_in(jnp.real(b_bar[d])), cluster_in(jnp.imag(b_bar[d]))], axis=-1)
        cwr = cluster_out(c_re[d])
        cwi = cluster_out(-c_im[d])
        lam_d = lam_bar[d].reshape(S5_LANES)
        pows = [lam_d]
        for _ in range(SCAN_TILE - 1):
            pows.append(pows[-1] * lam_d)
        pows = jnp.stack(pows)
        j = jnp.arange(SCAN_TILE)
        tabs = []
        for k in SCAN_LEVELS:
            valid = (j <= SCAN_TILE - 1 - k) if d == 1 else (j >= k)
            p = jnp.where(valid[:, None], pows[k - 1][None, :], 0.0)
            tabs += [jnp.real(p), jnp.imag(p)]
        pc = pows[::-1] if d == 1 else pows
        tabs += [jnp.real(pc), jnp.imag(pc)]
        out.append((bw.astype(BF16), cwr.astype(BF16), cwi.astype(BF16), jnp.stack(tabs).astype(F32)))
    return out


def _final_norm_kernel(x_ref, g_ref, o_ref):
    x = x_ref[...]
    ms = jnp.mean(x * x, axis=-1, keepdims=True)
    o_ref[...] = x * lax.rsqrt(ms + RMS_EPS) * g_ref[...]


def _final_norm(x, g, row0, t):
    tm = ROW_TILE
    b0 = row0 // tm
    return pl.pallas_call(
        _final_norm_kernel,
        grid=(t // tm,),
        in_specs=[pl.BlockSpec((tm, D_MODEL), lambda i: (b0 + i, 0)), _resident((1, D_MODEL))],
        out_specs=pl.BlockSpec((tm, D_MODEL), lambda i: (i, 0)),
        out_shape=jax.ShapeDtypeStruct((t, D_MODEL), F32),
        compiler_params=_params(("parallel",)),
        name="final_norm",
    )(x, g.reshape(1, D_MODEL))


def _rope_tables(L):
    n_rows = L // GRID_W
    row_idx = jnp.repeat(jnp.arange(n_rows, dtype=F32), GRID_W)
    col_idx = (jnp.arange(L) % GRID_W).astype(F32)
    inv = ROPE_THETA ** (-jnp.arange(0, ROPE_AXIS_DIM, 2, dtype=F32) / ROPE_AXIS_DIM)
    ang = jnp.concatenate([row_idx[:, None] * inv, col_idx[:, None] * inv], axis=-1)
    cos, sin = jnp.cos(ang), jnp.sin(ang)
    cos_full = jnp.repeat(cos, 2, axis=-1)
    sin_signed = jnp.stack([-sin, sin], axis=-1).reshape(L, HEAD_DIM)
    return cos_full, sin_signed


def kernel(x_prompt, x_sample, cache_k_l0, cache_v_l0, state_s5_re_l1, state_s5_im_l1, c, c_ctx,
           ada_w_l0, ada_b_l0, norm1_l0, norm2_l0,
           w_in_l0, w_out_l0, q_norm_l0, k_norm_l0, hy_conv_w_l0, hy_conv_b_l0,
           hy_ffn_w1_l0, hy_ffn_b1_l0, hy_ffn_w2_l0, hy_ffn_b2_l0, hy_ffn_w3_l0, hy_sin_freq_l0, hy_bias_l0,
           router_l0, router_bias_l0, exp_w13_l0, exp_w2_l0, shared_w13_l0, shared_w2_l0,
           ada_w_l1, ada_b_l1, norm1_l1, norm2_l1,
           w_in_l1, s5_a_re_l1, s5_a_im_l1, s5_log_dt_l1, s5_b_re_l1, s5_b_im_l1,
           s5_c_re_l1, s5_c_im_l1, s5_d_l1, w_glu_l1,
           router_l1, router_bias_l1, exp_w13_l1, exp_w2_l1, shared_w13_l1, shared_w2_l1,
           final_norm):
    x = (x_prompt.reshape(T_CTX, D_MODEL), x_sample.reshape(T_LAT, D_MODEL))
    cond = jnp.concatenate([c_ctx[None, :], c, jnp.zeros((N_COND - 1 - DEC_BATCH, D_MODEL), F32)], axis=0)

    mods = _ada_mods(cond, ada_w_l0, ada_b_l0)
    proj = _norm_mod_matmul(x, norm1_l0, mods, 0, 1, w_in_l0.astype(BF16))

    q_c, k_c, v_c, new_k = _qk_prep(proj, 0, T_CTX, q_norm_l0, k_norm_l0, None)
    attn_c = _attention(q_c, k_c, v_c, BATCH, SEQ, None)
    q_l, k_l, v_l = _qk_prep(proj, T_CTX, T_LAT, q_norm_l0, k_norm_l0, _rope_tables(DEC_SEQ))
    ctx_kv = (cache_k_l0.reshape(DEC_BATCH * PAST_LEN, KV_WIDTH).astype(BF16),
              cache_v_l0.reshape(DEC_BATCH * PAST_LEN, KV_WIDTH).astype(BF16))
    attn_l = _attention(q_l, k_l, v_l, DEC_BATCH, DEC_SEQ, ctx_kv)

    hy_f = (hy_ffn_w1_l0, hy_ffn_b1_l0, hy_ffn_w2_l0, hy_ffn_b2_l0, hy_ffn_w3_l0, hy_sin_freq_l0)
    spec_c, mats_c = _hyena_operands(SEQ, HY_BLOCK_CTX, *hy_f)
    z_c = _hyena(proj, 0, BATCH, SEQ, HY_BLOCK_CTX, HY_TC_CTX, hy_conv_w_l0, hy_conv_b_l0, hy_bias_l0,
                 spec_c, mats_c)
    spec_l, mats_l = _hyena_operands(DEC_SEQ, HY_BLOCK_LAT, *hy_f)
    z_l = _hyena(proj, T_CTX, DEC_BATCH, DEC_SEQ, HY_BLOCK_LAT, HY_TC_LAT, hy_conv_w_l0, hy_conv_b_l0, hy_bias_l0,
                 spec_l, mats_l)

    w_out = w_out_l0.astype(BF16)
    x = _out_proj((attn_c, attn_l), (z_c, z_l), x, mods, 2, w_out[:Q_WIDTH], w_out[Q_WIDTH:])
    x = _moe(x, norm2_l0, mods, 3, 4, 5, router_l0, router_bias_l0, exp_w13_l0, exp_w2_l0,
             shared_w13_l0, shared_w2_l0)

    new_v = proj[:T_CTX, Q_WIDTH + KV_WIDTH:QKV_WIDTH]

    mods = _ada_mods(cond, ada_w_l1, ada_b_l1)
    u = _norm_mod_matmul(x, norm1_l1, mods, 0, 1, w_in_l1.astype(BF16))
    s5_w = _s5_weights(s5_a_re_l1, s5_a_im_l1, s5_log_dt_l1, s5_b_re_l1, s5_b_im_l1, s5_c_re_l1, s5_c_im_l1)
    zero_state = jnp.zeros((BATCH, 1, S5_LANES), F32)
    ys, finals = [], []
    for d in range(2):
        y_c, f_re, f_im = _s5_direction(u, 0, BATCH, SEQ, zero_state, zero_state, *s5_w[d], reverse=d == 1)
        y_l, _, _ = _s5_direction(u, T_CTX, DEC_BATCH, DEC_SEQ,
                                  state_s5_re_l1[:, d].reshape(DEC_BATCH, 1, S5_LANES),
                                  state_s5_im_l1[:, d].reshape(DEC_BATCH, 1, S5_LANES), *s5_w[d], reverse=d == 1)
        ys.append((y_c, y_l))
        finals.append((f_re.reshape(BATCH, S5_GROUPS, S5_STATE), f_im.reshape(BATCH, S5_GROUPS, S5_STATE)))
    s_re = jnp.stack([finals[0][0], finals[1][0]], axis=1)
    s_im = jnp.stack([finals[0][1], finals[1][1]], axis=1)
    x = _glu_proj(ys[0], ys[1], u, s5_d_l1, x, mods, 2, w_glu_l1.astype(BF16))
    x = _moe(x, norm2_l1, mods, 3, 4, 5, router_l1, router_bias_l1, exp_w13_l1, exp_w2_l1,
             shared_w13_l1, shared_w2_l1)

    return (_final_norm(x, final_norm, 0, T_CTX).reshape(BATCH, SEQ, D_MODEL),
            _final_norm(x, final_norm, T_CTX, T_LAT).reshape(DEC_BATCH, DEC_SEQ, D_MODEL),
            new_k.reshape(BATCH, SEQ, N_KV_HEADS, HEAD_DIM),
            new_v.reshape(BATCH, SEQ, N_KV_HEADS, HEAD_DIM),
            s_re, s_im)
```

```python
import functools
import math

import jax
import jax.numpy as jnp
from jax import lax
from jax.experimental import pallas as pl
from jax.experimental.pallas import tpu as pltpu

F32 = jnp.float32
BF16 = jnp.bfloat16

D_MODEL = 2048
BATCH = 32
SEQ = 256
DEC_BATCH = 4
DEC_SEQ = 4096
PAST_LEN = 256
GRID_W = 64
RMS_EPS = 1e-6
HEAD_DIM = 128
N_HEADS = 8
N_KV_HEADS = 2
GQA_GROUP = N_HEADS // N_KV_HEADS
Q_WIDTH = N_HEADS * HEAD_DIM
KV_WIDTH = N_KV_HEADS * HEAD_DIM
QKV_WIDTH = Q_WIDTH + 2 * KV_WIDTH
ROPE_THETA = 10000.0
ROPE_AXIS_DIM = HEAD_DIM // 2
HYENA_WIDTH = D_MODEL // 2
HYENA_ORDER = 2
FILTER_BANDS = 16
DECAY_TARGET = 1e-2
DECAY_PCT_SHORT = 0.3
DECAY_PCT_LONG = 1.5
S5_WIDTH = D_MODEL // 2
S5_GROUP_CH = 16
S5_GROUPS = S5_WIDTH // S5_GROUP_CH
S5_STATE = 64
N_EXPERTS = 64
TOP_K = 8
N_EXPERT_GROUPS = 8
TOPK_GROUPS = 4
EXPERT_HIDDEN = 512
ROUTED_SCALE = 2.5
MOE_BLOCK = 256

T_CTX = BATCH * SEQ
T_LAT = DEC_BATCH * DEC_SEQ
T_ALL = T_CTX + T_LAT
N_COND = 8

ROW_TILE = 256
V7X_VMEM_LIMIT_BYTES = 56 * 1024 * 1024


def _params(semantics):
    return pltpu.CompilerParams(dimension_semantics=semantics, vmem_limit_bytes=V7X_VMEM_LIMIT_BYTES)


def _cond_row(i, tm):
    tok = i * tm
    return jnp.where(tok < T_CTX, 0, 1 + (tok - T_CTX) // DEC_SEQ)


def _mod_spec(k, tm):
    return pl.BlockSpec((1, 1, 1, D_MODEL), lambda i: (_cond_row(i, tm), k, 0, 0))


def _resident(shape):
    nd = len(shape)
    return pl.BlockSpec(shape, lambda i: (0,) * nd, pipeline_mode=pl.Buffered(1))


def _silu(x):
    return x * jax.nn.sigmoid(x)


def _norm_mod(x, g, sc, sh):
    ms = jnp.mean(x * x, axis=-1, keepdims=True)
    return (x * lax.rsqrt(ms + RMS_EPS) * g) * (1.0 + sc) + sh


def _ada_kernel(c_ref, w_ref, b_ref, o_ref):
    a = _silu(c_ref[...]).astype(BF16)
    o_ref[...] = jnp.dot(a, w_ref[...].astype(BF16), preferred_element_type=F32) + b_ref[...]


def _ada_mods(cond, w, b):
    n = w.shape[1]
    tn = 1024
    out = pl.pallas_call(
        _ada_kernel,
        grid=(n // tn,),
        in_specs=[pl.BlockSpec((N_COND, D_MODEL), lambda j: (0, 0)),
                  pl.BlockSpec((D_MODEL, tn), lambda j: (0, j)),
                  pl.BlockSpec((1, tn), lambda j: (0, j))],
        out_specs=pl.BlockSpec((N_COND, tn), lambda j: (0, j)),
        out_shape=jax.ShapeDtypeStruct((N_COND, n), F32),
        compiler_params=_params(("parallel",)),
        name="ada_mods",
    )(cond, w, b.reshape(1, n))
    return out.reshape(N_COND, 6, 1, D_MODEL)


def _split_row_specs(width, tm):
    n_ctx = T_CTX // tm
    return [pl.BlockSpec((tm, width), lambda i: (jnp.minimum(i, n_ctx - 1), 0)),
            pl.BlockSpec((tm, width), lambda i: (jnp.maximum(i - n_ctx, 0), 0))]


def _pick_rows(ctx_ref, lat_ref):
    return jnp.where(pl.program_id(0) < T_CTX // ctx_ref.shape[0], ctx_ref[...], lat_ref[...])


def _nmm_kernel(*refs, split):
    if split:
        xc_ref, xl_ref, g_ref, sc_ref, sh_ref, w_ref, o_ref = refs
        x = _pick_rows(xc_ref, xl_ref)
    else:
        x_ref, g_ref, sc_ref, sh_ref, w_ref, o_ref = refs
        x = x_ref[...]
    h = _norm_mod(x, g_ref[...], sc_ref[0, 0], sh_ref[0, 0])
    o_ref[...] = jnp.dot(h.astype(BF16), w_ref[...], preferred_element_type=F32)


def _norm_mod_matmul(x, g, mods4, k_shift, k_scale, w_bf16):
    n = w_bf16.shape[1]
    tm = ROW_TILE
    split = isinstance(x, tuple)
    x_args = list(x) if split else [x]
    x_specs = _split_row_specs(D_MODEL, tm) if split else [pl.BlockSpec((tm, D_MODEL), lambda i: (i, 0))]
    return pl.pallas_call(
        functools.partial(_nmm_kernel, split=split),
        grid=(T_ALL // tm,),
        in_specs=x_specs + [_resident((1, D_MODEL)), _mod_spec(k_scale, tm), _mod_spec(k_shift, tm),
                            _resident((D_MODEL, n))],
        out_specs=pl.BlockSpec((tm, n), lambda i: (i, 0)),
        out_shape=jax.ShapeDtypeStruct((T_ALL, n), F32),
        compiler_params=_params(("parallel",)),
        name="norm_mod_matmul",
    )(*x_args, g.reshape(1, D_MODEL), mods4, mods4, w_bf16)


def _head_norm(xh, g):
    ms = jnp.mean(xh * xh, axis=-1, keepdims=True)
    return xh * lax.rsqrt(ms + RMS_EPS) * g


def _rope(y, cos, sin_signed):
    lane = lax.broadcasted_iota(jnp.int32, y.shape, 1)
    partner = jnp.where(lane % 2 == 0, pltpu.roll(y, HEAD_DIM - 1, 1), pltpu.roll(y, 1, 1))
    return y * cos + partner * sin_signed


def _qkprep_ctx_kernel(qkv_ref, qg_ref, kg_ref, q_out, kb_out, vb_out, kf_out):
    for h in range(N_HEADS):
        sl = slice(h * HEAD_DIM, (h + 1) * HEAD_DIM)
        q_out[:, sl] = _head_norm(qkv_ref[:, sl], qg_ref[...]).astype(BF16)
    for j in range(N_KV_HEADS):
        k = _head_norm(qkv_ref[:, Q_WIDTH + j * HEAD_DIM:Q_WIDTH + (j + 1) * HEAD_DIM], kg_ref[...])
        kf_out[:, j * HEAD_DIM:(j + 1) * HEAD_DIM] = k
        kb_out[:, j * HEAD_DIM:(j + 1) * HEAD_DIM] = k.astype(BF16)
    vb_out[...] = qkv_ref[:, Q_WIDTH + KV_WIDTH:QKV_WIDTH].astype(BF16)


def _qkprep_lat_kernel(qkv_ref, qg_ref, kg_ref, cos_ref, sin_ref, q_out, kb_out, vb_out):
    cos, sin = cos_ref[...], sin_ref[...]
    for h in range(N_HEADS):
        sl = slice(h * HEAD_DIM, (h + 1) * HEAD_DIM)
        q_out[:, sl] = _rope(_head_norm(qkv_ref[:, sl], qg_ref[...]), cos, sin).astype(BF16)
    for j in range(N_KV_HEADS):
        k = _head_norm(qkv_ref[:, Q_WIDTH + j * HEAD_DIM:Q_WIDTH + (j + 1) * HEAD_DIM], kg_ref[...])
        kb_out[:, j * HEAD_DIM:(j + 1) * HEAD_DIM] = _rope(k, cos, sin).astype(BF16)
    vb_out[...] = qkv_ref[:, Q_WIDTH + KV_WIDTH:QKV_WIDTH].astype(BF16)


def _qk_prep(proj, row0, t, q_norm, k_norm, rope_tabs):
    tm = ROW_TILE
    b0 = row0 // tm
    qkv_spec = pl.BlockSpec((tm, QKV_WIDTH), lambda i: (b0 + i, 0))
    gain = _resident((1, HEAD_DIM))
    outs = [pl.BlockSpec((tm, Q_WIDTH), lambda i: (i, 0)),
            pl.BlockSpec((tm, KV_WIDTH), lambda i: (i, 0)),
            pl.BlockSpec((tm, KV_WIDTH), lambda i: (i, 0))]
    shapes = [jax.ShapeDtypeStruct((t, Q_WIDTH), BF16),
              jax.ShapeDtypeStruct((t, KV_WIDTH), BF16),
              jax.ShapeDtypeStruct((t, KV_WIDTH), BF16)]
    qg, kg = q_norm.reshape(1, HEAD_DIM), k_norm.reshape(1, HEAD_DIM)
    if rope_tabs is None:
        return pl.pallas_call(
            _qkprep_ctx_kernel, grid=(t // tm,),
            in_specs=[qkv_spec, gain, gain],
            out_specs=outs + [pl.BlockSpec((tm, KV_WIDTH), lambda i: (i, 0))],
            out_shape=shapes + [jax.ShapeDtypeStruct((t, KV_WIDTH), F32)],
            compiler_params=_params(("parallel",)), name="qk_prep_ctx",
        )(proj, qg, kg)
    cos, sin = rope_tabs
    nq = DEC_SEQ // tm
    tab = pl.BlockSpec((tm, HEAD_DIM), lambda i: (i % nq, 0))
    return pl.pallas_call(
        _qkprep_lat_kernel, grid=(t // tm,),
        in_specs=[qkv_spec, gain, gain, tab, tab],
        out_specs=outs, out_shape=shapes,
        compiler_params=_params(("parallel",)), name="qk_prep_lat",
    )(proj, qg, kg, cos, sin)


def _attn_kernel(*refs, with_ctx):
    if with_ctx:
        q_ref, k_ref, v_ref, ck_ref, cv_ref, o_ref = refs
    else:
        q_ref, k_ref, v_ref, o_ref = refs
    c = HEAD_DIM ** -0.5 * math.log2(math.e)
    nt = (((1,), (1,)), ((), ()))
    q = q_ref[...]
    s = lax.dot_general(q, k_ref[...], nt, preferred_element_type=F32)
    m = jnp.max(s, axis=-1, keepdims=True)
    if with_ctx:
        sc = lax.dot_general(q, ck_ref[...], nt, preferred_element_type=F32)
        m = jnp.maximum(m, jnp.max(sc, axis=-1, keepdims=True))
    p = jnp.exp2((s - m) * c)
    l = jnp.sum(p, axis=-1, keepdims=True)
    o = jnp.dot(p.astype(BF16), v_ref[...], preferred_element_type=F32)
    if with_ctx:
        pc = jnp.exp2((sc - m) * c)
        l = l + jnp.sum(pc, axis=-1, keepdims=True)
        o = o + jnp.dot(pc.astype(BF16), cv_ref[...], preferred_element_type=F32)
    o_ref[...] = (o / l).astype(BF16)


def _attention(q, k, v, n_batch, seq, ctx_kv):
    tq = ROW_TILE
    nq = seq // tq
    t = n_batch * seq
    q_spec = pl.BlockSpec((tq, HEAD_DIM), lambda b, h, i: (b * nq + i, h))
    kv_spec = pl.BlockSpec((seq, HEAD_DIM), lambda b, h, i: (b, h // GQA_GROUP))
    in_specs = [q_spec, kv_spec, kv_spec]
    args = [q, k, v]
    if ctx_kv is not None:
        c_spec = pl.BlockSpec((PAST_LEN, HEAD_DIM), lambda b, h, i: (b, h // GQA_GROUP))
        in_specs += [c_spec, c_spec]
        args += list(ctx_kv)
    return pl.pallas_call(
        functools.partial(_attn_kernel, with_ctx=ctx_kv is not None),
        grid=(n_batch, N_HEADS, nq),
        in_specs=in_specs,
        out_specs=pl.BlockSpec((tq, HEAD_DIM), lambda b, h, i: (b * nq + i, h)),
        out_shape=jax.ShapeDtypeStruct((t, Q_WIDTH), BF16),
        compiler_params=_params(("parallel", "parallel", "parallel")),
        name="attention_lat" if ctx_kv is not None else "attention_ctx",
    )(*args)


def _outproj_kernel(ac_ref, al_ref, zc_ref, zl_ref, xc_ref, xl_ref, gate_ref, wa_ref, wz_ref, o_ref):
    acc = jnp.dot(_pick_rows(ac_ref, al_ref), wa_ref[...], preferred_element_type=F32)
    acc = acc + jnp.dot(_pick_rows(zc_ref, zl_ref), wz_ref[...], preferred_element_type=F32)
    o_ref[...] = _pick_rows(xc_ref, xl_ref) + gate_ref[0, 0] * acc


def _out_proj(attn, z, x, mods4, k_gate, wa_bf16, wz_bf16):
    tm = ROW_TILE
    return pl.pallas_call(
        _outproj_kernel,
        grid=(T_ALL // tm,),
        in_specs=(_split_row_specs(Q_WIDTH, tm) + _split_row_specs(HYENA_WIDTH, tm) + _split_row_specs(D_MODEL, tm)
                  + [_mod_spec(k_gate, tm), _resident((Q_WIDTH, D_MODEL)), _resident((HYENA_WIDTH, D_MODEL))]),
        out_specs=pl.BlockSpec((tm, D_MODEL), lambda i: (i, 0)),
        out_shape=jax.ShapeDtypeStruct((T_ALL, D_MODEL), F32),
        compiler_params=_params(("parallel",)),
        name="out_proj",
    )(*attn, *z, *x, mods4, wa_bf16, wz_bf16)


def _gelu_tanh(x):
    return 0.5 * x * (1.0 + jnp.tanh(math.sqrt(2.0 / math.pi) * (x + 0.044715 * (x * x * x))))


def _glu_kernel(yfc_ref, yfl_ref, ybc_ref, ybl_ref, u_ref, d_ref, x_ref, gate_ref, w_ref, o_ref):
    y = (_pick_rows(yfc_ref, yfl_ref) + _pick_rows(ybc_ref, ybl_ref)) + d_ref[...] * u_ref[...]
    ag = jnp.dot(_gelu_tanh(y).astype(BF16), w_ref[...], preferred_element_type=F32)
    a, g = ag[:, :D_MODEL], ag[:, D_MODEL:]
    o_ref[...] = x_ref[...] + gate_ref[0, 0] * (a * jax.nn.sigmoid(g))


def _glu_proj(y_fwd, y_bwd, u, d_skip, x, mods4, k_gate, w_bf16):
    tm = ROW_TILE
    return pl.pallas_call(
        _glu_kernel,
        grid=(T_ALL // tm,),
        in_specs=(_split_row_specs(S5_WIDTH, tm) + _split_row_specs(S5_WIDTH, tm)
                  + [pl.BlockSpec((tm, S5_WIDTH), lambda i: (i, 0)), _resident((1, S5_WIDTH)),
                     pl.BlockSpec((tm, D_MODEL), lambda i: (i, 0)), _mod_spec(k_gate, tm),
                     _resident((S5_WIDTH, 2 * D_MODEL))]),
        out_specs=pl.BlockSpec((tm, D_MODEL), lambda i: (i, 0)),
        out_shape=jax.ShapeDtypeStruct((T_ALL, D_MODEL), F32),
        compiler_params=_params(("parallel",)),
        name="glu_proj",
    )(*y_fwd, *y_bwd, u, d_skip.reshape(1, S5_WIDTH), x, mods4, w_bf16)


def _gmm_kernel(be_ref, xs_ref, w13_ref, w2_ref, o_ref, w13_b, w2_b):
    i = pl.program_id(0)
    fresh = jnp.logical_or(i == 0, be_ref[i] != be_ref[jnp.maximum(i - 1, 0)])

    @pl.when(fresh)
    def _():
        w13_b[...] = w13_ref[0].astype(BF16)
        w2_b[...] = w2_ref[0].astype(BF16)

    gu = jnp.dot(xs_ref[...], w13_b[...], preferred_element_type=F32)
    hmid = _silu(gu[:, :EXPERT_HIDDEN]) * gu[:, EXPERT_HIDDEN:]
    o_ref[...] = jnp.dot(hmid.astype(BF16), w2_b[...], preferred_element_type=F32).astype(BF16)


def _grouped_experts(blk_e, xs, w13, w2):
    n_rows = xs.shape[0]
    n_blocks = n_rows // MOE_BLOCK
    grid_spec = pltpu.PrefetchScalarGridSpec(
        num_scalar_prefetch=1,
        grid=(n_blocks,),
        in_specs=[pl.BlockSpec((MOE_BLOCK, D_MODEL), lambda i, be: (i, 0)),
                  pl.BlockSpec((1, D_MODEL, 2 * EXPERT_HIDDEN), lambda i, be: (be[i], 0, 0)),
                  pl.BlockSpec((1, EXPERT_HIDDEN, D_MODEL), lambda i, be: (be[i], 0, 0))],
        out_specs=pl.BlockSpec((MOE_BLOCK, D_MODEL), lambda i, be: (i, 0)),
        scratch_shapes=[pltpu.VMEM((D_MODEL, 2 * EXPERT_HIDDEN), BF16),
                        pltpu.VMEM((EXPERT_HIDDEN, D_MODEL), BF16)],
    )
    return pl.pallas_call(
        _gmm_kernel,
        grid_spec=grid_spec,
        out_shape=jax.ShapeDtypeStruct((n_rows, D_MODEL), BF16),
        compiler_params=_params(("arbitrary",)),
        name="grouped_experts",
    )(blk_e, xs, w13, w2)


COMBINE_TILE = 256


def _shared_kernel(h_ref, x_ref, ye_ref, w_ref, gate_ref, w13_ref, w2_ref, o_ref):
    gu = jnp.dot(h_ref[...], w13_ref[...], preferred_element_type=F32)
    hmid = _silu(gu[:, :EXPERT_HIDDEN]) * gu[:, EXPERT_HIDDEN:]
    acc = jnp.dot(hmid.astype(BF16), w2_ref[...], preferred_element_type=F32)
    wts = w_ref[...]
    for k in range(TOP_K):
        acc = acc + wts[:, k:k + 1] * ye_ref[k].astype(F32)
    o_ref[...] = x_ref[...] + gate_ref[0, 0] * acc


def _shared_and_combine(h_bf16, x, y_exp, wts, mods4, k_gate, w13_bf16, w2_bf16):
    t = x.shape[0]
    tm = COMBINE_TILE
    row = lambda i: (i, 0)
    return pl.pallas_call(
        _shared_kernel,
        grid=(t // tm,),
        in_specs=[pl.BlockSpec((tm, D_MODEL), row), pl.BlockSpec((tm, D_MODEL), row),
                  pl.BlockSpec((TOP_K, tm, D_MODEL), lambda i: (0, i, 0)),
                  pl.BlockSpec((tm, TOP_K), row), _mod_spec(k_gate, tm),
                  _resident((D_MODEL, 2 * EXPERT_HIDDEN)), _resident((EXPERT_HIDDEN, D_MODEL))],
        out_specs=pl.BlockSpec((tm, D_MODEL), row),
        out_shape=jax.ShapeDtypeStruct((t, D_MODEL), F32),
        compiler_params=_params(("parallel",)),
        name="shared_expert_combine",
    )(h_bf16, x, y_exp, wts, mods4, w13_bf16, w2_bf16)


ROUTE_TILE = 256
GROUP_SIZE = N_EXPERTS // N_EXPERT_GROUPS
NEG_INF = float("-inf")


def _first_argmax(v, index, sentinel):
    m = jnp.max(v, axis=0, keepdims=True)
    first = jnp.min(jnp.where(v == m, index, sentinel), axis=0, keepdims=True)
    return m, first


def _router_kernel(x_ref, g_ref, sc_ref, sh_ref, wt_ref, rb_ref, tri_ref,
                   h_out, idx_out, gate_out, rank_out, cnt_out, base_ref):
    tm = x_ref.shape[0]

    @pl.when(pl.program_id(0) == 0)
    def _():
        base_ref[...] = jnp.zeros_like(base_ref)

    hb = _norm_mod(x_ref[...], g_ref[...], sc_ref[0, 0], sh_ref[0, 0]).astype(BF16)
    h_out[...] = hb
    logits = lax.dot_general(wt_ref[...], hb, (((1,), (1,)), ((), ())), preferred_element_type=F32)
    scores = jax.nn.sigmoid(logits)
    choice = scores + rb_ref[...]

    member = lax.broadcasted_iota(jnp.int32, (GROUP_SIZE, tm), 0).astype(F32)
    group_scores = []
    for g in range(N_EXPERT_GROUPS):
        cg = choice[g * GROUP_SIZE:(g + 1) * GROUP_SIZE, :]
        m1, first = _first_argmax(cg, member, float(GROUP_SIZE))
        m2 = jnp.max(jnp.where(member == first, NEG_INF, cg), axis=0, keepdims=True)
        group_scores.append(m1 + m2)
    gs = jnp.concatenate(group_scores, axis=0)

    group = lax.broadcasted_iota(jnp.int32, (N_EXPERT_GROUPS, tm), 0).astype(F32)
    keep = jnp.zeros((N_EXPERT_GROUPS, tm), F32)
    for _ in range(TOPK_GROUPS):
        _, first = _first_argmax(gs, group, float(N_EXPERT_GROUPS))
        sel = group == first
        keep = jnp.where(sel, 1.0, keep)
        gs = jnp.where(sel, NEG_INF, gs)
    masked = jnp.concatenate(
        [jnp.where(keep[g:g + 1, :] > 0.0, choice[g * GROUP_SIZE:(g + 1) * GROUP_SIZE, :], NEG_INF)
         for g in range(N_EXPERT_GROUPS)], axis=0)

    expert = lax.broadcasted_iota(jnp.int32, (N_EXPERTS, tm), 0).astype(F32)
    onehot = jnp.zeros((N_EXPERTS, tm), F32)
    sels, idx_rows, gate_rows = [], [], []
    for _ in range(TOP_K):
        _, first = _first_argmax(masked, expert, float(N_EXPERTS))
        sel = expert == first
        sels.append(sel)
        idx_rows.append(first)
        gate_rows.append(jnp.sum(jnp.where(sel, scores, 0.0), axis=0, keepdims=True))
        masked = jnp.where(sel, NEG_INF, masked)
        onehot = jnp.where(sel, 1.0, onehot)
    gates = jnp.concatenate(gate_rows, axis=0)
    gate_out[...] = gates / jnp.sum(gates, axis=0, keepdims=True) * ROUTED_SCALE
    idx_out[...] = jnp.concatenate(idx_rows, axis=0).astype(jnp.int32)

    cum = jnp.dot(onehot.astype(BF16), tri_ref[...], preferred_element_type=F32) + base_ref[:, 0:1]
    rank_rows = [jnp.sum(jnp.where(sel, cum, 0.0), axis=0, keepdims=True) - 1.0 for sel in sels]
    rank_out[...] = jnp.concatenate(rank_rows, axis=0).astype(jnp.int32)
    base_ref[...] = jnp.broadcast_to(cum[:, tm - 1:tm], base_ref.shape)
    cnt_out[...] = base_ref[...].astype(jnp.int32)


def _router(x, norm_g, mods4, k_shift, k_scale, router_w, router_bias):
    t = x.shape[0]
    tm = ROUTE_TILE
    tri = (jnp.arange(tm)[:, None] <= jnp.arange(tm)[None, :]).astype(BF16)
    choice_rows = pl.BlockSpec((TOP_K, tm), lambda i: (0, i))
    return pl.pallas_call(
        _router_kernel,
        grid=(t // tm,),
        in_specs=[pl.BlockSpec((tm, D_MODEL), lambda i: (i, 0)), _resident((1, D_MODEL)),
                  _mod_spec(k_scale, tm), _mod_spec(k_shift, tm),
                  _resident((N_EXPERTS, D_MODEL)), _resident((N_EXPERTS, 1)), _resident((tm, tm))],
        out_specs=[pl.BlockSpec((tm, D_MODEL), lambda i: (i, 0)), choice_rows, choice_rows, choice_rows,
                   pl.BlockSpec((N_EXPERTS, 128), lambda i: (0, 0))],
        out_shape=[jax.ShapeDtypeStruct((t, D_MODEL), BF16),
                   jax.ShapeDtypeStruct((TOP_K, t), jnp.int32), jax.ShapeDtypeStruct((TOP_K, t), F32),
                   jax.ShapeDtypeStruct((TOP_K, t), jnp.int32), jax.ShapeDtypeStruct((N_EXPERTS, 128), jnp.int32)],
        scratch_shapes=[pltpu.VMEM((N_EXPERTS, 128), F32)],
        compiler_params=_params(("arbitrary",)),
        name="router",
    )(x, norm_g.reshape(1, D_MODEL), mods4, mods4, router_w.T.astype(BF16), router_bias.reshape(N_EXPERTS, 1), tri)


def _pos_kernel(idx_ref, rank_ref, off_ref, pos_out):
    tm = idx_ref.shape[1]
    expert = lax.broadcasted_iota(jnp.int32, (N_EXPERTS, tm), 0)
    rows = []
    for k in range(TOP_K):
        sel = expert == idx_ref[k:k + 1, :]
        rows.append(jnp.sum(jnp.where(sel, off_ref[...], 0.0), axis=0, keepdims=True))
    pos_out[...] = jnp.concatenate(rows, axis=0).astype(jnp.int32) + rank_ref[...]


def _positions(idx, rank, offsets):
    t = idx.shape[1]
    tm = 1024
    rows = pl.BlockSpec((TOP_K, tm), lambda i: (0, i))
    return pl.pallas_call(
        _pos_kernel, grid=(t // tm,),
        in_specs=[rows, rows, _resident((N_EXPERTS, 1))],
        out_specs=rows, out_shape=jax.ShapeDtypeStruct((TOP_K, t), jnp.int32),
        compiler_params=_params(("parallel",)), name="expert_positions",
    )(idx, rank, offsets.astype(F32).reshape(N_EXPERTS, 1))


def _moe(x, norm_g, mods4, k_shift, k_scale, k_gate, router_w, router_bias, exp_w13, exp_w2, sh_w13, sh_w2):
    t = x.shape[0]
    h_bf16, top_idx, gate, rank, counts = _router(x, norm_g, mods4, k_shift, k_scale, router_w, router_bias)
    counts = counts[:, 0]
    padded = (counts + MOE_BLOCK - 1) // MOE_BLOCK * MOE_BLOCK
    e_iota = jnp.arange(N_EXPERTS)
    pend = jnp.sum(jnp.where(e_iota[:, None] <= e_iota[None, :], padded[:, None], 0), axis=0)
    pos = _positions(top_idx, rank, pend - padded)
    n_rows = -(-(t * TOP_K + N_EXPERTS * (MOE_BLOCK - 1)) // MOE_BLOCK) * MOE_BLOCK
    n_blocks = n_rows // MOE_BLOCK
    blk_start = jnp.arange(n_blocks, dtype=pend.dtype) * MOE_BLOCK
    blk_e = jnp.minimum(jnp.sum((pend[None, :] <= blk_start[:, None]).astype(jnp.int32), axis=1), N_EXPERTS - 1)
    tok = jnp.broadcast_to(jnp.arange(t, dtype=jnp.int32)[None, :], (TOP_K, t))
    _, sorted_tok = lax.sort_key_val(pos.reshape(-1), tok.reshape(-1))
    dense_end = jnp.sum(jnp.where(e_iota[:, None] <= e_iota[None, :], counts[:, None], 0), axis=0)
    shift_blk = ((pend - padded) - (dense_end - counts))[blk_e]
    end_blk = dense_end[blk_e]
    row = jnp.arange(n_rows, dtype=jnp.int32)
    dense = row - jnp.repeat(shift_blk, MOE_BLOCK)
    row_tok = jnp.where(dense < jnp.repeat(end_blk, MOE_BLOCK),
                        sorted_tok[jnp.clip(dense, 0, t * TOP_K - 1)], row % t)
    xs = h_bf16[row_tok]
    ys = _grouped_experts(blk_e, xs, exp_w13, exp_w2)
    y_exp = ys[pos.reshape(-1)].reshape(TOP_K, t, D_MODEL)
    return _shared_and_combine(h_bf16, x, y_exp, gate.T, mods4, k_gate, sh_w13.astype(BF16), sh_w2.astype(BF16))


HY_BLOCK_CTX, HY_TC_CTX = 256, 512
HY_BLOCK_LAT, HY_TC_LAT = 512, 128


def _hyena_filters(L, w1, b1, w2, b2, w3, sin_freq):
    t_norm = jnp.linspace(0.0, 1.0, L, dtype=F32)[:, None]
    omega = 2.0 * math.pi * jnp.arange(L, dtype=F32)[:, None] / L
    bands = jnp.linspace(1e-4, FILTER_BANDS - 1, FILTER_BANDS, dtype=F32)[None, :]
    z = jnp.concatenate([t_norm, jnp.cos(bands * omega), -jnp.sin(bands * omega)], axis=-1)
    h = jnp.sin(sin_freq * (z @ w1 + b1))
    h = jnp.sin(sin_freq * (h @ w2 + b2))
    h = (h @ w3).reshape(L, 2, HYENA_ORDER, HYENA_WIDTH)
    deltas = jnp.abs(jnp.linspace(math.log(DECAY_TARGET) / DECAY_PCT_LONG,
                                  math.log(DECAY_TARGET) / DECAY_PCT_SHORT, HYENA_WIDTH, dtype=F32))
    h = h * jnp.exp(-t_norm * deltas)[:, None, None, :]
    fwd, bwd = h[:, 0], h[:, 1] * (t_norm > 0.0)[:, :, None]
    norm = jnp.sum(jnp.abs(fwd), axis=0, keepdims=True) + jnp.sum(jnp.abs(bwd), axis=0, keepdims=True)
    return fwd / norm, bwd / norm


HY_ACC_ELEMS = 8192


def _dft_matrices(bk):
    k = jnp.arange(bk, dtype=jnp.int32)
    phase = ((2 * k[:, None] + 1) * k[None, :]) % (4 * bk)
    ang = phase.astype(F32) * (math.pi / (2 * bk))
    return jnp.cos(ang), jnp.sin(ang)


def _spectra_kernel(xf_ref, xb_ref, yf_ref, yb_ref, c1_ref, s1_ref, c2_ref, s2_ref, o_ref, *, nb):
    d = pl.program_id(1) - (nb - 1)
    x = jnp.where(d >= 0, xf_ref[0], xb_ref[0]).astype(BF16)
    y = jnp.where(d >= 1, yf_ref[0], yb_ref[0]).astype(BF16)
    cy = jnp.where(d == 0, c1_ref[...], c2_ref[...])
    sy = jnp.where(d == 0, s1_ref[...], s2_ref[...])
    re = jnp.dot(c1_ref[...], x, preferred_element_type=F32) + jnp.dot(cy, y, preferred_element_type=F32)
    im = jnp.dot(sy, y, preferred_element_type=F32) - jnp.dot(s1_ref[...], x, preferred_element_type=F32)
    o_ref[0, 0, 0] = re
    o_ref[0, 0, 1] = jnp.where(d >= 0, im, -im)


def _filter_spectra(fwd, bwd, bk, tc):
    L = fwd.shape[0]
    nb = L // bk
    n_ct = HYENA_WIDTH // tc
    cos, sin = _dft_matrices(bk)
    k = jnp.arange(bk, dtype=jnp.int32)
    phase = ((2 * k[:, None] + 1) * (bk - k[None, :])) % (4 * bk)
    ang = phase.astype(F32) * (math.pi / (2 * bk))
    live = (k[None, :] > 0).astype(F32)
    mats = [m.astype(BF16) for m in (cos, sin, jnp.cos(ang) * live, jnp.sin(ang) * live)]

    def seg(pick):
        return pl.BlockSpec((1, bk, tc), lambda o, di, ct: (pick(di - (nb - 1)), 0, o * n_ct + ct))

    mat = pl.BlockSpec((bk, bk), lambda o, di, ct: (0, 0), pipeline_mode=pl.Buffered(1))
    halves = [h.reshape(nb, bk, HYENA_ORDER * HYENA_WIDTH) for h in (fwd, bwd)]
    return pl.pallas_call(
        functools.partial(_spectra_kernel, nb=nb),
        grid=(HYENA_ORDER, 2 * nb - 1, n_ct),
        in_specs=[seg(lambda d: jnp.maximum(d, 0)), seg(lambda d: jnp.maximum(-d, 0)),
                  seg(lambda d: jnp.maximum(d - 1, 0)), seg(lambda d: jnp.maximum(-d - 1, 0)),
                  mat, mat, mat, mat],
        out_specs=pl.BlockSpec((1, 1, 2, bk, tc), lambda o, di, ct: (o, di, 0, 0, ct)),
        out_shape=jax.ShapeDtypeStruct((HYENA_ORDER, 2 * nb - 1, 2, bk, HYENA_WIDTH), F32),
        compiler_params=_params(("parallel", "parallel", "parallel")),
        name="filter_spectra",
    )(halves[0], halves[1], halves[0], halves[1], *mats)


def _hyena_kernel(v_ref, x1_ref, x2_ref, cw_ref, cb_ref, hb_ref, k0_ref, k1_ref, c_ref, s_ref, ct_ref, st_ref,
                  o_ref, vs_ref, z1_ref, x2s_ref, ur_ref, ui_ref, yr_ref, yi_ref, *, seq, bk, tc):
    nb = seq // bk
    ft = HY_ACC_ELEMS // tc
    row = lax.broadcasted_iota(jnp.int32, (seq, tc), 0)

    def short_conv(x, j):
        prev = jnp.where(row == 0, 0.0, pltpu.roll(x, 1, 0))
        nxt = jnp.where(row == seq - 1, 0.0, pltpu.roll(x, seq - 1, 0))
        return (prev * cw_ref[0, j:j + 1, :] + x * cw_ref[1, j:j + 1, :] + nxt * cw_ref[2, j:j + 1, :]
                + cb_ref[j:j + 1, :])

    vs_ref[...] = short_conv(v_ref[...], 0)
    z1_ref[...] = short_conv(x1_ref[...], 1)
    x2s_ref[...] = short_conv(x2_ref[...], 2)

    def long_conv(src_ref, k_ref, emit):
        for blk in range(nb):
            ub = src_ref[blk * bk:(blk + 1) * bk, :].astype(BF16)
            ur_ref[blk] = jnp.dot(c_ref[...], ub, preferred_element_type=F32)
            ui_ref[blk] = -jnp.dot(s_ref[...], ub, preferred_element_type=F32)
        for out_blk in range(nb):
            def acc_tile(f, carry, out_blk=out_blk):
                rows = pl.ds(pl.multiple_of(f * ft, ft), ft)
                ar = jnp.zeros((ft, tc), F32)
                ai = jnp.zeros((ft, tc), F32)
                for in_blk in range(nb):
                    d = out_blk - in_blk + nb - 1
                    kr, ki = k_ref[d, 0, rows, :], k_ref[d, 1, rows, :]
                    xr, xi = ur_ref[in_blk, rows, :], ui_ref[in_blk, rows, :]
                    ar = ar + (kr * xr - ki * xi)
                    ai = ai + (kr * xi + ki * xr)
                yr_ref[rows, :] = ar
                yi_ref[rows, :] = ai
                return carry

            lax.fori_loop(0, bk // ft, acc_tile, 0)
            y = jnp.dot(ct_ref[...], yr_ref[...].astype(BF16), preferred_element_type=F32)
            y = y - jnp.dot(st_ref[...], yi_ref[...].astype(BF16), preferred_element_type=F32)
            emit(slice(out_blk * bk, (out_blk + 1) * bk), y * (1.0 / bk))

    def emit_z1(rows, y):
        z1_ref[rows, :] = z1_ref[rows, :] * (y + hb_ref[0:1, :] * vs_ref[rows, :])

    def emit_out(rows, y):
        o_ref[rows, :] = (x2s_ref[rows, :] * (y + hb_ref[1:2, :] * z1_ref[rows, :])).astype(BF16)

    long_conv(vs_ref, k0_ref, emit_z1)
    long_conv(z1_ref, k1_ref, emit_out)


def _hyena(proj, row0, n_batch, seq, bk, tc, conv_w, conv_b, hy_bias, spectra, mats):
    nb = seq // bk
    n_ct = HYENA_WIDTH // tc
    col0 = QKV_WIDTH // tc
    per = HYENA_WIDTH // tc
    b0 = row0 // seq

    def data(which):
        return pl.BlockSpec((seq, tc), lambda ct, b: (b0 + b, col0 + which * per + ct))

    const2 = lambda shape: pl.BlockSpec(shape, lambda ct, b: (0, 0), pipeline_mode=pl.Buffered(1))
    kspec = pl.BlockSpec((2 * nb - 1, 2, bk, tc), lambda ct, b: (0, 0, 0, ct), pipeline_mode=pl.Buffered(1))
    cw = conv_w.reshape(3, HYENA_ORDER + 1, HYENA_WIDTH)
    cb = conv_b.reshape(HYENA_ORDER + 1, HYENA_WIDTH)
    return pl.pallas_call(
        functools.partial(_hyena_kernel, seq=seq, bk=bk, tc=tc),
        grid=(n_ct, n_batch),
        in_specs=[data(0), data(1), data(2),
                  pl.BlockSpec((3, HYENA_ORDER + 1, tc), lambda ct, b: (0, 0, ct)),
                  pl.BlockSpec((HYENA_ORDER + 1, tc), lambda ct, b: (0, ct)),
                  pl.BlockSpec((HYENA_ORDER, tc), lambda ct, b: (0, ct)),
                  kspec, kspec, const2((bk, bk)), const2((bk, bk)), const2((bk, bk)), const2((bk, bk))],
        out_specs=pl.BlockSpec((seq, tc), lambda ct, b: (b, ct)),
        out_shape=jax.ShapeDtypeStruct((n_batch * seq, HYENA_WIDTH), BF16),
        scratch_shapes=[pltpu.VMEM((seq, tc), F32), pltpu.VMEM((seq, tc), F32), pltpu.VMEM((seq, tc), F32),
                        pltpu.VMEM((nb, bk, tc), F32), pltpu.VMEM((nb, bk, tc), F32),
                        pltpu.VMEM((bk, tc), F32), pltpu.VMEM((bk, tc), F32)],
        compiler_params=_params(("parallel", "parallel")),
        name="hyena_seq%d" % seq,
    )(proj, proj, proj, cw, cb, hy_bias, spectra[0], spectra[1], *mats)


HY_SPECTRA_TC = 512


def _hyena_operands(L, bk, f_w1, f_b1, f_w2, f_b2, f_w3, sin_freq):
    fwd, bwd = _hyena_filters(L, f_w1, f_b1, f_w2, f_b2, f_w3, sin_freq)
    cos, sin = _dft_matrices(bk)
    mats = tuple(m.astype(BF16) for m in (cos, sin, cos.T, sin.T))
    return _filter_spectra(fwd, bwd, bk, HY_SPECTRA_TC), mats


S5_LANES = S5_GROUPS * S5_STATE
S5_CLUSTER_GROUPS = 16
S5_CLUSTERS = S5_GROUPS // S5_CLUSTER_GROUPS
S5_CLUSTER_CH = S5_CLUSTER_GROUPS * S5_GROUP_CH
S5_CLUSTER_LANES = S5_CLUSTER_GROUPS * S5_STATE
SCAN_CHUNK = 256
SCAN_TILE = 8
SCAN_LEVELS = (1, 2, 4)


def _s5_kernel(u_ref, s0r_ref, s0i_ref, bw_ref, cwr_ref, cwi_ref, tab_ref, y_ref, sfr_ref, sfi_ref,
               sre, sim, car, cai, *, reverse):
    @pl.when(pl.program_id(1) == 0)
    def _():
        car[...] = s0r_ref[0]
        cai[...] = s0i_ref[0]

    ub = u_ref[...].astype(BF16)
    for k in range(S5_CLUSTERS):
        bu = jnp.dot(ub[:, k * S5_CLUSTER_CH:(k + 1) * S5_CLUSTER_CH], bw_ref[k], preferred_element_type=F32)
        sre[:, k * S5_CLUSTER_LANES:(k + 1) * S5_CLUSTER_LANES] = bu[:, :S5_CLUSTER_LANES]
        sim[:, k * S5_CLUSTER_LANES:(k + 1) * S5_CLUSTER_LANES] = bu[:, S5_CLUSTER_LANES:]

    n_tiles = SCAN_CHUNK // SCAN_TILE
    shifts = tuple(SCAN_TILE - k for k in SCAN_LEVELS) if reverse else SCAN_LEVELS
    boundary_row = 0 if reverse else SCAN_TILE - 1
    for k in range(S5_CLUSTERS):
        lanes = slice(k * S5_CLUSTER_LANES, (k + 1) * S5_CLUSTER_LANES)

        def tile_step(i, carry, lanes=lanes):
            cr, ci = carry
            tile = (n_tiles - 1 - i) if reverse else i
            rows = pl.ds(pl.multiple_of(tile * SCAN_TILE, SCAN_TILE), SCAN_TILE)
            xr, xi = sre[rows, lanes], sim[rows, lanes]
            for lvl, sh in enumerate(shifts):
                pr, pi = tab_ref[2 * lvl, :, lanes], tab_ref[2 * lvl + 1, :, lanes]
                rr, ri = pltpu.roll(xr, sh, 0), pltpu.roll(xi, sh, 0)
                xr, xi = xr + (pr * rr - pi * ri), xi + (pr * ri + pi * rr)
            pr, pi = tab_ref[2 * len(shifts), :, lanes], tab_ref[2 * len(shifts) + 1, :, lanes]
            xr, xi = xr + (pr * cr - pi * ci), xi + (pr * ci + pi * cr)
            sre[rows, lanes] = xr
            sim[rows, lanes] = xi
            return xr[boundary_row:boundary_row + 1, :], xi[boundary_row:boundary_row + 1, :]

        cr, ci = lax.fori_loop(0, n_tiles, tile_step, (car[:, lanes], cai[:, lanes]))
        car[:, lanes] = cr
        cai[:, lanes] = ci

    for k in range(S5_CLUSTERS):
        lanes = slice(k * S5_CLUSTER_LANES, (k + 1) * S5_CLUSTER_LANES)
        yk = jnp.dot(sre[:, lanes].astype(BF16), cwr_ref[k], preferred_element_type=F32)
        yk = yk + jnp.dot(sim[:, lanes].astype(BF16), cwi_ref[k], preferred_element_type=F32)
        y_ref[:, k * S5_CLUSTER_CH:(k + 1) * S5_CLUSTER_CH] = yk
    sfr_ref[0] = car[...]
    sfi_ref[0] = cai[...]


def _s5_direction(u, row0, n_seq, seq, s0_re, s0_im, bw, cwr, cwi, tabs, reverse):
    nc = seq // SCAN_CHUNK
    b0 = row0 // SCAN_CHUNK

    def chunk(c):
        return (nc - 1 - c) if reverse else c

    state_spec = pl.BlockSpec((1, 1, S5_LANES), lambda b, c: (b, 0, 0))
    full3 = lambda shape: pl.BlockSpec(shape, lambda b, c: (0, 0, 0), pipeline_mode=pl.Buffered(1))
    return pl.pallas_call(
        functools.partial(_s5_kernel, reverse=reverse),
        grid=(n_seq, nc),
        in_specs=[pl.BlockSpec((SCAN_CHUNK, S5_WIDTH), lambda b, c: (b0 + b * nc + chunk(c), 0)),
                  state_spec, state_spec,
                  full3(bw.shape), full3(cwr.shape), full3(cwi.shape), full3(tabs.shape)],
        out_specs=[pl.BlockSpec((SCAN_CHUNK, S5_WIDTH), lambda b, c: (b * nc + chunk(c), 0)),
                   state_spec, state_spec],
        out_shape=[jax.ShapeDtypeStruct((n_seq * seq, S5_WIDTH), F32),
                   jax.ShapeDtypeStruct((n_seq, 1, S5_LANES), F32),
                   jax.ShapeDtypeStruct((n_seq, 1, S5_LANES), F32)],
        scratch_shapes=[pltpu.VMEM((SCAN_CHUNK, S5_LANES), F32), pltpu.VMEM((SCAN_CHUNK, S5_LANES), F32),
                        pltpu.VMEM((1, S5_LANES), F32), pltpu.VMEM((1, S5_LANES), F32)],
        compiler_params=_params(("parallel", "arbitrary")),
        name="s5_bwd" if reverse else "s5_fwd",
    )(u, s0_re, s0_im, bw, cwr, cwi, tabs)


def _s5_weights(a_re, a_im, log_dt, b_re, b_im, c_re, c_im):
    lam = lax.complex(a_re, a_im)
    lam_bar = jnp.exp(lam * jnp.exp(log_dt)[..., None])
    b_bar = ((lam_bar - 1.0) / lam)[..., None] * lax.complex(b_re, b_im)
    eye = jnp.eye(S5_CLUSTER_GROUPS, dtype=F32)
    out = []
    for d in range(2):
        def cluster_in(w):
            w = w.reshape(S5_CLUSTERS, S5_CLUSTER_GROUPS, S5_STATE, S5_GROUP_CH)
            return jnp.einsum('ab,kapc->kacbp', eye, w).reshape(S5_CLUSTERS, S5_CLUSTER_CH, S5_CLUSTER_LANES)

        def cluster_out(w):
            w = w.reshape(S5_CLUSTERS, S5_CLUSTER_GROUPS, S5_GROUP_CH, S5_STATE)
            return jnp.einsum('ab,kacp->kapbc', eye, w).reshape(S5_CLUSTERS, S5_CLUSTER_LANES, S5_CLUSTER_CH)

        bw = jnp.concatenate([cluster_in(jnp.real(b_bar[d])), cluster_in(jnp.imag(b_bar[d]))], axis=-1)
        cwr = cluster_out(c_re[d])
        cwi = cluster_out(-c_im[d])
        lam_d = lam_bar[d].reshape(S5_LANES)
        pows = [lam_d]
        for _ in range(SCAN_TILE - 1):
            pows.append(pows[-1] * lam_d)
        pows = jnp.stack(pows)
        j = jnp.arange(SCAN_TILE)
        tabs = []
        for k in SCAN_LEVELS:
            valid = (j <= SCAN_TILE - 1 - k) if d == 1 else (j >= k)
            p = jnp.where(valid[:, None], pows[k - 1][None, :], 0.0)
            tabs += [jnp.real(p), jnp.imag(p)]
        pc = pows[::-1] if d == 1 else pows
        tabs += [jnp.real(pc), jnp.imag(pc)]
        out.append((bw.astype(BF16), cwr.astype(BF16), cwi.astype(BF16), jnp.stack(tabs).astype(F32)))
    return out


def _final_norm_kernel(x_ref, g_ref, o_ref):
    x = x_ref[...]
    ms = jnp.mean(x * x, axis=-1, keepdims=True)
    o_ref[...] = x * lax.rsqrt(ms + RMS_EPS) * g_ref[...]


def _final_norm(x, g, row0, t):
    tm = ROW_TILE
    b0 = row0 // tm
    return pl.pallas_call(
        _final_norm_kernel,
        grid=(t // tm,),
        in_specs=[pl.BlockSpec((tm, D_MODEL), lambda i: (b0 + i, 0)), _resident((1, D_MODEL))],
        out_specs=pl.BlockSpec((tm, D_MODEL), lambda i: (i, 0)),
        out_shape=jax.ShapeDtypeStruct((t, D_MODEL), F32),
        compiler_params=_params(("parallel",)),
        name="final_norm",
    )(x, g.reshape(1, D_MODEL))


def _rope_tables(L):
    n_rows = L // GRID_W
    row_idx = jnp.repeat(jnp.arange(n_rows, dtype=F32), GRID_W)
    col_idx = (jnp.arange(L) % GRID_W).astype(F32)
    inv = ROPE_THETA ** (-jnp.arange(0, ROPE_AXIS_DIM, 2, dtype=F32) / ROPE_AXIS_DIM)
    ang = jnp.concatenate([row_idx[:, None] * inv, col_idx[:, None] * inv], axis=-1)
    cos, sin = jnp.cos(ang), jnp.sin(ang)
    cos_full = jnp.repeat(cos, 2, axis=-1)
    sin_signed = jnp.stack([-sin, sin], axis=-1).reshape(L, HEAD_DIM)
    return cos_full, sin_signed


def kernel(x_prompt, x_sample, cache_k_l0, cache_v_l0, state_s5_re_l1, state_s5_im_l1, c, c_ctx,
           ada_w_l0, ada_b_l0, norm1_l0, norm2_l0,
           w_in_l0, w_out_l0, q_norm_l0, k_norm_l0, hy_conv_w_l0, hy_conv_b_l0,
           hy_ffn_w1_l0, hy_ffn_b1_l0, hy_ffn_w2_l0, hy_ffn_b2_l0, hy_ffn_w3_l0, hy_sin_freq_l0, hy_bias_l0,
           router_l0, router_bias_l0, exp_w13_l0, exp_w2_l0, shared_w13_l0, shared_w2_l0,
           ada_w_l1, ada_b_l1, norm1_l1, norm2_l1,
           w_in_l1, s5_a_re_l1, s5_a_im_l1, s5_log_dt_l1, s5_b_re_l1, s5_b_im_l1,
           s5_c_re_l1, s5_c_im_l1, s5_d_l1, w_glu_l1,
           router_l1, router_bias_l1, exp_w13_l1, exp_w2_l1, shared_w13_l1, shared_w2_l1,
           final_norm):
    x = (x_prompt.reshape(T_CTX, D_MODEL), x_sample.reshape(T_LAT, D_MODEL))
    cond = jnp.concatenate([c_ctx[None, :], c, jnp.zeros((N_COND - 1 - DEC_BATCH, D_MODEL), F32)], axis=0)

    mods = _ada_mods(cond, ada_w_l0, ada_b_l0)
    proj = _norm_mod_matmul(x, norm1_l0, mods, 0, 1, w_in_l0.astype(BF16))

    q_c, k_c, v_c, new_k = _qk_prep(proj, 0, T_CTX, q_norm_l0, k_norm_l0, None)
    attn_c = _attention(q_c, k_c, v_c, BATCH, SEQ, None)
    q_l, k_l, v_l = _qk_prep(proj, T_CTX, T_LAT, q_norm_l0, k_norm_l0, _rope_tables(DEC_SEQ))
    ctx_kv = (cache_k_l0.reshape(DEC_BATCH * PAST_LEN, KV_WIDTH).astype(BF16),
              cache_v_l0.reshape(DEC_BATCH * PAST_LEN, KV_WIDTH).astype(BF16))
    attn_l = _attention(q_l, k_l, v_l, DEC_BATCH, DEC_SEQ, ctx_kv)

    hy_f = (hy_ffn_w1_l0, hy_ffn_b1_l0, hy_ffn_w2_l0, hy_ffn_b2_l0, hy_ffn_w3_l0, hy_sin_freq_l0)
    spec_c, mats_c = _hyena_operands(SEQ, HY_BLOCK_CTX, *hy_f)
    z_c = _hyena(proj, 0, BATCH, SEQ, HY_BLOCK_CTX, HY_TC_CTX, hy_conv_w_l0, hy_conv_b_l0, hy_bias_l0,
                 spec_c, mats_c)
    spec_l, mats_l = _hyena_operands(DEC_SEQ, HY_BLOCK_LAT, *hy_f)
    z_l = _hyena(proj, T_CTX, DEC_BATCH, DEC_SEQ, HY_BLOCK_LAT, HY_TC_LAT, hy_conv_w_l0, hy_conv_b_l0, hy_bias_l0,
                 spec_l, mats_l)

    w_out = w_out_l0.astype(BF16)
    x = _out_proj((attn_c, attn_l), (z_c, z_l), x, mods, 2, w_out[:Q_WIDTH], w_out[Q_WIDTH:])
    x = _moe(x, norm2_l0, mods, 3, 4, 5, router_l0, router_bias_l0, exp_w13_l0, exp_w2_l0,
             shared_w13_l0, shared_w2_l0)

    new_v = proj[:T_CTX, Q_WIDTH + KV_WIDTH:QKV_WIDTH]

    mods = _ada_mods(cond, ada_w_l1, ada_b_l1)
    u = _norm_mod_matmul(x, norm1_l1, mods, 0, 1, w_in_l1.astype(BF16))
    s5_w = _s5_weights(s5_a_re_l1, s5_a_im_l1, s5_log_dt_l1, s5_b_re_l1, s5_b_im_l1, s5_c_re_l1, s5_c_im_l1)
    zero_state = jnp.zeros((BATCH, 1, S5_LANES), F32)
    ys, finals = [], []
    for d in range(2):
        y_c, f_re, f_im = _s5_direction(u, 0, BATCH, SEQ, zero_state, zero_state, *s5_w[d], reverse=d == 1)
        y_l, _, _ = _s5_direction(u, T_CTX, DEC_BATCH, DEC_SEQ,
                                  state_s5_re_l1[:, d].reshape(DEC_BATCH, 1, S5_LANES),
                                  state_s5_im_l1[:, d].reshape(DEC_BATCH, 1, S5_LANES), *s5_w[d], reverse=d == 1)
        ys.append((y_c, y_l))
        finals.append((f_re.reshape(BATCH, S5_GROUPS, S5_STATE), f_im.reshape(BATCH, S5_GROUPS, S5_STATE)))
    s_re = jnp.stack([finals[0][0], finals[1][0]], axis=1)
    s_im = jnp.stack([finals[0][1], finals[1][1]], axis=1)
    x = _glu_proj(ys[0], ys[1], u, s5_d_l1, x, mods, 2, w_glu_l1.astype(BF16))
    x = _moe(x, norm2_l1, mods, 3, 4, 5, router_l1, router_bias_l1, exp_w13_l1, exp_w2_l1,
             shared_w13_l1, shared_w2_l1)

    return (_final_norm(x, final_norm, 0, T_CTX).reshape(BATCH, SEQ, D_MODEL),
            _final_norm(x, final_norm, T_CTX, T_LAT).reshape(DEC_BATCH, DEC_SEQ, D_MODEL),
            new_k.reshape(BATCH, SEQ, N_KV_HEADS, HEAD_DIM),
            new_v.reshape(BATCH, SEQ, N_KV_HEADS, HEAD_DIM),
            s_re, s_im)
```

```python
import functools
import math

import jax
import jax.numpy as jnp
from jax import lax
from jax.experimental import pallas as pl
from jax.experimental.pallas import tpu as pltpu

F32 = jnp.float32
BF16 = jnp.bfloat16

D_MODEL = 2048
BATCH = 32
SEQ = 256
DEC_BATCH = 4
DEC_SEQ = 4096
PAST_LEN = 256
GRID_W = 64
RMS_EPS = 1e-6
HEAD_DIM = 128
N_HEADS = 8
N_KV_HEADS = 2
GQA_GROUP = N_HEADS // N_KV_HEADS
Q_WIDTH = N_HEADS * HEAD_DIM
KV_WIDTH = N_KV_HEADS * HEAD_DIM
QKV_WIDTH = Q_WIDTH + 2 * KV_WIDTH
ROPE_THETA = 10000.0
ROPE_AXIS_DIM = HEAD_DIM // 2
HYENA_WIDTH = D_MODEL // 2
HYENA_ORDER = 2
FILTER_BANDS = 16
DECAY_TARGET = 1e-2
DECAY_PCT_SHORT = 0.3
DECAY_PCT_LONG = 1.5
S5_WIDTH = D_MODEL // 2
S5_GROUP_CH = 16
S5_GROUPS = S5_WIDTH // S5_GROUP_CH
S5_STATE = 64
N_EXPERTS = 64
TOP_K = 8
N_EXPERT_GROUPS = 8
TOPK_GROUPS = 4
EXPERT_HIDDEN = 512
ROUTED_SCALE = 2.5
MOE_BLOCK = 256

T_CTX = BATCH * SEQ
T_LAT = DEC_BATCH * DEC_SEQ
T_ALL = T_CTX + T_LAT
N_COND = 8

ROW_TILE = 256
V7X_VMEM_LIMIT_BYTES = 56 * 1024 * 1024


def _params(semantics):
    return pltpu.CompilerParams(dimension_semantics=semantics, vmem_limit_bytes=V7X_VMEM_LIMIT_BYTES)


def _cond_row(i, tm, row0=0):
    tok = row0 + i * tm
    return jnp.where(tok < T_CTX, 0, 1 + (tok - T_CTX) // DEC_SEQ)


def _mod_spec(k, tm, row0=0):
    return pl.BlockSpec((1, 1, 1, D_MODEL), lambda i: (_cond_row(i, tm, row0), k, 0, 0))


def _resident(shape):
    nd = len(shape)
    return pl.BlockSpec(shape, lambda i: (0,) * nd, pipeline_mode=pl.Buffered(1))


def _silu(x):
    return x * jax.nn.sigmoid(x)


def _norm_mod(x, g, sc, sh):
    ms = jnp.mean(x * x, axis=-1, keepdims=True)
    return (x * lax.rsqrt(ms + RMS_EPS) * g) * (1.0 + sc) + sh


def _ada_kernel(c_ref, w_ref, b_ref, o_ref):
    a = _silu(c_ref[...]).astype(BF16)
    o_ref[...] = jnp.dot(a, w_ref[...].astype(BF16), preferred_element_type=F32) + b_ref[...]


def _ada_mods(cond, w, b):
    n = w.shape[1]
    tn = 1024
    out = pl.pallas_call(
        _ada_kernel,
        grid=(n // tn,),
        in_specs=[pl.BlockSpec((N_COND, D_MODEL), lambda j: (0, 0)),
                  pl.BlockSpec((D_MODEL, tn), lambda j: (0, j)),
                  pl.BlockSpec((1, tn), lambda j: (0, j))],
        out_specs=pl.BlockSpec((N_COND, tn), lambda j: (0, j)),
        out_shape=jax.ShapeDtypeStruct((N_COND, n), F32),
        compiler_params=_params(("parallel",)),
        name="ada_mods",
    )(cond, w, b.reshape(1, n))
    return out.reshape(N_COND, 6, 1, D_MODEL)


def _split_row_specs(width, tm):
    n_ctx = T_CTX // tm
    return [pl.BlockSpec((tm, width), lambda i: (jnp.minimum(i, n_ctx - 1), 0)),
            pl.BlockSpec((tm, width), lambda i: (jnp.maximum(i - n_ctx, 0), 0))]


def _pick_rows(ctx_ref, lat_ref):
    return jnp.where(pl.program_id(0) < T_CTX // ctx_ref.shape[0], ctx_ref[...], lat_ref[...])


def _nmm_kernel(*refs, split):
    if split:
        xc_ref, xl_ref, g_ref, sc_ref, sh_ref, w_ref, o_ref = refs
        x = _pick_rows(xc_ref, xl_ref)
    else:
        x_ref, g_ref, sc_ref, sh_ref, w_ref, o_ref = refs
        x = x_ref[...]
    h = _norm_mod(x, g_ref[...], sc_ref[0, 0], sh_ref[0, 0])
    o_ref[...] = jnp.dot(h.astype(BF16), w_ref[...], preferred_element_type=F32)


def _norm_mod_matmul(x, g, mods4, k_shift, k_scale, w_bf16):
    n = w_bf16.shape[1]
    tm = ROW_TILE
    split = isinstance(x, tuple)
    x_args = list(x) if split else [x]
    x_specs = _split_row_specs(D_MODEL, tm) if split else [pl.BlockSpec((tm, D_MODEL), lambda i: (i, 0))]
    return pl.pallas_call(
        functools.partial(_nmm_kernel, split=split),
        grid=(T_ALL // tm,),
        in_specs=x_specs + [_resident((1, D_MODEL)), _mod_spec(k_scale, tm), _mod_spec(k_shift, tm),
                            _resident((D_MODEL, n))],
        out_specs=pl.BlockSpec((tm, n), lambda i: (i, 0)),
        out_shape=jax.ShapeDtypeStruct((T_ALL, n), F32),
        compiler_params=_params(("parallel",)),
        name="norm_mod_matmul",
    )(*x_args, g.reshape(1, D_MODEL), mods4, mods4, w_bf16)


def _head_norm(xh, g):
    ms = jnp.mean(xh * xh, axis=-1, keepdims=True)
    return xh * lax.rsqrt(ms + RMS_EPS) * g


def _rope(y, cos, sin_signed):
    lane = lax.broadcasted_iota(jnp.int32, y.shape, 1)
    partner = jnp.where(lane % 2 == 0, pltpu.roll(y, HEAD_DIM - 1, 1), pltpu.roll(y, 1, 1))
    return y * cos + partner * sin_signed


def _qkprep_ctx_kernel(qkv_ref, qg_ref, kg_ref, q_out, kb_out, vb_out, kf_out):
    for h in range(N_HEADS):
        sl = slice(h * HEAD_DIM, (h + 1) * HEAD_DIM)
        q_out[:, sl] = _head_norm(qkv_ref[:, sl], qg_ref[...]).astype(BF16)
    for j in range(N_KV_HEADS):
        k = _head_norm(qkv_ref[:, Q_WIDTH + j * HEAD_DIM:Q_WIDTH + (j + 1) * HEAD_DIM], kg_ref[...])
        kf_out[:, j * HEAD_DIM:(j + 1) * HEAD_DIM] = k
        kb_out[:, j * HEAD_DIM:(j + 1) * HEAD_DIM] = k.astype(BF16)
    vb_out[...] = qkv_ref[:, Q_WIDTH + KV_WIDTH:QKV_WIDTH].astype(BF16)


def _qkprep_lat_kernel(qkv_ref, qg_ref, kg_ref, cos_ref, sin_ref, q_out, kb_out, vb_out):
    cos, sin = cos_ref[...], sin_ref[...]
    for h in range(N_HEADS):
        sl = slice(h * HEAD_DIM, (h + 1) * HEAD_DIM)
        q_out[:, sl] = _rope(_head_norm(qkv_ref[:, sl], qg_ref[...]), cos, sin).astype(BF16)
    for j in range(N_KV_HEADS):
        k = _head_norm(qkv_ref[:, Q_WIDTH + j * HEAD_DIM:Q_WIDTH + (j + 1) * HEAD_DIM], kg_ref[...])
        kb_out[:, j * HEAD_DIM:(j + 1) * HEAD_DIM] = _rope(k, cos, sin).astype(BF16)
    vb_out[...] = qkv_ref[:, Q_WIDTH + KV_WIDTH:QKV_WIDTH].astype(BF16)


def _qk_prep(proj, row0, t, q_norm, k_norm, rope_tabs):
    tm = ROW_TILE
    b0 = row0 // tm
    qkv_spec = pl.BlockSpec((tm, QKV_WIDTH), lambda i: (b0 + i, 0))
    gain = _resident((1, HEAD_DIM))
    outs = [pl.BlockSpec((tm, Q_WIDTH), lambda i: (i, 0)),
            pl.BlockSpec((tm, KV_WIDTH), lambda i: (i, 0)),
            pl.BlockSpec((tm, KV_WIDTH), lambda i: (i, 0))]
    shapes = [jax.ShapeDtypeStruct((t, Q_WIDTH), BF16),
              jax.ShapeDtypeStruct((t, KV_WIDTH), BF16),
              jax.ShapeDtypeStruct((t, KV_WIDTH), BF16)]
    qg, kg = q_norm.reshape(1, HEAD_DIM), k_norm.reshape(1, HEAD_DIM)
    if rope_tabs is None:
        return pl.pallas_call(
            _qkprep_ctx_kernel, grid=(t // tm,),
            in_specs=[qkv_spec, gain, gain],
            out_specs=outs + [pl.BlockSpec((tm, KV_WIDTH), lambda i: (i, 0))],
            out_shape=shapes + [jax.ShapeDtypeStruct((t, KV_WIDTH), F32)],
            compiler_params=_params(("parallel",)), name="qk_prep_ctx",
        )(proj, qg, kg)
    cos, sin = rope_tabs
    nq = DEC_SEQ // tm
    tab = pl.BlockSpec((tm, HEAD_DIM), lambda i: (i % nq, 0))
    return pl.pallas_call(
        _qkprep_lat_kernel, grid=(t // tm,),
        in_specs=[qkv_spec, gain, gain, tab, tab],
        out_specs=outs, out_shape=shapes,
        compiler_params=_params(("parallel",)), name="qk_prep_lat",
    )(proj, qg, kg, cos, sin)


def _attn_kernel(*refs, with_ctx):
    if with_ctx:
        q_ref, k_ref, v_ref, ck_ref, cv_ref, o_ref = refs
    else:
        q_ref, k_ref, v_ref, o_ref = refs
    c = HEAD_DIM ** -0.5 * math.log2(math.e)
    nt = (((1,), (1,)), ((), ()))
    q = q_ref[...]
    s = lax.dot_general(q, k_ref[...], nt, preferred_element_type=F32)
    m = jnp.max(s, axis=-1, keepdims=True)
    if with_ctx:
        sc = lax.dot_general(q, ck_ref[...], nt, preferred_element_type=F32)
        m = jnp.maximum(m, jnp.max(sc, axis=-1, keepdims=True))
    p = jnp.exp2((s - m) * c)
    l = jnp.sum(p, axis=-1, keepdims=True)
    o = jnp.dot(p.astype(BF16), v_ref[...], preferred_element_type=F32)
    if with_ctx:
        pc = jnp.exp2((sc - m) * c)
        l = l + jnp.sum(pc, axis=-1, keepdims=True)
        o = o + jnp.dot(pc.astype(BF16), cv_ref[...], preferred_element_type=F32)
    o_ref[...] = (o / l).astype(BF16)


def _attention(q, k, v, n_batch, seq, ctx_kv):
    tq = ROW_TILE
    nq = seq // tq
    t = n_batch * seq
    q_spec = pl.BlockSpec((tq, HEAD_DIM), lambda b, h, i: (b * nq + i, h))
    kv_spec = pl.BlockSpec((seq, HEAD_DIM), lambda b, h, i: (b, h // GQA_GROUP))
    in_specs = [q_spec, kv_spec, kv_spec]
    args = [q, k, v]
    if ctx_kv is not None:
        c_spec = pl.BlockSpec((PAST_LEN, HEAD_DIM), lambda b, h, i: (b, h // GQA_GROUP))
        in_specs += [c_spec, c_spec]
        args += list(ctx_kv)
    return pl.pallas_call(
        functools.partial(_attn_kernel, with_ctx=ctx_kv is not None),
        grid=(n_batch, N_HEADS, nq),
        in_specs=in_specs,
        out_specs=pl.BlockSpec((tq, HEAD_DIM), lambda b, h, i: (b * nq + i, h)),
        out_shape=jax.ShapeDtypeStruct((t, Q_WIDTH), BF16),
        compiler_params=_params(("parallel", "parallel", "parallel")),
        name="attention_lat" if ctx_kv is not None else "attention_ctx",
    )(*args)


def _outproj_kernel(ac_ref, al_ref, zc_ref, zl_ref, xc_ref, xl_ref, gate_ref, wa_ref, wz_ref, o_ref):
    acc = jnp.dot(_pick_rows(ac_ref, al_ref), wa_ref[...], preferred_element_type=F32)
    acc = acc + jnp.dot(_pick_rows(zc_ref, zl_ref), wz_ref[...], preferred_element_type=F32)
    o_ref[...] = _pick_rows(xc_ref, xl_ref) + gate_ref[0, 0] * acc


def _out_proj(attn, z, x, mods4, k_gate, wa_bf16, wz_bf16):
    tm = ROW_TILE
    return pl.pallas_call(
        _outproj_kernel,
        grid=(T_ALL // tm,),
        in_specs=(_split_row_specs(Q_WIDTH, tm) + _split_row_specs(HYENA_WIDTH, tm) + _split_row_specs(D_MODEL, tm)
                  + [_mod_spec(k_gate, tm), _resident((Q_WIDTH, D_MODEL)), _resident((HYENA_WIDTH, D_MODEL))]),
        out_specs=pl.BlockSpec((tm, D_MODEL), lambda i: (i, 0)),
        out_shape=jax.ShapeDtypeStruct((T_ALL, D_MODEL), F32),
        compiler_params=_params(("parallel",)),
        name="out_proj",
    )(*attn, *z, *x, mods4, wa_bf16, wz_bf16)


def _gelu_tanh(x):
    return 0.5 * x * (1.0 + jnp.tanh(math.sqrt(2.0 / math.pi) * (x + 0.044715 * (x * x * x))))


def _glu_kernel(yfc_ref, yfl_ref, ybc_ref, ybl_ref, u_ref, d_ref, xc_ref, xl_ref, gate_ref, w_ref, o_ref):
    y = (_pick_rows(yfc_ref, yfl_ref) + _pick_rows(ybc_ref, ybl_ref)) + d_ref[...] * u_ref[...]
    ag = jnp.dot(_gelu_tanh(y).astype(BF16), w_ref[...], preferred_element_type=F32)
    a, g = ag[:, :D_MODEL], ag[:, D_MODEL:]
    o_ref[...] = _pick_rows(xc_ref, xl_ref) + gate_ref[0, 0] * (a * jax.nn.sigmoid(g))


def _glu_proj(y_fwd, y_bwd, u, d_skip, x, mods4, k_gate, w_bf16):
    tm = ROW_TILE
    return pl.pallas_call(
        _glu_kernel,
        grid=(T_ALL // tm,),
        in_specs=(_split_row_specs(S5_WIDTH, tm) + _split_row_specs(S5_WIDTH, tm)
                  + [pl.BlockSpec((tm, S5_WIDTH), lambda i: (i, 0)), _resident((1, S5_WIDTH))]
                  + _split_row_specs(D_MODEL, tm)
                  + [_mod_spec(k_gate, tm), _resident((S5_WIDTH, 2 * D_MODEL))]),
        out_specs=pl.BlockSpec((tm, D_MODEL), lambda i: (i, 0)),
        out_shape=jax.ShapeDtypeStruct((T_ALL, D_MODEL), F32),
        compiler_params=_params(("parallel",)),
        name="glu_proj",
    )(*y_fwd, *y_bwd, u, d_skip.reshape(1, S5_WIDTH), *x, mods4, w_bf16)


def _gmm_kernel(be_ref, nreal_ref, xs_ref, w13_ref, w2_ref, o_ref, w13_b, w2_b):
    i = pl.program_id(0)
    real = i < nreal_ref[0]
    fresh = jnp.logical_and(real, jnp.logical_or(i == 0, be_ref[i] != be_ref[jnp.maximum(i - 1, 0)]))

    @pl.when(fresh)
    def _():
        w13_b[...] = w13_ref[0].astype(BF16)
        w2_b[...] = w2_ref[0].astype(BF16)

    @pl.when(real)
    def _():
        gu = jnp.dot(xs_ref[...], w13_b[...], preferred_element_type=F32)
        hmid = _silu(gu[:, :EXPERT_HIDDEN]) * gu[:, EXPERT_HIDDEN:]
        o_ref[...] = jnp.dot(hmid.astype(BF16), w2_b[...], preferred_element_type=F32).astype(BF16)

    @pl.when(jnp.logical_not(real))
    def _():
        o_ref[...] = jnp.zeros_like(o_ref)


def _grouped_experts(blk_e, n_real, xs, w13, w2):
    n_rows = xs.shape[0]
    n_blocks = n_rows // MOE_BLOCK

    def live(i, nreal):
        return jnp.minimum(i, nreal[0] - 1)

    grid_spec = pltpu.PrefetchScalarGridSpec(
        num_scalar_prefetch=2,
        grid=(n_blocks,),
        in_specs=[pl.BlockSpec((MOE_BLOCK, D_MODEL), lambda i, be, nr: (live(i, nr), 0)),
                  pl.BlockSpec((1, D_MODEL, 2 * EXPERT_HIDDEN), lambda i, be, nr: (be[live(i, nr)], 0, 0)),
                  pl.BlockSpec((1, EXPERT_HIDDEN, D_MODEL), lambda i, be, nr: (be[live(i, nr)], 0, 0))],
        out_specs=pl.BlockSpec((MOE_BLOCK, D_MODEL), lambda i, be, nr: (i, 0)),
        scratch_shapes=[pltpu.VMEM((D_MODEL, 2 * EXPERT_HIDDEN), BF16),
                        pltpu.VMEM((EXPERT_HIDDEN, D_MODEL), BF16)],
    )
    return pl.pallas_call(
        _gmm_kernel,
        grid_spec=grid_spec,
        out_shape=jax.ShapeDtypeStruct((n_rows, D_MODEL), BF16),
        compiler_params=_params(("arbitrary",)),
        name="grouped_experts",
    )(blk_e, n_real, xs, w13, w2)


COMBINE_TILE = 256


def _shared_kernel(h_ref, x_ref, ye_ref, w_ref, gate_ref, w13_ref, w2_ref, o_ref):
    gu = jnp.dot(h_ref[...], w13_ref[...], preferred_element_type=F32)
    hmid = _silu(gu[:, :EXPERT_HIDDEN]) * gu[:, EXPERT_HIDDEN:]
    acc = jnp.dot(hmid.astype(BF16), w2_ref[...], preferred_element_type=F32)
    wts = w_ref[...]
    for k in range(TOP_K):
        acc = acc + wts[:, k:k + 1] * ye_ref[k].astype(F32)
    o_ref[...] = x_ref[...] + gate_ref[0, 0] * acc


def _shared_and_combine(h_bf16, x, row0, y_exp, wts, mods4, k_gate, w13_bf16, w2_bf16):
    t = h_bf16.shape[0]
    tm = COMBINE_TILE
    b0 = row0 // tm
    row = lambda i: (i, 0)
    return pl.pallas_call(
        _shared_kernel,
        grid=(t // tm,),
        in_specs=[pl.BlockSpec((tm, D_MODEL), row), pl.BlockSpec((tm, D_MODEL), lambda i: (b0 + i, 0)),
                  pl.BlockSpec((TOP_K, tm, D_MODEL), lambda i: (0, i, 0)),
                  pl.BlockSpec((tm, TOP_K), row), _mod_spec(k_gate, tm, row0),
                  _resident((D_MODEL, 2 * EXPERT_HIDDEN)), _resident((EXPERT_HIDDEN, D_MODEL))],
        out_specs=pl.BlockSpec((tm, D_MODEL), row),
        out_shape=jax.ShapeDtypeStruct((t, D_MODEL), F32),
        compiler_params=_params(("parallel",)),
        name="shared_expert_combine",
    )(h_bf16, x, y_exp, wts, mods4, w13_bf16, w2_bf16)


ROUTE_TILE = 256
GROUP_SIZE = N_EXPERTS // N_EXPERT_GROUPS
NEG_INF = float("-inf")


def _first_argmax(v, index, sentinel):
    m = jnp.max(v, axis=0, keepdims=True)
    first = jnp.min(jnp.where(v == m, index, sentinel), axis=0, keepdims=True)
    return m, first


def _router_kernel(x_ref, g_ref, sc_ref, sh_ref, wt_ref, rb_ref, tri_ref,
                   h_out, idx_out, gate_out, rank_out, cnt_out, base_ref):
    tm = x_ref.shape[0]

    @pl.when(pl.program_id(0) == 0)
    def _():
        base_ref[...] = jnp.zeros_like(base_ref)

    hb = _norm_mod(x_ref[...], g_ref[...], sc_ref[0, 0], sh_ref[0, 0]).astype(BF16)
    h_out[...] = hb
    logits = lax.dot_general(wt_ref[...], hb, (((1,), (1,)), ((), ())), preferred_element_type=F32)
    scores = jax.nn.sigmoid(logits)
    choice = scores + rb_ref[...]

    member = lax.broadcasted_iota(jnp.int32, (GROUP_SIZE, tm), 0).astype(F32)
    group_scores = []
    for g in range(N_EXPERT_GROUPS):
        cg = choice[g * GROUP_SIZE:(g + 1) * GROUP_SIZE, :]
        m1, first = _first_argmax(cg, member, float(GROUP_SIZE))
        m2 = jnp.max(jnp.where(member == first, NEG_INF, cg), axis=0, keepdims=True)
        group_scores.append(m1 + m2)
    gs = jnp.concatenate(group_scores, axis=0)

    group = lax.broadcasted_iota(jnp.int32, (N_EXPERT_GROUPS, tm), 0).astype(F32)
    keep = jnp.zeros((N_EXPERT_GROUPS, tm), F32)
    for _ in range(TOPK_GROUPS):
        _, first = _first_argmax(gs, group, float(N_EXPERT_GROUPS))
        sel = group == first
        keep = jnp.where(sel, 1.0, keep)
        gs = jnp.where(sel, NEG_INF, gs)
    masked = jnp.concatenate(
        [jnp.where(keep[g:g + 1, :] > 0.0, choice[g * GROUP_SIZE:(g + 1) * GROUP_SIZE, :], NEG_INF)
         for g in range(N_EXPERT_GROUPS)], axis=0)

    expert = lax.broadcasted_iota(jnp.int32, (N_EXPERTS, tm), 0).astype(F32)
    onehot = jnp.zeros((N_EXPERTS, tm), F32)
    sels, idx_rows, gate_rows = [], [], []
    for _ in range(TOP_K):
        _, first = _first_argmax(masked, expert, float(N_EXPERTS))
        sel = expert == first
        sels.append(sel)
        idx_rows.append(first)
        gate_rows.append(jnp.sum(jnp.where(sel, scores, 0.0), axis=0, keepdims=True))
        masked = jnp.where(sel, NEG_INF, masked)
        onehot = jnp.where(sel, 1.0, onehot)
    gates = jnp.concatenate(gate_rows, axis=0)
    gate_out[...] = gates / jnp.sum(gates, axis=0, keepdims=True) * ROUTED_SCALE
    idx_out[...] = jnp.concatenate(idx_rows, axis=0).astype(jnp.int32)

    cum = jnp.dot(onehot.astype(BF16), tri_ref[...], preferred_element_type=F32) + base_ref[:, 0:1]
    rank_rows = [jnp.sum(jnp.where(sel, cum, 0.0), axis=0, keepdims=True) - 1.0 for sel in sels]
    rank_out[...] = jnp.concatenate(rank_rows, axis=0).astype(jnp.int32)
    base_ref[...] = jnp.broadcast_to(cum[:, tm - 1:tm], base_ref.shape)
    cnt_out[...] = base_ref[...].astype(jnp.int32)


def _router(x, row0, t, norm_g, mods4, k_shift, k_scale, router_w_t, router_bias):
    tm = ROUTE_TILE
    b0 = row0 // tm
    tri = (jnp.arange(tm)[:, None] <= jnp.arange(tm)[None, :]).astype(BF16)
    choice_rows = pl.BlockSpec((TOP_K, tm), lambda i: (0, i))
    return pl.pallas_call(
        _router_kernel,
        grid=(t // tm,),
        in_specs=[pl.BlockSpec((tm, D_MODEL), lambda i: (b0 + i, 0)), _resident((1, D_MODEL)),
                  _mod_spec(k_scale, tm, row0), _mod_spec(k_shift, tm, row0),
                  _resident((N_EXPERTS, D_MODEL)), _resident((N_EXPERTS, 1)), _resident((tm, tm))],
        out_specs=[pl.BlockSpec((tm, D_MODEL), lambda i: (i, 0)), choice_rows, choice_rows, choice_rows,
                   pl.BlockSpec((N_EXPERTS, 128), lambda i: (0, 0))],
        out_shape=[jax.ShapeDtypeStruct((t, D_MODEL), BF16),
                   jax.ShapeDtypeStruct((TOP_K, t), jnp.int32), jax.ShapeDtypeStruct((TOP_K, t), F32),
                   jax.ShapeDtypeStruct((TOP_K, t), jnp.int32), jax.ShapeDtypeStruct((N_EXPERTS, 128), jnp.int32)],
        scratch_shapes=[pltpu.VMEM((N_EXPERTS, 128), F32)],
        compiler_params=_params(("arbitrary",)),
        name="router",
    )(x, norm_g.reshape(1, D_MODEL), mods4, mods4, router_w_t, router_bias.reshape(N_EXPERTS, 1), tri)


def _pos_kernel(idx_ref, rank_ref, off_ref, pos_out):
    tm = idx_ref.shape[1]
    expert = lax.broadcasted_iota(jnp.int32, (N_EXPERTS, tm), 0)
    rows = []
    for k in range(TOP_K):
        sel = expert == idx_ref[k:k + 1, :]
        rows.append(jnp.sum(jnp.where(sel, off_ref[...], 0.0), axis=0, keepdims=True))
    pos_out[...] = jnp.concatenate(rows, axis=0).astype(jnp.int32) + rank_ref[...]


def _positions(idx, rank, offsets):
    t = idx.shape[1]
    tm = 1024
    rows = pl.BlockSpec((TOP_K, tm), lambda i: (0, i))
    return pl.pallas_call(
        _pos_kernel, grid=(t // tm,),
        in_specs=[rows, rows, _resident((N_EXPERTS, 1))],
        out_specs=rows, out_shape=jax.ShapeDtypeStruct((TOP_K, t), jnp.int32),
        compiler_params=_params(("parallel",)), name="expert_positions",
    )(idx, rank, offsets.astype(F32).reshape(N_EXPERTS, 1))


def _moe(x, row0, t, norm_g, mods4, k_shift, k_scale, k_gate, router_w_t, router_bias, exp_w13, exp_w2,
         sh_w13_bf16, sh_w2_bf16):
    h_bf16, top_idx, gate, rank, counts = _router(x, row0, t, norm_g, mods4, k_shift, k_scale, router_w_t,
                                                  router_bias)
    counts = counts[:, 0]
    padded = (counts + MOE_BLOCK - 1) // MOE_BLOCK * MOE_BLOCK
    e_iota = jnp.arange(N_EXPERTS)
    pend = jnp.sum(jnp.where(e_iota[:, None] <= e_iota[None, :], padded[:, None], 0), axis=0)
    pos = _positions(top_idx, rank, pend - padded)
    n_rows = -(-(t * TOP_K + N_EXPERTS * (MOE_BLOCK - 1)) // MOE_BLOCK) * MOE_BLOCK
    n_blocks = n_rows // MOE_BLOCK
    blk_start = jnp.arange(n_blocks, dtype=pend.dtype) * MOE_BLOCK
    blk_e = jnp.minimum(jnp.sum((pend[None, :] <= blk_start[:, None]).astype(jnp.int32), axis=1), N_EXPERTS - 1)
    n_real = (pend[N_EXPERTS - 1:] // MOE_BLOCK).astype(jnp.int32)
    tok = jnp.broadcast_to(jnp.arange(t, dtype=jnp.int32)[None, :], (TOP_K, t))
    _, sorted_tok = lax.sort_key_val(pos.reshape(-1), tok.reshape(-1))
    dense_end = jnp.sum(jnp.where(e_iota[:, None] <= e_iota[None, :], counts[:, None], 0), axis=0)
    shift_blk = ((pend - padded) - (dense_end - counts))[blk_e]
    end_blk = dense_end[blk_e]
    row = jnp.arange(n_rows, dtype=jnp.int32)
    dense = row - jnp.repeat(shift_blk, MOE_BLOCK)
    row_tok = jnp.where(dense < jnp.repeat(end_blk, MOE_BLOCK),
                        sorted_tok[jnp.clip(dense, 0, t * TOP_K - 1)], row % t)
    xs = h_bf16[row_tok]
    ys = _grouped_experts(blk_e, n_real, xs, exp_w13, exp_w2)
    y_exp = ys[pos.reshape(-1)].reshape(TOP_K, t, D_MODEL)
    return _shared_and_combine(h_bf16, x, row0, y_exp, gate.T, mods4, k_gate, sh_w13_bf16, sh_w2_bf16)


def _moe_streams(x, norm_g, mods4, k_shift, k_scale, k_gate, router_w, router_bias, exp_w13, exp_w2, sh_w13, sh_w2):
    args = (norm_g, mods4, k_shift, k_scale, k_gate, router_w.T.astype(BF16), router_bias, exp_w13, exp_w2,
            sh_w13.astype(BF16), sh_w2.astype(BF16))
    return _moe(x, 0, T_CTX, *args), _moe(x, T_CTX, T_LAT, *args)


HY_BLOCK_CTX, HY_TC_CTX = 256, 512
HY_BLOCK_LAT, HY_TC_LAT = 512, 128


def _hyena_filters(L, w1, b1, w2, b2, w3, sin_freq):
    t_norm = jnp.linspace(0.0, 1.0, L, dtype=F32)[:, None]
    omega = 2.0 * math.pi * jnp.arange(L, dtype=F32)[:, None] / L
    bands = jnp.linspace(1e-4, FILTER_BANDS - 1, FILTER_BANDS, dtype=F32)[None, :]
    z = jnp.concatenate([t_norm, jnp.cos(bands * omega), -jnp.sin(bands * omega)], axis=-1)
    h = jnp.sin(sin_freq * (z @ w1 + b1))
    h = jnp.sin(sin_freq * (h @ w2 + b2))
    h = (h @ w3).reshape(L, 2, HYENA_ORDER, HYENA_WIDTH)
    deltas = jnp.abs(jnp.linspace(math.log(DECAY_TARGET) / DECAY_PCT_LONG,
                                  math.log(DECAY_TARGET) / DECAY_PCT_SHORT, HYENA_WIDTH, dtype=F32))
    h = h * jnp.exp(-t_norm * deltas)[:, None, None, :]
    fwd, bwd = h[:, 0], h[:, 1] * (t_norm > 0.0)[:, :, None]
    norm = jnp.sum(jnp.abs(fwd), axis=0, keepdims=True) + jnp.sum(jnp.abs(bwd), axis=0, keepdims=True)
    return fwd / norm, bwd / norm


HY_ACC_ELEMS = 8192


def _dft_matrices(bk):
    k = jnp.arange(bk, dtype=jnp.int32)
    phase = ((2 * k[:, None] + 1) * k[None, :]) % (4 * bk)
    ang = phase.astype(F32) * (math.pi / (2 * bk))
    return jnp.cos(ang), jnp.sin(ang)


def _spectra_kernel(xf_ref, xb_ref, yf_ref, yb_ref, c1_ref, s1_ref, c2_ref, s2_ref, o_ref, *, nb):
    d = pl.program_id(1) - (nb - 1)
    x = jnp.where(d >= 0, xf_ref[0], xb_ref[0]).astype(BF16)
    y = jnp.where(d >= 1, yf_ref[0], yb_ref[0]).astype(BF16)
    cy = jnp.where(d == 0, c1_ref[...], c2_ref[...])
    sy = jnp.where(d == 0, s1_ref[...], s2_ref[...])
    re = jnp.dot(c1_ref[...], x, preferred_element_type=F32) + jnp.dot(cy, y, preferred_element_type=F32)
    im = jnp.dot(sy, y, preferred_element_type=F32) - jnp.dot(s1_ref[...], x, preferred_element_type=F32)
    o_ref[0, 0, 0] = re
    o_ref[0, 0, 1] = jnp.where(d >= 0, im, -im)


def _filter_spectra(fwd, bwd, bk, tc):
    L = fwd.shape[0]
    nb = L // bk
    n_ct = HYENA_WIDTH // tc
    cos, sin = _dft_matrices(bk)
    k = jnp.arange(bk, dtype=jnp.int32)
    phase = ((2 * k[:, None] + 1) * (bk - k[None, :])) % (4 * bk)
    ang = phase.astype(F32) * (math.pi / (2 * bk))
    live = (k[None, :] > 0).astype(F32)
    mats = [m.astype(BF16) for m in (cos, sin, jnp.cos(ang) * live, jnp.sin(ang) * live)]

    def seg(pick):
        return pl.BlockSpec((1, bk, tc), lambda o, di, ct: (pick(di - (nb - 1)), 0, o * n_ct + ct))

    mat = pl.BlockSpec((bk, bk), lambda o, di, ct: (0, 0), pipeline_mode=pl.Buffered(1))
    halves = [h.reshape(nb, bk, HYENA_ORDER * HYENA_WIDTH) for h in (fwd, bwd)]
    return pl.pallas_call(
        functools.partial(_spectra_kernel, nb=nb),
        grid=(HYENA_ORDER, 2 * nb - 1, n_ct),
        in_specs=[seg(lambda d: jnp.maximum(d, 0)), seg(lambda d: jnp.maximum(-d, 0)),
                  seg(lambda d: jnp.maximum(d - 1, 0)), seg(lambda d: jnp.maximum(-d - 1, 0)),
                  mat, mat, mat, mat],
        out_specs=pl.BlockSpec((1, 1, 2, bk, tc), lambda o, di, ct: (o, di, 0, 0, ct)),
        out_shape=jax.ShapeDtypeStruct((HYENA_ORDER, 2 * nb - 1, 2, bk, HYENA_WIDTH), F32),
        compiler_params=_params(("parallel", "parallel", "parallel")),
        name="filter_spectra",
    )(halves[0], halves[1], halves[0], halves[1], *mats)


def _hyena_kernel(v_ref, x1_ref, x2_ref, cw_ref, cb_ref, hb_ref, k0_ref, k1_ref, c_ref, s_ref, ct_ref, st_ref,
                  o_ref, vs_ref, z1_ref, x2s_ref, ur_ref, ui_ref, yr_ref, yi_ref, *, seq, bk, tc):
    nb = seq // bk
    ft = HY_ACC_ELEMS // tc
    row = lax.broadcasted_iota(jnp.int32, (seq, tc), 0)

    def short_conv(x, j):
        prev = jnp.where(row == 0, 0.0, pltpu.roll(x, 1, 0))
        nxt = jnp.where(row == seq - 1, 0.0, pltpu.roll(x, seq - 1, 0))
        return (prev * cw_ref[0, j:j + 1, :] + x * cw_ref[1, j:j + 1, :] + nxt * cw_ref[2, j:j + 1, :]
                + cb_ref[j:j + 1, :])

    vs_ref[...] = short_conv(v_ref[...], 0)
    z1_ref[...] = short_conv(x1_ref[...], 1)
    x2s_ref[...] = short_conv(x2_ref[...], 2)

    def long_conv(src_ref, k_ref, emit):
        for blk in range(nb):
            ub = src_ref[blk * bk:(blk + 1) * bk, :].astype(BF16)
            ur_ref[blk] = jnp.dot(c_ref[...], ub, preferred_element_type=F32)
            ui_ref[blk] = -jnp.dot(s_ref[...], ub, preferred_element_type=F32)
        for out_blk in range(nb):
            def acc_tile(f, carry, out_blk=out_blk):
                rows = pl.ds(pl.multiple_of(f * ft, ft), ft)
                ar = jnp.zeros((ft, tc), F32)
                ai = jnp.zeros((ft, tc), F32)
                for in_blk in range(nb):
                    d = out_blk - in_blk + nb - 1
                    kr, ki = k_ref[d, 0, rows, :], k_ref[d, 1, rows, :]
                    xr, xi = ur_ref[in_blk, rows, :], ui_ref[in_blk, rows, :]
                    ar = ar + (kr * xr - ki * xi)
                    ai = ai + (kr * xi + ki * xr)
                yr_ref[rows, :] = ar
                yi_ref[rows, :] = ai
                return carry

            lax.fori_loop(0, bk // ft, acc_tile, 0)
            y = jnp.dot(ct_ref[...], yr_ref[...].astype(BF16), preferred_element_type=F32)
            y = y - jnp.dot(st_ref[...], yi_ref[...].astype(BF16), preferred_element_type=F32)
            emit(slice(out_blk * bk, (out_blk + 1) * bk), y * (1.0 / bk))

    def emit_z1(rows, y):
        z1_ref[rows, :] = z1_ref[rows, :] * (y + hb_ref[0:1, :] * vs_ref[rows, :])

    def emit_out(rows, y):
        o_ref[rows, :] = (x2s_ref[rows, :] * (y + hb_ref[1:2, :] * z1_ref[rows, :])).astype(BF16)

    long_conv(vs_ref, k0_ref, emit_z1)
    long_conv(z1_ref, k1_ref, emit_out)


def _hyena(proj, row0, n_batch, seq, bk, tc, conv_w, conv_b, hy_bias, spectra, mats):
    nb = seq // bk
    n_ct = HYENA_WIDTH // tc
    col0 = QKV_WIDTH // tc
    per = HYENA_WIDTH // tc
    b0 = row0 // seq

    def data(which):
        return pl.BlockSpec((seq, tc), lambda ct, b: (b0 + b, col0 + which * per + ct))

    const2 = lambda shape: pl.BlockSpec(shape, lambda ct, b: (0, 0), pipeline_mode=pl.Buffered(1))
    kspec = pl.BlockSpec((2 * nb - 1, 2, bk, tc), lambda ct, b: (0, 0, 0, ct), pipeline_mode=pl.Buffered(1))
    cw = conv_w.reshape(3, HYENA_ORDER + 1, HYENA_WIDTH)
    cb = conv_b.reshape(HYENA_ORDER + 1, HYENA_WIDTH)
    return pl.pallas_call(
        functools.partial(_hyena_kernel, seq=seq, bk=bk, tc=tc),
        grid=(n_ct, n_batch),
        in_specs=[data(0), data(1), data(2),
                  pl.BlockSpec((3, HYENA_ORDER + 1, tc), lambda ct, b: (0, 0, ct)),
                  pl.BlockSpec((HYENA_ORDER + 1, tc), lambda ct, b: (0, ct)),
                  pl.BlockSpec((HYENA_ORDER, tc), lambda ct, b: (0, ct)),
                  kspec, kspec, const2((bk, bk)), const2((bk, bk)), const2((bk, bk)), const2((bk, bk))],
        out_specs=pl.BlockSpec((seq, tc), lambda ct, b: (b, ct)),
        out_shape=jax.ShapeDtypeStruct((n_batch * seq, HYENA_WIDTH), BF16),
        scratch_shapes=[pltpu.VMEM((seq, tc), F32), pltpu.VMEM((seq, tc), F32), pltpu.VMEM((seq, tc), F32),
                        pltpu.VMEM((nb, bk, tc), F32), pltpu.VMEM((nb, bk, tc), F32),
                        pltpu.VMEM((bk, tc), F32), pltpu.VMEM((bk, tc), F32)],
        compiler_params=_params(("parallel", "parallel")),
        name="hyena_seq%d" % seq,
    )(proj, proj, proj, cw, cb, hy_bias, spectra[0], spectra[1], *mats)


HY_SPECTRA_TC = 512


def _hyena_operands(L, bk, f_w1, f_b1, f_w2, f_b2, f_w3, sin_freq):
    fwd, bwd = _hyena_filters(L, f_w1, f_b1, f_w2, f_b2, f_w3, sin_freq)
    cos, sin = _dft_matrices(bk)
    mats = tuple(m.astype(BF16) for m in (cos, sin, cos.T, sin.T))
    return _filter_spectra(fwd, bwd, bk, HY_SPECTRA_TC), mats


S5_LANES = S5_GROUPS * S5_STATE
S5_CLUSTER_GROUPS = 16
S5_CLUSTERS = S5_GROUPS // S5_CLUSTER_GROUPS
S5_CLUSTER_CH = S5_CLUSTER_GROUPS * S5_GROUP_CH
S5_CLUSTER_LANES = S5_CLUSTER_GROUPS * S5_STATE
SCAN_CHUNK = 256
SCAN_TILE = 8
SCAN_LEVELS = (1, 2, 4)


def _s5_kernel(u_ref, s0r_ref, s0i_ref, bw_ref, cwr_ref, cwi_ref, tab_ref, y_ref, sfr_ref, sfi_ref,
               sre, sim, car, cai, *, reverse):
    @pl.when(pl.program_id(1) == 0)
    def _():
        car[...] = s0r_ref[0]
        cai[...] = s0i_ref[0]

    ub = u_ref[...].astype(BF16)
    for k in range(S5_CLUSTERS):
        bu = jnp.dot(ub[:, k * S5_CLUSTER_CH:(k + 1) * S5_CLUSTER_CH], bw_ref[k], preferred_element_type=F32)
        sre[:, k * S5_CLUSTER_LANES:(k + 1) * S5_CLUSTER_LANES] = bu[:, :S5_CLUSTER_LANES]
        sim[:, k * S5_CLUSTER_LANES:(k + 1) * S5_CLUSTER_LANES] = bu[:, S5_CLUSTER_LANES:]

    n_tiles = SCAN_CHUNK // SCAN_TILE
    shifts = tuple(SCAN_TILE - k for k in SCAN_LEVELS) if reverse else SCAN_LEVELS
    boundary_row = 0 if reverse else SCAN_TILE - 1
    for k in range(S5_CLUSTERS):
        lanes = slice(k * S5_CLUSTER_LANES, (k + 1) * S5_CLUSTER_LANES)

        def tile_step(i, carry, lanes=lanes):
            cr, ci = carry
            tile = (n_tiles - 1 - i) if reverse else i
            rows = pl.ds(pl.multiple_of(tile * SCAN_TILE, SCAN_TILE), SCAN_TILE)
            xr, xi = sre[rows, lanes], sim[rows, lanes]
            for lvl, sh in enumerate(shifts):
                pr, pi = tab_ref[2 * lvl, :, lanes], tab_ref[2 * lvl + 1, :, lanes]
                rr, ri = pltpu.roll(xr, sh, 0), pltpu.roll(xi, sh, 0)
                xr, xi = xr + (pr * rr - pi * ri), xi + (pr * ri + pi * rr)
            pr, pi = tab_ref[2 * len(shifts), :, lanes], tab_ref[2 * len(shifts) + 1, :, lanes]
            xr, xi = xr + (pr * cr - pi * ci), xi + (pr * ci + pi * cr)
            sre[rows, lanes] = xr
            sim[rows, lanes] = xi
            return xr[boundary_row:boundary_row + 1, :], xi[boundary_row:boundary_row + 1, :]

        cr, ci = lax.fori_loop(0, n_tiles, tile_step, (car[:, lanes], cai[:, lanes]))
        car[:, lanes] = cr
        cai[:, lanes] = ci

    for k in range(S5_CLUSTERS):
        lanes = slice(k * S5_CLUSTER_LANES, (k + 1) * S5_CLUSTER_LANES)
        yk = jnp.dot(sre[:, lanes].astype(BF16), cwr_ref[k], preferred_element_type=F32)
        yk = yk + jnp.dot(sim[:, lanes].astype(BF16), cwi_ref[k], preferred_element_type=F32)
        y_ref[:, k * S5_CLUSTER_CH:(k + 1) * S5_CLUSTER_CH] = yk
    sfr_ref[0] = car[...]
    sfi_ref[0] = cai[...]


def _s5_direction(u, row0, n_seq, seq, s0_re, s0_im, bw, cwr, cwi, tabs, reverse):
    nc = seq // SCAN_CHUNK
    b0 = row0 // SCAN_CHUNK

    def chunk(c):
        return (nc - 1 - c) if reverse else c

    state_spec = pl.BlockSpec((1, 1, S5_LANES), lambda b, c: (b, 0, 0))
    full3 = lambda shape: pl.BlockSpec(shape, lambda b, c: (0, 0, 0), pipeline_mode=pl.Buffered(1))
    return pl.pallas_call(
        functools.partial(_s5_kernel, reverse=reverse),
        grid=(n_seq, nc),
        in_specs=[pl.BlockSpec((SCAN_CHUNK, S5_WIDTH), lambda b, c: (b0 + b * nc + chunk(c), 0)),
                  state_spec, state_spec,
                  full3(bw.shape), full3(cwr.shape), full3(cwi.shape), full3(tabs.shape)],
        out_specs=[pl.BlockSpec((SCAN_CHUNK, S5_WIDTH), lambda b, c: (b * nc + chunk(c), 0)),
                   state_spec, state_spec],
        out_shape=[jax.ShapeDtypeStruct((n_seq * seq, S5_WIDTH), F32),
                   jax.ShapeDtypeStruct((n_seq, 1, S5_LANES), F32),
                   jax.ShapeDtypeStruct((n_seq, 1, S5_LANES), F32)],
        scratch_shapes=[pltpu.VMEM((SCAN_CHUNK, S5_LANES), F32), pltpu.VMEM((SCAN_CHUNK, S5_LANES), F32),
                        pltpu.VMEM((1, S5_LANES), F32), pltpu.VMEM((1, S5_LANES), F32)],
        compiler_params=_params(("parallel", "arbitrary")),
        name="s5_bwd" if reverse else "s5_fwd",
    )(u, s0_re, s0_im, bw, cwr, cwi, tabs)


def _s5_weights(a_re, a_im, log_dt, b_re, b_im, c_re, c_im):
    lam = lax.complex(a_re, a_im)
    lam_bar = jnp.exp(lam * jnp.exp(log_dt)[..., None])
    b_bar = ((lam_bar - 1.0) / lam)[..., None] * lax.complex(b_re, b_im)
    eye = jnp.eye(S5_CLUSTER_GROUPS, dtype=F32)
    out = []
    for d in range(2):
        def cluster_in(w):
            w = w.reshape(S5_CLUSTERS, S5_CLUSTER_GROUPS, S5_STATE, S5_GROUP_CH)
            return jnp.einsum('ab,kapc->kacbp', eye, w).reshape(S5_CLUSTERS, S5_CLUSTER_CH, S5_CLUSTER_LANES)

        def cluster_out(w):
            w = w.reshape(S5_CLUSTERS, S5_CLUSTER_GROUPS, S5_GROUP_CH, S5_STATE)
            return jnp.einsum('ab,kacp->kapbc', eye, w).reshape(S5_CLUSTERS, S5_CLUSTER_LANES, S5_CLUSTER_CH)

        bw = jnp.concatenate([cluster_in(jnp.real(b_bar[d])), cluster_in(jnp.imag(b_bar[d]))], axis=-1)
        cwr = cluster_out(c_re[d])
        cwi = cluster_out(-c_im[d])
        lam_d = lam_bar[d].reshape(S5_LANES)
        pows = [lam_d]
        for _ in range(SCAN_TILE - 1):
            pows.append(pows[-1] * lam_d)
        pows = jnp.stack(pows)
        j = jnp.arange(SCAN_TILE)
        tabs = []
        for k in SCAN_LEVELS:
            valid = (j <= SCAN_TILE - 1 - k) if d == 1 else (j >= k)
            p = jnp.where(valid[:, None], pows[k - 1][None, :], 0.0)
            tabs += [jnp.real(p), jnp.imag(p)]
        pc = pows[::-1] if d == 1 else pows
        tabs += [jnp.real(pc), jnp.imag(pc)]
        out.append((bw.astype(BF16), cwr.astype(BF16), cwi.astype(BF16), jnp.stack(tabs).astype(F32)))
    return out


def _final_norm_kernel(x_ref, g_ref, o_ref):
    x = x_ref[...]
    ms = jnp.mean(x * x, axis=-1, keepdims=True)
    o_ref[...] = x * lax.rsqrt(ms + RMS_EPS) * g_ref[...]


def _final_norm(x, g, row0, t):
    tm = ROW_TILE
    b0 = row0 // tm
    return pl.pallas_call(
        _final_norm_kernel,
        grid=(t // tm,),
        in_specs=[pl.BlockSpec((tm, D_MODEL), lambda i: (b0 + i, 0)), _resident((1, D_MODEL))],
        out_specs=pl.BlockSpec((tm, D_MODEL), lambda i: (i, 0)),
        out_shape=jax.ShapeDtypeStruct((t, D_MODEL), F32),
        compiler_params=_params(("parallel",)),
        name="final_norm",
    )(x, g.reshape(1, D_MODEL))


def _rope_tables(L):
    n_rows = L // GRID_W
    row_idx = jnp.repeat(jnp.arange(n_rows, dtype=F32), GRID_W)
    col_idx = (jnp.arange(L) % GRID_W).astype(F32)
    inv = ROPE_THETA ** (-jnp.arange(0, ROPE_AXIS_DIM, 2, dtype=F32) / ROPE_AXIS_DIM)
    ang = jnp.concatenate([row_idx[:, None] * inv, col_idx[:, None] * inv], axis=-1)
    cos, sin = jnp.cos(ang), jnp.sin(ang)
    cos_full = jnp.repeat(cos, 2, axis=-1)
    sin_signed = jnp.stack([-sin, sin], axis=-1).reshape(L, HEAD_DIM)
    return cos_full, sin_signed


def kernel(x_prompt, x_sample, cache_k_l0, cache_v_l0, state_s5_re_l1, state_s5_im_l1, c, c_ctx,
           ada_w_l0, ada_b_l0, norm1_l0, norm2_l0,
           w_in_l0, w_out_l0, q_norm_l0, k_norm_l0, hy_conv_w_l0, hy_conv_b_l0,
           hy_ffn_w1_l0, hy_ffn_b1_l0, hy_ffn_w2_l0, hy_ffn_b2_l0, hy_ffn_w3_l0, hy_sin_freq_l0, hy_bias_l0,
           router_l0, router_bias_l0, exp_w13_l0, exp_w2_l0, shared_w13_l0, shared_w2_l0,
           ada_w_l1, ada_b_l1, norm1_l1, norm2_l1,
           w_in_l1, s5_a_re_l1, s5_a_im_l1, s5_log_dt_l1, s5_b_re_l1, s5_b_im_l1,
           s5_c_re_l1, s5_c_im_l1, s5_d_l1, w_glu_l1,
           router_l1, router_bias_l1, exp_w13_l1, exp_w2_l1, shared_w13_l1, shared_w2_l1,
           final_norm):
    x = (x_prompt.reshape(T_CTX, D_MODEL), x_sample.reshape(T_LAT, D_MODEL))
    cond = jnp.concatenate([c_ctx[None, :], c, jnp.zeros((N_COND - 1 - DEC_BATCH, D_MODEL), F32)], axis=0)

    mods = _ada_mods(cond, ada_w_l0, ada_b_l0)
    proj = _norm_mod_matmul(x, norm1_l0, mods, 0, 1, w_in_l0.astype(BF16))

    q_c, k_c, v_c, new_k = _qk_prep(proj, 0, T_CTX, q_norm_l0, k_norm_l0, None)
    attn_c = _attention(q_c, k_c, v_c, BATCH, SEQ, None)
    q_l, k_l, v_l = _qk_prep(proj, T_CTX, T_LAT, q_norm_l0, k_norm_l0, _rope_tables(DEC_SEQ))
    ctx_kv = (cache_k_l0.reshape(DEC_BATCH * PAST_LEN, KV_WIDTH).astype(BF16),
              cache_v_l0.reshape(DEC_BATCH * PAST_LEN, KV_WIDTH).astype(BF16))
    attn_l = _attention(q_l, k_l, v_l, DEC_BATCH, DEC_SEQ, ctx_kv)

    hy_f = (hy_ffn_w1_l0, hy_ffn_b1_l0, hy_ffn_w2_l0, hy_ffn_b2_l0, hy_ffn_w3_l0, hy_sin_freq_l0)
    spec_c, mats_c = _hyena_operands(SEQ, HY_BLOCK_CTX, *hy_f)
    z_c = _hyena(proj, 0, BATCH, SEQ, HY_BLOCK_CTX, HY_TC_CTX, hy_conv_w_l0, hy_conv_b_l0, hy_bias_l0,
                 spec_c, mats_c)
    spec_l, mats_l = _hyena_operands(DEC_SEQ, HY_BLOCK_LAT, *hy_f)
    z_l = _hyena(proj, T_CTX, DEC_BATCH, DEC_SEQ, HY_BLOCK_LAT, HY_TC_LAT, hy_conv_w_l0, hy_conv_b_l0, hy_bias_l0,
                 spec_l, mats_l)

    w_out = w_out_l0.astype(BF16)
    x = _out_proj((attn_c, attn_l), (z_c, z_l), x, mods, 2, w_out[:Q_WIDTH], w_out[Q_WIDTH:])
    x = _moe_streams(x, norm2_l0, mods, 3, 4, 5, router_l0, router_bias_l0, exp_w13_l0, exp_w2_l0,
                     shared_w13_l0, shared_w2_l0)

    new_v = proj[:T_CTX, Q_WIDTH + KV_WIDTH:QKV_WIDTH]

    mods = _ada_mods(cond, ada_w_l1, ada_b_l1)
    u = _norm_mod_matmul(x, norm1_l1, mods, 0, 1, w_in_l1.astype(BF16))
    s5_w = _s5_weights(s5_a_re_l1, s5_a_im_l1, s5_log_dt_l1, s5_b_re_l1, s5_b_im_l1, s5_c_re_l1, s5_c_im_l1)
    zero_state = jnp.zeros((BATCH, 1, S5_LANES), F32)
    ys, finals = [], []
    for d in range(2):
        y_c, f_re, f_im = _s5_direction(u, 0, BATCH, SEQ, zero_state, zero_state, *s5_w[d], reverse=d == 1)
        y_l, _, _ = _s5_direction(u, T_CTX, DEC_BATCH, DEC_SEQ,
                                  state_s5_re_l1[:, d].reshape(DEC_BATCH, 1, S5_LANES),
                                  state_s5_im_l1[:, d].reshape(DEC_BATCH, 1, S5_LANES), *s5_w[d], reverse=d == 1)
        ys.append((y_c, y_l))
        finals.append((f_re.reshape(BATCH, S5_GROUPS, S5_STATE), f_im.reshape(BATCH, S5_GROUPS, S5_STATE)))
    s_re = jnp.stack([finals[0][0], finals[1][0]], axis=1)
    s_im = jnp.stack([finals[0][1], finals[1][1]], axis=1)
    x = _glu_proj(ys[0], ys[1], u, s5_d_l1, x, mods, 2, w_glu_l1.astype(BF16))
    x_c, x_l = _moe_streams(x, norm2_l1, mods, 3, 4, 5, router_l1, router_bias_l1, exp_w13_l1, exp_w2_l1,
                            shared_w13_l1, shared_w2_l1)

    return (_final_norm(x_c, final_norm, 0, T_CTX).reshape(BATCH, SEQ, D_MODEL),
            _final_norm(x_l, final_norm, 0, T_LAT).reshape(DEC_BATCH, DEC_SEQ, D_MODEL),
            new_k.reshape(BATCH, SEQ, N_KV_HEADS, HEAD_DIM),
            new_v.reshape(BATCH, SEQ, N_KV_HEADS, HEAD_DIM),
            s_re, s_im)
```

```python
import functools
import math

import jax
import jax.numpy as jnp
from jax import lax
from jax.experimental import pallas as pl
from jax.experimental.pallas import tpu as pltpu

F32 = jnp.float32
BF16 = jnp.bfloat16

D_MODEL = 2048
BATCH = 32
SEQ = 256
DEC_BATCH = 4
DEC_SEQ = 4096
PAST_LEN = 256
GRID_W = 64
RMS_EPS = 1e-6
HEAD_DIM = 128
N_HEADS = 8
N_KV_HEADS = 2
GQA_GROUP = N_HEADS // N_KV_HEADS
Q_WIDTH = N_HEADS * HEAD_DIM
KV_WIDTH = N_KV_HEADS * HEAD_DIM
QKV_WIDTH = Q_WIDTH + 2 * KV_WIDTH
ROPE_THETA = 10000.0
ROPE_AXIS_DIM = HEAD_DIM // 2
HYENA_WIDTH = D_MODEL // 2
HYENA_ORDER = 2
FILTER_BANDS = 16
DECAY_TARGET = 1e-2
DECAY_PCT_SHORT = 0.3
DECAY_PCT_LONG = 1.5
S5_WIDTH = D_MODEL // 2
S5_GROUP_CH = 16
S5_GROUPS = S5_WIDTH // S5_GROUP_CH
S5_STATE = 64
N_EXPERTS = 64
TOP_K = 8
N_EXPERT_GROUPS = 8
TOPK_GROUPS = 4
EXPERT_HIDDEN = 512
ROUTED_SCALE = 2.5
MOE_BLOCK = 256

T_CTX = BATCH * SEQ
T_LAT = DEC_BATCH * DEC_SEQ
T_ALL = T_CTX + T_LAT
N_COND = 8

ROW_TILE = 256
V7X_VMEM_LIMIT_BYTES = 56 * 1024 * 1024


def _params(semantics):
    return pltpu.CompilerParams(dimension_semantics=semantics, vmem_limit_bytes=V7X_VMEM_LIMIT_BYTES)


def _cond_row(i, tm, row0=0):
    tok = row0 + i * tm
    return jnp.where(tok < T_CTX, 0, 1 + (tok - T_CTX) // DEC_SEQ)


def _mod_spec(k, tm, row0=0):
    return pl.BlockSpec((1, 1, 1, D_MODEL), lambda i: (_cond_row(i, tm, row0), k, 0, 0))


def _resident(shape):
    nd = len(shape)
    return pl.BlockSpec(shape, lambda i: (0,) * nd, pipeline_mode=pl.Buffered(1))


def _silu(x):
    return x * jax.nn.sigmoid(x)


def _norm_mod(x, g, sc, sh):
    ms = jnp.mean(x * x, axis=-1, keepdims=True)
    return (x * lax.rsqrt(ms + RMS_EPS) * g) * (1.0 + sc) + sh


def _ada_kernel(c_ref, w_ref, b_ref, o_ref):
    a = _silu(c_ref[...]).astype(BF16)
    o_ref[...] = jnp.dot(a, w_ref[...].astype(BF16), preferred_element_type=F32) + b_ref[...]


def _ada_mods(cond, w, b):
    n = w.shape[1]
    tn = 1024
    out = pl.pallas_call(
        _ada_kernel,
        grid=(n // tn,),
        in_specs=[pl.BlockSpec((N_COND, D_MODEL), lambda j: (0, 0)),
                  pl.BlockSpec((D_MODEL, tn), lambda j: (0, j)),
                  pl.BlockSpec((1, tn), lambda j: (0, j))],
        out_specs=pl.BlockSpec((N_COND, tn), lambda j: (0, j)),
        out_shape=jax.ShapeDtypeStruct((N_COND, n), F32),
        compiler_params=_params(("parallel",)),
        name="ada_mods",
    )(cond, w, b.reshape(1, n))
    return out.reshape(N_COND, 6, 1, D_MODEL)


def _split_row_specs(width, tm):
    n_ctx = T_CTX // tm
    return [pl.BlockSpec((tm, width), lambda i: (jnp.minimum(i, n_ctx - 1), 0)),
            pl.BlockSpec((tm, width), lambda i: (jnp.maximum(i - n_ctx, 0), 0))]


def _pick_rows(ctx_ref, lat_ref):
    return jnp.where(pl.program_id(0) < T_CTX // ctx_ref.shape[0], ctx_ref[...], lat_ref[...])


def _nmm_kernel(*refs, split):
    if split:
        xc_ref, xl_ref, g_ref, sc_ref, sh_ref, w_ref, o_ref = refs
        x = _pick_rows(xc_ref, xl_ref)
    else:
        x_ref, g_ref, sc_ref, sh_ref, w_ref, o_ref = refs
        x = x_ref[...]
    h = _norm_mod(x, g_ref[...], sc_ref[0, 0], sh_ref[0, 0])
    o_ref[...] = jnp.dot(h.astype(BF16), w_ref[...], preferred_element_type=F32)


def _norm_mod_matmul(x, g, mods4, k_shift, k_scale, w_bf16):
    n = w_bf16.shape[1]
    tm = ROW_TILE
    split = isinstance(x, tuple)
    x_args = list(x) if split else [x]
    x_specs = _split_row_specs(D_MODEL, tm) if split else [pl.BlockSpec((tm, D_MODEL), lambda i: (i, 0))]
    return pl.pallas_call(
        functools.partial(_nmm_kernel, split=split),
        grid=(T_ALL // tm,),
        in_specs=x_specs + [_resident((1, D_MODEL)), _mod_spec(k_scale, tm), _mod_spec(k_shift, tm),
                            _resident((D_MODEL, n))],
        out_specs=pl.BlockSpec((tm, n), lambda i: (i, 0)),
        out_shape=jax.ShapeDtypeStruct((T_ALL, n), F32),
        compiler_params=_params(("parallel",)),
        name="norm_mod_matmul",
    )(*x_args, g.reshape(1, D_MODEL), mods4, mods4, w_bf16)


def _head_norm(xh, g):
    ms = jnp.mean(xh * xh, axis=-1, keepdims=True)
    return xh * lax.rsqrt(ms + RMS_EPS) * g


def _rope(y, cos, sin_signed):
    lane = lax.broadcasted_iota(jnp.int32, y.shape, 1)
    partner = jnp.where(lane % 2 == 0, pltpu.roll(y, HEAD_DIM - 1, 1), pltpu.roll(y, 1, 1))
    return y * cos + partner * sin_signed


def _qkprep_ctx_kernel(qkv_ref, qg_ref, kg_ref, q_out, kb_out, vb_out, kf_out):
    for h in range(N_HEADS):
        sl = slice(h * HEAD_DIM, (h + 1) * HEAD_DIM)
        q_out[:, sl] = _head_norm(qkv_ref[:, sl], qg_ref[...]).astype(BF16)
    for j in range(N_KV_HEADS):
        k = _head_norm(qkv_ref[:, Q_WIDTH + j * HEAD_DIM:Q_WIDTH + (j + 1) * HEAD_DIM], kg_ref[...])
        kf_out[:, j * HEAD_DIM:(j + 1) * HEAD_DIM] = k
        kb_out[:, j * HEAD_DIM:(j + 1) * HEAD_DIM] = k.astype(BF16)
    vb_out[...] = qkv_ref[:, Q_WIDTH + KV_WIDTH:QKV_WIDTH].astype(BF16)


def _qkprep_lat_kernel(qkv_ref, qg_ref, kg_ref, cos_ref, sin_ref, q_out, kb_out, vb_out):
    cos, sin = cos_ref[...], sin_ref[...]
    for h in range(N_HEADS):
        sl = slice(h * HEAD_DIM, (h + 1) * HEAD_DIM)
        q_out[:, sl] = _rope(_head_norm(qkv_ref[:, sl], qg_ref[...]), cos, sin).astype(BF16)
    for j in range(N_KV_HEADS):
        k = _head_norm(qkv_ref[:, Q_WIDTH + j * HEAD_DIM:Q_WIDTH + (j + 1) * HEAD_DIM], kg_ref[...])
        kb_out[:, j * HEAD_DIM:(j + 1) * HEAD_DIM] = _rope(k, cos, sin).astype(BF16)
    vb_out[...] = qkv_ref[:, Q_WIDTH + KV_WIDTH:QKV_WIDTH].astype(BF16)


def _qk_prep(proj, row0, t, q_norm, k_norm, rope_tabs):
    tm = ROW_TILE
    b0 = row0 // tm
    qkv_spec = pl.BlockSpec((tm, QKV_WIDTH), lambda i: (b0 + i, 0))
    gain = _resident((1, HEAD_DIM))
    outs = [pl.BlockSpec((tm, Q_WIDTH), lambda i: (i, 0)),
            pl.BlockSpec((tm, KV_WIDTH), lambda i: (i, 0)),
            pl.BlockSpec((tm, KV_WIDTH), lambda i: (i, 0))]
    shapes = [jax.ShapeDtypeStruct((t, Q_WIDTH), BF16),
              jax.ShapeDtypeStruct((t, KV_WIDTH), BF16),
              jax.ShapeDtypeStruct((t, KV_WIDTH), BF16)]
    qg, kg = q_norm.reshape(1, HEAD_DIM), k_norm.reshape(1, HEAD_DIM)
    if rope_tabs is None:
        return pl.pallas_call(
            _qkprep_ctx_kernel, grid=(t // tm,),
            in_specs=[qkv_spec, gain, gain],
            out_specs=outs + [pl.BlockSpec((tm, KV_WIDTH), lambda i: (i, 0))],
            out_shape=shapes + [jax.ShapeDtypeStruct((t, KV_WIDTH), F32)],
            compiler_params=_params(("parallel",)), name="qk_prep_ctx",
        )(proj, qg, kg)
    cos, sin = rope_tabs
    nq = DEC_SEQ // tm
    tab = pl.BlockSpec((tm, HEAD_DIM), lambda i: (i % nq, 0))
    return pl.pallas_call(
        _qkprep_lat_kernel, grid=(t // tm,),
        in_specs=[qkv_spec, gain, gain, tab, tab],
        out_specs=outs, out_shape=shapes,
        compiler_params=_params(("parallel",)), name="qk_prep_lat",
    )(proj, qg, kg, cos, sin)


def _attn_kernel(*refs, with_ctx):
    if with_ctx:
        q_ref, k_ref, v_ref, ck_ref, cv_ref, o_ref = refs
    else:
        q_ref, k_ref, v_ref, o_ref = refs
    c = HEAD_DIM ** -0.5 * math.log2(math.e)
    nt = (((1,), (1,)), ((), ()))
    q = q_ref[...]
    s = lax.dot_general(q, k_ref[...], nt, preferred_element_type=F32)
    m = jnp.max(s, axis=-1, keepdims=True)
    if with_ctx:
        sc = lax.dot_general(q, ck_ref[...], nt, preferred_element_type=F32)
        m = jnp.maximum(m, jnp.max(sc, axis=-1, keepdims=True))
    p = jnp.exp2((s - m) * c)
    l = jnp.sum(p, axis=-1, keepdims=True)
    o = jnp.dot(p.astype(BF16), v_ref[...], preferred_element_type=F32)
    if with_ctx:
        pc = jnp.exp2((sc - m) * c)
        l = l + jnp.sum(pc, axis=-1, keepdims=True)
        o = o + jnp.dot(pc.astype(BF16), cv_ref[...], preferred_element_type=F32)
    o_ref[...] = (o / l).astype(BF16)


def _attention(q, k, v, n_batch, seq, ctx_kv):
    tq = ROW_TILE
    nq = seq // tq
    t = n_batch * seq
    q_spec = pl.BlockSpec((tq, HEAD_DIM), lambda b, h, i: (b * nq + i, h))
    kv_spec = pl.BlockSpec((seq, HEAD_DIM), lambda b, h, i: (b, h // GQA_GROUP))
    in_specs = [q_spec, kv_spec, kv_spec]
    args = [q, k, v]
    if ctx_kv is not None:
        c_spec = pl.BlockSpec((PAST_LEN, HEAD_DIM), lambda b, h, i: (b, h // GQA_GROUP))
        in_specs += [c_spec, c_spec]
        args += list(ctx_kv)
    return pl.pallas_call(
        functools.partial(_attn_kernel, with_ctx=ctx_kv is not None),
        grid=(n_batch, N_HEADS, nq),
        in_specs=in_specs,
        out_specs=pl.BlockSpec((tq, HEAD_DIM), lambda b, h, i: (b * nq + i, h)),
        out_shape=jax.ShapeDtypeStruct((t, Q_WIDTH), BF16),
        compiler_params=_params(("parallel", "parallel", "parallel")),
        name="attention_lat" if ctx_kv is not None else "attention_ctx",
    )(*args)


def _outproj_kernel(ac_ref, al_ref, zc_ref, zl_ref, xc_ref, xl_ref, gate_ref, wa_ref, wz_ref, o_ref):
    acc = jnp.dot(_pick_rows(ac_ref, al_ref), wa_ref[...], preferred_element_type=F32)
    acc = acc + jnp.dot(_pick_rows(zc_ref, zl_ref), wz_ref[...], preferred_element_type=F32)
    o_ref[...] = _pick_rows(xc_ref, xl_ref) + gate_ref[0, 0] * acc


def _out_proj(attn, z, x, mods4, k_gate, wa_bf16, wz_bf16):
    tm = ROW_TILE
    return pl.pallas_call(
        _outproj_kernel,
        grid=(T_ALL // tm,),
        in_specs=(_split_row_specs(Q_WIDTH, tm) + _split_row_specs(HYENA_WIDTH, tm) + _split_row_specs(D_MODEL, tm)
                  + [_mod_spec(k_gate, tm), _resident((Q_WIDTH, D_MODEL)), _resident((HYENA_WIDTH, D_MODEL))]),
        out_specs=pl.BlockSpec((tm, D_MODEL), lambda i: (i, 0)),
        out_shape=jax.ShapeDtypeStruct((T_ALL, D_MODEL), F32),
        compiler_params=_params(("parallel",)),
        name="out_proj",
    )(*attn, *z, *x, mods4, wa_bf16, wz_bf16)


def _gelu_tanh(x):
    return 0.5 * x * (1.0 + jnp.tanh(math.sqrt(2.0 / math.pi) * (x + 0.044715 * (x * x * x))))


def _glu_kernel(yfc_ref, yfl_ref, ybc_ref, ybl_ref, u_ref, d_ref, xc_ref, xl_ref, gate_ref, w_ref, o_ref):
    y = (_pick_rows(yfc_ref, yfl_ref) + _pick_rows(ybc_ref, ybl_ref)) + d_ref[...] * u_ref[...]
    ag = jnp.dot(_gelu_tanh(y).astype(BF16), w_ref[...], preferred_element_type=F32)
    a, g = ag[:, :D_MODEL], ag[:, D_MODEL:]
    o_ref[...] = _pick_rows(xc_ref, xl_ref) + gate_ref[0, 0] * (a * jax.nn.sigmoid(g))


def _glu_proj(y_fwd, y_bwd, u, d_skip, x, mods4, k_gate, w_bf16):
    tm = ROW_TILE
    return pl.pallas_call(
        _glu_kernel,
        grid=(T_ALL // tm,),
        in_specs=(_split_row_specs(S5_WIDTH, tm) + _split_row_specs(S5_WIDTH, tm)
                  + [pl.BlockSpec((tm, S5_WIDTH), lambda i: (i, 0)), _resident((1, S5_WIDTH))]
                  + _split_row_specs(D_MODEL, tm)
                  + [_mod_spec(k_gate, tm), _resident((S5_WIDTH, 2 * D_MODEL))]),
        out_specs=pl.BlockSpec((tm, D_MODEL), lambda i: (i, 0)),
        out_shape=jax.ShapeDtypeStruct((T_ALL, D_MODEL), F32),
        compiler_params=_params(("parallel",)),
        name="glu_proj",
    )(*y_fwd, *y_bwd, u, d_skip.reshape(1, S5_WIDTH), *x, mods4, w_bf16)


def _gmm_kernel(be_ref, nreal_ref, xs_ref, w13_ref, w2_ref, o_ref, w13_b, w2_b):
    i = pl.program_id(0)
    real = i < nreal_ref[0]
    fresh = jnp.logical_and(real, jnp.logical_or(i == 0, be_ref[i] != be_ref[jnp.maximum(i - 1, 0)]))

    @pl.when(fresh)
    def _():
        w13_b[...] = w13_ref[0].astype(BF16)
        w2_b[...] = w2_ref[0].astype(BF16)

    @pl.when(real)
    def _():
        gu = jnp.dot(xs_ref[...], w13_b[...], preferred_element_type=F32)
        hmid = _silu(gu[:, :EXPERT_HIDDEN]) * gu[:, EXPERT_HIDDEN:]
        o_ref[...] = jnp.dot(hmid.astype(BF16), w2_b[...], preferred_element_type=F32).astype(BF16)

    @pl.when(jnp.logical_not(real))
    def _():
        o_ref[...] = jnp.zeros_like(o_ref)


def _grouped_experts(blk_e, n_real, xs, w13, w2):
    n_rows = xs.shape[0]
    n_blocks = n_rows // MOE_BLOCK

    def live(i, nreal):
        return jnp.minimum(i, nreal[0] - 1)

    grid_spec = pltpu.PrefetchScalarGridSpec(
        num_scalar_prefetch=2,
        grid=(n_blocks,),
        in_specs=[pl.BlockSpec((MOE_BLOCK, D_MODEL), lambda i, be, nr: (live(i, nr), 0)),
                  pl.BlockSpec((1, D_MODEL, 2 * EXPERT_HIDDEN), lambda i, be, nr: (be[live(i, nr)], 0, 0)),
                  pl.BlockSpec((1, EXPERT_HIDDEN, D_MODEL), lambda i, be, nr: (be[live(i, nr)], 0, 0))],
        out_specs=pl.BlockSpec((MOE_BLOCK, D_MODEL), lambda i, be, nr: (i, 0)),
        scratch_shapes=[pltpu.VMEM((D_MODEL, 2 * EXPERT_HIDDEN), BF16),
                        pltpu.VMEM((EXPERT_HIDDEN, D_MODEL), BF16)],
    )
    return pl.pallas_call(
        _gmm_kernel,
        grid_spec=grid_spec,
        out_shape=jax.ShapeDtypeStruct((n_rows, D_MODEL), BF16),
        compiler_params=_params(("arbitrary",)),
        name="grouped_experts",
    )(blk_e, n_real, xs, w13, w2)


COMBINE_TILE = 256


def _shared_kernel(h_ref, x_ref, ye_ref, w_ref, gate_ref, w13_ref, w2_ref, o_ref):
    gu = jnp.dot(h_ref[...], w13_ref[...], preferred_element_type=F32)
    hmid = _silu(gu[:, :EXPERT_HIDDEN]) * gu[:, EXPERT_HIDDEN:]
    acc = jnp.dot(hmid.astype(BF16), w2_ref[...], preferred_element_type=F32)
    wts = w_ref[...]
    for k in range(TOP_K):
        acc = acc + wts[:, k:k + 1] * ye_ref[k].astype(F32)
    o_ref[...] = x_ref[...] + gate_ref[0, 0] * acc


def _shared_and_combine(h_bf16, x, row0, t, y_exp, wts, mods4, k_gate, w13_bf16, w2_bf16):
    tm = COMBINE_TILE
    b0 = row0 // tm
    row = lambda i: (i, 0)
    off = lambda i: (b0 + i, 0)
    return pl.pallas_call(
        _shared_kernel,
        grid=(t // tm,),
        in_specs=[pl.BlockSpec((tm, D_MODEL), off), pl.BlockSpec((tm, D_MODEL), off),
                  pl.BlockSpec((TOP_K, tm, D_MODEL), lambda i: (0, i, 0)),
                  pl.BlockSpec((tm, TOP_K), row), _mod_spec(k_gate, tm, row0),
                  _resident((D_MODEL, 2 * EXPERT_HIDDEN)), _resident((EXPERT_HIDDEN, D_MODEL))],
        out_specs=pl.BlockSpec((tm, D_MODEL), row),
        out_shape=jax.ShapeDtypeStruct((t, D_MODEL), F32),
        compiler_params=_params(("parallel",)),
        name="shared_expert_combine",
    )(h_bf16, x, y_exp, wts, mods4, w13_bf16, w2_bf16)


ROUTE_TILE = 256
GROUP_SIZE = N_EXPERTS // N_EXPERT_GROUPS
NEG_INF = float("-inf")


def _first_argmax(v, index, sentinel):
    m = jnp.max(v, axis=0, keepdims=True)
    first = jnp.min(jnp.where(v == m, index, sentinel), axis=0, keepdims=True)
    return m, first


def _router_kernel(x_ref, g_ref, sc_ref, sh_ref, wt_ref, rb_ref, tri_ref,
                   h_out, idx_out, gate_out, rank_out, cnt_out, base_ref):
    tm = x_ref.shape[0]
    step = pl.program_id(0)

    @pl.when(jnp.logical_or(step == 0, step == T_CTX // tm))
    def _():
        base_ref[...] = jnp.zeros_like(base_ref)

    hb = _norm_mod(x_ref[...], g_ref[...], sc_ref[0, 0], sh_ref[0, 0]).astype(BF16)
    h_out[...] = hb
    logits = lax.dot_general(wt_ref[...], hb, (((1,), (1,)), ((), ())), preferred_element_type=F32)
    scores = jax.nn.sigmoid(logits)
    choice = scores + rb_ref[...]

    member = lax.broadcasted_iota(jnp.int32, (GROUP_SIZE, tm), 0).astype(F32)
    group_scores = []
    for g in range(N_EXPERT_GROUPS):
        cg = choice[g * GROUP_SIZE:(g + 1) * GROUP_SIZE, :]
        m1, first = _first_argmax(cg, member, float(GROUP_SIZE))
        m2 = jnp.max(jnp.where(member == first, NEG_INF, cg), axis=0, keepdims=True)
        group_scores.append(m1 + m2)
    gs = jnp.concatenate(group_scores, axis=0)

    group = lax.broadcasted_iota(jnp.int32, (N_EXPERT_GROUPS, tm), 0).astype(F32)
    keep = jnp.zeros((N_EXPERT_GROUPS, tm), F32)
    for _ in range(TOPK_GROUPS):
        _, first = _first_argmax(gs, group, float(N_EXPERT_GROUPS))
        sel = group == first
        keep = jnp.where(sel, 1.0, keep)
        gs = jnp.where(sel, NEG_INF, gs)
    masked = jnp.concatenate(
        [jnp.where(keep[g:g + 1, :] > 0.0, choice[g * GROUP_SIZE:(g + 1) * GROUP_SIZE, :], NEG_INF)
         for g in range(N_EXPERT_GROUPS)], axis=0)

    expert = lax.broadcasted_iota(jnp.int32, (N_EXPERTS, tm), 0).astype(F32)
    onehot = jnp.zeros((N_EXPERTS, tm), F32)
    sels, idx_rows, gate_rows = [], [], []
    for _ in range(TOP_K):
        _, first = _first_argmax(masked, expert, float(N_EXPERTS))
        sel = expert == first
        sels.append(sel)
        idx_rows.append(first)
        gate_rows.append(jnp.sum(jnp.where(sel, scores, 0.0), axis=0, keepdims=True))
        masked = jnp.where(sel, NEG_INF, masked)
        onehot = jnp.where(sel, 1.0, onehot)
    gates = jnp.concatenate(gate_rows, axis=0)
    gate_out[...] = gates / jnp.sum(gates, axis=0, keepdims=True) * ROUTED_SCALE
    idx_out[...] = jnp.concatenate(idx_rows, axis=0).astype(jnp.int32)

    cum = jnp.dot(onehot.astype(BF16), tri_ref[...], preferred_element_type=F32) + base_ref[:, 0:1]
    rank_rows = [jnp.sum(jnp.where(sel, cum, 0.0), axis=0, keepdims=True) - 1.0 for sel in sels]
    rank_out[...] = jnp.concatenate(rank_rows, axis=0).astype(jnp.int32)
    base_ref[...] = jnp.broadcast_to(cum[:, tm - 1:tm], base_ref.shape)
    cnt_out[0] = base_ref[...].astype(jnp.int32)


def _router(x, norm_g, mods4, k_shift, k_scale, router_w_t, router_bias):
    t = T_ALL
    tm = ROUTE_TILE
    n_ctx = T_CTX // tm
    tri = (jnp.arange(tm)[:, None] <= jnp.arange(tm)[None, :]).astype(BF16)
    choice_rows = pl.BlockSpec((TOP_K, tm), lambda i: (0, i))
    return pl.pallas_call(
        _router_kernel,
        grid=(t // tm,),
        in_specs=[pl.BlockSpec((tm, D_MODEL), lambda i: (i, 0)), _resident((1, D_MODEL)),
                  _mod_spec(k_scale, tm), _mod_spec(k_shift, tm),
                  _resident((N_EXPERTS, D_MODEL)), _resident((N_EXPERTS, 1)), _resident((tm, tm))],
        out_specs=[pl.BlockSpec((tm, D_MODEL), lambda i: (i, 0)), choice_rows, choice_rows, choice_rows,
                   pl.BlockSpec((1, N_EXPERTS, 128), lambda i: (jnp.where(i < n_ctx, 0, 1), 0, 0))],
        out_shape=[jax.ShapeDtypeStruct((t, D_MODEL), BF16),
                   jax.ShapeDtypeStruct((TOP_K, t), jnp.int32), jax.ShapeDtypeStruct((TOP_K, t), F32),
                   jax.ShapeDtypeStruct((TOP_K, t), jnp.int32),
                   jax.ShapeDtypeStruct((2, N_EXPERTS, 128), jnp.int32)],
        scratch_shapes=[pltpu.VMEM((N_EXPERTS, 128), F32)],
        compiler_params=_params(("arbitrary",)),
        name="router",
    )(x, norm_g.reshape(1, D_MODEL), mods4, mods4, router_w_t, router_bias.reshape(N_EXPERTS, 1), tri)


def _pos_kernel(idx_ref, rank_ref, off_ref, pos_out):
    tm = idx_ref.shape[1]
    expert = lax.broadcasted_iota(jnp.int32, (N_EXPERTS, tm), 0)
    rows = []
    for k in range(TOP_K):
        sel = expert == idx_ref[k:k + 1, :]
        rows.append(jnp.sum(jnp.where(sel, off_ref[...], 0.0), axis=0, keepdims=True))
    pos_out[...] = jnp.concatenate(rows, axis=0).astype(jnp.int32) + rank_ref[...]


def _positions(idx, rank, offsets):
    t = idx.shape[1]
    tm = 1024
    rows = pl.BlockSpec((TOP_K, tm), lambda i: (0, i))
    return pl.pallas_call(
        _pos_kernel, grid=(t // tm,),
        in_specs=[rows, rows, _resident((N_EXPERTS, 1))],
        out_specs=rows, out_shape=jax.ShapeDtypeStruct((TOP_K, t), jnp.int32),
        compiler_params=_params(("parallel",)), name="expert_positions",
    )(idx, rank, offsets.astype(F32).reshape(N_EXPERTS, 1))


def _moe_stream(x, row0, t, h_bf16, top_idx, gate, rank, counts, mods4, k_gate, exp_w13, exp_w2,
                sh_w13_bf16, sh_w2_bf16):
    padded = (counts + MOE_BLOCK - 1) // MOE_BLOCK * MOE_BLOCK
    e_iota = jnp.arange(N_EXPERTS)
    pend = jnp.sum(jnp.where(e_iota[:, None] <= e_iota[None, :], padded[:, None], 0), axis=0)
    pos = _positions(top_idx, rank, pend - padded)
    n_rows = -(-(t * TOP_K + N_EXPERTS * (MOE_BLOCK - 1)) // MOE_BLOCK) * MOE_BLOCK
    n_blocks = n_rows // MOE_BLOCK
    blk_start = jnp.arange(n_blocks, dtype=pend.dtype) * MOE_BLOCK
    blk_e = jnp.minimum(jnp.sum((pend[None, :] <= blk_start[:, None]).astype(jnp.int32), axis=1), N_EXPERTS - 1)
    n_real = (pend[N_EXPERTS - 1:] // MOE_BLOCK).astype(jnp.int32)
    tok = jnp.broadcast_to(jnp.arange(t, dtype=jnp.int32)[None, :], (TOP_K, t))
    _, sorted_tok = lax.sort_key_val(pos.reshape(-1), tok.reshape(-1))
    dense_end = jnp.sum(jnp.where(e_iota[:, None] <= e_iota[None, :], counts[:, None], 0), axis=0)
    shift_blk = ((pend - padded) - (dense_end - counts))[blk_e]
    end_blk = dense_end[blk_e]
    row = jnp.arange(n_rows, dtype=jnp.int32)
    dense = row - jnp.repeat(shift_blk, MOE_BLOCK)
    row_tok = jnp.where(dense < jnp.repeat(end_blk, MOE_BLOCK),
                        sorted_tok[jnp.clip(dense, 0, t * TOP_K - 1)], row % t)
    xs = h_bf16[row_tok + row0]
    ys = _grouped_experts(blk_e, n_real, xs, exp_w13, exp_w2)
    y_exp = ys[pos.reshape(-1)].reshape(TOP_K, t, D_MODEL)
    return _shared_and_combine(h_bf16, x, row0, t, y_exp, gate.T, mods4, k_gate, sh_w13_bf16, sh_w2_bf16)


def _moe_streams(x, norm_g, mods4, k_shift, k_scale, k_gate, router_w, router_bias, exp_w13, exp_w2, sh_w13, sh_w2):
    h_bf16, top_idx, gate, rank, counts = _router(x, norm_g, mods4, k_shift, k_scale, router_w.T.astype(BF16),
                                                  router_bias)
    sh13, sh2 = sh_w13.astype(BF16), sh_w2.astype(BF16)
    outs = []
    for s, (row0, t) in enumerate(((0, T_CTX), (T_CTX, T_LAT))):
        cols = slice(row0, row0 + t)
        outs.append(_moe_stream(x, row0, t, h_bf16, top_idx[:, cols], gate[:, cols], rank[:, cols], counts[s, :, 0],
                                mods4, k_gate, exp_w13, exp_w2, sh13, sh2))
    return tuple(outs)


HY_BLOCK_CTX, HY_TC_CTX = 256, 512
HY_BLOCK_LAT, HY_TC_LAT = 512, 128


def _hyena_filters(L, w1, b1, w2, b2, w3, sin_freq):
    t_norm = jnp.linspace(0.0, 1.0, L, dtype=F32)[:, None]
    omega = 2.0 * math.pi * jnp.arange(L, dtype=F32)[:, None] / L
    bands = jnp.linspace(1e-4, FILTER_BANDS - 1, FILTER_BANDS, dtype=F32)[None, :]
    z = jnp.concatenate([t_norm, jnp.cos(bands * omega), -jnp.sin(bands * omega)], axis=-1)
    h = jnp.sin(sin_freq * (z @ w1 + b1))
    h = jnp.sin(sin_freq * (h @ w2 + b2))
    h = (h @ w3).reshape(L, 2, HYENA_ORDER, HYENA_WIDTH)
    deltas = jnp.abs(jnp.linspace(math.log(DECAY_TARGET) / DECAY_PCT_LONG,
                                  math.log(DECAY_TARGET) / DECAY_PCT_SHORT, HYENA_WIDTH, dtype=F32))
    h = h * jnp.exp(-t_norm * deltas)[:, None, None, :]
    fwd, bwd = h[:, 0], h[:, 1] * (t_norm > 0.0)[:, :, None]
    norm = jnp.sum(jnp.abs(fwd), axis=0, keepdims=True) + jnp.sum(jnp.abs(bwd), axis=0, keepdims=True)
    return fwd / norm, bwd / norm


HY_ACC_ELEMS = 8192


def _dft_matrices(bk):
    k = jnp.arange(bk, dtype=jnp.int32)
    phase = ((2 * k[:, None] + 1) * k[None, :]) % (4 * bk)
    ang = phase.astype(F32) * (math.pi / (2 * bk))
    return jnp.cos(ang), jnp.sin(ang)


def _spectra_kernel(xf_ref, xb_ref, yf_ref, yb_ref, c1_ref, s1_ref, c2_ref, s2_ref, o_ref, *, nb):
    d = pl.program_id(1) - (nb - 1)
    x = jnp.where(d >= 0, xf_ref[0], xb_ref[0]).astype(BF16)
    y = jnp.where(d >= 1, yf_ref[0], yb_ref[0]).astype(BF16)
    cy = jnp.where(d == 0, c1_ref[...], c2_ref[...])
    sy = jnp.where(d == 0, s1_ref[...], s2_ref[...])
    re = jnp.dot(c1_ref[...], x, preferred_element_type=F32) + jnp.dot(cy, y, preferred_element_type=F32)
    im = jnp.dot(sy, y, preferred_element_type=F32) - jnp.dot(s1_ref[...], x, preferred_element_type=F32)
    o_ref[0, 0, 0] = re
    o_ref[0, 0, 1] = jnp.where(d >= 0, im, -im)


def _filter_spectra(fwd, bwd, bk, tc):
    L = fwd.shape[0]
    nb = L // bk
    n_ct = HYENA_WIDTH // tc
    cos, sin = _dft_matrices(bk)
    k = jnp.arange(bk, dtype=jnp.int32)
    phase = ((2 * k[:, None] + 1) * (bk - k[None, :])) % (4 * bk)
    ang = phase.astype(F32) * (math.pi / (2 * bk))
    live = (k[None, :] > 0).astype(F32)
    mats = [m.astype(BF16) for m in (cos, sin, jnp.cos(ang) * live, jnp.sin(ang) * live)]

    def seg(pick):
        return pl.BlockSpec((1, bk, tc), lambda o, di, ct: (pick(di - (nb - 1)), 0, o * n_ct + ct))

    mat = pl.BlockSpec((bk, bk), lambda o, di, ct: (0, 0), pipeline_mode=pl.Buffered(1))
    halves = [h.reshape(nb, bk, HYENA_ORDER * HYENA_WIDTH) for h in (fwd, bwd)]
    return pl.pallas_call(
        functools.partial(_spectra_kernel, nb=nb),
        grid=(HYENA_ORDER, 2 * nb - 1, n_ct),
        in_specs=[seg(lambda d: jnp.maximum(d, 0)), seg(lambda d: jnp.maximum(-d, 0)),
                  seg(lambda d: jnp.maximum(d - 1, 0)), seg(lambda d: jnp.maximum(-d - 1, 0)),
                  mat, mat, mat, mat],
        out_specs=pl.BlockSpec((1, 1, 2, bk, tc), lambda o, di, ct: (o, di, 0, 0, ct)),
        out_shape=jax.ShapeDtypeStruct((HYENA_ORDER, 2 * nb - 1, 2, bk, HYENA_WIDTH), F32),
        compiler_params=_params(("parallel", "parallel", "parallel")),
        name="filter_spectra",
    )(halves[0], halves[1], halves[0], halves[1], *mats)


def _hyena_kernel(v_ref, x1_ref, x2_ref, cw_ref, cb_ref, hb_ref, k0_ref, k1_ref, c_ref, s_ref, ct_ref, st_ref,
                  o_ref, vs_ref, z1_ref, x2s_ref, ur_ref, ui_ref, yr_ref, yi_ref, *, seq, bk, tc):
    nb = seq // bk
    ft = HY_ACC_ELEMS // tc
    row = lax.broadcasted_iota(jnp.int32, (seq, tc), 0)

    def short_conv(x, j):
        prev = jnp.where(row == 0, 0.0, pltpu.roll(x, 1, 0))
        nxt = jnp.where(row == seq - 1, 0.0, pltpu.roll(x, seq - 1, 0))
        return (prev * cw_ref[0, j:j + 1, :] + x * cw_ref[1, j:j + 1, :] + nxt * cw_ref[2, j:j + 1, :]
                + cb_ref[j:j + 1, :])

    vs_ref[...] = short_conv(v_ref[...], 0)
    z1_ref[...] = short_conv(x1_ref[...], 1)
    x2s_ref[...] = short_conv(x2_ref[...], 2)

    def long_conv(src_ref, k_ref, emit):
        for blk in range(nb):
            ub = src_ref[blk * bk:(blk + 1) * bk, :].astype(BF16)
            ur_ref[blk] = jnp.dot(c_ref[...], ub, preferred_element_type=F32)
            ui_ref[blk] = -jnp.dot(s_ref[...], ub, preferred_element_type=F32)
        for out_blk in range(nb):
            def acc_tile(f, carry, out_blk=out_blk):
                rows = pl.ds(pl.multiple_of(f * ft, ft), ft)
                ar = jnp.zeros((ft, tc), F32)
                ai = jnp.zeros((ft, tc), F32)
                for in_blk in range(nb):
                    d = out_blk - in_blk + nb - 1
                    kr, ki = k_ref[d, 0, rows, :], k_ref[d, 1, rows, :]
                    xr, xi = ur_ref[in_blk, rows, :], ui_ref[in_blk, rows, :]
                    ar = ar + (kr * xr - ki * xi)
                    ai = ai + (kr * xi + ki * xr)
                yr_ref[rows, :] = ar
                yi_ref[rows, :] = ai
                return carry

            lax.fori_loop(0, bk // ft, acc_tile, 0)
            y = jnp.dot(ct_ref[...], yr_ref[...].astype(BF16), preferred_element_type=F32)
            y = y - jnp.dot(st_ref[...], yi_ref[...].astype(BF16), preferred_element_type=F32)
            emit(slice(out_blk * bk, (out_blk + 1) * bk), y * (1.0 / bk))

    def emit_z1(rows, y):
        z1_ref[rows, :] = z1_ref[rows, :] * (y + hb_ref[0:1, :] * vs_ref[rows, :])

    def emit_out(rows, y):
        o_ref[rows, :] = (x2s_ref[rows, :] * (y + hb_ref[1:2, :] * z1_ref[rows, :])).astype(BF16)

    long_conv(vs_ref, k0_ref, emit_z1)
    long_conv(z1_ref, k1_ref, emit_out)


def _hyena(proj, row0, n_batch, seq, bk, tc, conv_w, conv_b, hy_bias, spectra, mats):
    nb = seq // bk
    n_ct = HYENA_WIDTH // tc
    col0 = QKV_WIDTH // tc
    per = HYENA_WIDTH // tc
    b0 = row0 // seq

    def data(which):
        return pl.BlockSpec((seq, tc), lambda ct, b: (b0 + b, col0 + which * per + ct))

    const2 = lambda shape: pl.BlockSpec(shape, lambda ct, b: (0, 0), pipeline_mode=pl.Buffered(1))
    kspec = pl.BlockSpec((2 * nb - 1, 2, bk, tc), lambda ct, b: (0, 0, 0, ct), pipeline_mode=pl.Buffered(1))
    cw = conv_w.reshape(3, HYENA_ORDER + 1, HYENA_WIDTH)
    cb = conv_b.reshape(HYENA_ORDER + 1, HYENA_WIDTH)
    return pl.pallas_call(
        functools.partial(_hyena_kernel, seq=seq, bk=bk, tc=tc),
        grid=(n_ct, n_batch),
        in_specs=[data(0), data(1), data(2),
                  pl.BlockSpec((3, HYENA_ORDER + 1, tc), lambda ct, b: (0, 0, ct)),
                  pl.BlockSpec((HYENA_ORDER + 1, tc), lambda ct, b: (0, ct)),
                  pl.BlockSpec((HYENA_ORDER, tc), lambda ct, b: (0, ct)),
                  kspec, kspec, const2((bk, bk)), const2((bk, bk)), const2((bk, bk)), const2((bk, bk))],
        out_specs=pl.BlockSpec((seq, tc), lambda ct, b: (b, ct)),
        out_shape=jax.ShapeDtypeStruct((n_batch * seq, HYENA_WIDTH), BF16),
        scratch_shapes=[pltpu.VMEM((seq, tc), F32), pltpu.VMEM((seq, tc), F32), pltpu.VMEM((seq, tc), F32),
                        pltpu.VMEM((nb, bk, tc), F32), pltpu.VMEM((nb, bk, tc), F32),
                        pltpu.VMEM((bk, tc), F32), pltpu.VMEM((bk, tc), F32)],
        compiler_params=_params(("parallel", "parallel")),
        name="hyena_seq%d" % seq,
    )(proj, proj, proj, cw, cb, hy_bias, spectra[0], spectra[1], *mats)


HY_SPECTRA_TC = 512


def _hyena_operands(L, bk, f_w1, f_b1, f_w2, f_b2, f_w3, sin_freq):
    fwd, bwd = _hyena_filters(L, f_w1, f_b1, f_w2, f_b2, f_w3, sin_freq)
    cos, sin = _dft_matrices(bk)
    mats = tuple(m.astype(BF16) for m in (cos, sin, cos.T, sin.T))
    return _filter_spectra(fwd, bwd, bk, HY_SPECTRA_TC), mats


S5_LANES = S5_GROUPS * S5_STATE
S5_CLUSTER_GROUPS = 16
S5_CLUSTERS = S5_GROUPS // S5_CLUSTER_GROUPS
S5_CLUSTER_CH = S5_CLUSTER_GROUPS * S5_GROUP_CH
S5_CLUSTER_LANES = S5_CLUSTER_GROUPS * S5_STATE
SCAN_CHUNK = 256
SCAN_TILE = 8
SCAN_LEVELS = (1, 2, 4)


def _s5_kernel(u_ref, s0r_ref, s0i_ref, bw_ref, cwr_ref, cwi_ref, tab_ref, y_ref, sfr_ref, sfi_ref,
               sre, sim, car, cai, *, reverse):
    @pl.when(pl.program_id(1) == 0)
    def _():
        car[...] = s0r_ref[0]
        cai[...] = s0i_ref[0]

    ub = u_ref[...].astype(BF16)
    for k in range(S5_CLUSTERS):
        bu = jnp.dot(ub[:, k * S5_CLUSTER_CH:(k + 1) * S5_CLUSTER_CH], bw_ref[k], preferred_element_type=F32)
        sre[:, k * S5_CLUSTER_LANES:(k + 1) * S5_CLUSTER_LANES] = bu[:, :S5_CLUSTER_LANES]
        sim[:, k * S5_CLUSTER_LANES:(k + 1) * S5_CLUSTER_LANES] = bu[:, S5_CLUSTER_LANES:]

    n_tiles = SCAN_CHUNK // SCAN_TILE
    shifts = tuple(SCAN_TILE - k for k in SCAN_LEVELS) if reverse else SCAN_LEVELS
    boundary_row = 0 if reverse else SCAN_TILE - 1
    for k in range(S5_CLUSTERS):
        lanes = slice(k * S5_CLUSTER_LANES, (k + 1) * S5_CLUSTER_LANES)

        def tile_step(i, carry, lanes=lanes):
            cr, ci = carry
            tile = (n_tiles - 1 - i) if reverse else i
            rows = pl.ds(pl.multiple_of(tile * SCAN_TILE, SCAN_TILE), SCAN_TILE)
            xr, xi = sre[rows, lanes], sim[rows, lanes]
            for lvl, sh in enumerate(shifts):
                pr, pi = tab_ref[2 * lvl, :, lanes], tab_ref[2 * lvl + 1, :, lanes]
                rr, ri = pltpu.roll(xr, sh, 0), pltpu.roll(xi, sh, 0)
                xr, xi = xr + (pr * rr - pi * ri), xi + (pr * ri + pi * rr)
            pr, pi = tab_ref[2 * len(shifts), :, lanes], tab_ref[2 * len(shifts) + 1, :, lanes]
            xr, xi = xr + (pr * cr - pi * ci), xi + (pr * ci + pi * cr)
            sre[rows, lanes] = xr
            sim[rows, lanes] = xi
            return xr[boundary_row:boundary_row + 1, :], xi[boundary_row:boundary_row + 1, :]

        cr, ci = lax.fori_loop(0, n_tiles, tile_step, (car[:, lanes], cai[:, lanes]))
        car[:, lanes] = cr
        cai[:, lanes] = ci

    for k in range(S5_CLUSTERS):
        lanes = slice(k * S5_CLUSTER_LANES, (k + 1) * S5_CLUSTER_LANES)
        yk = jnp.dot(sre[:, lanes].astype(BF16), cwr_ref[k], preferred_element_type=F32)
        yk = yk + jnp.dot(sim[:, lanes].astype(BF16), cwi_ref[k], preferred_element_type=F32)
        y_ref[:, k * S5_CLUSTER_CH:(k + 1) * S5_CLUSTER_CH] = yk
    sfr_ref[0] = car[...]
    sfi_ref[0] = cai[...]


def _s5_direction(u, row0, n_seq, seq, s0_re, s0_im, bw, cwr, cwi, tabs, reverse):
    nc = seq // SCAN_CHUNK
    b0 = row0 // SCAN_CHUNK

    def chunk(c):
        return (nc - 1 - c) if reverse else c

    state_spec = pl.BlockSpec((1, 1, S5_LANES), lambda b, c: (b, 0, 0))
    full3 = lambda shape: pl.BlockSpec(shape, lambda b, c: (0, 0, 0), pipeline_mode=pl.Buffered(1))
    return pl.pallas_call(
        functools.partial(_s5_kernel, reverse=reverse),
        grid=(n_seq, nc),
        in_specs=[pl.BlockSpec((SCAN_CHUNK, S5_WIDTH), lambda b, c: (b0 + b * nc + chunk(c), 0)),
                  state_spec, state_spec,
                  full3(bw.shape), full3(cwr.shape), full3(cwi.shape), full3(tabs.shape)],
        out_specs=[pl.BlockSpec((SCAN_CHUNK, S5_WIDTH), lambda b, c: (b * nc + chunk(c), 0)),
                   state_spec, state_spec],
        out_shape=[jax.ShapeDtypeStruct((n_seq * seq, S5_WIDTH), F32),
                   jax.ShapeDtypeStruct((n_seq, 1, S5_LANES), F32),
                   jax.ShapeDtypeStruct((n_seq, 1, S5_LANES), F32)],
        scratch_shapes=[pltpu.VMEM((SCAN_CHUNK, S5_LANES), F32), pltpu.VMEM((SCAN_CHUNK, S5_LANES), F32),
                        pltpu.VMEM((1, S5_LANES), F32), pltpu.VMEM((1, S5_LANES), F32)],
        compiler_params=_params(("parallel", "arbitrary")),
        name="s5_bwd" if reverse else "s5_fwd",
    )(u, s0_re, s0_im, bw, cwr, cwi, tabs)


def _s5_weights(a_re, a_im, log_dt, b_re, b_im, c_re, c_im):
    lam = lax.complex(a_re, a_im)
    lam_bar = jnp.exp(lam * jnp.exp(log_dt)[..., None])
    b_bar = ((lam_bar - 1.0) / lam)[..., None] * lax.complex(b_re, b_im)
    eye = jnp.eye(S5_CLUSTER_GROUPS, dtype=F32)
    out = []
    for d in range(2):
        def cluster_in(w):
            w = w.reshape(S5_CLUSTERS, S5_CLUSTER_GROUPS, S5_STATE, S5_GROUP_CH)
            return jnp.einsum('ab,kapc->kacbp', eye, w).reshape(S5_CLUSTERS, S5_CLUSTER_CH, S5_CLUSTER_LANES)

        def cluster_out(w):
            w = w.reshape(S5_CLUSTERS, S5_CLUSTER_GROUPS, S5_GROUP_CH, S5_STATE)
            return jnp.einsum('ab,kacp->kapbc', eye, w).reshape(S5_CLUSTERS, S5_CLUSTER_LANES, S5_CLUSTER_CH)

        bw = jnp.concatenate([cluster_in(jnp.real(b_bar[d])), cluster_in(jnp.imag(b_bar[d]))], axis=-1)
        cwr = cluster_out(c_re[d])
        cwi = cluster_out(-c_im[d])
        lam_d = lam_bar[d].reshape(S5_LANES)
        pows = [lam_d]
        for _ in range(SCAN_TILE - 1):
            pows.append(pows[-1] * lam_d)
        pows = jnp.stack(pows)
        j = jnp.arange(SCAN_TILE)
        tabs = []
        for k in SCAN_LEVELS:
            valid = (j <= SCAN_TILE - 1 - k) if d == 1 else (j >= k)
            p = jnp.where(valid[:, None], pows[k - 1][None, :], 0.0)
            tabs += [jnp.real(p), jnp.imag(p)]
        pc = pows[::-1] if d == 1 else pows
        tabs += [jnp.real(pc), jnp.imag(pc)]
        out.append((bw.astype(BF16), cwr.astype(BF16), cwi.astype(BF16), jnp.stack(tabs).astype(F32)))
    return out


def _final_norm_kernel(x_ref, g_ref, o_ref):
    x = x_ref[...]
    ms = jnp.mean(x * x, axis=-1, keepdims=True)
    o_ref[...] = x * lax.rsqrt(ms + RMS_EPS) * g_ref[...]


def _final_norm(x, g, row0, t):
    tm = ROW_TILE
    b0 = row0 // tm
    return pl.pallas_call(
        _final_norm_kernel,
        grid=(t // tm,),
        in_specs=[pl.BlockSpec((tm, D_MODEL), lambda i: (b0 + i, 0)), _resident((1, D_MODEL))],
        out_specs=pl.BlockSpec((tm, D_MODEL), lambda i: (i, 0)),
        out_shape=jax.ShapeDtypeStruct((t, D_MODEL), F32),
        compiler_params=_params(("parallel",)),
        name="final_norm",
    )(x, g.reshape(1, D_MODEL))


def _rope_tables(L):
    n_rows = L // GRID_W
    row_idx = jnp.repeat(jnp.arange(n_rows, dtype=F32), GRID_W)
    col_idx = (jnp.arange(L) % GRID_W).astype(F32)
    inv = ROPE_THETA ** (-jnp.arange(0, ROPE_AXIS_DIM, 2, dtype=F32) / ROPE_AXIS_DIM)
    ang = jnp.concatenate([row_idx[:, None] * inv, col_idx[:, None] * inv], axis=-1)
    cos, sin = jnp.cos(ang), jnp.sin(ang)
    cos_full = jnp.repeat(cos, 2, axis=-1)
    sin_signed = jnp.stack([-sin, sin], axis=-1).reshape(L, HEAD_DIM)
    return cos_full, sin_signed


def kernel(x_prompt, x_sample, cache_k_l0, cache_v_l0, state_s5_re_l1, state_s5_im_l1, c, c_ctx,
           ada_w_l0, ada_b_l0, norm1_l0, norm2_l0,
           w_in_l0, w_out_l0, q_norm_l0, k_norm_l0, hy_conv_w_l0, hy_conv_b_l0,
           hy_ffn_w1_l0, hy_ffn_b1_l0, hy_ffn_w2_l0, hy_ffn_b2_l0, hy_ffn_w3_l0, hy_sin_freq_l0, hy_bias_l0,
           router_l0, router_bias_l0, exp_w13_l0, exp_w2_l0, shared_w13_l0, shared_w2_l0,
           ada_w_l1, ada_b_l1, norm1_l1, norm2_l1,
           w_in_l1, s5_a_re_l1, s5_a_im_l1, s5_log_dt_l1, s5_b_re_l1, s5_b_im_l1,
           s5_c_re_l1, s5_c_im_l1, s5_d_l1, w_glu_l1,
           router_l1, router_bias_l1, exp_w13_l1, exp_w2_l1, shared_w13_l1, shared_w2_l1,
           final_norm):
    x = (x_prompt.reshape(T_CTX, D_MODEL), x_sample.reshape(T_LAT, D_MODEL))
    cond = jnp.concatenate([c_ctx[None, :], c, jnp.zeros((N_COND - 1 - DEC_BATCH, D_MODEL), F32)], axis=0)

    mods = _ada_mods(cond, ada_w_l0, ada_b_l0)
    proj = _norm_mod_matmul(x, norm1_l0, mods, 0, 1, w_in_l0.astype(BF16))

    q_c, k_c, v_c, new_k = _qk_prep(proj, 0, T_CTX, q_norm_l0, k_norm_l0, None)
    attn_c = _attention(q_c, k_c, v_c, BATCH, SEQ, None)
    q_l, k_l, v_l = _qk_prep(proj, T_CTX, T_LAT, q_norm_l0, k_norm_l0, _rope_tables(DEC_SEQ))
    ctx_kv = (cache_k_l0.reshape(DEC_BATCH * PAST_LEN, KV_WIDTH).astype(BF16),
              cache_v_l0.reshape(DEC_BATCH * PAST_LEN, KV_WIDTH).astype(BF16))
    attn_l = _attention(q_l, k_l, v_l, DEC_BATCH, DEC_SEQ, ctx_kv)

    hy_f = (hy_ffn_w1_l0, hy_ffn_b1_l0, hy_ffn_w2_l0, hy_ffn_b2_l0, hy_ffn_w3_l0, hy_sin_freq_l0)
    spec_c, mats_c = _hyena_operands(SEQ, HY_BLOCK_CTX, *hy_f)
    z_c = _hyena(proj, 0, BATCH, SEQ, HY_BLOCK_CTX, HY_TC_CTX, hy_conv_w_l0, hy_conv_b_l0, hy_bias_l0,
                 spec_c, mats_c)
    spec_l, mats_l = _hyena_operands(DEC_SEQ, HY_BLOCK_LAT, *hy_f)
    z_l = _hyena(proj, T_CTX, DEC_BATCH, DEC_SEQ, HY_BLOCK_LAT, HY_TC_LAT, hy_conv_w_l0, hy_conv_b_l0, hy_bias_l0,
                 spec_l, mats_l)

    w_out = w_out_l0.astype(BF16)
    x = _out_proj((attn_c, attn_l), (z_c, z_l), x, mods, 2, w_out[:Q_WIDTH], w_out[Q_WIDTH:])
    x = _moe_streams(x, norm2_l0, mods, 3, 4, 5, router_l0, router_bias_l0, exp_w13_l0, exp_w2_l0,
                     shared_w13_l0, shared_w2_l0)

    new_v = proj[:T_CTX, Q_WIDTH + KV_WIDTH:QKV_WIDTH]

    mods = _ada_mods(cond, ada_w_l1, ada_b_l1)
    u = _norm_mod_matmul(x, norm1_l1, mods, 0, 1, w_in_l1.astype(BF16))
    s5_w = _s5_weights(s5_a_re_l1, s5_a_im_l1, s5_log_dt_l1, s5_b_re_l1, s5_b_im_l1, s5_c_re_l1, s5_c_im_l1)
    zero_state = jnp.zeros((BATCH, 1, S5_LANES), F32)
    ys, finals = [], []
    for d in range(2):
        y_c, f_re, f_im = _s5_direction(u, 0, BATCH, SEQ, zero_state, zero_state, *s5_w[d], reverse=d == 1)
        y_l, _, _ = _s5_direction(u, T_CTX, DEC_BATCH, DEC_SEQ,
                                  state_s5_re_l1[:, d].reshape(DEC_BATCH, 1, S5_LANES),
                                  state_s5_im_l1[:, d].reshape(DEC_BATCH, 1, S5_LANES), *s5_w[d], reverse=d == 1)
        ys.append((y_c, y_l))
        finals.append((f_re.reshape(BATCH, S5_GROUPS, S5_STATE), f_im.reshape(BATCH, S5_GROUPS, S5_STATE)))
    s_re = jnp.stack([finals[0][0], finals[1][0]], axis=1)
    s_im = jnp.stack([finals[0][1], finals[1][1]], axis=1)
    x = _glu_proj(ys[0], ys[1], u, s5_d_l1, x, mods, 2, w_glu_l1.astype(BF16))
    x_c, x_l = _moe_streams(x, norm2_l1, mods, 3, 4, 5, router_l1, router_bias_l1, exp_w13_l1, exp_w2_l1,
                            shared_w13_l1, shared_w2_l1)

    return (_final_norm(x_c, final_norm, 0, T_CTX).reshape(BATCH, SEQ, D_MODEL),
            _final_norm(x_l, final_norm, 0, T_LAT).reshape(DEC_BATCH, DEC_SEQ, D_MODEL),
            new_k.reshape(BATCH, SEQ, N_KV_HEADS, HEAD_DIM),
            new_v.reshape(BATCH, SEQ, N_KV_HEADS, HEAD_DIM),
            s_re, s_im)
```

```python
import functools
import math

import jax
import jax.numpy as jnp
from jax import lax
from jax.experimental import pallas as pl
from jax.experimental.pallas import tpu as pltpu

F32 = jnp.float32
BF16 = jnp.bfloat16

D_MODEL = 2048
BATCH = 32
SEQ = 256
DEC_BATCH = 4
DEC_SEQ = 4096
PAST_LEN = 256
GRID_W = 64
RMS_EPS = 1e-6
HEAD_DIM = 128
N_HEADS = 8
N_KV_HEADS = 2
GQA_GROUP = N_HEADS // N_KV_HEADS
Q_WIDTH = N_HEADS * HEAD_DIM
KV_WIDTH = N_KV_HEADS * HEAD_DIM
QKV_WIDTH = Q_WIDTH + 2 * KV_WIDTH
ROPE_THETA = 10000.0
ROPE_AXIS_DIM = HEAD_DIM // 2
HYENA_WIDTH = D_MODEL // 2
HYENA_ORDER = 2
FILTER_BANDS = 16
DECAY_TARGET = 1e-2
DECAY_PCT_SHORT = 0.3
DECAY_PCT_LONG = 1.5
S5_WIDTH = D_MODEL // 2
S5_GROUP_CH = 16
S5_GROUPS = S5_WIDTH // S5_GROUP_CH
S5_STATE = 64
N_EXPERTS = 64
TOP_K = 8
N_EXPERT_GROUPS = 8
TOPK_GROUPS = 4
EXPERT_HIDDEN = 512
ROUTED_SCALE = 2.5
MOE_BLOCK = 256

T_CTX = BATCH * SEQ
T_LAT = DEC_BATCH * DEC_SEQ
T_ALL = T_CTX + T_LAT
N_COND = 8

ROW_TILE = 256
V7X_VMEM_LIMIT_BYTES = 56 * 1024 * 1024


def _params(semantics):
    return pltpu.CompilerParams(dimension_semantics=semantics, vmem_limit_bytes=V7X_VMEM_LIMIT_BYTES)


def _cond_row(i, tm, row0=0):
    tok = row0 + i * tm
    return jnp.where(tok < T_CTX, 0, 1 + (tok - T_CTX) // DEC_SEQ)


def _mod_spec(k, tm, row0=0):
    return pl.BlockSpec((1, 1, 1, D_MODEL), lambda i: (_cond_row(i, tm, row0), k, 0, 0))


def _resident(shape):
    nd = len(shape)
    return pl.BlockSpec(shape, lambda i: (0,) * nd, pipeline_mode=pl.Buffered(1))


def _silu(x):
    return x * jax.nn.sigmoid(x)


def _norm_mod(x, g, sc, sh):
    ms = jnp.mean(x * x, axis=-1, keepdims=True)
    return (x * lax.rsqrt(ms + RMS_EPS) * g) * (1.0 + sc) + sh


def _ada_kernel(c_ref, w_ref, b_ref, o_ref):
    a = _silu(c_ref[...]).astype(BF16)
    o_ref[...] = jnp.dot(a, w_ref[...].astype(BF16), preferred_element_type=F32) + b_ref[...]


def _ada_mods(cond, w, b):
    n = w.shape[1]
    tn = 1024
    out = pl.pallas_call(
        _ada_kernel,
        grid=(n // tn,),
        in_specs=[pl.BlockSpec((N_COND, D_MODEL), lambda j: (0, 0)),
                  pl.BlockSpec((D_MODEL, tn), lambda j: (0, j)),
                  pl.BlockSpec((1, tn), lambda j: (0, j))],
        out_specs=pl.BlockSpec((N_COND, tn), lambda j: (0, j)),
        out_shape=jax.ShapeDtypeStruct((N_COND, n), F32),
        compiler_params=_params(("parallel",)),
        name="ada_mods",
    )(cond, w, b.reshape(1, n))
    return out.reshape(N_COND, 6, 1, D_MODEL)


def _split_row_specs(width, tm):
    n_ctx = T_CTX // tm
    return [pl.BlockSpec((tm, width), lambda i: (jnp.minimum(i, n_ctx - 1), 0)),
            pl.BlockSpec((tm, width), lambda i: (jnp.maximum(i - n_ctx, 0), 0))]


def _pick_rows(ctx_ref, lat_ref):
    return jnp.where(pl.program_id(0) < T_CTX // ctx_ref.shape[0], ctx_ref[...], lat_ref[...])


def _nmm_kernel(*refs, split):
    if split:
        xc_ref, xl_ref, g_ref, sc_ref, sh_ref, w_ref, o_ref = refs
        x = _pick_rows(xc_ref, xl_ref)
    else:
        x_ref, g_ref, sc_ref, sh_ref, w_ref, o_ref = refs
        x = x_ref[...]
    h = _norm_mod(x, g_ref[...], sc_ref[0, 0], sh_ref[0, 0])
    o_ref[...] = jnp.dot(h.astype(BF16), w_ref[...], preferred_element_type=F32)


def _norm_mod_matmul(x, g, mods4, k_shift, k_scale, w_bf16):
    n = w_bf16.shape[1]
    tm = ROW_TILE
    split = isinstance(x, tuple)
    x_args = list(x) if split else [x]
    x_specs = _split_row_specs(D_MODEL, tm) if split else [pl.BlockSpec((tm, D_MODEL), lambda i: (i, 0))]
    return pl.pallas_call(
        functools.partial(_nmm_kernel, split=split),
        grid=(T_ALL // tm,),
        in_specs=x_specs + [_resident((1, D_MODEL)), _mod_spec(k_scale, tm), _mod_spec(k_shift, tm),
                            _resident((D_MODEL, n))],
        out_specs=pl.BlockSpec((tm, n), lambda i: (i, 0)),
        out_shape=jax.ShapeDtypeStruct((T_ALL, n), F32),
        compiler_params=_params(("parallel",)),
        name="norm_mod_matmul",
    )(*x_args, g.reshape(1, D_MODEL), mods4, mods4, w_bf16)


def _head_norm(xh, g):
    ms = jnp.mean(xh * xh, axis=-1, keepdims=True)
    return xh * lax.rsqrt(ms + RMS_EPS) * g


def _rope(y, cos, sin_signed):
    lane = lax.broadcasted_iota(jnp.int32, y.shape, 1)
    partner = jnp.where(lane % 2 == 0, pltpu.roll(y, HEAD_DIM - 1, 1), pltpu.roll(y, 1, 1))
    return y * cos + partner * sin_signed


def _qkprep_ctx_kernel(qkv_ref, qg_ref, kg_ref, q_out, kb_out, vb_out, kf_out):
    for h in range(N_HEADS):
        sl = slice(h * HEAD_DIM, (h + 1) * HEAD_DIM)
        q_out[:, sl] = _head_norm(qkv_ref[:, sl], qg_ref[...]).astype(BF16)
    for j in range(N_KV_HEADS):
        k = _head_norm(qkv_ref[:, Q_WIDTH + j * HEAD_DIM:Q_WIDTH + (j + 1) * HEAD_DIM], kg_ref[...])
        kf_out[:, j * HEAD_DIM:(j + 1) * HEAD_DIM] = k
        kb_out[:, j * HEAD_DIM:(j + 1) * HEAD_DIM] = k.astype(BF16)
    vb_out[...] = qkv_ref[:, Q_WIDTH + KV_WIDTH:QKV_WIDTH].astype(BF16)


def _qkprep_lat_kernel(qkv_ref, qg_ref, kg_ref, cos_ref, sin_ref, q_out, kb_out, vb_out):
    cos, sin = cos_ref[...], sin_ref[...]
    for h in range(N_HEADS):
        sl = slice(h * HEAD_DIM, (h + 1) * HEAD_DIM)
        q_out[:, sl] = _rope(_head_norm(qkv_ref[:, sl], qg_ref[...]), cos, sin).astype(BF16)
    for j in range(N_KV_HEADS):
        k = _head_norm(qkv_ref[:, Q_WIDTH + j * HEAD_DIM:Q_WIDTH + (j + 1) * HEAD_DIM], kg_ref[...])
        kb_out[:, j * HEAD_DIM:(j + 1) * HEAD_DIM] = _rope(k, cos, sin).astype(BF16)
    vb_out[...] = qkv_ref[:, Q_WIDTH + KV_WIDTH:QKV_WIDTH].astype(BF16)


def _qk_prep(proj, row0, t, q_norm, k_norm, rope_tabs):
    tm = ROW_TILE
    b0 = row0 // tm
    qkv_spec = pl.BlockSpec((tm, QKV_WIDTH), lambda i: (b0 + i, 0))
    gain = _resident((1, HEAD_DIM))
    outs = [pl.BlockSpec((tm, Q_WIDTH), lambda i: (i, 0)),
            pl.BlockSpec((tm, KV_WIDTH), lambda i: (i, 0)),
            pl.BlockSpec((tm, KV_WIDTH), lambda i: (i, 0))]
    shapes = [jax.ShapeDtypeStruct((t, Q_WIDTH), BF16),
              jax.ShapeDtypeStruct((t, KV_WIDTH), BF16),
              jax.ShapeDtypeStruct((t, KV_WIDTH), BF16)]
    qg, kg = q_norm.reshape(1, HEAD_DIM), k_norm.reshape(1, HEAD_DIM)
    if rope_tabs is None:
        return pl.pallas_call(
            _qkprep_ctx_kernel, grid=(t // tm,),
            in_specs=[qkv_spec, gain, gain],
            out_specs=outs + [pl.BlockSpec((tm, KV_WIDTH), lambda i: (i, 0))],
            out_shape=shapes + [jax.ShapeDtypeStruct((t, KV_WIDTH), F32)],
            compiler_params=_params(("parallel",)), name="qk_prep_ctx",
        )(proj, qg, kg)
    cos, sin = rope_tabs
    nq = DEC_SEQ // tm
    tab = pl.BlockSpec((tm, HEAD_DIM), lambda i: (i % nq, 0))
    return pl.pallas_call(
        _qkprep_lat_kernel, grid=(t // tm,),
        in_specs=[qkv_spec, gain, gain, tab, tab],
        out_specs=outs, out_shape=shapes,
        compiler_params=_params(("parallel",)), name="qk_prep_lat",
    )(proj, qg, kg, cos, sin)


def _attn_kernel(*refs, with_ctx):
    if with_ctx:
        q_ref, k_ref, v_ref, ck_ref, cv_ref, o_ref = refs
    else:
        q_ref, k_ref, v_ref, o_ref = refs
    c = HEAD_DIM ** -0.5 * math.log2(math.e)
    nt = (((1,), (1,)), ((), ()))
    q = q_ref[...]
    s = lax.dot_general(q, k_ref[...], nt, preferred_element_type=F32)
    m = jnp.max(s, axis=-1, keepdims=True)
    if with_ctx:
        sc = lax.dot_general(q, ck_ref[...], nt, preferred_element_type=F32)
        m = jnp.maximum(m, jnp.max(sc, axis=-1, keepdims=True))
    p = jnp.exp2((s - m) * c)
    l = jnp.sum(p, axis=-1, keepdims=True)
    o = jnp.dot(p.astype(BF16), v_ref[...], preferred_element_type=F32)
    if with_ctx:
        pc = jnp.exp2((sc - m) * c)
        l = l + jnp.sum(pc, axis=-1, keepdims=True)
        o = o + jnp.dot(pc.astype(BF16), cv_ref[...], preferred_element_type=F32)
    o_ref[...] = (o / l).astype(BF16)


def _attention(q, k, v, n_batch, seq, ctx_kv):
    tq = ROW_TILE
    nq = seq // tq
    t = n_batch * seq
    q_spec = pl.BlockSpec((tq, HEAD_DIM), lambda b, h, i: (b * nq + i, h))
    kv_spec = pl.BlockSpec((seq, HEAD_DIM), lambda b, h, i: (b, h // GQA_GROUP))
    in_specs = [q_spec, kv_spec, kv_spec]
    args = [q, k, v]
    if ctx_kv is not None:
        c_spec = pl.BlockSpec((PAST_LEN, HEAD_DIM), lambda b, h, i: (b, h // GQA_GROUP))
        in_specs += [c_spec, c_spec]
        args += list(ctx_kv)
    return pl.pallas_call(
        functools.partial(_attn_kernel, with_ctx=ctx_kv is not None),
        grid=(n_batch, N_HEADS, nq),
        in_specs=in_specs,
        out_specs=pl.BlockSpec((tq, HEAD_DIM), lambda b, h, i: (b * nq + i, h)),
        out_shape=jax.ShapeDtypeStruct((t, Q_WIDTH), BF16),
        compiler_params=_params(("parallel", "parallel", "parallel")),
        name="attention_lat" if ctx_kv is not None else "attention_ctx",
    )(*args)


def _outproj_kernel(ac_ref, al_ref, zc_ref, zl_ref, xc_ref, xl_ref, gate_ref, wa_ref, wz_ref, o_ref):
    acc = jnp.dot(_pick_rows(ac_ref, al_ref), wa_ref[...], preferred_element_type=F32)
    acc = acc + jnp.dot(_pick_rows(zc_ref, zl_ref), wz_ref[...], preferred_element_type=F32)
    o_ref[...] = _pick_rows(xc_ref, xl_ref) + gate_ref[0, 0] * acc


def _out_proj(attn, z, x, mods4, k_gate, wa_bf16, wz_bf16):
    tm = ROW_TILE
    return pl.pallas_call(
        _outproj_kernel,
        grid=(T_ALL // tm,),
        in_specs=(_split_row_specs(Q_WIDTH, tm) + _split_row_specs(HYENA_WIDTH, tm) + _split_row_specs(D_MODEL, tm)
                  + [_mod_spec(k_gate, tm), _resident((Q_WIDTH, D_MODEL)), _resident((HYENA_WIDTH, D_MODEL))]),
        out_specs=pl.BlockSpec((tm, D_MODEL), lambda i: (i, 0)),
        out_shape=jax.ShapeDtypeStruct((T_ALL, D_MODEL), F32),
        compiler_params=_params(("parallel",)),
        name="out_proj",
    )(*attn, *z, *x, mods4, wa_bf16, wz_bf16)


def _gelu_tanh(x):
    return 0.5 * x * (1.0 + jnp.tanh(math.sqrt(2.0 / math.pi) * (x + 0.044715 * (x * x * x))))


def _glu_kernel(yfc_ref, yfl_ref, ybc_ref, ybl_ref, u_ref, d_ref, xc_ref, xl_ref, gate_ref, w_ref, o_ref):
    y = (_pick_rows(yfc_ref, yfl_ref) + _pick_rows(ybc_ref, ybl_ref)) + d_ref[...] * u_ref[...]
    ag = jnp.dot(_gelu_tanh(y).astype(BF16), w_ref[...], preferred_element_type=F32)
    a, g = ag[:, :D_MODEL], ag[:, D_MODEL:]
    o_ref[...] = _pick_rows(xc_ref, xl_ref) + gate_ref[0, 0] * (a * jax.nn.sigmoid(g))


def _glu_proj(y_fwd, y_bwd, u, d_skip, x, mods4, k_gate, w_bf16):
    tm = ROW_TILE
    return pl.pallas_call(
        _glu_kernel,
        grid=(T_ALL // tm,),
        in_specs=(_split_row_specs(S5_WIDTH, tm) + _split_row_specs(S5_WIDTH, tm)
                  + [pl.BlockSpec((tm, S5_WIDTH), lambda i: (i, 0)), _resident((1, S5_WIDTH))]
                  + _split_row_specs(D_MODEL, tm)
                  + [_mod_spec(k_gate, tm), _resident((S5_WIDTH, 2 * D_MODEL))]),
        out_specs=pl.BlockSpec((tm, D_MODEL), lambda i: (i, 0)),
        out_shape=jax.ShapeDtypeStruct((T_ALL, D_MODEL), F32),
        compiler_params=_params(("parallel",)),
        name="glu_proj",
    )(*y_fwd, *y_bwd, u, d_skip.reshape(1, S5_WIDTH), *x, mods4, w_bf16)


def _gmm_kernel(be_ref, nreal_ref, xs_ref, w13_ref, w2_ref, o_ref, w13_b, w2_b):
    i = pl.program_id(0)
    real = i < nreal_ref[0]
    fresh = jnp.logical_and(real, jnp.logical_or(i == 0, be_ref[i] != be_ref[jnp.maximum(i - 1, 0)]))

    @pl.when(fresh)
    def _():
        w13_b[...] = w13_ref[0].astype(BF16)
        w2_b[...] = w2_ref[0].astype(BF16)

    @pl.when(real)
    def _():
        gu = jnp.dot(xs_ref[...], w13_b[...], preferred_element_type=F32)
        hmid = _silu(gu[:, :EXPERT_HIDDEN]) * gu[:, EXPERT_HIDDEN:]
        o_ref[...] = jnp.dot(hmid.astype(BF16), w2_b[...], preferred_element_type=F32).astype(BF16)

    @pl.when(jnp.logical_not(real))
    def _():
        o_ref[...] = jnp.zeros_like(o_ref)


def _grouped_experts(blk_e, n_real, xs, w13, w2):
    n_rows = xs.shape[0]
    n_blocks = n_rows // MOE_BLOCK

    def live(i, nreal):
        return jnp.minimum(i, nreal[0] - 1)

    grid_spec = pltpu.PrefetchScalarGridSpec(
        num_scalar_prefetch=2,
        grid=(n_blocks,),
        in_specs=[pl.BlockSpec((MOE_BLOCK, D_MODEL), lambda i, be, nr: (live(i, nr), 0)),
                  pl.BlockSpec((1, D_MODEL, 2 * EXPERT_HIDDEN), lambda i, be, nr: (be[live(i, nr)], 0, 0)),
                  pl.BlockSpec((1, EXPERT_HIDDEN, D_MODEL), lambda i, be, nr: (be[live(i, nr)], 0, 0))],
        out_specs=pl.BlockSpec((MOE_BLOCK, D_MODEL), lambda i, be, nr: (i, 0)),
        scratch_shapes=[pltpu.VMEM((D_MODEL, 2 * EXPERT_HIDDEN), BF16),
                        pltpu.VMEM((EXPERT_HIDDEN, D_MODEL), BF16)],
    )
    return pl.pallas_call(
        _gmm_kernel,
        grid_spec=grid_spec,
        out_shape=jax.ShapeDtypeStruct((n_rows, D_MODEL), BF16),
        compiler_params=_params(("arbitrary",)),
        name="grouped_experts",
    )(blk_e, n_real, xs, w13, w2)


COMBINE_TILE = 256


def _shared_kernel(h_ref, x_ref, ye_ref, w_ref, gate_ref, w13_ref, w2_ref, o_ref):
    gu = jnp.dot(h_ref[...], w13_ref[...], preferred_element_type=F32)
    hmid = _silu(gu[:, :EXPERT_HIDDEN]) * gu[:, EXPERT_HIDDEN:]
    acc = jnp.dot(hmid.astype(BF16), w2_ref[...], preferred_element_type=F32)
    wts = w_ref[...]
    for k in range(TOP_K):
        acc = acc + wts[:, k:k + 1] * ye_ref[k].astype(F32)
    o_ref[...] = x_ref[...] + gate_ref[0, 0] * acc


def _shared_and_combine(h_bf16, x, row0, t, y_exp, wts, mods4, k_gate, w13_bf16, w2_bf16):
    tm = COMBINE_TILE
    b0 = row0 // tm
    row = lambda i: (i, 0)
    off = lambda i: (b0 + i, 0)
    return pl.pallas_call(
        _shared_kernel,
        grid=(t // tm,),
        in_specs=[pl.BlockSpec((tm, D_MODEL), off), pl.BlockSpec((tm, D_MODEL), off),
                  pl.BlockSpec((TOP_K, tm, D_MODEL), lambda i: (0, i, 0)),
                  pl.BlockSpec((tm, TOP_K), row), _mod_spec(k_gate, tm, row0),
                  _resident((D_MODEL, 2 * EXPERT_HIDDEN)), _resident((EXPERT_HIDDEN, D_MODEL))],
        out_specs=pl.BlockSpec((tm, D_MODEL), row),
        out_shape=jax.ShapeDtypeStruct((t, D_MODEL), F32),
        compiler_params=_params(("parallel",)),
        name="shared_expert_combine",
    )(h_bf16, x, y_exp, wts, mods4, w13_bf16, w2_bf16)


ROUTE_TILE = 256
GROUP_SIZE = N_EXPERTS // N_EXPERT_GROUPS
NEG_INF = float("-inf")


def _first_argmax(v, index, sentinel):
    m = jnp.max(v, axis=0, keepdims=True)
    first = jnp.min(jnp.where(v == m, index, sentinel), axis=0, keepdims=True)
    return m, first


def _router_kernel(x_ref, g_ref, sc_ref, sh_ref, wt_ref, rb_ref, tri_ref,
                   h_out, idx_out, gate_out, rank_out, cnt_out, base_ref):
    tm = x_ref.shape[0]
    step = pl.program_id(0)

    @pl.when(jnp.logical_or(step == 0, step == T_CTX // tm))
    def _():
        base_ref[...] = jnp.zeros_like(base_ref)

    hb = _norm_mod(x_ref[...], g_ref[...], sc_ref[0, 0], sh_ref[0, 0]).astype(BF16)
    h_out[...] = hb
    logits = lax.dot_general(wt_ref[...], hb, (((1,), (1,)), ((), ())), preferred_element_type=F32)
    scores = jax.nn.sigmoid(logits)
    choice = scores + rb_ref[...]

    member = lax.broadcasted_iota(jnp.int32, (GROUP_SIZE, tm), 0).astype(F32)
    group_scores = []
    for g in range(N_EXPERT_GROUPS):
        cg = choice[g * GROUP_SIZE:(g + 1) * GROUP_SIZE, :]
        m1, first = _first_argmax(cg, member, float(GROUP_SIZE))
        m2 = jnp.max(jnp.where(member == first, NEG_INF, cg), axis=0, keepdims=True)
        group_scores.append(m1 + m2)
    gs = jnp.concatenate(group_scores, axis=0)

    group = lax.broadcasted_iota(jnp.int32, (N_EXPERT_GROUPS, tm), 0).astype(F32)
    keep = jnp.zeros((N_EXPERT_GROUPS, tm), F32)
    for _ in range(TOPK_GROUPS):
        _, first = _first_argmax(gs, group, float(N_EXPERT_GROUPS))
        sel = group == first
        keep = jnp.where(sel, 1.0, keep)
        gs = jnp.where(sel, NEG_INF, gs)
    masked = jnp.concatenate(
        [jnp.where(keep[g:g + 1, :] > 0.0, choice[g * GROUP_SIZE:(g + 1) * GROUP_SIZE, :], NEG_INF)
         for g in range(N_EXPERT_GROUPS)], axis=0)

    expert = lax.broadcasted_iota(jnp.int32, (N_EXPERTS, tm), 0).astype(F32)
    onehot = jnp.zeros((N_EXPERTS, tm), F32)
    sels, idx_rows, gate_rows = [], [], []
    for _ in range(TOP_K):
        _, first = _first_argmax(masked, expert, float(N_EXPERTS))
        sel = expert == first
        sels.append(sel)
        idx_rows.append(first)
        gate_rows.append(jnp.sum(jnp.where(sel, scores, 0.0), axis=0, keepdims=True))
        masked = jnp.where(sel, NEG_INF, masked)
        onehot = jnp.where(sel, 1.0, onehot)
    gates = jnp.concatenate(gate_rows, axis=0)
    gate_out[...] = gates / jnp.sum(gates, axis=0, keepdims=True) * ROUTED_SCALE
    idx_out[...] = jnp.concatenate(idx_rows, axis=0).astype(jnp.int32)

    cum = jnp.dot(onehot.astype(BF16), tri_ref[...], preferred_element_type=F32) + base_ref[:, 0:1]
    rank_rows = [jnp.sum(jnp.where(sel, cum, 0.0), axis=0, keepdims=True) - 1.0 for sel in sels]
    rank_out[...] = jnp.concatenate(rank_rows, axis=0).astype(jnp.int32)
    base_ref[...] = jnp.broadcast_to(cum[:, tm - 1:tm], base_ref.shape)
    cnt_out[0] = base_ref[...].astype(jnp.int32)


def _router(x, norm_g, mods4, k_shift, k_scale, router_w_t, router_bias):
    t = T_ALL
    tm = ROUTE_TILE
    n_ctx = T_CTX // tm
    tri = (jnp.arange(tm)[:, None] <= jnp.arange(tm)[None, :]).astype(BF16)
    choice_rows = pl.BlockSpec((TOP_K, tm), lambda i: (0, i))
    return pl.pallas_call(
        _router_kernel,
        grid=(t // tm,),
        in_specs=[pl.BlockSpec((tm, D_MODEL), lambda i: (i, 0)), _resident((1, D_MODEL)),
                  _mod_spec(k_scale, tm), _mod_spec(k_shift, tm),
                  _resident((N_EXPERTS, D_MODEL)), _resident((N_EXPERTS, 1)), _resident((tm, tm))],
        out_specs=[pl.BlockSpec((tm, D_MODEL), lambda i: (i, 0)), choice_rows, choice_rows, choice_rows,
                   pl.BlockSpec((1, N_EXPERTS, 128), lambda i: (jnp.where(i < n_ctx, 0, 1), 0, 0))],
        out_shape=[jax.ShapeDtypeStruct((t, D_MODEL), BF16),
                   jax.ShapeDtypeStruct((TOP_K, t), jnp.int32), jax.ShapeDtypeStruct((TOP_K, t), F32),
                   jax.ShapeDtypeStruct((TOP_K, t), jnp.int32),
                   jax.ShapeDtypeStruct((2, N_EXPERTS, 128), jnp.int32)],
        scratch_shapes=[pltpu.VMEM((N_EXPERTS, 128), F32)],
        compiler_params=_params(("arbitrary",)),
        name="router",
    )(x, norm_g.reshape(1, D_MODEL), mods4, mods4, router_w_t, router_bias.reshape(N_EXPERTS, 1), tri)


def _pos_kernel(idx_ref, rank_ref, off_ref, pos_out):
    tm = idx_ref.shape[1]
    expert = lax.broadcasted_iota(jnp.int32, (N_EXPERTS, tm), 0)
    rows = []
    for k in range(TOP_K):
        sel = expert == idx_ref[k:k + 1, :]
        rows.append(jnp.sum(jnp.where(sel, off_ref[...], 0.0), axis=0, keepdims=True))
    pos_out[...] = jnp.concatenate(rows, axis=0).astype(jnp.int32) + rank_ref[...]


def _positions(idx, rank, offsets):
    t = idx.shape[1]
    tm = 1024
    rows = pl.BlockSpec((TOP_K, tm), lambda i: (0, i))
    return pl.pallas_call(
        _pos_kernel, grid=(t // tm,),
        in_specs=[rows, rows, _resident((N_EXPERTS, 1))],
        out_specs=rows, out_shape=jax.ShapeDtypeStruct((TOP_K, t), jnp.int32),
        compiler_params=_params(("parallel",)), name="expert_positions",
    )(idx, rank, offsets.astype(F32).reshape(N_EXPERTS, 1))


def _moe_stream(x, row0, t, h_bf16, top_idx, gate, rank, counts, mods4, k_gate, exp_w13, exp_w2,
                sh_w13_bf16, sh_w2_bf16):
    padded = (counts + MOE_BLOCK - 1) // MOE_BLOCK * MOE_BLOCK
    e_iota = jnp.arange(N_EXPERTS)
    pend = jnp.sum(jnp.where(e_iota[:, None] <= e_iota[None, :], padded[:, None], 0), axis=0)
    pos = _positions(top_idx, rank, pend - padded)
    n_rows = -(-(t * TOP_K + N_EXPERTS * (MOE_BLOCK - 1)) // MOE_BLOCK) * MOE_BLOCK
    n_blocks = n_rows // MOE_BLOCK
    blk_start = jnp.arange(n_blocks, dtype=pend.dtype) * MOE_BLOCK
    blk_e = jnp.minimum(jnp.sum((pend[None, :] <= blk_start[:, None]).astype(jnp.int32), axis=1), N_EXPERTS - 1)
    n_real = (pend[N_EXPERTS - 1:] // MOE_BLOCK).astype(jnp.int32)
    tok = jnp.broadcast_to(jnp.arange(t, dtype=jnp.int32)[None, :], (TOP_K, t))
    _, sorted_tok = lax.sort_key_val(pos.reshape(-1), tok.reshape(-1))
    dense_end = jnp.sum(jnp.where(e_iota[:, None] <= e_iota[None, :], counts[:, None], 0), axis=0)
    shift_blk = ((pend - padded) - (dense_end - counts))[blk_e]
    end_blk = dense_end[blk_e]
    row = jnp.arange(n_rows, dtype=jnp.int32)
    dense = row - jnp.repeat(shift_blk, MOE_BLOCK)
    row_tok = jnp.where(dense < jnp.repeat(end_blk, MOE_BLOCK),
                        sorted_tok[jnp.clip(dense, 0, t * TOP_K - 1)], row % t)
    xs = h_bf16[row_tok + row0]
    ys = _grouped_experts(blk_e, n_real, xs, exp_w13, exp_w2)
    y_exp = ys[pos.reshape(-1)].reshape(TOP_K, t, D_MODEL)
    return _shared_and_combine(h_bf16, x, row0, t, y_exp, gate.T, mods4, k_gate, sh_w13_bf16, sh_w2_bf16)


def _moe_streams(x, norm_g, mods4, k_shift, k_scale, k_gate, router_w, router_bias, exp_w13, exp_w2, sh_w13, sh_w2):
    h_bf16, top_idx, gate, rank, counts = _router(x, norm_g, mods4, k_shift, k_scale, router_w.T.astype(BF16),
                                                  router_bias)
    sh13, sh2 = sh_w13.astype(BF16), sh_w2.astype(BF16)
    outs = []
    for s, (row0, t) in enumerate(((0, T_CTX), (T_CTX, T_LAT))):
        cols = slice(row0, row0 + t)
        outs.append(_moe_stream(x, row0, t, h_bf16, top_idx[:, cols], gate[:, cols], rank[:, cols], counts[s, :, 0],
                                mods4, k_gate, exp_w13, exp_w2, sh13, sh2))
    return tuple(outs)


HY_BLOCK_CTX, HY_TC_CTX = 256, 512
HY_BLOCK_LAT, HY_TC_LAT = 512, 128


def _hyena_filters(L, w1, b1, w2, b2, w3, sin_freq):
    t_norm = jnp.linspace(0.0, 1.0, L, dtype=F32)[:, None]
    omega = 2.0 * math.pi * jnp.arange(L, dtype=F32)[:, None] / L
    bands = jnp.linspace(1e-4, FILTER_BANDS - 1, FILTER_BANDS, dtype=F32)[None, :]
    z = jnp.concatenate([t_norm, jnp.cos(bands * omega), -jnp.sin(bands * omega)], axis=-1)
    h = jnp.sin(sin_freq * (z @ w1 + b1))
    h = jnp.sin(sin_freq * (h @ w2 + b2))
    h = (h @ w3).reshape(L, 2, HYENA_ORDER, HYENA_WIDTH)
    deltas = jnp.abs(jnp.linspace(math.log(DECAY_TARGET) / DECAY_PCT_LONG,
                                  math.log(DECAY_TARGET) / DECAY_PCT_SHORT, HYENA_WIDTH, dtype=F32))
    h = h * jnp.exp(-t_norm * deltas)[:, None, None, :]
    fwd, bwd = h[:, 0], h[:, 1] * (t_norm > 0.0)[:, :, None]
    norm = jnp.sum(jnp.abs(fwd), axis=0, keepdims=True) + jnp.sum(jnp.abs(bwd), axis=0, keepdims=True)
    return fwd / norm, bwd / norm


HY_ACC_ELEMS = 8192


def _dft_matrices(bk):
    k = jnp.arange(bk, dtype=jnp.int32)
    phase = ((2 * k[:, None] + 1) * k[None, :]) % (4 * bk)
    ang = phase.astype(F32) * (math.pi / (2 * bk))
    return jnp.cos(ang), jnp.sin(ang)


def _spectra_kernel(xf_ref, xb_ref, yf_ref, yb_ref, c1_ref, s1_ref, c2_ref, s2_ref, o_ref, *, nb):
    d = pl.program_id(1) - (nb - 1)
    x = jnp.where(d >= 0, xf_ref[0], xb_ref[0]).astype(BF16)
    y = jnp.where(d >= 1, yf_ref[0], yb_ref[0]).astype(BF16)
    cy = jnp.where(d == 0, c1_ref[...], c2_ref[...])
    sy = jnp.where(d == 0, s1_ref[...], s2_ref[...])
    re = jnp.dot(c1_ref[...], x, preferred_element_type=F32) + jnp.dot(cy, y, preferred_element_type=F32)
    im = jnp.dot(sy, y, preferred_element_type=F32) - jnp.dot(s1_ref[...], x, preferred_element_type=F32)
    o_ref[0, 0, 0] = re
    o_ref[0, 0, 1] = jnp.where(d >= 0, im, -im)


def _filter_spectra(fwd, bwd, bk, tc):
    L = fwd.shape[0]
    nb = L // bk
    n_ct = HYENA_WIDTH // tc
    cos, sin = _dft_matrices(bk)
    k = jnp.arange(bk, dtype=jnp.int32)
    phase = ((2 * k[:, None] + 1) * (bk - k[None, :])) % (4 * bk)
    ang = phase.astype(F32) * (math.pi / (2 * bk))
    live = (k[None, :] > 0).astype(F32)
    mats = [m.astype(BF16) for m in (cos, sin, jnp.cos(ang) * live, jnp.sin(ang) * live)]

    def seg(pick):
        return pl.BlockSpec((1, bk, tc), lambda o, di, ct: (pick(di - (nb - 1)), 0, o * n_ct + ct))

    mat = pl.BlockSpec((bk, bk), lambda o, di, ct: (0, 0), pipeline_mode=pl.Buffered(1))
    halves = [h.reshape(nb, bk, HYENA_ORDER * HYENA_WIDTH) for h in (fwd, bwd)]
    return pl.pallas_call(
        functools.partial(_spectra_kernel, nb=nb),
        grid=(HYENA_ORDER, 2 * nb - 1, n_ct),
        in_specs=[seg(lambda d: jnp.maximum(d, 0)), seg(lambda d: jnp.maximum(-d, 0)),
                  seg(lambda d: jnp.maximum(d - 1, 0)), seg(lambda d: jnp.maximum(-d - 1, 0)),
                  mat, mat, mat, mat],
        out_specs=pl.BlockSpec((1, 1, 2, bk, tc), lambda o, di, ct: (o, di, 0, 0, ct)),
        out_shape=jax.ShapeDtypeStruct((HYENA_ORDER, 2 * nb - 1, 2, bk, HYENA_WIDTH), F32),
        compiler_params=_params(("parallel", "parallel", "parallel")),
        name="filter_spectra",
    )(halves[0], halves[1], halves[0], halves[1], *mats)


def _hyena_kernel(v_ref, x1_ref, x2_ref, cw_ref, cb_ref, hb_ref, k0_ref, k1_ref, c_ref, s_ref, ct_ref, st_ref,
                  o_ref, vs_ref, z1_ref, x2s_ref, ur_ref, ui_ref, yr_ref, yi_ref, *, seq, bk, tc):
    nb = seq // bk
    ft = HY_ACC_ELEMS // tc
    row = lax.broadcasted_iota(jnp.int32, (seq, tc), 0)

    def short_conv(x, j):
        prev = jnp.where(row == 0, 0.0, pltpu.roll(x, 1, 0))
        nxt = jnp.where(row == seq - 1, 0.0, pltpu.roll(x, seq - 1, 0))
        return (prev * cw_ref[0, j:j + 1, :] + x * cw_ref[1, j:j + 1, :] + nxt * cw_ref[2, j:j + 1, :]
                + cb_ref[j:j + 1, :])

    vs_ref[...] = short_conv(v_ref[...], 0)
    z1_ref[...] = short_conv(x1_ref[...], 1)
    x2s_ref[...] = short_conv(x2_ref[...], 2)

    def long_conv(src_ref, k_ref, emit):
        for blk in range(nb):
            ub = src_ref[blk * bk:(blk + 1) * bk, :].astype(BF16)
            ur_ref[blk] = jnp.dot(c_ref[...], ub, preferred_element_type=F32)
            ui_ref[blk] = -jnp.dot(s_ref[...], ub, preferred_element_type=F32)
        for out_blk in range(nb):
            def acc_tile(f, carry, out_blk=out_blk):
                rows = pl.ds(pl.multiple_of(f * ft, ft), ft)
                ar = jnp.zeros((ft, tc), F32)
                ai = jnp.zeros((ft, tc), F32)
                for in_blk in range(nb):
                    d = out_blk - in_blk + nb - 1
                    kr, ki = k_ref[d, 0, rows, :], k_ref[d, 1, rows, :]
                    xr, xi = ur_ref[in_blk, rows, :], ui_ref[in_blk, rows, :]
                    ar = ar + (kr * xr - ki * xi)
                    ai = ai + (kr * xi + ki * xr)
                yr_ref[rows, :] = ar
                yi_ref[rows, :] = ai
                return carry

            lax.fori_loop(0, bk // ft, acc_tile, 0)
            y = jnp.dot(ct_ref[...], yr_ref[...].astype(BF16), preferred_element_type=F32)
            y = y - jnp.dot(st_ref[...], yi_ref[...].astype(BF16), preferred_element_type=F32)
            emit(slice(out_blk * bk, (out_blk + 1) * bk), y * (1.0 / bk))

    def emit_z1(rows, y):
        z1_ref[rows, :] = z1_ref[rows, :] * (y + hb_ref[0:1, :] * vs_ref[rows, :])

    def emit_out(rows, y):
        o_ref[rows, :] = (x2s_ref[rows, :] * (y + hb_ref[1:2, :] * z1_ref[rows, :])).astype(BF16)

    long_conv(vs_ref, k0_ref, emit_z1)
    long_conv(z1_ref, k1_ref, emit_out)


def _hyena(proj, row0, n_batch, seq, bk, tc, conv_w, conv_b, hy_bias, spectra, mats):
    nb = seq // bk
    n_ct = HYENA_WIDTH // tc
    col0 = QKV_WIDTH // tc
    per = HYENA_WIDTH // tc
    b0 = row0 // seq

    def data(which):
        return pl.BlockSpec((seq, tc), lambda ct, b: (b0 + b, col0 + which * per + ct))

    const2 = lambda shape: pl.BlockSpec(shape, lambda ct, b: (0, 0), pipeline_mode=pl.Buffered(1))
    kspec = pl.BlockSpec((2 * nb - 1, 2, bk, tc), lambda ct, b: (0, 0, 0, ct), pipeline_mode=pl.Buffered(1))
    cw = conv_w.reshape(3, HYENA_ORDER + 1, HYENA_WIDTH)
    cb = conv_b.reshape(HYENA_ORDER + 1, HYENA_WIDTH)
    return pl.pallas_call(
        functools.partial(_hyena_kernel, seq=seq, bk=bk, tc=tc),
        grid=(n_ct, n_batch),
        in_specs=[data(0), data(1), data(2),
                  pl.BlockSpec((3, HYENA_ORDER + 1, tc), lambda ct, b: (0, 0, ct)),
                  pl.BlockSpec((HYENA_ORDER + 1, tc), lambda ct, b: (0, ct)),
                  pl.BlockSpec((HYENA_ORDER, tc), lambda ct, b: (0, ct)),
                  kspec, kspec, const2((bk, bk)), const2((bk, bk)), const2((bk, bk)), const2((bk, bk))],
        out_specs=pl.BlockSpec((seq, tc), lambda ct, b: (b, ct)),
        out_shape=jax.ShapeDtypeStruct((n_batch * seq, HYENA_WIDTH), BF16),
        scratch_shapes=[pltpu.VMEM((seq, tc), F32), pltpu.VMEM((seq, tc), F32), pltpu.VMEM((seq, tc), F32),
                        pltpu.VMEM((nb, bk, tc), F32), pltpu.VMEM((nb, bk, tc), F32),
                        pltpu.VMEM((bk, tc), F32), pltpu.VMEM((bk, tc), F32)],
        compiler_params=_params(("parallel", "parallel")),
        name="hyena_seq%d" % seq,
    )(proj, proj, proj, cw, cb, hy_bias, spectra[0], spectra[1], *mats)


HY_SPECTRA_TC = 512


def _hyena_operands(L, bk, f_w1, f_b1, f_w2, f_b2, f_w3, sin_freq):
    fwd, bwd = _hyena_filters(L, f_w1, f_b1, f_w2, f_b2, f_w3, sin_freq)
    cos, sin = _dft_matrices(bk)
    mats = tuple(m.astype(BF16) for m in (cos, sin, cos.T, sin.T))
    return _filter_spectra(fwd, bwd, bk, HY_SPECTRA_TC), mats


S5_LANES = S5_GROUPS * S5_STATE
S5_CLUSTER_GROUPS = 16
S5_CLUSTERS = S5_GROUPS // S5_CLUSTER_GROUPS
S5_CLUSTER_CH = S5_CLUSTER_GROUPS * S5_GROUP_CH
S5_CLUSTER_LANES = S5_CLUSTER_GROUPS * S5_STATE
SCAN_SEQS = 8
SCAN_ROWS = 512


def _s5_kernel(u_ref, s0r_ref, s0i_ref, bw_ref, cwr_ref, cwi_ref, lam_ref, y_ref, sfr_ref, sfi_ref,
               sre, sim, car, cai, *, reverse):
    n_seq, tc, _ = u_ref.shape

    @pl.when(pl.program_id(1) == 0)
    def _():
        car[...] = jnp.zeros_like(car)
        cai[...] = jnp.zeros_like(cai)
        car[0:n_seq, :] = s0r_ref[0]
        cai[0:n_seq, :] = s0i_ref[0]

    u = u_ref[...]
    if n_seq < SCAN_SEQS:
        u = jnp.concatenate([u, jnp.zeros((SCAN_SEQS - n_seq, tc, S5_WIDTH), F32)], axis=0)
    ub = jnp.swapaxes(u, 0, 1).reshape(tc * SCAN_SEQS, S5_WIDTH).astype(BF16)
    for k in range(S5_CLUSTERS):
        bu = jnp.dot(ub[:, k * S5_CLUSTER_CH:(k + 1) * S5_CLUSTER_CH], bw_ref[k], preferred_element_type=F32)
        sre[:, k * S5_CLUSTER_LANES:(k + 1) * S5_CLUSTER_LANES] = bu[:, :S5_CLUSTER_LANES]
        sim[:, k * S5_CLUSTER_LANES:(k + 1) * S5_CLUSTER_LANES] = bu[:, S5_CLUSTER_LANES:]

    for k in range(S5_CLUSTERS):
        lanes = slice(k * S5_CLUSTER_LANES, (k + 1) * S5_CLUSTER_LANES)
        lr = jnp.broadcast_to(lam_ref[0:1, lanes], (SCAN_SEQS, S5_CLUSTER_LANES))
        li = jnp.broadcast_to(lam_ref[1:2, lanes], (SCAN_SEQS, S5_CLUSTER_LANES))

        def step(i, state, lanes=lanes, lr=lr, li=li):
            sr, si = state
            t = (tc - 1 - i) if reverse else i
            rows = pl.ds(pl.multiple_of(t * SCAN_SEQS, SCAN_SEQS), SCAN_SEQS)
            nr = (lr * sr - li * si) + sre[rows, lanes]
            ni = (lr * si + li * sr) + sim[rows, lanes]
            sre[rows, lanes] = nr
            sim[rows, lanes] = ni
            return nr, ni

        sr, si = lax.fori_loop(0, tc, step, (car[:, lanes], cai[:, lanes]), unroll=2)
        car[:, lanes] = sr
        cai[:, lanes] = si

    parts = []
    for k in range(S5_CLUSTERS):
        lanes = slice(k * S5_CLUSTER_LANES, (k + 1) * S5_CLUSTER_LANES)
        yk = jnp.dot(sre[:, lanes].astype(BF16), cwr_ref[k], preferred_element_type=F32)
        parts.append(yk + jnp.dot(sim[:, lanes].astype(BF16), cwi_ref[k], preferred_element_type=F32))
    y = jnp.concatenate(parts, axis=1).reshape(tc, SCAN_SEQS, S5_WIDTH)
    y_ref[...] = jnp.swapaxes(y, 0, 1)[0:n_seq]
    sfr_ref[0] = car[0:n_seq, :]
    sfi_ref[0] = cai[0:n_seq, :]


def _s5_direction(u, n_seq, seq, group, s0_re, s0_im, bw, cwr, cwi, lam, reverse):
    tc = SCAN_ROWS // SCAN_SEQS
    nc = seq // tc
    ng = n_seq // group

    def chunk(c):
        return (nc - 1 - c) if reverse else c

    state_spec = pl.BlockSpec((1, group, S5_LANES), lambda g, c: (g, 0, 0))
    const = lambda a: pl.BlockSpec(a.shape, lambda g, c: (0,) * a.ndim, pipeline_mode=pl.Buffered(1))
    y, f_re, f_im = pl.pallas_call(
        functools.partial(_s5_kernel, reverse=reverse),
        grid=(ng, nc),
        in_specs=[pl.BlockSpec((group, tc, S5_WIDTH), lambda g, c: (g, chunk(c), 0)),
                  state_spec, state_spec, const(bw), const(cwr), const(cwi), const(lam)],
        out_specs=[pl.BlockSpec((group, tc, S5_WIDTH), lambda g, c: (g, chunk(c), 0)),
                   state_spec, state_spec],
        out_shape=[jax.ShapeDtypeStruct((n_seq, seq, S5_WIDTH), F32),
                   jax.ShapeDtypeStruct((ng, group, S5_LANES), F32),
                   jax.ShapeDtypeStruct((ng, group, S5_LANES), F32)],
        scratch_shapes=[pltpu.VMEM((SCAN_ROWS, S5_LANES), F32), pltpu.VMEM((SCAN_ROWS, S5_LANES), F32),
                        pltpu.VMEM((SCAN_SEQS, S5_LANES), F32), pltpu.VMEM((SCAN_SEQS, S5_LANES), F32)],
        compiler_params=_params(("parallel", "arbitrary")),
        name="s5_bwd" if reverse else "s5_fwd",
    )(u.reshape(n_seq, seq, S5_WIDTH), s0_re.reshape(ng, group, S5_LANES), s0_im.reshape(ng, group, S5_LANES),
      bw, cwr, cwi, lam)
    return y.reshape(n_seq * seq, S5_WIDTH), f_re.reshape(n_seq, S5_LANES), f_im.reshape(n_seq, S5_LANES)


def _s5_weights(a_re, a_im, log_dt, b_re, b_im, c_re, c_im):
    lam = lax.complex(a_re, a_im)
    lam_bar = jnp.exp(lam * jnp.exp(log_dt)[..., None])
    b_bar = ((lam_bar - 1.0) / lam)[..., None] * lax.complex(b_re, b_im)
    eye = jnp.eye(S5_CLUSTER_GROUPS, dtype=F32)
    out = []
    for d in range(2):
        def cluster_in(w):
            w = w.reshape(S5_CLUSTERS, S5_CLUSTER_GROUPS, S5_STATE, S5_GROUP_CH)
            return jnp.einsum('ab,kapc->kacbp', eye, w).reshape(S5_CLUSTERS, S5_CLUSTER_CH, S5_CLUSTER_LANES)

        def cluster_out(w):
            w = w.reshape(S5_CLUSTERS, S5_CLUSTER_GROUPS, S5_GROUP_CH, S5_STATE)
            return jnp.einsum('ab,kacp->kapbc', eye, w).reshape(S5_CLUSTERS, S5_CLUSTER_LANES, S5_CLUSTER_CH)

        bw = jnp.concatenate([cluster_in(jnp.real(b_bar[d])), cluster_in(jnp.imag(b_bar[d]))], axis=-1)
        cwr = cluster_out(c_re[d])
        cwi = cluster_out(-c_im[d])
        lam_d = lam_bar[d].reshape(S5_LANES)
        lam2 = jnp.stack([jnp.real(lam_d), jnp.imag(lam_d)]).astype(F32)
        out.append((bw.astype(BF16), cwr.astype(BF16), cwi.astype(BF16), lam2))
    return out


def _final_norm_kernel(x_ref, g_ref, o_ref):
    x = x_ref[...]
    ms = jnp.mean(x * x, axis=-1, keepdims=True)
    o_ref[...] = x * lax.rsqrt(ms + RMS_EPS) * g_ref[...]


def _final_norm(x, g, row0, t):
    tm = ROW_TILE
    b0 = row0 // tm
    return pl.pallas_call(
        _final_norm_kernel,
        grid=(t // tm,),
        in_specs=[pl.BlockSpec((tm, D_MODEL), lambda i: (b0 + i, 0)), _resident((1, D_MODEL))],
        out_specs=pl.BlockSpec((tm, D_MODEL), lambda i: (i, 0)),
        out_shape=jax.ShapeDtypeStruct((t, D_MODEL), F32),
        compiler_params=_params(("parallel",)),
        name="final_norm",
    )(x, g.reshape(1, D_MODEL))


def _rope_tables(L):
    n_rows = L // GRID_W
    row_idx = jnp.repeat(jnp.arange(n_rows, dtype=F32), GRID_W)
    col_idx = (jnp.arange(L) % GRID_W).astype(F32)
    inv = ROPE_THETA ** (-jnp.arange(0, ROPE_AXIS_DIM, 2, dtype=F32) / ROPE_AXIS_DIM)
    ang = jnp.concatenate([row_idx[:, None] * inv, col_idx[:, None] * inv], axis=-1)
    cos, sin = jnp.cos(ang), jnp.sin(ang)
    cos_full = jnp.repeat(cos, 2, axis=-1)
    sin_signed = jnp.stack([-sin, sin], axis=-1).reshape(L, HEAD_DIM)
    return cos_full, sin_signed


def kernel(x_prompt, x_sample, cache_k_l0, cache_v_l0, state_s5_re_l1, state_s5_im_l1, c, c_ctx,
           ada_w_l0, ada_b_l0, norm1_l0, norm2_l0,
           w_in_l0, w_out_l0, q_norm_l0, k_norm_l0, hy_conv_w_l0, hy_conv_b_l0,
           hy_ffn_w1_l0, hy_ffn_b1_l0, hy_ffn_w2_l0, hy_ffn_b2_l0, hy_ffn_w3_l0, hy_sin_freq_l0, hy_bias_l0,
           router_l0, router_bias_l0, exp_w13_l0, exp_w2_l0, shared_w13_l0, shared_w2_l0,
           ada_w_l1, ada_b_l1, norm1_l1, norm2_l1,
           w_in_l1, s5_a_re_l1, s5_a_im_l1, s5_log_dt_l1, s5_b_re_l1, s5_b_im_l1,
           s5_c_re_l1, s5_c_im_l1, s5_d_l1, w_glu_l1,
           router_l1, router_bias_l1, exp_w13_l1, exp_w2_l1, shared_w13_l1, shared_w2_l1,
           final_norm):
    x = (x_prompt.reshape(T_CTX, D_MODEL), x_sample.reshape(T_LAT, D_MODEL))
    cond = jnp.concatenate([c_ctx[None, :], c, jnp.zeros((N_COND - 1 - DEC_BATCH, D_MODEL), F32)], axis=0)

    mods = _ada_mods(cond, ada_w_l0, ada_b_l0)
    proj = _norm_mod_matmul(x, norm1_l0, mods, 0, 1, w_in_l0.astype(BF16))

    q_c, k_c, v_c, new_k = _qk_prep(proj, 0, T_CTX, q_norm_l0, k_norm_l0, None)
    attn_c = _attention(q_c, k_c, v_c, BATCH, SEQ, None)
    q_l, k_l, v_l = _qk_prep(proj, T_CTX, T_LAT, q_norm_l0, k_norm_l0, _rope_tables(DEC_SEQ))
    ctx_kv = (cache_k_l0.reshape(DEC_BATCH * PAST_LEN, KV_WIDTH).astype(BF16),
              cache_v_l0.reshape(DEC_BATCH * PAST_LEN, KV_WIDTH).astype(BF16))
    attn_l = _attention(q_l, k_l, v_l, DEC_BATCH, DEC_SEQ, ctx_kv)

    hy_f = (hy_ffn_w1_l0, hy_ffn_b1_l0, hy_ffn_w2_l0, hy_ffn_b2_l0, hy_ffn_w3_l0, hy_sin_freq_l0)
    spec_c, mats_c = _hyena_operands(SEQ, HY_BLOCK_CTX, *hy_f)
    z_c = _hyena(proj, 0, BATCH, SEQ, HY_BLOCK_CTX, HY_TC_CTX, hy_conv_w_l0, hy_conv_b_l0, hy_bias_l0,
                 spec_c, mats_c)
    spec_l, mats_l = _hyena_operands(DEC_SEQ, HY_BLOCK_LAT, *hy_f)
    z_l = _hyena(proj, T_CTX, DEC_BATCH, DEC_SEQ, HY_BLOCK_LAT, HY_TC_LAT, hy_conv_w_l0, hy_conv_b_l0, hy_bias_l0,
                 spec_l, mats_l)

    w_out = w_out_l0.astype(BF16)
    x = _out_proj((attn_c, attn_l), (z_c, z_l), x, mods, 2, w_out[:Q_WIDTH], w_out[Q_WIDTH:])
    x = _moe_streams(x, norm2_l0, mods, 3, 4, 5, router_l0, router_bias_l0, exp_w13_l0, exp_w2_l0,
                     shared_w13_l0, shared_w2_l0)

    new_v = proj[:T_CTX, Q_WIDTH + KV_WIDTH:QKV_WIDTH]

    mods = _ada_mods(cond, ada_w_l1, ada_b_l1)
    u = _norm_mod_matmul(x, norm1_l1, mods, 0, 1, w_in_l1.astype(BF16))
    s5_w = _s5_weights(s5_a_re_l1, s5_a_im_l1, s5_log_dt_l1, s5_b_re_l1, s5_b_im_l1, s5_c_re_l1, s5_c_im_l1)
    zero_state = jnp.zeros((BATCH, S5_LANES), F32)
    u_c, u_l = u[:T_CTX], u[T_CTX:]
    ys, finals = [], []
    for d in range(2):
        y_c, f_re, f_im = _s5_direction(u_c, BATCH, SEQ, SCAN_SEQS, zero_state, zero_state, *s5_w[d],
                                        reverse=d == 1)
        y_l, _, _ = _s5_direction(u_l, DEC_BATCH, DEC_SEQ, DEC_BATCH,
                                  state_s5_re_l1[:, d].reshape(DEC_BATCH, S5_LANES),
                                  state_s5_im_l1[:, d].reshape(DEC_BATCH, S5_LANES), *s5_w[d], reverse=d == 1)
        ys.append((y_c, y_l))
        finals.append((f_re.reshape(BATCH, S5_GROUPS, S5_STATE), f_im.reshape(BATCH, S5_GROUPS, S5_STATE)))
    s_re = jnp.stack([finals[0][0], finals[1][0]], axis=1)
    s_im = jnp.stack([finals[0][1], finals[1][1]], axis=1)
    x = _glu_proj(ys[0], ys[1], u, s5_d_l1, x, mods, 2, w_glu_l1.astype(BF16))
    x_c, x_l = _moe_streams(x, norm2_l1, mods, 3, 4, 5, router_l1, router_bias_l1, exp_w13_l1, exp_w2_l1,
                            shared_w13_l1, shared_w2_l1)

    return (_final_norm(x_c, final_norm, 0, T_CTX).reshape(BATCH, SEQ, D_MODEL),
            _final_norm(x_l, final_norm, 0, T_LAT).reshape(DEC_BATCH, DEC_SEQ, D_MODEL),
            new_k.reshape(BATCH, SEQ, N_KV_HEADS, HEAD_DIM),
            new_v.reshape(BATCH, SEQ, N_KV_HEADS, HEAD_DIM),
            s_re, s_im)
```

```python
import functools
import math

import jax
import jax.numpy as jnp
from jax import lax
from jax.experimental import pallas as pl
from jax.experimental.pallas import tpu as pltpu

F32 = jnp.float32
BF16 = jnp.bfloat16

D_MODEL = 2048
BATCH = 32
SEQ = 256
DEC_BATCH = 4
DEC_SEQ = 4096
PAST_LEN = 256
GRID_W = 64
RMS_EPS = 1e-6
HEAD_DIM = 128
N_HEADS = 8
N_KV_HEADS = 2
GQA_GROUP = N_HEADS // N_KV_HEADS
Q_WIDTH = N_HEADS * HEAD_DIM
KV_WIDTH = N_KV_HEADS * HEAD_DIM
QKV_WIDTH = Q_WIDTH + 2 * KV_WIDTH
ROPE_THETA = 10000.0
ROPE_AXIS_DIM = HEAD_DIM // 2
HYENA_WIDTH = D_MODEL // 2
HYENA_ORDER = 2
FILTER_BANDS = 16
DECAY_TARGET = 1e-2
DECAY_PCT_SHORT = 0.3
DECAY_PCT_LONG = 1.5
S5_WIDTH = D_MODEL // 2
S5_GROUP_CH = 16
S5_GROUPS = S5_WIDTH // S5_GROUP_CH
S5_STATE = 64
N_EXPERTS = 64
TOP_K = 8
N_EXPERT_GROUPS = 8
TOPK_GROUPS = 4
EXPERT_HIDDEN = 512
ROUTED_SCALE = 2.5
MOE_BLOCK = 256

T_CTX = BATCH * SEQ
T_LAT = DEC_BATCH * DEC_SEQ
T_ALL = T_CTX + T_LAT
N_COND = 8

ROW_TILE = 256
V7X_VMEM_LIMIT_BYTES = 56 * 1024 * 1024


def _params(semantics):
    return pltpu.CompilerParams(dimension_semantics=semantics, vmem_limit_bytes=V7X_VMEM_LIMIT_BYTES)


def _cond_row(i, tm, row0=0):
    tok = row0 + i * tm
    return jnp.where(tok < T_CTX, 0, 1 + (tok - T_CTX) // DEC_SEQ)


def _mod_spec(k, tm, row0=0):
    return pl.BlockSpec((1, 1, 1, D_MODEL), lambda i: (_cond_row(i, tm, row0), k, 0, 0))


def _resident(shape):
    nd = len(shape)
    return pl.BlockSpec(shape, lambda i: (0,) * nd, pipeline_mode=pl.Buffered(1))


def _silu(x):
    return x * jax.nn.sigmoid(x)


def _norm_mod(x, g, sc, sh):
    ms = jnp.mean(x * x, axis=-1, keepdims=True)
    return (x * lax.rsqrt(ms + RMS_EPS) * g) * (1.0 + sc) + sh


def _ada_kernel(c_ref, w_ref, b_ref, o_ref):
    a = _silu(c_ref[...]).astype(BF16)
    o_ref[...] = jnp.dot(a, w_ref[...].astype(BF16), preferred_element_type=F32) + b_ref[...]


def _ada_mods(cond, w, b):
    n = w.shape[1]
    tn = 1024
    out = pl.pallas_call(
        _ada_kernel,
        grid=(n // tn,),
        in_specs=[pl.BlockSpec((N_COND, D_MODEL), lambda j: (0, 0)),
                  pl.BlockSpec((D_MODEL, tn), lambda j: (0, j)),
                  pl.BlockSpec((1, tn), lambda j: (0, j))],
        out_specs=pl.BlockSpec((N_COND, tn), lambda j: (0, j)),
        out_shape=jax.ShapeDtypeStruct((N_COND, n), F32),
        compiler_params=_params(("parallel",)),
        name="ada_mods",
    )(cond, w, b.reshape(1, n))
    return out.reshape(N_COND, 6, 1, D_MODEL)


def _split_row_specs(width, tm):
    n_ctx = T_CTX // tm
    return [pl.BlockSpec((tm, width), lambda i: (jnp.minimum(i, n_ctx - 1), 0)),
            pl.BlockSpec((tm, width), lambda i: (jnp.maximum(i - n_ctx, 0), 0))]


def _pick_rows(ctx_ref, lat_ref):
    return jnp.where(pl.program_id(0) < T_CTX // ctx_ref.shape[0], ctx_ref[...], lat_ref[...])


def _nmm_kernel(*refs, split):
    if split:
        xc_ref, xl_ref, g_ref, sc_ref, sh_ref, w_ref, o_ref = refs
        x = _pick_rows(xc_ref, xl_ref)
    else:
        x_ref, g_ref, sc_ref, sh_ref, w_ref, o_ref = refs
        x = x_ref[...]
    h = _norm_mod(x, g_ref[...], sc_ref[0, 0], sh_ref[0, 0])
    o_ref[...] = jnp.dot(h.astype(BF16), w_ref[...], preferred_element_type=F32)


def _norm_mod_matmul(x, g, mods4, k_shift, k_scale, w_bf16):
    n = w_bf16.shape[1]
    tm = ROW_TILE
    split = isinstance(x, tuple)
    x_args = list(x) if split else [x]
    x_specs = _split_row_specs(D_MODEL, tm) if split else [pl.BlockSpec((tm, D_MODEL), lambda i: (i, 0))]
    return pl.pallas_call(
        functools.partial(_nmm_kernel, split=split),
        grid=(T_ALL // tm,),
        in_specs=x_specs + [_resident((1, D_MODEL)), _mod_spec(k_scale, tm), _mod_spec(k_shift, tm),
                            _resident((D_MODEL, n))],
        out_specs=pl.BlockSpec((tm, n), lambda i: (i, 0)),
        out_shape=jax.ShapeDtypeStruct((T_ALL, n), F32),
        compiler_params=_params(("parallel",)),
        name="norm_mod_matmul",
    )(*x_args, g.reshape(1, D_MODEL), mods4, mods4, w_bf16)


def _head_norm(xh, g):
    ms = jnp.mean(xh * xh, axis=-1, keepdims=True)
    return xh * lax.rsqrt(ms + RMS_EPS) * g


def _rope(y, cos, sin_signed):
    lane = lax.broadcasted_iota(jnp.int32, y.shape, 1)
    partner = jnp.where(lane % 2 == 0, pltpu.roll(y, HEAD_DIM - 1, 1), pltpu.roll(y, 1, 1))
    return y * cos + partner * sin_signed


def _qkprep_ctx_kernel(qkv_ref, qg_ref, kg_ref, q_out, kb_out, vb_out, kf_out):
    for h in range(N_HEADS):
        sl = slice(h * HEAD_DIM, (h + 1) * HEAD_DIM)
        q_out[:, sl] = _head_norm(qkv_ref[:, sl], qg_ref[...]).astype(BF16)
    for j in range(N_KV_HEADS):
        k = _head_norm(qkv_ref[:, Q_WIDTH + j * HEAD_DIM:Q_WIDTH + (j + 1) * HEAD_DIM], kg_ref[...])
        kf_out[:, j * HEAD_DIM:(j + 1) * HEAD_DIM] = k
        kb_out[:, j * HEAD_DIM:(j + 1) * HEAD_DIM] = k.astype(BF16)
    vb_out[...] = qkv_ref[:, Q_WIDTH + KV_WIDTH:QKV_WIDTH].astype(BF16)


def _qkprep_lat_kernel(qkv_ref, qg_ref, kg_ref, cos_ref, sin_ref, q_out, kb_out, vb_out):
    cos, sin = cos_ref[...], sin_ref[...]
    for h in range(N_HEADS):
        sl = slice(h * HEAD_DIM, (h + 1) * HEAD_DIM)
        q_out[:, sl] = _rope(_head_norm(qkv_ref[:, sl], qg_ref[...]), cos, sin).astype(BF16)
    for j in range(N_KV_HEADS):
        k = _head_norm(qkv_ref[:, Q_WIDTH + j * HEAD_DIM:Q_WIDTH + (j + 1) * HEAD_DIM], kg_ref[...])
        kb_out[:, j * HEAD_DIM:(j + 1) * HEAD_DIM] = _rope(k, cos, sin).astype(BF16)
    vb_out[...] = qkv_ref[:, Q_WIDTH + KV_WIDTH:QKV_WIDTH].astype(BF16)


def _qk_prep(proj, row0, t, q_norm, k_norm, rope_tabs):
    tm = ROW_TILE
    b0 = row0 // tm
    qkv_spec = pl.BlockSpec((tm, QKV_WIDTH), lambda i: (b0 + i, 0))
    gain = _resident((1, HEAD_DIM))
    outs = [pl.BlockSpec((tm, Q_WIDTH), lambda i: (i, 0)),
            pl.BlockSpec((tm, KV_WIDTH), lambda i: (i, 0)),
            pl.BlockSpec((tm, KV_WIDTH), lambda i: (i, 0))]
    shapes = [jax.ShapeDtypeStruct((t, Q_WIDTH), BF16),
              jax.ShapeDtypeStruct((t, KV_WIDTH), BF16),
              jax.ShapeDtypeStruct((t, KV_WIDTH), BF16)]
    qg, kg = q_norm.reshape(1, HEAD_DIM), k_norm.reshape(1, HEAD_DIM)
    if rope_tabs is None:
        return pl.pallas_call(
            _qkprep_ctx_kernel, grid=(t // tm,),
            in_specs=[qkv_spec, gain, gain],
            out_specs=outs + [pl.BlockSpec((tm, KV_WIDTH), lambda i: (i, 0))],
            out_shape=shapes + [jax.ShapeDtypeStruct((t, KV_WIDTH), F32)],
            compiler_params=_params(("parallel",)), name="qk_prep_ctx",
        )(proj, qg, kg)
    cos, sin = rope_tabs
    nq = DEC_SEQ // tm
    tab = pl.BlockSpec((tm, HEAD_DIM), lambda i: (i % nq, 0))
    return pl.pallas_call(
        _qkprep_lat_kernel, grid=(t // tm,),
        in_specs=[qkv_spec, gain, gain, tab, tab],
        out_specs=outs, out_shape=shapes,
        compiler_params=_params(("parallel",)), name="qk_prep_lat",
    )(proj, qg, kg, cos, sin)


def _attn_kernel(*refs, with_ctx):
    if with_ctx:
        q_ref, k_ref, v_ref, ck_ref, cv_ref, o_ref = refs
    else:
        q_ref, k_ref, v_ref, o_ref = refs
    c = HEAD_DIM ** -0.5 * math.log2(math.e)
    nt = (((1,), (1,)), ((), ()))
    q = q_ref[...]
    s = lax.dot_general(q, k_ref[...], nt, preferred_element_type=F32)
    m = jnp.max(s, axis=-1, keepdims=True)
    if with_ctx:
        sc = lax.dot_general(q, ck_ref[...], nt, preferred_element_type=F32)
        m = jnp.maximum(m, jnp.max(sc, axis=-1, keepdims=True))
    p = jnp.exp2((s - m) * c)
    l = jnp.sum(p, axis=-1, keepdims=True)
    o = jnp.dot(p.astype(BF16), v_ref[...], preferred_element_type=F32)
    if with_ctx:
        pc = jnp.exp2((sc - m) * c)
        l = l + jnp.sum(pc, axis=-1, keepdims=True)
        o = o + jnp.dot(pc.astype(BF16), cv_ref[...], preferred_element_type=F32)
    o_ref[...] = (o / l).astype(BF16)


def _attention(q, k, v, n_batch, seq, ctx_kv):
    tq = ROW_TILE
    nq = seq // tq
    t = n_batch * seq
    q_spec = pl.BlockSpec((tq, HEAD_DIM), lambda b, h, i: (b * nq + i, h))
    kv_spec = pl.BlockSpec((seq, HEAD_DIM), lambda b, h, i: (b, h // GQA_GROUP))
    in_specs = [q_spec, kv_spec, kv_spec]
    args = [q, k, v]
    if ctx_kv is not None:
        c_spec = pl.BlockSpec((PAST_LEN, HEAD_DIM), lambda b, h, i: (b, h // GQA_GROUP))
        in_specs += [c_spec, c_spec]
        args += list(ctx_kv)
    return pl.pallas_call(
        functools.partial(_attn_kernel, with_ctx=ctx_kv is not None),
        grid=(n_batch, N_HEADS, nq),
        in_specs=in_specs,
        out_specs=pl.BlockSpec((tq, HEAD_DIM), lambda b, h, i: (b * nq + i, h)),
        out_shape=jax.ShapeDtypeStruct((t, Q_WIDTH), BF16),
        compiler_params=_params(("parallel", "parallel", "parallel")),
        name="attention_lat" if ctx_kv is not None else "attention_ctx",
    )(*args)


def _outproj_kernel(ac_ref, al_ref, zc_ref, zl_ref, xc_ref, xl_ref, gate_ref, wa_ref, wz_ref, o_ref):
    acc = jnp.dot(_pick_rows(ac_ref, al_ref), wa_ref[...], preferred_element_type=F32)
    acc = acc + jnp.dot(_pick_rows(zc_ref, zl_ref), wz_ref[...], preferred_element_type=F32)
    o_ref[...] = _pick_rows(xc_ref, xl_ref) + gate_ref[0, 0] * acc


def _out_proj(attn, z, x, mods4, k_gate, wa_bf16, wz_bf16):
    tm = ROW_TILE
    return pl.pallas_call(
        _outproj_kernel,
        grid=(T_ALL // tm,),
        in_specs=(_split_row_specs(Q_WIDTH, tm) + _split_row_specs(HYENA_WIDTH, tm) + _split_row_specs(D_MODEL, tm)
                  + [_mod_spec(k_gate, tm), _resident((Q_WIDTH, D_MODEL)), _resident((HYENA_WIDTH, D_MODEL))]),
        out_specs=pl.BlockSpec((tm, D_MODEL), lambda i: (i, 0)),
        out_shape=jax.ShapeDtypeStruct((T_ALL, D_MODEL), F32),
        compiler_params=_params(("parallel",)),
        name="out_proj",
    )(*attn, *z, *x, mods4, wa_bf16, wz_bf16)


def _gelu_tanh(x):
    return 0.5 * x * (1.0 + jnp.tanh(math.sqrt(2.0 / math.pi) * (x + 0.044715 * (x * x * x))))


def _glu_kernel(yfc_ref, yfl_ref, ybc_ref, ybl_ref, u_ref, d_ref, xc_ref, xl_ref, gate_ref, w_ref, o_ref):
    y = (_pick_rows(yfc_ref, yfl_ref) + _pick_rows(ybc_ref, ybl_ref)) + d_ref[...] * u_ref[...]
    ag = jnp.dot(_gelu_tanh(y).astype(BF16), w_ref[...], preferred_element_type=F32)
    a, g = ag[:, :D_MODEL], ag[:, D_MODEL:]
    o_ref[...] = _pick_rows(xc_ref, xl_ref) + gate_ref[0, 0] * (a * jax.nn.sigmoid(g))


def _glu_proj(y_fwd, y_bwd, u, d_skip, x, mods4, k_gate, w_bf16):
    tm = ROW_TILE
    return pl.pallas_call(
        _glu_kernel,
        grid=(T_ALL // tm,),
        in_specs=(_split_row_specs(S5_WIDTH, tm) + _split_row_specs(S5_WIDTH, tm)
                  + [pl.BlockSpec((tm, S5_WIDTH), lambda i: (i, 0)), _resident((1, S5_WIDTH))]
                  + _split_row_specs(D_MODEL, tm)
                  + [_mod_spec(k_gate, tm), _resident((S5_WIDTH, 2 * D_MODEL))]),
        out_specs=pl.BlockSpec((tm, D_MODEL), lambda i: (i, 0)),
        out_shape=jax.ShapeDtypeStruct((T_ALL, D_MODEL), F32),
        compiler_params=_params(("parallel",)),
        name="glu_proj",
    )(*y_fwd, *y_bwd, u, d_skip.reshape(1, S5_WIDTH), *x, mods4, w_bf16)


def _gmm_kernel(be_ref, nreal_ref, slot_ref, next_ref, xs_ref, w13_hbm, w2_hbm, o_ref,
                w13_f, w2_f, w13_b, w2_b, sems):
    i = pl.program_id(0)
    real = i < nreal_ref[0]
    fresh = jnp.logical_and(real, jnp.logical_or(i == 0, be_ref[i] != be_ref[jnp.maximum(i - 1, 0)]))

    def weight_copies(expert, slot):
        return (pltpu.make_async_copy(w13_hbm.at[expert], w13_f.at[slot], sems.at[slot, 0]),
                pltpu.make_async_copy(w2_hbm.at[expert], w2_f.at[slot], sems.at[slot, 1]))

    @pl.when(fresh)
    def _():
        expert, slot, nxt = be_ref[i], slot_ref[i], next_ref[i]

        @pl.when(i == 0)
        def _():
            for cp in weight_copies(expert, slot):
                cp.start()

        for cp in weight_copies(expert, slot):
            cp.wait()
        w13_b[...] = w13_f[slot].astype(BF16)
        w2_b[...] = w2_f[slot].astype(BF16)

        @pl.when(nxt >= 0)
        def _():
            for cp in weight_copies(nxt, 1 - slot):
                cp.start()

    @pl.when(real)
    def _():
        gu = jnp.dot(xs_ref[...], w13_b[...], preferred_element_type=F32)
        hmid = _silu(gu[:, :EXPERT_HIDDEN]) * gu[:, EXPERT_HIDDEN:]
        o_ref[...] = jnp.dot(hmid.astype(BF16), w2_b[...], preferred_element_type=F32).astype(BF16)

    @pl.when(jnp.logical_not(real))
    def _():
        o_ref[...] = jnp.zeros_like(o_ref)


def _grouped_experts(blk_e, n_real, padded, xs, w13, w2):
    n_rows = xs.shape[0]
    n_blocks = n_rows // MOE_BLOCK
    b_iota = jnp.arange(n_blocks)
    first = jnp.concatenate([jnp.ones((1,), jnp.int32), (blk_e[1:] != blk_e[:-1]).astype(jnp.int32)])
    run = jnp.sum(jnp.where(b_iota[:, None] <= b_iota[None, :], first[:, None], 0), axis=0) - 1
    slot = (run % 2).astype(jnp.int32)
    e_iota = jnp.arange(N_EXPERTS)
    later = jnp.logical_and(e_iota[None, :] > e_iota[:, None], padded[None, :] > 0)
    next_e = jnp.min(jnp.where(later, e_iota[None, :], N_EXPERTS), axis=1)
    next_e = jnp.where(next_e == N_EXPERTS, -1, next_e)
    next_blk = jnp.sum(jnp.where(blk_e[:, None] == e_iota[None, :], next_e[None, :], 0), axis=1).astype(jnp.int32)

    def live(i, nreal):
        return jnp.minimum(i, nreal[0] - 1)

    grid_spec = pltpu.PrefetchScalarGridSpec(
        num_scalar_prefetch=4,
        grid=(n_blocks,),
        in_specs=[pl.BlockSpec((MOE_BLOCK, D_MODEL), lambda i, be, nr, sl, nx: (live(i, nr), 0)),
                  pl.BlockSpec(memory_space=pl.ANY), pl.BlockSpec(memory_space=pl.ANY)],
        out_specs=pl.BlockSpec((MOE_BLOCK, D_MODEL), lambda i, be, nr, sl, nx: (i, 0)),
        scratch_shapes=[pltpu.VMEM((2, D_MODEL, 2 * EXPERT_HIDDEN), F32),
                        pltpu.VMEM((2, EXPERT_HIDDEN, D_MODEL), F32),
                        pltpu.VMEM((D_MODEL, 2 * EXPERT_HIDDEN), BF16),
                        pltpu.VMEM((EXPERT_HIDDEN, D_MODEL), BF16),
                        pltpu.SemaphoreType.DMA((2, 2))],
    )
    return pl.pallas_call(
        _gmm_kernel,
        grid_spec=grid_spec,
        out_shape=jax.ShapeDtypeStruct((n_rows, D_MODEL), BF16),
        compiler_params=_params(("arbitrary",)),
        name="grouped_experts",
    )(blk_e, n_real, slot, next_blk, xs, w13, w2)


COMBINE_TILE = 256


def _shared_kernel(h_ref, x_ref, ye_ref, w_ref, gate_ref, w13_ref, w2_ref, o_ref):
    gu = jnp.dot(h_ref[...], w13_ref[...], preferred_element_type=F32)
    hmid = _silu(gu[:, :EXPERT_HIDDEN]) * gu[:, EXPERT_HIDDEN:]
    acc = jnp.dot(hmid.astype(BF16), w2_ref[...], preferred_element_type=F32)
    wts = w_ref[...]
    for k in range(TOP_K):
        acc = acc + wts[:, k:k + 1] * ye_ref[k].astype(F32)
    o_ref[...] = x_ref[...] + gate_ref[0, 0] * acc


def _shared_and_combine(h_bf16, x, row0, t, y_exp, wts, mods4, k_gate, w13_bf16, w2_bf16):
    tm = COMBINE_TILE
    b0 = row0 // tm
    row = lambda i: (i, 0)
    off = lambda i: (b0 + i, 0)
    return pl.pallas_call(
        _shared_kernel,
        grid=(t // tm,),
        in_specs=[pl.BlockSpec((tm, D_MODEL), off), pl.BlockSpec((tm, D_MODEL), off),
                  pl.BlockSpec((TOP_K, tm, D_MODEL), lambda i: (0, i, 0)),
                  pl.BlockSpec((tm, TOP_K), row), _mod_spec(k_gate, tm, row0),
                  _resident((D_MODEL, 2 * EXPERT_HIDDEN)), _resident((EXPERT_HIDDEN, D_MODEL))],
        out_specs=pl.BlockSpec((tm, D_MODEL), row),
        out_shape=jax.ShapeDtypeStruct((t, D_MODEL), F32),
        compiler_params=_params(("parallel",)),
        name="shared_expert_combine",
    )(h_bf16, x, y_exp, wts, mods4, w13_bf16, w2_bf16)


ROUTE_TILE = 256
GROUP_SIZE = N_EXPERTS // N_EXPERT_GROUPS
NEG_INF = float("-inf")


def _first_argmax(v, index, sentinel):
    m = jnp.max(v, axis=0, keepdims=True)
    first = jnp.min(jnp.where(v == m, index, sentinel), axis=0, keepdims=True)
    return m, first


def _router_kernel(x_ref, g_ref, sc_ref, sh_ref, wt_ref, rb_ref, tri_ref,
                   h_out, idx_out, gate_out, rank_out, cnt_out, base_ref):
    tm = x_ref.shape[0]
    step = pl.program_id(0)

    @pl.when(jnp.logical_or(step == 0, step == T_CTX // tm))
    def _():
        base_ref[...] = jnp.zeros_like(base_ref)

    hb = _norm_mod(x_ref[...], g_ref[...], sc_ref[0, 0], sh_ref[0, 0]).astype(BF16)
    h_out[...] = hb
    logits = lax.dot_general(wt_ref[...], hb, (((1,), (1,)), ((), ())), preferred_element_type=F32)
    scores = jax.nn.sigmoid(logits)
    choice = scores + rb_ref[...]

    member = lax.broadcasted_iota(jnp.int32, (GROUP_SIZE, tm), 0).astype(F32)
    group_scores = []
    for g in range(N_EXPERT_GROUPS):
        cg = choice[g * GROUP_SIZE:(g + 1) * GROUP_SIZE, :]
        m1, first = _first_argmax(cg, member, float(GROUP_SIZE))
        m2 = jnp.max(jnp.where(member == first, NEG_INF, cg), axis=0, keepdims=True)
        group_scores.append(m1 + m2)
    gs = jnp.concatenate(group_scores, axis=0)

    group = lax.broadcasted_iota(jnp.int32, (N_EXPERT_GROUPS, tm), 0).astype(F32)
    keep = jnp.zeros((N_EXPERT_GROUPS, tm), F32)
    for _ in range(TOPK_GROUPS):
        _, first = _first_argmax(gs, group, float(N_EXPERT_GROUPS))
        sel = group == first
        keep = jnp.where(sel, 1.0, keep)
        gs = jnp.where(sel, NEG_INF, gs)
    masked = jnp.concatenate(
        [jnp.where(keep[g:g + 1, :] > 0.0, choice[g * GROUP_SIZE:(g + 1) * GROUP_SIZE, :], NEG_INF)
         for g in range(N_EXPERT_GROUPS)], axis=0)

    expert = lax.broadcasted_iota(jnp.int32, (N_EXPERTS, tm), 0).astype(F32)
    onehot = jnp.zeros((N_EXPERTS, tm), F32)
    sels, idx_rows, gate_rows = [], [], []
    for _ in range(TOP_K):
        _, first = _first_argmax(masked, expert, float(N_EXPERTS))
        sel = expert == first
        sels.append(sel)
        idx_rows.append(first)
        gate_rows.append(jnp.sum(jnp.where(sel, scores, 0.0), axis=0, keepdims=True))
        masked = jnp.where(sel, NEG_INF, masked)
        onehot = jnp.where(sel, 1.0, onehot)
    gates = jnp.concatenate(gate_rows, axis=0)
    gate_out[...] = gates / jnp.sum(gates, axis=0, keepdims=True) * ROUTED_SCALE
    idx_out[...] = jnp.concatenate(idx_rows, axis=0).astype(jnp.int32)

    cum = jnp.dot(onehot.astype(BF16), tri_ref[...], preferred_element_type=F32) + base_ref[:, 0:1]
    rank_rows = [jnp.sum(jnp.where(sel, cum, 0.0), axis=0, keepdims=True) - 1.0 for sel in sels]
    rank_out[...] = jnp.concatenate(rank_rows, axis=0).astype(jnp.int32)
    base_ref[...] = jnp.broadcast_to(cum[:, tm - 1:tm], base_ref.shape)
    cnt_out[0] = base_ref[...].astype(jnp.int32)


def _router(x, norm_g, mods4, k_shift, k_scale, router_w_t, router_bias):
    t = T_ALL
    tm = ROUTE_TILE
    n_ctx = T_CTX // tm
    tri = (jnp.arange(tm)[:, None] <= jnp.arange(tm)[None, :]).astype(BF16)
    choice_rows = pl.BlockSpec((TOP_K, tm), lambda i: (0, i))
    return pl.pallas_call(
        _router_kernel,
        grid=(t // tm,),
        in_specs=[pl.BlockSpec((tm, D_MODEL), lambda i: (i, 0)), _resident((1, D_MODEL)),
                  _mod_spec(k_scale, tm), _mod_spec(k_shift, tm),
                  _resident((N_EXPERTS, D_MODEL)), _resident((N_EXPERTS, 1)), _resident((tm, tm))],
        out_specs=[pl.BlockSpec((tm, D_MODEL), lambda i: (i, 0)), choice_rows, choice_rows, choice_rows,
                   pl.BlockSpec((1, N_EXPERTS, 128), lambda i: (jnp.where(i < n_ctx, 0, 1), 0, 0))],
        out_shape=[jax.ShapeDtypeStruct((t, D_MODEL), BF16),
                   jax.ShapeDtypeStruct((TOP_K, t), jnp.int32), jax.ShapeDtypeStruct((TOP_K, t), F32),
                   jax.ShapeDtypeStruct((TOP_K, t), jnp.int32),
                   jax.ShapeDtypeStruct((2, N_EXPERTS, 128), jnp.int32)],
        scratch_shapes=[pltpu.VMEM((N_EXPERTS, 128), F32)],
        compiler_params=_params(("arbitrary",)),
        name="router",
    )(x, norm_g.reshape(1, D_MODEL), mods4, mods4, router_w_t, router_bias.reshape(N_EXPERTS, 1), tri)


def _pos_kernel(idx_ref, rank_ref, off_ref, pos_out):
    tm = idx_ref.shape[1]
    expert = lax.broadcasted_iota(jnp.int32, (N_EXPERTS, tm), 0)
    rows = []
    for k in range(TOP_K):
        sel = expert == idx_ref[k:k + 1, :]
        rows.append(jnp.sum(jnp.where(sel, off_ref[...], 0.0), axis=0, keepdims=True))
    pos_out[...] = jnp.concatenate(rows, axis=0).astype(jnp.int32) + rank_ref[...]


def _positions(idx, rank, offsets):
    t = idx.shape[1]
    tm = 1024
    rows = pl.BlockSpec((TOP_K, tm), lambda i: (0, i))
    return pl.pallas_call(
        _pos_kernel, grid=(t // tm,),
        in_specs=[rows, rows, _resident((N_EXPERTS, 1))],
        out_specs=rows, out_shape=jax.ShapeDtypeStruct((TOP_K, t), jnp.int32),
        compiler_params=_params(("parallel",)), name="expert_positions",
    )(idx, rank, offsets.astype(F32).reshape(N_EXPERTS, 1))


def _moe_stream(x, row0, t, h_bf16, top_idx, gate, rank, counts, mods4, k_gate, exp_w13, exp_w2,
                sh_w13_bf16, sh_w2_bf16):
    padded = (counts + MOE_BLOCK - 1) // MOE_BLOCK * MOE_BLOCK
    e_iota = jnp.arange(N_EXPERTS)
    pend = jnp.sum(jnp.where(e_iota[:, None] <= e_iota[None, :], padded[:, None], 0), axis=0)
    pos = _positions(top_idx, rank, pend - padded)
    n_rows = -(-(t * TOP_K + N_EXPERTS * (MOE_BLOCK - 1)) // MOE_BLOCK) * MOE_BLOCK
    n_blocks = n_rows // MOE_BLOCK
    blk_start = jnp.arange(n_blocks, dtype=pend.dtype) * MOE_BLOCK
    blk_e = jnp.minimum(jnp.sum((pend[None, :] <= blk_start[:, None]).astype(jnp.int32), axis=1), N_EXPERTS - 1)
    n_real = (pend[N_EXPERTS - 1:] // MOE_BLOCK).astype(jnp.int32)
    tok = jnp.broadcast_to(jnp.arange(t, dtype=jnp.int32)[None, :], (TOP_K, t))
    _, sorted_tok = lax.sort_key_val(pos.reshape(-1), tok.reshape(-1))
    dense_end = jnp.sum(jnp.where(e_iota[:, None] <= e_iota[None, :], counts[:, None], 0), axis=0)
    shift_blk = ((pend - padded) - (dense_end - counts))[blk_e]
    end_blk = dense_end[blk_e]
    row = jnp.arange(n_rows, dtype=jnp.int32)
    dense = row - jnp.repeat(shift_blk, MOE_BLOCK)
    row_tok = jnp.where(dense < jnp.repeat(end_blk, MOE_BLOCK),
                        sorted_tok[jnp.clip(dense, 0, t * TOP_K - 1)], row % t)
    xs = h_bf16[row_tok + row0]
    ys = _grouped_experts(blk_e, n_real, padded, xs, exp_w13, exp_w2)
    y_exp = ys[pos.reshape(-1)].reshape(TOP_K, t, D_MODEL)
    return _shared_and_combine(h_bf16, x, row0, t, y_exp, gate.T, mods4, k_gate, sh_w13_bf16, sh_w2_bf16)


def _moe_streams(x, norm_g, mods4, k_shift, k_scale, k_gate, router_w, router_bias, exp_w13, exp_w2, sh_w13, sh_w2):
    h_bf16, top_idx, gate, rank, counts = _router(x, norm_g, mods4, k_shift, k_scale, router_w.T.astype(BF16),
                                                  router_bias)
    sh13, sh2 = sh_w13.astype(BF16), sh_w2.astype(BF16)
    outs = []
    for s, (row0, t) in enumerate(((0, T_CTX), (T_CTX, T_LAT))):
        cols = slice(row0, row0 + t)
        outs.append(_moe_stream(x, row0, t, h_bf16, top_idx[:, cols], gate[:, cols], rank[:, cols], counts[s, :, 0],
                                mods4, k_gate, exp_w13, exp_w2, sh13, sh2))
    return tuple(outs)


HY_BLOCK_CTX, HY_TC_CTX = 256, 512
HY_BLOCK_LAT, HY_TC_LAT = 512, 128


def _hyena_filters(L, w1, b1, w2, b2, w3, sin_freq):
    t_norm = jnp.linspace(0.0, 1.0, L, dtype=F32)[:, None]
    omega = 2.0 * math.pi * jnp.arange(L, dtype=F32)[:, None] / L
    bands = jnp.linspace(1e-4, FILTER_BANDS - 1, FILTER_BANDS, dtype=F32)[None, :]
    z = jnp.concatenate([t_norm, jnp.cos(bands * omega), -jnp.sin(bands * omega)], axis=-1)
    h = jnp.sin(sin_freq * (z @ w1 + b1))
    h = jnp.sin(sin_freq * (h @ w2 + b2))
    h = (h @ w3).reshape(L, 2, HYENA_ORDER, HYENA_WIDTH)
    deltas = jnp.abs(jnp.linspace(math.log(DECAY_TARGET) / DECAY_PCT_LONG,
                                  math.log(DECAY_TARGET) / DECAY_PCT_SHORT, HYENA_WIDTH, dtype=F32))
    h = h * jnp.exp(-t_norm * deltas)[:, None, None, :]
    fwd, bwd = h[:, 0], h[:, 1] * (t_norm > 0.0)[:, :, None]
    norm = jnp.sum(jnp.abs(fwd), axis=0, keepdims=True) + jnp.sum(jnp.abs(bwd), axis=0, keepdims=True)
    return fwd / norm, bwd / norm


HY_ACC_ELEMS = 8192


def _dft_matrices(bk):
    k = jnp.arange(bk, dtype=jnp.int32)
    phase = ((2 * k[:, None] + 1) * k[None, :]) % (4 * bk)
    ang = phase.astype(F32) * (math.pi / (2 * bk))
    return jnp.cos(ang), jnp.sin(ang)


def _spectra_kernel(xf_ref, xb_ref, yf_ref, yb_ref, c1_ref, s1_ref, c2_ref, s2_ref, o_ref, *, nb):
    d = pl.program_id(1) - (nb - 1)
    x = jnp.where(d >= 0, xf_ref[0], xb_ref[0]).astype(BF16)
    y = jnp.where(d >= 1, yf_ref[0], yb_ref[0]).astype(BF16)
    cy = jnp.where(d == 0, c1_ref[...], c2_ref[...])
    sy = jnp.where(d == 0, s1_ref[...], s2_ref[...])
    re = jnp.dot(c1_ref[...], x, preferred_element_type=F32) + jnp.dot(cy, y, preferred_element_type=F32)
    im = jnp.dot(sy, y, preferred_element_type=F32) - jnp.dot(s1_ref[...], x, preferred_element_type=F32)
    o_ref[0, 0, 0] = re
    o_ref[0, 0, 1] = jnp.where(d >= 0, im, -im)


def _filter_spectra(fwd, bwd, bk, tc):
    L = fwd.shape[0]
    nb = L // bk
    n_ct = HYENA_WIDTH // tc
    cos, sin = _dft_matrices(bk)
    k = jnp.arange(bk, dtype=jnp.int32)
    phase = ((2 * k[:, None] + 1) * (bk - k[None, :])) % (4 * bk)
    ang = phase.astype(F32) * (math.pi / (2 * bk))
    live = (k[None, :] > 0).astype(F32)
    mats = [m.astype(BF16) for m in (cos, sin, jnp.cos(ang) * live, jnp.sin(ang) * live)]

    def seg(pick):
        return pl.BlockSpec((1, bk, tc), lambda o, di, ct: (pick(di - (nb - 1)), 0, o * n_ct + ct))

    mat = pl.BlockSpec((bk, bk), lambda o, di, ct: (0, 0), pipeline_mode=pl.Buffered(1))
    halves = [h.reshape(nb, bk, HYENA_ORDER * HYENA_WIDTH) for h in (fwd, bwd)]
    return pl.pallas_call(
        functools.partial(_spectra_kernel, nb=nb),
        grid=(HYENA_ORDER, 2 * nb - 1, n_ct),
        in_specs=[seg(lambda d: jnp.maximum(d, 0)), seg(lambda d: jnp.maximum(-d, 0)),
                  seg(lambda d: jnp.maximum(d - 1, 0)), seg(lambda d: jnp.maximum(-d - 1, 0)),
                  mat, mat, mat, mat],
        out_specs=pl.BlockSpec((1, 1, 2, bk, tc), lambda o, di, ct: (o, di, 0, 0, ct)),
        out_shape=jax.ShapeDtypeStruct((HYENA_ORDER, 2 * nb - 1, 2, bk, HYENA_WIDTH), F32),
        compiler_params=_params(("parallel", "parallel", "parallel")),
        name="filter_spectra",
    )(halves[0], halves[1], halves[0], halves[1], *mats)


def _hyena_kernel(v_ref, x1_ref, x2_ref, cw_ref, cb_ref, hb_ref, k0_ref, k1_ref, c_ref, s_ref, ct_ref, st_ref,
                  o_ref, vs_ref, z1_ref, x2s_ref, ur_ref, ui_ref, yr_ref, yi_ref, *, seq, bk, tc):
    nb = seq // bk
    ft = HY_ACC_ELEMS // tc
    row = lax.broadcasted_iota(jnp.int32, (seq, tc), 0)

    def short_conv(x, j):
        prev = jnp.where(row == 0, 0.0, pltpu.roll(x, 1, 0))
        nxt = jnp.where(row == seq - 1, 0.0, pltpu.roll(x, seq - 1, 0))
        return (prev * cw_ref[0, j:j + 1, :] + x * cw_ref[1, j:j + 1, :] + nxt * cw_ref[2, j:j + 1, :]
                + cb_ref[j:j + 1, :])

    vs_ref[...] = short_conv(v_ref[...], 0)
    z1_ref[...] = short_conv(x1_ref[...], 1)
    x2s_ref[...] = short_conv(x2_ref[...], 2)

    def long_conv(src_ref, k_ref, emit):
        for blk in range(nb):
            ub = src_ref[blk * bk:(blk + 1) * bk, :].astype(BF16)
            ur_ref[blk] = jnp.dot(c_ref[...], ub, preferred_element_type=F32)
            ui_ref[blk] = -jnp.dot(s_ref[...], ub, preferred_element_type=F32)
        for out_blk in range(nb):
            def acc_tile(f, carry, out_blk=out_blk):
                rows = pl.ds(pl.multiple_of(f * ft, ft), ft)
                ar = jnp.zeros((ft, tc), F32)
                ai = jnp.zeros((ft, tc), F32)
                for in_blk in range(nb):
                    d = out_blk - in_blk + nb - 1
                    kr, ki = k_ref[d, 0, rows, :], k_ref[d, 1, rows, :]
                    xr, xi = ur_ref[in_blk, rows, :], ui_ref[in_blk, rows, :]
                    ar = ar + (kr * xr - ki * xi)
                    ai = ai + (kr * xi + ki * xr)
                yr_ref[rows, :] = ar
                yi_ref[rows, :] = ai
                return carry

            lax.fori_loop(0, bk // ft, acc_tile, 0)
            y = jnp.dot(ct_ref[...], yr_ref[...].astype(BF16), preferred_element_type=F32)
            y = y - jnp.dot(st_ref[...], yi_ref[...].astype(BF16), preferred_element_type=F32)
            emit(slice(out_blk * bk, (out_blk + 1) * bk), y * (1.0 / bk))

    def emit_z1(rows, y):
        z1_ref[rows, :] = z1_ref[rows, :] * (y + hb_ref[0:1, :] * vs_ref[rows, :])

    def emit_out(rows, y):
        o_ref[rows, :] = (x2s_ref[rows, :] * (y + hb_ref[1:2, :] * z1_ref[rows, :])).astype(BF16)

    long_conv(vs_ref, k0_ref, emit_z1)
    long_conv(z1_ref, k1_ref, emit_out)


def _hyena(proj, row0, n_batch, seq, bk, tc, conv_w, conv_b, hy_bias, spectra, mats):
    nb = seq // bk
    n_ct = HYENA_WIDTH // tc
    col0 = QKV_WIDTH // tc
    per = HYENA_WIDTH // tc
    b0 = row0 // seq

    def data(which):
        return pl.BlockSpec((seq, tc), lambda ct, b: (b0 + b, col0 + which * per + ct))

    const2 = lambda shape: pl.BlockSpec(shape, lambda ct, b: (0, 0), pipeline_mode=pl.Buffered(1))
    kspec = pl.BlockSpec((2 * nb - 1, 2, bk, tc), lambda ct, b: (0, 0, 0, ct), pipeline_mode=pl.Buffered(1))
    cw = conv_w.reshape(3, HYENA_ORDER + 1, HYENA_WIDTH)
    cb = conv_b.reshape(HYENA_ORDER + 1, HYENA_WIDTH)
    return pl.pallas_call(
        functools.partial(_hyena_kernel, seq=seq, bk=bk, tc=tc),
        grid=(n_ct, n_batch),
        in_specs=[data(0), data(1), data(2),
                  pl.BlockSpec((3, HYENA_ORDER + 1, tc), lambda ct, b: (0, 0, ct)),
                  pl.BlockSpec((HYENA_ORDER + 1, tc), lambda ct, b: (0, ct)),
                  pl.BlockSpec((HYENA_ORDER, tc), lambda ct, b: (0, ct)),
                  kspec, kspec, const2((bk, bk)), const2((bk, bk)), const2((bk, bk)), const2((bk, bk))],
        out_specs=pl.BlockSpec((seq, tc), lambda ct, b: (b, ct)),
        out_shape=jax.ShapeDtypeStruct((n_batch * seq, HYENA_WIDTH), BF16),
        scratch_shapes=[pltpu.VMEM((seq, tc), F32), pltpu.VMEM((seq, tc), F32), pltpu.VMEM((seq, tc), F32),
                        pltpu.VMEM((nb, bk, tc), F32), pltpu.VMEM((nb, bk, tc), F32),
                        pltpu.VMEM((bk, tc), F32), pltpu.VMEM((bk, tc), F32)],
        compiler_params=_params(("parallel", "parallel")),
        name="hyena_seq%d" % seq,
    )(proj, proj, proj, cw, cb, hy_bias, spectra[0], spectra[1], *mats)


HY_SPECTRA_TC = 512


def _hyena_operands(L, bk, f_w1, f_b1, f_w2, f_b2, f_w3, sin_freq):
    fwd, bwd = _hyena_filters(L, f_w1, f_b1, f_w2, f_b2, f_w3, sin_freq)
    cos, sin = _dft_matrices(bk)
    mats = tuple(m.astype(BF16) for m in (cos, sin, cos.T, sin.T))
    return _filter_spectra(fwd, bwd, bk, HY_SPECTRA_TC), mats


S5_LANES = S5_GROUPS * S5_STATE
S5_CLUSTER_GROUPS = 16
S5_CLUSTERS = S5_GROUPS // S5_CLUSTER_GROUPS
S5_CLUSTER_CH = S5_CLUSTER_GROUPS * S5_GROUP_CH
S5_CLUSTER_LANES = S5_CLUSTER_GROUPS * S5_STATE
SCAN_SEQS = 8
SCAN_ROWS = 512


def _s5_kernel(u_ref, s0r_ref, s0i_ref, bw_ref, cwr_ref, cwi_ref, lam_ref, y_ref, sfr_ref, sfi_ref,
               sre, sim, car, cai, *, reverse):
    n_seq, tc, _ = u_ref.shape

    @pl.when(pl.program_id(1) == 0)
    def _():
        car[...] = jnp.zeros_like(car)
        cai[...] = jnp.zeros_like(cai)
        car[0:n_seq, :] = s0r_ref[0]
        cai[0:n_seq, :] = s0i_ref[0]

    u = u_ref[...]
    if n_seq < SCAN_SEQS:
        u = jnp.concatenate([u, jnp.zeros((SCAN_SEQS - n_seq, tc, S5_WIDTH), F32)], axis=0)
    ub = jnp.swapaxes(u, 0, 1).reshape(tc * SCAN_SEQS, S5_WIDTH).astype(BF16)
    for k in range(S5_CLUSTERS):
        bu = jnp.dot(ub[:, k * S5_CLUSTER_CH:(k + 1) * S5_CLUSTER_CH], bw_ref[k], preferred_element_type=F32)
        sre[:, k * S5_CLUSTER_LANES:(k + 1) * S5_CLUSTER_LANES] = bu[:, :S5_CLUSTER_LANES]
        sim[:, k * S5_CLUSTER_LANES:(k + 1) * S5_CLUSTER_LANES] = bu[:, S5_CLUSTER_LANES:]

    for k in range(S5_CLUSTERS):
        lanes = slice(k * S5_CLUSTER_LANES, (k + 1) * S5_CLUSTER_LANES)
        lr = jnp.broadcast_to(lam_ref[0:1, lanes], (SCAN_SEQS, S5_CLUSTER_LANES))
        li = jnp.broadcast_to(lam_ref[1:2, lanes], (SCAN_SEQS, S5_CLUSTER_LANES))

        def step(i, state, lanes=lanes, lr=lr, li=li):
            sr, si = state
            t = (tc - 1 - i) if reverse else i
            rows = pl.ds(pl.multiple_of(t * SCAN_SEQS, SCAN_SEQS), SCAN_SEQS)
            nr = (lr * sr - li * si) + sre[rows, lanes]
            ni = (lr * si + li * sr) + sim[rows, lanes]
            sre[rows, lanes] = nr
            sim[rows, lanes] = ni
            return nr, ni

        sr, si = lax.fori_loop(0, tc, step, (car[:, lanes], cai[:, lanes]), unroll=2)
        car[:, lanes] = sr
        cai[:, lanes] = si

    parts = []
    for k in range(S5_CLUSTERS):
        lanes = slice(k * S5_CLUSTER_LANES, (k + 1) * S5_CLUSTER_LANES)
        yk = jnp.dot(sre[:, lanes].astype(BF16), cwr_ref[k], preferred_element_type=F32)
        parts.append(yk + jnp.dot(sim[:, lanes].astype(BF16), cwi_ref[k], preferred_element_type=F32))
    y = jnp.concatenate(parts, axis=1).reshape(tc, SCAN_SEQS, S5_WIDTH)
    y_ref[...] = jnp.swapaxes(y, 0, 1)[0:n_seq]
    sfr_ref[0] = car[0:n_seq, :]
    sfi_ref[0] = cai[0:n_seq, :]


def _s5_direction(u, n_seq, seq, group, s0_re, s0_im, bw, cwr, cwi, lam, reverse):
    tc = SCAN_ROWS // SCAN_SEQS
    nc = seq // tc
    ng = n_seq // group

    def chunk(c):
        return (nc - 1 - c) if reverse else c

    state_spec = pl.BlockSpec((1, group, S5_LANES), lambda g, c: (g, 0, 0))
    const = lambda a: pl.BlockSpec(a.shape, lambda g, c: (0,) * a.ndim, pipeline_mode=pl.Buffered(1))
    y, f_re, f_im = pl.pallas_call(
        functools.partial(_s5_kernel, reverse=reverse),
        grid=(ng, nc),
        in_specs=[pl.BlockSpec((group, tc, S5_WIDTH), lambda g, c: (g, chunk(c), 0)),
                  state_spec, state_spec, const(bw), const(cwr), const(cwi), const(lam)],
        out_specs=[pl.BlockSpec((group, tc, S5_WIDTH), lambda g, c: (g, chunk(c), 0)),
                   state_spec, state_spec],
        out_shape=[jax.ShapeDtypeStruct((n_seq, seq, S5_WIDTH), F32),
                   jax.ShapeDtypeStruct((ng, group, S5_LANES), F32),
                   jax.ShapeDtypeStruct((ng, group, S5_LANES), F32)],
        scratch_shapes=[pltpu.VMEM((SCAN_ROWS, S5_LANES), F32), pltpu.VMEM((SCAN_ROWS, S5_LANES), F32),
                        pltpu.VMEM((SCAN_SEQS, S5_LANES), F32), pltpu.VMEM((SCAN_SEQS, S5_LANES), F32)],
        compiler_params=_params(("parallel", "arbitrary")),
        name="s5_bwd" if reverse else "s5_fwd",
    )(u.reshape(n_seq, seq, S5_WIDTH), s0_re.reshape(ng, group, S5_LANES), s0_im.reshape(ng, group, S5_LANES),
      bw, cwr, cwi, lam)
    return y.reshape(n_seq * seq, S5_WIDTH), f_re.reshape(n_seq, S5_LANES), f_im.reshape(n_seq, S5_LANES)


def _s5_weights(a_re, a_im, log_dt, b_re, b_im, c_re, c_im):
    lam = lax.complex(a_re, a_im)
    lam_bar = jnp.exp(lam * jnp.exp(log_dt)[..., None])
    b_bar = ((lam_bar - 1.0) / lam)[..., None] * lax.complex(b_re, b_im)
    eye = jnp.eye(S5_CLUSTER_GROUPS, dtype=F32)
    out = []
    for d in range(2):
        def cluster_in(w):
            w = w.reshape(S5_CLUSTERS, S5_CLUSTER_GROUPS, S5_STATE, S5_GROUP_CH)
            return jnp.einsum('ab,kapc->kacbp', eye, w).reshape(S5_CLUSTERS, S5_CLUSTER_CH, S5_CLUSTER_LANES)

        def cluster_out(w):
            w = w.reshape(S5_CLUSTERS, S5_CLUSTER_GROUPS, S5_GROUP_CH, S5_STATE)
            return jnp.einsum('ab,kacp->kapbc', eye, w).reshape(S5_CLUSTERS, S5_CLUSTER_LANES, S5_CLUSTER_CH)

        bw = jnp.concatenate([cluster_in(jnp.real(b_bar[d])), cluster_in(jnp.imag(b_bar[d]))], axis=-1)
        cwr = cluster_out(c_re[d])
        cwi = cluster_out(-c_im[d])
        lam_d = lam_bar[d].reshape(S5_LANES)
        lam2 = jnp.stack([jnp.real(lam_d), jnp.imag(lam_d)]).astype(F32)
        out.append((bw.astype(BF16), cwr.astype(BF16), cwi.astype(BF16), lam2))
    return out


def _final_norm_kernel(x_ref, g_ref, o_ref):
    x = x_ref[...]
    ms = jnp.mean(x * x, axis=-1, keepdims=True)
    o_ref[...] = x * lax.rsqrt(ms + RMS_EPS) * g_ref[...]


def _final_norm(x, g, row0, t):
    tm = ROW_TILE
    b0 = row0 // tm
    return pl.pallas_call(
        _final_norm_kernel,
        grid=(t // tm,),
        in_specs=[pl.BlockSpec((tm, D_MODEL), lambda i: (b0 + i, 0)), _resident((1, D_MODEL))],
        out_specs=pl.BlockSpec((tm, D_MODEL), lambda i: (i, 0)),
        out_shape=jax.ShapeDtypeStruct((t, D_MODEL), F32),
        compiler_params=_params(("parallel",)),
        name="final_norm",
    )(x, g.reshape(1, D_MODEL))


def _rope_tables(L):
    n_rows = L // GRID_W
    row_idx = jnp.repeat(jnp.arange(n_rows, dtype=F32), GRID_W)
    col_idx = (jnp.arange(L) % GRID_W).astype(F32)
    inv = ROPE_THETA ** (-jnp.arange(0, ROPE_AXIS_DIM, 2, dtype=F32) / ROPE_AXIS_DIM)
    ang = jnp.concatenate([row_idx[:, None] * inv, col_idx[:, None] * inv], axis=-1)
    cos, sin = jnp.cos(ang), jnp.sin(ang)
    cos_full = jnp.repeat(cos, 2, axis=-1)
    sin_signed = jnp.stack([-sin, sin], axis=-1).reshape(L, HEAD_DIM)
    return cos_full, sin_signed


def kernel(x_prompt, x_sample, cache_k_l0, cache_v_l0, state_s5_re_l1, state_s5_im_l1, c, c_ctx,
           ada_w_l0, ada_b_l0, norm1_l0, norm2_l0,
           w_in_l0, w_out_l0, q_norm_l0, k_norm_l0, hy_conv_w_l0, hy_conv_b_l0,
           hy_ffn_w1_l0, hy_ffn_b1_l0, hy_ffn_w2_l0, hy_ffn_b2_l0, hy_ffn_w3_l0, hy_sin_freq_l0, hy_bias_l0,
           router_l0, router_bias_l0, exp_w13_l0, exp_w2_l0, shared_w13_l0, shared_w2_l0,
           ada_w_l1, ada_b_l1, norm1_l1, norm2_l1,
           w_in_l1, s5_a_re_l1, s5_a_im_l1, s5_log_dt_l1, s5_b_re_l1, s5_b_im_l1,
           s5_c_re_l1, s5_c_im_l1, s5_d_l1, w_glu_l1,
           router_l1, router_bias_l1, exp_w13_l1, exp_w2_l1, shared_w13_l1, shared_w2_l1,
           final_norm):
    x = (x_prompt.reshape(T_CTX, D_MODEL), x_sample.reshape(T_LAT, D_MODEL))
    cond = jnp.concatenate([c_ctx[None, :], c, jnp.zeros((N_COND - 1 - DEC_BATCH, D_MODEL), F32)], axis=0)

    mods = _ada_mods(cond, ada_w_l0, ada_b_l0)
    proj = _norm_mod_matmul(x, norm1_l0, mods, 0, 1, w_in_l0.astype(BF16))

    q_c, k_c, v_c, new_k = _qk_prep(proj, 0, T_CTX, q_norm_l0, k_norm_l0, None)
    attn_c = _attention(q_c, k_c, v_c, BATCH, SEQ, None)
    q_l, k_l, v_l = _qk_prep(proj, T_CTX, T_LAT, q_norm_l0, k_norm_l0, _rope_tables(DEC_SEQ))
    ctx_kv = (cache_k_l0.reshape(DEC_BATCH * PAST_LEN, KV_WIDTH).astype(BF16),
              cache_v_l0.reshape(DEC_BATCH * PAST_LEN, KV_WIDTH).astype(BF16))
    attn_l = _attention(q_l, k_l, v_l, DEC_BATCH, DEC_SEQ, ctx_kv)

    hy_f = (hy_ffn_w1_l0, hy_ffn_b1_l0, hy_ffn_w2_l0, hy_ffn_b2_l0, hy_ffn_w3_l0, hy_sin_freq_l0)
    spec_c, mats_c = _hyena_operands(SEQ, HY_BLOCK_CTX, *hy_f)
    z_c = _hyena(proj, 0, BATCH, SEQ, HY_BLOCK_CTX, HY_TC_CTX, hy_conv_w_l0, hy_conv_b_l0, hy_bias_l0,
                 spec_c, mats_c)
    spec_l, mats_l = _hyena_operands(DEC_SEQ, HY_BLOCK_LAT, *hy_f)
    z_l = _hyena(proj, T_CTX, DEC_BATCH, DEC_SEQ, HY_BLOCK_LAT, HY_TC_LAT, hy_conv_w_l0, hy_conv_b_l0, hy_bias_l0,
                 spec_l, mats_l)

    w_out = w_out_l0.astype(BF16)
    x = _out_proj((attn_c, attn_l), (z_c, z_l), x, mods, 2, w_out[:Q_WIDTH], w_out[Q_WIDTH:])
    x = _moe_streams(x, norm2_l0, mods, 3, 4, 5, router_l0, router_bias_l0, exp_w13_l0, exp_w2_l0,
                     shared_w13_l0, shared_w2_l0)

    new_v = proj[:T_CTX, Q_WIDTH + KV_WIDTH:QKV_WIDTH]

    mods = _ada_mods(cond, ada_w_l1, ada_b_l1)
    u = _norm_mod_matmul(x, norm1_l1, mods, 0, 1, w_in_l1.astype(BF16))
    s5_w = _s5_weights(s5_a_re_l1, s5_a_im_l1, s5_log_dt_l1, s5_b_re_l1, s5_b_im_l1, s5_c_re_l1, s5_c_im_l1)
    zero_state = jnp.zeros((BATCH, S5_LANES), F32)
    u_c, u_l = u[:T_CTX], u[T_CTX:]
    ys, finals = [], []
    for d in range(2):
        y_c, f_re, f_im = _s5_direction(u_c, BATCH, SEQ, SCAN_SEQS, zero_state, zero_state, *s5_w[d],
                                        reverse=d == 1)
        y_l, _, _ = _s5_direction(u_l, DEC_BATCH, DEC_SEQ, DEC_BATCH,
                                  state_s5_re_l1[:, d].reshape(DEC_BATCH, S5_LANES),
                                  state_s5_im_l1[:, d].reshape(DEC_BATCH, S5_LANES), *s5_w[d], reverse=d == 1)
        ys.append((y_c, y_l))
        finals.append((f_re.reshape(BATCH, S5_GROUPS, S5_STATE), f_im.reshape(BATCH, S5_GROUPS, S5_STATE)))
    s_re = jnp.stack([finals[0][0], finals[1][0]], axis=1)
    s_im = jnp.stack([finals[0][1], finals[1][1]], axis=1)
    x = _glu_proj(ys[0], ys[1], u, s5_d_l1, x, mods, 2, w_glu_l1.astype(BF16))
    x_c, x_l = _moe_streams(x, norm2_l1, mods, 3, 4, 5, router_l1, router_bias_l1, exp_w13_l1, exp_w2_l1,
                            shared_w13_l1, shared_w2_l1)

    return (_final_norm(x_c, final_norm, 0, T_CTX).reshape(BATCH, SEQ, D_MODEL),
            _final_norm(x_l, final_norm, 0, T_LAT).reshape(DEC_BATCH, DEC_SEQ, D_MODEL),
            new_k.reshape(BATCH, SEQ, N_KV_HEADS, HEAD_DIM),
            new_v.reshape(BATCH, SEQ, N_KV_HEADS, HEAD_DIM),
            s_re, s_im)
```

```python
import functools
import math

import jax
import jax.numpy as jnp
from jax import lax
from jax.experimental import pallas as pl
from jax.experimental.pallas import tpu as pltpu

F32 = jnp.float32
BF16 = jnp.bfloat16

D_MODEL = 2048
BATCH = 32
SEQ = 256
DEC_BATCH = 4
DEC_SEQ = 4096
PAST_LEN = 256
GRID_W = 64
RMS_EPS = 1e-6
HEAD_DIM = 128
N_HEADS = 8
N_KV_HEADS = 2
GQA_GROUP = N_HEADS // N_KV_HEADS
Q_WIDTH = N_HEADS * HEAD_DIM
KV_WIDTH = N_KV_HEADS * HEAD_DIM
QKV_WIDTH = Q_WIDTH + 2 * KV_WIDTH
ROPE_THETA = 10000.0
ROPE_AXIS_DIM = HEAD_DIM // 2
HYENA_WIDTH = D_MODEL // 2
HYENA_ORDER = 2
FILTER_BANDS = 16
DECAY_TARGET = 1e-2
DECAY_PCT_SHORT = 0.3
DECAY_PCT_LONG = 1.5
S5_WIDTH = D_MODEL // 2
S5_GROUP_CH = 16
S5_GROUPS = S5_WIDTH // S5_GROUP_CH
S5_STATE = 64
N_EXPERTS = 64
TOP_K = 8
N_EXPERT_GROUPS = 8
TOPK_GROUPS = 4
EXPERT_HIDDEN = 512
ROUTED_SCALE = 2.5
MOE_BLOCK = 256

T_CTX = BATCH * SEQ
T_LAT = DEC_BATCH * DEC_SEQ
T_ALL = T_CTX + T_LAT
N_COND = 8

ROW_TILE = 256
V7X_VMEM_LIMIT_BYTES = 56 * 1024 * 1024


def _params(semantics):
    return pltpu.CompilerParams(dimension_semantics=semantics, vmem_limit_bytes=V7X_VMEM_LIMIT_BYTES)


def _cond_row(i, tm, row0=0):
    tok = row0 + i * tm
    return jnp.where(tok < T_CTX, 0, 1 + (tok - T_CTX) // DEC_SEQ)


def _mod_spec(k, tm, row0=0):
    return pl.BlockSpec((1, 1, 1, D_MODEL), lambda i: (_cond_row(i, tm, row0), k, 0, 0))


def _resident(shape):
    nd = len(shape)
    return pl.BlockSpec(shape, lambda i: (0,) * nd, pipeline_mode=pl.Buffered(1))


def _silu(x):
    return x * jax.nn.sigmoid(x)


def _norm_mod(x, g, sc, sh):
    ms = jnp.mean(x * x, axis=-1, keepdims=True)
    return (x * lax.rsqrt(ms + RMS_EPS) * g) * (1.0 + sc) + sh


def _ada_kernel(c_ref, w_ref, b_ref, o_ref):
    a = _silu(c_ref[...]).astype(BF16)
    o_ref[...] = jnp.dot(a, w_ref[...].astype(BF16), preferred_element_type=F32) + b_ref[...]


def _ada_mods(cond, w, b):
    n = w.shape[1]
    tn = 1024
    out = pl.pallas_call(
        _ada_kernel,
        grid=(n // tn,),
        in_specs=[pl.BlockSpec((N_COND, D_MODEL), lambda j: (0, 0)),
                  pl.BlockSpec((D_MODEL, tn), lambda j: (0, j)),
                  pl.BlockSpec((1, tn), lambda j: (0, j))],
        out_specs=pl.BlockSpec((N_COND, tn), lambda j: (0, j)),
        out_shape=jax.ShapeDtypeStruct((N_COND, n), F32),
        compiler_params=_params(("parallel",)),
        name="ada_mods",
    )(cond, w, b.reshape(1, n))
    return out.reshape(N_COND, 6, 1, D_MODEL)


def _split_row_specs(width, tm):
    n_ctx = T_CTX // tm
    return [pl.BlockSpec((tm, width), lambda i: (jnp.minimum(i, n_ctx - 1), 0)),
            pl.BlockSpec((tm, width), lambda i: (jnp.maximum(i - n_ctx, 0), 0))]


def _pick_rows(ctx_ref, lat_ref):
    return jnp.where(pl.program_id(0) < T_CTX // ctx_ref.shape[0], ctx_ref[...], lat_ref[...])


def _nmm_kernel(*refs, split):
    if split:
        xc_ref, xl_ref, g_ref, sc_ref, sh_ref, w_ref, o_ref = refs
        x = _pick_rows(xc_ref, xl_ref)
    else:
        x_ref, g_ref, sc_ref, sh_ref, w_ref, o_ref = refs
        x = x_ref[...]
    h = _norm_mod(x, g_ref[...], sc_ref[0, 0], sh_ref[0, 0])
    o_ref[...] = jnp.dot(h.astype(BF16), w_ref[...], preferred_element_type=F32)


def _norm_mod_matmul(x, g, mods4, k_shift, k_scale, w_bf16):
    n = w_bf16.shape[1]
    tm = ROW_TILE
    split = isinstance(x, tuple)
    x_args = list(x) if split else [x]
    x_specs = _split_row_specs(D_MODEL, tm) if split else [pl.BlockSpec((tm, D_MODEL), lambda i: (i, 0))]
    return pl.pallas_call(
        functools.partial(_nmm_kernel, split=split),
        grid=(T_ALL // tm,),
        in_specs=x_specs + [_resident((1, D_MODEL)), _mod_spec(k_scale, tm), _mod_spec(k_shift, tm),
                            _resident((D_MODEL, n))],
        out_specs=pl.BlockSpec((tm, n), lambda i: (i, 0)),
        out_shape=jax.ShapeDtypeStruct((T_ALL, n), F32),
        compiler_params=_params(("parallel",)),
        name="norm_mod_matmul",
    )(*x_args, g.reshape(1, D_MODEL), mods4, mods4, w_bf16)


def _head_norm(xh, g):
    ms = jnp.mean(xh * xh, axis=-1, keepdims=True)
    return xh * lax.rsqrt(ms + RMS_EPS) * g


def _rope(y, cos, sin_signed):
    lane = lax.broadcasted_iota(jnp.int32, y.shape, 1)
    partner = jnp.where(lane % 2 == 0, pltpu.roll(y, HEAD_DIM - 1, 1), pltpu.roll(y, 1, 1))
    return y * cos + partner * sin_signed


def _qkprep_ctx_kernel(qkv_ref, qg_ref, kg_ref, q_out, kb_out, vb_out, kf_out):
    for h in range(N_HEADS):
        sl = slice(h * HEAD_DIM, (h + 1) * HEAD_DIM)
        q_out[:, sl] = _head_norm(qkv_ref[:, sl], qg_ref[...]).astype(BF16)
    for j in range(N_KV_HEADS):
        k = _head_norm(qkv_ref[:, Q_WIDTH + j * HEAD_DIM:Q_WIDTH + (j + 1) * HEAD_DIM], kg_ref[...])
        kf_out[:, j * HEAD_DIM:(j + 1) * HEAD_DIM] = k
        kb_out[:, j * HEAD_DIM:(j + 1) * HEAD_DIM] = k.astype(BF16)
    vb_out[...] = qkv_ref[:, Q_WIDTH + KV_WIDTH:QKV_WIDTH].astype(BF16)


def _qkprep_lat_kernel(qkv_ref, qg_ref, kg_ref, cos_ref, sin_ref, q_out, kb_out, vb_out):
    cos, sin = cos_ref[...], sin_ref[...]
    for h in range(N_HEADS):
        sl = slice(h * HEAD_DIM, (h + 1) * HEAD_DIM)
        q_out[:, sl] = _rope(_head_norm(qkv_ref[:, sl], qg_ref[...]), cos, sin).astype(BF16)
    for j in range(N_KV_HEADS):
        k = _head_norm(qkv_ref[:, Q_WIDTH + j * HEAD_DIM:Q_WIDTH + (j + 1) * HEAD_DIM], kg_ref[...])
        kb_out[:, j * HEAD_DIM:(j + 1) * HEAD_DIM] = _rope(k, cos, sin).astype(BF16)
    vb_out[...] = qkv_ref[:, Q_WIDTH + KV_WIDTH:QKV_WIDTH].astype(BF16)


def _qk_prep(proj, row0, t, q_norm, k_norm, rope_tabs):
    tm = ROW_TILE
    b0 = row0 // tm
    qkv_spec = pl.BlockSpec((tm, QKV_WIDTH), lambda i: (b0 + i, 0))
    gain = _resident((1, HEAD_DIM))
    outs = [pl.BlockSpec((tm, Q_WIDTH), lambda i: (i, 0)),
            pl.BlockSpec((tm, KV_WIDTH), lambda i: (i, 0)),
            pl.BlockSpec((tm, KV_WIDTH), lambda i: (i, 0))]
    shapes = [jax.ShapeDtypeStruct((t, Q_WIDTH), BF16),
              jax.ShapeDtypeStruct((t, KV_WIDTH), BF16),
              jax.ShapeDtypeStruct((t, KV_WIDTH), BF16)]
    qg, kg = q_norm.reshape(1, HEAD_DIM), k_norm.reshape(1, HEAD_DIM)
    if rope_tabs is None:
        return pl.pallas_call(
            _qkprep_ctx_kernel, grid=(t // tm,),
            in_specs=[qkv_spec, gain, gain],
            out_specs=outs + [pl.BlockSpec((tm, KV_WIDTH), lambda i: (i, 0))],
            out_shape=shapes + [jax.ShapeDtypeStruct((t, KV_WIDTH), F32)],
            compiler_params=_params(("parallel",)), name="qk_prep_ctx",
        )(proj, qg, kg)
    cos, sin = rope_tabs
    nq = DEC_SEQ // tm
    tab = pl.BlockSpec((tm, HEAD_DIM), lambda i: (i % nq, 0))
    return pl.pallas_call(
        _qkprep_lat_kernel, grid=(t // tm,),
        in_specs=[qkv_spec, gain, gain, tab, tab],
        out_specs=outs, out_shape=shapes,
        compiler_params=_params(("parallel",)), name="qk_prep_lat",
    )(proj, qg, kg, cos, sin)


def _attn_kernel(*refs, with_ctx):
    if with_ctx:
        q_ref, k_ref, v_ref, ck_ref, cv_ref, o_ref = refs
    else:
        q_ref, k_ref, v_ref, o_ref = refs
    c = HEAD_DIM ** -0.5 * math.log2(math.e)
    nt = (((1,), (1,)), ((), ()))
    for hh in range(ATTN_HEADS_PER_STEP):
        cols = slice(hh * HEAD_DIM, (hh + 1) * HEAD_DIM)
        q = q_ref[:, cols]
        s = lax.dot_general(q, k_ref[...], nt, preferred_element_type=F32)
        m = jnp.max(s, axis=-1, keepdims=True)
        if with_ctx:
            sc = lax.dot_general(q, ck_ref[...], nt, preferred_element_type=F32)
            m = jnp.maximum(m, jnp.max(sc, axis=-1, keepdims=True))
        p = jnp.exp2((s - m) * c)
        l = jnp.sum(p, axis=-1, keepdims=True)
        o = jnp.dot(p.astype(BF16), v_ref[...], preferred_element_type=F32)
        if with_ctx:
            pc = jnp.exp2((sc - m) * c)
            l = l + jnp.sum(pc, axis=-1, keepdims=True)
            o = o + jnp.dot(pc.astype(BF16), cv_ref[...], preferred_element_type=F32)
        o_ref[:, cols] = (o / l).astype(BF16)


ATTN_HEADS_PER_STEP = 1
ATTN_Q_TILE = 256


def _attention(q, k, v, n_batch, seq, ctx_kv):
    tq = ATTN_Q_TILE
    nq = seq // tq
    t = n_batch * seq
    hw = ATTN_HEADS_PER_STEP * HEAD_DIM
    kv_of = GQA_GROUP // ATTN_HEADS_PER_STEP
    q_spec = pl.BlockSpec((tq, hw), lambda b, h, i: (b * nq + i, h))
    kv_spec = pl.BlockSpec((seq, HEAD_DIM), lambda b, h, i: (b, h // kv_of))
    in_specs = [q_spec, kv_spec, kv_spec]
    args = [q, k, v]
    if ctx_kv is not None:
        c_spec = pl.BlockSpec((PAST_LEN, HEAD_DIM), lambda b, h, i: (b, h // kv_of))
        in_specs += [c_spec, c_spec]
        args += list(ctx_kv)
    return pl.pallas_call(
        functools.partial(_attn_kernel, with_ctx=ctx_kv is not None),
        grid=(n_batch, N_HEADS // ATTN_HEADS_PER_STEP, nq),
        in_specs=in_specs,
        out_specs=pl.BlockSpec((tq, hw), lambda b, h, i: (b * nq + i, h)),
        out_shape=jax.ShapeDtypeStruct((t, Q_WIDTH), BF16),
        compiler_params=_params(("parallel", "parallel", "parallel")),
        name="attention_lat" if ctx_kv is not None else "attention_ctx",
    )(*args)


def _outproj_kernel(ac_ref, al_ref, zc_ref, zl_ref, xc_ref, xl_ref, gate_ref, wa_ref, wz_ref, o_ref):
    acc = jnp.dot(_pick_rows(ac_ref, al_ref), wa_ref[...], preferred_element_type=F32)
    acc = acc + jnp.dot(_pick_rows(zc_ref, zl_ref), wz_ref[...], preferred_element_type=F32)
    o_ref[...] = _pick_rows(xc_ref, xl_ref) + gate_ref[0, 0] * acc


def _out_proj(attn, z, x, mods4, k_gate, wa_bf16, wz_bf16):
    tm = ROW_TILE
    return pl.pallas_call(
        _outproj_kernel,
        grid=(T_ALL // tm,),
        in_specs=(_split_row_specs(Q_WIDTH, tm) + _split_row_specs(HYENA_WIDTH, tm) + _split_row_specs(D_MODEL, tm)
                  + [_mod_spec(k_gate, tm), _resident((Q_WIDTH, D_MODEL)), _resident((HYENA_WIDTH, D_MODEL))]),
        out_specs=pl.BlockSpec((tm, D_MODEL), lambda i: (i, 0)),
        out_shape=jax.ShapeDtypeStruct((T_ALL, D_MODEL), F32),
        compiler_params=_params(("parallel",)),
        name="out_proj",
    )(*attn, *z, *x, mods4, wa_bf16, wz_bf16)


def _gelu_tanh(x):
    return 0.5 * x * (1.0 + jnp.tanh(math.sqrt(2.0 / math.pi) * (x + 0.044715 * (x * x * x))))


def _glu_kernel(yfc_ref, yfl_ref, ybc_ref, ybl_ref, u_ref, d_ref, xc_ref, xl_ref, gate_ref, w_ref, o_ref):
    y = (_pick_rows(yfc_ref, yfl_ref) + _pick_rows(ybc_ref, ybl_ref)) + d_ref[...] * u_ref[...]
    ag = jnp.dot(_gelu_tanh(y).astype(BF16), w_ref[...], preferred_element_type=F32)
    a, g = ag[:, :D_MODEL], ag[:, D_MODEL:]
    o_ref[...] = _pick_rows(xc_ref, xl_ref) + gate_ref[0, 0] * (a * jax.nn.sigmoid(g))


def _glu_proj(y_fwd, y_bwd, u, d_skip, x, mods4, k_gate, w_bf16):
    tm = ROW_TILE
    return pl.pallas_call(
        _glu_kernel,
        grid=(T_ALL // tm,),
        in_specs=(_split_row_specs(S5_WIDTH, tm) + _split_row_specs(S5_WIDTH, tm)
                  + [pl.BlockSpec((tm, S5_WIDTH), lambda i: (i, 0)), _resident((1, S5_WIDTH))]
                  + _split_row_specs(D_MODEL, tm)
                  + [_mod_spec(k_gate, tm), _resident((S5_WIDTH, 2 * D_MODEL))]),
        out_specs=pl.BlockSpec((tm, D_MODEL), lambda i: (i, 0)),
        out_shape=jax.ShapeDtypeStruct((T_ALL, D_MODEL), F32),
        compiler_params=_params(("parallel",)),
        name="glu_proj",
    )(*y_fwd, *y_bwd, u, d_skip.reshape(1, S5_WIDTH), *x, mods4, w_bf16)


def _gmm_kernel(be_ref, nreal_ref, slot_ref, next_ref, xs_ref, w13_hbm, w2_hbm, o_ref,
                w13_f, w2_f, w13_b, w2_b, sems):
    i = pl.program_id(0)
    real = i < nreal_ref[0]
    fresh = jnp.logical_and(real, jnp.logical_or(i == 0, be_ref[i] != be_ref[jnp.maximum(i - 1, 0)]))

    def weight_copies(expert, slot):
        return (pltpu.make_async_copy(w13_hbm.at[expert], w13_f.at[slot], sems.at[slot, 0]),
                pltpu.make_async_copy(w2_hbm.at[expert], w2_f.at[slot], sems.at[slot, 1]))

    @pl.when(fresh)
    def _():
        expert, slot, nxt = be_ref[i], slot_ref[i], next_ref[i]

        @pl.when(i == 0)
        def _():
            for cp in weight_copies(expert, slot):
                cp.start()

        for cp in weight_copies(expert, slot):
            cp.wait()
        w13_b[...] = w13_f[slot].astype(BF16)
        w2_b[...] = w2_f[slot].astype(BF16)

        @pl.when(nxt >= 0)
        def _():
            for cp in weight_copies(nxt, 1 - slot):
                cp.start()

    @pl.when(real)
    def _():
        gu = jnp.dot(xs_ref[...], w13_b[...], preferred_element_type=F32)
        hmid = _silu(gu[:, :EXPERT_HIDDEN]) * gu[:, EXPERT_HIDDEN:]
        o_ref[...] = jnp.dot(hmid.astype(BF16), w2_b[...], preferred_element_type=F32).astype(BF16)

    @pl.when(jnp.logical_not(real))
    def _():
        o_ref[...] = jnp.zeros_like(o_ref)


def _grouped_experts(blk_e, n_real, padded, xs, w13, w2):
    n_rows = xs.shape[0]
    n_blocks = n_rows // MOE_BLOCK
    b_iota = jnp.arange(n_blocks)
    first = jnp.concatenate([jnp.ones((1,), jnp.int32), (blk_e[1:] != blk_e[:-1]).astype(jnp.int32)])
    run = jnp.sum(jnp.where(b_iota[:, None] <= b_iota[None, :], first[:, None], 0), axis=0) - 1
    slot = (run % 2).astype(jnp.int32)
    e_iota = jnp.arange(N_EXPERTS)
    later = jnp.logical_and(e_iota[None, :] > e_iota[:, None], padded[None, :] > 0)
    next_e = jnp.min(jnp.where(later, e_iota[None, :], N_EXPERTS), axis=1)
    next_e = jnp.where(next_e == N_EXPERTS, -1, next_e)
    next_blk = jnp.sum(jnp.where(blk_e[:, None] == e_iota[None, :], next_e[None, :], 0), axis=1).astype(jnp.int32)

    def live(i, nreal):
        return jnp.minimum(i, nreal[0] - 1)

    grid_spec = pltpu.PrefetchScalarGridSpec(
        num_scalar_prefetch=4,
        grid=(n_blocks,),
        in_specs=[pl.BlockSpec((MOE_BLOCK, D_MODEL), lambda i, be, nr, sl, nx: (live(i, nr), 0)),
                  pl.BlockSpec(memory_space=pl.ANY), pl.BlockSpec(memory_space=pl.ANY)],
        out_specs=pl.BlockSpec((MOE_BLOCK, D_MODEL), lambda i, be, nr, sl, nx: (i, 0)),
        scratch_shapes=[pltpu.VMEM((2, D_MODEL, 2 * EXPERT_HIDDEN), F32),
                        pltpu.VMEM((2, EXPERT_HIDDEN, D_MODEL), F32),
                        pltpu.VMEM((D_MODEL, 2 * EXPERT_HIDDEN), BF16),
                        pltpu.VMEM((EXPERT_HIDDEN, D_MODEL), BF16),
                        pltpu.SemaphoreType.DMA((2, 2))],
    )
    return pl.pallas_call(
        _gmm_kernel,
        grid_spec=grid_spec,
        out_shape=jax.ShapeDtypeStruct((n_rows, D_MODEL), BF16),
        compiler_params=_params(("arbitrary",)),
        name="grouped_experts",
    )(blk_e, n_real, slot, next_blk, xs, w13, w2)


COMBINE_TILE = 256


def _shared_ffn_kernel(h_ref, w13_ref, w2_ref, o_ref):
    gu = jnp.dot(h_ref[...], w13_ref[...], preferred_element_type=F32)
    hmid = _silu(gu[:, :EXPERT_HIDDEN]) * gu[:, EXPERT_HIDDEN:]
    o_ref[...] = jnp.dot(hmid.astype(BF16), w2_ref[...], preferred_element_type=F32)


def _shared_ffn(h_bf16, w13_bf16, w2_bf16):
    tm = ROW_TILE
    return pl.pallas_call(
        _shared_ffn_kernel,
        grid=(T_ALL // tm,),
        in_specs=[pl.BlockSpec((tm, D_MODEL), lambda i: (i, 0)),
                  _resident((D_MODEL, 2 * EXPERT_HIDDEN)), _resident((EXPERT_HIDDEN, D_MODEL))],
        out_specs=pl.BlockSpec((tm, D_MODEL), lambda i: (i, 0)),
        out_shape=jax.ShapeDtypeStruct((T_ALL, D_MODEL), F32),
        compiler_params=_params(("parallel",)),
        name="shared_expert",
    )(h_bf16, w13_bf16, w2_bf16)


def _combine_kernel(s_ref, x_ref, ye_ref, w_ref, gate_ref, o_ref):
    acc = s_ref[...]
    wts = w_ref[...]
    for k in range(TOP_K):
        acc = acc + wts[:, k:k + 1] * ye_ref[k].astype(F32)
    o_ref[...] = x_ref[...] + gate_ref[0, 0] * acc


def _combine(shared, x, row0, t, y_exp, wts, mods4, k_gate):
    tm = COMBINE_TILE
    b0 = row0 // tm
    row = lambda i: (i, 0)
    off = lambda i: (b0 + i, 0)
    return pl.pallas_call(
        _combine_kernel,
        grid=(t // tm,),
        in_specs=[pl.BlockSpec((tm, D_MODEL), off), pl.BlockSpec((tm, D_MODEL), off),
                  pl.BlockSpec((TOP_K, tm, D_MODEL), lambda i: (0, i, 0)),
                  pl.BlockSpec((tm, TOP_K), row), _mod_spec(k_gate, tm, row0)],
        out_specs=pl.BlockSpec((tm, D_MODEL), row),
        out_shape=jax.ShapeDtypeStruct((t, D_MODEL), F32),
        compiler_params=_params(("parallel",)),
        name="expert_combine",
    )(shared, x, y_exp, wts, mods4)


ROUTE_TILE = 256
GROUP_SIZE = N_EXPERTS // N_EXPERT_GROUPS
NEG_INF = float("-inf")


def _first_argmax(v, index, sentinel):
    m = jnp.max(v, axis=0, keepdims=True)
    first = jnp.min(jnp.where(v == m, index, sentinel), axis=0, keepdims=True)
    return m, first


def _router_kernel(x_ref, g_ref, sc_ref, sh_ref, wt_ref, rb_ref, tri_ref,
                   h_out, idx_out, gate_out, rank_out, cnt_out, base_ref):
    tm = x_ref.shape[0]
    step = pl.program_id(0)

    @pl.when(jnp.logical_or(step == 0, step == T_CTX // tm))
    def _():
        base_ref[...] = jnp.zeros_like(base_ref)

    hb = _norm_mod(x_ref[...], g_ref[...], sc_ref[0, 0], sh_ref[0, 0]).astype(BF16)
    h_out[...] = hb
    logits = lax.dot_general(wt_ref[...], hb, (((1,), (1,)), ((), ())), preferred_element_type=F32)
    scores = jax.nn.sigmoid(logits)
    choice = scores + rb_ref[...]

    member = lax.broadcasted_iota(jnp.int32, (GROUP_SIZE, tm), 0).astype(F32)
    group_scores = []
    for g in range(N_EXPERT_GROUPS):
        cg = choice[g * GROUP_SIZE:(g + 1) * GROUP_SIZE, :]
        m1, first = _first_argmax(cg, member, float(GROUP_SIZE))
        m2 = jnp.max(jnp.where(member == first, NEG_INF, cg), axis=0, keepdims=True)
        group_scores.append(m1 + m2)
    gs = jnp.concatenate(group_scores, axis=0)

    group = lax.broadcasted_iota(jnp.int32, (N_EXPERT_GROUPS, tm), 0).astype(F32)
    keep = jnp.zeros((N_EXPERT_GROUPS, tm), F32)
    for _ in range(TOPK_GROUPS):
        _, first = _first_argmax(gs, group, float(N_EXPERT_GROUPS))
        sel = group == first
        keep = jnp.where(sel, 1.0, keep)
        gs = jnp.where(sel, NEG_INF, gs)
    masked = jnp.concatenate(
        [jnp.where(keep[g:g + 1, :] > 0.0, choice[g * GROUP_SIZE:(g + 1) * GROUP_SIZE, :], NEG_INF)
         for g in range(N_EXPERT_GROUPS)], axis=0)

    expert = lax.broadcasted_iota(jnp.int32, (N_EXPERTS, tm), 0).astype(F32)
    onehot = jnp.zeros((N_EXPERTS, tm), F32)
    sels, idx_rows, gate_rows = [], [], []
    for _ in range(TOP_K):
        _, first = _first_argmax(masked, expert, float(N_EXPERTS))
        sel = expert == first
        sels.append(sel)
        idx_rows.append(first)
        gate_rows.append(jnp.sum(jnp.where(sel, scores, 0.0), axis=0, keepdims=True))
        masked = jnp.where(sel, NEG_INF, masked)
        onehot = jnp.where(sel, 1.0, onehot)
    gates = jnp.concatenate(gate_rows, axis=0)
    gate_out[...] = gates / jnp.sum(gates, axis=0, keepdims=True) * ROUTED_SCALE
    idx_out[...] = jnp.concatenate(idx_rows, axis=0).astype(jnp.int32)

    cum = jnp.dot(onehot.astype(BF16), tri_ref[...], preferred_element_type=F32) + base_ref[:, 0:1]
    rank_rows = [jnp.sum(jnp.where(sel, cum, 0.0), axis=0, keepdims=True) - 1.0 for sel in sels]
    rank_out[...] = jnp.concatenate(rank_rows, axis=0).astype(jnp.int32)
    base_ref[...] = jnp.broadcast_to(cum[:, tm - 1:tm], base_ref.shape)
    cnt_out[0] = base_ref[...].astype(jnp.int32)


def _router(x, norm_g, mods4, k_shift, k_scale, router_w_t, router_bias):
    t = T_ALL
    tm = ROUTE_TILE
    n_ctx = T_CTX // tm
    tri = (jnp.arange(tm)[:, None] <= jnp.arange(tm)[None, :]).astype(BF16)
    choice_rows = pl.BlockSpec((TOP_K, tm), lambda i: (0, i))
    return pl.pallas_call(
        _router_kernel,
        grid=(t // tm,),
        in_specs=[pl.BlockSpec((tm, D_MODEL), lambda i: (i, 0)), _resident((1, D_MODEL)),
                  _mod_spec(k_scale, tm), _mod_spec(k_shift, tm),
                  _resident((N_EXPERTS, D_MODEL)), _resident((N_EXPERTS, 1)), _resident((tm, tm))],
        out_specs=[pl.BlockSpec((tm, D_MODEL), lambda i: (i, 0)), choice_rows, choice_rows, choice_rows,
                   pl.BlockSpec((1, N_EXPERTS, 128), lambda i: (jnp.where(i < n_ctx, 0, 1), 0, 0))],
        out_shape=[jax.ShapeDtypeStruct((t, D_MODEL), BF16),
                   jax.ShapeDtypeStruct((TOP_K, t), jnp.int32), jax.ShapeDtypeStruct((TOP_K, t), F32),
                   jax.ShapeDtypeStruct((TOP_K, t), jnp.int32),
                   jax.ShapeDtypeStruct((2, N_EXPERTS, 128), jnp.int32)],
        scratch_shapes=[pltpu.VMEM((N_EXPERTS, 128), F32)],
        compiler_params=_params(("arbitrary",)),
        name="router",
    )(x, norm_g.reshape(1, D_MODEL), mods4, mods4, router_w_t, router_bias.reshape(N_EXPERTS, 1), tri)


def _pos_kernel(idx_ref, rank_ref, off_ref, pos_out):
    tm = idx_ref.shape[1]
    expert = lax.broadcasted_iota(jnp.int32, (N_EXPERTS, tm), 0)
    rows = []
    for k in range(TOP_K):
        sel = expert == idx_ref[k:k + 1, :]
        rows.append(jnp.sum(jnp.where(sel, off_ref[...], 0.0), axis=0, keepdims=True))
    pos_out[...] = jnp.concatenate(rows, axis=0).astype(jnp.int32) + rank_ref[...]


def _positions(idx, rank, offsets):
    t = idx.shape[1]
    tm = 1024
    rows = pl.BlockSpec((TOP_K, tm), lambda i: (0, i))
    return pl.pallas_call(
        _pos_kernel, grid=(t // tm,),
        in_specs=[rows, rows, _resident((N_EXPERTS, 1))],
        out_specs=rows, out_shape=jax.ShapeDtypeStruct((TOP_K, t), jnp.int32),
        compiler_params=_params(("parallel",)), name="expert_positions",
    )(idx, rank, offsets.astype(F32).reshape(N_EXPERTS, 1))


def _moe_stream(x, row0, t, h_bf16, shared, top_idx, gate, rank, counts, mods4, k_gate, exp_w13, exp_w2):
    padded = (counts + MOE_BLOCK - 1) // MOE_BLOCK * MOE_BLOCK
    e_iota = jnp.arange(N_EXPERTS)
    pend = jnp.sum(jnp.where(e_iota[:, None] <= e_iota[None, :], padded[:, None], 0), axis=0)
    pos = _positions(top_idx, rank, pend - padded)
    n_rows = -(-(t * TOP_K + N_EXPERTS * (MOE_BLOCK - 1)) // MOE_BLOCK) * MOE_BLOCK
    n_blocks = n_rows // MOE_BLOCK
    blk_start = jnp.arange(n_blocks, dtype=pend.dtype) * MOE_BLOCK
    blk_e = jnp.minimum(jnp.sum((pend[None, :] <= blk_start[:, None]).astype(jnp.int32), axis=1), N_EXPERTS - 1)
    n_real = (pend[N_EXPERTS - 1:] // MOE_BLOCK).astype(jnp.int32)
    tok = jnp.broadcast_to(jnp.arange(t, dtype=jnp.int32)[None, :], (TOP_K, t))
    _, sorted_tok = lax.sort_key_val(pos.reshape(-1), tok.reshape(-1))
    dense_end = jnp.sum(jnp.where(e_iota[:, None] <= e_iota[None, :], counts[:, None], 0), axis=0)
    shift_blk = ((pend - padded) - (dense_end - counts))[blk_e]
    end_blk = dense_end[blk_e]
    row = jnp.arange(n_rows, dtype=jnp.int32)
    dense = row - jnp.repeat(shift_blk, MOE_BLOCK)
    spread = (row & (t - 1)) if t & (t - 1) == 0 else (row % t)
    row_tok = jnp.where(dense < jnp.repeat(end_blk, MOE_BLOCK),
                        sorted_tok[jnp.clip(dense, 0, t * TOP_K - 1)], spread)
    xs = h_bf16[row_tok + row0]
    ys = _grouped_experts(blk_e, n_real, padded, xs, exp_w13, exp_w2)
    y_exp = ys[pos.reshape(-1)].reshape(TOP_K, t, D_MODEL)
    return _combine(shared, x, row0, t, y_exp, gate.T, mods4, k_gate)


def _moe_streams(x, norm_g, mods4, k_shift, k_scale, k_gate, router_w, router_bias, exp_w13, exp_w2, sh_w13, sh_w2):
    h_bf16, top_idx, gate, rank, counts = _router(x, norm_g, mods4, k_shift, k_scale, router_w.T.astype(BF16),
                                                  router_bias)
    shared = _shared_ffn(h_bf16, sh_w13.astype(BF16), sh_w2.astype(BF16))
    outs = []
    for s, (row0, t) in enumerate(((0, T_CTX), (T_CTX, T_LAT))):
        cols = slice(row0, row0 + t)
        outs.append(_moe_stream(x, row0, t, h_bf16, shared, top_idx[:, cols], gate[:, cols], rank[:, cols],
                                counts[s, :, 0], mods4, k_gate, exp_w13, exp_w2))
    return tuple(outs)


HY_BLOCK_CTX, HY_TC_CTX = 256, 512
HY_BLOCK_LAT, HY_TC_LAT = 512, 128


def _hyena_filters(L, w1, b1, w2, b2, w3, sin_freq):
    t_norm = jnp.linspace(0.0, 1.0, L, dtype=F32)[:, None]
    omega = 2.0 * math.pi * jnp.arange(L, dtype=F32)[:, None] / L
    bands = jnp.linspace(1e-4, FILTER_BANDS - 1, FILTER_BANDS, dtype=F32)[None, :]
    z = jnp.concatenate([t_norm, jnp.cos(bands * omega), -jnp.sin(bands * omega)], axis=-1)
    h = jnp.sin(sin_freq * (z @ w1 + b1))
    h = jnp.sin(sin_freq * (h @ w2 + b2))
    h = (h @ w3).reshape(L, 2, HYENA_ORDER, HYENA_WIDTH)
    deltas = jnp.abs(jnp.linspace(math.log(DECAY_TARGET) / DECAY_PCT_LONG,
                                  math.log(DECAY_TARGET) / DECAY_PCT_SHORT, HYENA_WIDTH, dtype=F32))
    h = h * jnp.exp(-t_norm * deltas)[:, None, None, :]
    fwd, bwd = h[:, 0], h[:, 1] * (t_norm > 0.0)[:, :, None]
    norm = jnp.sum(jnp.abs(fwd), axis=0, keepdims=True) + jnp.sum(jnp.abs(bwd), axis=0, keepdims=True)
    return fwd / norm, bwd / norm


HY_ACC_ELEMS = 4096


def _dft_matrices(bk):
    k = jnp.arange(bk, dtype=jnp.int32)
    phase = ((2 * k[:, None] + 1) * k[None, :]) % (4 * bk)
    ang = phase.astype(F32) * (math.pi / (2 * bk))
    return jnp.cos(ang), jnp.sin(ang)


def _spectra_kernel(xf_ref, xb_ref, yf_ref, yb_ref, c1_ref, s1_ref, c2_ref, s2_ref, o_ref, *, nb):
    d = pl.program_id(1) - (nb - 1)
    x = jnp.where(d >= 0, xf_ref[0], xb_ref[0]).astype(BF16)
    y = jnp.where(d >= 1, yf_ref[0], yb_ref[0]).astype(BF16)
    cy = jnp.where(d == 0, c1_ref[...], c2_ref[...])
    sy = jnp.where(d == 0, s1_ref[...], s2_ref[...])
    re = jnp.dot(c1_ref[...], x, preferred_element_type=F32) + jnp.dot(cy, y, preferred_element_type=F32)
    im = jnp.dot(sy, y, preferred_element_type=F32) - jnp.dot(s1_ref[...], x, preferred_element_type=F32)
    o_ref[0, 0, 0] = re
    o_ref[0, 0, 1] = jnp.where(d >= 0, im, -im)


def _filter_spectra(fwd, bwd, bk, tc):
    L = fwd.shape[0]
    nb = L // bk
    n_ct = HYENA_WIDTH // tc
    cos, sin = _dft_matrices(bk)
    k = jnp.arange(bk, dtype=jnp.int32)
    phase = ((2 * k[:, None] + 1) * (bk - k[None, :])) % (4 * bk)
    ang = phase.astype(F32) * (math.pi / (2 * bk))
    live = (k[None, :] > 0).astype(F32)
    mats = [m.astype(BF16) for m in (cos, sin, jnp.cos(ang) * live, jnp.sin(ang) * live)]

    def seg(pick):
        return pl.BlockSpec((1, bk, tc), lambda o, di, ct: (pick(di - (nb - 1)), 0, o * n_ct + ct))

    mat = pl.BlockSpec((bk, bk), lambda o, di, ct: (0, 0), pipeline_mode=pl.Buffered(1))
    halves = [h.reshape(nb, bk, HYENA_ORDER * HYENA_WIDTH) for h in (fwd, bwd)]
    return pl.pallas_call(
        functools.partial(_spectra_kernel, nb=nb),
        grid=(HYENA_ORDER, 2 * nb - 1, n_ct),
        in_specs=[seg(lambda d: jnp.maximum(d, 0)), seg(lambda d: jnp.maximum(-d, 0)),
                  seg(lambda d: jnp.maximum(d - 1, 0)), seg(lambda d: jnp.maximum(-d - 1, 0)),
                  mat, mat, mat, mat],
        out_specs=pl.BlockSpec((1, 1, 2, bk, tc), lambda o, di, ct: (o, di, 0, 0, ct)),
        out_shape=jax.ShapeDtypeStruct((HYENA_ORDER, 2 * nb - 1, 2, bk, HYENA_WIDTH), F32),
        compiler_params=_params(("parallel", "parallel", "parallel")),
        name="filter_spectra",
    )(halves[0], halves[1], halves[0], halves[1], *mats)


def _hyena_kernel(v_ref, x1_ref, x2_ref, cw_ref, cb_ref, hb_ref, k0_ref, k1_ref, c_ref, s_ref, ct_ref, st_ref,
                  o_ref, vs_ref, z1_ref, x2s_ref, ur_ref, ui_ref, yr_ref, yi_ref, *, seq, bk, tc):
    nb = seq // bk
    ft = HY_ACC_ELEMS // tc
    row = lax.broadcasted_iota(jnp.int32, (seq, tc), 0)

    def short_conv(x, j):
        prev = jnp.where(row == 0, 0.0, pltpu.roll(x, 1, 0))
        nxt = jnp.where(row == seq - 1, 0.0, pltpu.roll(x, seq - 1, 0))
        return (prev * cw_ref[0, j:j + 1, :] + x * cw_ref[1, j:j + 1, :] + nxt * cw_ref[2, j:j + 1, :]
                + cb_ref[j:j + 1, :])

    vs_ref[...] = short_conv(v_ref[...], 0)
    z1_ref[...] = short_conv(x1_ref[...], 1)
    x2s_ref[...] = short_conv(x2_ref[...], 2)

    def long_conv(src_ref, k_ref, emit):
        for blk in range(nb):
            ub = src_ref[blk * bk:(blk + 1) * bk, :].astype(BF16)
            ur_ref[blk] = jnp.dot(c_ref[...], ub, preferred_element_type=F32)
            ui_ref[blk] = -jnp.dot(s_ref[...], ub, preferred_element_type=F32)
        for out_blk in range(nb):
            def acc_tile(f, carry, out_blk=out_blk):
                rows = pl.ds(pl.multiple_of(f * ft, ft), ft)
                ar = jnp.zeros((ft, tc), F32)
                ai = jnp.zeros((ft, tc), F32)
                for in_blk in range(nb):
                    d = out_blk - in_blk + nb - 1
                    kr, ki = k_ref[d, 0, rows, :], k_ref[d, 1, rows, :]
                    xr, xi = ur_ref[in_blk, rows, :], ui_ref[in_blk, rows, :]
                    ar = ar + (kr * xr - ki * xi)
                    ai = ai + (kr * xi + ki * xr)
                yr_ref[rows, :] = ar
                yi_ref[rows, :] = ai
                return carry

            lax.fori_loop(0, bk // ft, acc_tile, 0)
            y = jnp.dot(ct_ref[...], yr_ref[...].astype(BF16), preferred_element_type=F32)
            y = y - jnp.dot(st_ref[...], yi_ref[...].astype(BF16), preferred_element_type=F32)
            emit(slice(out_blk * bk, (out_blk + 1) * bk), y * (1.0 / bk))

    def emit_z1(rows, y):
        z1_ref[rows, :] = z1_ref[rows, :] * (y + hb_ref[0:1, :] * vs_ref[rows, :])

    def emit_out(rows, y):
        o_ref[rows, :] = (x2s_ref[rows, :] * (y + hb_ref[1:2, :] * z1_ref[rows, :])).astype(BF16)

    long_conv(vs_ref, k0_ref, emit_z1)
    long_conv(z1_ref, k1_ref, emit_out)


def _hyena(proj, row0, n_batch, seq, bk, tc, conv_w, conv_b, hy_bias, spectra, mats):
    nb = seq // bk
    n_ct = HYENA_WIDTH // tc
    col0 = QKV_WIDTH // tc
    per = HYENA_WIDTH // tc
    b0 = row0 // seq

    def data(which):
        return pl.BlockSpec((seq, tc), lambda ct, b: (b0 + b, col0 + which * per + ct))

    const2 = lambda shape: pl.BlockSpec(shape, lambda ct, b: (0, 0), pipeline_mode=pl.Buffered(1))
    kspec = pl.BlockSpec((2 * nb - 1, 2, bk, tc), lambda ct, b: (0, 0, 0, ct), pipeline_mode=pl.Buffered(1))
    cw = conv_w.reshape(3, HYENA_ORDER + 1, HYENA_WIDTH)
    cb = conv_b.reshape(HYENA_ORDER + 1, HYENA_WIDTH)
    return pl.pallas_call(
        functools.partial(_hyena_kernel, seq=seq, bk=bk, tc=tc),
        grid=(n_ct, n_batch),
        in_specs=[data(0), data(1), data(2),
                  pl.BlockSpec((3, HYENA_ORDER + 1, tc), lambda ct, b: (0, 0, ct)),
                  pl.BlockSpec((HYENA_ORDER + 1, tc), lambda ct, b: (0, ct)),
                  pl.BlockSpec((HYENA_ORDER, tc), lambda ct, b: (0, ct)),
                  kspec, kspec, const2((bk, bk)), const2((bk, bk)), const2((bk, bk)), const2((bk, bk))],
        out_specs=pl.BlockSpec((seq, tc), lambda ct, b: (b, ct)),
        out_shape=jax.ShapeDtypeStruct((n_batch * seq, HYENA_WIDTH), BF16),
        scratch_shapes=[pltpu.VMEM((seq, tc), F32), pltpu.VMEM((seq, tc), F32), pltpu.VMEM((seq, tc), F32),
                        pltpu.VMEM((nb, bk, tc), F32), pltpu.VMEM((nb, bk, tc), F32),
                        pltpu.VMEM((bk, tc), F32), pltpu.VMEM((bk, tc), F32)],
        compiler_params=_params(("parallel", "parallel")),
        name="hyena_seq%d" % seq,
    )(proj, proj, proj, cw, cb, hy_bias, spectra[0], spectra[1], *mats)


HY_SPECTRA_TC = 512


def _hyena_operands(L, bk, f_w1, f_b1, f_w2, f_b2, f_w3, sin_freq):
    fwd, bwd = _hyena_filters(L, f_w1, f_b1, f_w2, f_b2, f_w3, sin_freq)
    cos, sin = _dft_matrices(bk)
    mats = tuple(m.astype(BF16) for m in (cos, sin, cos.T, sin.T))
    return _filter_spectra(fwd, bwd, bk, HY_SPECTRA_TC), mats


S5_LANES = S5_GROUPS * S5_STATE
S5_CLUSTER_GROUPS = 16
S5_CLUSTERS = S5_GROUPS // S5_CLUSTER_GROUPS
S5_CLUSTER_CH = S5_CLUSTER_GROUPS * S5_GROUP_CH
S5_CLUSTER_LANES = S5_CLUSTER_GROUPS * S5_STATE
SCAN_SEQS = 8
SCAN_ROWS = 512


def _s5_kernel(u_ref, s0r_ref, s0i_ref, bw_ref, cwr_ref, cwi_ref, lam_ref, y_ref, sfr_ref, sfi_ref,
               sre, sim, car, cai, *, reverse):
    n_seq, tc, _ = u_ref.shape

    @pl.when(pl.program_id(1) == 0)
    def _():
        car[...] = jnp.zeros_like(car)
        cai[...] = jnp.zeros_like(cai)
        car[0:n_seq, :] = s0r_ref[0]
        cai[0:n_seq, :] = s0i_ref[0]

    u = u_ref[...]
    if n_seq < SCAN_SEQS:
        u = jnp.concatenate([u, jnp.zeros((SCAN_SEQS - n_seq, tc, S5_WIDTH), F32)], axis=0)
    ub = jnp.swapaxes(u, 0, 1).reshape(tc * SCAN_SEQS, S5_WIDTH).astype(BF16)
    for k in range(S5_CLUSTERS):
        bu = jnp.dot(ub[:, k * S5_CLUSTER_CH:(k + 1) * S5_CLUSTER_CH], bw_ref[k], preferred_element_type=F32)
        sre[:, k * S5_CLUSTER_LANES:(k + 1) * S5_CLUSTER_LANES] = bu[:, :S5_CLUSTER_LANES]
        sim[:, k * S5_CLUSTER_LANES:(k + 1) * S5_CLUSTER_LANES] = bu[:, S5_CLUSTER_LANES:]

    for k in range(S5_CLUSTERS):
        lanes = slice(k * S5_CLUSTER_LANES, (k + 1) * S5_CLUSTER_LANES)
        lr = jnp.broadcast_to(lam_ref[0:1, lanes], (SCAN_SEQS, S5_CLUSTER_LANES))
        li = jnp.broadcast_to(lam_ref[1:2, lanes], (SCAN_SEQS, S5_CLUSTER_LANES))

        def step(i, state, lanes=lanes, lr=lr, li=li):
            sr, si = state
            t = (tc - 1 - i) if reverse else i
            rows = pl.ds(pl.multiple_of(t * SCAN_SEQS, SCAN_SEQS), SCAN_SEQS)
            nr = (lr * sr - li * si) + sre[rows, lanes]
            ni = (lr * si + li * sr) + sim[rows, lanes]
            sre[rows, lanes] = nr
            sim[rows, lanes] = ni
            return nr, ni

        sr, si = lax.fori_loop(0, tc, step, (car[:, lanes], cai[:, lanes]), unroll=2)
        car[:, lanes] = sr
        cai[:, lanes] = si

    parts = []
    for k in range(S5_CLUSTERS):
        lanes = slice(k * S5_CLUSTER_LANES, (k + 1) * S5_CLUSTER_LANES)
        yk = jnp.dot(sre[:, lanes].astype(BF16), cwr_ref[k], preferred_element_type=F32)
        parts.append(yk + jnp.dot(sim[:, lanes].astype(BF16), cwi_ref[k], preferred_element_type=F32))
    y = jnp.concatenate(parts, axis=1).reshape(tc, SCAN_SEQS, S5_WIDTH)
    y_ref[...] = jnp.swapaxes(y, 0, 1)[0:n_seq]
    sfr_ref[0] = car[0:n_seq, :]
    sfi_ref[0] = cai[0:n_seq, :]


def _s5_direction(u, n_seq, seq, group, s0_re, s0_im, bw, cwr, cwi, lam, reverse):
    tc = SCAN_ROWS // SCAN_SEQS
    nc = seq // tc
    ng = n_seq // group

    def chunk(c):
        return (nc - 1 - c) if reverse else c

    state_spec = pl.BlockSpec((1, group, S5_LANES), lambda g, c: (g, 0, 0))
    const = lambda a: pl.BlockSpec(a.shape, lambda g, c: (0,) * a.ndim, pipeline_mode=pl.Buffered(1))
    y, f_re, f_im = pl.pallas_call(
        functools.partial(_s5_kernel, reverse=reverse),
        grid=(ng, nc),
        in_specs=[pl.BlockSpec((group, tc, S5_WIDTH), lambda g, c: (g, chunk(c), 0)),
                  state_spec, state_spec, const(bw), const(cwr), const(cwi), const(lam)],
        out_specs=[pl.BlockSpec((group, tc, S5_WIDTH), lambda g, c: (g, chunk(c), 0)),
                   state_spec, state_spec],
        out_shape=[jax.ShapeDtypeStruct((n_seq, seq, S5_WIDTH), F32),
                   jax.ShapeDtypeStruct((ng, group, S5_LANES), F32),
                   jax.ShapeDtypeStruct((ng, group, S5_LANES), F32)],
        scratch_shapes=[pltpu.VMEM((SCAN_ROWS, S5_LANES), F32), pltpu.VMEM((SCAN_ROWS, S5_LANES), F32),
                        pltpu.VMEM((SCAN_SEQS, S5_LANES), F32), pltpu.VMEM((SCAN_SEQS, S5_LANES), F32)],
        compiler_params=_params(("parallel", "arbitrary")),
        name="s5_bwd" if reverse else "s5_fwd",
    )(u.reshape(n_seq, seq, S5_WIDTH), s0_re.reshape(ng, group, S5_LANES), s0_im.reshape(ng, group, S5_LANES),
      bw, cwr, cwi, lam)
    return y.reshape(n_seq * seq, S5_WIDTH), f_re.reshape(n_seq, S5_LANES), f_im.reshape(n_seq, S5_LANES)


def _s5_weights(a_re, a_im, log_dt, b_re, b_im, c_re, c_im):
    lam = lax.complex(a_re, a_im)
    lam_bar = jnp.exp(lam * jnp.exp(log_dt)[..., None])
    b_bar = ((lam_bar - 1.0) / lam)[..., None] * lax.complex(b_re, b_im)
    eye = jnp.eye(S5_CLUSTER_GROUPS, dtype=F32)
    out = []
    for d in range(2):
        def cluster_in(w):
            w = w.reshape(S5_CLUSTERS, S5_CLUSTER_GROUPS, S5_STATE, S5_GROUP_CH)
            return jnp.einsum('ab,kapc->kacbp', eye, w).reshape(S5_CLUSTERS, S5_CLUSTER_CH, S5_CLUSTER_LANES)

        def cluster_out(w):
            w = w.reshape(S5_CLUSTERS, S5_CLUSTER_GROUPS, S5_GROUP_CH, S5_STATE)
            return jnp.einsum('ab,kacp->kapbc', eye, w).reshape(S5_CLUSTERS, S5_CLUSTER_LANES, S5_CLUSTER_CH)

        bw = jnp.concatenate([cluster_in(jnp.real(b_bar[d])), cluster_in(jnp.imag(b_bar[d]))], axis=-1)
        cwr = cluster_out(c_re[d])
        cwi = cluster_out(-c_im[d])
        lam_d = lam_bar[d].reshape(S5_LANES)
        lam2 = jnp.stack([jnp.real(lam_d), jnp.imag(lam_d)]).astype(F32)
        out.append((bw.astype(BF16), cwr.astype(BF16), cwi.astype(BF16), lam2))
    return out


def _final_norm_kernel(x_ref, g_ref, o_ref):
    x = x_ref[...]
    ms = jnp.mean(x * x, axis=-1, keepdims=True)
    o_ref[...] = x * lax.rsqrt(ms + RMS_EPS) * g_ref[...]


def _final_norm(x, g, row0, t):
    tm = ROW_TILE
    b0 = row0 // tm
    return pl.pallas_call(
        _final_norm_kernel,
        grid=(t // tm,),
        in_specs=[pl.BlockSpec((tm, D_MODEL), lambda i: (b0 + i, 0)), _resident((1, D_MODEL))],
        out_specs=pl.BlockSpec((tm, D_MODEL), lambda i: (i, 0)),
        out_shape=jax.ShapeDtypeStruct((t, D_MODEL), F32),
        compiler_params=_params(("parallel",)),
        name="final_norm",
    )(x, g.reshape(1, D_MODEL))


def _rope_tables(L):
    n_rows = L // GRID_W
    row_idx = jnp.repeat(jnp.arange(n_rows, dtype=F32), GRID_W)
    col_idx = (jnp.arange(L) % GRID_W).astype(F32)
    inv = ROPE_THETA ** (-jnp.arange(0, ROPE_AXIS_DIM, 2, dtype=F32) / ROPE_AXIS_DIM)
    ang = jnp.concatenate([row_idx[:, None] * inv, col_idx[:, None] * inv], axis=-1)
    cos, sin = jnp.cos(ang), jnp.sin(ang)
    cos_full = jnp.repeat(cos, 2, axis=-1)
    sin_signed = jnp.stack([-sin, sin], axis=-1).reshape(L, HEAD_DIM)
    return cos_full, sin_signed


def kernel(x_prompt, x_sample, cache_k_l0, cache_v_l0, state_s5_re_l1, state_s5_im_l1, c, c_ctx,
           ada_w_l0, ada_b_l0, norm1_l0, norm2_l0,
           w_in_l0, w_out_l0, q_norm_l0, k_norm_l0, hy_conv_w_l0, hy_conv_b_l0,
           hy_ffn_w1_l0, hy_ffn_b1_l0, hy_ffn_w2_l0, hy_ffn_b2_l0, hy_ffn_w3_l0, hy_sin_freq_l0, hy_bias_l0,
           router_l0, router_bias_l0, exp_w13_l0, exp_w2_l0, shared_w13_l0, shared_w2_l0,
           ada_w_l1, ada_b_l1, norm1_l1, norm2_l1,
           w_in_l1, s5_a_re_l1, s5_a_im_l1, s5_log_dt_l1, s5_b_re_l1, s5_b_im_l1,
           s5_c_re_l1, s5_c_im_l1, s5_d_l1, w_glu_l1,
           router_l1, router_bias_l1, exp_w13_l1, exp_w2_l1, shared_w13_l1, shared_w2_l1,
           final_norm):
    x = (x_prompt.reshape(T_CTX, D_MODEL), x_sample.reshape(T_LAT, D_MODEL))
    cond = jnp.concatenate([c_ctx[None, :], c, jnp.zeros((N_COND - 1 - DEC_BATCH, D_MODEL), F32)], axis=0)

    mods = _ada_mods(cond, ada_w_l0, ada_b_l0)
    proj = _norm_mod_matmul(x, norm1_l0, mods, 0, 1, w_in_l0.astype(BF16))

    q_c, k_c, v_c, new_k = _qk_prep(proj, 0, T_CTX, q_norm_l0, k_norm_l0, None)
    attn_c = _attention(q_c, k_c, v_c, BATCH, SEQ, None)
    q_l, k_l, v_l = _qk_prep(proj, T_CTX, T_LAT, q_norm_l0, k_norm_l0, _rope_tables(DEC_SEQ))
    ctx_kv = (cache_k_l0.reshape(DEC_BATCH * PAST_LEN, KV_WIDTH).astype(BF16),
              cache_v_l0.reshape(DEC_BATCH * PAST_LEN, KV_WIDTH).astype(BF16))
    attn_l = _attention(q_l, k_l, v_l, DEC_BATCH, DEC_SEQ, ctx_kv)

    hy_f = (hy_ffn_w1_l0, hy_ffn_b1_l0, hy_ffn_w2_l0, hy_ffn_b2_l0, hy_ffn_w3_l0, hy_sin_freq_l0)
    spec_c, mats_c = _hyena_operands(SEQ, HY_BLOCK_CTX, *hy_f)
    z_c = _hyena(proj, 0, BATCH, SEQ, HY_BLOCK_CTX, HY_TC_CTX, hy_conv_w_l0, hy_conv_b_l0, hy_bias_l0,
                 spec_c, mats_c)
    spec_l, mats_l = _hyena_operands(DEC_SEQ, HY_BLOCK_LAT, *hy_f)
    z_l = _hyena(proj, T_CTX, DEC_BATCH, DEC_SEQ, HY_BLOCK_LAT, HY_TC_LAT, hy_conv_w_l0, hy_conv_b_l0, hy_bias_l0,
                 spec_l, mats_l)

    w_out = w_out_l0.astype(BF16)
    x = _out_proj((attn_c, attn_l), (z_c, z_l), x, mods, 2, w_out[:Q_WIDTH], w_out[Q_WIDTH:])
    x = _moe_streams(x, norm2_l0, mods, 3, 4, 5, router_l0, router_bias_l0, exp_w13_l0, exp_w2_l0,
                     shared_w13_l0, shared_w2_l0)

    new_v = proj[:T_CTX, Q_WIDTH + KV_WIDTH:QKV_WIDTH]

    mods = _ada_mods(cond, ada_w_l1, ada_b_l1)
    u = _norm_mod_matmul(x, norm1_l1, mods, 0, 1, w_in_l1.astype(BF16))
    s5_w = _s5_weights(s5_a_re_l1, s5_a_im_l1, s5_log_dt_l1, s5_b_re_l1, s5_b_im_l1, s5_c_re_l1, s5_c_im_l1)
    zero_state = jnp.zeros((BATCH, S5_LANES), F32)
    u_c, u_l = u[:T_CTX], u[T_CTX:]
    ys, finals = [], []
    for d in range(2):
        y_c, f_re, f_im = _s5_direction(u_c, BATCH, SEQ, SCAN_SEQS, zero_state, zero_state, *s5_w[d],
                                        reverse=d == 1)
        y_l, _, _ = _s5_direction(u_l, DEC_BATCH, DEC_SEQ, DEC_BATCH,
                                  state_s5_re_l1[:, d].reshape(DEC_BATCH, S5_LANES),
                                  state_s5_im_l1[:, d].reshape(DEC_BATCH, S5_LANES), *s5_w[d], reverse=d == 1)
        ys.append((y_c, y_l))
        finals.append((f_re.reshape(BATCH, S5_GROUPS, S5_STATE), f_im.reshape(BATCH, S5_GROUPS, S5_STATE)))
    s_re = jnp.stack([finals[0][0], finals[1][0]], axis=1)
    s_im = jnp.stack([finals[0][1], finals[1][1]], axis=1)
    x = _glu_proj(ys[0], ys[1], u, s5_d_l1, x, mods, 2, w_glu_l1.astype(BF16))
    x_c, x_l = _moe_streams(x, norm2_l1, mods, 3, 4, 5, router_l1, router_bias_l1, exp_w13_l1, exp_w2_l1,
                            shared_w13_l1, shared_w2_l1)

    return (_final_norm(x_c, final_norm, 0, T_CTX).reshape(BATCH, SEQ, D_MODEL),
            _final_norm(x_l, final_norm, 0, T_LAT).reshape(DEC_BATCH, DEC_SEQ, D_MODEL),
            new_k.reshape(BATCH, SEQ, N_KV_HEADS, HEAD_DIM),
            new_v.reshape(BATCH, SEQ, N_KV_HEADS, HEAD_DIM),
            s_re, s_im)
```

```python
import functools
import math

import jax
import jax.numpy as jnp
from jax import lax
from jax.experimental import pallas as pl
from jax.experimental.pallas import tpu as pltpu

F32 = jnp.float32
BF16 = jnp.bfloat16

D_MODEL = 2048
BATCH = 32
SEQ = 256
DEC_BATCH = 4
DEC_SEQ = 4096
PAST_LEN = 256
GRID_W = 64
RMS_EPS = 1e-6
HEAD_DIM = 128
N_HEADS = 8
N_KV_HEADS = 2
GQA_GROUP = N_HEADS // N_KV_HEADS
Q_WIDTH = N_HEADS * HEAD_DIM
KV_WIDTH = N_KV_HEADS * HEAD_DIM
QKV_WIDTH = Q_WIDTH + 2 * KV_WIDTH
ROPE_THETA = 10000.0
ROPE_AXIS_DIM = HEAD_DIM // 2
HYENA_WIDTH = D_MODEL // 2
HYENA_ORDER = 2
FILTER_BANDS = 16
DECAY_TARGET = 1e-2
DECAY_PCT_SHORT = 0.3
DECAY_PCT_LONG = 1.5
S5_WIDTH = D_MODEL // 2
S5_GROUP_CH = 16
S5_GROUPS = S5_WIDTH // S5_GROUP_CH
S5_STATE = 64
N_EXPERTS = 64
TOP_K = 8
N_EXPERT_GROUPS = 8
TOPK_GROUPS = 4
EXPERT_HIDDEN = 512
ROUTED_SCALE = 2.5
MOE_BLOCK = 256

T_CTX = BATCH * SEQ
T_LAT = DEC_BATCH * DEC_SEQ
T_ALL = T_CTX + T_LAT
N_COND = 8

ROW_TILE = 256
V7X_VMEM_LIMIT_BYTES = 56 * 1024 * 1024


def _params(semantics):
    return pltpu.CompilerParams(dimension_semantics=semantics, vmem_limit_bytes=V7X_VMEM_LIMIT_BYTES)


def _cond_row(i, tm, row0=0):
    tok = row0 + i * tm
    return jnp.where(tok < T_CTX, 0, 1 + (tok - T_CTX) // DEC_SEQ)


def _mod_spec(k, tm, row0=0):
    return pl.BlockSpec((1, 1, 1, D_MODEL), lambda i: (_cond_row(i, tm, row0), k, 0, 0))


def _resident(shape):
    nd = len(shape)
    return pl.BlockSpec(shape, lambda i: (0,) * nd, pipeline_mode=pl.Buffered(1))


def _silu(x):
    return x * jax.nn.sigmoid(x)


def _norm_mod(x, g, sc, sh):
    ms = jnp.mean(x * x, axis=-1, keepdims=True)
    return (x * lax.rsqrt(ms + RMS_EPS) * g) * (1.0 + sc) + sh


def _ada_kernel(c_ref, w_ref, b_ref, o_ref):
    a = _silu(c_ref[...]).astype(BF16)
    o_ref[...] = jnp.dot(a, w_ref[...].astype(BF16), preferred_element_type=F32) + b_ref[...]


def _ada_mods(cond, w, b):
    n = w.shape[1]
    tn = 2048
    out = pl.pallas_call(
        _ada_kernel,
        grid=(n // tn,),
        in_specs=[pl.BlockSpec((N_COND, D_MODEL), lambda j: (0, 0)),
                  pl.BlockSpec((D_MODEL, tn), lambda j: (0, j)),
                  pl.BlockSpec((1, tn), lambda j: (0, j))],
        out_specs=pl.BlockSpec((N_COND, tn), lambda j: (0, j)),
        out_shape=jax.ShapeDtypeStruct((N_COND, n), F32),
        compiler_params=_params(("parallel",)),
        name="ada_mods",
    )(cond, w, b.reshape(1, n))
    return out.reshape(N_COND, 6, 1, D_MODEL)


def _split_row_specs(width, tm):
    n_ctx = T_CTX // tm
    return [pl.BlockSpec((tm, width), lambda i: (jnp.minimum(i, n_ctx - 1), 0)),
            pl.BlockSpec((tm, width), lambda i: (jnp.maximum(i - n_ctx, 0), 0))]


def _pick_rows(ctx_ref, lat_ref):
    return jnp.where(pl.program_id(0) < T_CTX // ctx_ref.shape[0], ctx_ref[...], lat_ref[...])


def _nmm_kernel(*refs, split):
    if split:
        xc_ref, xl_ref, g_ref, sc_ref, sh_ref, w_ref, o_ref = refs
        x = _pick_rows(xc_ref, xl_ref)
    else:
        x_ref, g_ref, sc_ref, sh_ref, w_ref, o_ref = refs
        x = x_ref[...]
    h = _norm_mod(x, g_ref[...], sc_ref[0, 0], sh_ref[0, 0])
    o_ref[...] = jnp.dot(h.astype(BF16), w_ref[...], preferred_element_type=F32)


def _norm_mod_matmul(x, g, mods4, k_shift, k_scale, w_bf16):
    n = w_bf16.shape[1]
    tm = ROW_TILE
    split = isinstance(x, tuple)
    x_args = list(x) if split else [x]
    x_specs = _split_row_specs(D_MODEL, tm) if split else [pl.BlockSpec((tm, D_MODEL), lambda i: (i, 0))]
    return pl.pallas_call(
        functools.partial(_nmm_kernel, split=split),
        grid=(T_ALL // tm,),
        in_specs=x_specs + [_resident((1, D_MODEL)), _mod_spec(k_scale, tm), _mod_spec(k_shift, tm),
                            _resident((D_MODEL, n))],
        out_specs=pl.BlockSpec((tm, n), lambda i: (i, 0)),
        out_shape=jax.ShapeDtypeStruct((T_ALL, n), F32),
        compiler_params=_params(("parallel",)),
        name="norm_mod_matmul",
    )(*x_args, g.reshape(1, D_MODEL), mods4, mods4, w_bf16)


def _head_norm(xh, g):
    ms = jnp.mean(xh * xh, axis=-1, keepdims=True)
    return xh * lax.rsqrt(ms + RMS_EPS) * g


def _rope(y, cos, sin_signed):
    lane = lax.broadcasted_iota(jnp.int32, y.shape, 1)
    partner = jnp.where(lane % 2 == 0, pltpu.roll(y, HEAD_DIM - 1, 1), pltpu.roll(y, 1, 1))
    return y * cos + partner * sin_signed


def _qkprep_ctx_kernel(qkv_ref, qg_ref, kg_ref, q_out, kb_out, vb_out, kf_out):
    for h in range(N_HEADS):
        sl = slice(h * HEAD_DIM, (h + 1) * HEAD_DIM)
        q_out[:, sl] = _head_norm(qkv_ref[:, sl], qg_ref[...]).astype(BF16)
    for j in range(N_KV_HEADS):
        k = _head_norm(qkv_ref[:, Q_WIDTH + j * HEAD_DIM:Q_WIDTH + (j + 1) * HEAD_DIM], kg_ref[...])
        kf_out[:, j * HEAD_DIM:(j + 1) * HEAD_DIM] = k
        kb_out[:, j * HEAD_DIM:(j + 1) * HEAD_DIM] = k.astype(BF16)
    vb_out[...] = qkv_ref[:, Q_WIDTH + KV_WIDTH:QKV_WIDTH].astype(BF16)


def _qkprep_lat_kernel(qkv_ref, qg_ref, kg_ref, cos_ref, sin_ref, q_out, kb_out, vb_out):
    cos, sin = cos_ref[...], sin_ref[...]
    for h in range(N_HEADS):
        sl = slice(h * HEAD_DIM, (h + 1) * HEAD_DIM)
        q_out[:, sl] = _rope(_head_norm(qkv_ref[:, sl], qg_ref[...]), cos, sin).astype(BF16)
    for j in range(N_KV_HEADS):
        k = _head_norm(qkv_ref[:, Q_WIDTH + j * HEAD_DIM:Q_WIDTH + (j + 1) * HEAD_DIM], kg_ref[...])
        kb_out[:, j * HEAD_DIM:(j + 1) * HEAD_DIM] = _rope(k, cos, sin).astype(BF16)
    vb_out[...] = qkv_ref[:, Q_WIDTH + KV_WIDTH:QKV_WIDTH].astype(BF16)


def _qk_prep(proj, row0, t, q_norm, k_norm, rope_tabs):
    tm = ROW_TILE
    b0 = row0 // tm
    qkv_spec = pl.BlockSpec((tm, QKV_WIDTH), lambda i: (b0 + i, 0))
    gain = _resident((1, HEAD_DIM))
    outs = [pl.BlockSpec((tm, Q_WIDTH), lambda i: (i, 0)),
            pl.BlockSpec((tm, KV_WIDTH), lambda i: (i, 0)),
            pl.BlockSpec((tm, KV_WIDTH), lambda i: (i, 0))]
    shapes = [jax.ShapeDtypeStruct((t, Q_WIDTH), BF16),
              jax.ShapeDtypeStruct((t, KV_WIDTH), BF16),
              jax.ShapeDtypeStruct((t, KV_WIDTH), BF16)]
    qg, kg = q_norm.reshape(1, HEAD_DIM), k_norm.reshape(1, HEAD_DIM)
    if rope_tabs is None:
        return pl.pallas_call(
            _qkprep_ctx_kernel, grid=(t // tm,),
            in_specs=[qkv_spec, gain, gain],
            out_specs=outs + [pl.BlockSpec((tm, KV_WIDTH), lambda i: (i, 0))],
            out_shape=shapes + [jax.ShapeDtypeStruct((t, KV_WIDTH), F32)],
            compiler_params=_params(("parallel",)), name="qk_prep_ctx",
        )(proj, qg, kg)
    cos, sin = rope_tabs
    nq = DEC_SEQ // tm
    tab = pl.BlockSpec((tm, HEAD_DIM), lambda i: (i % nq, 0))
    return pl.pallas_call(
        _qkprep_lat_kernel, grid=(t // tm,),
        in_specs=[qkv_spec, gain, gain, tab, tab],
        out_specs=outs, out_shape=shapes,
        compiler_params=_params(("parallel",)), name="qk_prep_lat",
    )(proj, qg, kg, cos, sin)


def _attn_kernel(*refs, with_ctx):
    if with_ctx:
        q_ref, k_ref, v_ref, ck_ref, cv_ref, o_ref = refs
    else:
        q_ref, k_ref, v_ref, o_ref = refs
    c = HEAD_DIM ** -0.5 * math.log2(math.e)
    nt = (((1,), (1,)), ((), ()))
    for hh in range(ATTN_HEADS_PER_STEP):
        cols = slice(hh * HEAD_DIM, (hh + 1) * HEAD_DIM)
        q = q_ref[:, cols]
        s = lax.dot_general(q, k_ref[...], nt, preferred_element_type=F32)
        m = jnp.max(s, axis=-1, keepdims=True)
        if with_ctx:
            sc = lax.dot_general(q, ck_ref[...], nt, preferred_element_type=F32)
            m = jnp.maximum(m, jnp.max(sc, axis=-1, keepdims=True))
        p = jnp.exp2((s - m) * c)
        l = jnp.sum(p, axis=-1, keepdims=True)
        o = jnp.dot(p.astype(BF16), v_ref[...], preferred_element_type=F32)
        if with_ctx:
            pc = jnp.exp2((sc - m) * c)
            l = l + jnp.sum(pc, axis=-1, keepdims=True)
            o = o + jnp.dot(pc.astype(BF16), cv_ref[...], preferred_element_type=F32)
        o_ref[:, cols] = (o / l).astype(BF16)


ATTN_HEADS_PER_STEP = 1
ATTN_Q_TILE = 256


def _attention(q, k, v, n_batch, seq, ctx_kv):
    tq = ATTN_Q_TILE
    nq = seq // tq
    t = n_batch * seq
    hw = ATTN_HEADS_PER_STEP * HEAD_DIM
    kv_of = GQA_GROUP // ATTN_HEADS_PER_STEP
    q_spec = pl.BlockSpec((tq, hw), lambda b, h, i: (b * nq + i, h))
    kv_spec = pl.BlockSpec((seq, HEAD_DIM), lambda b, h, i: (b, h // kv_of))
    in_specs = [q_spec, kv_spec, kv_spec]
    args = [q, k, v]
    if ctx_kv is not None:
        c_spec = pl.BlockSpec((PAST_LEN, HEAD_DIM), lambda b, h, i: (b, h // kv_of))
        in_specs += [c_spec, c_spec]
        args += list(ctx_kv)
    return pl.pallas_call(
        functools.partial(_attn_kernel, with_ctx=ctx_kv is not None),
        grid=(n_batch, N_HEADS // ATTN_HEADS_PER_STEP, nq),
        in_specs=in_specs,
        out_specs=pl.BlockSpec((tq, hw), lambda b, h, i: (b * nq + i, h)),
        out_shape=jax.ShapeDtypeStruct((t, Q_WIDTH), BF16),
        compiler_params=_params(("parallel", "parallel", "parallel")),
        name="attention_lat" if ctx_kv is not None else "attention_ctx",
    )(*args)


def _outproj_kernel(ac_ref, al_ref, zc_ref, zl_ref, xc_ref, xl_ref, gate_ref, wa_ref, wz_ref, o_ref):
    acc = jnp.dot(_pick_rows(ac_ref, al_ref), wa_ref[...], preferred_element_type=F32)
    acc = acc + jnp.dot(_pick_rows(zc_ref, zl_ref), wz_ref[...], preferred_element_type=F32)
    o_ref[...] = _pick_rows(xc_ref, xl_ref) + gate_ref[0, 0] * acc


def _out_proj(attn, z, x, mods4, k_gate, wa_bf16, wz_bf16):
    tm = ROW_TILE
    return pl.pallas_call(
        _outproj_kernel,
        grid=(T_ALL // tm,),
        in_specs=(_split_row_specs(Q_WIDTH, tm) + _split_row_specs(HYENA_WIDTH, tm) + _split_row_specs(D_MODEL, tm)
                  + [_mod_spec(k_gate, tm), _resident((Q_WIDTH, D_MODEL)), _resident((HYENA_WIDTH, D_MODEL))]),
        out_specs=pl.BlockSpec((tm, D_MODEL), lambda i: (i, 0)),
        out_shape=jax.ShapeDtypeStruct((T_ALL, D_MODEL), F32),
        compiler_params=_params(("parallel",)),
        name="out_proj",
    )(*attn, *z, *x, mods4, wa_bf16, wz_bf16)


def _gelu_tanh(x):
    return 0.5 * x * (1.0 + jnp.tanh(math.sqrt(2.0 / math.pi) * (x + 0.044715 * (x * x * x))))


def _glu_kernel(yfc_ref, yfl_ref, ybc_ref, ybl_ref, u_ref, d_ref, xc_ref, xl_ref, gate_ref, w_ref, o_ref):
    y = (_pick_rows(yfc_ref, yfl_ref) + _pick_rows(ybc_ref, ybl_ref)) + d_ref[...] * u_ref[...]
    ag = jnp.dot(_gelu_tanh(y).astype(BF16), w_ref[...], preferred_element_type=F32)
    a, g = ag[:, :D_MODEL], ag[:, D_MODEL:]
    o_ref[...] = _pick_rows(xc_ref, xl_ref) + gate_ref[0, 0] * (a * jax.nn.sigmoid(g))


def _glu_proj(y_fwd, y_bwd, u, d_skip, x, mods4, k_gate, w_bf16):
    tm = ROW_TILE
    return pl.pallas_call(
        _glu_kernel,
        grid=(T_ALL // tm,),
        in_specs=(_split_row_specs(S5_WIDTH, tm) + _split_row_specs(S5_WIDTH, tm)
                  + [pl.BlockSpec((tm, S5_WIDTH), lambda i: (i, 0)), _resident((1, S5_WIDTH))]
                  + _split_row_specs(D_MODEL, tm)
                  + [_mod_spec(k_gate, tm), _resident((S5_WIDTH, 2 * D_MODEL))]),
        out_specs=pl.BlockSpec((tm, D_MODEL), lambda i: (i, 0)),
        out_shape=jax.ShapeDtypeStruct((T_ALL, D_MODEL), F32),
        compiler_params=_params(("parallel",)),
        name="glu_proj",
    )(*y_fwd, *y_bwd, u, d_skip.reshape(1, S5_WIDTH), *x, mods4, w_bf16)


def _gmm_kernel(be_ref, nreal_ref, slot_ref, next_ref, xs_ref, w13_hbm, w2_hbm, o_ref,
                w13_f, w2_f, w13_b, w2_b, sems):
    i = pl.program_id(0)
    real = i < nreal_ref[0]
    fresh = jnp.logical_and(real, jnp.logical_or(i == 0, be_ref[i] != be_ref[jnp.maximum(i - 1, 0)]))

    def weight_copies(expert, slot):
        return (pltpu.make_async_copy(w13_hbm.at[expert], w13_f.at[slot], sems.at[slot, 0]),
                pltpu.make_async_copy(w2_hbm.at[expert], w2_f.at[slot], sems.at[slot, 1]))

    @pl.when(fresh)
    def _():
        expert, slot, nxt = be_ref[i], slot_ref[i], next_ref[i]

        @pl.when(i == 0)
        def _():
            for cp in weight_copies(expert, slot):
                cp.start()

        for cp in weight_copies(expert, slot):
            cp.wait()
        w13_b[...] = w13_f[slot].astype(BF16)
        w2_b[...] = w2_f[slot].astype(BF16)

        @pl.when(nxt >= 0)
        def _():
            for cp in weight_copies(nxt, 1 - slot):
                cp.start()

    @pl.when(real)
    def _():
        gu = jnp.dot(xs_ref[...], w13_b[...], preferred_element_type=F32)
        hmid = _silu(gu[:, :EXPERT_HIDDEN]) * gu[:, EXPERT_HIDDEN:]
        o_ref[...] = jnp.dot(hmid.astype(BF16), w2_b[...], preferred_element_type=F32).astype(BF16)

    @pl.when(jnp.logical_not(real))
    def _():
        o_ref[...] = jnp.zeros_like(o_ref)


def _grouped_experts(blk_e, n_real, padded, xs, w13, w2):
    n_rows = xs.shape[0]
    n_blocks = n_rows // MOE_BLOCK
    b_iota = jnp.arange(n_blocks)
    first = jnp.concatenate([jnp.ones((1,), jnp.int32), (blk_e[1:] != blk_e[:-1]).astype(jnp.int32)])
    run = jnp.sum(jnp.where(b_iota[:, None] <= b_iota[None, :], first[:, None], 0), axis=0) - 1
    slot = (run % 2).astype(jnp.int32)
    e_iota = jnp.arange(N_EXPERTS)
    later = jnp.logical_and(e_iota[None, :] > e_iota[:, None], padded[None, :] > 0)
    next_e = jnp.min(jnp.where(later, e_iota[None, :], N_EXPERTS), axis=1)
    next_e = jnp.where(next_e == N_EXPERTS, -1, next_e)
    next_blk = jnp.sum(jnp.where(blk_e[:, None] == e_iota[None, :], next_e[None, :], 0), axis=1).astype(jnp.int32)

    def live(i, nreal):
        return jnp.minimum(i, nreal[0] - 1)

    grid_spec = pltpu.PrefetchScalarGridSpec(
        num_scalar_prefetch=4,
        grid=(n_blocks,),
        in_specs=[pl.BlockSpec((MOE_BLOCK, D_MODEL), lambda i, be, nr, sl, nx: (live(i, nr), 0)),
                  pl.BlockSpec(memory_space=pl.ANY), pl.BlockSpec(memory_space=pl.ANY)],
        out_specs=pl.BlockSpec((MOE_BLOCK, D_MODEL), lambda i, be, nr, sl, nx: (i, 0)),
        scratch_shapes=[pltpu.VMEM((2, D_MODEL, 2 * EXPERT_HIDDEN), F32),
                        pltpu.VMEM((2, EXPERT_HIDDEN, D_MODEL), F32),
                        pltpu.VMEM((D_MODEL, 2 * EXPERT_HIDDEN), BF16),
                        pltpu.VMEM((EXPERT_HIDDEN, D_MODEL), BF16),
                        pltpu.SemaphoreType.DMA((2, 2))],
    )
    return pl.pallas_call(
        _gmm_kernel,
        grid_spec=grid_spec,
        out_shape=jax.ShapeDtypeStruct((n_rows, D_MODEL), BF16),
        compiler_params=_params(("arbitrary",)),
        name="grouped_experts",
    )(blk_e, n_real, slot, next_blk, xs, w13, w2)


COMBINE_TILE = 256


def _shared_kernel(h_ref, x_ref, ye_ref, w_ref, gate_ref, w13_ref, w2_ref, o_ref):
    gu = jnp.dot(h_ref[...], w13_ref[...], preferred_element_type=F32)
    hmid = _silu(gu[:, :EXPERT_HIDDEN]) * gu[:, EXPERT_HIDDEN:]
    acc = jnp.dot(hmid.astype(BF16), w2_ref[...], preferred_element_type=F32)
    wts = w_ref[...]
    for k in range(TOP_K):
        acc = acc + wts[:, k:k + 1] * ye_ref[k].astype(F32)
    o_ref[...] = x_ref[...] + gate_ref[0, 0] * acc


def _shared_and_combine(h_bf16, x, row0, t, y_exp, wts, mods4, k_gate, w13_bf16, w2_bf16):
    tm = COMBINE_TILE
    b0 = row0 // tm
    row = lambda i: (i, 0)
    off = lambda i: (b0 + i, 0)
    return pl.pallas_call(
        _shared_kernel,
        grid=(t // tm,),
        in_specs=[pl.BlockSpec((tm, D_MODEL), off), pl.BlockSpec((tm, D_MODEL), off),
                  pl.BlockSpec((TOP_K, tm, D_MODEL), lambda i: (0, i, 0)),
                  pl.BlockSpec((tm, TOP_K), row), _mod_spec(k_gate, tm, row0),
                  _resident((D_MODEL, 2 * EXPERT_HIDDEN)), _resident((EXPERT_HIDDEN, D_MODEL))],
        out_specs=pl.BlockSpec((tm, D_MODEL), row),
        out_shape=jax.ShapeDtypeStruct((t, D_MODEL), F32),
        compiler_params=_params(("parallel",)),
        name="shared_expert_combine",
    )(h_bf16, x, y_exp, wts, mods4, w13_bf16, w2_bf16)


ROUTE_TILE = 256
GROUP_SIZE = N_EXPERTS // N_EXPERT_GROUPS
NEG_INF = float("-inf")


def _first_argmax(v, index, sentinel):
    m = jnp.max(v, axis=0, keepdims=True)
    first = jnp.min(jnp.where(v == m, index, sentinel), axis=0, keepdims=True)
    return m, first


def _router_kernel(x_ref, g_ref, sc_ref, sh_ref, wt_ref, rb_ref, tri_ref,
                   h_out, idx_out, gate_out, rank_out, cnt_out, base_ref):
    tm = x_ref.shape[0]
    step = pl.program_id(0)

    @pl.when(jnp.logical_or(step == 0, step == T_CTX // tm))
    def _():
        base_ref[...] = jnp.zeros_like(base_ref)

    hb = _norm_mod(x_ref[...], g_ref[...], sc_ref[0, 0], sh_ref[0, 0]).astype(BF16)
    h_out[...] = hb
    logits = lax.dot_general(wt_ref[...], hb, (((1,), (1,)), ((), ())), preferred_element_type=F32)
    scores = jax.nn.sigmoid(logits)
    choice = scores + rb_ref[...]

    member = lax.broadcasted_iota(jnp.int32, (GROUP_SIZE, tm), 0).astype(F32)
    group_scores = []
    for g in range(N_EXPERT_GROUPS):
        cg = choice[g * GROUP_SIZE:(g + 1) * GROUP_SIZE, :]
        m1, first = _first_argmax(cg, member, float(GROUP_SIZE))
        m2 = jnp.max(jnp.where(member == first, NEG_INF, cg), axis=0, keepdims=True)
        group_scores.append(m1 + m2)
    gs = jnp.concatenate(group_scores, axis=0)

    group = lax.broadcasted_iota(jnp.int32, (N_EXPERT_GROUPS, tm), 0).astype(F32)
    keep = jnp.zeros((N_EXPERT_GROUPS, tm), F32)
    for _ in range(TOPK_GROUPS):
        _, first = _first_argmax(gs, group, float(N_EXPERT_GROUPS))
        sel = group == first
        keep = jnp.where(sel, 1.0, keep)
        gs = jnp.where(sel, NEG_INF, gs)
    masked = jnp.concatenate(
        [jnp.where(keep[g:g + 1, :] > 0.0, choice[g * GROUP_SIZE:(g + 1) * GROUP_SIZE, :], NEG_INF)
         for g in range(N_EXPERT_GROUPS)], axis=0)

    expert = lax.broadcasted_iota(jnp.int32, (N_EXPERTS, tm), 0).astype(F32)
    onehot = jnp.zeros((N_EXPERTS, tm), F32)
    sels, idx_rows, gate_rows = [], [], []
    for _ in range(TOP_K):
        _, first = _first_argmax(masked, expert, float(N_EXPERTS))
        sel = expert == first
        sels.append(sel)
        idx_rows.append(first)
        gate_rows.append(jnp.sum(jnp.where(sel, scores, 0.0), axis=0, keepdims=True))
        masked = jnp.where(sel, NEG_INF, masked)
        onehot = jnp.where(sel, 1.0, onehot)
    gates = jnp.concatenate(gate_rows, axis=0)
    gate_out[...] = gates / jnp.sum(gates, axis=0, keepdims=True) * ROUTED_SCALE
    idx_out[...] = jnp.concatenate(idx_rows, axis=0).astype(jnp.int32)

    cum = jnp.dot(onehot.astype(BF16), tri_ref[...], preferred_element_type=F32) + base_ref[:, 0:1]
    rank_rows = [jnp.sum(jnp.where(sel, cum, 0.0), axis=0, keepdims=True) - 1.0 for sel in sels]
    rank_out[...] = jnp.concatenate(rank_rows, axis=0).astype(jnp.int32)
    base_ref[...] = jnp.broadcast_to(cum[:, tm - 1:tm], base_ref.shape)
    cnt_out[0] = base_ref[...].astype(jnp.int32)


def _router(x, norm_g, mods4, k_shift, k_scale, router_w_t, router_bias):
    t = T_ALL
    tm = ROUTE_TILE
    n_ctx = T_CTX // tm
    tri = (jnp.arange(tm)[:, None] <= jnp.arange(tm)[None, :]).astype(BF16)
    choice_rows = pl.BlockSpec((TOP_K, tm), lambda i: (0, i))
    return pl.pallas_call(
        _router_kernel,
        grid=(t // tm,),
        in_specs=[pl.BlockSpec((tm, D_MODEL), lambda i: (i, 0)), _resident((1, D_MODEL)),
                  _mod_spec(k_scale, tm), _mod_spec(k_shift, tm),
                  _resident((N_EXPERTS, D_MODEL)), _resident((N_EXPERTS, 1)), _resident((tm, tm))],
        out_specs=[pl.BlockSpec((tm, D_MODEL), lambda i: (i, 0)), choice_rows, choice_rows, choice_rows,
                   pl.BlockSpec((1, N_EXPERTS, 128), lambda i: (jnp.where(i < n_ctx, 0, 1), 0, 0))],
        out_shape=[jax.ShapeDtypeStruct((t, D_MODEL), BF16),
                   jax.ShapeDtypeStruct((TOP_K, t), jnp.int32), jax.ShapeDtypeStruct((TOP_K, t), F32),
                   jax.ShapeDtypeStruct((TOP_K, t), jnp.int32),
                   jax.ShapeDtypeStruct((2, N_EXPERTS, 128), jnp.int32)],
        scratch_shapes=[pltpu.VMEM((N_EXPERTS, 128), F32)],
        compiler_params=_params(("arbitrary",)),
        name="router",
    )(x, norm_g.reshape(1, D_MODEL), mods4, mods4, router_w_t, router_bias.reshape(N_EXPERTS, 1), tri)


def _pos_kernel(idx_ref, rank_ref, off_ref, pos_out):
    tm = idx_ref.shape[1]
    expert = lax.broadcasted_iota(jnp.int32, (N_EXPERTS, tm), 0)
    rows = []
    for k in range(TOP_K):
        sel = expert == idx_ref[k:k + 1, :]
        rows.append(jnp.sum(jnp.where(sel, off_ref[...], 0.0), axis=0, keepdims=True))
    pos_out[...] = jnp.concatenate(rows, axis=0).astype(jnp.int32) + rank_ref[...]


def _positions(idx, rank, offsets):
    t = idx.shape[1]
    tm = 1024
    rows = pl.BlockSpec((TOP_K, tm), lambda i: (0, i))
    return pl.pallas_call(
        _pos_kernel, grid=(t // tm,),
        in_specs=[rows, rows, _resident((N_EXPERTS, 1))],
        out_specs=rows, out_shape=jax.ShapeDtypeStruct((TOP_K, t), jnp.int32),
        compiler_params=_params(("parallel",)), name="expert_positions",
    )(idx, rank, offsets.astype(F32).reshape(N_EXPERTS, 1))


def _moe_stream(x, row0, t, h_bf16, top_idx, gate, rank, counts, mods4, k_gate, exp_w13, exp_w2,
                sh_w13_bf16, sh_w2_bf16):
    padded = (counts + MOE_BLOCK - 1) // MOE_BLOCK * MOE_BLOCK
    e_iota = jnp.arange(N_EXPERTS)
    pend = jnp.sum(jnp.where(e_iota[:, None] <= e_iota[None, :], padded[:, None], 0), axis=0)
    pos = _positions(top_idx, rank, pend - padded)
    n_rows = -(-(t * TOP_K + N_EXPERTS * (MOE_BLOCK - 1)) // MOE_BLOCK) * MOE_BLOCK
    n_blocks = n_rows // MOE_BLOCK
    blk_start = jnp.arange(n_blocks, dtype=pend.dtype) * MOE_BLOCK
    blk_e = jnp.minimum(jnp.sum((pend[None, :] <= blk_start[:, None]).astype(jnp.int32), axis=1), N_EXPERTS - 1)
    n_real = (pend[N_EXPERTS - 1:] // MOE_BLOCK).astype(jnp.int32)
    n_assign = t * TOP_K
    tok = jnp.broadcast_to(jnp.arange(t, dtype=jnp.int32)[None, :], (TOP_K, t))
    _, sorted_tok = lax.sort_key_val(pos.reshape(-1), tok.reshape(-1))
    dense_end = jnp.sum(jnp.where(e_iota[:, None] <= e_iota[None, :], counts[:, None], 0), axis=0)
    shift_blk = ((pend - padded) - (dense_end - counts))[blk_e]
    dense = jnp.arange(n_rows, dtype=jnp.int32) - jnp.repeat(shift_blk, MOE_BLOCK)
    wrapped = (dense & (n_assign - 1)) if n_assign & (n_assign - 1) == 0 else (dense % n_assign)
    xs = h_bf16[sorted_tok[wrapped] + row0]
    ys = _grouped_experts(blk_e, n_real, padded, xs, exp_w13, exp_w2)
    y_exp = ys[pos.reshape(-1)].reshape(TOP_K, t, D_MODEL)
    return _shared_and_combine(h_bf16, x, row0, t, y_exp, gate.T, mods4, k_gate, sh_w13_bf16, sh_w2_bf16)


def _moe_streams(x, norm_g, mods4, k_shift, k_scale, k_gate, router_w, router_bias, exp_w13, exp_w2, sh_w13, sh_w2):
    h_bf16, top_idx, gate, rank, counts = _router(x, norm_g, mods4, k_shift, k_scale, router_w.T.astype(BF16),
                                                  router_bias)
    sh13, sh2 = sh_w13.astype(BF16), sh_w2.astype(BF16)
    outs = []
    for s, (row0, t) in enumerate(((0, T_CTX), (T_CTX, T_LAT))):
        cols = slice(row0, row0 + t)
        outs.append(_moe_stream(x, row0, t, h_bf16, top_idx[:, cols], gate[:, cols], rank[:, cols], counts[s, :, 0],
                                mods4, k_gate, exp_w13, exp_w2, sh13, sh2))
    return tuple(outs)


HY_BLOCK_CTX, HY_TC_CTX = 256, 512
HY_BLOCK_LAT, HY_TC_LAT = 512, 128


def _hyena_filters(L, w1, b1, w2, b2, w3, sin_freq):
    t_norm = jnp.linspace(0.0, 1.0, L, dtype=F32)[:, None]
    omega = 2.0 * math.pi * jnp.arange(L, dtype=F32)[:, None] / L
    bands = jnp.linspace(1e-4, FILTER_BANDS - 1, FILTER_BANDS, dtype=F32)[None, :]
    z = jnp.concatenate([t_norm, jnp.cos(bands * omega), -jnp.sin(bands * omega)], axis=-1)
    h = jnp.sin(sin_freq * (z @ w1 + b1))
    h = jnp.sin(sin_freq * (h @ w2 + b2))
    h = (h @ w3).reshape(L, 2, HYENA_ORDER, HYENA_WIDTH)
    deltas = jnp.abs(jnp.linspace(math.log(DECAY_TARGET) / DECAY_PCT_LONG,
                                  math.log(DECAY_TARGET) / DECAY_PCT_SHORT, HYENA_WIDTH, dtype=F32))
    h = h * jnp.exp(-t_norm * deltas)[:, None, None, :]
    fwd, bwd = h[:, 0], h[:, 1] * (t_norm > 0.0)[:, :, None]
    norm = jnp.sum(jnp.abs(fwd), axis=0, keepdims=True) + jnp.sum(jnp.abs(bwd), axis=0, keepdims=True)
    return fwd / norm, bwd / norm


HY_ACC_ELEMS = 4096


def _dft_matrices(bk):
    k = jnp.arange(bk, dtype=jnp.int32)
    phase = ((2 * k[:, None] + 1) * k[None, :]) % (4 * bk)
    ang = phase.astype(F32) * (math.pi / (2 * bk))
    return jnp.cos(ang), jnp.sin(ang)


def _spectra_kernel(xf_ref, xb_ref, yf_ref, yb_ref, c1_ref, s1_ref, c2_ref, s2_ref, o_ref, *, nb):
    d = pl.program_id(1) - (nb - 1)
    x = jnp.where(d >= 0, xf_ref[0], xb_ref[0]).astype(BF16)
    y = jnp.where(d >= 1, yf_ref[0], yb_ref[0]).astype(BF16)
    cy = jnp.where(d == 0, c1_ref[...], c2_ref[...])
    sy = jnp.where(d == 0, s1_ref[...], s2_ref[...])
    re = jnp.dot(c1_ref[...], x, preferred_element_type=F32) + jnp.dot(cy, y, preferred_element_type=F32)
    im = jnp.dot(sy, y, preferred_element_type=F32) - jnp.dot(s1_ref[...], x, preferred_element_type=F32)
    o_ref[0, 0, 0] = re
    o_ref[0, 0, 1] = jnp.where(d >= 0, im, -im)


def _filter_spectra(fwd, bwd, bk, tc):
    L = fwd.shape[0]
    nb = L // bk
    n_ct = HYENA_WIDTH // tc
    cos, sin = _dft_matrices(bk)
    k = jnp.arange(bk, dtype=jnp.int32)
    phase = ((2 * k[:, None] + 1) * (bk - k[None, :])) % (4 * bk)
    ang = phase.astype(F32) * (math.pi / (2 * bk))
    live = (k[None, :] > 0).astype(F32)
    mats = [m.astype(BF16) for m in (cos, sin, jnp.cos(ang) * live, jnp.sin(ang) * live)]

    def seg(pick):
        return pl.BlockSpec((1, bk, tc), lambda o, di, ct: (pick(di - (nb - 1)), 0, o * n_ct + ct))

    mat = pl.BlockSpec((bk, bk), lambda o, di, ct: (0, 0), pipeline_mode=pl.Buffered(1))
    halves = [h.reshape(nb, bk, HYENA_ORDER * HYENA_WIDTH) for h in (fwd, bwd)]
    return pl.pallas_call(
        functools.partial(_spectra_kernel, nb=nb),
        grid=(HYENA_ORDER, 2 * nb - 1, n_ct),
        in_specs=[seg(lambda d: jnp.maximum(d, 0)), seg(lambda d: jnp.maximum(-d, 0)),
                  seg(lambda d: jnp.maximum(d - 1, 0)), seg(lambda d: jnp.maximum(-d - 1, 0)),
                  mat, mat, mat, mat],
        out_specs=pl.BlockSpec((1, 1, 2, bk, tc), lambda o, di, ct: (o, di, 0, 0, ct)),
        out_shape=jax.ShapeDtypeStruct((HYENA_ORDER, 2 * nb - 1, 2, bk, HYENA_WIDTH), F32),
        compiler_params=_params(("parallel", "parallel", "parallel")),
        name="filter_spectra",
    )(halves[0], halves[1], halves[0], halves[1], *mats)


def _hyena_kernel(v_ref, x1_ref, x2_ref, cw_ref, cb_ref, hb_ref, k0_ref, k1_ref, c_ref, s_ref, ct_ref, st_ref,
                  o_ref, vs_ref, z1_ref, x2s_ref, ur_ref, ui_ref, yr_ref, yi_ref, *, seq, bk, tc):
    nb = seq // bk
    ft = HY_ACC_ELEMS // tc
    row = lax.broadcasted_iota(jnp.int32, (seq, tc), 0)

    def short_conv(x, j):
        prev = jnp.where(row == 0, 0.0, pltpu.roll(x, 1, 0))
        nxt = jnp.where(row == seq - 1, 0.0, pltpu.roll(x, seq - 1, 0))
        return (prev * cw_ref[0, j:j + 1, :] + x * cw_ref[1, j:j + 1, :] + nxt * cw_ref[2, j:j + 1, :]
                + cb_ref[j:j + 1, :])

    vs_ref[...] = short_conv(v_ref[...], 0)
    z1_ref[...] = short_conv(x1_ref[...], 1)
    x2s_ref[...] = short_conv(x2_ref[...], 2)

    def long_conv(src_ref, k_ref, emit):
        for blk in range(nb):
            ub = src_ref[blk * bk:(blk + 1) * bk, :].astype(BF16)
            ur_ref[blk] = jnp.dot(c_ref[...], ub, preferred_element_type=F32)
            ui_ref[blk] = -jnp.dot(s_ref[...], ub, preferred_element_type=F32)
        for out_blk in range(nb):
            def acc_tile(f, carry, out_blk=out_blk):
                rows = pl.ds(pl.multiple_of(f * ft, ft), ft)
                ar = jnp.zeros((ft, tc), F32)
                ai = jnp.zeros((ft, tc), F32)
                for in_blk in range(nb):
                    d = out_blk - in_blk + nb - 1
                    kr, ki = k_ref[d, 0, rows, :], k_ref[d, 1, rows, :]
                    xr, xi = ur_ref[in_blk, rows, :], ui_ref[in_blk, rows, :]
                    ar = ar + (kr * xr - ki * xi)
                    ai = ai + (kr * xi + ki * xr)
                yr_ref[rows, :] = ar
                yi_ref[rows, :] = ai
                return carry

            lax.fori_loop(0, bk // ft, acc_tile, 0)
            y = jnp.dot(ct_ref[...], yr_ref[...].astype(BF16), preferred_element_type=F32)
            y = y - jnp.dot(st_ref[...], yi_ref[...].astype(BF16), preferred_element_type=F32)
            emit(slice(out_blk * bk, (out_blk + 1) * bk), y * (1.0 / bk))

    def emit_z1(rows, y):
        z1_ref[rows, :] = z1_ref[rows, :] * (y + hb_ref[0:1, :] * vs_ref[rows, :])

    def emit_out(rows, y):
        o_ref[rows, :] = (x2s_ref[rows, :] * (y + hb_ref[1:2, :] * z1_ref[rows, :])).astype(BF16)

    long_conv(vs_ref, k0_ref, emit_z1)
    long_conv(z1_ref, k1_ref, emit_out)


def _hyena(proj, row0, n_batch, seq, bk, tc, conv_w, conv_b, hy_bias, spectra, mats):
    nb = seq // bk
    n_ct = HYENA_WIDTH // tc
    col0 = QKV_WIDTH // tc
    per = HYENA_WIDTH // tc
    b0 = row0 // seq

    def data(which):
        return pl.BlockSpec((seq, tc), lambda ct, b: (b0 + b, col0 + which * per + ct))

    const2 = lambda shape: pl.BlockSpec(shape, lambda ct, b: (0, 0), pipeline_mode=pl.Buffered(1))
    kspec = pl.BlockSpec((2 * nb - 1, 2, bk, tc), lambda ct, b: (0, 0, 0, ct), pipeline_mode=pl.Buffered(1))
    cw = conv_w.reshape(3, HYENA_ORDER + 1, HYENA_WIDTH)
    cb = conv_b.reshape(HYENA_ORDER + 1, HYENA_WIDTH)
    return pl.pallas_call(
        functools.partial(_hyena_kernel, seq=seq, bk=bk, tc=tc),
        grid=(n_ct, n_batch),
        in_specs=[data(0), data(1), data(2),
                  pl.BlockSpec((3, HYENA_ORDER + 1, tc), lambda ct, b: (0, 0, ct)),
                  pl.BlockSpec((HYENA_ORDER + 1, tc), lambda ct, b: (0, ct)),
                  pl.BlockSpec((HYENA_ORDER, tc), lambda ct, b: (0, ct)),
                  kspec, kspec, const2((bk, bk)), const2((bk, bk)), const2((bk, bk)), const2((bk, bk))],
        out_specs=pl.BlockSpec((seq, tc), lambda ct, b: (b, ct)),
        out_shape=jax.ShapeDtypeStruct((n_batch * seq, HYENA_WIDTH), BF16),
        scratch_shapes=[pltpu.VMEM((seq, tc), F32), pltpu.VMEM((seq, tc), F32), pltpu.VMEM((seq, tc), F32),
                        pltpu.VMEM((nb, bk, tc), F32), pltpu.VMEM((nb, bk, tc), F32),
                        pltpu.VMEM((bk, tc), F32), pltpu.VMEM((bk, tc), F32)],
        compiler_params=_params(("parallel", "parallel")),
        name="hyena_seq%d" % seq,
    )(proj, proj, proj, cw, cb, hy_bias, spectra[0], spectra[1], *mats)


HY_SPECTRA_TC = 512


def _hyena_operands(L, bk, f_w1, f_b1, f_w2, f_b2, f_w3, sin_freq):
    fwd, bwd = _hyena_filters(L, f_w1, f_b1, f_w2, f_b2, f_w3, sin_freq)
    cos, sin = _dft_matrices(bk)
    mats = tuple(m.astype(BF16) for m in (cos, sin, cos.T, sin.T))
    return _filter_spectra(fwd, bwd, bk, HY_SPECTRA_TC), mats


S5_LANES = S5_GROUPS * S5_STATE
S5_CLUSTER_GROUPS = 16
S5_CLUSTERS = S5_GROUPS // S5_CLUSTER_GROUPS
S5_CLUSTER_CH = S5_CLUSTER_GROUPS * S5_GROUP_CH
S5_CLUSTER_LANES = S5_CLUSTER_GROUPS * S5_STATE
SCAN_SEQS = 8
SCAN_ROWS = 512


def _s5_kernel(u_ref, s0r_ref, s0i_ref, bw_ref, cwr_ref, cwi_ref, lam_ref, y_ref, sfr_ref, sfi_ref,
               sre, sim, car, cai, *, reverse):
    n_seq, tc, _ = u_ref.shape

    @pl.when(pl.program_id(1) == 0)
    def _():
        car[...] = jnp.zeros_like(car)
        cai[...] = jnp.zeros_like(cai)
        car[0:n_seq, :] = s0r_ref[0]
        cai[0:n_seq, :] = s0i_ref[0]

    u = u_ref[...]
    if n_seq < SCAN_SEQS:
        u = jnp.concatenate([u, jnp.zeros((SCAN_SEQS - n_seq, tc, S5_WIDTH), F32)], axis=0)
    ub = jnp.swapaxes(u, 0, 1).reshape(tc * SCAN_SEQS, S5_WIDTH).astype(BF16)
    for k in range(S5_CLUSTERS):
        bu = jnp.dot(ub[:, k * S5_CLUSTER_CH:(k + 1) * S5_CLUSTER_CH], bw_ref[k], preferred_element_type=F32)
        sre[:, k * S5_CLUSTER_LANES:(k + 1) * S5_CLUSTER_LANES] = bu[:, :S5_CLUSTER_LANES]
        sim[:, k * S5_CLUSTER_LANES:(k + 1) * S5_CLUSTER_LANES] = bu[:, S5_CLUSTER_LANES:]

    for k in range(S5_CLUSTERS):
        lanes = slice(k * S5_CLUSTER_LANES, (k + 1) * S5_CLUSTER_LANES)
        lr = jnp.broadcast_to(lam_ref[0:1, lanes], (SCAN_SEQS, S5_CLUSTER_LANES))
        li = jnp.broadcast_to(lam_ref[1:2, lanes], (SCAN_SEQS, S5_CLUSTER_LANES))

        def step(i, state, lanes=lanes, lr=lr, li=li):
            sr, si = state
            t = (tc - 1 - i) if reverse else i
            rows = pl.ds(pl.multiple_of(t * SCAN_SEQS, SCAN_SEQS), SCAN_SEQS)
            nr = (lr * sr - li * si) + sre[rows, lanes]
            ni = (lr * si + li * sr) + sim[rows, lanes]
            sre[rows, lanes] = nr
            sim[rows, lanes] = ni
            return nr, ni

        sr, si = lax.fori_loop(0, tc, step, (car[:, lanes], cai[:, lanes]), unroll=2)
        car[:, lanes] = sr
        cai[:, lanes] = si

    parts = []
    for k in range(S5_CLUSTERS):
        lanes = slice(k * S5_CLUSTER_LANES, (k + 1) * S5_CLUSTER_LANES)
        yk = jnp.dot(sre[:, lanes].astype(BF16), cwr_ref[k], preferred_element_type=F32)
        parts.append(yk + jnp.dot(sim[:, lanes].astype(BF16), cwi_ref[k], preferred_element_type=F32))
    y = jnp.concatenate(parts, axis=1).reshape(tc, SCAN_SEQS, S5_WIDTH)
    y_ref[...] = jnp.swapaxes(y, 0, 1)[0:n_seq]
    sfr_ref[0] = car[0:n_seq, :]
    sfi_ref[0] = cai[0:n_seq, :]


def _s5_direction(u, n_seq, seq, group, s0_re, s0_im, bw, cwr, cwi, lam, reverse):
    tc = SCAN_ROWS // SCAN_SEQS
    nc = seq // tc
    ng = n_seq // group

    def chunk(c):
        return (nc - 1 - c) if reverse else c

    state_spec = pl.BlockSpec((1, group, S5_LANES), lambda g, c: (g, 0, 0))
    const = lambda a: pl.BlockSpec(a.shape, lambda g, c: (0,) * a.ndim, pipeline_mode=pl.Buffered(1))
    y, f_re, f_im = pl.pallas_call(
        functools.partial(_s5_kernel, reverse=reverse),
        grid=(ng, nc),
        in_specs=[pl.BlockSpec((group, tc, S5_WIDTH), lambda g, c: (g, chunk(c), 0)),
                  state_spec, state_spec, const(bw), const(cwr), const(cwi), const(lam)],
        out_specs=[pl.BlockSpec((group, tc, S5_WIDTH), lambda g, c: (g, chunk(c), 0)),
                   state_spec, state_spec],
        out_shape=[jax.ShapeDtypeStruct((n_seq, seq, S5_WIDTH), F32),
                   jax.ShapeDtypeStruct((ng, group, S5_LANES), F32),
                   jax.ShapeDtypeStruct((ng, group, S5_LANES), F32)],
        scratch_shapes=[pltpu.VMEM((SCAN_ROWS, S5_LANES), F32), pltpu.VMEM((SCAN_ROWS, S5_LANES), F32),
                        pltpu.VMEM((SCAN_SEQS, S5_LANES), F32), pltpu.VMEM((SCAN_SEQS, S5_LANES), F32)],
        compiler_params=_params(("parallel", "arbitrary")),
        name="s5_bwd" if reverse else "s5_fwd",
    )(u.reshape(n_seq, seq, S5_WIDTH), s0_re.reshape(ng, group, S5_LANES), s0_im.reshape(ng, group, S5_LANES),
      bw, cwr, cwi, lam)
    return y.reshape(n_seq * seq, S5_WIDTH), f_re.reshape(n_seq, S5_LANES), f_im.reshape(n_seq, S5_LANES)


def _s5_weights(a_re, a_im, log_dt, b_re, b_im, c_re, c_im):
    lam = lax.complex(a_re, a_im)
    lam_bar = jnp.exp(lam * jnp.exp(log_dt)[..., None])
    b_bar = ((lam_bar - 1.0) / lam)[..., None] * lax.complex(b_re, b_im)
    eye = jnp.eye(S5_CLUSTER_GROUPS, dtype=F32)
    out = []
    for d in range(2):
        def cluster_in(w):
            w = w.reshape(S5_CLUSTERS, S5_CLUSTER_GROUPS, S5_STATE, S5_GROUP_CH)
            return jnp.einsum('ab,kapc->kacbp', eye, w).reshape(S5_CLUSTERS, S5_CLUSTER_CH, S5_CLUSTER_LANES)

        def cluster_out(w):
            w = w.reshape(S5_CLUSTERS, S5_CLUSTER_GROUPS, S5_GROUP_CH, S5_STATE)
            return jnp.einsum('ab,kacp->kapbc', eye, w).reshape(S5_CLUSTERS, S5_CLUSTER_LANES, S5_CLUSTER_CH)

        bw = jnp.concatenate([cluster_in(jnp.real(b_bar[d])), cluster_in(jnp.imag(b_bar[d]))], axis=-1)
        cwr = cluster_out(c_re[d])
        cwi = cluster_out(-c_im[d])
        lam_d = lam_bar[d].reshape(S5_LANES)
        lam2 = jnp.stack([jnp.real(lam_d), jnp.imag(lam_d)]).astype(F32)
        out.append((bw.astype(BF16), cwr.astype(BF16), cwi.astype(BF16), lam2))
    return out


def _final_norm_kernel(x_ref, g_ref, o_ref):
    x = x_ref[...]
    ms = jnp.mean(x * x, axis=-1, keepdims=True)
    o_ref[...] = x * lax.rsqrt(ms + RMS_EPS) * g_ref[...]


def _final_norm(x, g, row0, t):
    tm = ROW_TILE
    b0 = row0 // tm
    return pl.pallas_call(
        _final_norm_kernel,
        grid=(t // tm,),
        in_specs=[pl.BlockSpec((tm, D_MODEL), lambda i: (b0 + i, 0)), _resident((1, D_MODEL))],
        out_specs=pl.BlockSpec((tm, D_MODEL), lambda i: (i, 0)),
        out_shape=jax.ShapeDtypeStruct((t, D_MODEL), F32),
        compiler_params=_params(("parallel",)),
        name="final_norm",
    )(x, g.reshape(1, D_MODEL))


def _rope_tables(L):
    n_rows = L // GRID_W
    row_idx = jnp.repeat(jnp.arange(n_rows, dtype=F32), GRID_W)
    col_idx = (jnp.arange(L) % GRID_W).astype(F32)
    inv = ROPE_THETA ** (-jnp.arange(0, ROPE_AXIS_DIM, 2, dtype=F32) / ROPE_AXIS_DIM)
    ang = jnp.concatenate([row_idx[:, None] * inv, col_idx[:, None] * inv], axis=-1)
    cos, sin = jnp.cos(ang), jnp.sin(ang)
    cos_full = jnp.repeat(cos, 2, axis=-1)
    sin_signed = jnp.stack([-sin, sin], axis=-1).reshape(L, HEAD_DIM)
    return cos_full, sin_signed


def kernel(x_prompt, x_sample, cache_k_l0, cache_v_l0, state_s5_re_l1, state_s5_im_l1, c, c_ctx,
           ada_w_l0, ada_b_l0, norm1_l0, norm2_l0,
           w_in_l0, w_out_l0, q_norm_l0, k_norm_l0, hy_conv_w_l0, hy_conv_b_l0,
           hy_ffn_w1_l0, hy_ffn_b1_l0, hy_ffn_w2_l0, hy_ffn_b2_l0, hy_ffn_w3_l0, hy_sin_freq_l0, hy_bias_l0,
           router_l0, router_bias_l0, exp_w13_l0, exp_w2_l0, shared_w13_l0, shared_w2_l0,
           ada_w_l1, ada_b_l1, norm1_l1, norm2_l1,
           w_in_l1, s5_a_re_l1, s5_a_im_l1, s5_log_dt_l1, s5_b_re_l1, s5_b_im_l1,
           s5_c_re_l1, s5_c_im_l1, s5_d_l1, w_glu_l1,
           router_l1, router_bias_l1, exp_w13_l1, exp_w2_l1, shared_w13_l1, shared_w2_l1,
           final_norm):
    x = (x_prompt.reshape(T_CTX, D_MODEL), x_sample.reshape(T_LAT, D_MODEL))
    cond = jnp.concatenate([c_ctx[None, :], c, jnp.zeros((N_COND - 1 - DEC_BATCH, D_MODEL), F32)], axis=0)

    mods = _ada_mods(cond, ada_w_l0, ada_b_l0)
    proj = _norm_mod_matmul(x, norm1_l0, mods, 0, 1, w_in_l0.astype(BF16))

    q_c, k_c, v_c, new_k = _qk_prep(proj, 0, T_CTX, q_norm_l0, k_norm_l0, None)
    attn_c = _attention(q_c, k_c, v_c, BATCH, SEQ, None)
    q_l, k_l, v_l = _qk_prep(proj, T_CTX, T_LAT, q_norm_l0, k_norm_l0, _rope_tables(DEC_SEQ))
    ctx_kv = (cache_k_l0.reshape(DEC_BATCH * PAST_LEN, KV_WIDTH).astype(BF16),
              cache_v_l0.reshape(DEC_BATCH * PAST_LEN, KV_WIDTH).astype(BF16))
    attn_l = _attention(q_l, k_l, v_l, DEC_BATCH, DEC_SEQ, ctx_kv)

    hy_f = (hy_ffn_w1_l0, hy_ffn_b1_l0, hy_ffn_w2_l0, hy_ffn_b2_l0, hy_ffn_w3_l0, hy_sin_freq_l0)
    spec_c, mats_c = _hyena_operands(SEQ, HY_BLOCK_CTX, *hy_f)
    z_c = _hyena(proj, 0, BATCH, SEQ, HY_BLOCK_CTX, HY_TC_CTX, hy_conv_w_l0, hy_conv_b_l0, hy_bias_l0,
                 spec_c, mats_c)
    spec_l, mats_l = _hyena_operands(DEC_SEQ, HY_BLOCK_LAT, *hy_f)
    z_l = _hyena(proj, T_CTX, DEC_BATCH, DEC_SEQ, HY_BLOCK_LAT, HY_TC_LAT, hy_conv_w_l0, hy_conv_b_l0, hy_bias_l0,
                 spec_l, mats_l)

    w_out = w_out_l0.astype(BF16)
    x = _out_proj((attn_c, attn_l), (z_c, z_l), x, mods, 2, w_out[:Q_WIDTH], w_out[Q_WIDTH:])
    x = _moe_streams(x, norm2_l0, mods, 3, 4, 5, router_l0, router_bias_l0, exp_w13_l0, exp_w2_l0,
                     shared_w13_l0, shared_w2_l0)

    new_v = proj[:T_CTX, Q_WIDTH + KV_WIDTH:QKV_WIDTH]

    mods = _ada_mods(cond, ada_w_l1, ada_b_l1)
    u = _norm_mod_matmul(x, norm1_l1, mods, 0, 1, w_in_l1.astype(BF16))
    s5_w = _s5_weights(s5_a_re_l1, s5_a_im_l1, s5_log_dt_l1, s5_b_re_l1, s5_b_im_l1, s5_c_re_l1, s5_c_im_l1)
    zero_state = jnp.zeros((BATCH, S5_LANES), F32)
    u_c, u_l = u[:T_CTX], u[T_CTX:]
    ys, finals = [], []
    for d in range(2):
        y_c, f_re, f_im = _s5_direction(u_c, BATCH, SEQ, SCAN_SEQS, zero_state, zero_state, *s5_w[d],
                                        reverse=d == 1)
        y_l, _, _ = _s5_direction(u_l, DEC_BATCH, DEC_SEQ, DEC_BATCH,
                                  state_s5_re_l1[:, d].reshape(DEC_BATCH, S5_LANES),
                                  state_s5_im_l1[:, d].reshape(DEC_BATCH, S5_LANES), *s5_w[d], reverse=d == 1)
        ys.append((y_c, y_l))
        finals.append((f_re.reshape(BATCH, S5_GROUPS, S5_STATE), f_im.reshape(BATCH, S5_GROUPS, S5_STATE)))
    s_re = jnp.stack([finals[0][0], finals[1][0]], axis=1)
    s_im = jnp.stack([finals[0][1], finals[1][1]], axis=1)
    x = _glu_proj(ys[0], ys[1], u, s5_d_l1, x, mods, 2, w_glu_l1.astype(BF16))
    x_c, x_l = _moe_streams(x, norm2_l1, mods, 3, 4, 5, router_l1, router_bias_l1, exp_w13_l1, exp_w2_l1,
                            shared_w13_l1, shared_w2_l1)

    return (_final_norm(x_c, final_norm, 0, T_CTX).reshape(BATCH, SEQ, D_MODEL),
            _final_norm(x_l, final_norm, 0, T_LAT).reshape(DEC_BATCH, DEC_SEQ, D_MODEL),
            new_k.reshape(BATCH, SEQ, N_KV_HEADS, HEAD_DIM),
            new_v.reshape(BATCH, SEQ, N_KV_HEADS, HEAD_DIM),
            s_re, s_im)
```

```python
import functools
import math

import jax
import jax.numpy as jnp
from jax import lax
from jax.experimental import pallas as pl
from jax.experimental.pallas import tpu as pltpu

F32 = jnp.float32
BF16 = jnp.bfloat16

D_MODEL = 2048
BATCH = 32
SEQ = 256
DEC_BATCH = 4
DEC_SEQ = 4096
PAST_LEN = 256
GRID_W = 64
RMS_EPS = 1e-6
HEAD_DIM = 128
N_HEADS = 8
N_KV_HEADS = 2
GQA_GROUP = N_HEADS // N_KV_HEADS
Q_WIDTH = N_HEADS * HEAD_DIM
KV_WIDTH = N_KV_HEADS * HEAD_DIM
QKV_WIDTH = Q_WIDTH + 2 * KV_WIDTH
ROPE_THETA = 10000.0
ROPE_AXIS_DIM = HEAD_DIM // 2
HYENA_WIDTH = D_MODEL // 2
HYENA_ORDER = 2
FILTER_BANDS = 16
DECAY_TARGET = 1e-2
DECAY_PCT_SHORT = 0.3
DECAY_PCT_LONG = 1.5
S5_WIDTH = D_MODEL // 2
S5_GROUP_CH = 16
S5_GROUPS = S5_WIDTH // S5_GROUP_CH
S5_STATE = 64
N_EXPERTS = 64
TOP_K = 8
N_EXPERT_GROUPS = 8
TOPK_GROUPS = 4
EXPERT_HIDDEN = 512
ROUTED_SCALE = 2.5
MOE_BLOCK = 256

T_CTX = BATCH * SEQ
T_LAT = DEC_BATCH * DEC_SEQ
T_ALL = T_CTX + T_LAT
N_COND = 8

ROW_TILE = 256
V7X_VMEM_LIMIT_BYTES = 56 * 1024 * 1024


def _params(semantics):
    return pltpu.CompilerParams(dimension_semantics=semantics, vmem_limit_bytes=V7X_VMEM_LIMIT_BYTES)


def _cond_row(i, tm, row0=0):
    tok = row0 + i * tm
    return jnp.where(tok < T_CTX, 0, 1 + (tok - T_CTX) // DEC_SEQ)


def _mod_spec(k, tm, row0=0):
    return pl.BlockSpec((1, 1, 1, D_MODEL), lambda i: (_cond_row(i, tm, row0), k, 0, 0))


def _resident(shape):
    nd = len(shape)
    return pl.BlockSpec(shape, lambda i: (0,) * nd, pipeline_mode=pl.Buffered(1))


def _silu(x):
    return x * jax.nn.sigmoid(x)


def _norm_mod(x, g, sc, sh):
    ms = jnp.mean(x * x, axis=-1, keepdims=True)
    return (x * lax.rsqrt(ms + RMS_EPS) * g) * (1.0 + sc) + sh


def _ada_kernel(c_ref, w_ref, b_ref, o_ref):
    a = _silu(c_ref[...]).astype(BF16)
    o_ref[...] = jnp.dot(a, w_ref[...].astype(BF16), preferred_element_type=F32) + b_ref[...]


def _ada_mods(cond, w, b):
    n = w.shape[1]
    tn = 2048
    out = pl.pallas_call(
        _ada_kernel,
        grid=(n // tn,),
        in_specs=[pl.BlockSpec((N_COND, D_MODEL), lambda j: (0, 0)),
                  pl.BlockSpec((D_MODEL, tn), lambda j: (0, j)),
                  pl.BlockSpec((1, tn), lambda j: (0, j))],
        out_specs=pl.BlockSpec((N_COND, tn), lambda j: (0, j)),
        out_shape=jax.ShapeDtypeStruct((N_COND, n), F32),
        compiler_params=_params(("parallel",)),
        name="ada_mods",
    )(cond, w, b.reshape(1, n))
    return out.reshape(N_COND, 6, 1, D_MODEL)


def _split_row_specs(width, tm):
    n_ctx = T_CTX // tm
    return [pl.BlockSpec((tm, width), lambda i: (jnp.minimum(i, n_ctx - 1), 0)),
            pl.BlockSpec((tm, width), lambda i: (jnp.maximum(i - n_ctx, 0), 0))]


def _pick_rows(ctx_ref, lat_ref):
    return jnp.where(pl.program_id(0) < T_CTX // ctx_ref.shape[0], ctx_ref[...], lat_ref[...])


def _nmm_kernel(*refs, split):
    if split:
        xc_ref, xl_ref, g_ref, sc_ref, sh_ref, w_ref, o_ref = refs
        x = _pick_rows(xc_ref, xl_ref)
    else:
        x_ref, g_ref, sc_ref, sh_ref, w_ref, o_ref = refs
        x = x_ref[...]
    h = _norm_mod(x, g_ref[...], sc_ref[0, 0], sh_ref[0, 0])
    o_ref[...] = jnp.dot(h.astype(BF16), w_ref[...], preferred_element_type=F32)


def _norm_mod_matmul(x, g, mods4, k_shift, k_scale, w_bf16):
    n = w_bf16.shape[1]
    tm = ROW_TILE
    split = isinstance(x, tuple)
    x_args = list(x) if split else [x]
    x_specs = _split_row_specs(D_MODEL, tm) if split else [pl.BlockSpec((tm, D_MODEL), lambda i: (i, 0))]
    return pl.pallas_call(
        functools.partial(_nmm_kernel, split=split),
        grid=(T_ALL // tm,),
        in_specs=x_specs + [_resident((1, D_MODEL)), _mod_spec(k_scale, tm), _mod_spec(k_shift, tm),
                            _resident((D_MODEL, n))],
        out_specs=pl.BlockSpec((tm, n), lambda i: (i, 0)),
        out_shape=jax.ShapeDtypeStruct((T_ALL, n), F32),
        compiler_params=_params(("parallel",)),
        name="norm_mod_matmul",
    )(*x_args, g.reshape(1, D_MODEL), mods4, mods4, w_bf16)


def _head_norm(xh, g):
    ms = jnp.mean(xh * xh, axis=-1, keepdims=True)
    return xh * lax.rsqrt(ms + RMS_EPS) * g


def _rope(y, cos, sin_signed):
    lane = lax.broadcasted_iota(jnp.int32, y.shape, 1)
    partner = jnp.where(lane % 2 == 0, pltpu.roll(y, HEAD_DIM - 1, 1), pltpu.roll(y, 1, 1))
    return y * cos + partner * sin_signed


def _qkprep_ctx_kernel(qkv_ref, qg_ref, kg_ref, q_out, kb_out, vb_out, kf_out):
    for h in range(N_HEADS):
        sl = slice(h * HEAD_DIM, (h + 1) * HEAD_DIM)
        q_out[:, sl] = _head_norm(qkv_ref[:, sl], qg_ref[...]).astype(BF16)
    for j in range(N_KV_HEADS):
        k = _head_norm(qkv_ref[:, Q_WIDTH + j * HEAD_DIM:Q_WIDTH + (j + 1) * HEAD_DIM], kg_ref[...])
        kf_out[:, j * HEAD_DIM:(j + 1) * HEAD_DIM] = k
        kb_out[:, j * HEAD_DIM:(j + 1) * HEAD_DIM] = k.astype(BF16)
    vb_out[...] = qkv_ref[:, Q_WIDTH + KV_WIDTH:QKV_WIDTH].astype(BF16)


def _qkprep_lat_kernel(qkv_ref, qg_ref, kg_ref, cos_ref, sin_ref, q_out, kb_out, vb_out):
    cos, sin = cos_ref[...], sin_ref[...]
    for h in range(N_HEADS):
        sl = slice(h * HEAD_DIM, (h + 1) * HEAD_DIM)
        q_out[:, sl] = _rope(_head_norm(qkv_ref[:, sl], qg_ref[...]), cos, sin).astype(BF16)
    for j in range(N_KV_HEADS):
        k = _head_norm(qkv_ref[:, Q_WIDTH + j * HEAD_DIM:Q_WIDTH + (j + 1) * HEAD_DIM], kg_ref[...])
        kb_out[:, j * HEAD_DIM:(j + 1) * HEAD_DIM] = _rope(k, cos, sin).astype(BF16)
    vb_out[...] = qkv_ref[:, Q_WIDTH + KV_WIDTH:QKV_WIDTH].astype(BF16)


def _qk_prep(proj, row0, t, q_norm, k_norm, rope_tabs):
    tm = ROW_TILE
    b0 = row0 // tm
    qkv_spec = pl.BlockSpec((tm, QKV_WIDTH), lambda i: (b0 + i, 0))
    gain = _resident((1, HEAD_DIM))
    outs = [pl.BlockSpec((tm, Q_WIDTH), lambda i: (i, 0)),
            pl.BlockSpec((tm, KV_WIDTH), lambda i: (i, 0)),
            pl.BlockSpec((tm, KV_WIDTH), lambda i: (i, 0))]
    shapes = [jax.ShapeDtypeStruct((t, Q_WIDTH), BF16),
              jax.ShapeDtypeStruct((t, KV_WIDTH), BF16),
              jax.ShapeDtypeStruct((t, KV_WIDTH), BF16)]
    qg, kg = q_norm.reshape(1, HEAD_DIM), k_norm.reshape(1, HEAD_DIM)
    if rope_tabs is None:
        return pl.pallas_call(
            _qkprep_ctx_kernel, grid=(t // tm,),
            in_specs=[qkv_spec, gain, gain],
            out_specs=outs + [pl.BlockSpec((tm, KV_WIDTH), lambda i: (i, 0))],
            out_shape=shapes + [jax.ShapeDtypeStruct((t, KV_WIDTH), F32)],
            compiler_params=_params(("parallel",)), name="qk_prep_ctx",
        )(proj, qg, kg)
    cos, sin = rope_tabs
    nq = DEC_SEQ // tm
    tab = pl.BlockSpec((tm, HEAD_DIM), lambda i: (i % nq, 0))
    return pl.pallas_call(
        _qkprep_lat_kernel, grid=(t // tm,),
        in_specs=[qkv_spec, gain, gain, tab, tab],
        out_specs=outs, out_shape=shapes,
        compiler_params=_params(("parallel",)), name="qk_prep_lat",
    )(proj, qg, kg, cos, sin)


def _attn_kernel(*refs, with_ctx):
    if with_ctx:
        q_ref, k_ref, v_ref, ck_ref, cv_ref, o_ref = refs
    else:
        q_ref, k_ref, v_ref, o_ref = refs
    c = HEAD_DIM ** -0.5 * math.log2(math.e)
    nt = (((1,), (1,)), ((), ()))
    for hh in range(ATTN_HEADS_PER_STEP):
        cols = slice(hh * HEAD_DIM, (hh + 1) * HEAD_DIM)
        q = q_ref[:, cols]
        s = lax.dot_general(q, k_ref[...], nt, preferred_element_type=F32)
        m = jnp.max(s, axis=-1, keepdims=True)
        if with_ctx:
            sc = lax.dot_general(q, ck_ref[...], nt, preferred_element_type=F32)
            m = jnp.maximum(m, jnp.max(sc, axis=-1, keepdims=True))
        p = jnp.exp2((s - m) * c)
        l = jnp.sum(p, axis=-1, keepdims=True)
        o = jnp.dot(p.astype(BF16), v_ref[...], preferred_element_type=F32)
        if with_ctx:
            pc = jnp.exp2((sc - m) * c)
            l = l + jnp.sum(pc, axis=-1, keepdims=True)
            o = o + jnp.dot(pc.astype(BF16), cv_ref[...], preferred_element_type=F32)
        o_ref[:, cols] = (o / l).astype(BF16)


ATTN_HEADS_PER_STEP = 1
ATTN_Q_TILE = 256


def _attention(q, k, v, n_batch, seq, ctx_kv):
    tq = ATTN_Q_TILE
    nq = seq // tq
    t = n_batch * seq
    hw = ATTN_HEADS_PER_STEP * HEAD_DIM
    kv_of = GQA_GROUP // ATTN_HEADS_PER_STEP
    q_spec = pl.BlockSpec((tq, hw), lambda b, h, i: (b * nq + i, h))
    kv_spec = pl.BlockSpec((seq, HEAD_DIM), lambda b, h, i: (b, h // kv_of))
    in_specs = [q_spec, kv_spec, kv_spec]
    args = [q, k, v]
    if ctx_kv is not None:
        c_spec = pl.BlockSpec((PAST_LEN, HEAD_DIM), lambda b, h, i: (b, h // kv_of))
        in_specs += [c_spec, c_spec]
        args += list(ctx_kv)
    return pl.pallas_call(
        functools.partial(_attn_kernel, with_ctx=ctx_kv is not None),
        grid=(n_batch, N_HEADS // ATTN_HEADS_PER_STEP, nq),
        in_specs=in_specs,
        out_specs=pl.BlockSpec((tq, hw), lambda b, h, i: (b * nq + i, h)),
        out_shape=jax.ShapeDtypeStruct((t, Q_WIDTH), BF16),
        compiler_params=_params(("parallel", "parallel", "parallel")),
        name="attention_lat" if ctx_kv is not None else "attention_ctx",
    )(*args)


def _outproj_kernel(ac_ref, al_ref, zc_ref, zl_ref, xc_ref, xl_ref, gate_ref, wa_ref, wz_ref, o_ref):
    acc = jnp.dot(_pick_rows(ac_ref, al_ref), wa_ref[...], preferred_element_type=F32)
    acc = acc + jnp.dot(_pick_rows(zc_ref, zl_ref), wz_ref[...], preferred_element_type=F32)
    o_ref[...] = _pick_rows(xc_ref, xl_ref) + gate_ref[0, 0] * acc


def _out_proj(attn, z, x, mods4, k_gate, wa_bf16, wz_bf16):
    tm = ROW_TILE
    return pl.pallas_call(
        _outproj_kernel,
        grid=(T_ALL // tm,),
        in_specs=(_split_row_specs(Q_WIDTH, tm) + _split_row_specs(HYENA_WIDTH, tm) + _split_row_specs(D_MODEL, tm)
                  + [_mod_spec(k_gate, tm), _resident((Q_WIDTH, D_MODEL)), _resident((HYENA_WIDTH, D_MODEL))]),
        out_specs=pl.BlockSpec((tm, D_MODEL), lambda i: (i, 0)),
        out_shape=jax.ShapeDtypeStruct((T_ALL, D_MODEL), F32),
        compiler_params=_params(("parallel",)),
        name="out_proj",
    )(*attn, *z, *x, mods4, wa_bf16, wz_bf16)


def _gelu_tanh(x):
    return 0.5 * x * (1.0 + jnp.tanh(math.sqrt(2.0 / math.pi) * (x + 0.044715 * (x * x * x))))


def _glu_kernel(yfc_ref, yfl_ref, ybc_ref, ybl_ref, u_ref, d_ref, xc_ref, xl_ref, gate_ref, w_ref, o_ref):
    y = (_pick_rows(yfc_ref, yfl_ref) + _pick_rows(ybc_ref, ybl_ref)) + d_ref[...] * u_ref[...]
    ag = jnp.dot(_gelu_tanh(y).astype(BF16), w_ref[...], preferred_element_type=F32)
    a, g = ag[:, :D_MODEL], ag[:, D_MODEL:]
    o_ref[...] = _pick_rows(xc_ref, xl_ref) + gate_ref[0, 0] * (a * jax.nn.sigmoid(g))


def _glu_proj(y_fwd, y_bwd, u, d_skip, x, mods4, k_gate, w_bf16):
    tm = ROW_TILE
    return pl.pallas_call(
        _glu_kernel,
        grid=(T_ALL // tm,),
        in_specs=(_split_row_specs(S5_WIDTH, tm) + _split_row_specs(S5_WIDTH, tm)
                  + [pl.BlockSpec((tm, S5_WIDTH), lambda i: (i, 0)), _resident((1, S5_WIDTH))]
                  + _split_row_specs(D_MODEL, tm)
                  + [_mod_spec(k_gate, tm), _resident((S5_WIDTH, 2 * D_MODEL))]),
        out_specs=pl.BlockSpec((tm, D_MODEL), lambda i: (i, 0)),
        out_shape=jax.ShapeDtypeStruct((T_ALL, D_MODEL), F32),
        compiler_params=_params(("parallel",)),
        name="glu_proj",
    )(*y_fwd, *y_bwd, u, d_skip.reshape(1, S5_WIDTH), *x, mods4, w_bf16)


def _gmm_kernel(be_ref, nreal_ref, slot_ref, next_ref, xs_ref, w13_hbm, w2_hbm, o_ref,
                w13_f, w2_f, w13_b, w2_b, sems):
    i = pl.program_id(0)
    real = i < nreal_ref[0]
    fresh = jnp.logical_and(real, jnp.logical_or(i == 0, be_ref[i] != be_ref[jnp.maximum(i - 1, 0)]))

    def weight_copies(expert, slot):
        return (pltpu.make_async_copy(w13_hbm.at[expert], w13_f.at[slot], sems.at[slot, 0]),
                pltpu.make_async_copy(w2_hbm.at[expert], w2_f.at[slot], sems.at[slot, 1]))

    @pl.when(fresh)
    def _():
        expert, slot, nxt = be_ref[i], slot_ref[i], next_ref[i]

        @pl.when(i == 0)
        def _():
            for cp in weight_copies(expert, slot):
                cp.start()

        for cp in weight_copies(expert, slot):
            cp.wait()
        w13_b[...] = w13_f[slot].astype(BF16)
        w2_b[...] = w2_f[slot].astype(BF16)

        @pl.when(nxt >= 0)
        def _():
            for cp in weight_copies(nxt, 1 - slot):
                cp.start()

    @pl.when(real)
    def _():
        gu = jnp.dot(xs_ref[...], w13_b[...], preferred_element_type=F32)
        hmid = _silu(gu[:, :EXPERT_HIDDEN]) * gu[:, EXPERT_HIDDEN:]
        o_ref[...] = jnp.dot(hmid.astype(BF16), w2_b[...], preferred_element_type=F32).astype(BF16)

    @pl.when(jnp.logical_not(real))
    def _():
        o_ref[...] = jnp.zeros_like(o_ref)


def _grouped_experts(blk_e, n_real, padded, xs, w13, w2):
    n_rows = xs.shape[0]
    n_blocks = n_rows // MOE_BLOCK
    b_iota = jnp.arange(n_blocks)
    first = jnp.concatenate([jnp.ones((1,), jnp.int32), (blk_e[1:] != blk_e[:-1]).astype(jnp.int32)])
    run = jnp.sum(jnp.where(b_iota[:, None] <= b_iota[None, :], first[:, None], 0), axis=0) - 1
    slot = (run % 2).astype(jnp.int32)
    e_iota = jnp.arange(N_EXPERTS)
    later = jnp.logical_and(e_iota[None, :] > e_iota[:, None], padded[None, :] > 0)
    next_e = jnp.min(jnp.where(later, e_iota[None, :], N_EXPERTS), axis=1)
    next_e = jnp.where(next_e == N_EXPERTS, -1, next_e)
    next_blk = jnp.sum(jnp.where(blk_e[:, None] == e_iota[None, :], next_e[None, :], 0), axis=1).astype(jnp.int32)

    def live(i, nreal):
        return jnp.minimum(i, nreal[0] - 1)

    grid_spec = pltpu.PrefetchScalarGridSpec(
        num_scalar_prefetch=4,
        grid=(n_blocks,),
        in_specs=[pl.BlockSpec((MOE_BLOCK, D_MODEL), lambda i, be, nr, sl, nx: (live(i, nr), 0)),
                  pl.BlockSpec(memory_space=pl.ANY), pl.BlockSpec(memory_space=pl.ANY)],
        out_specs=pl.BlockSpec((MOE_BLOCK, D_MODEL), lambda i, be, nr, sl, nx: (i, 0)),
        scratch_shapes=[pltpu.VMEM((2, D_MODEL, 2 * EXPERT_HIDDEN), F32),
                        pltpu.VMEM((2, EXPERT_HIDDEN, D_MODEL), F32),
                        pltpu.VMEM((D_MODEL, 2 * EXPERT_HIDDEN), BF16),
                        pltpu.VMEM((EXPERT_HIDDEN, D_MODEL), BF16),
                        pltpu.SemaphoreType.DMA((2, 2))],
    )
    return pl.pallas_call(
        _gmm_kernel,
        grid_spec=grid_spec,
        out_shape=jax.ShapeDtypeStruct((n_rows, D_MODEL), BF16),
        compiler_params=_params(("arbitrary",)),
        name="grouped_experts",
    )(blk_e, n_real, slot, next_blk, xs, w13, w2)


COMBINE_TILE = 256


def _shared_kernel(*refs, out_norm):
    if out_norm:
        h_ref, x_ref, ye_ref, w_ref, gate_ref, w13_ref, w2_ref, g_ref, o_ref = refs
    else:
        h_ref, x_ref, ye_ref, w_ref, gate_ref, w13_ref, w2_ref, o_ref = refs
    gu = jnp.dot(h_ref[...], w13_ref[...], preferred_element_type=F32)
    hmid = _silu(gu[:, :EXPERT_HIDDEN]) * gu[:, EXPERT_HIDDEN:]
    acc = jnp.dot(hmid.astype(BF16), w2_ref[...], preferred_element_type=F32)
    wts = w_ref[...]
    for k in range(TOP_K):
        acc = acc + wts[:, k:k + 1] * ye_ref[k].astype(F32)
    x = x_ref[...] + gate_ref[0, 0] * acc
    if out_norm:
        ms = jnp.mean(x * x, axis=-1, keepdims=True)
        x = x * lax.rsqrt(ms + RMS_EPS) * g_ref[...]
    o_ref[...] = x


def _shared_and_combine(h_bf16, x, row0, t, y_exp, wts, mods4, k_gate, w13_bf16, w2_bf16, out_norm_g=None):
    tm = COMBINE_TILE
    b0 = row0 // tm
    row = lambda i: (i, 0)
    off = lambda i: (b0 + i, 0)
    out_norm = out_norm_g is not None
    extra_specs = [_resident((1, D_MODEL))] if out_norm else []
    extra_args = [out_norm_g.reshape(1, D_MODEL)] if out_norm else []
    return pl.pallas_call(
        functools.partial(_shared_kernel, out_norm=out_norm),
        grid=(t // tm,),
        in_specs=[pl.BlockSpec((tm, D_MODEL), off), pl.BlockSpec((tm, D_MODEL), off),
                  pl.BlockSpec((TOP_K, tm, D_MODEL), lambda i: (0, i, 0)),
                  pl.BlockSpec((tm, TOP_K), row), _mod_spec(k_gate, tm, row0),
                  _resident((D_MODEL, 2 * EXPERT_HIDDEN)), _resident((EXPERT_HIDDEN, D_MODEL))] + extra_specs,
        out_specs=pl.BlockSpec((tm, D_MODEL), row),
        out_shape=jax.ShapeDtypeStruct((t, D_MODEL), F32),
        compiler_params=_params(("parallel",)),
        name="shared_expert_combine",
    )(h_bf16, x, y_exp, wts, mods4, w13_bf16, w2_bf16, *extra_args)


ROUTE_TILE = 256
GROUP_SIZE = N_EXPERTS // N_EXPERT_GROUPS
NEG_INF = float("-inf")


def _first_argmax(v, index, sentinel):
    m = jnp.max(v, axis=0, keepdims=True)
    first = jnp.min(jnp.where(v == m, index, sentinel), axis=0, keepdims=True)
    return m, first


def _router_kernel(x_ref, g_ref, sc_ref, sh_ref, wt_ref, rb_ref, tri_ref,
                   h_out, idx_out, gate_out, rank_out, cnt_out, base_ref):
    tm = x_ref.shape[0]
    step = pl.program_id(0)

    @pl.when(jnp.logical_or(step == 0, step == T_CTX // tm))
    def _():
        base_ref[...] = jnp.zeros_like(base_ref)

    hb = _norm_mod(x_ref[...], g_ref[...], sc_ref[0, 0], sh_ref[0, 0]).astype(BF16)
    h_out[...] = hb
    logits = lax.dot_general(wt_ref[...], hb, (((1,), (1,)), ((), ())), preferred_element_type=F32)
    scores = jax.nn.sigmoid(logits)
    choice = scores + rb_ref[...]

    member = lax.broadcasted_iota(jnp.int32, (GROUP_SIZE, tm), 0).astype(F32)
    group_scores = []
    for g in range(N_EXPERT_GROUPS):
        cg = choice[g * GROUP_SIZE:(g + 1) * GROUP_SIZE, :]
        m1, first = _first_argmax(cg, member, float(GROUP_SIZE))
        m2 = jnp.max(jnp.where(member == first, NEG_INF, cg), axis=0, keepdims=True)
        group_scores.append(m1 + m2)
    gs = jnp.concatenate(group_scores, axis=0)

    group = lax.broadcasted_iota(jnp.int32, (N_EXPERT_GROUPS, tm), 0).astype(F32)
    keep = jnp.zeros((N_EXPERT_GROUPS, tm), F32)
    for _ in range(TOPK_GROUPS):
        _, first = _first_argmax(gs, group, float(N_EXPERT_GROUPS))
        sel = group == first
        keep = jnp.where(sel, 1.0, keep)
        gs = jnp.where(sel, NEG_INF, gs)
    masked = jnp.concatenate(
        [jnp.where(keep[g:g + 1, :] > 0.0, choice[g * GROUP_SIZE:(g + 1) * GROUP_SIZE, :], NEG_INF)
         for g in range(N_EXPERT_GROUPS)], axis=0)

    expert = lax.broadcasted_iota(jnp.int32, (N_EXPERTS, tm), 0).astype(F32)
    onehot = jnp.zeros((N_EXPERTS, tm), F32)
    sels, idx_rows, gate_rows = [], [], []
    for _ in range(TOP_K):
        _, first = _first_argmax(masked, expert, float(N_EXPERTS))
        sel = expert == first
        sels.append(sel)
        idx_rows.append(first)
        gate_rows.append(jnp.sum(jnp.where(sel, scores, 0.0), axis=0, keepdims=True))
        masked = jnp.where(sel, NEG_INF, masked)
        onehot = jnp.where(sel, 1.0, onehot)
    gates = jnp.concatenate(gate_rows, axis=0)
    gate_out[...] = gates / jnp.sum(gates, axis=0, keepdims=True) * ROUTED_SCALE
    idx_out[...] = jnp.concatenate(idx_rows, axis=0).astype(jnp.int32)

    cum = jnp.dot(onehot.astype(BF16), tri_ref[...], preferred_element_type=F32) + base_ref[:, 0:1]
    rank_rows = [jnp.sum(jnp.where(sel, cum, 0.0), axis=0, keepdims=True) - 1.0 for sel in sels]
    rank_out[...] = jnp.concatenate(rank_rows, axis=0).astype(jnp.int32)
    base_ref[...] = jnp.broadcast_to(cum[:, tm - 1:tm], base_ref.shape)
    cnt_out[0] = base_ref[...].astype(jnp.int32)


def _router(x, norm_g, mods4, k_shift, k_scale, router_w_t, router_bias):
    t = T_ALL
    tm = ROUTE_TILE
    n_ctx = T_CTX // tm
    tri = (jnp.arange(tm)[:, None] <= jnp.arange(tm)[None, :]).astype(BF16)
    choice_rows = pl.BlockSpec((TOP_K, tm), lambda i: (0, i))
    return pl.pallas_call(
        _router_kernel,
        grid=(t // tm,),
        in_specs=[pl.BlockSpec((tm, D_MODEL), lambda i: (i, 0)), _resident((1, D_MODEL)),
                  _mod_spec(k_scale, tm), _mod_spec(k_shift, tm),
                  _resident((N_EXPERTS, D_MODEL)), _resident((N_EXPERTS, 1)), _resident((tm, tm))],
        out_specs=[pl.BlockSpec((tm, D_MODEL), lambda i: (i, 0)), choice_rows, choice_rows, choice_rows,
                   pl.BlockSpec((1, N_EXPERTS, 128), lambda i: (jnp.where(i < n_ctx, 0, 1), 0, 0))],
        out_shape=[jax.ShapeDtypeStruct((t, D_MODEL), BF16),
                   jax.ShapeDtypeStruct((TOP_K, t), jnp.int32), jax.ShapeDtypeStruct((TOP_K, t), F32),
                   jax.ShapeDtypeStruct((TOP_K, t), jnp.int32),
                   jax.ShapeDtypeStruct((2, N_EXPERTS, 128), jnp.int32)],
        scratch_shapes=[pltpu.VMEM((N_EXPERTS, 128), F32)],
        compiler_params=_params(("arbitrary",)),
        name="router",
    )(x, norm_g.reshape(1, D_MODEL), mods4, mods4, router_w_t, router_bias.reshape(N_EXPERTS, 1), tri)


def _pos_kernel(idx_ref, rank_ref, off_ref, pos_out):
    tm = idx_ref.shape[1]
    expert = lax.broadcasted_iota(jnp.int32, (N_EXPERTS, tm), 0)
    rows = []
    for k in range(TOP_K):
        sel = expert == idx_ref[k:k + 1, :]
        rows.append(jnp.sum(jnp.where(sel, off_ref[...], 0.0), axis=0, keepdims=True))
    pos_out[...] = jnp.concatenate(rows, axis=0).astype(jnp.int32) + rank_ref[...]


def _positions(idx, rank, offsets):
    t = idx.shape[1]
    tm = 1024
    rows = pl.BlockSpec((TOP_K, tm), lambda i: (0, i))
    return pl.pallas_call(
        _pos_kernel, grid=(t // tm,),
        in_specs=[rows, rows, _resident((N_EXPERTS, 1))],
        out_specs=rows, out_shape=jax.ShapeDtypeStruct((TOP_K, t), jnp.int32),
        compiler_params=_params(("parallel",)), name="expert_positions",
    )(idx, rank, offsets.astype(F32).reshape(N_EXPERTS, 1))


def _moe_stream(x, row0, t, h_bf16, top_idx, gate, rank, counts, mods4, k_gate, exp_w13, exp_w2,
                sh_w13_bf16, sh_w2_bf16, out_norm_g):
    padded = (counts + MOE_BLOCK - 1) // MOE_BLOCK * MOE_BLOCK
    e_iota = jnp.arange(N_EXPERTS)
    pend = jnp.sum(jnp.where(e_iota[:, None] <= e_iota[None, :], padded[:, None], 0), axis=0)
    pos = _positions(top_idx, rank, pend - padded)
    n_rows = -(-(t * TOP_K + N_EXPERTS * (MOE_BLOCK - 1)) // MOE_BLOCK) * MOE_BLOCK
    n_blocks = n_rows // MOE_BLOCK
    blk_start = jnp.arange(n_blocks, dtype=pend.dtype) * MOE_BLOCK
    blk_e = jnp.minimum(jnp.sum((pend[None, :] <= blk_start[:, None]).astype(jnp.int32), axis=1), N_EXPERTS - 1)
    n_real = (pend[N_EXPERTS - 1:] // MOE_BLOCK).astype(jnp.int32)
    n_assign = t * TOP_K
    tok = jnp.broadcast_to(jnp.arange(t, dtype=jnp.int32)[None, :], (TOP_K, t))
    _, sorted_tok = lax.sort_key_val(pos.reshape(-1), tok.reshape(-1))
    dense_end = jnp.sum(jnp.where(e_iota[:, None] <= e_iota[None, :], counts[:, None], 0), axis=0)
    shift_blk = ((pend - padded) - (dense_end - counts))[blk_e]
    dense = jnp.arange(n_rows, dtype=jnp.int32) - jnp.repeat(shift_blk, MOE_BLOCK)
    wrapped = (dense & (n_assign - 1)) if n_assign & (n_assign - 1) == 0 else (dense % n_assign)
    xs = h_bf16[sorted_tok[wrapped] + row0]
    ys = _grouped_experts(blk_e, n_real, padded, xs, exp_w13, exp_w2)
    y_exp = ys[pos.reshape(-1)].reshape(TOP_K, t, D_MODEL)
    return _shared_and_combine(h_bf16, x, row0, t, y_exp, gate.T, mods4, k_gate, sh_w13_bf16, sh_w2_bf16,
                               out_norm_g)


def _moe_streams(x, norm_g, mods4, k_shift, k_scale, k_gate, router_w, router_bias, exp_w13, exp_w2, sh_w13, sh_w2,
                 out_norm_g=None):
    h_bf16, top_idx, gate, rank, counts = _router(x, norm_g, mods4, k_shift, k_scale, router_w.T.astype(BF16),
                                                  router_bias)
    sh13, sh2 = sh_w13.astype(BF16), sh_w2.astype(BF16)
    outs = []
    for s, (row0, t) in enumerate(((0, T_CTX), (T_CTX, T_LAT))):
        cols = slice(row0, row0 + t)
        outs.append(_moe_stream(x, row0, t, h_bf16, top_idx[:, cols], gate[:, cols], rank[:, cols], counts[s, :, 0],
                                mods4, k_gate, exp_w13, exp_w2, sh13, sh2, out_norm_g))
    return tuple(outs)


HY_BLOCK_CTX, HY_TC_CTX = 256, 512
HY_BLOCK_LAT, HY_TC_LAT = 512, 128


def _hyena_filters(L, w1, b1, w2, b2, w3, sin_freq):
    t_norm = jnp.linspace(0.0, 1.0, L, dtype=F32)[:, None]
    omega = 2.0 * math.pi * jnp.arange(L, dtype=F32)[:, None] / L
    bands = jnp.linspace(1e-4, FILTER_BANDS - 1, FILTER_BANDS, dtype=F32)[None, :]
    z = jnp.concatenate([t_norm, jnp.cos(bands * omega), -jnp.sin(bands * omega)], axis=-1)
    h = jnp.sin(sin_freq * (z @ w1 + b1))
    h = jnp.sin(sin_freq * (h @ w2 + b2))
    h = (h @ w3).reshape(L, 2, HYENA_ORDER, HYENA_WIDTH)
    deltas = jnp.abs(jnp.linspace(math.log(DECAY_TARGET) / DECAY_PCT_LONG,
                                  math.log(DECAY_TARGET) / DECAY_PCT_SHORT, HYENA_WIDTH, dtype=F32))
    h = h * jnp.exp(-t_norm * deltas)[:, None, None, :]
    fwd, bwd = h[:, 0], h[:, 1] * (t_norm > 0.0)[:, :, None]
    norm = jnp.sum(jnp.abs(fwd), axis=0, keepdims=True) + jnp.sum(jnp.abs(bwd), axis=0, keepdims=True)
    return fwd / norm, bwd / norm


HY_ACC_ELEMS = 4096


def _dft_matrices(bk):
    k = jnp.arange(bk, dtype=jnp.int32)
    phase = ((2 * k[:, None] + 1) * k[None, :]) % (4 * bk)
    ang = phase.astype(F32) * (math.pi / (2 * bk))
    return jnp.cos(ang), jnp.sin(ang)


def _spectra_kernel(xf_ref, xb_ref, yf_ref, yb_ref, c1_ref, s1_ref, c2_ref, s2_ref, o_ref, *, nb):
    d = pl.program_id(1) - (nb - 1)
    x = jnp.where(d >= 0, xf_ref[0], xb_ref[0]).astype(BF16)
    y = jnp.where(d >= 1, yf_ref[0], yb_ref[0]).astype(BF16)
    cy = jnp.where(d == 0, c1_ref[...], c2_ref[...])
    sy = jnp.where(d == 0, s1_ref[...], s2_ref[...])
    re = jnp.dot(c1_ref[...], x, preferred_element_type=F32) + jnp.dot(cy, y, preferred_element_type=F32)
    im = jnp.dot(sy, y, preferred_element_type=F32) - jnp.dot(s1_ref[...], x, preferred_element_type=F32)
    o_ref[0, 0, 0] = re
    o_ref[0, 0, 1] = jnp.where(d >= 0, im, -im)


def _filter_spectra(fwd, bwd, bk, tc):
    L = fwd.shape[0]
    nb = L // bk
    n_ct = HYENA_WIDTH // tc
    cos, sin = _dft_matrices(bk)
    k = jnp.arange(bk, dtype=jnp.int32)
    phase = ((2 * k[:, None] + 1) * (bk - k[None, :])) % (4 * bk)
    ang = phase.astype(F32) * (math.pi / (2 * bk))
    live = (k[None, :] > 0).astype(F32)
    mats = [m.astype(BF16) for m in (cos, sin, jnp.cos(ang) * live, jnp.sin(ang) * live)]

    def seg(pick):
        return pl.BlockSpec((1, bk, tc), lambda o, di, ct: (pick(di - (nb - 1)), 0, o * n_ct + ct))

    mat = pl.BlockSpec((bk, bk), lambda o, di, ct: (0, 0), pipeline_mode=pl.Buffered(1))
    halves = [h.reshape(nb, bk, HYENA_ORDER * HYENA_WIDTH) for h in (fwd, bwd)]
    return pl.pallas_call(
        functools.partial(_spectra_kernel, nb=nb),
        grid=(HYENA_ORDER, 2 * nb - 1, n_ct),
        in_specs=[seg(lambda d: jnp.maximum(d, 0)), seg(lambda d: jnp.maximum(-d, 0)),
                  seg(lambda d: jnp.maximum(d - 1, 0)), seg(lambda d: jnp.maximum(-d - 1, 0)),
                  mat, mat, mat, mat],
        out_specs=pl.BlockSpec((1, 1, 2, bk, tc), lambda o, di, ct: (o, di, 0, 0, ct)),
        out_shape=jax.ShapeDtypeStruct((HYENA_ORDER, 2 * nb - 1, 2, bk, HYENA_WIDTH), F32),
        compiler_params=_params(("parallel", "parallel", "parallel")),
        name="filter_spectra",
    )(halves[0], halves[1], halves[0], halves[1], *mats)


def _hyena_kernel(v_ref, x1_ref, x2_ref, cw_ref, cb_ref, hb_ref, k0_ref, k1_ref, c_ref, s_ref, ct_ref, st_ref,
                  o_ref, vs_ref, z1_ref, x2s_ref, ur_ref, ui_ref, yr_ref, yi_ref, *, seq, bk, tc):
    nb = seq // bk
    ft = HY_ACC_ELEMS // tc
    row = lax.broadcasted_iota(jnp.int32, (seq, tc), 0)

    def short_conv(x, j):
        prev = jnp.where(row == 0, 0.0, pltpu.roll(x, 1, 0))
        nxt = jnp.where(row == seq - 1, 0.0, pltpu.roll(x, seq - 1, 0))
        return (prev * cw_ref[0, j:j + 1, :] + x * cw_ref[1, j:j + 1, :] + nxt * cw_ref[2, j:j + 1, :]
                + cb_ref[j:j + 1, :])

    vs_ref[...] = short_conv(v_ref[...], 0)
    z1_ref[...] = short_conv(x1_ref[...], 1)
    x2s_ref[...] = short_conv(x2_ref[...], 2)

    def long_conv(src_ref, k_ref, emit):
        for blk in range(nb):
            ub = src_ref[blk * bk:(blk + 1) * bk, :].astype(BF16)
            ur_ref[blk] = jnp.dot(c_ref[...], ub, preferred_element_type=F32)
            ui_ref[blk] = -jnp.dot(s_ref[...], ub, preferred_element_type=F32)
        for out_blk in range(nb):
            def acc_tile(f, carry, out_blk=out_blk):
                rows = pl.ds(pl.multiple_of(f * ft, ft), ft)
                ar = jnp.zeros((ft, tc), F32)
                ai = jnp.zeros((ft, tc), F32)
                for in_blk in range(nb):
                    d = out_blk - in_blk + nb - 1
                    kr, ki = k_ref[d, 0, rows, :], k_ref[d, 1, rows, :]
                    xr, xi = ur_ref[in_blk, rows, :], ui_ref[in_blk, rows, :]
                    ar = ar + (kr * xr - ki * xi)
                    ai = ai + (kr * xi + ki * xr)
                yr_ref[rows, :] = ar
                yi_ref[rows, :] = ai
                return carry

            lax.fori_loop(0, bk // ft, acc_tile, 0)
            y = jnp.dot(ct_ref[...], yr_ref[...].astype(BF16), preferred_element_type=F32)
            y = y - jnp.dot(st_ref[...], yi_ref[...].astype(BF16), preferred_element_type=F32)
            emit(slice(out_blk * bk, (out_blk + 1) * bk), y * (1.0 / bk))

    def emit_z1(rows, y):
        z1_ref[rows, :] = z1_ref[rows, :] * (y + hb_ref[0:1, :] * vs_ref[rows, :])

    def emit_out(rows, y):
        o_ref[rows, :] = (x2s_ref[rows, :] * (y + hb_ref[1:2, :] * z1_ref[rows, :])).astype(BF16)

    long_conv(vs_ref, k0_ref, emit_z1)
    long_conv(z1_ref, k1_ref, emit_out)


def _hyena(proj, row0, n_batch, seq, bk, tc, conv_w, conv_b, hy_bias, spectra, mats):
    nb = seq // bk
    n_ct = HYENA_WIDTH // tc
    col0 = QKV_WIDTH // tc
    per = HYENA_WIDTH // tc
    b0 = row0 // seq

    def data(which):
        return pl.BlockSpec((seq, tc), lambda ct, b: (b0 + b, col0 + which * per + ct))

    const2 = lambda shape: pl.BlockSpec(shape, lambda ct, b: (0, 0), pipeline_mode=pl.Buffered(1))
    kspec = pl.BlockSpec((2 * nb - 1, 2, bk, tc), lambda ct, b: (0, 0, 0, ct), pipeline_mode=pl.Buffered(1))
    cw = conv_w.reshape(3, HYENA_ORDER + 1, HYENA_WIDTH)
    cb = conv_b.reshape(HYENA_ORDER + 1, HYENA_WIDTH)
    return pl.pallas_call(
        functools.partial(_hyena_kernel, seq=seq, bk=bk, tc=tc),
        grid=(n_ct, n_batch),
        in_specs=[data(0), data(1), data(2),
                  pl.BlockSpec((3, HYENA_ORDER + 1, tc), lambda ct, b: (0, 0, ct)),
                  pl.BlockSpec((HYENA_ORDER + 1, tc), lambda ct, b: (0, ct)),
                  pl.BlockSpec((HYENA_ORDER, tc), lambda ct, b: (0, ct)),
                  kspec, kspec, const2((bk, bk)), const2((bk, bk)), const2((bk, bk)), const2((bk, bk))],
        out_specs=pl.BlockSpec((seq, tc), lambda ct, b: (b, ct)),
        out_shape=jax.ShapeDtypeStruct((n_batch * seq, HYENA_WIDTH), BF16),
        scratch_shapes=[pltpu.VMEM((seq, tc), F32), pltpu.VMEM((seq, tc), F32), pltpu.VMEM((seq, tc), F32),
                        pltpu.VMEM((nb, bk, tc), F32), pltpu.VMEM((nb, bk, tc), F32),
                        pltpu.VMEM((bk, tc), F32), pltpu.VMEM((bk, tc), F32)],
        compiler_params=_params(("parallel", "parallel")),
        name="hyena_seq%d" % seq,
    )(proj, proj, proj, cw, cb, hy_bias, spectra[0], spectra[1], *mats)


HY_SPECTRA_TC = 512


def _hyena_operands(L, bk, f_w1, f_b1, f_w2, f_b2, f_w3, sin_freq):
    fwd, bwd = _hyena_filters(L, f_w1, f_b1, f_w2, f_b2, f_w3, sin_freq)
    cos, sin = _dft_matrices(bk)
    mats = tuple(m.astype(BF16) for m in (cos, sin, cos.T, sin.T))
    return _filter_spectra(fwd, bwd, bk, HY_SPECTRA_TC), mats


S5_LANES = S5_GROUPS * S5_STATE
S5_CLUSTER_GROUPS = 16
S5_CLUSTERS = S5_GROUPS // S5_CLUSTER_GROUPS
S5_CLUSTER_CH = S5_CLUSTER_GROUPS * S5_GROUP_CH
S5_CLUSTER_LANES = S5_CLUSTER_GROUPS * S5_STATE
SCAN_SEQS = 8
SCAN_ROWS = 512


def _s5_kernel(u_ref, s0r_ref, s0i_ref, bw_ref, cwr_ref, cwi_ref, lam_ref, y_ref, sfr_ref, sfi_ref,
               sre, sim, car, cai, *, reverse):
    n_seq, tc, _ = u_ref.shape

    @pl.when(pl.program_id(1) == 0)
    def _():
        car[...] = jnp.zeros_like(car)
        cai[...] = jnp.zeros_like(cai)
        car[0:n_seq, :] = s0r_ref[0]
        cai[0:n_seq, :] = s0i_ref[0]

    u = u_ref[...]
    if n_seq < SCAN_SEQS:
        u = jnp.concatenate([u, jnp.zeros((SCAN_SEQS - n_seq, tc, S5_WIDTH), F32)], axis=0)
    ub = jnp.swapaxes(u, 0, 1).reshape(tc * SCAN_SEQS, S5_WIDTH).astype(BF16)
    for k in range(S5_CLUSTERS):
        bu = jnp.dot(ub[:, k * S5_CLUSTER_CH:(k + 1) * S5_CLUSTER_CH], bw_ref[k], preferred_element_type=F32)
        sre[:, k * S5_CLUSTER_LANES:(k + 1) * S5_CLUSTER_LANES] = bu[:, :S5_CLUSTER_LANES]
        sim[:, k * S5_CLUSTER_LANES:(k + 1) * S5_CLUSTER_LANES] = bu[:, S5_CLUSTER_LANES:]

    for k in range(S5_CLUSTERS):
        lanes = slice(k * S5_CLUSTER_LANES, (k + 1) * S5_CLUSTER_LANES)
        lr = jnp.broadcast_to(lam_ref[0:1, lanes], (SCAN_SEQS, S5_CLUSTER_LANES))
        li = jnp.broadcast_to(lam_ref[1:2, lanes], (SCAN_SEQS, S5_CLUSTER_LANES))

        def step(i, state, lanes=lanes, lr=lr, li=li):
            sr, si = state
            t = (tc - 1 - i) if reverse else i
            rows = pl.ds(pl.multiple_of(t * SCAN_SEQS, SCAN_SEQS), SCAN_SEQS)
            nr = (lr * sr - li * si) + sre[rows, lanes]
            ni = (lr * si + li * sr) + sim[rows, lanes]
            sre[rows, lanes] = nr
            sim[rows, lanes] = ni
            return nr, ni

        sr, si = lax.fori_loop(0, tc, step, (car[:, lanes], cai[:, lanes]), unroll=2)
        car[:, lanes] = sr
        cai[:, lanes] = si

    parts = []
    for k in range(S5_CLUSTERS):
        lanes = slice(k * S5_CLUSTER_LANES, (k + 1) * S5_CLUSTER_LANES)
        yk = jnp.dot(sre[:, lanes].astype(BF16), cwr_ref[k], preferred_element_type=F32)
        parts.append(yk + jnp.dot(sim[:, lanes].astype(BF16), cwi_ref[k], preferred_element_type=F32))
    y = jnp.concatenate(parts, axis=1).reshape(tc, SCAN_SEQS, S5_WIDTH)
    y_ref[...] = jnp.swapaxes(y, 0, 1)[0:n_seq]
    sfr_ref[0] = car[0:n_seq, :]
    sfi_ref[0] = cai[0:n_seq, :]


def _s5_direction(u, n_seq, seq, group, s0_re, s0_im, bw, cwr, cwi, lam, reverse):
    tc = SCAN_ROWS // SCAN_SEQS
    nc = seq // tc
    ng = n_seq // group

    def chunk(c):
        return (nc - 1 - c) if reverse else c

    state_spec = pl.BlockSpec((1, group, S5_LANES), lambda g, c: (g, 0, 0))
    const = lambda a: pl.BlockSpec(a.shape, lambda g, c: (0,) * a.ndim, pipeline_mode=pl.Buffered(1))
    y, f_re, f_im = pl.pallas_call(
        functools.partial(_s5_kernel, reverse=reverse),
        grid=(ng, nc),
        in_specs=[pl.BlockSpec((group, tc, S5_WIDTH), lambda g, c: (g, chunk(c), 0)),
                  state_spec, state_spec, const(bw), const(cwr), const(cwi), const(lam)],
        out_specs=[pl.BlockSpec((group, tc, S5_WIDTH), lambda g, c: (g, chunk(c), 0)),
                   state_spec, state_spec],
        out_shape=[jax.ShapeDtypeStruct((n_seq, seq, S5_WIDTH), F32),
                   jax.ShapeDtypeStruct((ng, group, S5_LANES), F32),
                   jax.ShapeDtypeStruct((ng, group, S5_LANES), F32)],
        scratch_shapes=[pltpu.VMEM((SCAN_ROWS, S5_LANES), F32), pltpu.VMEM((SCAN_ROWS, S5_LANES), F32),
                        pltpu.VMEM((SCAN_SEQS, S5_LANES), F32), pltpu.VMEM((SCAN_SEQS, S5_LANES), F32)],
        compiler_params=_params(("parallel", "arbitrary")),
        name="s5_bwd" if reverse else "s5_fwd",
    )(u.reshape(n_seq, seq, S5_WIDTH), s0_re.reshape(ng, group, S5_LANES), s0_im.reshape(ng, group, S5_LANES),
      bw, cwr, cwi, lam)
    return y.reshape(n_seq * seq, S5_WIDTH), f_re.reshape(n_seq, S5_LANES), f_im.reshape(n_seq, S5_LANES)


def _s5_weights(a_re, a_im, log_dt, b_re, b_im, c_re, c_im):
    lam = lax.complex(a_re, a_im)
    lam_bar = jnp.exp(lam * jnp.exp(log_dt)[..., None])
    b_bar = ((lam_bar - 1.0) / lam)[..., None] * lax.complex(b_re, b_im)
    eye = jnp.eye(S5_CLUSTER_GROUPS, dtype=F32)
    out = []
    for d in range(2):
        def cluster_in(w):
            w = w.reshape(S5_CLUSTERS, S5_CLUSTER_GROUPS, S5_STATE, S5_GROUP_CH)
            return jnp.einsum('ab,kapc->kacbp', eye, w).reshape(S5_CLUSTERS, S5_CLUSTER_CH, S5_CLUSTER_LANES)

        def cluster_out(w):
            w = w.reshape(S5_CLUSTERS, S5_CLUSTER_GROUPS, S5_GROUP_CH, S5_STATE)
            return jnp.einsum('ab,kacp->kapbc', eye, w).reshape(S5_CLUSTERS, S5_CLUSTER_LANES, S5_CLUSTER_CH)

        bw = jnp.concatenate([cluster_in(jnp.real(b_bar[d])), cluster_in(jnp.imag(b_bar[d]))], axis=-1)
        cwr = cluster_out(c_re[d])
        cwi = cluster_out(-c_im[d])
        lam_d = lam_bar[d].reshape(S5_LANES)
        lam2 = jnp.stack([jnp.real(lam_d), jnp.imag(lam_d)]).astype(F32)
        out.append((bw.astype(BF16), cwr.astype(BF16), cwi.astype(BF16), lam2))
    return out


def _rope_tables(L):
    n_rows = L // GRID_W
    row_idx = jnp.repeat(jnp.arange(n_rows, dtype=F32), GRID_W)
    col_idx = (jnp.arange(L) % GRID_W).astype(F32)
    inv = ROPE_THETA ** (-jnp.arange(0, ROPE_AXIS_DIM, 2, dtype=F32) / ROPE_AXIS_DIM)
    ang = jnp.concatenate([row_idx[:, None] * inv, col_idx[:, None] * inv], axis=-1)
    cos, sin = jnp.cos(ang), jnp.sin(ang)
    cos_full = jnp.repeat(cos, 2, axis=-1)
    sin_signed = jnp.stack([-sin, sin], axis=-1).reshape(L, HEAD_DIM)
    return cos_full, sin_signed


def kernel(x_prompt, x_sample, cache_k_l0, cache_v_l0, state_s5_re_l1, state_s5_im_l1, c, c_ctx,
           ada_w_l0, ada_b_l0, norm1_l0, norm2_l0,
           w_in_l0, w_out_l0, q_norm_l0, k_norm_l0, hy_conv_w_l0, hy_conv_b_l0,
           hy_ffn_w1_l0, hy_ffn_b1_l0, hy_ffn_w2_l0, hy_ffn_b2_l0, hy_ffn_w3_l0, hy_sin_freq_l0, hy_bias_l0,
           router_l0, router_bias_l0, exp_w13_l0, exp_w2_l0, shared_w13_l0, shared_w2_l0,
           ada_w_l1, ada_b_l1, norm1_l1, norm2_l1,
           w_in_l1, s5_a_re_l1, s5_a_im_l1, s5_log_dt_l1, s5_b_re_l1, s5_b_im_l1,
           s5_c_re_l1, s5_c_im_l1, s5_d_l1, w_glu_l1,
           router_l1, router_bias_l1, exp_w13_l1, exp_w2_l1, shared_w13_l1, shared_w2_l1,
           final_norm):
    x = (x_prompt.reshape(T_CTX, D_MODEL), x_sample.reshape(T_LAT, D_MODEL))
    cond = jnp.concatenate([c_ctx[None, :], c, jnp.zeros((N_COND - 1 - DEC_BATCH, D_MODEL), F32)], axis=0)

    mods = _ada_mods(cond, ada_w_l0, ada_b_l0)
    proj = _norm_mod_matmul(x, norm1_l0, mods, 0, 1, w_in_l0.astype(BF16))

    q_c, k_c, v_c, new_k = _qk_prep(proj, 0, T_CTX, q_norm_l0, k_norm_l0, None)
    attn_c = _attention(q_c, k_c, v_c, BATCH, SEQ, None)
    q_l, k_l, v_l = _qk_prep(proj, T_CTX, T_LAT, q_norm_l0, k_norm_l0, _rope_tables(DEC_SEQ))
    ctx_kv = (cache_k_l0.reshape(DEC_BATCH * PAST_LEN, KV_WIDTH).astype(BF16),
              cache_v_l0.reshape(DEC_BATCH * PAST_LEN, KV_WIDTH).astype(BF16))
    attn_l = _attention(q_l, k_l, v_l, DEC_BATCH, DEC_SEQ, ctx_kv)

    hy_f = (hy_ffn_w1_l0, hy_ffn_b1_l0, hy_ffn_w2_l0, hy_ffn_b2_l0, hy_ffn_w3_l0, hy_sin_freq_l0)
    spec_c, mats_c = _hyena_operands(SEQ, HY_BLOCK_CTX, *hy_f)
    z_c = _hyena(proj, 0, BATCH, SEQ, HY_BLOCK_CTX, HY_TC_CTX, hy_conv_w_l0, hy_conv_b_l0, hy_bias_l0,
                 spec_c, mats_c)
    spec_l, mats_l = _hyena_operands(DEC_SEQ, HY_BLOCK_LAT, *hy_f)
    z_l = _hyena(proj, T_CTX, DEC_BATCH, DEC_SEQ, HY_BLOCK_LAT, HY_TC_LAT, hy_conv_w_l0, hy_conv_b_l0, hy_bias_l0,
                 spec_l, mats_l)

    w_out = w_out_l0.astype(BF16)
    x = _out_proj((attn_c, attn_l), (z_c, z_l), x, mods, 2, w_out[:Q_WIDTH], w_out[Q_WIDTH:])
    x = _moe_streams(x, norm2_l0, mods, 3, 4, 5, router_l0, router_bias_l0, exp_w13_l0, exp_w2_l0,
                     shared_w13_l0, shared_w2_l0)

    new_v = proj[:T_CTX, Q_WIDTH + KV_WIDTH:QKV_WIDTH]

    mods = _ada_mods(cond, ada_w_l1, ada_b_l1)
    u = _norm_mod_matmul(x, norm1_l1, mods, 0, 1, w_in_l1.astype(BF16))
    s5_w = _s5_weights(s5_a_re_l1, s5_a_im_l1, s5_log_dt_l1, s5_b_re_l1, s5_b_im_l1, s5_c_re_l1, s5_c_im_l1)
    zero_state = jnp.zeros((BATCH, S5_LANES), F32)
    u_c, u_l = u[:T_CTX], u[T_CTX:]
    ys, finals = [], []
    for d in range(2):
        y_c, f_re, f_im = _s5_direction(u_c, BATCH, SEQ, SCAN_SEQS, zero_state, zero_state, *s5_w[d],
                                        reverse=d == 1)
        y_l, _, _ = _s5_direction(u_l, DEC_BATCH, DEC_SEQ, DEC_BATCH,
                                  state_s5_re_l1[:, d].reshape(DEC_BATCH, S5_LANES),
                                  state_s5_im_l1[:, d].reshape(DEC_BATCH, S5_LANES), *s5_w[d], reverse=d == 1)
        ys.append((y_c, y_l))
        finals.append((f_re.reshape(BATCH, S5_GROUPS, S5_STATE), f_im.reshape(BATCH, S5_GROUPS, S5_STATE)))
    s_re = jnp.stack([finals[0][0], finals[1][0]], axis=1)
    s_im = jnp.stack([finals[0][1], finals[1][1]], axis=1)
    x = _glu_proj(ys[0], ys[1], u, s5_d_l1, x, mods, 2, w_glu_l1.astype(BF16))
    y_c, y_l = _moe_streams(x, norm2_l1, mods, 3, 4, 5, router_l1, router_bias_l1, exp_w13_l1, exp_w2_l1,
                            shared_w13_l1, shared_w2_l1, out_norm_g=final_norm)

    return (y_c.reshape(BATCH, SEQ, D_MODEL),
            y_l.reshape(DEC_BATCH, DEC_SEQ, D_MODEL),
            new_k.reshape(BATCH, SEQ, N_KV_HEADS, HEAD_DIM),
            new_v.reshape(BATCH, SEQ, N_KV_HEADS, HEAD_DIM),
            s_re, s_im)
```

```python
import functools
import math

import jax
import jax.numpy as jnp
from jax import lax
from jax.experimental import pallas as pl
from jax.experimental.pallas import tpu as pltpu

F32 = jnp.float32
BF16 = jnp.bfloat16

D_MODEL = 2048
BATCH = 32
SEQ = 256
DEC_BATCH = 4
DEC_SEQ = 4096
PAST_LEN = 256
GRID_W = 64
RMS_EPS = 1e-6
HEAD_DIM = 128
N_HEADS = 8
N_KV_HEADS = 2
GQA_GROUP = N_HEADS // N_KV_HEADS
Q_WIDTH = N_HEADS * HEAD_DIM
KV_WIDTH = N_KV_HEADS * HEAD_DIM
QKV_WIDTH = Q_WIDTH + 2 * KV_WIDTH
ROPE_THETA = 10000.0
ROPE_AXIS_DIM = HEAD_DIM // 2
HYENA_WIDTH = D_MODEL // 2
HYENA_ORDER = 2
FILTER_BANDS = 16
DECAY_TARGET = 1e-2
DECAY_PCT_SHORT = 0.3
DECAY_PCT_LONG = 1.5
S5_WIDTH = D_MODEL // 2
S5_GROUP_CH = 16
S5_GROUPS = S5_WIDTH // S5_GROUP_CH
S5_STATE = 64
N_EXPERTS = 64
TOP_K = 8
N_EXPERT_GROUPS = 8
TOPK_GROUPS = 4
EXPERT_HIDDEN = 512
ROUTED_SCALE = 2.5
MOE_BLOCK = 256

T_CTX = BATCH * SEQ
T_LAT = DEC_BATCH * DEC_SEQ
T_ALL = T_CTX + T_LAT
N_COND = 8

ROW_TILE = 256
V7X_VMEM_LIMIT_BYTES = 56 * 1024 * 1024


def _params(semantics):
    return pltpu.CompilerParams(dimension_semantics=semantics, vmem_limit_bytes=V7X_VMEM_LIMIT_BYTES)


def _cond_row(i, tm, row0=0):
    tok = row0 + i * tm
    return jnp.where(tok < T_CTX, 0, 1 + (tok - T_CTX) // DEC_SEQ)


def _mod_spec(k, tm, row0=0):
    return pl.BlockSpec((1, 1, 1, D_MODEL), lambda i: (_cond_row(i, tm, row0), k, 0, 0))


def _resident(shape):
    nd = len(shape)
    return pl.BlockSpec(shape, lambda i: (0,) * nd, pipeline_mode=pl.Buffered(1))


def _silu(x):
    return x * jax.nn.sigmoid(x)


def _norm_mod(x, g, sc, sh):
    ms = jnp.mean(x * x, axis=-1, keepdims=True)
    return (x * lax.rsqrt(ms + RMS_EPS) * g) * (1.0 + sc) + sh


def _ada_kernel(c_ref, w_ref, b_ref, o_ref):
    a = _silu(c_ref[...]).astype(BF16)
    o_ref[...] = jnp.dot(a, w_ref[...].astype(BF16), preferred_element_type=F32) + b_ref[...]


def _ada_mods(cond, w, b):
    n = w.shape[1]
    tn = 2048
    out = pl.pallas_call(
        _ada_kernel,
        grid=(n // tn,),
        in_specs=[pl.BlockSpec((N_COND, D_MODEL), lambda j: (0, 0)),
                  pl.BlockSpec((D_MODEL, tn), lambda j: (0, j)),
                  pl.BlockSpec((1, tn), lambda j: (0, j))],
        out_specs=pl.BlockSpec((N_COND, tn), lambda j: (0, j)),
        out_shape=jax.ShapeDtypeStruct((N_COND, n), F32),
        compiler_params=_params(("parallel",)),
        name="ada_mods",
    )(cond, w, b.reshape(1, n))
    return out.reshape(N_COND, 6, 1, D_MODEL)


def _split_row_specs(width, tm):
    n_ctx = T_CTX // tm
    return [pl.BlockSpec((tm, width), lambda i: (jnp.minimum(i, n_ctx - 1), 0)),
            pl.BlockSpec((tm, width), lambda i: (jnp.maximum(i - n_ctx, 0), 0))]


def _pick_rows(ctx_ref, lat_ref):
    return jnp.where(pl.program_id(0) < T_CTX // ctx_ref.shape[0], ctx_ref[...], lat_ref[...])


def _nmm_kernel(*refs, split):
    if split:
        xc_ref, xl_ref, g_ref, sc_ref, sh_ref, w_ref, o_ref = refs
        x = _pick_rows(xc_ref, xl_ref)
    else:
        x_ref, g_ref, sc_ref, sh_ref, w_ref, o_ref = refs
        x = x_ref[...]
    h = _norm_mod(x, g_ref[...], sc_ref[0, 0], sh_ref[0, 0])
    o_ref[...] = jnp.dot(h.astype(BF16), w_ref[...], preferred_element_type=F32)


def _norm_mod_matmul(x, g, mods4, k_shift, k_scale, w_bf16):
    n = w_bf16.shape[1]
    tm = ROW_TILE
    split = isinstance(x, tuple)
    x_args = list(x) if split else [x]
    x_specs = _split_row_specs(D_MODEL, tm) if split else [pl.BlockSpec((tm, D_MODEL), lambda i: (i, 0))]
    return pl.pallas_call(
        functools.partial(_nmm_kernel, split=split),
        grid=(T_ALL // tm,),
        in_specs=x_specs + [_resident((1, D_MODEL)), _mod_spec(k_scale, tm), _mod_spec(k_shift, tm),
                            _resident((D_MODEL, n))],
        out_specs=pl.BlockSpec((tm, n), lambda i: (i, 0)),
        out_shape=jax.ShapeDtypeStruct((T_ALL, n), F32),
        compiler_params=_params(("parallel",)),
        name="norm_mod_matmul",
    )(*x_args, g.reshape(1, D_MODEL), mods4, mods4, w_bf16)


def _head_norm(xh, g):
    ms = jnp.mean(xh * xh, axis=-1, keepdims=True)
    return xh * lax.rsqrt(ms + RMS_EPS) * g


def _rope(y, cos, sin_signed):
    lane = lax.broadcasted_iota(jnp.int32, y.shape, 1)
    partner = jnp.where(lane % 2 == 0, pltpu.roll(y, HEAD_DIM - 1, 1), pltpu.roll(y, 1, 1))
    return y * cos + partner * sin_signed


def _qkprep_ctx_kernel(qkv_ref, qg_ref, kg_ref, q_out, kb_out, vb_out, kf_out):
    for h in range(N_HEADS):
        sl = slice(h * HEAD_DIM, (h + 1) * HEAD_DIM)
        q_out[:, sl] = _head_norm(qkv_ref[:, sl], qg_ref[...]).astype(BF16)
    for j in range(N_KV_HEADS):
        k = _head_norm(qkv_ref[:, Q_WIDTH + j * HEAD_DIM:Q_WIDTH + (j + 1) * HEAD_DIM], kg_ref[...])
        kf_out[:, j * HEAD_DIM:(j + 1) * HEAD_DIM] = k
        kb_out[:, j * HEAD_DIM:(j + 1) * HEAD_DIM] = k.astype(BF16)
    vb_out[...] = qkv_ref[:, Q_WIDTH + KV_WIDTH:QKV_WIDTH].astype(BF16)


def _qkprep_lat_kernel(qkv_ref, qg_ref, kg_ref, cos_ref, sin_ref, q_out, kb_out, vb_out):
    cos, sin = cos_ref[...], sin_ref[...]
    for h in range(N_HEADS):
        sl = slice(h * HEAD_DIM, (h + 1) * HEAD_DIM)
        q_out[:, sl] = _rope(_head_norm(qkv_ref[:, sl], qg_ref[...]), cos, sin).astype(BF16)
    for j in range(N_KV_HEADS):
        k = _head_norm(qkv_ref[:, Q_WIDTH + j * HEAD_DIM:Q_WIDTH + (j + 1) * HEAD_DIM], kg_ref[...])
        kb_out[:, j * HEAD_DIM:(j + 1) * HEAD_DIM] = _rope(k, cos, sin).astype(BF16)
    vb_out[...] = qkv_ref[:, Q_WIDTH + KV_WIDTH:QKV_WIDTH].astype(BF16)


def _qk_prep(proj, row0, t, q_norm, k_norm, rope_tabs):
    tm = ROW_TILE
    b0 = row0 // tm
    qkv_spec = pl.BlockSpec((tm, QKV_WIDTH), lambda i: (b0 + i, 0))
    gain = _resident((1, HEAD_DIM))
    outs = [pl.BlockSpec((tm, Q_WIDTH), lambda i: (i, 0)),
            pl.BlockSpec((tm, KV_WIDTH), lambda i: (i, 0)),
            pl.BlockSpec((tm, KV_WIDTH), lambda i: (i, 0))]
    shapes = [jax.ShapeDtypeStruct((t, Q_WIDTH), BF16),
              jax.ShapeDtypeStruct((t, KV_WIDTH), BF16),
              jax.ShapeDtypeStruct((t, KV_WIDTH), BF16)]
    qg, kg = q_norm.reshape(1, HEAD_DIM), k_norm.reshape(1, HEAD_DIM)
    if rope_tabs is None:
        return pl.pallas_call(
            _qkprep_ctx_kernel, grid=(t // tm,),
            in_specs=[qkv_spec, gain, gain],
            out_specs=outs + [pl.BlockSpec((tm, KV_WIDTH), lambda i: (i, 0))],
            out_shape=shapes + [jax.ShapeDtypeStruct((t, KV_WIDTH), F32)],
            compiler_params=_params(("parallel",)), name="qk_prep_ctx",
        )(proj, qg, kg)
    cos, sin = rope_tabs
    nq = DEC_SEQ // tm
    tab = pl.BlockSpec((tm, HEAD_DIM), lambda i: (i % nq, 0))
    return pl.pallas_call(
        _qkprep_lat_kernel, grid=(t // tm,),
        in_specs=[qkv_spec, gain, gain, tab, tab],
        out_specs=outs, out_shape=shapes,
        compiler_params=_params(("parallel",)), name="qk_prep_lat",
    )(proj, qg, kg, cos, sin)


def _attn_kernel(*refs, with_ctx):
    if with_ctx:
        q_ref, k_ref, v_ref, ck_ref, cv_ref, o_ref = refs
    else:
        q_ref, k_ref, v_ref, o_ref = refs
    c = HEAD_DIM ** -0.5 * math.log2(math.e)
    nt = (((1,), (1,)), ((), ()))
    for hh in range(ATTN_HEADS_PER_STEP):
        cols = slice(hh * HEAD_DIM, (hh + 1) * HEAD_DIM)
        q = q_ref[:, cols]
        s = lax.dot_general(q, k_ref[...], nt, preferred_element_type=F32)
        m = jnp.max(s, axis=-1, keepdims=True)
        if with_ctx:
            sc = lax.dot_general(q, ck_ref[...], nt, preferred_element_type=F32)
            m = jnp.maximum(m, jnp.max(sc, axis=-1, keepdims=True))
        p = jnp.exp2((s - m) * c)
        l = jnp.sum(p, axis=-1, keepdims=True)
        o = jnp.dot(p.astype(BF16), v_ref[...], preferred_element_type=F32)
        if with_ctx:
            pc = jnp.exp2((sc - m) * c)
            l = l + jnp.sum(pc, axis=-1, keepdims=True)
            o = o + jnp.dot(pc.astype(BF16), cv_ref[...], preferred_element_type=F32)
        o_ref[:, cols] = (o / l).astype(BF16)


ATTN_HEADS_PER_STEP = 1
ATTN_Q_TILE = 256


def _attention(q, k, v, n_batch, seq, ctx_kv):
    tq = ATTN_Q_TILE
    nq = seq // tq
    t = n_batch * seq
    hw = ATTN_HEADS_PER_STEP * HEAD_DIM
    kv_of = GQA_GROUP // ATTN_HEADS_PER_STEP
    q_spec = pl.BlockSpec((tq, hw), lambda b, h, i: (b * nq + i, h))
    kv_spec = pl.BlockSpec((seq, HEAD_DIM), lambda b, h, i: (b, h // kv_of))
    in_specs = [q_spec, kv_spec, kv_spec]
    args = [q, k, v]
    if ctx_kv is not None:
        c_spec = pl.BlockSpec((PAST_LEN, HEAD_DIM), lambda b, h, i: (b, h // kv_of))
        in_specs += [c_spec, c_spec]
        args += list(ctx_kv)
    return pl.pallas_call(
        functools.partial(_attn_kernel, with_ctx=ctx_kv is not None),
        grid=(n_batch, N_HEADS // ATTN_HEADS_PER_STEP, nq),
        in_specs=in_specs,
        out_specs=pl.BlockSpec((tq, hw), lambda b, h, i: (b * nq + i, h)),
        out_shape=jax.ShapeDtypeStruct((t, Q_WIDTH), BF16),
        compiler_params=_params(("parallel", "parallel", "parallel")),
        name="attention_lat" if ctx_kv is not None else "attention_ctx",
    )(*args)


def _outproj_kernel(ac_ref, al_ref, zc_ref, zl_ref, xc_ref, xl_ref, gate_ref, wa_ref, wz_ref, o_ref):
    acc = jnp.dot(_pick_rows(ac_ref, al_ref), wa_ref[...], preferred_element_type=F32)
    acc = acc + jnp.dot(_pick_rows(zc_ref, zl_ref), wz_ref[...], preferred_element_type=F32)
    o_ref[...] = _pick_rows(xc_ref, xl_ref) + gate_ref[0, 0] * acc


def _out_proj(attn, z, x, mods4, k_gate, wa_bf16, wz_bf16):
    tm = ROW_TILE
    return pl.pallas_call(
        _outproj_kernel,
        grid=(T_ALL // tm,),
        in_specs=(_split_row_specs(Q_WIDTH, tm) + _split_row_specs(HYENA_WIDTH, tm) + _split_row_specs(D_MODEL, tm)
                  + [_mod_spec(k_gate, tm), _resident((Q_WIDTH, D_MODEL)), _resident((HYENA_WIDTH, D_MODEL))]),
        out_specs=pl.BlockSpec((tm, D_MODEL), lambda i: (i, 0)),
        out_shape=jax.ShapeDtypeStruct((T_ALL, D_MODEL), F32),
        compiler_params=_params(("parallel",)),
        name="out_proj",
    )(*attn, *z, *x, mods4, wa_bf16, wz_bf16)


def _gelu_tanh(x):
    return 0.5 * x * (1.0 + jnp.tanh(math.sqrt(2.0 / math.pi) * (x + 0.044715 * (x * x * x))))


def _glu_kernel(yfc_ref, yfl_ref, ybc_ref, ybl_ref, uc_ref, ul_ref, d_ref, xc_ref, xl_ref, gate_ref, w_ref, o_ref):
    y = (_pick_rows(yfc_ref, yfl_ref) + _pick_rows(ybc_ref, ybl_ref)) + d_ref[...] * _pick_rows(uc_ref, ul_ref)
    ag = jnp.dot(_gelu_tanh(y).astype(BF16), w_ref[...], preferred_element_type=F32)
    a, g = ag[:, :D_MODEL], ag[:, D_MODEL:]
    o_ref[...] = _pick_rows(xc_ref, xl_ref) + gate_ref[0, 0] * (a * jax.nn.sigmoid(g))


def _glu_proj(y_fwd, y_bwd, u, d_skip, x, mods4, k_gate, w_bf16):
    tm = ROW_TILE
    return pl.pallas_call(
        _glu_kernel,
        grid=(T_ALL // tm,),
        in_specs=(_split_row_specs(S5_WIDTH, tm) + _split_row_specs(S5_WIDTH, tm) + _split_row_specs(S5_WIDTH, tm)
                  + [_resident((1, S5_WIDTH))]
                  + _split_row_specs(D_MODEL, tm)
                  + [_mod_spec(k_gate, tm), _resident((S5_WIDTH, 2 * D_MODEL))]),
        out_specs=pl.BlockSpec((tm, D_MODEL), lambda i: (i, 0)),
        out_shape=jax.ShapeDtypeStruct((T_ALL, D_MODEL), F32),
        compiler_params=_params(("parallel",)),
        name="glu_proj",
    )(*y_fwd, *y_bwd, *u, d_skip.reshape(1, S5_WIDTH), *x, mods4, w_bf16)


def _gmm_kernel(be_ref, nreal_ref, slot_ref, next_ref, xs_ref, w13_hbm, w2_hbm, o_ref,
                w13_f, w2_f, w13_b, w2_b, sems):
    i = pl.program_id(0)
    real = i < nreal_ref[0]
    fresh = jnp.logical_and(real, jnp.logical_or(i == 0, be_ref[i] != be_ref[jnp.maximum(i - 1, 0)]))

    def weight_copies(expert, slot):
        return (pltpu.make_async_copy(w13_hbm.at[expert], w13_f.at[slot], sems.at[slot, 0]),
                pltpu.make_async_copy(w2_hbm.at[expert], w2_f.at[slot], sems.at[slot, 1]))

    @pl.when(fresh)
    def _():
        expert, slot, nxt = be_ref[i], slot_ref[i], next_ref[i]

        @pl.when(i == 0)
        def _():
            for cp in weight_copies(expert, slot):
                cp.start()

        for cp in weight_copies(expert, slot):
            cp.wait()
        w13_b[...] = w13_f[slot].astype(BF16)
        w2_b[...] = w2_f[slot].astype(BF16)

        @pl.when(nxt >= 0)
        def _():
            for cp in weight_copies(nxt, 1 - slot):
                cp.start()

    @pl.when(real)
    def _():
        gu = jnp.dot(xs_ref[...], w13_b[...], preferred_element_type=F32)
        hmid = _silu(gu[:, :EXPERT_HIDDEN]) * gu[:, EXPERT_HIDDEN:]
        o_ref[...] = jnp.dot(hmid.astype(BF16), w2_b[...], preferred_element_type=F32).astype(BF16)

    @pl.when(jnp.logical_not(real))
    def _():
        o_ref[...] = jnp.zeros_like(o_ref)


def _grouped_experts(blk_e, n_real, padded, xs, w13, w2):
    n_rows = xs.shape[0]
    n_blocks = n_rows // MOE_BLOCK
    b_iota = jnp.arange(n_blocks)
    first = jnp.concatenate([jnp.ones((1,), jnp.int32), (blk_e[1:] != blk_e[:-1]).astype(jnp.int32)])
    run = jnp.sum(jnp.where(b_iota[:, None] <= b_iota[None, :], first[:, None], 0), axis=0) - 1
    slot = (run % 2).astype(jnp.int32)
    e_iota = jnp.arange(N_EXPERTS)
    later = jnp.logical_and(e_iota[None, :] > e_iota[:, None], padded[None, :] > 0)
    next_e = jnp.min(jnp.where(later, e_iota[None, :], N_EXPERTS), axis=1)
    next_e = jnp.where(next_e == N_EXPERTS, -1, next_e)
    next_blk = jnp.sum(jnp.where(blk_e[:, None] == e_iota[None, :], next_e[None, :], 0), axis=1).astype(jnp.int32)

    def live(i, nreal):
        return jnp.minimum(i, nreal[0] - 1)

    grid_spec = pltpu.PrefetchScalarGridSpec(
        num_scalar_prefetch=4,
        grid=(n_blocks,),
        in_specs=[pl.BlockSpec((MOE_BLOCK, D_MODEL), lambda i, be, nr, sl, nx: (live(i, nr), 0)),
                  pl.BlockSpec(memory_space=pl.ANY), pl.BlockSpec(memory_space=pl.ANY)],
        out_specs=pl.BlockSpec((MOE_BLOCK, D_MODEL), lambda i, be, nr, sl, nx: (i, 0)),
        scratch_shapes=[pltpu.VMEM((2, D_MODEL, 2 * EXPERT_HIDDEN), F32),
                        pltpu.VMEM((2, EXPERT_HIDDEN, D_MODEL), F32),
                        pltpu.VMEM((D_MODEL, 2 * EXPERT_HIDDEN), BF16),
                        pltpu.VMEM((EXPERT_HIDDEN, D_MODEL), BF16),
                        pltpu.SemaphoreType.DMA((2, 2))],
    )
    return pl.pallas_call(
        _gmm_kernel,
        grid_spec=grid_spec,
        out_shape=jax.ShapeDtypeStruct((n_rows, D_MODEL), BF16),
        compiler_params=_params(("arbitrary",)),
        name="grouped_experts",
    )(blk_e, n_real, slot, next_blk, xs, w13, w2)


COMBINE_TILE = 256


def _shared_kernel(*refs, tail):
    h_ref, x_ref, ye_ref, w_ref, gate_ref, w13_ref, w2_ref = refs[:7]
    gu = jnp.dot(h_ref[...], w13_ref[...], preferred_element_type=F32)
    hmid = _silu(gu[:, :EXPERT_HIDDEN]) * gu[:, EXPERT_HIDDEN:]
    acc = jnp.dot(hmid.astype(BF16), w2_ref[...], preferred_element_type=F32)
    wts = w_ref[...]
    for k in range(TOP_K):
        acc = acc + wts[:, k:k + 1] * ye_ref[k].astype(F32)
    x = x_ref[...] + gate_ref[0, 0] * acc
    if tail == "norm":
        g_ref, o_ref = refs[7:]
        ms = jnp.mean(x * x, axis=-1, keepdims=True)
        o_ref[...] = x * lax.rsqrt(ms + RMS_EPS) * g_ref[...]
    elif tail == "proj":
        g_ref, sc_ref, sh_ref, wn_ref, o_ref, u_ref = refs[7:]
        o_ref[...] = x
        hn = _norm_mod(x, g_ref[...], sc_ref[0, 0], sh_ref[0, 0])
        u_ref[...] = jnp.dot(hn.astype(BF16), wn_ref[...], preferred_element_type=F32)
    else:
        o_ref, = refs[7:]
        o_ref[...] = x


def _shared_and_combine(h_bf16, x, row0, t, y_exp, wts, mods4, k_gate, w13_bf16, w2_bf16, out_norm_g=None,
                        next_proj=None):
    tm = COMBINE_TILE
    b0 = row0 // tm
    row = lambda i: (i, 0)
    off = lambda i: (b0 + i, 0)
    out_specs = pl.BlockSpec((tm, D_MODEL), row)
    out_shape = jax.ShapeDtypeStruct((t, D_MODEL), F32)
    tail, extra_specs, extra_args = "plain", [], []
    if out_norm_g is not None:
        tail, extra_specs, extra_args = "norm", [_resident((1, D_MODEL))], [out_norm_g.reshape(1, D_MODEL)]
    elif next_proj is not None:
        g_n, mods_n, k_shift_n, k_scale_n, w_n = next_proj
        n = w_n.shape[1]
        tail = "proj"
        extra_specs = [_resident((1, D_MODEL)), _mod_spec(k_scale_n, tm, row0), _mod_spec(k_shift_n, tm, row0),
                       _resident((D_MODEL, n))]
        extra_args = [g_n.reshape(1, D_MODEL), mods_n, mods_n, w_n]
        out_specs = [out_specs, pl.BlockSpec((tm, n), row)]
        out_shape = [out_shape, jax.ShapeDtypeStruct((t, n), F32)]
    return pl.pallas_call(
        functools.partial(_shared_kernel, tail=tail),
        grid=(t // tm,),
        in_specs=[pl.BlockSpec((tm, D_MODEL), off), pl.BlockSpec((tm, D_MODEL), off),
                  pl.BlockSpec((TOP_K, tm, D_MODEL), lambda i: (0, i, 0)),
                  pl.BlockSpec((tm, TOP_K), row), _mod_spec(k_gate, tm, row0),
                  _resident((D_MODEL, 2 * EXPERT_HIDDEN)), _resident((EXPERT_HIDDEN, D_MODEL))] + extra_specs,
        out_specs=out_specs,
        out_shape=out_shape,
        compiler_params=_params(("parallel",)),
        name="shared_expert_combine",
    )(h_bf16, x, y_exp, wts, mods4, w13_bf16, w2_bf16, *extra_args)


ROUTE_TILE = 256
GROUP_SIZE = N_EXPERTS // N_EXPERT_GROUPS
NEG_INF = float("-inf")


def _first_argmax(v, index, sentinel):
    m = jnp.max(v, axis=0, keepdims=True)
    first = jnp.min(jnp.where(v == m, index, sentinel), axis=0, keepdims=True)
    return m, first


def _router_kernel(x_ref, g_ref, sc_ref, sh_ref, wt_ref, rb_ref, tri_ref,
                   h_out, idx_out, gate_out, rank_out, cnt_out, base_ref):
    tm = x_ref.shape[0]
    step = pl.program_id(0)

    @pl.when(jnp.logical_or(step == 0, step == T_CTX // tm))
    def _():
        base_ref[...] = jnp.zeros_like(base_ref)

    hb = _norm_mod(x_ref[...], g_ref[...], sc_ref[0, 0], sh_ref[0, 0]).astype(BF16)
    h_out[...] = hb
    logits = lax.dot_general(wt_ref[...], hb, (((1,), (1,)), ((), ())), preferred_element_type=F32)
    scores = jax.nn.sigmoid(logits)
    choice = scores + rb_ref[...]

    member = lax.broadcasted_iota(jnp.int32, (GROUP_SIZE, tm), 0).astype(F32)
    group_scores = []
    for g in range(N_EXPERT_GROUPS):
        cg = choice[g * GROUP_SIZE:(g + 1) * GROUP_SIZE, :]
        m1, first = _first_argmax(cg, member, float(GROUP_SIZE))
        m2 = jnp.max(jnp.where(member == first, NEG_INF, cg), axis=0, keepdims=True)
        group_scores.append(m1 + m2)
    gs = jnp.concatenate(group_scores, axis=0)

    group = lax.broadcasted_iota(jnp.int32, (N_EXPERT_GROUPS, tm), 0).astype(F32)
    keep = jnp.zeros((N_EXPERT_GROUPS, tm), F32)
    for _ in range(TOPK_GROUPS):
        _, first = _first_argmax(gs, group, float(N_EXPERT_GROUPS))
        sel = group == first
        keep = jnp.where(sel, 1.0, keep)
        gs = jnp.where(sel, NEG_INF, gs)
    masked = jnp.concatenate(
        [jnp.where(keep[g:g + 1, :] > 0.0, choice[g * GROUP_SIZE:(g + 1) * GROUP_SIZE, :], NEG_INF)
         for g in range(N_EXPERT_GROUPS)], axis=0)

    expert = lax.broadcasted_iota(jnp.int32, (N_EXPERTS, tm), 0).astype(F32)
    onehot = jnp.zeros((N_EXPERTS, tm), F32)
    sels, idx_rows, gate_rows = [], [], []
    for _ in range(TOP_K):
        _, first = _first_argmax(masked, expert, float(N_EXPERTS))
        sel = expert == first
        sels.append(sel)
        idx_rows.append(first)
        gate_rows.append(jnp.sum(jnp.where(sel, scores, 0.0), axis=0, keepdims=True))
        masked = jnp.where(sel, NEG_INF, masked)
        onehot = jnp.where(sel, 1.0, onehot)
    gates = jnp.concatenate(gate_rows, axis=0)
    gate_out[...] = gates / jnp.sum(gates, axis=0, keepdims=True) * ROUTED_SCALE
    idx_out[...] = jnp.concatenate(idx_rows, axis=0).astype(jnp.int32)

    cum = jnp.dot(onehot.astype(BF16), tri_ref[...], preferred_element_type=F32) + base_ref[:, 0:1]
    rank_rows = [jnp.sum(jnp.where(sel, cum, 0.0), axis=0, keepdims=True) - 1.0 for sel in sels]
    rank_out[...] = jnp.concatenate(rank_rows, axis=0).astype(jnp.int32)
    base_ref[...] = jnp.broadcast_to(cum[:, tm - 1:tm], base_ref.shape)
    cnt_out[0] = base_ref[...].astype(jnp.int32)


def _router(x, norm_g, mods4, k_shift, k_scale, router_w_t, router_bias):
    t = T_ALL
    tm = ROUTE_TILE
    n_ctx = T_CTX // tm
    tri = (jnp.arange(tm)[:, None] <= jnp.arange(tm)[None, :]).astype(BF16)
    choice_rows = pl.BlockSpec((TOP_K, tm), lambda i: (0, i))
    return pl.pallas_call(
        _router_kernel,
        grid=(t // tm,),
        in_specs=[pl.BlockSpec((tm, D_MODEL), lambda i: (i, 0)), _resident((1, D_MODEL)),
                  _mod_spec(k_scale, tm), _mod_spec(k_shift, tm),
                  _resident((N_EXPERTS, D_MODEL)), _resident((N_EXPERTS, 1)), _resident((tm, tm))],
        out_specs=[pl.BlockSpec((tm, D_MODEL), lambda i: (i, 0)), choice_rows, choice_rows, choice_rows,
                   pl.BlockSpec((1, N_EXPERTS, 128), lambda i: (jnp.where(i < n_ctx, 0, 1), 0, 0))],
        out_shape=[jax.ShapeDtypeStruct((t, D_MODEL), BF16),
                   jax.ShapeDtypeStruct((TOP_K, t), jnp.int32), jax.ShapeDtypeStruct((TOP_K, t), F32),
                   jax.ShapeDtypeStruct((TOP_K, t), jnp.int32),
                   jax.ShapeDtypeStruct((2, N_EXPERTS, 128), jnp.int32)],
        scratch_shapes=[pltpu.VMEM((N_EXPERTS, 128), F32)],
        compiler_params=_params(("arbitrary",)),
        name="router",
    )(x, norm_g.reshape(1, D_MODEL), mods4, mods4, router_w_t, router_bias.reshape(N_EXPERTS, 1), tri)


def _pos_kernel(idx_ref, rank_ref, off_ref, pos_out):
    tm = idx_ref.shape[1]
    expert = lax.broadcasted_iota(jnp.int32, (N_EXPERTS, tm), 0)
    rows = []
    for k in range(TOP_K):
        sel = expert == idx_ref[k:k + 1, :]
        rows.append(jnp.sum(jnp.where(sel, off_ref[...], 0.0), axis=0, keepdims=True))
    pos_out[...] = jnp.concatenate(rows, axis=0).astype(jnp.int32) + rank_ref[...]


def _positions(idx, rank, offsets):
    t = idx.shape[1]
    tm = 1024
    rows = pl.BlockSpec((TOP_K, tm), lambda i: (0, i))
    return pl.pallas_call(
        _pos_kernel, grid=(t // tm,),
        in_specs=[rows, rows, _resident((N_EXPERTS, 1))],
        out_specs=rows, out_shape=jax.ShapeDtypeStruct((TOP_K, t), jnp.int32),
        compiler_params=_params(("parallel",)), name="expert_positions",
    )(idx, rank, offsets.astype(F32).reshape(N_EXPERTS, 1))


def _moe_stream(x, row0, t, h_bf16, top_idx, gate, rank, counts, mods4, k_gate, exp_w13, exp_w2,
                sh_w13_bf16, sh_w2_bf16, out_norm_g, next_proj):
    padded = (counts + MOE_BLOCK - 1) // MOE_BLOCK * MOE_BLOCK
    e_iota = jnp.arange(N_EXPERTS)
    pend = jnp.sum(jnp.where(e_iota[:, None] <= e_iota[None, :], padded[:, None], 0), axis=0)
    pos = _positions(top_idx, rank, pend - padded)
    n_rows = -(-(t * TOP_K + N_EXPERTS * (MOE_BLOCK - 1)) // MOE_BLOCK) * MOE_BLOCK
    n_blocks = n_rows // MOE_BLOCK
    blk_start = jnp.arange(n_blocks, dtype=pend.dtype) * MOE_BLOCK
    blk_e = jnp.minimum(jnp.sum((pend[None, :] <= blk_start[:, None]).astype(jnp.int32), axis=1), N_EXPERTS - 1)
    n_real = (pend[N_EXPERTS - 1:] // MOE_BLOCK).astype(jnp.int32)
    n_assign = t * TOP_K
    tok = jnp.broadcast_to(jnp.arange(t, dtype=jnp.int32)[None, :], (TOP_K, t))
    _, sorted_tok = lax.sort_key_val(pos.reshape(-1), tok.reshape(-1))
    dense_end = jnp.sum(jnp.where(e_iota[:, None] <= e_iota[None, :], counts[:, None], 0), axis=0)
    shift_blk = ((pend - padded) - (dense_end - counts))[blk_e]
    dense = jnp.arange(n_rows, dtype=jnp.int32) - jnp.repeat(shift_blk, MOE_BLOCK)
    wrapped = (dense & (n_assign - 1)) if n_assign & (n_assign - 1) == 0 else (dense % n_assign)
    xs = h_bf16[sorted_tok[wrapped] + row0]
    ys = _grouped_experts(blk_e, n_real, padded, xs, exp_w13, exp_w2)
    y_exp = ys[pos.reshape(-1)].reshape(TOP_K, t, D_MODEL)
    return _shared_and_combine(h_bf16, x, row0, t, y_exp, gate.T, mods4, k_gate, sh_w13_bf16, sh_w2_bf16,
                               out_norm_g, next_proj)


def _moe_streams(x, norm_g, mods4, k_shift, k_scale, k_gate, router_w, router_bias, exp_w13, exp_w2, sh_w13, sh_w2,
                 out_norm_g=None, next_proj=None):
    h_bf16, top_idx, gate, rank, counts = _router(x, norm_g, mods4, k_shift, k_scale, router_w.T.astype(BF16),
                                                  router_bias)
    sh13, sh2 = sh_w13.astype(BF16), sh_w2.astype(BF16)
    outs = []
    for s, (row0, t) in enumerate(((0, T_CTX), (T_CTX, T_LAT))):
        cols = slice(row0, row0 + t)
        outs.append(_moe_stream(x, row0, t, h_bf16, top_idx[:, cols], gate[:, cols], rank[:, cols], counts[s, :, 0],
                                mods4, k_gate, exp_w13, exp_w2, sh13, sh2, out_norm_g, next_proj))
    return tuple(outs)


HY_BLOCK_CTX, HY_TC_CTX = 256, 512
HY_BLOCK_LAT, HY_TC_LAT = 512, 128


def _hyena_filters(L, w1, b1, w2, b2, w3, sin_freq):
    t_norm = jnp.linspace(0.0, 1.0, L, dtype=F32)[:, None]
    omega = 2.0 * math.pi * jnp.arange(L, dtype=F32)[:, None] / L
    bands = jnp.linspace(1e-4, FILTER_BANDS - 1, FILTER_BANDS, dtype=F32)[None, :]
    z = jnp.concatenate([t_norm, jnp.cos(bands * omega), -jnp.sin(bands * omega)], axis=-1)
    h = jnp.sin(sin_freq * (z @ w1 + b1))
    h = jnp.sin(sin_freq * (h @ w2 + b2))
    h = (h @ w3).reshape(L, 2, HYENA_ORDER, HYENA_WIDTH)
    deltas = jnp.abs(jnp.linspace(math.log(DECAY_TARGET) / DECAY_PCT_LONG,
                                  math.log(DECAY_TARGET) / DECAY_PCT_SHORT, HYENA_WIDTH, dtype=F32))
    h = h * jnp.exp(-t_norm * deltas)[:, None, None, :]
    fwd, bwd = h[:, 0], h[:, 1] * (t_norm > 0.0)[:, :, None]
    norm = jnp.sum(jnp.abs(fwd), axis=0, keepdims=True) + jnp.sum(jnp.abs(bwd), axis=0, keepdims=True)
    return fwd / norm, bwd / norm


HY_ACC_ELEMS = 4096


def _dft_matrices(bk):
    k = jnp.arange(bk, dtype=jnp.int32)
    phase = ((2 * k[:, None] + 1) * k[None, :]) % (4 * bk)
    ang = phase.astype(F32) * (math.pi / (2 * bk))
    return jnp.cos(ang), jnp.sin(ang)


def _spectra_kernel(xf_ref, xb_ref, yf_ref, yb_ref, c1_ref, s1_ref, c2_ref, s2_ref, o_ref, *, nb):
    d = pl.program_id(1) - (nb - 1)
    x = jnp.where(d >= 0, xf_ref[0], xb_ref[0]).astype(BF16)
    y = jnp.where(d >= 1, yf_ref[0], yb_ref[0]).astype(BF16)
    cy = jnp.where(d == 0, c1_ref[...], c2_ref[...])
    sy = jnp.where(d == 0, s1_ref[...], s2_ref[...])
    re = jnp.dot(c1_ref[...], x, preferred_element_type=F32) + jnp.dot(cy, y, preferred_element_type=F32)
    im = jnp.dot(sy, y, preferred_element_type=F32) - jnp.dot(s1_ref[...], x, preferred_element_type=F32)
    o_ref[0, 0, 0] = re
    o_ref[0, 0, 1] = jnp.where(d >= 0, im, -im)


def _filter_spectra(fwd, bwd, bk, tc):
    L = fwd.shape[0]
    nb = L // bk
    n_ct = HYENA_WIDTH // tc
    cos, sin = _dft_matrices(bk)
    k = jnp.arange(bk, dtype=jnp.int32)
    phase = ((2 * k[:, None] + 1) * (bk - k[None, :])) % (4 * bk)
    ang = phase.astype(F32) * (math.pi / (2 * bk))
    live = (k[None, :] > 0).astype(F32)
    mats = [m.astype(BF16) for m in (cos, sin, jnp.cos(ang) * live, jnp.sin(ang) * live)]

    def seg(pick):
        return pl.BlockSpec((1, bk, tc), lambda o, di, ct: (pick(di - (nb - 1)), 0, o * n_ct + ct))

    mat = pl.BlockSpec((bk, bk), lambda o, di, ct: (0, 0), pipeline_mode=pl.Buffered(1))
    halves = [h.reshape(nb, bk, HYENA_ORDER * HYENA_WIDTH) for h in (fwd, bwd)]
    return pl.pallas_call(
        functools.partial(_spectra_kernel, nb=nb),
        grid=(HYENA_ORDER, 2 * nb - 1, n_ct),
        in_specs=[seg(lambda d: jnp.maximum(d, 0)), seg(lambda d: jnp.maximum(-d, 0)),
                  seg(lambda d: jnp.maximum(d - 1, 0)), seg(lambda d: jnp.maximum(-d - 1, 0)),
                  mat, mat, mat, mat],
        out_specs=pl.BlockSpec((1, 1, 2, bk, tc), lambda o, di, ct: (o, di, 0, 0, ct)),
        out_shape=jax.ShapeDtypeStruct((HYENA_ORDER, 2 * nb - 1, 2, bk, HYENA_WIDTH), F32),
        compiler_params=_params(("parallel", "parallel", "parallel")),
        name="filter_spectra",
    )(halves[0], halves[1], halves[0], halves[1], *mats)


def _hyena_kernel(v_ref, x1_ref, x2_ref, cw_ref, cb_ref, hb_ref, k0_ref, k1_ref, c_ref, s_ref, ct_ref, st_ref,
                  o_ref, vs_ref, z1_ref, x2s_ref, ur_ref, ui_ref, yr_ref, yi_ref, *, seq, bk, tc):
    nb = seq // bk
    ft = HY_ACC_ELEMS // tc
    row = lax.broadcasted_iota(jnp.int32, (seq, tc), 0)

    def short_conv(x, j):
        prev = jnp.where(row == 0, 0.0, pltpu.roll(x, 1, 0))
        nxt = jnp.where(row == seq - 1, 0.0, pltpu.roll(x, seq - 1, 0))
        return (prev * cw_ref[0, j:j + 1, :] + x * cw_ref[1, j:j + 1, :] + nxt * cw_ref[2, j:j + 1, :]
                + cb_ref[j:j + 1, :])

    vs_ref[...] = short_conv(v_ref[...], 0)
    z1_ref[...] = short_conv(x1_ref[...], 1)
    x2s_ref[...] = short_conv(x2_ref[...], 2)

    def long_conv(src_ref, k_ref, emit):
        for blk in range(nb):
            ub = src_ref[blk * bk:(blk + 1) * bk, :].astype(BF16)
            ur_ref[blk] = jnp.dot(c_ref[...], ub, preferred_element_type=F32)
            ui_ref[blk] = -jnp.dot(s_ref[...], ub, preferred_element_type=F32)
        for out_blk in range(nb):
            def acc_tile(f, carry, out_blk=out_blk):
                rows = pl.ds(pl.multiple_of(f * ft, ft), ft)
                ar = jnp.zeros((ft, tc), F32)
                ai = jnp.zeros((ft, tc), F32)
                for in_blk in range(nb):
                    d = out_blk - in_blk + nb - 1
                    kr, ki = k_ref[d, 0, rows, :], k_ref[d, 1, rows, :]
                    xr, xi = ur_ref[in_blk, rows, :], ui_ref[in_blk, rows, :]
                    ar = ar + (kr * xr - ki * xi)
                    ai = ai + (kr * xi + ki * xr)
                yr_ref[rows, :] = ar
                yi_ref[rows, :] = ai
                return carry

            lax.fori_loop(0, bk // ft, acc_tile, 0)
            y = jnp.dot(ct_ref[...], yr_ref[...].astype(BF16), preferred_element_type=F32)
            y = y - jnp.dot(st_ref[...], yi_ref[...].astype(BF16), preferred_element_type=F32)
            emit(slice(out_blk * bk, (out_blk + 1) * bk), y * (1.0 / bk))

    def emit_z1(rows, y):
        z1_ref[rows, :] = z1_ref[rows, :] * (y + hb_ref[0:1, :] * vs_ref[rows, :])

    def emit_out(rows, y):
        o_ref[rows, :] = (x2s_ref[rows, :] * (y + hb_ref[1:2, :] * z1_ref[rows, :])).astype(BF16)

    long_conv(vs_ref, k0_ref, emit_z1)
    long_conv(z1_ref, k1_ref, emit_out)


def _hyena(proj, row0, n_batch, seq, bk, tc, conv_w, conv_b, hy_bias, spectra, mats):
    nb = seq // bk
    n_ct = HYENA_WIDTH // tc
    col0 = QKV_WIDTH // tc
    per = HYENA_WIDTH // tc
    b0 = row0 // seq

    def data(which):
        return pl.BlockSpec((seq, tc), lambda ct, b: (b0 + b, col0 + which * per + ct))

    const2 = lambda shape: pl.BlockSpec(shape, lambda ct, b: (0, 0), pipeline_mode=pl.Buffered(1))
    kspec = pl.BlockSpec((2 * nb - 1, 2, bk, tc), lambda ct, b: (0, 0, 0, ct), pipeline_mode=pl.Buffered(1))
    cw = conv_w.reshape(3, HYENA_ORDER + 1, HYENA_WIDTH)
    cb = conv_b.reshape(HYENA_ORDER + 1, HYENA_WIDTH)
    return pl.pallas_call(
        functools.partial(_hyena_kernel, seq=seq, bk=bk, tc=tc),
        grid=(n_ct, n_batch),
        in_specs=[data(0), data(1), data(2),
                  pl.BlockSpec((3, HYENA_ORDER + 1, tc), lambda ct, b: (0, 0, ct)),
                  pl.BlockSpec((HYENA_ORDER + 1, tc), lambda ct, b: (0, ct)),
                  pl.BlockSpec((HYENA_ORDER, tc), lambda ct, b: (0, ct)),
                  kspec, kspec, const2((bk, bk)), const2((bk, bk)), const2((bk, bk)), const2((bk, bk))],
        out_specs=pl.BlockSpec((seq, tc), lambda ct, b: (b, ct)),
        out_shape=jax.ShapeDtypeStruct((n_batch * seq, HYENA_WIDTH), BF16),
        scratch_shapes=[pltpu.VMEM((seq, tc), F32), pltpu.VMEM((seq, tc), F32), pltpu.VMEM((seq, tc), F32),
                        pltpu.VMEM((nb, bk, tc), F32), pltpu.VMEM((nb, bk, tc), F32),
                        pltpu.VMEM((bk, tc), F32), pltpu.VMEM((bk, tc), F32)],
        compiler_params=_params(("parallel", "parallel")),
        name="hyena_seq%d" % seq,
    )(proj, proj, proj, cw, cb, hy_bias, spectra[0], spectra[1], *mats)


HY_SPECTRA_TC = 512


def _hyena_operands(L, bk, f_w1, f_b1, f_w2, f_b2, f_w3, sin_freq):
    fwd, bwd = _hyena_filters(L, f_w1, f_b1, f_w2, f_b2, f_w3, sin_freq)
    cos, sin = _dft_matrices(bk)
    mats = tuple(m.astype(BF16) for m in (cos, sin, cos.T, sin.T))
    return _filter_spectra(fwd, bwd, bk, HY_SPECTRA_TC), mats


S5_LANES = S5_GROUPS * S5_STATE
S5_CLUSTER_GROUPS = 16
S5_CLUSTERS = S5_GROUPS // S5_CLUSTER_GROUPS
S5_CLUSTER_CH = S5_CLUSTER_GROUPS * S5_GROUP_CH
S5_CLUSTER_LANES = S5_CLUSTER_GROUPS * S5_STATE
SCAN_SEQS = 8
SCAN_ROWS = 512


def _s5_kernel(u_ref, s0r_ref, s0i_ref, bw_ref, cwr_ref, cwi_ref, lam_ref, y_ref, sfr_ref, sfi_ref,
               sre, sim, car, cai, *, reverse):
    n_seq, tc, _ = u_ref.shape

    @pl.when(pl.program_id(1) == 0)
    def _():
        car[...] = jnp.zeros_like(car)
        cai[...] = jnp.zeros_like(cai)
        car[0:n_seq, :] = s0r_ref[0]
        cai[0:n_seq, :] = s0i_ref[0]

    u = u_ref[...]
    if n_seq < SCAN_SEQS:
        u = jnp.concatenate([u, jnp.zeros((SCAN_SEQS - n_seq, tc, S5_WIDTH), F32)], axis=0)
    ub = jnp.swapaxes(u, 0, 1).reshape(tc * SCAN_SEQS, S5_WIDTH).astype(BF16)
    for k in range(S5_CLUSTERS):
        bu = jnp.dot(ub[:, k * S5_CLUSTER_CH:(k + 1) * S5_CLUSTER_CH], bw_ref[k], preferred_element_type=F32)
        sre[:, k * S5_CLUSTER_LANES:(k + 1) * S5_CLUSTER_LANES] = bu[:, :S5_CLUSTER_LANES]
        sim[:, k * S5_CLUSTER_LANES:(k + 1) * S5_CLUSTER_LANES] = bu[:, S5_CLUSTER_LANES:]

    for k in range(S5_CLUSTERS):
        lanes = slice(k * S5_CLUSTER_LANES, (k + 1) * S5_CLUSTER_LANES)
        lr = jnp.broadcast_to(lam_ref[0:1, lanes], (SCAN_SEQS, S5_CLUSTER_LANES))
        li = jnp.broadcast_to(lam_ref[1:2, lanes], (SCAN_SEQS, S5_CLUSTER_LANES))

        def step(i, state, lanes=lanes, lr=lr, li=li):
            sr, si = state
            t = (tc - 1 - i) if reverse else i
            rows = pl.ds(pl.multiple_of(t * SCAN_SEQS, SCAN_SEQS), SCAN_SEQS)
            nr = (lr * sr - li * si) + sre[rows, lanes]
            ni = (lr * si + li * sr) + sim[rows, lanes]
            sre[rows, lanes] = nr
            sim[rows, lanes] = ni
            return nr, ni

        sr, si = lax.fori_loop(0, tc, step, (car[:, lanes], cai[:, lanes]), unroll=2)
        car[:, lanes] = sr
        cai[:, lanes] = si

    parts = []
    for k in range(S5_CLUSTERS):
        lanes = slice(k * S5_CLUSTER_LANES, (k + 1) * S5_CLUSTER_LANES)
        yk = jnp.dot(sre[:, lanes].astype(BF16), cwr_ref[k], preferred_element_type=F32)
        parts.append(yk + jnp.dot(sim[:, lanes].astype(BF16), cwi_ref[k], preferred_element_type=F32))
    y = jnp.concatenate(parts, axis=1).reshape(tc, SCAN_SEQS, S5_WIDTH)
    y_ref[...] = jnp.swapaxes(y, 0, 1)[0:n_seq]
    sfr_ref[0] = car[0:n_seq, :]
    sfi_ref[0] = cai[0:n_seq, :]


def _s5_direction(u, n_seq, seq, group, s0_re, s0_im, bw, cwr, cwi, lam, reverse):
    tc = SCAN_ROWS // SCAN_SEQS
    nc = seq // tc
    ng = n_seq // group

    def chunk(c):
        return (nc - 1 - c) if reverse else c

    state_spec = pl.BlockSpec((1, group, S5_LANES), lambda g, c: (g, 0, 0))
    const = lambda a: pl.BlockSpec(a.shape, lambda g, c: (0,) * a.ndim, pipeline_mode=pl.Buffered(1))
    y, f_re, f_im = pl.pallas_call(
        functools.partial(_s5_kernel, reverse=reverse),
        grid=(ng, nc),
        in_specs=[pl.BlockSpec((group, tc, S5_WIDTH), lambda g, c: (g, chunk(c), 0)),
                  state_spec, state_spec, const(bw), const(cwr), const(cwi), const(lam)],
        out_specs=[pl.BlockSpec((group, tc, S5_WIDTH), lambda g, c: (g, chunk(c), 0)),
                   state_spec, state_spec],
        out_shape=[jax.ShapeDtypeStruct((n_seq, seq, S5_WIDTH), F32),
                   jax.ShapeDtypeStruct((ng, group, S5_LANES), F32),
                   jax.ShapeDtypeStruct((ng, group, S5_LANES), F32)],
        scratch_shapes=[pltpu.VMEM((SCAN_ROWS, S5_LANES), F32), pltpu.VMEM((SCAN_ROWS, S5_LANES), F32),
                        pltpu.VMEM((SCAN_SEQS, S5_LANES), F32), pltpu.VMEM((SCAN_SEQS, S5_LANES), F32)],
        compiler_params=_params(("parallel", "arbitrary")),
        name="s5_bwd" if reverse else "s5_fwd",
    )(u.reshape(n_seq, seq, S5_WIDTH), s0_re.reshape(ng, group, S5_LANES), s0_im.reshape(ng, group, S5_LANES),
      bw, cwr, cwi, lam)
    return y.reshape(n_seq * seq, S5_WIDTH), f_re.reshape(n_seq, S5_LANES), f_im.reshape(n_seq, S5_LANES)


def _s5_weights(a_re, a_im, log_dt, b_re, b_im, c_re, c_im):
    lam = lax.complex(a_re, a_im)
    lam_bar = jnp.exp(lam * jnp.exp(log_dt)[..., None])
    b_bar = ((lam_bar - 1.0) / lam)[..., None] * lax.complex(b_re, b_im)
    eye = jnp.eye(S5_CLUSTER_GROUPS, dtype=F32)
    out = []
    for d in range(2):
        def cluster_in(w):
            w = w.reshape(S5_CLUSTERS, S5_CLUSTER_GROUPS, S5_STATE, S5_GROUP_CH)
            return jnp.einsum('ab,kapc->kacbp', eye, w).reshape(S5_CLUSTERS, S5_CLUSTER_CH, S5_CLUSTER_LANES)

        def cluster_out(w):
            w = w.reshape(S5_CLUSTERS, S5_CLUSTER_GROUPS, S5_GROUP_CH, S5_STATE)
            return jnp.einsum('ab,kacp->kapbc', eye, w).reshape(S5_CLUSTERS, S5_CLUSTER_LANES, S5_CLUSTER_CH)

        bw = jnp.concatenate([cluster_in(jnp.real(b_bar[d])), cluster_in(jnp.imag(b_bar[d]))], axis=-1)
        cwr = cluster_out(c_re[d])
        cwi = cluster_out(-c_im[d])
        lam_d = lam_bar[d].reshape(S5_LANES)
        lam2 = jnp.stack([jnp.real(lam_d), jnp.imag(lam_d)]).astype(F32)
        out.append((bw.astype(BF16), cwr.astype(BF16), cwi.astype(BF16), lam2))
    return out


def _rope_tables(L):
    n_rows = L // GRID_W
    row_idx = jnp.repeat(jnp.arange(n_rows, dtype=F32), GRID_W)
    col_idx = (jnp.arange(L) % GRID_W).astype(F32)
    inv = ROPE_THETA ** (-jnp.arange(0, ROPE_AXIS_DIM, 2, dtype=F32) / ROPE_AXIS_DIM)
    ang = jnp.concatenate([row_idx[:, None] * inv, col_idx[:, None] * inv], axis=-1)
    cos, sin = jnp.cos(ang), jnp.sin(ang)
    cos_full = jnp.repeat(cos, 2, axis=-1)
    sin_signed = jnp.stack([-sin, sin], axis=-1).reshape(L, HEAD_DIM)
    return cos_full, sin_signed


def kernel(x_prompt, x_sample, cache_k_l0, cache_v_l0, state_s5_re_l1, state_s5_im_l1, c, c_ctx,
           ada_w_l0, ada_b_l0, norm1_l0, norm2_l0,
           w_in_l0, w_out_l0, q_norm_l0, k_norm_l0, hy_conv_w_l0, hy_conv_b_l0,
           hy_ffn_w1_l0, hy_ffn_b1_l0, hy_ffn_w2_l0, hy_ffn_b2_l0, hy_ffn_w3_l0, hy_sin_freq_l0, hy_bias_l0,
           router_l0, router_bias_l0, exp_w13_l0, exp_w2_l0, shared_w13_l0, shared_w2_l0,
           ada_w_l1, ada_b_l1, norm1_l1, norm2_l1,
           w_in_l1, s5_a_re_l1, s5_a_im_l1, s5_log_dt_l1, s5_b_re_l1, s5_b_im_l1,
           s5_c_re_l1, s5_c_im_l1, s5_d_l1, w_glu_l1,
           router_l1, router_bias_l1, exp_w13_l1, exp_w2_l1, shared_w13_l1, shared_w2_l1,
           final_norm):
    x = (x_prompt.reshape(T_CTX, D_MODEL), x_sample.reshape(T_LAT, D_MODEL))
    cond = jnp.concatenate([c_ctx[None, :], c, jnp.zeros((N_COND - 1 - DEC_BATCH, D_MODEL), F32)], axis=0)

    mods = _ada_mods(cond, ada_w_l0, ada_b_l0)
    proj = _norm_mod_matmul(x, norm1_l0, mods, 0, 1, w_in_l0.astype(BF16))

    q_c, k_c, v_c, new_k = _qk_prep(proj, 0, T_CTX, q_norm_l0, k_norm_l0, None)
    attn_c = _attention(q_c, k_c, v_c, BATCH, SEQ, None)
    q_l, k_l, v_l = _qk_prep(proj, T_CTX, T_LAT, q_norm_l0, k_norm_l0, _rope_tables(DEC_SEQ))
    ctx_kv = (cache_k_l0.reshape(DEC_BATCH * PAST_LEN, KV_WIDTH).astype(BF16),
              cache_v_l0.reshape(DEC_BATCH * PAST_LEN, KV_WIDTH).astype(BF16))
    attn_l = _attention(q_l, k_l, v_l, DEC_BATCH, DEC_SEQ, ctx_kv)

    hy_f = (hy_ffn_w1_l0, hy_ffn_b1_l0, hy_ffn_w2_l0, hy_ffn_b2_l0, hy_ffn_w3_l0, hy_sin_freq_l0)
    spec_c, mats_c = _hyena_operands(SEQ, HY_BLOCK_CTX, *hy_f)
    z_c = _hyena(proj, 0, BATCH, SEQ, HY_BLOCK_CTX, HY_TC_CTX, hy_conv_w_l0, hy_conv_b_l0, hy_bias_l0,
                 spec_c, mats_c)
    spec_l, mats_l = _hyena_operands(DEC_SEQ, HY_BLOCK_LAT, *hy_f)
    z_l = _hyena(proj, T_CTX, DEC_BATCH, DEC_SEQ, HY_BLOCK_LAT, HY_TC_LAT, hy_conv_w_l0, hy_conv_b_l0, hy_bias_l0,
                 spec_l, mats_l)

    w_out = w_out_l0.astype(BF16)
    x = _out_proj((attn_c, attn_l), (z_c, z_l), x, mods, 2, w_out[:Q_WIDTH], w_out[Q_WIDTH:])
    mods_l1 = _ada_mods(cond, ada_w_l1, ada_b_l1)
    (x_c, u_c), (x_l, u_l) = _moe_streams(x, norm2_l0, mods, 3, 4, 5, router_l0, router_bias_l0, exp_w13_l0,
                                          exp_w2_l0, shared_w13_l0, shared_w2_l0,
                                          next_proj=(norm1_l1, mods_l1, 0, 1, w_in_l1.astype(BF16)))
    x, u, mods = (x_c, x_l), (u_c, u_l), mods_l1

    new_v = proj[:T_CTX, Q_WIDTH + KV_WIDTH:QKV_WIDTH]

    s5_w = _s5_weights(s5_a_re_l1, s5_a_im_l1, s5_log_dt_l1, s5_b_re_l1, s5_b_im_l1, s5_c_re_l1, s5_c_im_l1)
    zero_state = jnp.zeros((BATCH, S5_LANES), F32)
    ys, finals = [], []
    for d in range(2):
        y_c, f_re, f_im = _s5_direction(u_c, BATCH, SEQ, SCAN_SEQS, zero_state, zero_state, *s5_w[d],
                                        reverse=d == 1)
        y_l, _, _ = _s5_direction(u_l, DEC_BATCH, DEC_SEQ, DEC_BATCH,
                                  state_s5_re_l1[:, d].reshape(DEC_BATCH, S5_LANES),
                                  state_s5_im_l1[:, d].reshape(DEC_BATCH, S5_LANES), *s5_w[d], reverse=d == 1)
        ys.append((y_c, y_l))
        finals.append((f_re.reshape(BATCH, S5_GROUPS, S5_STATE), f_im.reshape(BATCH, S5_GROUPS, S5_STATE)))
    s_re = jnp.stack([finals[0][0], finals[1][0]], axis=1)
    s_im = jnp.stack([finals[0][1], finals[1][1]], axis=1)
    x = _glu_proj(ys[0], ys[1], u, s5_d_l1, x, mods, 2, w_glu_l1.astype(BF16))
    y_c, y_l = _moe_streams(x, norm2_l1, mods, 3, 4, 5, router_l1, router_bias_l1, exp_w13_l1, exp_w2_l1,
                            shared_w13_l1, shared_w2_l1, out_norm_g=final_norm)

    return (y_c.reshape(BATCH, SEQ, D_MODEL),
            y_l.reshape(DEC_BATCH, DEC_SEQ, D_MODEL),
            new_k.reshape(BATCH, SEQ, N_KV_HEADS, HEAD_DIM),
            new_v.reshape(BATCH, SEQ, N_KV_HEADS, HEAD_DIM),
            s_re, s_im)
```

```python
import functools
import math

import jax
import jax.numpy as jnp
from jax import lax
from jax.experimental import pallas as pl
from jax.experimental.pallas import tpu as pltpu

F32 = jnp.float32
BF16 = jnp.bfloat16

D_MODEL = 2048
BATCH = 32
SEQ = 256
DEC_BATCH = 4
DEC_SEQ = 4096
PAST_LEN = 256
GRID_W = 64
RMS_EPS = 1e-6
HEAD_DIM = 128
N_HEADS = 8
N_KV_HEADS = 2
GQA_GROUP = N_HEADS // N_KV_HEADS
Q_WIDTH = N_HEADS * HEAD_DIM
KV_WIDTH = N_KV_HEADS * HEAD_DIM
QKV_WIDTH = Q_WIDTH + 2 * KV_WIDTH
ROPE_THETA = 10000.0
ROPE_AXIS_DIM = HEAD_DIM // 2
HYENA_WIDTH = D_MODEL // 2
HYENA_ORDER = 2
FILTER_BANDS = 16
DECAY_TARGET = 1e-2
DECAY_PCT_SHORT = 0.3
DECAY_PCT_LONG = 1.5
S5_WIDTH = D_MODEL // 2
S5_GROUP_CH = 16
S5_GROUPS = S5_WIDTH // S5_GROUP_CH
S5_STATE = 64
N_EXPERTS = 64
TOP_K = 8
N_EXPERT_GROUPS = 8
TOPK_GROUPS = 4
EXPERT_HIDDEN = 512
ROUTED_SCALE = 2.5
MOE_BLOCK = 256

T_CTX = BATCH * SEQ
T_LAT = DEC_BATCH * DEC_SEQ
T_ALL = T_CTX + T_LAT
N_COND = 8

ROW_TILE = 256
V7X_VMEM_LIMIT_BYTES = 56 * 1024 * 1024


def _params(semantics):
    return pltpu.CompilerParams(dimension_semantics=semantics, vmem_limit_bytes=V7X_VMEM_LIMIT_BYTES)


def _cond_row(i, tm, row0=0):
    tok = row0 + i * tm
    return jnp.where(tok < T_CTX, 0, 1 + (tok - T_CTX) // DEC_SEQ)


def _mod_spec(k, tm, row0=0):
    return pl.BlockSpec((1, 1, 1, D_MODEL), lambda i: (_cond_row(i, tm, row0), k, 0, 0))


def _resident(shape):
    nd = len(shape)
    return pl.BlockSpec(shape, lambda i: (0,) * nd, pipeline_mode=pl.Buffered(1))


def _silu(x):
    return x * jax.nn.sigmoid(x)


def _norm_mod(x, g, sc, sh):
    ms = jnp.mean(x * x, axis=-1, keepdims=True)
    return (x * lax.rsqrt(ms + RMS_EPS) * g) * (1.0 + sc) + sh


def _ada_kernel(c_ref, w_ref, b_ref, o_ref):
    a = _silu(c_ref[...]).astype(BF16)
    o_ref[...] = jnp.dot(a, w_ref[...].astype(BF16), preferred_element_type=F32) + b_ref[...]


def _ada_mods(cond, w, b):
    n = w.shape[1]
    tn = 2048
    out = pl.pallas_call(
        _ada_kernel,
        grid=(n // tn,),
        in_specs=[pl.BlockSpec((N_COND, D_MODEL), lambda j: (0, 0)),
                  pl.BlockSpec((D_MODEL, tn), lambda j: (0, j)),
                  pl.BlockSpec((1, tn), lambda j: (0, j))],
        out_specs=pl.BlockSpec((N_COND, tn), lambda j: (0, j)),
        out_shape=jax.ShapeDtypeStruct((N_COND, n), F32),
        compiler_params=_params(("parallel",)),
        name="ada_mods",
    )(cond, w, b.reshape(1, n))
    return out.reshape(N_COND, 6, 1, D_MODEL)


def _split_row_specs(width, tm):
    n_ctx = T_CTX // tm
    return [pl.BlockSpec((tm, width), lambda i: (jnp.minimum(i, n_ctx - 1), 0)),
            pl.BlockSpec((tm, width), lambda i: (jnp.maximum(i - n_ctx, 0), 0))]


def _pick_rows(ctx_ref, lat_ref):
    return jnp.where(pl.program_id(0) < T_CTX // ctx_ref.shape[0], ctx_ref[...], lat_ref[...])


def _nmm_kernel(*refs, split):
    if split:
        xc_ref, xl_ref, g_ref, sc_ref, sh_ref, w_ref, o_ref = refs
        x = _pick_rows(xc_ref, xl_ref)
    else:
        x_ref, g_ref, sc_ref, sh_ref, w_ref, o_ref = refs
        x = x_ref[...]
    h = _norm_mod(x, g_ref[...], sc_ref[0, 0], sh_ref[0, 0])
    o_ref[...] = jnp.dot(h.astype(BF16), w_ref[...], preferred_element_type=F32)


def _norm_mod_matmul(x, g, mods4, k_shift, k_scale, w_bf16):
    n = w_bf16.shape[1]
    tm = ROW_TILE
    split = isinstance(x, tuple)
    x_args = list(x) if split else [x]
    x_specs = _split_row_specs(D_MODEL, tm) if split else [pl.BlockSpec((tm, D_MODEL), lambda i: (i, 0))]
    return pl.pallas_call(
        functools.partial(_nmm_kernel, split=split),
        grid=(T_ALL // tm,),
        in_specs=x_specs + [_resident((1, D_MODEL)), _mod_spec(k_scale, tm), _mod_spec(k_shift, tm),
                            _resident((D_MODEL, n))],
        out_specs=pl.BlockSpec((tm, n), lambda i: (i, 0)),
        out_shape=jax.ShapeDtypeStruct((T_ALL, n), F32),
        compiler_params=_params(("parallel",)),
        name="norm_mod_matmul",
    )(*x_args, g.reshape(1, D_MODEL), mods4, mods4, w_bf16)


def _head_norm(xh, g):
    ms = jnp.mean(xh * xh, axis=-1, keepdims=True)
    return xh * lax.rsqrt(ms + RMS_EPS) * g


def _rope(y, cos, sin_signed):
    lane = lax.broadcasted_iota(jnp.int32, y.shape, 1)
    partner = jnp.where(lane % 2 == 0, pltpu.roll(y, HEAD_DIM - 1, 1), pltpu.roll(y, 1, 1))
    return y * cos + partner * sin_signed


def _qkprep_ctx_kernel(qkv_ref, qg_ref, kg_ref, q_out, kb_out, vb_out, kf_out):
    for h in range(N_HEADS):
        sl = slice(h * HEAD_DIM, (h + 1) * HEAD_DIM)
        q_out[:, sl] = _head_norm(qkv_ref[:, sl], qg_ref[...]).astype(BF16)
    for j in range(N_KV_HEADS):
        k = _head_norm(qkv_ref[:, Q_WIDTH + j * HEAD_DIM:Q_WIDTH + (j + 1) * HEAD_DIM], kg_ref[...])
        kf_out[:, j * HEAD_DIM:(j + 1) * HEAD_DIM] = k
        kb_out[:, j * HEAD_DIM:(j + 1) * HEAD_DIM] = k.astype(BF16)
    vb_out[...] = qkv_ref[:, Q_WIDTH + KV_WIDTH:QKV_WIDTH].astype(BF16)


def _qkprep_lat_kernel(qkv_ref, qg_ref, kg_ref, cos_ref, sin_ref, q_out, kb_out, vb_out):
    cos, sin = cos_ref[...], sin_ref[...]
    for h in range(N_HEADS):
        sl = slice(h * HEAD_DIM, (h + 1) * HEAD_DIM)
        q_out[:, sl] = _rope(_head_norm(qkv_ref[:, sl], qg_ref[...]), cos, sin).astype(BF16)
    for j in range(N_KV_HEADS):
        k = _head_norm(qkv_ref[:, Q_WIDTH + j * HEAD_DIM:Q_WIDTH + (j + 1) * HEAD_DIM], kg_ref[...])
        kb_out[:, j * HEAD_DIM:(j + 1) * HEAD_DIM] = _rope(k, cos, sin).astype(BF16)
    vb_out[...] = qkv_ref[:, Q_WIDTH + KV_WIDTH:QKV_WIDTH].astype(BF16)


def _qk_prep(proj, row0, t, q_norm, k_norm, rope_tabs):
    tm = ROW_TILE
    b0 = row0 // tm
    qkv_spec = pl.BlockSpec((tm, QKV_WIDTH), lambda i: (b0 + i, 0))
    gain = _resident((1, HEAD_DIM))
    outs = [pl.BlockSpec((tm, Q_WIDTH), lambda i: (i, 0)),
            pl.BlockSpec((tm, KV_WIDTH), lambda i: (i, 0)),
            pl.BlockSpec((tm, KV_WIDTH), lambda i: (i, 0))]
    shapes = [jax.ShapeDtypeStruct((t, Q_WIDTH), BF16),
              jax.ShapeDtypeStruct((t, KV_WIDTH), BF16),
              jax.ShapeDtypeStruct((t, KV_WIDTH), BF16)]
    qg, kg = q_norm.reshape(1, HEAD_DIM), k_norm.reshape(1, HEAD_DIM)
    if rope_tabs is None:
        return pl.pallas_call(
            _qkprep_ctx_kernel, grid=(t // tm,),
            in_specs=[qkv_spec, gain, gain],
            out_specs=outs + [pl.BlockSpec((tm, KV_WIDTH), lambda i: (i, 0))],
            out_shape=shapes + [jax.ShapeDtypeStruct((t, KV_WIDTH), F32)],
            compiler_params=_params(("parallel",)), name="qk_prep_ctx",
        )(proj, qg, kg)
    cos, sin = rope_tabs
    nq = DEC_SEQ // tm
    tab = pl.BlockSpec((tm, HEAD_DIM), lambda i: (i % nq, 0))
    return pl.pallas_call(
        _qkprep_lat_kernel, grid=(t // tm,),
        in_specs=[qkv_spec, gain, gain, tab, tab],
        out_specs=outs, out_shape=shapes,
        compiler_params=_params(("parallel",)), name="qk_prep_lat",
    )(proj, qg, kg, cos, sin)


def _attn_kernel(*refs, with_ctx):
    if with_ctx:
        q_ref, k_ref, v_ref, ck_ref, cv_ref, o_ref = refs
    else:
        q_ref, k_ref, v_ref, o_ref = refs
    c = HEAD_DIM ** -0.5 * math.log2(math.e)
    nt = (((1,), (1,)), ((), ()))
    for hh in range(ATTN_HEADS_PER_STEP):
        cols = slice(hh * HEAD_DIM, (hh + 1) * HEAD_DIM)
        q = q_ref[:, cols]
        s = lax.dot_general(q, k_ref[...], nt, preferred_element_type=F32)
        m = jnp.max(s, axis=-1, keepdims=True)
        if with_ctx:
            sc = lax.dot_general(q, ck_ref[...], nt, preferred_element_type=F32)
            m = jnp.maximum(m, jnp.max(sc, axis=-1, keepdims=True))
        p = jnp.exp2((s - m) * c)
        l = jnp.sum(p, axis=-1, keepdims=True)
        o = jnp.dot(p.astype(BF16), v_ref[...], preferred_element_type=F32)
        if with_ctx:
            pc = jnp.exp2((sc - m) * c)
            l = l + jnp.sum(pc, axis=-1, keepdims=True)
            o = o + jnp.dot(pc.astype(BF16), cv_ref[...], preferred_element_type=F32)
        o_ref[:, cols] = (o / l).astype(BF16)


ATTN_HEADS_PER_STEP = 1
ATTN_Q_TILE = 256


def _attention(q, k, v, n_batch, seq, ctx_kv):
    tq = ATTN_Q_TILE
    nq = seq // tq
    t = n_batch * seq
    hw = ATTN_HEADS_PER_STEP * HEAD_DIM
    kv_of = GQA_GROUP // ATTN_HEADS_PER_STEP
    q_spec = pl.BlockSpec((tq, hw), lambda b, h, i: (b * nq + i, h))
    kv_spec = pl.BlockSpec((seq, HEAD_DIM), lambda b, h, i: (b, h // kv_of))
    in_specs = [q_spec, kv_spec, kv_spec]
    args = [q, k, v]
    if ctx_kv is not None:
        c_spec = pl.BlockSpec((PAST_LEN, HEAD_DIM), lambda b, h, i: (b, h // kv_of))
        in_specs += [c_spec, c_spec]
        args += list(ctx_kv)
    return pl.pallas_call(
        functools.partial(_attn_kernel, with_ctx=ctx_kv is not None),
        grid=(n_batch, N_HEADS // ATTN_HEADS_PER_STEP, nq),
        in_specs=in_specs,
        out_specs=pl.BlockSpec((tq, hw), lambda b, h, i: (b * nq + i, h)),
        out_shape=jax.ShapeDtypeStruct((t, Q_WIDTH), BF16),
        compiler_params=_params(("parallel", "parallel", "parallel")),
        name="attention_lat" if ctx_kv is not None else "attention_ctx",
    )(*args)


def _emit_residual(x, g2_ref, sc2_ref, sh2_ref, o_ref, h_ref):
    o_ref[...] = x
    h_ref[...] = _norm_mod(x, g2_ref[...], sc2_ref[0, 0], sh2_ref[0, 0]).astype(BF16)


def _moe_input_specs(k_shift, k_scale, tm):
    return [_resident((1, D_MODEL)), _mod_spec(k_scale, tm), _mod_spec(k_shift, tm)]


def _residual_outputs(tm):
    rows = pl.BlockSpec((tm, D_MODEL), lambda i: (i, 0))
    return [rows, rows], [jax.ShapeDtypeStruct((T_ALL, D_MODEL), F32), jax.ShapeDtypeStruct((T_ALL, D_MODEL), BF16)]


def _outproj_kernel(ac_ref, al_ref, zc_ref, zl_ref, xc_ref, xl_ref, gate_ref, wa_ref, wz_ref,
                    g2_ref, sc2_ref, sh2_ref, o_ref, h_ref):
    acc = jnp.dot(_pick_rows(ac_ref, al_ref), wa_ref[...], preferred_element_type=F32)
    acc = acc + jnp.dot(_pick_rows(zc_ref, zl_ref), wz_ref[...], preferred_element_type=F32)
    _emit_residual(_pick_rows(xc_ref, xl_ref) + gate_ref[0, 0] * acc, g2_ref, sc2_ref, sh2_ref, o_ref, h_ref)


def _out_proj(attn, z, x, mods4, k_gate, wa_bf16, wz_bf16, norm2_g, k_shift2, k_scale2):
    tm = ROW_TILE
    out_specs, out_shape = _residual_outputs(tm)
    return pl.pallas_call(
        _outproj_kernel,
        grid=(T_ALL // tm,),
        in_specs=(_split_row_specs(Q_WIDTH, tm) + _split_row_specs(HYENA_WIDTH, tm) + _split_row_specs(D_MODEL, tm)
                  + [_mod_spec(k_gate, tm), _resident((Q_WIDTH, D_MODEL)), _resident((HYENA_WIDTH, D_MODEL))]
                  + _moe_input_specs(k_shift2, k_scale2, tm)),
        out_specs=out_specs,
        out_shape=out_shape,
        compiler_params=_params(("parallel",)),
        name="out_proj",
    )(*attn, *z, *x, mods4, wa_bf16, wz_bf16, norm2_g.reshape(1, D_MODEL), mods4, mods4)


def _gelu_tanh(x):
    return 0.5 * x * (1.0 + jnp.tanh(math.sqrt(2.0 / math.pi) * (x + 0.044715 * (x * x * x))))


def _glu_kernel(yfc_ref, yfl_ref, ybc_ref, ybl_ref, uc_ref, ul_ref, d_ref, xc_ref, xl_ref, gate_ref, w_ref,
                g2_ref, sc2_ref, sh2_ref, o_ref, h_ref):
    y = (_pick_rows(yfc_ref, yfl_ref) + _pick_rows(ybc_ref, ybl_ref)) + d_ref[...] * _pick_rows(uc_ref, ul_ref)
    ag = jnp.dot(_gelu_tanh(y).astype(BF16), w_ref[...], preferred_element_type=F32)
    a, g = ag[:, :D_MODEL], ag[:, D_MODEL:]
    _emit_residual(_pick_rows(xc_ref, xl_ref) + gate_ref[0, 0] * (a * jax.nn.sigmoid(g)),
                   g2_ref, sc2_ref, sh2_ref, o_ref, h_ref)


def _glu_proj(y_fwd, y_bwd, u, d_skip, x, mods4, k_gate, w_bf16, norm2_g, k_shift2, k_scale2):
    tm = ROW_TILE
    out_specs, out_shape = _residual_outputs(tm)
    return pl.pallas_call(
        _glu_kernel,
        grid=(T_ALL // tm,),
        in_specs=(_split_row_specs(S5_WIDTH, tm) + _split_row_specs(S5_WIDTH, tm) + _split_row_specs(S5_WIDTH, tm)
                  + [_resident((1, S5_WIDTH))]
                  + _split_row_specs(D_MODEL, tm)
                  + [_mod_spec(k_gate, tm), _resident((S5_WIDTH, 2 * D_MODEL))]
                  + _moe_input_specs(k_shift2, k_scale2, tm)),
        out_specs=out_specs,
        out_shape=out_shape,
        compiler_params=_params(("parallel",)),
        name="glu_proj",
    )(*y_fwd, *y_bwd, *u, d_skip.reshape(1, S5_WIDTH), *x, mods4, w_bf16, norm2_g.reshape(1, D_MODEL), mods4, mods4)


def _gmm_kernel(be_ref, nreal_ref, slot_ref, next_ref, xs_ref, w13_hbm, w2_hbm, o_ref,
                w13_f, w2_f, w13_b, w2_b, sems):
    i = pl.program_id(0)
    real = i < nreal_ref[0]
    fresh = jnp.logical_and(real, jnp.logical_or(i == 0, be_ref[i] != be_ref[jnp.maximum(i - 1, 0)]))

    def weight_copies(expert, slot):
        return (pltpu.make_async_copy(w13_hbm.at[expert], w13_f.at[slot], sems.at[slot, 0]),
                pltpu.make_async_copy(w2_hbm.at[expert], w2_f.at[slot], sems.at[slot, 1]))

    @pl.when(fresh)
    def _():
        expert, slot, nxt = be_ref[i], slot_ref[i], next_ref[i]

        @pl.when(i == 0)
        def _():
            for cp in weight_copies(expert, slot):
                cp.start()

        for cp in weight_copies(expert, slot):
            cp.wait()
        w13_b[...] = w13_f[slot].astype(BF16)
        w2_b[...] = w2_f[slot].astype(BF16)

        @pl.when(nxt >= 0)
        def _():
            for cp in weight_copies(nxt, 1 - slot):
                cp.start()

    @pl.when(real)
    def _():
        gu = jnp.dot(xs_ref[...], w13_b[...], preferred_element_type=F32)
        hmid = _silu(gu[:, :EXPERT_HIDDEN]) * gu[:, EXPERT_HIDDEN:]
        o_ref[...] = jnp.dot(hmid.astype(BF16), w2_b[...], preferred_element_type=F32).astype(BF16)

    @pl.when(jnp.logical_not(real))
    def _():
        o_ref[...] = jnp.zeros_like(o_ref)


def _grouped_experts(blk_e, n_real, padded, xs, w13, w2):
    n_rows = xs.shape[0]
    n_blocks = n_rows // MOE_BLOCK
    b_iota = jnp.arange(n_blocks)
    first = jnp.concatenate([jnp.ones((1,), jnp.int32), (blk_e[1:] != blk_e[:-1]).astype(jnp.int32)])
    run = jnp.sum(jnp.where(b_iota[:, None] <= b_iota[None, :], first[:, None], 0), axis=0) - 1
    slot = (run % 2).astype(jnp.int32)
    e_iota = jnp.arange(N_EXPERTS)
    later = jnp.logical_and(e_iota[None, :] > e_iota[:, None], padded[None, :] > 0)
    next_e = jnp.min(jnp.where(later, e_iota[None, :], N_EXPERTS), axis=1)
    next_e = jnp.where(next_e == N_EXPERTS, -1, next_e)
    next_blk = jnp.sum(jnp.where(blk_e[:, None] == e_iota[None, :], next_e[None, :], 0), axis=1).astype(jnp.int32)

    def live(i, nreal):
        return jnp.minimum(i, nreal[0] - 1)

    grid_spec = pltpu.PrefetchScalarGridSpec(
        num_scalar_prefetch=4,
        grid=(n_blocks,),
        in_specs=[pl.BlockSpec((MOE_BLOCK, D_MODEL), lambda i, be, nr, sl, nx: (live(i, nr), 0)),
                  pl.BlockSpec(memory_space=pl.ANY), pl.BlockSpec(memory_space=pl.ANY)],
        out_specs=pl.BlockSpec((MOE_BLOCK, D_MODEL), lambda i, be, nr, sl, nx: (i, 0)),
        scratch_shapes=[pltpu.VMEM((2, D_MODEL, 2 * EXPERT_HIDDEN), F32),
                        pltpu.VMEM((2, EXPERT_HIDDEN, D_MODEL), F32),
                        pltpu.VMEM((D_MODEL, 2 * EXPERT_HIDDEN), BF16),
                        pltpu.VMEM((EXPERT_HIDDEN, D_MODEL), BF16),
                        pltpu.SemaphoreType.DMA((2, 2))],
    )
    return pl.pallas_call(
        _gmm_kernel,
        grid_spec=grid_spec,
        out_shape=jax.ShapeDtypeStruct((n_rows, D_MODEL), BF16),
        compiler_params=_params(("arbitrary",)),
        name="grouped_experts",
    )(blk_e, n_real, slot, next_blk, xs, w13, w2)


COMBINE_TILE = 256


def _shared_kernel(*refs, tail):
    h_ref, x_ref, ye_ref, w_ref, gate_ref, w13_ref, w2_ref = refs[:7]
    gu = jnp.dot(h_ref[...], w13_ref[...], preferred_element_type=F32)
    hmid = _silu(gu[:, :EXPERT_HIDDEN]) * gu[:, EXPERT_HIDDEN:]
    acc = jnp.dot(hmid.astype(BF16), w2_ref[...], preferred_element_type=F32)
    wts = w_ref[...]
    for k in range(TOP_K):
        acc = acc + wts[:, k:k + 1] * ye_ref[k].astype(F32)
    x = x_ref[...] + gate_ref[0, 0] * acc
    if tail == "norm":
        g_ref, o_ref = refs[7:]
        ms = jnp.mean(x * x, axis=-1, keepdims=True)
        o_ref[...] = x * lax.rsqrt(ms + RMS_EPS) * g_ref[...]
    elif tail == "proj":
        g_ref, sc_ref, sh_ref, wn_ref, o_ref, u_ref = refs[7:]
        o_ref[...] = x
        hn = _norm_mod(x, g_ref[...], sc_ref[0, 0], sh_ref[0, 0])
        u_ref[...] = jnp.dot(hn.astype(BF16), wn_ref[...], preferred_element_type=F32)
    else:
        o_ref, = refs[7:]
        o_ref[...] = x


def _shared_and_combine(h_bf16, x, row0, t, y_exp, wts, mods4, k_gate, w13_bf16, w2_bf16, out_norm_g=None,
                        next_proj=None):
    tm = COMBINE_TILE
    b0 = row0 // tm
    row = lambda i: (i, 0)
    off = lambda i: (b0 + i, 0)
    out_specs = pl.BlockSpec((tm, D_MODEL), row)
    out_shape = jax.ShapeDtypeStruct((t, D_MODEL), F32)
    tail, extra_specs, extra_args = "plain", [], []
    if out_norm_g is not None:
        tail, extra_specs, extra_args = "norm", [_resident((1, D_MODEL))], [out_norm_g.reshape(1, D_MODEL)]
    elif next_proj is not None:
        g_n, mods_n, k_shift_n, k_scale_n, w_n = next_proj
        n = w_n.shape[1]
        tail = "proj"
        extra_specs = [_resident((1, D_MODEL)), _mod_spec(k_scale_n, tm, row0), _mod_spec(k_shift_n, tm, row0),
                       _resident((D_MODEL, n))]
        extra_args = [g_n.reshape(1, D_MODEL), mods_n, mods_n, w_n]
        out_specs = [out_specs, pl.BlockSpec((tm, n), row)]
        out_shape = [out_shape, jax.ShapeDtypeStruct((t, n), F32)]
    return pl.pallas_call(
        functools.partial(_shared_kernel, tail=tail),
        grid=(t // tm,),
        in_specs=[pl.BlockSpec((tm, D_MODEL), off), pl.BlockSpec((tm, D_MODEL), off),
                  pl.BlockSpec((TOP_K, tm, D_MODEL), lambda i: (0, i, 0)),
                  pl.BlockSpec((tm, TOP_K), row), _mod_spec(k_gate, tm, row0),
                  _resident((D_MODEL, 2 * EXPERT_HIDDEN)), _resident((EXPERT_HIDDEN, D_MODEL))] + extra_specs,
        out_specs=out_specs,
        out_shape=out_shape,
        compiler_params=_params(("parallel",)),
        name="shared_expert_combine",
    )(h_bf16, x, y_exp, wts, mods4, w13_bf16, w2_bf16, *extra_args)


ROUTE_TILE = 256
GROUP_SIZE = N_EXPERTS // N_EXPERT_GROUPS
NEG_INF = float("-inf")


def _first_argmax(v, index, sentinel):
    m = jnp.max(v, axis=0, keepdims=True)
    first = jnp.min(jnp.where(v == m, index, sentinel), axis=0, keepdims=True)
    return m, first


def _router_kernel(h_ref, wt_ref, rb_ref, tri_ref, idx_out, gate_out, rank_out, cnt_out, base_ref):
    tm = h_ref.shape[0]
    step = pl.program_id(0)

    @pl.when(jnp.logical_or(step == 0, step == T_CTX // tm))
    def _():
        base_ref[...] = jnp.zeros_like(base_ref)

    logits = lax.dot_general(wt_ref[...], h_ref[...], (((1,), (1,)), ((), ())), preferred_element_type=F32)
    scores = jax.nn.sigmoid(logits)
    choice = scores + rb_ref[...]

    member = lax.broadcasted_iota(jnp.int32, (GROUP_SIZE, tm), 0).astype(F32)
    group_scores = []
    for g in range(N_EXPERT_GROUPS):
        cg = choice[g * GROUP_SIZE:(g + 1) * GROUP_SIZE, :]
        m1, first = _first_argmax(cg, member, float(GROUP_SIZE))
        m2 = jnp.max(jnp.where(member == first, NEG_INF, cg), axis=0, keepdims=True)
        group_scores.append(m1 + m2)
    gs = jnp.concatenate(group_scores, axis=0)

    group = lax.broadcasted_iota(jnp.int32, (N_EXPERT_GROUPS, tm), 0).astype(F32)
    keep = jnp.zeros((N_EXPERT_GROUPS, tm), F32)
    for _ in range(TOPK_GROUPS):
        _, first = _first_argmax(gs, group, float(N_EXPERT_GROUPS))
        sel = group == first
        keep = jnp.where(sel, 1.0, keep)
        gs = jnp.where(sel, NEG_INF, gs)
    masked = jnp.concatenate(
        [jnp.where(keep[g:g + 1, :] > 0.0, choice[g * GROUP_SIZE:(g + 1) * GROUP_SIZE, :], NEG_INF)
         for g in range(N_EXPERT_GROUPS)], axis=0)

    expert = lax.broadcasted_iota(jnp.int32, (N_EXPERTS, tm), 0).astype(F32)
    onehot = jnp.zeros((N_EXPERTS, tm), F32)
    sels, idx_rows, gate_rows = [], [], []
    for _ in range(TOP_K):
        _, first = _first_argmax(masked, expert, float(N_EXPERTS))
        sel = expert == first
        sels.append(sel)
        idx_rows.append(first)
        gate_rows.append(jnp.sum(jnp.where(sel, scores, 0.0), axis=0, keepdims=True))
        masked = jnp.where(sel, NEG_INF, masked)
        onehot = jnp.where(sel, 1.0, onehot)
    gates = jnp.concatenate(gate_rows, axis=0)
    gate_out[...] = gates / jnp.sum(gates, axis=0, keepdims=True) * ROUTED_SCALE
    idx_out[...] = jnp.concatenate(idx_rows, axis=0).astype(jnp.int32)

    cum = jnp.dot(onehot.astype(BF16), tri_ref[...], preferred_element_type=F32) + base_ref[:, 0:1]
    rank_rows = [jnp.sum(jnp.where(sel, cum, 0.0), axis=0, keepdims=True) - 1.0 for sel in sels]
    rank_out[...] = jnp.concatenate(rank_rows, axis=0).astype(jnp.int32)
    base_ref[...] = jnp.broadcast_to(cum[:, tm - 1:tm], base_ref.shape)
    cnt_out[0] = base_ref[...].astype(jnp.int32)


def _router(h_bf16, router_w_t, router_bias):
    t = T_ALL
    tm = ROUTE_TILE
    n_ctx = T_CTX // tm
    tri = (jnp.arange(tm)[:, None] <= jnp.arange(tm)[None, :]).astype(BF16)
    choice_rows = pl.BlockSpec((TOP_K, tm), lambda i: (0, i))
    return pl.pallas_call(
        _router_kernel,
        grid=(t // tm,),
        in_specs=[pl.BlockSpec((tm, D_MODEL), lambda i: (i, 0)),
                  _resident((N_EXPERTS, D_MODEL)), _resident((N_EXPERTS, 1)), _resident((tm, tm))],
        out_specs=[choice_rows, choice_rows, choice_rows,
                   pl.BlockSpec((1, N_EXPERTS, 128), lambda i: (jnp.where(i < n_ctx, 0, 1), 0, 0))],
        out_shape=[jax.ShapeDtypeStruct((TOP_K, t), jnp.int32), jax.ShapeDtypeStruct((TOP_K, t), F32),
                   jax.ShapeDtypeStruct((TOP_K, t), jnp.int32),
                   jax.ShapeDtypeStruct((2, N_EXPERTS, 128), jnp.int32)],
        scratch_shapes=[pltpu.VMEM((N_EXPERTS, 128), F32)],
        compiler_params=_params(("arbitrary",)),
        name="router",
    )(h_bf16, router_w_t, router_bias.reshape(N_EXPERTS, 1), tri)


def _pos_kernel(idx_ref, rank_ref, off_ref, pos_out):
    tm = idx_ref.shape[1]
    expert = lax.broadcasted_iota(jnp.int32, (N_EXPERTS, tm), 0)
    rows = []
    for k in range(TOP_K):
        sel = expert == idx_ref[k:k + 1, :]
        rows.append(jnp.sum(jnp.where(sel, off_ref[...], 0.0), axis=0, keepdims=True))
    pos_out[...] = jnp.concatenate(rows, axis=0).astype(jnp.int32) + rank_ref[...]


def _positions(idx, rank, offsets):
    t = idx.shape[1]
    tm = 1024
    rows = pl.BlockSpec((TOP_K, tm), lambda i: (0, i))
    return pl.pallas_call(
        _pos_kernel, grid=(t // tm,),
        in_specs=[rows, rows, _resident((N_EXPERTS, 1))],
        out_specs=rows, out_shape=jax.ShapeDtypeStruct((TOP_K, t), jnp.int32),
        compiler_params=_params(("parallel",)), name="expert_positions",
    )(idx, rank, offsets.astype(F32).reshape(N_EXPERTS, 1))


def _moe_stream(x, row0, t, h_bf16, top_idx, gate, rank, counts, mods4, k_gate, exp_w13, exp_w2,
                sh_w13_bf16, sh_w2_bf16, out_norm_g, next_proj):
    padded = (counts + MOE_BLOCK - 1) // MOE_BLOCK * MOE_BLOCK
    e_iota = jnp.arange(N_EXPERTS)
    pend = jnp.sum(jnp.where(e_iota[:, None] <= e_iota[None, :], padded[:, None], 0), axis=0)
    pos = _positions(top_idx, rank, pend - padded)
    n_rows = -(-(t * TOP_K + N_EXPERTS * (MOE_BLOCK - 1)) // MOE_BLOCK) * MOE_BLOCK
    n_blocks = n_rows // MOE_BLOCK
    blk_start = jnp.arange(n_blocks, dtype=pend.dtype) * MOE_BLOCK
    blk_e = jnp.minimum(jnp.sum((pend[None, :] <= blk_start[:, None]).astype(jnp.int32), axis=1), N_EXPERTS - 1)
    n_real = (pend[N_EXPERTS - 1:] // MOE_BLOCK).astype(jnp.int32)
    n_assign = t * TOP_K
    tok = jnp.broadcast_to(jnp.arange(t, dtype=jnp.int32)[None, :], (TOP_K, t))
    _, sorted_tok = lax.sort_key_val(pos.reshape(-1), tok.reshape(-1))
    dense_end = jnp.sum(jnp.where(e_iota[:, None] <= e_iota[None, :], counts[:, None], 0), axis=0)
    shift_blk = ((pend - padded) - (dense_end - counts))[blk_e]
    dense = jnp.arange(n_rows, dtype=jnp.int32) - jnp.repeat(shift_blk, MOE_BLOCK)
    wrapped = (dense & (n_assign - 1)) if n_assign & (n_assign - 1) == 0 else (dense % n_assign)
    xs = h_bf16[sorted_tok[wrapped] + row0]
    ys = _grouped_experts(blk_e, n_real, padded, xs, exp_w13, exp_w2)
    y_exp = ys[pos.reshape(-1)].reshape(TOP_K, t, D_MODEL)
    return _shared_and_combine(h_bf16, x, row0, t, y_exp, gate.T, mods4, k_gate, sh_w13_bf16, sh_w2_bf16,
                               out_norm_g, next_proj)


def _moe_streams(x, h_bf16, mods4, k_gate, router_w, router_bias, exp_w13, exp_w2, sh_w13, sh_w2,
                 out_norm_g=None, next_proj=None):
    top_idx, gate, rank, counts = _router(h_bf16, router_w.T.astype(BF16), router_bias)
    sh13, sh2 = sh_w13.astype(BF16), sh_w2.astype(BF16)
    outs = []
    for s, (row0, t) in enumerate(((0, T_CTX), (T_CTX, T_LAT))):
        cols = slice(row0, row0 + t)
        outs.append(_moe_stream(x, row0, t, h_bf16, top_idx[:, cols], gate[:, cols], rank[:, cols], counts[s, :, 0],
                                mods4, k_gate, exp_w13, exp_w2, sh13, sh2, out_norm_g, next_proj))
    return tuple(outs)


HY_BLOCK_CTX, HY_TC_CTX = 256, 512
HY_BLOCK_LAT, HY_TC_LAT = 512, 128


def _hyena_filters(L, w1, b1, w2, b2, w3, sin_freq):
    t_norm = jnp.linspace(0.0, 1.0, L, dtype=F32)[:, None]
    omega = 2.0 * math.pi * jnp.arange(L, dtype=F32)[:, None] / L
    bands = jnp.linspace(1e-4, FILTER_BANDS - 1, FILTER_BANDS, dtype=F32)[None, :]
    z = jnp.concatenate([t_norm, jnp.cos(bands * omega), -jnp.sin(bands * omega)], axis=-1)
    h = jnp.sin(sin_freq * (z @ w1 + b1))
    h = jnp.sin(sin_freq * (h @ w2 + b2))
    h = (h @ w3).reshape(L, 2, HYENA_ORDER, HYENA_WIDTH)
    deltas = jnp.abs(jnp.linspace(math.log(DECAY_TARGET) / DECAY_PCT_LONG,
                                  math.log(DECAY_TARGET) / DECAY_PCT_SHORT, HYENA_WIDTH, dtype=F32))
    h = h * jnp.exp(-t_norm * deltas)[:, None, None, :]
    fwd, bwd = h[:, 0], h[:, 1] * (t_norm > 0.0)[:, :, None]
    norm = jnp.sum(jnp.abs(fwd), axis=0, keepdims=True) + jnp.sum(jnp.abs(bwd), axis=0, keepdims=True)
    return fwd / norm, bwd / norm


HY_ACC_ELEMS = 4096


def _dft_matrices(bk):
    k = jnp.arange(bk, dtype=jnp.int32)
    phase = ((2 * k[:, None] + 1) * k[None, :]) % (4 * bk)
    ang = phase.astype(F32) * (math.pi / (2 * bk))
    return jnp.cos(ang), jnp.sin(ang)


def _spectra_kernel(xf_ref, xb_ref, yf_ref, yb_ref, c1_ref, s1_ref, c2_ref, s2_ref, o_ref, *, nb):
    d = pl.program_id(1) - (nb - 1)
    x = jnp.where(d >= 0, xf_ref[0], xb_ref[0]).astype(BF16)
    y = jnp.where(d >= 1, yf_ref[0], yb_ref[0]).astype(BF16)
    cy = jnp.where(d == 0, c1_ref[...], c2_ref[...])
    sy = jnp.where(d == 0, s1_ref[...], s2_ref[...])
    re = jnp.dot(c1_ref[...], x, preferred_element_type=F32) + jnp.dot(cy, y, preferred_element_type=F32)
    im = jnp.dot(sy, y, preferred_element_type=F32) - jnp.dot(s1_ref[...], x, preferred_element_type=F32)
    o_ref[0, 0, 0] = re
    o_ref[0, 0, 1] = jnp.where(d >= 0, im, -im)


def _filter_spectra(fwd, bwd, bk, tc):
    L = fwd.shape[0]
    nb = L // bk
    n_ct = HYENA_WIDTH // tc
    cos, sin = _dft_matrices(bk)
    k = jnp.arange(bk, dtype=jnp.int32)
    phase = ((2 * k[:, None] + 1) * (bk - k[None, :])) % (4 * bk)
    ang = phase.astype(F32) * (math.pi / (2 * bk))
    live = (k[None, :] > 0).astype(F32)
    mats = [m.astype(BF16) for m in (cos, sin, jnp.cos(ang) * live, jnp.sin(ang) * live)]

    def seg(pick):
        return pl.BlockSpec((1, bk, tc), lambda o, di, ct: (pick(di - (nb - 1)), 0, o * n_ct + ct))

    mat = pl.BlockSpec((bk, bk), lambda o, di, ct: (0, 0), pipeline_mode=pl.Buffered(1))
    halves = [h.reshape(nb, bk, HYENA_ORDER * HYENA_WIDTH) for h in (fwd, bwd)]
    return pl.pallas_call(
        functools.partial(_spectra_kernel, nb=nb),
        grid=(HYENA_ORDER, 2 * nb - 1, n_ct),
        in_specs=[seg(lambda d: jnp.maximum(d, 0)), seg(lambda d: jnp.maximum(-d, 0)),
                  seg(lambda d: jnp.maximum(d - 1, 0)), seg(lambda d: jnp.maximum(-d - 1, 0)),
                  mat, mat, mat, mat],
        out_specs=pl.BlockSpec((1, 1, 2, bk, tc), lambda o, di, ct: (o, di, 0, 0, ct)),
        out_shape=jax.ShapeDtypeStruct((HYENA_ORDER, 2 * nb - 1, 2, bk, HYENA_WIDTH), F32),
        compiler_params=_params(("parallel", "parallel", "parallel")),
        name="filter_spectra",
    )(halves[0], halves[1], halves[0], halves[1], *mats)


def _hyena_kernel(v_ref, x1_ref, x2_ref, cw_ref, cb_ref, hb_ref, k0_ref, k1_ref, c_ref, s_ref, ct_ref, st_ref,
                  o_ref, vs_ref, z1_ref, x2s_ref, ur_ref, ui_ref, yr_ref, yi_ref, *, seq, bk, tc):
    nb = seq // bk
    ft = HY_ACC_ELEMS // tc
    row = lax.broadcasted_iota(jnp.int32, (seq, tc), 0)

    def short_conv(x, j):
        prev = jnp.where(row == 0, 0.0, pltpu.roll(x, 1, 0))
        nxt = jnp.where(row == seq - 1, 0.0, pltpu.roll(x, seq - 1, 0))
        return (prev * cw_ref[0, j:j + 1, :] + x * cw_ref[1, j:j + 1, :] + nxt * cw_ref[2, j:j + 1, :]
                + cb_ref[j:j + 1, :])

    vs_ref[...] = short_conv(v_ref[...], 0)
    z1_ref[...] = short_conv(x1_ref[...], 1)
    x2s_ref[...] = short_conv(x2_ref[...], 2)

    def long_conv(src_ref, k_ref, emit):
        for blk in range(nb):
            ub = src_ref[blk * bk:(blk + 1) * bk, :].astype(BF16)
            ur_ref[blk] = jnp.dot(c_ref[...], ub, preferred_element_type=F32)
            ui_ref[blk] = -jnp.dot(s_ref[...], ub, preferred_element_type=F32)
        for out_blk in range(nb):
            def acc_tile(f, carry, out_blk=out_blk):
                rows = pl.ds(pl.multiple_of(f * ft, ft), ft)
                ar = jnp.zeros((ft, tc), F32)
                ai = jnp.zeros((ft, tc), F32)
                for in_blk in range(nb):
                    d = out_blk - in_blk + nb - 1
                    kr, ki = k_ref[d, 0, rows, :], k_ref[d, 1, rows, :]
                    xr, xi = ur_ref[in_blk, rows, :], ui_ref[in_blk, rows, :]
                    ar = ar + (kr * xr - ki * xi)
                    ai = ai + (kr * xi + ki * xr)
                yr_ref[rows, :] = ar
                yi_ref[rows, :] = ai
                return carry

            lax.fori_loop(0, bk // ft, acc_tile, 0)
            y = jnp.dot(ct_ref[...], yr_ref[...].astype(BF16), preferred_element_type=F32)
            y = y - jnp.dot(st_ref[...], yi_ref[...].astype(BF16), preferred_element_type=F32)
            emit(slice(out_blk * bk, (out_blk + 1) * bk), y * (1.0 / bk))

    def emit_z1(rows, y):
        z1_ref[rows, :] = z1_ref[rows, :] * (y + hb_ref[0:1, :] * vs_ref[rows, :])

    def emit_out(rows, y):
        o_ref[rows, :] = (x2s_ref[rows, :] * (y + hb_ref[1:2, :] * z1_ref[rows, :])).astype(BF16)

    long_conv(vs_ref, k0_ref, emit_z1)
    long_conv(z1_ref, k1_ref, emit_out)


def _hyena(proj, row0, n_batch, seq, bk, tc, conv_w, conv_b, hy_bias, spectra, mats):
    nb = seq // bk
    n_ct = HYENA_WIDTH // tc
    col0 = QKV_WIDTH // tc
    per = HYENA_WIDTH // tc
    b0 = row0 // seq

    def data(which):
        return pl.BlockSpec((seq, tc), lambda ct, b: (b0 + b, col0 + which * per + ct))

    const2 = lambda shape: pl.BlockSpec(shape, lambda ct, b: (0, 0), pipeline_mode=pl.Buffered(1))
    kspec = pl.BlockSpec((2 * nb - 1, 2, bk, tc), lambda ct, b: (0, 0, 0, ct), pipeline_mode=pl.Buffered(1))
    cw = conv_w.reshape(3, HYENA_ORDER + 1, HYENA_WIDTH)
    cb = conv_b.reshape(HYENA_ORDER + 1, HYENA_WIDTH)
    return pl.pallas_call(
        functools.partial(_hyena_kernel, seq=seq, bk=bk, tc=tc),
        grid=(n_ct, n_batch),
        in_specs=[data(0), data(1), data(2),
                  pl.BlockSpec((3, HYENA_ORDER + 1, tc), lambda ct, b: (0, 0, ct)),
                  pl.BlockSpec((HYENA_ORDER + 1, tc), lambda ct, b: (0, ct)),
                  pl.BlockSpec((HYENA_ORDER, tc), lambda ct, b: (0, ct)),
                  kspec, kspec, const2((bk, bk)), const2((bk, bk)), const2((bk, bk)), const2((bk, bk))],
        out_specs=pl.BlockSpec((seq, tc), lambda ct, b: (b, ct)),
        out_shape=jax.ShapeDtypeStruct((n_batch * seq, HYENA_WIDTH), BF16),
        scratch_shapes=[pltpu.VMEM((seq, tc), F32), pltpu.VMEM((seq, tc), F32), pltpu.VMEM((seq, tc), F32),
                        pltpu.VMEM((nb, bk, tc), F32), pltpu.VMEM((nb, bk, tc), F32),
                        pltpu.VMEM((bk, tc), F32), pltpu.VMEM((bk, tc), F32)],
        compiler_params=_params(("parallel", "parallel")),
        name="hyena_seq%d" % seq,
    )(proj, proj, proj, cw, cb, hy_bias, spectra[0], spectra[1], *mats)


HY_SPECTRA_TC = 512


def _hyena_operands(L, bk, f_w1, f_b1, f_w2, f_b2, f_w3, sin_freq):
    fwd, bwd = _hyena_filters(L, f_w1, f_b1, f_w2, f_b2, f_w3, sin_freq)
    cos, sin = _dft_matrices(bk)
    mats = tuple(m.astype(BF16) for m in (cos, sin, cos.T, sin.T))
    return _filter_spectra(fwd, bwd, bk, HY_SPECTRA_TC), mats


S5_LANES = S5_GROUPS * S5_STATE
S5_CLUSTER_GROUPS = 16
S5_CLUSTERS = S5_GROUPS // S5_CLUSTER_GROUPS
S5_CLUSTER_CH = S5_CLUSTER_GROUPS * S5_GROUP_CH
S5_CLUSTER_LANES = S5_CLUSTER_GROUPS * S5_STATE
SCAN_SEQS = 8
SCAN_ROWS = 512


def _s5_kernel(u_ref, s0r_ref, s0i_ref, bw_ref, cwr_ref, cwi_ref, lam_ref, y_ref, sfr_ref, sfi_ref,
               sre, sim, car, cai, *, reverse):
    n_seq, tc, _ = u_ref.shape

    @pl.when(pl.program_id(1) == 0)
    def _():
        car[...] = jnp.zeros_like(car)
        cai[...] = jnp.zeros_like(cai)
        car[0:n_seq, :] = s0r_ref[0]
        cai[0:n_seq, :] = s0i_ref[0]

    u = u_ref[...]
    if n_seq < SCAN_SEQS:
        u = jnp.concatenate([u, jnp.zeros((SCAN_SEQS - n_seq, tc, S5_WIDTH), F32)], axis=0)
    ub = jnp.swapaxes(u, 0, 1).reshape(tc * SCAN_SEQS, S5_WIDTH).astype(BF16)
    for k in range(S5_CLUSTERS):
        bu = jnp.dot(ub[:, k * S5_CLUSTER_CH:(k + 1) * S5_CLUSTER_CH], bw_ref[k], preferred_element_type=F32)
        sre[:, k * S5_CLUSTER_LANES:(k + 1) * S5_CLUSTER_LANES] = bu[:, :S5_CLUSTER_LANES]
        sim[:, k * S5_CLUSTER_LANES:(k + 1) * S5_CLUSTER_LANES] = bu[:, S5_CLUSTER_LANES:]

    for k in range(S5_CLUSTERS):
        lanes = slice(k * S5_CLUSTER_LANES, (k + 1) * S5_CLUSTER_LANES)
        lr = jnp.broadcast_to(lam_ref[0:1, lanes], (SCAN_SEQS, S5_CLUSTER_LANES))
        li = jnp.broadcast_to(lam_ref[1:2, lanes], (SCAN_SEQS, S5_CLUSTER_LANES))

        def step(i, state, lanes=lanes, lr=lr, li=li):
            sr, si = state
            t = (tc - 1 - i) if reverse else i
            rows = pl.ds(pl.multiple_of(t * SCAN_SEQS, SCAN_SEQS), SCAN_SEQS)
            nr = (lr * sr - li * si) + sre[rows, lanes]
            ni = (lr * si + li * sr) + sim[rows, lanes]
            sre[rows, lanes] = nr
            sim[rows, lanes] = ni
            return nr, ni

        sr, si = lax.fori_loop(0, tc, step, (car[:, lanes], cai[:, lanes]), unroll=2)
        car[:, lanes] = sr
        cai[:, lanes] = si

    parts = []
    for k in range(S5_CLUSTERS):
        lanes = slice(k * S5_CLUSTER_LANES, (k + 1) * S5_CLUSTER_LANES)
        yk = jnp.dot(sre[:, lanes].astype(BF16), cwr_ref[k], preferred_element_type=F32)
        parts.append(yk + jnp.dot(sim[:, lanes].astype(BF16), cwi_ref[k], preferred_element_type=F32))
    y = jnp.concatenate(parts, axis=1).reshape(tc, SCAN_SEQS, S5_WIDTH)
    y_ref[...] = jnp.swapaxes(y, 0, 1)[0:n_seq]
    sfr_ref[0] = car[0:n_seq, :]
    sfi_ref[0] = cai[0:n_seq, :]


def _s5_direction(u, n_seq, seq, group, s0_re, s0_im, bw, cwr, cwi, lam, reverse):
    tc = SCAN_ROWS // SCAN_SEQS
    nc = seq // tc
    ng = n_seq // group

    def chunk(c):
        return (nc - 1 - c) if reverse else c

    state_spec = pl.BlockSpec((1, group, S5_LANES), lambda g, c: (g, 0, 0))
    const = lambda a: pl.BlockSpec(a.shape, lambda g, c: (0,) * a.ndim, pipeline_mode=pl.Buffered(1))
    y, f_re, f_im = pl.pallas_call(
        functools.partial(_s5_kernel, reverse=reverse),
        grid=(ng, nc),
        in_specs=[pl.BlockSpec((group, tc, S5_WIDTH), lambda g, c: (g, chunk(c), 0)),
                  state_spec, state_spec, const(bw), const(cwr), const(cwi), const(lam)],
        out_specs=[pl.BlockSpec((group, tc, S5_WIDTH), lambda g, c: (g, chunk(c), 0)),
                   state_spec, state_spec],
        out_shape=[jax.ShapeDtypeStruct((n_seq, seq, S5_WIDTH), F32),
                   jax.ShapeDtypeStruct((ng, group, S5_LANES), F32),
                   jax.ShapeDtypeStruct((ng, group, S5_LANES), F32)],
        scratch_shapes=[pltpu.VMEM((SCAN_ROWS, S5_LANES), F32), pltpu.VMEM((SCAN_ROWS, S5_LANES), F32),
                        pltpu.VMEM((SCAN_SEQS, S5_LANES), F32), pltpu.VMEM((SCAN_SEQS, S5_LANES), F32)],
        compiler_params=_params(("parallel", "arbitrary")),
        name="s5_bwd" if reverse else "s5_fwd",
    )(u.reshape(n_seq, seq, S5_WIDTH), s0_re.reshape(ng, group, S5_LANES), s0_im.reshape(ng, group, S5_LANES),
      bw, cwr, cwi, lam)
    return y.reshape(n_seq * seq, S5_WIDTH), f_re.reshape(n_seq, S5_LANES), f_im.reshape(n_seq, S5_LANES)


def _s5_weights(a_re, a_im, log_dt, b_re, b_im, c_re, c_im):
    lam = lax.complex(a_re, a_im)
    lam_bar = jnp.exp(lam * jnp.exp(log_dt)[..., None])
    b_bar = ((lam_bar - 1.0) / lam)[..., None] * lax.complex(b_re, b_im)
    eye = jnp.eye(S5_CLUSTER_GROUPS, dtype=F32)
    out = []
    for d in range(2):
        def cluster_in(w):
            w = w.reshape(S5_CLUSTERS, S5_CLUSTER_GROUPS, S5_STATE, S5_GROUP_CH)
            return jnp.einsum('ab,kapc->kacbp', eye, w).reshape(S5_CLUSTERS, S5_CLUSTER_CH, S5_CLUSTER_LANES)

        def cluster_out(w):
            w = w.reshape(S5_CLUSTERS, S5_CLUSTER_GROUPS, S5_GROUP_CH, S5_STATE)
            return jnp.einsum('ab,kacp->kapbc', eye, w).reshape(S5_CLUSTERS, S5_CLUSTER_LANES, S5_CLUSTER_CH)

        bw = jnp.concatenate([cluster_in(jnp.real(b_bar[d])), cluster_in(jnp.imag(b_bar[d]))], axis=-1)
        cwr = cluster_out(c_re[d])
        cwi = cluster_out(-c_im[d])
        lam_d = lam_bar[d].reshape(S5_LANES)
        lam2 = jnp.stack([jnp.real(lam_d), jnp.imag(lam_d)]).astype(F32)
        out.append((bw.astype(BF16), cwr.astype(BF16), cwi.astype(BF16), lam2))
    return out


def _rope_tables(L):
    n_rows = L // GRID_W
    row_idx = jnp.repeat(jnp.arange(n_rows, dtype=F32), GRID_W)
    col_idx = (jnp.arange(L) % GRID_W).astype(F32)
    inv = ROPE_THETA ** (-jnp.arange(0, ROPE_AXIS_DIM, 2, dtype=F32) / ROPE_AXIS_DIM)
    ang = jnp.concatenate([row_idx[:, None] * inv, col_idx[:, None] * inv], axis=-1)
    cos, sin = jnp.cos(ang), jnp.sin(ang)
    cos_full = jnp.repeat(cos, 2, axis=-1)
    sin_signed = jnp.stack([-sin, sin], axis=-1).reshape(L, HEAD_DIM)
    return cos_full, sin_signed


def kernel(x_prompt, x_sample, cache_k_l0, cache_v_l0, state_s5_re_l1, state_s5_im_l1, c, c_ctx,
           ada_w_l0, ada_b_l0, norm1_l0, norm2_l0,
           w_in_l0, w_out_l0, q_norm_l0, k_norm_l0, hy_conv_w_l0, hy_conv_b_l0,
           hy_ffn_w1_l0, hy_ffn_b1_l0, hy_ffn_w2_l0, hy_ffn_b2_l0, hy_ffn_w3_l0, hy_sin_freq_l0, hy_bias_l0,
           router_l0, router_bias_l0, exp_w13_l0, exp_w2_l0, shared_w13_l0, shared_w2_l0,
           ada_w_l1, ada_b_l1, norm1_l1, norm2_l1,
           w_in_l1, s5_a_re_l1, s5_a_im_l1, s5_log_dt_l1, s5_b_re_l1, s5_b_im_l1,
           s5_c_re_l1, s5_c_im_l1, s5_d_l1, w_glu_l1,
           router_l1, router_bias_l1, exp_w13_l1, exp_w2_l1, shared_w13_l1, shared_w2_l1,
           final_norm):
    x = (x_prompt.reshape(T_CTX, D_MODEL), x_sample.reshape(T_LAT, D_MODEL))
    cond = jnp.concatenate([c_ctx[None, :], c, jnp.zeros((N_COND - 1 - DEC_BATCH, D_MODEL), F32)], axis=0)

    mods = _ada_mods(cond, ada_w_l0, ada_b_l0)
    proj = _norm_mod_matmul(x, norm1_l0, mods, 0, 1, w_in_l0.astype(BF16))

    q_c, k_c, v_c, new_k = _qk_prep(proj, 0, T_CTX, q_norm_l0, k_norm_l0, None)
    attn_c = _attention(q_c, k_c, v_c, BATCH, SEQ, None)
    q_l, k_l, v_l = _qk_prep(proj, T_CTX, T_LAT, q_norm_l0, k_norm_l0, _rope_tables(DEC_SEQ))
    ctx_kv = (cache_k_l0.reshape(DEC_BATCH * PAST_LEN, KV_WIDTH).astype(BF16),
              cache_v_l0.reshape(DEC_BATCH * PAST_LEN, KV_WIDTH).astype(BF16))
    attn_l = _attention(q_l, k_l, v_l, DEC_BATCH, DEC_SEQ, ctx_kv)

    hy_f = (hy_ffn_w1_l0, hy_ffn_b1_l0, hy_ffn_w2_l0, hy_ffn_b2_l0, hy_ffn_w3_l0, hy_sin_freq_l0)
    spec_c, mats_c = _hyena_operands(SEQ, HY_BLOCK_CTX, *hy_f)
    z_c = _hyena(proj, 0, BATCH, SEQ, HY_BLOCK_CTX, HY_TC_CTX, hy_conv_w_l0, hy_conv_b_l0, hy_bias_l0,
                 spec_c, mats_c)
    spec_l, mats_l = _hyena_operands(DEC_SEQ, HY_BLOCK_LAT, *hy_f)
    z_l = _hyena(proj, T_CTX, DEC_BATCH, DEC_SEQ, HY_BLOCK_LAT, HY_TC_LAT, hy_conv_w_l0, hy_conv_b_l0, hy_bias_l0,
                 spec_l, mats_l)

    w_out = w_out_l0.astype(BF16)
    x, h_moe = _out_proj((attn_c, attn_l), (z_c, z_l), x, mods, 2, w_out[:Q_WIDTH], w_out[Q_WIDTH:],
                         norm2_l0, 3, 4)
    mods_l1 = _ada_mods(cond, ada_w_l1, ada_b_l1)
    (x_c, u_c), (x_l, u_l) = _moe_streams(x, h_moe, mods, 5, router_l0, router_bias_l0, exp_w13_l0,
                                          exp_w2_l0, shared_w13_l0, shared_w2_l0,
                                          next_proj=(norm1_l1, mods_l1, 0, 1, w_in_l1.astype(BF16)))
    x, u, mods = (x_c, x_l), (u_c, u_l), mods_l1

    new_v = proj[:T_CTX, Q_WIDTH + KV_WIDTH:QKV_WIDTH]

    s5_w = _s5_weights(s5_a_re_l1, s5_a_im_l1, s5_log_dt_l1, s5_b_re_l1, s5_b_im_l1, s5_c_re_l1, s5_c_im_l1)
    zero_state = jnp.zeros((BATCH, S5_LANES), F32)
    ys, finals = [], []
    for d in range(2):
        y_c, f_re, f_im = _s5_direction(u_c, BATCH, SEQ, SCAN_SEQS, zero_state, zero_state, *s5_w[d],
                                        reverse=d == 1)
        y_l, _, _ = _s5_direction(u_l, DEC_BATCH, DEC_SEQ, DEC_BATCH,
                                  state_s5_re_l1[:, d].reshape(DEC_BATCH, S5_LANES),
                                  state_s5_im_l1[:, d].reshape(DEC_BATCH, S5_LANES), *s5_w[d], reverse=d == 1)
        ys.append((y_c, y_l))
        finals.append((f_re.reshape(BATCH, S5_GROUPS, S5_STATE), f_im.reshape(BATCH, S5_GROUPS, S5_STATE)))
    s_re = jnp.stack([finals[0][0], finals[1][0]], axis=1)
    s_im = jnp.stack([finals[0][1], finals[1][1]], axis=1)
    x, h_moe = _glu_proj(ys[0], ys[1], u, s5_d_l1, x, mods, 2, w_glu_l1.astype(BF16), norm2_l1, 3, 4)
    y_c, y_l = _moe_streams(x, h_moe, mods, 5, router_l1, router_bias_l1, exp_w13_l1, exp_w2_l1,
                            shared_w13_l1, shared_w2_l1, out_norm_g=final_norm)

    return (y_c.reshape(BATCH, SEQ, D_MODEL),
            y_l.reshape(DEC_BATCH, DEC_SEQ, D_MODEL),
            new_k.reshape(BATCH, SEQ, N_KV_HEADS, HEAD_DIM),
            new_v.reshape(BATCH, SEQ, N_KV_HEADS, HEAD_DIM),
            s_re, s_im)
```

```python
import functools
import math

import jax
import jax.numpy as jnp
from jax import lax
from jax.experimental import pallas as pl
from jax.experimental.pallas import tpu as pltpu

F32 = jnp.float32
BF16 = jnp.bfloat16

D_MODEL = 2048
BATCH = 32
SEQ = 256
DEC_BATCH = 4
DEC_SEQ = 4096
PAST_LEN = 256
GRID_W = 64
RMS_EPS = 1e-6
HEAD_DIM = 128
N_HEADS = 8
N_KV_HEADS = 2
GQA_GROUP = N_HEADS // N_KV_HEADS
Q_WIDTH = N_HEADS * HEAD_DIM
KV_WIDTH = N_KV_HEADS * HEAD_DIM
QKV_WIDTH = Q_WIDTH + 2 * KV_WIDTH
ROPE_THETA = 10000.0
ROPE_AXIS_DIM = HEAD_DIM // 2
HYENA_WIDTH = D_MODEL // 2
HYENA_ORDER = 2
FILTER_BANDS = 16
DECAY_TARGET = 1e-2
DECAY_PCT_SHORT = 0.3
DECAY_PCT_LONG = 1.5
S5_WIDTH = D_MODEL // 2
S5_GROUP_CH = 16
S5_GROUPS = S5_WIDTH // S5_GROUP_CH
S5_STATE = 64
N_EXPERTS = 64
TOP_K = 8
N_EXPERT_GROUPS = 8
TOPK_GROUPS = 4
EXPERT_HIDDEN = 512
ROUTED_SCALE = 2.5
MOE_BLOCK = 256

T_CTX = BATCH * SEQ
T_LAT = DEC_BATCH * DEC_SEQ
T_ALL = T_CTX + T_LAT
N_COND = 8

ROW_TILE = 256
V7X_VMEM_LIMIT_BYTES = 56 * 1024 * 1024


def _params(semantics):
    return pltpu.CompilerParams(dimension_semantics=semantics, vmem_limit_bytes=V7X_VMEM_LIMIT_BYTES)


def _cond_row(i, tm, row0=0):
    tok = row0 + i * tm
    return jnp.where(tok < T_CTX, 0, 1 + (tok - T_CTX) // DEC_SEQ)


def _mod_spec(k, tm, row0=0):
    return pl.BlockSpec((1, 1, 1, D_MODEL), lambda i: (_cond_row(i, tm, row0), k, 0, 0))


def _resident(shape):
    nd = len(shape)
    return pl.BlockSpec(shape, lambda i: (0,) * nd, pipeline_mode=pl.Buffered(1))


def _silu(x):
    return x * jax.nn.sigmoid(x)


def _norm_mod(x, g, sc, sh):
    ms = jnp.mean(x * x, axis=-1, keepdims=True)
    return (x * lax.rsqrt(ms + RMS_EPS) * g) * (1.0 + sc) + sh


def _ada_kernel(c_ref, w_ref, b_ref, o_ref):
    a = _silu(c_ref[...]).astype(BF16)
    o_ref[...] = jnp.dot(a, w_ref[...].astype(BF16), preferred_element_type=F32) + b_ref[...]


def _ada_mods(cond, w, b):
    n = w.shape[1]
    tn = 2048
    out = pl.pallas_call(
        _ada_kernel,
        grid=(n // tn,),
        in_specs=[pl.BlockSpec((N_COND, D_MODEL), lambda j: (0, 0)),
                  pl.BlockSpec((D_MODEL, tn), lambda j: (0, j)),
                  pl.BlockSpec((1, tn), lambda j: (0, j))],
        out_specs=pl.BlockSpec((N_COND, tn), lambda j: (0, j)),
        out_shape=jax.ShapeDtypeStruct((N_COND, n), F32),
        compiler_params=_params(("parallel",)),
        name="ada_mods",
    )(cond, w, b.reshape(1, n))
    return out.reshape(N_COND, 6, 1, D_MODEL)


def _split_row_specs(width, tm):
    n_ctx = T_CTX // tm
    return [pl.BlockSpec((tm, width), lambda i: (jnp.minimum(i, n_ctx - 1), 0)),
            pl.BlockSpec((tm, width), lambda i: (jnp.maximum(i - n_ctx, 0), 0))]


def _pick_rows(ctx_ref, lat_ref):
    return jnp.where(pl.program_id(0) < T_CTX // ctx_ref.shape[0], ctx_ref[...], lat_ref[...])


def _nmm_kernel(*refs, split):
    if split:
        xc_ref, xl_ref, g_ref, sc_ref, sh_ref, w_ref, o_ref = refs
        x = _pick_rows(xc_ref, xl_ref)
    else:
        x_ref, g_ref, sc_ref, sh_ref, w_ref, o_ref = refs
        x = x_ref[...]
    h = _norm_mod(x, g_ref[...], sc_ref[0, 0], sh_ref[0, 0])
    o_ref[...] = jnp.dot(h.astype(BF16), w_ref[...], preferred_element_type=F32)


def _norm_mod_matmul(x, g, mods4, k_shift, k_scale, w_bf16):
    n = w_bf16.shape[1]
    tm = ROW_TILE
    split = isinstance(x, tuple)
    x_args = list(x) if split else [x]
    x_specs = _split_row_specs(D_MODEL, tm) if split else [pl.BlockSpec((tm, D_MODEL), lambda i: (i, 0))]
    return pl.pallas_call(
        functools.partial(_nmm_kernel, split=split),
        grid=(T_ALL // tm,),
        in_specs=x_specs + [_resident((1, D_MODEL)), _mod_spec(k_scale, tm), _mod_spec(k_shift, tm),
                            _resident((D_MODEL, n))],
        out_specs=pl.BlockSpec((tm, n), lambda i: (i, 0)),
        out_shape=jax.ShapeDtypeStruct((T_ALL, n), F32),
        compiler_params=_params(("parallel",)),
        name="norm_mod_matmul",
    )(*x_args, g.reshape(1, D_MODEL), mods4, mods4, w_bf16)


def _head_norm(xh, g):
    ms = jnp.mean(xh * xh, axis=-1, keepdims=True)
    return xh * lax.rsqrt(ms + RMS_EPS) * g


def _rope(y, cos, sin_signed):
    lane = lax.broadcasted_iota(jnp.int32, y.shape, 1)
    partner = jnp.where(lane % 2 == 0, pltpu.roll(y, HEAD_DIM - 1, 1), pltpu.roll(y, 1, 1))
    return y * cos + partner * sin_signed


def _qkprep_ctx_kernel(qkv_ref, qg_ref, kg_ref, q_out, kb_out, vb_out, kf_out):
    for h in range(N_HEADS):
        sl = slice(h * HEAD_DIM, (h + 1) * HEAD_DIM)
        q_out[:, sl] = _head_norm(qkv_ref[:, sl], qg_ref[...]).astype(BF16)
    for j in range(N_KV_HEADS):
        k = _head_norm(qkv_ref[:, Q_WIDTH + j * HEAD_DIM:Q_WIDTH + (j + 1) * HEAD_DIM], kg_ref[...])
        kf_out[:, j * HEAD_DIM:(j + 1) * HEAD_DIM] = k
        kb_out[:, j * HEAD_DIM:(j + 1) * HEAD_DIM] = k.astype(BF16)
    vb_out[...] = qkv_ref[:, Q_WIDTH + KV_WIDTH:QKV_WIDTH].astype(BF16)


def _qkprep_lat_kernel(qkv_ref, qg_ref, kg_ref, cos_ref, sin_ref, q_out, kb_out, vb_out):
    cos, sin = cos_ref[...], sin_ref[...]
    for h in range(N_HEADS):
        sl = slice(h * HEAD_DIM, (h + 1) * HEAD_DIM)
        q_out[:, sl] = _rope(_head_norm(qkv_ref[:, sl], qg_ref[...]), cos, sin).astype(BF16)
    for j in range(N_KV_HEADS):
        k = _head_norm(qkv_ref[:, Q_WIDTH + j * HEAD_DIM:Q_WIDTH + (j + 1) * HEAD_DIM], kg_ref[...])
        kb_out[:, j * HEAD_DIM:(j + 1) * HEAD_DIM] = _rope(k, cos, sin).astype(BF16)
    vb_out[...] = qkv_ref[:, Q_WIDTH + KV_WIDTH:QKV_WIDTH].astype(BF16)


def _qk_prep(proj, row0, t, q_norm, k_norm, rope_tabs):
    tm = ROW_TILE
    b0 = row0 // tm
    qkv_spec = pl.BlockSpec((tm, QKV_WIDTH), lambda i: (b0 + i, 0))
    gain = _resident((1, HEAD_DIM))
    outs = [pl.BlockSpec((tm, Q_WIDTH), lambda i: (i, 0)),
            pl.BlockSpec((tm, KV_WIDTH), lambda i: (i, 0)),
            pl.BlockSpec((tm, KV_WIDTH), lambda i: (i, 0))]
    shapes = [jax.ShapeDtypeStruct((t, Q_WIDTH), BF16),
              jax.ShapeDtypeStruct((t, KV_WIDTH), BF16),
              jax.ShapeDtypeStruct((t, KV_WIDTH), BF16)]
    qg, kg = q_norm.reshape(1, HEAD_DIM), k_norm.reshape(1, HEAD_DIM)
    if rope_tabs is None:
        return pl.pallas_call(
            _qkprep_ctx_kernel, grid=(t // tm,),
            in_specs=[qkv_spec, gain, gain],
            out_specs=outs + [pl.BlockSpec((tm, KV_WIDTH), lambda i: (i, 0))],
            out_shape=shapes + [jax.ShapeDtypeStruct((t, KV_WIDTH), F32)],
            compiler_params=_params(("parallel",)), name="qk_prep_ctx",
        )(proj, qg, kg)
    cos, sin = rope_tabs
    nq = DEC_SEQ // tm
    tab = pl.BlockSpec((tm, HEAD_DIM), lambda i: (i % nq, 0))
    return pl.pallas_call(
        _qkprep_lat_kernel, grid=(t // tm,),
        in_specs=[qkv_spec, gain, gain, tab, tab],
        out_specs=outs, out_shape=shapes,
        compiler_params=_params(("parallel",)), name="qk_prep_lat",
    )(proj, qg, kg, cos, sin)


def _attn_kernel(*refs, with_ctx):
    if with_ctx:
        q_ref, k_ref, v_ref, ck_ref, cv_ref, o_ref = refs
    else:
        q_ref, k_ref, v_ref, o_ref = refs
    c = HEAD_DIM ** -0.5 * math.log2(math.e)
    nt = (((1,), (1,)), ((), ()))
    for hh in range(ATTN_HEADS_PER_STEP):
        cols = slice(hh * HEAD_DIM, (hh + 1) * HEAD_DIM)
        q = q_ref[:, cols]
        s = lax.dot_general(q, k_ref[...], nt, preferred_element_type=F32)
        m = jnp.max(s, axis=-1, keepdims=True)
        if with_ctx:
            sc = lax.dot_general(q, ck_ref[...], nt, preferred_element_type=F32)
            m = jnp.maximum(m, jnp.max(sc, axis=-1, keepdims=True))
        p = jnp.exp2((s - m) * c)
        l = jnp.sum(p, axis=-1, keepdims=True)
        o = jnp.dot(p.astype(BF16), v_ref[...], preferred_element_type=F32)
        if with_ctx:
            pc = jnp.exp2((sc - m) * c)
            l = l + jnp.sum(pc, axis=-1, keepdims=True)
            o = o + jnp.dot(pc.astype(BF16), cv_ref[...], preferred_element_type=F32)
        o_ref[:, cols] = (o / l).astype(BF16)


ATTN_HEADS_PER_STEP = 1
ATTN_Q_TILE = 256


def _attention(q, k, v, n_batch, seq, ctx_kv):
    tq = ATTN_Q_TILE
    nq = seq // tq
    t = n_batch * seq
    hw = ATTN_HEADS_PER_STEP * HEAD_DIM
    kv_of = GQA_GROUP // ATTN_HEADS_PER_STEP
    q_spec = pl.BlockSpec((tq, hw), lambda b, h, i: (b * nq + i, h))
    kv_spec = pl.BlockSpec((seq, HEAD_DIM), lambda b, h, i: (b, h // kv_of))
    in_specs = [q_spec, kv_spec, kv_spec]
    args = [q, k, v]
    if ctx_kv is not None:
        c_spec = pl.BlockSpec((PAST_LEN, HEAD_DIM), lambda b, h, i: (b, h // kv_of))
        in_specs += [c_spec, c_spec]
        args += list(ctx_kv)
    return pl.pallas_call(
        functools.partial(_attn_kernel, with_ctx=ctx_kv is not None),
        grid=(n_batch, N_HEADS // ATTN_HEADS_PER_STEP, nq),
        in_specs=in_specs,
        out_specs=pl.BlockSpec((tq, hw), lambda b, h, i: (b * nq + i, h)),
        out_shape=jax.ShapeDtypeStruct((t, Q_WIDTH), BF16),
        compiler_params=_params(("parallel", "parallel", "parallel")),
        name="attention_lat" if ctx_kv is not None else "attention_ctx",
    )(*args)


def _outproj_kernel(ac_ref, al_ref, zc_ref, zl_ref, xc_ref, xl_ref, gate_ref, wa_ref, wz_ref, o_ref):
    acc = jnp.dot(_pick_rows(ac_ref, al_ref), wa_ref[...], preferred_element_type=F32)
    acc = acc + jnp.dot(_pick_rows(zc_ref, zl_ref), wz_ref[...], preferred_element_type=F32)
    o_ref[...] = _pick_rows(xc_ref, xl_ref) + gate_ref[0, 0] * acc


def _out_proj(attn, z, x, mods4, k_gate, wa_bf16, wz_bf16):
    tm = ROW_TILE
    return pl.pallas_call(
        _outproj_kernel,
        grid=(T_ALL // tm,),
        in_specs=(_split_row_specs(Q_WIDTH, tm) + _split_row_specs(HYENA_WIDTH, tm) + _split_row_specs(D_MODEL, tm)
                  + [_mod_spec(k_gate, tm), _resident((Q_WIDTH, D_MODEL)), _resident((HYENA_WIDTH, D_MODEL))]),
        out_specs=pl.BlockSpec((tm, D_MODEL), lambda i: (i, 0)),
        out_shape=jax.ShapeDtypeStruct((T_ALL, D_MODEL), F32),
        compiler_params=_params(("parallel",)),
        name="out_proj",
    )(*attn, *z, *x, mods4, wa_bf16, wz_bf16)


def _gelu_tanh(x):
    return 0.5 * x * (1.0 + jnp.tanh(math.sqrt(2.0 / math.pi) * (x + 0.044715 * (x * x * x))))


def _glu_kernel(yfc_ref, yfl_ref, ybc_ref, ybl_ref, uc_ref, ul_ref, d_ref, xc_ref, xl_ref, gate_ref, w_ref, o_ref):
    y = (_pick_rows(yfc_ref, yfl_ref) + _pick_rows(ybc_ref, ybl_ref)) + d_ref[...] * _pick_rows(uc_ref, ul_ref)
    ag = jnp.dot(_gelu_tanh(y).astype(BF16), w_ref[...], preferred_element_type=F32)
    a, g = ag[:, :D_MODEL], ag[:, D_MODEL:]
    o_ref[...] = _pick_rows(xc_ref, xl_ref) + gate_ref[0, 0] * (a * jax.nn.sigmoid(g))


def _glu_proj(y_fwd, y_bwd, u, d_skip, x, mods4, k_gate, w_bf16):
    tm = ROW_TILE
    return pl.pallas_call(
        _glu_kernel,
        grid=(T_ALL // tm,),
        in_specs=(_split_row_specs(S5_WIDTH, tm) + _split_row_specs(S5_WIDTH, tm) + _split_row_specs(S5_WIDTH, tm)
                  + [_resident((1, S5_WIDTH))]
                  + _split_row_specs(D_MODEL, tm)
                  + [_mod_spec(k_gate, tm), _resident((S5_WIDTH, 2 * D_MODEL))]),
        out_specs=pl.BlockSpec((tm, D_MODEL), lambda i: (i, 0)),
        out_shape=jax.ShapeDtypeStruct((T_ALL, D_MODEL), F32),
        compiler_params=_params(("parallel",)),
        name="glu_proj",
    )(*y_fwd, *y_bwd, *u, d_skip.reshape(1, S5_WIDTH), *x, mods4, w_bf16)


def _gmm_kernel(be_ref, nreal_ref, slot_ref, next_ref, xs_ref, w13_hbm, w2_hbm, o_ref,
                w13_f, w2_f, w13_b, w2_b, sems):
    i = pl.program_id(0)
    real = i < nreal_ref[0]
    fresh = jnp.logical_and(real, jnp.logical_or(i == 0, be_ref[i] != be_ref[jnp.maximum(i - 1, 0)]))

    half = D_MODEL // 2

    def weight_copies(expert, slot):
        return (pltpu.make_async_copy(w13_hbm.at[expert, 0:half], w13_f.at[slot, 0:half], sems.at[slot, 0]),
                pltpu.make_async_copy(w13_hbm.at[expert, half:D_MODEL], w13_f.at[slot, half:D_MODEL], sems.at[slot, 1]),
                pltpu.make_async_copy(w2_hbm.at[expert], w2_f.at[slot], sems.at[slot, 2]))

    def start_all(expert, slot):
        for n, cp in enumerate(weight_copies(expert, slot)):
            cp.start(priority=n % 2)

    @pl.when(fresh)
    def _():
        expert, slot, nxt = be_ref[i], slot_ref[i], next_ref[i]

        @pl.when(i == 0)
        def _():
            start_all(expert, slot)

        for cp in weight_copies(expert, slot):
            cp.wait()
        w13_b[...] = w13_f[slot].astype(BF16)
        w2_b[...] = w2_f[slot].astype(BF16)

        @pl.when(nxt >= 0)
        def _():
            start_all(nxt, 1 - slot)

    @pl.when(real)
    def _():
        gu = jnp.dot(xs_ref[...], w13_b[...], preferred_element_type=F32)
        hmid = _silu(gu[:, :EXPERT_HIDDEN]) * gu[:, EXPERT_HIDDEN:]
        o_ref[...] = jnp.dot(hmid.astype(BF16), w2_b[...], preferred_element_type=F32).astype(BF16)

    @pl.when(jnp.logical_not(real))
    def _():
        o_ref[...] = jnp.zeros_like(o_ref)


def _grouped_experts(blk_e, n_real, padded, xs, w13, w2):
    n_rows = xs.shape[0]
    n_blocks = n_rows // MOE_BLOCK
    b_iota = jnp.arange(n_blocks)
    first = jnp.concatenate([jnp.ones((1,), jnp.int32), (blk_e[1:] != blk_e[:-1]).astype(jnp.int32)])
    run = jnp.sum(jnp.where(b_iota[:, None] <= b_iota[None, :], first[:, None], 0), axis=0) - 1
    slot = (run % 2).astype(jnp.int32)
    e_iota = jnp.arange(N_EXPERTS)
    later = jnp.logical_and(e_iota[None, :] > e_iota[:, None], padded[None, :] > 0)
    next_e = jnp.min(jnp.where(later, e_iota[None, :], N_EXPERTS), axis=1)
    next_e = jnp.where(next_e == N_EXPERTS, -1, next_e)
    next_blk = jnp.sum(jnp.where(blk_e[:, None] == e_iota[None, :], next_e[None, :], 0), axis=1).astype(jnp.int32)

    def live(i, nreal):
        return jnp.minimum(i, nreal[0] - 1)

    grid_spec = pltpu.PrefetchScalarGridSpec(
        num_scalar_prefetch=4,
        grid=(n_blocks,),
        in_specs=[pl.BlockSpec((MOE_BLOCK, D_MODEL), lambda i, be, nr, sl, nx: (live(i, nr), 0)),
                  pl.BlockSpec(memory_space=pl.ANY), pl.BlockSpec(memory_space=pl.ANY)],
        out_specs=pl.BlockSpec((MOE_BLOCK, D_MODEL), lambda i, be, nr, sl, nx: (i, 0)),
        scratch_shapes=[pltpu.VMEM((2, D_MODEL, 2 * EXPERT_HIDDEN), F32),
                        pltpu.VMEM((2, EXPERT_HIDDEN, D_MODEL), F32),
                        pltpu.VMEM((D_MODEL, 2 * EXPERT_HIDDEN), BF16),
                        pltpu.VMEM((EXPERT_HIDDEN, D_MODEL), BF16),
                        pltpu.SemaphoreType.DMA((2, 3))],
    )
    return pl.pallas_call(
        _gmm_kernel,
        grid_spec=grid_spec,
        out_shape=jax.ShapeDtypeStruct((n_rows, D_MODEL), BF16),
        compiler_params=_params(("arbitrary",)),
        name="grouped_experts",
    )(blk_e, n_real, slot, next_blk, xs, w13, w2)


COMBINE_TILE = 256


def _shared_kernel(*refs, tail):
    h_ref, x_ref, ye_ref, w_ref, gate_ref, w13_ref, w2_ref = refs[:7]
    gu = jnp.dot(h_ref[...], w13_ref[...], preferred_element_type=F32)
    hmid = _silu(gu[:, :EXPERT_HIDDEN]) * gu[:, EXPERT_HIDDEN:]
    acc = jnp.dot(hmid.astype(BF16), w2_ref[...], preferred_element_type=F32)
    wts = w_ref[...]
    for k in range(TOP_K):
        acc = acc + wts[:, k:k + 1] * ye_ref[k].astype(F32)
    x = x_ref[...] + gate_ref[0, 0] * acc
    if tail == "norm":
        g_ref, o_ref = refs[7:]
        ms = jnp.mean(x * x, axis=-1, keepdims=True)
        o_ref[...] = x * lax.rsqrt(ms + RMS_EPS) * g_ref[...]
    elif tail == "proj":
        g_ref, sc_ref, sh_ref, wn_ref, o_ref, u_ref = refs[7:]
        o_ref[...] = x
        hn = _norm_mod(x, g_ref[...], sc_ref[0, 0], sh_ref[0, 0])
        u_ref[...] = jnp.dot(hn.astype(BF16), wn_ref[...], preferred_element_type=F32)
    else:
        o_ref, = refs[7:]
        o_ref[...] = x


def _shared_and_combine(h_bf16, x, row0, t, y_exp, wts, mods4, k_gate, w13_bf16, w2_bf16, out_norm_g=None,
                        next_proj=None):
    tm = COMBINE_TILE
    b0 = row0 // tm
    row = lambda i: (i, 0)
    off = lambda i: (b0 + i, 0)
    out_specs = pl.BlockSpec((tm, D_MODEL), row)
    out_shape = jax.ShapeDtypeStruct((t, D_MODEL), F32)
    tail, extra_specs, extra_args = "plain", [], []
    if out_norm_g is not None:
        tail, extra_specs, extra_args = "norm", [_resident((1, D_MODEL))], [out_norm_g.reshape(1, D_MODEL)]
    elif next_proj is not None:
        g_n, mods_n, k_shift_n, k_scale_n, w_n = next_proj
        n = w_n.shape[1]
        tail = "proj"
        extra_specs = [_resident((1, D_MODEL)), _mod_spec(k_scale_n, tm, row0), _mod_spec(k_shift_n, tm, row0),
                       _resident((D_MODEL, n))]
        extra_args = [g_n.reshape(1, D_MODEL), mods_n, mods_n, w_n]
        out_specs = [out_specs, pl.BlockSpec((tm, n), row)]
        out_shape = [out_shape, jax.ShapeDtypeStruct((t, n), F32)]
    return pl.pallas_call(
        functools.partial(_shared_kernel, tail=tail),
        grid=(t // tm,),
        in_specs=[pl.BlockSpec((tm, D_MODEL), off), pl.BlockSpec((tm, D_MODEL), off),
                  pl.BlockSpec((TOP_K, tm, D_MODEL), lambda i: (0, i, 0)),
                  pl.BlockSpec((tm, TOP_K), row), _mod_spec(k_gate, tm, row0),
                  _resident((D_MODEL, 2 * EXPERT_HIDDEN)), _resident((EXPERT_HIDDEN, D_MODEL))] + extra_specs,
        out_specs=out_specs,
        out_shape=out_shape,
        compiler_params=_params(("parallel",)),
        name="shared_expert_combine",
    )(h_bf16, x, y_exp, wts, mods4, w13_bf16, w2_bf16, *extra_args)


ROUTE_TILE = 256
GROUP_SIZE = N_EXPERTS // N_EXPERT_GROUPS
NEG_INF = float("-inf")


def _first_argmax(v, index, sentinel):
    m = jnp.max(v, axis=0, keepdims=True)
    first = jnp.min(jnp.where(v == m, index, sentinel), axis=0, keepdims=True)
    return m, first


def _router_kernel(x_ref, g_ref, sc_ref, sh_ref, wt_ref, rb_ref, tri_ref,
                   h_out, idx_out, gate_out, rank_out, cnt_out, base_ref):
    tm = x_ref.shape[0]
    step = pl.program_id(0)

    @pl.when(jnp.logical_or(step == 0, step == T_CTX // tm))
    def _():
        base_ref[...] = jnp.zeros_like(base_ref)

    hb = _norm_mod(x_ref[...], g_ref[...], sc_ref[0, 0], sh_ref[0, 0]).astype(BF16)
    h_out[...] = hb
    logits = lax.dot_general(wt_ref[...], hb, (((1,), (1,)), ((), ())), preferred_element_type=F32)
    scores = jax.nn.sigmoid(logits)
    choice = scores + rb_ref[...]

    member = lax.broadcasted_iota(jnp.int32, (GROUP_SIZE, tm), 0).astype(F32)
    group_scores = []
    for g in range(N_EXPERT_GROUPS):
        cg = choice[g * GROUP_SIZE:(g + 1) * GROUP_SIZE, :]
        m1, first = _first_argmax(cg, member, float(GROUP_SIZE))
        m2 = jnp.max(jnp.where(member == first, NEG_INF, cg), axis=0, keepdims=True)
        group_scores.append(m1 + m2)
    gs = jnp.concatenate(group_scores, axis=0)

    group = lax.broadcasted_iota(jnp.int32, (N_EXPERT_GROUPS, tm), 0).astype(F32)
    keep = jnp.zeros((N_EXPERT_GROUPS, tm), F32)
    for _ in range(TOPK_GROUPS):
        _, first = _first_argmax(gs, group, float(N_EXPERT_GROUPS))
        sel = group == first
        keep = jnp.where(sel, 1.0, keep)
        gs = jnp.where(sel, NEG_INF, gs)
    masked = jnp.concatenate(
        [jnp.where(keep[g:g + 1, :] > 0.0, choice[g * GROUP_SIZE:(g + 1) * GROUP_SIZE, :], NEG_INF)
         for g in range(N_EXPERT_GROUPS)], axis=0)

    expert = lax.broadcasted_iota(jnp.int32, (N_EXPERTS, tm), 0).astype(F32)
    onehot = jnp.zeros((N_EXPERTS, tm), F32)
    sels, idx_rows, gate_rows = [], [], []
    for _ in range(TOP_K):
        _, first = _first_argmax(masked, expert, float(N_EXPERTS))
        sel = expert == first
        sels.append(sel)
        idx_rows.append(first)
        gate_rows.append(jnp.sum(jnp.where(sel, scores, 0.0), axis=0, keepdims=True))
        masked = jnp.where(sel, NEG_INF, masked)
        onehot = jnp.where(sel, 1.0, onehot)
    gates = jnp.concatenate(gate_rows, axis=0)
    gate_out[...] = gates / jnp.sum(gates, axis=0, keepdims=True) * ROUTED_SCALE
    idx_out[...] = jnp.concatenate(idx_rows, axis=0).astype(jnp.int32)

    cum = jnp.dot(onehot.astype(BF16), tri_ref[...], preferred_element_type=F32) + base_ref[:, 0:1]
    rank_rows = [jnp.sum(jnp.where(sel, cum, 0.0), axis=0, keepdims=True) - 1.0 for sel in sels]
    rank_out[...] = jnp.concatenate(rank_rows, axis=0).astype(jnp.int32)
    base_ref[...] = jnp.broadcast_to(cum[:, tm - 1:tm], base_ref.shape)
    cnt_out[0] = base_ref[...].astype(jnp.int32)


def _router(x, norm_g, mods4, k_shift, k_scale, router_w_t, router_bias):
    t = T_ALL
    tm = ROUTE_TILE
    n_ctx = T_CTX // tm
    tri = (jnp.arange(tm)[:, None] <= jnp.arange(tm)[None, :]).astype(BF16)
    choice_rows = pl.BlockSpec((TOP_K, tm), lambda i: (0, i))
    return pl.pallas_call(
        _router_kernel,
        grid=(t // tm,),
        in_specs=[pl.BlockSpec((tm, D_MODEL), lambda i: (i, 0)), _resident((1, D_MODEL)),
                  _mod_spec(k_scale, tm), _mod_spec(k_shift, tm),
                  _resident((N_EXPERTS, D_MODEL)), _resident((N_EXPERTS, 1)), _resident((tm, tm))],
        out_specs=[pl.BlockSpec((tm, D_MODEL), lambda i: (i, 0)), choice_rows, choice_rows, choice_rows,
                   pl.BlockSpec((1, N_EXPERTS, 128), lambda i: (jnp.where(i < n_ctx, 0, 1), 0, 0))],
        out_shape=[jax.ShapeDtypeStruct((t, D_MODEL), BF16),
                   jax.ShapeDtypeStruct((TOP_K, t), jnp.int32), jax.ShapeDtypeStruct((TOP_K, t), F32),
                   jax.ShapeDtypeStruct((TOP_K, t), jnp.int32),
                   jax.ShapeDtypeStruct((2, N_EXPERTS, 128), jnp.int32)],
        scratch_shapes=[pltpu.VMEM((N_EXPERTS, 128), F32)],
        compiler_params=_params(("arbitrary",)),
        name="router",
    )(x, norm_g.reshape(1, D_MODEL), mods4, mods4, router_w_t, router_bias.reshape(N_EXPERTS, 1), tri)


def _pos_kernel(idx_ref, rank_ref, off_ref, pos_out):
    tm = idx_ref.shape[1]
    expert = lax.broadcasted_iota(jnp.int32, (N_EXPERTS, tm), 0)
    rows = []
    for k in range(TOP_K):
        sel = expert == idx_ref[k:k + 1, :]
        rows.append(jnp.sum(jnp.where(sel, off_ref[...], 0.0), axis=0, keepdims=True))
    pos_out[...] = jnp.concatenate(rows, axis=0).astype(jnp.int32) + rank_ref[...]


def _positions(idx, rank, offsets):
    t = idx.shape[1]
    tm = 1024
    rows = pl.BlockSpec((TOP_K, tm), lambda i: (0, i))
    return pl.pallas_call(
        _pos_kernel, grid=(t // tm,),
        in_specs=[rows, rows, _resident((N_EXPERTS, 1))],
        out_specs=rows, out_shape=jax.ShapeDtypeStruct((TOP_K, t), jnp.int32),
        compiler_params=_params(("parallel",)), name="expert_positions",
    )(idx, rank, offsets.astype(F32).reshape(N_EXPERTS, 1))


def _moe_stream(x, row0, t, h_bf16, top_idx, gate, rank, counts, mods4, k_gate, exp_w13, exp_w2,
                sh_w13_bf16, sh_w2_bf16, out_norm_g, next_proj):
    padded = (counts + MOE_BLOCK - 1) // MOE_BLOCK * MOE_BLOCK
    e_iota = jnp.arange(N_EXPERTS)
    pend = jnp.sum(jnp.where(e_iota[:, None] <= e_iota[None, :], padded[:, None], 0), axis=0)
    pos = _positions(top_idx, rank, pend - padded)
    n_rows = -(-(t * TOP_K + N_EXPERTS * (MOE_BLOCK - 1)) // MOE_BLOCK) * MOE_BLOCK
    n_blocks = n_rows // MOE_BLOCK
    blk_start = jnp.arange(n_blocks, dtype=pend.dtype) * MOE_BLOCK
    blk_e = jnp.minimum(jnp.sum((pend[None, :] <= blk_start[:, None]).astype(jnp.int32), axis=1), N_EXPERTS - 1)
    n_real = (pend[N_EXPERTS - 1:] // MOE_BLOCK).astype(jnp.int32)
    n_assign = t * TOP_K
    tok = jnp.broadcast_to(jnp.arange(t, dtype=jnp.int32)[None, :], (TOP_K, t))
    _, sorted_tok = lax.sort_key_val(pos.reshape(-1), tok.reshape(-1))
    dense_end = jnp.sum(jnp.where(e_iota[:, None] <= e_iota[None, :], counts[:, None], 0), axis=0)
    shift_blk = ((pend - padded) - (dense_end - counts))[blk_e]
    dense = jnp.arange(n_rows, dtype=jnp.int32) - jnp.repeat(shift_blk, MOE_BLOCK)
    wrapped = (dense & (n_assign - 1)) if n_assign & (n_assign - 1) == 0 else (dense % n_assign)
    xs = h_bf16[sorted_tok[wrapped] + row0]
    ys = _grouped_experts(blk_e, n_real, padded, xs, exp_w13, exp_w2)
    y_exp = ys[pos.reshape(-1)].reshape(TOP_K, t, D_MODEL)
    return _shared_and_combine(h_bf16, x, row0, t, y_exp, gate.T, mods4, k_gate, sh_w13_bf16, sh_w2_bf16,
                               out_norm_g, next_proj)


def _moe_streams(x, norm_g, mods4, k_shift, k_scale, k_gate, router_w, router_bias, exp_w13, exp_w2, sh_w13, sh_w2,
                 out_norm_g=None, next_proj=None):
    h_bf16, top_idx, gate, rank, counts = _router(x, norm_g, mods4, k_shift, k_scale, router_w.T.astype(BF16),
                                                  router_bias)
    sh13, sh2 = sh_w13.astype(BF16), sh_w2.astype(BF16)
    outs = []
    for s, (row0, t) in enumerate(((0, T_CTX), (T_CTX, T_LAT))):
        cols = slice(row0, row0 + t)
        outs.append(_moe_stream(x, row0, t, h_bf16, top_idx[:, cols], gate[:, cols], rank[:, cols], counts[s, :, 0],
                                mods4, k_gate, exp_w13, exp_w2, sh13, sh2, out_norm_g, next_proj))
    return tuple(outs)


HY_BLOCK_CTX, HY_TC_CTX = 256, 512
HY_BLOCK_LAT, HY_TC_LAT = 512, 128


def _hyena_filters(L, w1, b1, w2, b2, w3, sin_freq):
    t_norm = jnp.linspace(0.0, 1.0, L, dtype=F32)[:, None]
    omega = 2.0 * math.pi * jnp.arange(L, dtype=F32)[:, None] / L
    bands = jnp.linspace(1e-4, FILTER_BANDS - 1, FILTER_BANDS, dtype=F32)[None, :]
    z = jnp.concatenate([t_norm, jnp.cos(bands * omega), -jnp.sin(bands * omega)], axis=-1)
    h = jnp.sin(sin_freq * (z @ w1 + b1))
    h = jnp.sin(sin_freq * (h @ w2 + b2))
    h = (h @ w3).reshape(L, 2, HYENA_ORDER, HYENA_WIDTH)
    deltas = jnp.abs(jnp.linspace(math.log(DECAY_TARGET) / DECAY_PCT_LONG,
                                  math.log(DECAY_TARGET) / DECAY_PCT_SHORT, HYENA_WIDTH, dtype=F32))
    h = h * jnp.exp(-t_norm * deltas)[:, None, None, :]
    fwd, bwd = h[:, 0], h[:, 1] * (t_norm > 0.0)[:, :, None]
    norm = jnp.sum(jnp.abs(fwd), axis=0, keepdims=True) + jnp.sum(jnp.abs(bwd), axis=0, keepdims=True)
    return fwd / norm, bwd / norm


HY_ACC_ELEMS = 4096


def _dft_matrices(bk):
    k = jnp.arange(bk, dtype=jnp.int32)
    phase = ((2 * k[:, None] + 1) * k[None, :]) % (4 * bk)
    ang = phase.astype(F32) * (math.pi / (2 * bk))
    return jnp.cos(ang), jnp.sin(ang)


def _spectra_kernel(xf_ref, xb_ref, yf_ref, yb_ref, c1_ref, s1_ref, c2_ref, s2_ref, o_ref, *, nb):
    d = pl.program_id(1) - (nb - 1)
    x = jnp.where(d >= 0, xf_ref[0], xb_ref[0]).astype(BF16)
    y = jnp.where(d >= 1, yf_ref[0], yb_ref[0]).astype(BF16)
    cy = jnp.where(d == 0, c1_ref[...], c2_ref[...])
    sy = jnp.where(d == 0, s1_ref[...], s2_ref[...])
    re = jnp.dot(c1_ref[...], x, preferred_element_type=F32) + jnp.dot(cy, y, preferred_element_type=F32)
    im = jnp.dot(sy, y, preferred_element_type=F32) - jnp.dot(s1_ref[...], x, preferred_element_type=F32)
    o_ref[0, 0, 0] = re
    o_ref[0, 0, 1] = jnp.where(d >= 0, im, -im)


def _filter_spectra(fwd, bwd, bk, tc):
    L = fwd.shape[0]
    nb = L // bk
    n_ct = HYENA_WIDTH // tc
    cos, sin = _dft_matrices(bk)
    k = jnp.arange(bk, dtype=jnp.int32)
    phase = ((2 * k[:, None] + 1) * (bk - k[None, :])) % (4 * bk)
    ang = phase.astype(F32) * (math.pi / (2 * bk))
    live = (k[None, :] > 0).astype(F32)
    mats = [m.astype(BF16) for m in (cos, sin, jnp.cos(ang) * live, jnp.sin(ang) * live)]

    def seg(pick):
        return pl.BlockSpec((1, bk, tc), lambda o, di, ct: (pick(di - (nb - 1)), 0, o * n_ct + ct))

    mat = pl.BlockSpec((bk, bk), lambda o, di, ct: (0, 0), pipeline_mode=pl.Buffered(1))
    halves = [h.reshape(nb, bk, HYENA_ORDER * HYENA_WIDTH) for h in (fwd, bwd)]
    return pl.pallas_call(
        functools.partial(_spectra_kernel, nb=nb),
        grid=(HYENA_ORDER, 2 * nb - 1, n_ct),
        in_specs=[seg(lambda d: jnp.maximum(d, 0)), seg(lambda d: jnp.maximum(-d, 0)),
                  seg(lambda d: jnp.maximum(d - 1, 0)), seg(lambda d: jnp.maximum(-d - 1, 0)),
                  mat, mat, mat, mat],
        out_specs=pl.BlockSpec((1, 1, 2, bk, tc), lambda o, di, ct: (o, di, 0, 0, ct)),
        out_shape=jax.ShapeDtypeStruct((HYENA_ORDER, 2 * nb - 1, 2, bk, HYENA_WIDTH), F32),
        compiler_params=_params(("parallel", "parallel", "parallel")),
        name="filter_spectra",
    )(halves[0], halves[1], halves[0], halves[1], *mats)


def _hyena_kernel(v_ref, x1_ref, x2_ref, cw_ref, cb_ref, hb_ref, k0_ref, k1_ref, c_ref, s_ref, ct_ref, st_ref,
                  o_ref, vs_ref, z1_ref, x2s_ref, ur_ref, ui_ref, yr_ref, yi_ref, *, seq, bk, tc):
    nb = seq // bk
    ft = HY_ACC_ELEMS // tc
    row = lax.broadcasted_iota(jnp.int32, (seq, tc), 0)

    def short_conv(x, j):
        prev = jnp.where(row == 0, 0.0, pltpu.roll(x, 1, 0))
        nxt = jnp.where(row == seq - 1, 0.0, pltpu.roll(x, seq - 1, 0))
        return (prev * cw_ref[0, j:j + 1, :] + x * cw_ref[1, j:j + 1, :] + nxt * cw_ref[2, j:j + 1, :]
                + cb_ref[j:j + 1, :])

    vs_ref[...] = short_conv(v_ref[...], 0)
    z1_ref[...] = short_conv(x1_ref[...], 1)
    x2s_ref[...] = short_conv(x2_ref[...], 2)

    def long_conv(src_ref, k_ref, emit):
        for blk in range(nb):
            ub = src_ref[blk * bk:(blk + 1) * bk, :].astype(BF16)
            ur_ref[blk] = jnp.dot(c_ref[...], ub, preferred_element_type=F32)
            ui_ref[blk] = -jnp.dot(s_ref[...], ub, preferred_element_type=F32)
        for out_blk in range(nb):
            def acc_tile(f, carry, out_blk=out_blk):
                rows = pl.ds(pl.multiple_of(f * ft, ft), ft)
                ar = jnp.zeros((ft, tc), F32)
                ai = jnp.zeros((ft, tc), F32)
                for in_blk in range(nb):
                    d = out_blk - in_blk + nb - 1
                    kr, ki = k_ref[d, 0, rows, :], k_ref[d, 1, rows, :]
                    xr, xi = ur_ref[in_blk, rows, :], ui_ref[in_blk, rows, :]
                    ar = ar + (kr * xr - ki * xi)
                    ai = ai + (kr * xi + ki * xr)
                yr_ref[rows, :] = ar
                yi_ref[rows, :] = ai
                return carry

            lax.fori_loop(0, bk // ft, acc_tile, 0)
            y = jnp.dot(ct_ref[...], yr_ref[...].astype(BF16), preferred_element_type=F32)
            y = y - jnp.dot(st_ref[...], yi_ref[...].astype(BF16), preferred_element_type=F32)
            emit(slice(out_blk * bk, (out_blk + 1) * bk), y * (1.0 / bk))

    def emit_z1(rows, y):
        z1_ref[rows, :] = z1_ref[rows, :] * (y + hb_ref[0:1, :] * vs_ref[rows, :])

    def emit_out(rows, y):
        o_ref[rows, :] = (x2s_ref[rows, :] * (y + hb_ref[1:2, :] * z1_ref[rows, :])).astype(BF16)

    long_conv(vs_ref, k0_ref, emit_z1)
    long_conv(z1_ref, k1_ref, emit_out)


def _hyena(proj, row0, n_batch, seq, bk, tc, conv_w, conv_b, hy_bias, spectra, mats):
    nb = seq // bk
    n_ct = HYENA_WIDTH // tc
    col0 = QKV_WIDTH // tc
    per = HYENA_WIDTH // tc
    b0 = row0 // seq

    def data(which):
        return pl.BlockSpec((seq, tc), lambda ct, b: (b0 + b, col0 + which * per + ct))

    const2 = lambda shape: pl.BlockSpec(shape, lambda ct, b: (0, 0), pipeline_mode=pl.Buffered(1))
    kspec = pl.BlockSpec((2 * nb - 1, 2, bk, tc), lambda ct, b: (0, 0, 0, ct), pipeline_mode=pl.Buffered(1))
    cw = conv_w.reshape(3, HYENA_ORDER + 1, HYENA_WIDTH)
    cb = conv_b.reshape(HYENA_ORDER + 1, HYENA_WIDTH)
    return pl.pallas_call(
        functools.partial(_hyena_kernel, seq=seq, bk=bk, tc=tc),
        grid=(n_ct, n_batch),
        in_specs=[data(0), data(1), data(2),
                  pl.BlockSpec((3, HYENA_ORDER + 1, tc), lambda ct, b: (0, 0, ct)),
                  pl.BlockSpec((HYENA_ORDER + 1, tc), lambda ct, b: (0, ct)),
                  pl.BlockSpec((HYENA_ORDER, tc), lambda ct, b: (0, ct)),
                  kspec, kspec, const2((bk, bk)), const2((bk, bk)), const2((bk, bk)), const2((bk, bk))],
        out_specs=pl.BlockSpec((seq, tc), lambda ct, b: (b, ct)),
        out_shape=jax.ShapeDtypeStruct((n_batch * seq, HYENA_WIDTH), BF16),
        scratch_shapes=[pltpu.VMEM((seq, tc), F32), pltpu.VMEM((seq, tc), F32), pltpu.VMEM((seq, tc), F32),
                        pltpu.VMEM((nb, bk, tc), F32), pltpu.VMEM((nb, bk, tc), F32),
                        pltpu.VMEM((bk, tc), F32), pltpu.VMEM((bk, tc), F32)],
        compiler_params=_params(("parallel", "parallel")),
        name="hyena_seq%d" % seq,
    )(proj, proj, proj, cw, cb, hy_bias, spectra[0], spectra[1], *mats)


HY_SPECTRA_TC = 512


def _hyena_operands(L, bk, f_w1, f_b1, f_w2, f_b2, f_w3, sin_freq):
    fwd, bwd = _hyena_filters(L, f_w1, f_b1, f_w2, f_b2, f_w3, sin_freq)
    cos, sin = _dft_matrices(bk)
    mats = tuple(m.astype(BF16) for m in (cos, sin, cos.T, sin.T))
    return _filter_spectra(fwd, bwd, bk, HY_SPECTRA_TC), mats


S5_LANES = S5_GROUPS * S5_STATE
S5_CLUSTER_GROUPS = 16
S5_CLUSTERS = S5_GROUPS // S5_CLUSTER_GROUPS
S5_CLUSTER_CH = S5_CLUSTER_GROUPS * S5_GROUP_CH
S5_CLUSTER_LANES = S5_CLUSTER_GROUPS * S5_STATE
SCAN_SEQS = 8
SCAN_ROWS = 512


def _s5_kernel(u_ref, s0r_ref, s0i_ref, bw_ref, cwr_ref, cwi_ref, lam_ref, y_ref, sfr_ref, sfi_ref,
               sre, sim, car, cai, *, reverse):
    n_seq, tc, _ = u_ref.shape

    @pl.when(pl.program_id(1) == 0)
    def _():
        car[...] = jnp.zeros_like(car)
        cai[...] = jnp.zeros_like(cai)
        car[0:n_seq, :] = s0r_ref[0]
        cai[0:n_seq, :] = s0i_ref[0]

    u = u_ref[...]
    if n_seq < SCAN_SEQS:
        u = jnp.concatenate([u, jnp.zeros((SCAN_SEQS - n_seq, tc, S5_WIDTH), F32)], axis=0)
    ub = jnp.swapaxes(u, 0, 1).reshape(tc * SCAN_SEQS, S5_WIDTH).astype(BF16)
    for k in range(S5_CLUSTERS):
        bu = jnp.dot(ub[:, k * S5_CLUSTER_CH:(k + 1) * S5_CLUSTER_CH], bw_ref[k], preferred_element_type=F32)
        sre[:, k * S5_CLUSTER_LANES:(k + 1) * S5_CLUSTER_LANES] = bu[:, :S5_CLUSTER_LANES]
        sim[:, k * S5_CLUSTER_LANES:(k + 1) * S5_CLUSTER_LANES] = bu[:, S5_CLUSTER_LANES:]

    for k in range(S5_CLUSTERS):
        lanes = slice(k * S5_CLUSTER_LANES, (k + 1) * S5_CLUSTER_LANES)
        lr = jnp.broadcast_to(lam_ref[0:1, lanes], (SCAN_SEQS, S5_CLUSTER_LANES))
        li = jnp.broadcast_to(lam_ref[1:2, lanes], (SCAN_SEQS, S5_CLUSTER_LANES))

        def step(i, state, lanes=lanes, lr=lr, li=li):
            sr, si = state
            t = (tc - 1 - i) if reverse else i
            rows = pl.ds(pl.multiple_of(t * SCAN_SEQS, SCAN_SEQS), SCAN_SEQS)
            nr = (lr * sr - li * si) + sre[rows, lanes]
            ni = (lr * si + li * sr) + sim[rows, lanes]
            sre[rows, lanes] = nr
            sim[rows, lanes] = ni
            return nr, ni

        sr, si = lax.fori_loop(0, tc, step, (car[:, lanes], cai[:, lanes]), unroll=2)
        car[:, lanes] = sr
        cai[:, lanes] = si

    parts = []
    for k in range(S5_CLUSTERS):
        lanes = slice(k * S5_CLUSTER_LANES, (k + 1) * S5_CLUSTER_LANES)
        yk = jnp.dot(sre[:, lanes].astype(BF16), cwr_ref[k], preferred_element_type=F32)
        parts.append(yk + jnp.dot(sim[:, lanes].astype(BF16), cwi_ref[k], preferred_element_type=F32))
    y = jnp.concatenate(parts, axis=1).reshape(tc, SCAN_SEQS, S5_WIDTH)
    y_ref[...] = jnp.swapaxes(y, 0, 1)[0:n_seq]
    sfr_ref[0] = car[0:n_seq, :]
    sfi_ref[0] = cai[0:n_seq, :]


def _s5_direction(u, n_seq, seq, group, s0_re, s0_im, bw, cwr, cwi, lam, reverse):
    tc = SCAN_ROWS // SCAN_SEQS
    nc = seq // tc
    ng = n_seq // group

    def chunk(c):
        return (nc - 1 - c) if reverse else c

    state_spec = pl.BlockSpec((1, group, S5_LANES), lambda g, c: (g, 0, 0))
    const = lambda a: pl.BlockSpec(a.shape, lambda g, c: (0,) * a.ndim, pipeline_mode=pl.Buffered(1))
    y, f_re, f_im = pl.pallas_call(
        functools.partial(_s5_kernel, reverse=reverse),
        grid=(ng, nc),
        in_specs=[pl.BlockSpec((group, tc, S5_WIDTH), lambda g, c: (g, chunk(c), 0)),
                  state_spec, state_spec, const(bw), const(cwr), const(cwi), const(lam)],
        out_specs=[pl.BlockSpec((group, tc, S5_WIDTH), lambda g, c: (g, chunk(c), 0)),
                   state_spec, state_spec],
        out_shape=[jax.ShapeDtypeStruct((n_seq, seq, S5_WIDTH), F32),
                   jax.ShapeDtypeStruct((ng, group, S5_LANES), F32),
                   jax.ShapeDtypeStruct((ng, group, S5_LANES), F32)],
        scratch_shapes=[pltpu.VMEM((SCAN_ROWS, S5_LANES), F32), pltpu.VMEM((SCAN_ROWS, S5_LANES), F32),
                        pltpu.VMEM((SCAN_SEQS, S5_LANES), F32), pltpu.VMEM((SCAN_SEQS, S5_LANES), F32)],
        compiler_params=_params(("parallel", "arbitrary")),
        name="s5_bwd" if reverse else "s5_fwd",
    )(u.reshape(n_seq, seq, S5_WIDTH), s0_re.reshape(ng, group, S5_LANES), s0_im.reshape(ng, group, S5_LANES),
      bw, cwr, cwi, lam)
    return y.reshape(n_seq * seq, S5_WIDTH), f_re.reshape(n_seq, S5_LANES), f_im.reshape(n_seq, S5_LANES)


def _s5_weights(a_re, a_im, log_dt, b_re, b_im, c_re, c_im):
    lam = lax.complex(a_re, a_im)
    lam_bar = jnp.exp(lam * jnp.exp(log_dt)[..., None])
    b_bar = ((lam_bar - 1.0) / lam)[..., None] * lax.complex(b_re, b_im)
    eye = jnp.eye(S5_CLUSTER_GROUPS, dtype=F32)
    out = []
    for d in range(2):
        def cluster_in(w):
            w = w.reshape(S5_CLUSTERS, S5_CLUSTER_GROUPS, S5_STATE, S5_GROUP_CH)
            return jnp.einsum('ab,kapc->kacbp', eye, w).reshape(S5_CLUSTERS, S5_CLUSTER_CH, S5_CLUSTER_LANES)

        def cluster_out(w):
            w = w.reshape(S5_CLUSTERS, S5_CLUSTER_GROUPS, S5_GROUP_CH, S5_STATE)
            return jnp.einsum('ab,kacp->kapbc', eye, w).reshape(S5_CLUSTERS, S5_CLUSTER_LANES, S5_CLUSTER_CH)

        bw = jnp.concatenate([cluster_in(jnp.real(b_bar[d])), cluster_in(jnp.imag(b_bar[d]))], axis=-1)
        cwr = cluster_out(c_re[d])
        cwi = cluster_out(-c_im[d])
        lam_d = lam_bar[d].reshape(S5_LANES)
        lam2 = jnp.stack([jnp.real(lam_d), jnp.imag(lam_d)]).astype(F32)
        out.append((bw.astype(BF16), cwr.astype(BF16), cwi.astype(BF16), lam2))
    return out


def _rope_tables(L):
    n_rows = L // GRID_W
    row_idx = jnp.repeat(jnp.arange(n_rows, dtype=F32), GRID_W)
    col_idx = (jnp.arange(L) % GRID_W).astype(F32)
    inv = ROPE_THETA ** (-jnp.arange(0, ROPE_AXIS_DIM, 2, dtype=F32) / ROPE_AXIS_DIM)
    ang = jnp.concatenate([row_idx[:, None] * inv, col_idx[:, None] * inv], axis=-1)
    cos, sin = jnp.cos(ang), jnp.sin(ang)
    cos_full = jnp.repeat(cos, 2, axis=-1)
    sin_signed = jnp.stack([-sin, sin], axis=-1).reshape(L, HEAD_DIM)
    return cos_full, sin_signed


def kernel(x_prompt, x_sample, cache_k_l0, cache_v_l0, state_s5_re_l1, state_s5_im_l1, c, c_ctx,
           ada_w_l0, ada_b_l0, norm1_l0, norm2_l0,
           w_in_l0, w_out_l0, q_norm_l0, k_norm_l0, hy_conv_w_l0, hy_conv_b_l0,
           hy_ffn_w1_l0, hy_ffn_b1_l0, hy_ffn_w2_l0, hy_ffn_b2_l0, hy_ffn_w3_l0, hy_sin_freq_l0, hy_bias_l0,
           router_l0, router_bias_l0, exp_w13_l0, exp_w2_l0, shared_w13_l0, shared_w2_l0,
           ada_w_l1, ada_b_l1, norm1_l1, norm2_l1,
           w_in_l1, s5_a_re_l1, s5_a_im_l1, s5_log_dt_l1, s5_b_re_l1, s5_b_im_l1,
           s5_c_re_l1, s5_c_im_l1, s5_d_l1, w_glu_l1,
           router_l1, router_bias_l1, exp_w13_l1, exp_w2_l1, shared_w13_l1, shared_w2_l1,
           final_norm):
    x = (x_prompt.reshape(T_CTX, D_MODEL), x_sample.reshape(T_LAT, D_MODEL))
    cond = jnp.concatenate([c_ctx[None, :], c, jnp.zeros((N_COND - 1 - DEC_BATCH, D_MODEL), F32)], axis=0)

    mods = _ada_mods(cond, ada_w_l0, ada_b_l0)
    proj = _norm_mod_matmul(x, norm1_l0, mods, 0, 1, w_in_l0.astype(BF16))

    q_c, k_c, v_c, new_k = _qk_prep(proj, 0, T_CTX, q_norm_l0, k_norm_l0, None)
    attn_c = _attention(q_c, k_c, v_c, BATCH, SEQ, None)
    q_l, k_l, v_l = _qk_prep(proj, T_CTX, T_LAT, q_norm_l0, k_norm_l0, _rope_tables(DEC_SEQ))
    ctx_kv = (cache_k_l0.reshape(DEC_BATCH * PAST_LEN, KV_WIDTH).astype(BF16),
              cache_v_l0.reshape(DEC_BATCH * PAST_LEN, KV_WIDTH).astype(BF16))
    attn_l = _attention(q_l, k_l, v_l, DEC_BATCH, DEC_SEQ, ctx_kv)

    hy_f = (hy_ffn_w1_l0, hy_ffn_b1_l0, hy_ffn_w2_l0, hy_ffn_b2_l0, hy_ffn_w3_l0, hy_sin_freq_l0)
    spec_c, mats_c = _hyena_operands(SEQ, HY_BLOCK_CTX, *hy_f)
    z_c = _hyena(proj, 0, BATCH, SEQ, HY_BLOCK_CTX, HY_TC_CTX, hy_conv_w_l0, hy_conv_b_l0, hy_bias_l0,
                 spec_c, mats_c)
    spec_l, mats_l = _hyena_operands(DEC_SEQ, HY_BLOCK_LAT, *hy_f)
    z_l = _hyena(proj, T_CTX, DEC_BATCH, DEC_SEQ, HY_BLOCK_LAT, HY_TC_LAT, hy_conv_w_l0, hy_conv_b_l0, hy_bias_l0,
                 spec_l, mats_l)

    w_out = w_out_l0.astype(BF16)
    x = _out_proj((attn_c, attn_l), (z_c, z_l), x, mods, 2, w_out[:Q_WIDTH], w_out[Q_WIDTH:])
    mods_l1 = _ada_mods(cond, ada_w_l1, ada_b_l1)
    (x_c, u_c), (x_l, u_l) = _moe_streams(x, norm2_l0, mods, 3, 4, 5, router_l0, router_bias_l0, exp_w13_l0,
                                          exp_w2_l0, shared_w13_l0, shared_w2_l0,
                                          next_proj=(norm1_l1, mods_l1, 0, 1, w_in_l1.astype(BF16)))
    x, u, mods = (x_c, x_l), (u_c, u_l), mods_l1

    new_v = proj[:T_CTX, Q_WIDTH + KV_WIDTH:QKV_WIDTH]

    s5_w = _s5_weights(s5_a_re_l1, s5_a_im_l1, s5_log_dt_l1, s5_b_re_l1, s5_b_im_l1, s5_c_re_l1, s5_c_im_l1)
    zero_state = jnp.zeros((BATCH, S5_LANES), F32)
    ys, finals = [], []
    for d in range(2):
        y_c, f_re, f_im = _s5_direction(u_c, BATCH, SEQ, SCAN_SEQS, zero_state, zero_state, *s5_w[d],
                                        reverse=d == 1)
        y_l, _, _ = _s5_direction(u_l, DEC_BATCH, DEC_SEQ, DEC_BATCH,
                                  state_s5_re_l1[:, d].reshape(DEC_BATCH, S5_LANES),
                                  state_s5_im_l1[:, d].reshape(DEC_BATCH, S5_LANES), *s5_w[d], reverse=d == 1)
        ys.append((y_c, y_l))
        finals.append((f_re.reshape(BATCH, S5_GROUPS, S5_STATE), f_im.reshape(BATCH, S5_GROUPS, S5_STATE)))
    s_re = jnp.stack([finals[0][0], finals[1][0]], axis=1)
    s_im = jnp.stack([finals[0][1], finals[1][1]], axis=1)
    x = _glu_proj(ys[0], ys[1], u, s5_d_l1, x, mods, 2, w_glu_l1.astype(BF16))
    y_c, y_l = _moe_streams(x, norm2_l1, mods, 3, 4, 5, router_l1, router_bias_l1, exp_w13_l1, exp_w2_l1,
                            shared_w13_l1, shared_w2_l1, out_norm_g=final_norm)

    return (y_c.reshape(BATCH, SEQ, D_MODEL),
            y_l.reshape(DEC_BATCH, DEC_SEQ, D_MODEL),
            new_k.reshape(BATCH, SEQ, N_KV_HEADS, HEAD_DIM),
            new_v.reshape(BATCH, SEQ, N_KV_HEADS, HEAD_DIM),
            s_re, s_im)
```
